```python
import jax, jax.numpy as jnp
from jax import lax
import numpy as np


D_MODEL = 1024
BATCH = 8
SEQ = 4096
DEPTH = 2

D_MIX = D_MODEL
HEAD_DIM = 64
RWKV_WIDTH = D_MIX // 2
RWKV_HEADS = RWKV_WIDTH // HEAD_DIM
POOL_WIDTH = D_MIX // 4
POOL_WINDOWS = (2, 4, 8, 16)
POOL_GROUPS = len(POOL_WINDOWS)
POOL_GROUP_DIM = POOL_WIDTH // POOL_GROUPS
MLA_WIDTH = D_MIX - RWKV_WIDTH - POOL_WIDTH
MLA_V_DIM = HEAD_DIM
MLA_HEADS = MLA_WIDTH // MLA_V_DIM
MLA_QK_NOPE = 64
MLA_QK_ROPE = 32
MLA_QK_DIM = MLA_QK_NOPE + MLA_QK_ROPE
MLA_Q_LORA = 256
MLA_KV_LORA = 128
ROPE_BASE = 10000.0
Q_BLOCK = 128
RWKV_DECAY_LORA = 64
RWKV_ICLR_LORA = 64
RWKV_VRES_LORA = 32
RWKV_GATE_LORA = 160
RWKV_LNX_EPS = 64e-5
RWKV_SPLITS = (RWKV_WIDTH, 2 * RWKV_WIDTH, 3 * RWKV_WIDTH,
               3 * RWKV_WIDTH + RWKV_DECAY_LORA,
               3 * RWKV_WIDTH + RWKV_DECAY_LORA + RWKV_ICLR_LORA)
RWKV_SHIFT_COLS = 3 * RWKV_WIDTH + RWKV_DECAY_LORA + RWKV_ICLR_LORA + RWKV_GATE_LORA
OFF_POOL = RWKV_SHIFT_COLS
OFF_QLAT = OFF_POOL + POOL_WIDTH
OFF_KVLAT = OFF_QLAT + MLA_Q_LORA
OFF_KROPE = OFF_KVLAT + MLA_KV_LORA
N_IN_BASE = OFF_KROPE + MLA_QK_ROPE
N_IN_REST = N_IN_BASE + RWKV_VRES_LORA
D_FF = 2816
N_EXPERTS = 8
TOP_K = 2
D_FF_EXPERT = 3584
MOE_BLOCK = 512
N_DENSE = (DEPTH + 1) // 2
N_MOE = DEPTH // 2
NORM_EPS = 1e-6
NEG_INF = -1e30

kernel_name = "hymba_rwkv7_pool_mla_moe_adaln"


def rms_norm(x, gain):
    xf = x.astype(jnp.float32)
    xf = xf * lax.rsqrt(jnp.mean(xf * xf, axis=-1, keepdims=True) + NORM_EPS)
    return (xf * gain.astype(jnp.float32)).astype(x.dtype)


def token_shift(y, mu):
    y_prev = jnp.pad(y, ((0, 0), (1, 0), (0, 0)))[:, :-1]
    return y + mu * (y_prev - y)


def rope_tables(positions):
    inv_freq = ROPE_BASE ** (-jnp.arange(0, MLA_QK_ROPE, 2, dtype=jnp.float32) / MLA_QK_ROPE)
    ang = positions.astype(jnp.float32)[..., None] * inv_freq
    return jnp.cos(ang), jnp.sin(ang)


def apply_rope(x, cos, sin):
    cos = cos.astype(x.dtype)
    sin = sin.astype(x.dtype)
    x1, x2 = jnp.split(x, 2, axis=-1)
    return jnp.concatenate([x1 * cos - x2 * sin, x1 * sin + x2 * cos], axis=-1)


def wkv7_scan(r, w, k, v, a, b):
    B_, T_, H_, N_ = r.shape
    xs = tuple(jnp.moveaxis(t.astype(jnp.float32), 1, 0) for t in (r, w, k, v, a, b))

    def step(S, inp):
        r_t, w_t, k_t, v_t, a_t, b_t = inp
        sa = jnp.einsum('bhvk,bhk->bhv', S, a_t)
        S = (S * w_t[:, :, None, :] + sa[..., None] * b_t[:, :, None, :]
             + v_t[..., None] * k_t[:, :, None, :])
        return S, jnp.einsum('bhvk,bhk->bhv', S, r_t)

    S0 = jnp.zeros((B_, H_, N_, N_), jnp.float32)
    _, y = lax.scan(step, S0, xs)
    return jnp.moveaxis(y, 0, 1)


def rwkv7_time_mix(r, k, v, wd, ad, gd, vec, w2, a2, g2):
    w0, a0, k_k, k_a, r_k, lnx_gain, lnx_bias = (vec[i] for i in range(7))
    B_, T_, _ = r.shape

    def heads(t):
        return t.reshape(B_, T_, RWKV_HEADS, HEAD_DIM)

    w_log = -jax.nn.softplus(-(w0 + jnp.tanh(wd) @ w2)) - 0.5
    decay = jnp.exp(-jnp.exp(w_log.astype(jnp.float32)))
    a = jax.nn.sigmoid(a0 + ad @ a2)
    g = jax.nn.sigmoid(gd) @ g2
    kk = heads((k * k_k).astype(jnp.float32))
    kk = kk / jnp.maximum(jnp.sqrt(jnp.sum(kk * kk, axis=-1, keepdims=True)), 1e-12)
    k = k * (1 + (a - 1) * k_a)
    rh, kh, vh, ah = heads(r), heads(k), heads(v), heads(a)
    y = wkv7_scan(rh, heads(decay), kh, vh, -kk, kk * ah.astype(jnp.float32))
    mu = jnp.mean(y, axis=-1, keepdims=True)
    var = jnp.mean(jnp.square(y - mu), axis=-1, keepdims=True)
    y = ((y - mu) * lax.rsqrt(var + RWKV_LNX_EPS)).reshape(B_, T_, RWKV_WIDTH)
    y = (y * lnx_gain.astype(jnp.float32) + lnx_bias.astype(jnp.float32)).astype(r.dtype)
    bonus = jnp.sum(rh * kh * r_k.reshape(RWKV_HEADS, HEAD_DIM), axis=-1, keepdims=True) * vh
    y = y + bonus.reshape(B_, T_, RWKV_WIDTH)
    return y * g


def multiscale_pool(u, w_pool, scale):
    B_, T_, C_ = u.shape
    uf = u.astype(jnp.float32).reshape(B_, T_, POOL_GROUPS, POOL_GROUP_DIM)
    cs = jnp.cumsum(uf, axis=1)
    t_idx = jnp.arange(T_)
    outs = []
    for gi, win in enumerate(POOL_WINDOWS):
        csg = cs[:, :, gi]
        lag = jnp.pad(csg, ((0, 0), (win, 0), (0, 0)))[:, :T_]
        cnt = jnp.minimum(t_idx + 1, win).astype(jnp.float32)[None, :, None]
        outs.append((csg - lag) / cnt - uf[:, :, gi])
    p = jnp.stack(outs, axis=2).astype(u.dtype)
    y = jnp.einsum('btgc,gcd->btgd', p, w_pool)
    return y.reshape(B_, T_, C_) * scale


def causal_block_attention(q, k, v):
    B_, T_, H_, Dq = q.shape
    nb = T_ // Q_BLOCK
    scale = Dq ** -0.5
    qb = q.reshape(B_, nb, Q_BLOCK, H_, Dq).transpose(1, 0, 2, 3, 4)
    key_pos = jnp.arange(T_)

    def one_block(args):
        q_blk, blk = args
        s = jnp.einsum('bqhd,bkhd->bhqk', q_blk, k).astype(jnp.float32) * scale
        q_pos = blk * Q_BLOCK + jnp.arange(Q_BLOCK)
        s = jnp.where(key_pos[None, :] <= q_pos[:, None], s, NEG_INF)
        p = jax.nn.softmax(s, axis=-1).astype(v.dtype)
        return jnp.einsum('bhqk,bkhd->bqhd', p, v)

    o = lax.map(one_block, (qb, jnp.arange(nb)))
    return o.transpose(1, 0, 2, 3, 4).reshape(B_, T_, H_, v.shape[-1])


def mla_attention(q_lat, kv_lat, k_rope, cos, sin, q_lat_gain, kv_lat_gain, wq_up, wkv_up, qk_gain):
    B_, T_, _ = q_lat.shape
    q = (rms_norm(q_lat, q_lat_gain) @ wq_up).reshape(B_, T_, MLA_HEADS, MLA_QK_DIM)
    kv = (rms_norm(kv_lat, kv_lat_gain) @ wkv_up).reshape(B_, T_, MLA_HEADS, MLA_QK_NOPE + MLA_V_DIM)
    q_pe = apply_rope(q[..., MLA_QK_NOPE:], cos[:, :, None], sin[:, :, None])
    k_pe = apply_rope(k_rope, cos, sin)[:, :, None, :]
    q = jnp.concatenate([q[..., :MLA_QK_NOPE], q_pe], axis=-1)
    k = jnp.concatenate([kv[..., :MLA_QK_NOPE],
                         jnp.broadcast_to(k_pe, (B_, T_, MLA_HEADS, MLA_QK_ROPE))], axis=-1)
    v = kv[..., MLA_QK_NOPE:]
    q = rms_norm(q, qk_gain[0])
    k = rms_norm(k, qk_gain[1])
    o = causal_block_attention(q, k, v)
    return o.reshape(B_, T_, MLA_WIDTH)


def swiglu(h, w_in, w_out):
    g, u = jnp.split(h @ w_in, 2, axis=-1)
    return (jax.nn.silu(g) * u) @ w_out


def moe_swiglu(h, router, w_in, w_out):
    B_, T_, D_ = h.shape
    n_tok = B_ * T_
    hf = h.reshape(n_tok, D_)
    logits = (hf @ router).astype(jnp.float32)
    top_val, top_idx = lax.top_k(logits, TOP_K)
    gate = jax.nn.softmax(top_val, axis=-1)
    n_asg = n_tok * TOP_K
    exp_flat = top_idx.reshape(-1)
    tok_flat = (jnp.arange(n_asg) // TOP_K).astype(jnp.int32)
    gate_flat = gate.reshape(-1)
    order = jnp.argsort(exp_flat)
    exp_sorted = exp_flat[order]
    counts = jnp.bincount(exp_flat, length=N_EXPERTS)
    padded = (counts + MOE_BLOCK - 1) // MOE_BLOCK * MOE_BLOCK
    start = jnp.cumsum(counts) - counts
    pend = jnp.cumsum(padded)
    pstart = pend - padded
    dest = pstart[exp_sorted] + jnp.arange(n_asg) - start[exp_sorted]
    n_blocks = -(-n_asg // MOE_BLOCK) + N_EXPERTS
    n_rows = n_blocks * MOE_BLOCK
    row_tok = jnp.full((n_rows,), n_tok, jnp.int32).at[dest].set(tok_flat[order])
    row_gate = jnp.zeros((n_rows,), jnp.float32).at[dest].set(gate_flat[order])
    block_exp = jnp.clip(jnp.searchsorted(pend, jnp.arange(n_blocks) * MOE_BLOCK, side='right'),
                         0, N_EXPERTS - 1)
    x_pad = jnp.concatenate([hf, jnp.zeros((1, D_), hf.dtype)], axis=0)
    xb = x_pad[row_tok].reshape(n_blocks, MOE_BLOCK, D_)

    def expert_block(args):
        x_blk, e = args
        return swiglu(x_blk, w_in[e], w_out[e])

    yb = lax.map(expert_block, (xb, block_exp)).reshape(n_rows, D_)
    y = jax.ops.segment_sum(yb * row_gate[:, None].astype(yb.dtype), row_tok,
                            num_segments=n_tok + 1)[:n_tok]
    return y.reshape(B_, T_, D_)


def setup_inputs(seed: int = 0) -> dict:
    key = jax.random.key(seed)
    ks = iter(jax.random.split(key, 48))

    def nrm(shape, scale):
        return jax.random.normal(next(ks), shape, jnp.float32) * scale

    def near_one(shape):
        return 1.0 + nrm(shape, 0.1)

    L, Lr = DEPTH, DEPTH - 1
    x = nrm((BATCH, SEQ, D_MODEL), 1.0)
    c = nrm((BATCH, D_MODEL), 1.0)
    positions = (jax.random.randint(next(ks), (BATCH, 1), 0, 1024)
                 + jnp.arange(SEQ)[None, :]).astype(jnp.int32)
    w_ada = nrm((L, D_MODEL, 6 * D_MODEL), 0.5 * D_MODEL ** -0.5)
    b_ada = nrm((L, 6 * D_MODEL), 0.01)
    norm_gain = near_one((L, 2, D_MODEL))
    w_in_first = nrm((D_MODEL, N_IN_BASE), D_MODEL ** -0.5)
    w_in_rest = nrm((Lr, D_MODEL, N_IN_REST), D_MODEL ** -0.5)
    mu_shift = jax.random.uniform(next(ks), (L, RWKV_SHIFT_COLS), jnp.float32)
    mu_shift_v = jax.random.uniform(next(ks), (Lr, RWKV_VRES_LORA), jnp.float32)
    w0 = jax.random.uniform(next(ks), (L, RWKV_WIDTH), jnp.float32, -4.0, 0.0)
    rwkv_vec = jnp.stack([
        w0,
        nrm((L, RWKV_WIDTH), 0.1),
        0.85 + nrm((L, RWKV_WIDTH), 0.05),
        near_one((L, RWKV_WIDTH)),
        nrm((L, RWKV_WIDTH), 0.1),
        near_one((L, RWKV_WIDTH)),
        nrm((L, RWKV_WIDTH), 0.01),
    ], axis=1)
    rwkv_v0 = near_one((Lr, RWKV_WIDTH))
    rwkv_w2 = nrm((L, RWKV_DECAY_LORA, RWKV_WIDTH), 0.1)
    rwkv_a2 = nrm((L, RWKV_ICLR_LORA, RWKV_WIDTH), 0.1)
    rwkv_g2 = nrm((L, RWKV_GATE_LORA, RWKV_WIDTH), RWKV_GATE_LORA ** -0.5)
    rwkv_v2 = nrm((Lr, RWKV_VRES_LORA, RWKV_WIDTH), 0.1)
    pool_w = nrm((L, POOL_GROUPS, POOL_GROUP_DIM, POOL_GROUP_DIM), POOL_GROUP_DIM ** -0.5)
    pool_scale = near_one((L, POOL_WIDTH))
    mla_q_lat_gain = near_one((L, MLA_Q_LORA))
    mla_kv_lat_gain = near_one((L, MLA_KV_LORA))
    mla_wq_up = nrm((L, MLA_Q_LORA, MLA_HEADS * MLA_QK_DIM), MLA_Q_LORA ** -0.5)
    mla_wkv_up = nrm((L, MLA_KV_LORA, MLA_HEADS * (MLA_QK_NOPE + MLA_V_DIM)), MLA_KV_LORA ** -0.5)
    mla_qk_gain = near_one((L, 2, MLA_QK_DIM))
    w_out = nrm((L, D_MIX, D_MODEL), D_MIX ** -0.5)
    ffn_w_in = nrm((N_DENSE, D_MODEL, 2 * D_FF), D_MODEL ** -0.5)
    ffn_w_out = nrm((N_DENSE, D_FF, D_MODEL), D_FF ** -0.5)
    moe_router = nrm((N_MOE, D_MODEL, N_EXPERTS), D_MODEL ** -0.5)
    moe_w_in = nrm((N_MOE, N_EXPERTS, D_MODEL, 2 * D_FF_EXPERT), D_MODEL ** -0.5)
    moe_w_out = nrm((N_MOE, N_EXPERTS, D_FF_EXPERT, D_MODEL), D_FF_EXPERT ** -0.5)
    return {"x": x, "c": c, "positions": positions, "w_ada": w_ada, "b_ada": b_ada,
            "norm_gain": norm_gain, "w_in_first": w_in_first, "w_in_rest": w_in_rest,
            "mu_shift": mu_shift, "mu_shift_v": mu_shift_v, "rwkv_vec": rwkv_vec,
            "rwkv_v0": rwkv_v0, "rwkv_w2": rwkv_w2, "rwkv_a2": rwkv_a2, "rwkv_g2": rwkv_g2,
            "rwkv_v2": rwkv_v2, "pool_w": pool_w, "pool_scale": pool_scale,
            "mla_q_lat_gain": mla_q_lat_gain, "mla_kv_lat_gain": mla_kv_lat_gain,
            "mla_wq_up": mla_wq_up, "mla_wkv_up": mla_wkv_up, "mla_qk_gain": mla_qk_gain,
            "w_out": w_out, "ffn_w_in": ffn_w_in, "ffn_w_out": ffn_w_out,
            "moe_router": moe_router, "moe_w_in": moe_w_in, "moe_w_out": moe_w_out}


def reference(x, c, positions, w_ada, b_ada, norm_gain, w_in_first, w_in_rest, mu_shift,
              mu_shift_v, rwkv_vec, rwkv_v0, rwkv_w2, rwkv_a2, rwkv_g2, rwkv_v2, pool_w,
              pool_scale, mla_q_lat_gain, mla_kv_lat_gain, mla_wq_up, mla_wkv_up, mla_qk_gain,
              w_out, ffn_w_in, ffn_w_out, moe_router, moe_w_in, moe_w_out):
    cos, sin = rope_tables(positions)
    c_act = jax.nn.silu(c)
    v_first = None
    for l in range(DEPTH):
        mod = (c_act @ w_ada[l] + b_ada[l]).reshape(-1, 6, 1, D_MODEL)
        shift1, scale1, gate1, shift2, scale2, gate2 = (mod[:, i] for i in range(6))

        h = rms_norm(x, norm_gain[l, 0]) * (1 + scale1) + shift1
        z = h @ (w_in_first if l == 0 else w_in_rest[l - 1])
        zs = token_shift(z[..., :RWKV_SHIFT_COLS], mu_shift[l])
        r, k, v, wd, ad, gd = jnp.split(zs, RWKV_SPLITS, axis=-1)
        if l == 0:
            v_first = v
        else:
            vd = token_shift(z[..., N_IN_BASE:], mu_shift_v[l - 1])
            v = v + (v_first - v) * jax.nn.sigmoid(rwkv_v0[l - 1] + vd @ rwkv_v2[l - 1])
        y_rwkv = rwkv7_time_mix(r, k, v, wd, ad, gd, rwkv_vec[l], rwkv_w2[l], rwkv_a2[l], rwkv_g2[l])
        y_pool = multiscale_pool(z[..., OFF_POOL:OFF_QLAT], pool_w[l], pool_scale[l])
        y_mla = mla_attention(z[..., OFF_QLAT:OFF_KVLAT], z[..., OFF_KVLAT:OFF_KROPE],
                              z[..., OFF_KROPE:N_IN_BASE], cos, sin, mla_q_lat_gain[l],
                              mla_kv_lat_gain[l], mla_wq_up[l], mla_wkv_up[l], mla_qk_gain[l])
        mix = jnp.concatenate([y_rwkv, y_pool, y_mla], axis=-1) @ w_out[l]
        x = x + gate1 * mix

        h = rms_norm(x, norm_gain[l, 1]) * (1 + scale2) + shift2
        if l % 2 == 0:
            f = swiglu(h, ffn_w_in[l // 2], ffn_w_out[l // 2])
        else:
            f = moe_swiglu(h, moe_router[l // 2], moe_w_in[l // 2], moe_w_out[l // 2])
        x = x + gate2 * f
    return x
```

```python
import functools

import numpy as np
import jax
import jax.numpy as jnp
from jax import lax
from jax.experimental import pallas as pl
from jax.experimental.pallas import tpu as pltpu

F32 = jnp.float32
BF16 = jnp.bfloat16

D_MODEL = 1024
HEAD_DIM = 64
RWKV_WIDTH = 512
RWKV_HEADS = 8
POOL_WIDTH = 256
POOL_WINDOWS = (2, 4, 8, 16)
POOL_HALO = 16
MLA_HEADS = 4
MLA_QK_NOPE = 64
MLA_QK_ROPE = 32
MLA_QK_DIM = 96
MLA_Q_LORA = 256
MLA_KV_LORA = 128
ROPE_BASE = 10000.0
RWKV_DECAY_LORA = 64
RWKV_ICLR_LORA = 64
RWKV_VRES_LORA = 32
RWKV_GATE_LORA = 160
RWKV_LNX_EPS = 64e-5
D_FF = 2816
N_EXPERTS = 8
TOP_K = 2
D_FF_EXPERT = 3584
NORM_EPS = 1e-6
NEG_INF = -1e30

LANES = 128
SUBLANES = 8
VMEM_LIMIT = 56 * 1024 * 1024

ZR_COLS = 1920
Z_POOL_OFF = ZR_COLS
Z_QLAT_OFF = Z_POOL_OFF + POOL_WIDTH
Z_KVLAT_OFF = Z_QLAT_OFF + MLA_Q_LORA
Z_KROPE_OFF = Z_KVLAT_OFF + MLA_KV_LORA
Z_COLS = Z_KROPE_OFF + LANES

TM_MIX = 512
WKV_CHUNK = 64
TQ = 512
TM_FFN = 512
TF_FFN = 1408
MOE_BLOCK = 1024
TF_MOE = 896

NN = (((1,), (0,)), ((), ()))
NT = (((1,), (1,)), ((), ()))


def _dot(a, b, dims=NN):
    return lax.dot_general(a, b, dims, preferred_element_type=F32)


def _split2(a):
    hi = a.astype(BF16)
    lo = (a - hi.astype(F32)).astype(BF16)
    return hi, lo


def _mm(a, b, dims=NN, passes=3):
    if passes == 1:
        return _dot(a.astype(BF16), b.astype(BF16), dims)
    ah, al = _split2(a)
    bh, bl = _split2(b)
    return _dot(ah, bh, dims) + (_dot(ah, bl, dims) + _dot(al, bh, dims))


def _mm_exact_rhs(a, b_bf16, dims=NN):
    a0 = a.astype(BF16)
    r1 = a - a0.astype(F32)
    a1 = r1.astype(BF16)
    a2 = (r1 - a1.astype(F32)).astype(BF16)
    return _dot(a0, b_bf16, dims) + (_dot(a1, b_bf16, dims) + _dot(a2, b_bf16, dims))


def _sigmoid(x):
    return 1.0 / (1.0 + jnp.exp(-x))


def _silu(x):
    return x * _sigmoid(x)


def _rms(x, eps=NORM_EPS):
    return x * lax.rsqrt(jnp.mean(x * x, axis=-1, keepdims=True) + eps)


def _cparams(sem):
    return pltpu.CompilerParams(dimension_semantics=sem, vmem_limit_bytes=VMEM_LIMIT)


def _adaln_kernel(c_ref, w_ref, b_ref, o_ref):
    ca = _silu(c_ref[...])
    o_ref[0] = _mm(ca, w_ref[0]) + b_ref[0]


def _adaln(c, w_ada, b_ada):
    L = w_ada.shape[0]
    B = c.shape[0]
    n = w_ada.shape[2] // D_MODEL
    return pl.pallas_call(
        _adaln_kernel,
        grid=(L, n),
        in_specs=[pl.BlockSpec((B, D_MODEL), lambda l, j: (0, 0)),
                  pl.BlockSpec((1, D_MODEL, D_MODEL), lambda l, j: (l, 0, j)),
                  pl.BlockSpec((1, 1, D_MODEL), lambda l, j: (l, 0, j))],
        out_specs=pl.BlockSpec((1, B, D_MODEL), lambda l, j: (l, 0, j)),
        out_shape=jax.ShapeDtypeStruct((L, B, n * D_MODEL), F32),
        compiler_params=_cparams(("arbitrary", "arbitrary")),
    )(c, w_ada, b_ada.reshape(L, 1, -1))


def _rope(x, cosf, sinf, lane):
    up = pltpu.roll(x, LANES - MLA_QK_ROPE // 2, axis=1)
    dn = pltpu.roll(x, MLA_QK_ROPE // 2, axis=1)
    rot = jnp.where(lane < MLA_QK_NOPE + MLA_QK_ROPE // 2, -up, dn)
    return x * cosf + rot * sinf


def _mixin_kernel(x_ref, mod_ref, gain_ref, win_ref, pos_ref, freq_ref, poolw_ref, pools_ref,
                  qg_ref, kvg_ref, wq_ref, wk_ref, wv_ref, qkg_ref, bd_ref,
                  zr_ref, yp_ref, q_ref, k_ref, v_ref, ubuf):
    i = pl.program_id(1)
    tm = x_ref.shape[1]
    x = x_ref[0]
    mod = mod_ref[0]
    h = _rms(x) * gain_ref[...] * (1.0 + mod[1:2]) + mod[0:1]
    z = _dot(h.astype(BF16), win_ref[...])
    zr_ref[0] = z[:, :ZR_COLS]

    @pl.when(i == 0)
    def _():
        ubuf[0:POOL_HALO, :] = jnp.zeros((POOL_HALO, POOL_WIDTH), F32)

    u = z[:, Z_POOL_OFF:Z_POOL_OFF + POOL_WIDTH]
    ubuf[POOL_HALO:, :] = u
    ue = ubuf[...]
    s2 = ue + pltpu.roll(ue, 1, axis=0)
    s4 = s2 + pltpu.roll(s2, 2, axis=0)
    s8 = s4 + pltpu.roll(s4, 4, axis=0)
    s16 = s8 + pltpu.roll(s8, 8, axis=0)
    ubuf[0:POOL_HALO, :] = u[tm - POOL_HALO:, :]
    lane_p = lax.broadcasted_iota(jnp.int32, (tm, POOL_WIDTH), 1)
    grp = lane_p // (POOL_WIDTH // len(POOL_WINDOWS))
    win_sum = jnp.where(grp == 0, s2[POOL_HALO:], jnp.where(grp == 1, s4[POOL_HALO:],
                        jnp.where(grp == 2, s8[POOL_HALO:], s16[POOL_HALO:])))
    win = jnp.where(grp == 0, 2, jnp.where(grp == 1, 4, jnp.where(grp == 2, 8, 16)))
    t_abs = i * tm + lax.broadcasted_iota(jnp.int32, (tm, POOL_WIDTH), 0)
    cnt = jnp.minimum(t_abs + 1, win).astype(F32)
    p = win_sum / cnt - u
    yp = _dot(p.astype(BF16), poolw_ref[...]) * pools_ref[...]
    yp_ref[0] = yp.astype(BF16)

    lane = lax.broadcasted_iota(jnp.int32, (tm, LANES), 1)
    in_rope = (lane >= MLA_QK_NOPE) & (lane < MLA_QK_DIM)
    ang = pos_ref[0].astype(F32) * freq_ref[...]
    cosf = jnp.where(in_rope, jnp.cos(ang), 1.0)
    sinf = jnp.where(in_rope, jnp.sin(ang), 0.0)

    q_lat = z[:, Z_QLAT_OFF:Z_QLAT_OFF + MLA_Q_LORA]
    kv_lat = z[:, Z_KVLAT_OFF:Z_KVLAT_OFF + MLA_KV_LORA]
    k_rope = z[:, Z_KROPE_OFF:Z_KROPE_OFF + LANES]
    qn = (_rms(q_lat) * qg_ref[...]).astype(BF16)
    kvn = (_rms(kv_lat) * kvg_ref[...]).astype(BF16)
    q = _dot(qn, wq_ref[...])
    kx = _dot(kvn, wk_ref[...])
    v = _dot(kvn, wv_ref[...])
    k_pe = _rope(k_rope, cosf, sinf, lane)
    qs, ks = [], []
    for hd in range(MLA_HEADS):
        sl = slice(hd * LANES, (hd + 1) * LANES)
        qs.append(_rope(q[:, sl], cosf, sinf, lane))
        ks.append(kx[:, sl] + k_pe)
    q = jnp.concatenate(qs, axis=1)
    k = jnp.concatenate(ks, axis=1)
    qss = _mm_exact_rhs(q * q, bd_ref[...]) * (1.0 / MLA_QK_DIM)
    kss = _mm_exact_rhs(k * k, bd_ref[...]) * (1.0 / MLA_QK_DIM)
    qkg = qkg_ref[...]
    q = q * lax.rsqrt(qss + NORM_EPS) * qkg[0:1] * (MLA_QK_DIM ** -0.5)
    k = k * lax.rsqrt(kss + NORM_EPS) * qkg[1:2]
    q_ref[0] = q.astype(BF16)
    k_ref[0] = k.astype(BF16)
    v_ref[0] = v.astype(BF16)


def _mixin(x, mod_l, gain, win, pos3, freq, poolw, pools, qg, kvg, wq, wk, wv, qkg, bd128):
    B, T, _ = x.shape
    tm = TM_MIX
    const = lambda shape: pl.BlockSpec(shape, lambda b, i: tuple(0 for _ in shape))
    tok = lambda w: pl.BlockSpec((1, tm, w), lambda b, i: (b, i, 0))
    return pl.pallas_call(
        _mixin_kernel,
        grid=(B, T // tm),
        in_specs=[tok(D_MODEL),
                  pl.BlockSpec((1, 6, D_MODEL), lambda b, i: (b, 0, 0)),
                  const((1, D_MODEL)), const((D_MODEL, Z_COLS)),
                  tok(1), const((1, LANES)),
                  const((POOL_WIDTH, POOL_WIDTH)), const((1, POOL_WIDTH)),
                  const((1, MLA_Q_LORA)), const((1, MLA_KV_LORA)),
                  const((MLA_Q_LORA, MLA_HEADS * LANES)), const((MLA_KV_LORA, MLA_HEADS * LANES)),
                  const((MLA_KV_LORA, MLA_HEADS * HEAD_DIM)), const((2, MLA_HEADS * LANES)),
                  const((MLA_HEADS * LANES, MLA_HEADS * LANES))],
        out_specs=[tok(ZR_COLS), tok(POOL_WIDTH), tok(MLA_HEADS * LANES), tok(MLA_HEADS * LANES),
                   tok(MLA_HEADS * HEAD_DIM)],
        out_shape=[jax.ShapeDtypeStruct((B, T, ZR_COLS), F32),
                   jax.ShapeDtypeStruct((B, T, POOL_WIDTH), BF16),
                   jax.ShapeDtypeStruct((B, T, MLA_HEADS * LANES), BF16),
                   jax.ShapeDtypeStruct((B, T, MLA_HEADS * LANES), BF16),
                   jax.ShapeDtypeStruct((B, T, MLA_HEADS * HEAD_DIM), BF16)],
        scratch_shapes=[pltpu.VMEM((POOL_HALO + tm, POOL_WIDTH), F32)],
        compiler_params=_cparams(("arbitrary", "arbitrary")),
    )(x, mod_l, gain, win, pos3, freq, poolw, pools, qg, kvg, wq, wk, wv, qkg, bd128)


WKV_PASSES = 3


def _stack_heads(xp, lane):
    return jnp.concatenate([jnp.where(lane < HEAD_DIM, xp, 0.0),
                            jnp.where(lane >= HEAD_DIM, xp, 0.0)], axis=0)


def _wkv_pair(r, lw, k, v, kk, a, S, tri, masks):
    L = r.shape[0]
    lane = lax.broadcasted_iota(jnp.int32, (L, LANES), 1)
    cum = _mm_exact_rhs_left(tri, lw)
    cum_last = cum[L - 1:L, :]
    e_w = jnp.exp(cum)
    e_wm = jnp.exp(cum - lw)
    e_iw = jnp.exp(-cum)
    e_d = jnp.exp(cum_last - cum)
    beta = kk * a
    r_t = _stack_heads(r * e_w, lane)
    a_t = _stack_heads(-kk * e_wm, lane)
    b_t = _stack_heads(beta * e_iw, lane)
    k_t = _stack_heads(k * e_iw, lane)
    b_d = _stack_heads(beta * e_d, lane)
    k_d = _stack_heads(k * e_d, lane)
    v_s = _stack_heads(v, lane)
    n = 2 * L
    lhs = jnp.concatenate([a_t, r_t], axis=0)
    rhs = jnp.concatenate([b_t, k_t], axis=0)
    g = _mm(lhs, rhs, NT, WKV_PASSES)
    strict, incl, levels = masks
    a_ab = jnp.where(strict, g[:n, :n], 0.0)
    a_ak = jnp.where(strict, g[:n, n:], 0.0)
    s_rb = jnp.where(incl, g[n:, :n], 0.0)
    s_rk = jnp.where(incl, g[n:, n:], 0.0)
    eye = jnp.where(levels[0][1], 1.0, 0.0)
    tinv = eye + jnp.where(levels[0][0], a_ab, 0.0)
    for lvl_mask, _ in levels[1:]:
        e = jnp.where(lvl_mask, a_ab, 0.0)
        tinv = tinv + _mm(tinv, _mm(e, tinv, NN, WKV_PASSES), NN, WKV_PASSES)
    ls = _mm(lhs, S, NT, WKV_PASSES)
    rhs_u = ls[:n] + _mm(a_ak, v_s, NN, WKV_PASSES)
    u = _mm(tinv, rhs_u, NN, WKV_PASSES)
    uv = jnp.concatenate([u, v_s], axis=0)
    y_s = ls[n:] + _mm(jnp.concatenate([s_rb, s_rk], axis=1), uv, NN, WKV_PASSES)
    y = y_s[:L] + y_s[L:]
    bk = jnp.concatenate([b_d, k_d], axis=0)
    s_new = S * jnp.exp(cum_last) + _mm(uv.T, bk, NN, WKV_PASSES)
    return y, s_new


def _mm_exact_rhs_left(tri_bf16, x):
    x0 = x.astype(BF16)
    r1 = x - x0.astype(F32)
    x1 = r1.astype(BF16)
    x2 = (r1 - x1.astype(F32)).astype(BF16)
    return _dot(tri_bf16, x0) + (_dot(tri_bf16, x1) + _dot(tri_bf16, x2))


def _wkv_masks(L):
    n = 2 * L
    row = lax.broadcasted_iota(jnp.int32, (n, n), 0)
    col = lax.broadcasted_iota(jnp.int32, (n, n), 1)
    strict = row > col
    incl = row >= col
    levels = []
    m = 1
    while m < L:
        same = (row // (2 * m)) == (col // (2 * m))
        lvl = same & ((row % (2 * m)) >= m) & ((col % (2 * m)) < m)
        levels.append((lvl, row == col))
        m *= 2
    return strict, incl, levels


def _rwkv_kernel(has_vres, *refs):
    if has_vres:
        (z_ref, vf_ref, mu_ref, vec_ref, w2a2_ref, g2v2_ref, bd_ref,
         y_ref, carry, state) = refs
    else:
        (z_ref, mu_ref, vec_ref, w2a2_ref, g2v2_ref, bd_ref,
         y_ref, vout_ref, carry, state) = refs
    c = pl.program_id(1)
    L = z_ref.shape[1]
    W = RWKV_WIDTH

    @pl.when(c == 0)
    def _():
        carry[...] = jnp.zeros(carry.shape, F32)
        state[...] = jnp.zeros(state.shape, F32)

    z = z_ref[0]
    row = lax.broadcasted_iota(jnp.int32, z.shape, 0)
    prev = jnp.where(row == 0, carry[SUBLANES - 1:SUBLANES, :], pltpu.roll(z, 1, axis=0))
    carry[...] = z[L - SUBLANES:, :]
    zs = z + mu_ref[...] * (prev - z)
    r = zs[:, 0:W]
    k = zs[:, W:2 * W]
    v = zs[:, 2 * W:3 * W]
    wa = zs[:, 3 * W:3 * W + LANES]
    gb = zs[:, 3 * W + LANES:ZR_COLS]
    vec = vec_ref[...]
    w0, a0, k_k, k_a, r_k, ln_g, ln_b, v0 = (vec[j:j + 1] for j in range(8))

    lane_a = lax.broadcasted_iota(jnp.int32, wa.shape, 1)
    t1 = _dot(jnp.where(lane_a < RWKV_DECAY_LORA, jnp.tanh(wa), wa).astype(BF16), w2a2_ref[...])
    lane_g = lax.broadcasted_iota(jnp.int32, gb.shape, 1)
    t2 = _dot(jnp.where(lane_g < RWKV_GATE_LORA, _sigmoid(gb), gb).astype(BF16), g2v2_ref[...])
    xw = w0 + t1[:, :W]
    w_log = -(jnp.maximum(-xw, 0.0) + jnp.log1p(jnp.exp(-jnp.abs(xw)))) - 0.5
    lw = -jnp.exp(w_log)
    a = _sigmoid(a0 + t1[:, W:])
    g = t2[:, :W]
    if has_vres:
        v = v + (vf_ref[0] - v) * _sigmoid(v0 + t2[:, W:])
    else:
        vout_ref[0] = v
    kk = k * k_k
    bd = bd_ref[...]
    nrm = jnp.sqrt(_mm_exact_rhs(kk * kk, bd))
    kk = kk / jnp.maximum(nrm, 1e-12)
    k = k * (1.0 + (a - 1.0) * k_a)

    masks = _wkv_masks(L)
    rowt = lax.broadcasted_iota(jnp.int32, (L, L), 0)
    colt = lax.broadcasted_iota(jnp.int32, (L, L), 1)
    tri = jnp.where(rowt >= colt, 1.0, 0.0).astype(BF16)
    ys = []
    for p in range(RWKV_HEADS // 2):
        sl = slice(p * LANES, (p + 1) * LANES)
        y_p, s_new = _wkv_pair(r[:, sl], lw[:, sl], k[:, sl], v[:, sl], kk[:, sl], a[:, sl],
                               state[p], tri, masks)
        state[p] = s_new
        ys.append(y_p)
    y = jnp.concatenate(ys, axis=1)

    inv = 1.0 / HEAD_DIM
    mean = _mm_exact_rhs(y, bd) * inv
    yc = y - mean
    var = _mm_exact_rhs(yc * yc, bd) * inv
    yn = yc * lax.rsqrt(var + RWKV_LNX_EPS) * ln_g + ln_b
    bonus = _mm_exact_rhs(r * k * r_k, bd) * v
    y_ref[0] = ((yn + bonus) * g).astype(BF16)


def _rwkv(zr, v_first, mu, vec8, w2a2, g2v2, bd64):
    B, T, _ = zr.shape
    L = WKV_CHUNK
    has_vres = v_first is not None
    const = lambda shape: pl.BlockSpec(shape, lambda b, c: tuple(0 for _ in shape))
    tok = lambda w: pl.BlockSpec((1, L, w), lambda b, c: (b, c, 0))
    in_specs = [tok(ZR_COLS)]
    args = [zr]
    if has_vres:
        in_specs.append(tok(RWKV_WIDTH))
        args.append(v_first)
    in_specs += [const((1, ZR_COLS)), const((8, RWKV_WIDTH)), const((LANES, 2 * RWKV_WIDTH)),
                 const((2 * LANES, 2 * RWKV_WIDTH)), const((RWKV_WIDTH, RWKV_WIDTH))]
    args += [mu, vec8, w2a2, g2v2, bd64]
    out_specs = [tok(RWKV_WIDTH)]
    out_shape = [jax.ShapeDtypeStruct((B, T, RWKV_WIDTH), BF16)]
    if not has_vres:
        out_specs.append(tok(RWKV_WIDTH))
        out_shape.append(jax.ShapeDtypeStruct((B, T, RWKV_WIDTH), F32))
    outs = pl.pallas_call(
        functools.partial(_rwkv_kernel, has_vres),
        grid=(B, T // L),
        in_specs=in_specs, out_specs=out_specs, out_shape=out_shape,
        scratch_shapes=[pltpu.VMEM((SUBLANES, ZR_COLS), F32),
                        pltpu.VMEM((RWKV_HEADS // 2, LANES, LANES), F32)],
        compiler_params=_cparams(("arbitrary", "arbitrary")),
    )(*args)
    return (outs[0], v_first) if has_vres else (outs[0], outs[1])


def _attn_step(q_ref, k_ref, v_ref, m_sc, l_sc, acc_sc, masked):
    tq = q_ref.shape[1]
    tk = k_ref.shape[1]
    lane = lax.broadcasted_iota(jnp.int32, (tq, LANES), 1)
    if masked:
        rowi = lax.broadcasted_iota(jnp.int32, (tq, tk), 0)
        coli = lax.broadcasted_iota(jnp.int32, (tq, tk), 1)
        keep = coli <= rowi
    for pr in range(MLA_HEADS // 2):
        vp = v_ref[0, :, pr * LANES:(pr + 1) * LANES]
        alphas, pvs = [], []
        for hh in range(2):
            hd = 2 * pr + hh
            q = q_ref[0, :, hd * LANES:(hd + 1) * LANES]
            k = k_ref[0, :, hd * LANES:(hd + 1) * LANES]
            s = _dot(q, k, NT)
            if masked:
                s = jnp.where(keep, s, NEG_INF)
            m_prev = m_sc[hd]
            m_new = jnp.maximum(m_prev, jnp.max(s, axis=-1, keepdims=True))
            alpha = jnp.exp(m_prev - m_new)
            p = jnp.exp(s - jnp.concatenate([m_new] * (tk // LANES), axis=1))
            l_sc[hd] = alpha * l_sc[hd] + jnp.sum(p, axis=-1, keepdims=True)
            m_sc[hd] = m_new
            alphas.append(alpha)
            pvs.append(_dot(p.astype(BF16), vp))
        first = lane < HEAD_DIM
        acc_sc[pr] = (acc_sc[pr] * jnp.where(first, alphas[0], alphas[1])
                      + jnp.where(first, pvs[0], pvs[1]))


def _attn_kernel(q_ref, k_ref, v_ref, o_ref, m_sc, l_sc, acc_sc):
    i = pl.program_id(1)
    j = pl.program_id(2)

    @pl.when(j == 0)
    def _():
        m_sc[...] = jnp.full(m_sc.shape, NEG_INF, F32)
        l_sc[...] = jnp.zeros(l_sc.shape, F32)
        acc_sc[...] = jnp.zeros(acc_sc.shape, F32)

    @pl.when(j < i)
    def _():
        _attn_step(q_ref, k_ref, v_ref, m_sc, l_sc, acc_sc, masked=False)

    @pl.when(j == i)
    def _():
        _attn_step(q_ref, k_ref, v_ref, m_sc, l_sc, acc_sc, masked=True)
        tq = q_ref.shape[1]
        lane = lax.broadcasted_iota(jnp.int32, (tq, LANES), 1)
        outs = []
        for pr in range(MLA_HEADS // 2):
            den = jnp.where(lane < HEAD_DIM, l_sc[2 * pr], l_sc[2 * pr + 1])
            outs.append(acc_sc[pr] / den)
        o_ref[0] = jnp.concatenate(outs, axis=1).astype(BF16)


def _attention(q, k, v):
    B, T, _ = q.shape
    nq = T // TQ
    return pl.pallas_call(
        _attn_kernel,
        grid=(B, nq, nq),
        in_specs=[pl.BlockSpec((1, TQ, MLA_HEADS * LANES), lambda b, i, j: (b, i, 0)),
                  pl.BlockSpec((1, TQ, MLA_HEADS * LANES), lambda b, i, j: (b, jnp.minimum(i, j), 0)),
                  pl.BlockSpec((1, TQ, MLA_HEADS * HEAD_DIM), lambda b, i, j: (b, jnp.minimum(i, j), 0))],
        out_specs=pl.BlockSpec((1, TQ, MLA_HEADS * HEAD_DIM), lambda b, i, j: (b, i, 0)),
        out_shape=jax.ShapeDtypeStruct((B, T, MLA_HEADS * HEAD_DIM), BF16),
        scratch_shapes=[pltpu.VMEM((MLA_HEADS, TQ, LANES), F32),
                        pltpu.VMEM((MLA_HEADS, TQ, LANES), F32),
                        pltpu.VMEM((MLA_HEADS // 2, TQ, LANES), F32)],
        compiler_params=_cparams(("arbitrary", "arbitrary", "arbitrary")),
    )(q, k, v)


def _mixout_kernel(has_router, *refs):
    if has_router:
        (x_ref, yr_ref, yp_ref, ym_ref, wo_ref, mod_ref, gain_ref, rt_ref,
         xo_ref, h_ref, route_ref) = refs
    else:
        x_ref, yr_ref, yp_ref, ym_ref, wo_ref, mod_ref, gain_ref, xo_ref, h_ref = refs
    mod = mod_ref[0]
    o1 = RWKV_WIDTH
    o2 = RWKV_WIDTH + POOL_WIDTH
    mix = (_dot(yr_ref[0], wo_ref[0:o1, :]) + _dot(yp_ref[0], wo_ref[o1:o2, :])
           + _dot(ym_ref[0], wo_ref[o2:, :]))
    x = x_ref[0] + mod[2:3] * mix
    xo_ref[0] = x
    h = _rms(x) * gain_ref[...] * (1.0 + mod[4:5]) + mod[3:4]
    h_ref[0] = h.astype(BF16)
    if has_router:
        logits = _mm(h, rt_ref[...])
        lane = lax.broadcasted_iota(jnp.int32, logits.shape, 1).astype(F32)
        lg = jnp.where(lane < N_EXPERTS, logits, -jnp.inf)
        m1 = jnp.max(lg, axis=-1, keepdims=True)
        i1 = jnp.min(jnp.where(lg == m1, lane, float(LANES)), axis=-1, keepdims=True)
        lg2 = jnp.where(lane == i1, -jnp.inf, lg)
        m2 = jnp.max(lg2, axis=-1, keepdims=True)
        i2 = jnp.min(jnp.where(lg2 == m2, lane, float(LANES)), axis=-1, keepdims=True)
        e2 = jnp.exp(m2 - m1)
        g1 = 1.0 / (1.0 + e2)
        g2 = e2 / (1.0 + e2)
        route_ref[0] = jnp.where(lane == 0, i1,
                                 jnp.where(lane == 1, i2,
                                           jnp.where(lane == 2, g1, jnp.where(lane == 3, g2, 0.0))))


def _mixout(x, yr, yp, ym, wo, mod_l, gain, router_p):
    B, T, _ = x.shape
    tm = TM_MIX
    has_router = router_p is not None
    const = lambda shape: pl.BlockSpec(shape, lambda b, i: tuple(0 for _ in shape))
    tok = lambda w: pl.BlockSpec((1, tm, w), lambda b, i: (b, i, 0))
    in_specs = [tok(D_MODEL), tok(RWKV_WIDTH), tok(POOL_WIDTH), tok(MLA_HEADS * HEAD_DIM),
                const((D_MODEL, D_MODEL)), pl.BlockSpec((1, 6, D_MODEL), lambda b, i: (b, 0, 0)),
                const((1, D_MODEL))]
    args = [x, yr, yp, ym, wo, mod_l, gain]
    out_specs = [tok(D_MODEL), tok(D_MODEL)]
    out_shape = [jax.ShapeDtypeStruct((B, T, D_MODEL), F32), jax.ShapeDtypeStruct((B, T, D_MODEL), BF16)]
    if has_router:
        in_specs.append(const((D_MODEL, LANES)))
        args.append(router_p)
        out_specs.append(tok(LANES))
        out_shape.append(jax.ShapeDtypeStruct((B, T, LANES), F32))
    return pl.pallas_call(
        functools.partial(_mixout_kernel, has_router),
        grid=(B, T // tm),
        in_specs=in_specs, out_specs=out_specs, out_shape=out_shape,
        compiler_params=_cparams(("arbitrary", "arbitrary")),
    )(*args)


def _ffn_kernel(h_ref, wg_ref, wu_ref, wo_ref, x_ref, mod_ref, o_ref, acc):
    j = pl.program_id(1)

    @pl.when(j == 0)
    def _():
        acc[...] = jnp.zeros(acc.shape, F32)

    h = h_ref[...]
    gg = _dot(h, wg_ref[...])
    uu = _dot(h, wu_ref[...])
    acc[...] += _dot((_silu(gg) * uu).astype(BF16), wo_ref[...])

    @pl.when(j == pl.num_programs(1) - 1)
    def _():
        o_ref[...] = x_ref[...] + mod_ref[0][5:6] * acc[...]


def _ffn(h2, w_in, w_out, x, mod_l):
    N = h2.shape[0]
    T = N // mod_l.shape[0]
    tm, tf = TM_FFN, TF_FFN
    nf = D_FF // tf
    per_b = T // tm
    return pl.pallas_call(
        _ffn_kernel,
        grid=(N // tm, nf),
        in_specs=[pl.BlockSpec((tm, D_MODEL), lambda i, j: (i, 0)),
                  pl.BlockSpec((D_MODEL, tf), lambda i, j: (0, j)),
                  pl.BlockSpec((D_MODEL, tf), lambda i, j: (0, j + nf)),
                  pl.BlockSpec((tf, D_MODEL), lambda i, j: (j, 0)),
                  pl.BlockSpec((tm, D_MODEL), lambda i, j: (i, 0)),
                  pl.BlockSpec((1, 6, D_MODEL), lambda i, j: (i // per_b, 0, 0))],
        out_specs=pl.BlockSpec((tm, D_MODEL), lambda i, j: (i, 0)),
        out_shape=jax.ShapeDtypeStruct((N, D_MODEL), F32),
        scratch_shapes=[pltpu.VMEM((tm, D_MODEL), F32)],
        compiler_params=_cparams(("arbitrary", "arbitrary")),
    )(h2, w_in, w_in, w_out, x, mod_l)


def _moe_kernel(be_ref, nb_ref, x_ref, wg_ref, wu_ref, wo_ref, gate_ref, o_ref, acc):
    i = pl.program_id(0)
    j = pl.program_id(1)

    @pl.when(j == 0)
    def _():
        acc[...] = jnp.zeros(acc.shape, F32)

    @pl.when(i < nb_ref[0])
    def _():
        x = x_ref[...]
        gg = _dot(x, wg_ref[0])
        uu = _dot(x, wu_ref[0])
        acc[...] += _dot((_silu(gg) * uu).astype(BF16), wo_ref[0])

    @pl.when(j == pl.num_programs(1) - 1)
    def _():
        o_ref[...] = acc[...] * gate_ref[...]


def _moe_experts(xs, w_in, w_out, row_gate, block_exp, n_used):
    n_rows = xs.shape[0]
    tm, tf = MOE_BLOCK, TF_MOE
    nf = D_FF_EXPERT // tf
    grid_spec = pltpu.PrefetchScalarGridSpec(
        num_scalar_prefetch=2,
        grid=(n_rows // tm, nf),
        in_specs=[pl.BlockSpec((tm, D_MODEL), lambda i, j, be, nb: (i, 0)),
                  pl.BlockSpec((1, D_MODEL, tf), lambda i, j, be, nb: (be[i], 0, j)),
                  pl.BlockSpec((1, D_MODEL, tf), lambda i, j, be, nb: (be[i], 0, j + nf)),
                  pl.BlockSpec((1, tf, D_MODEL), lambda i, j, be, nb: (be[i], j, 0)),
                  pl.BlockSpec((tm, 1), lambda i, j, be, nb: (i, 0))],
        out_specs=pl.BlockSpec((tm, D_MODEL), lambda i, j, be, nb: (i, 0)),
        scratch_shapes=[pltpu.VMEM((tm, D_MODEL), F32)])
    return pl.pallas_call(
        _moe_kernel,
        grid_spec=grid_spec,
        out_shape=jax.ShapeDtypeStruct((n_rows, D_MODEL), F32),
        compiler_params=_cparams(("arbitrary", "arbitrary")),
    )(block_exp, n_used, xs, w_in, w_in, w_out, row_gate)


def _combine_kernel(x_ref, ya_ref, yb_ref, mod_ref, o_ref):
    o_ref[...] = x_ref[...] + mod_ref[0][5:6] * (ya_ref[...] + yb_ref[...])


def _combine(x, ya, yb, mod_l):
    N = x.shape[0]
    T = N // mod_l.shape[0]
    tm = 1024
    per_b = T // tm
    tok = pl.BlockSpec((tm, D_MODEL), lambda i: (i, 0))
    return pl.pallas_call(
        _combine_kernel,
        grid=(N // tm,),
        in_specs=[tok, tok, tok, pl.BlockSpec((1, 6, D_MODEL), lambda i: (i // per_b, 0, 0))],
        out_specs=tok,
        out_shape=jax.ShapeDtypeStruct((N, D_MODEL), F32),
        compiler_params=_cparams(("arbitrary",)),
    )(x, ya, yb, mod_l)


def _moe(h2, route, w_in, w_out, x, mod_l):
    N = h2.shape[0]
    blk = MOE_BLOCK
    exp_flat = route[:, 0:TOP_K].astype(jnp.int32).reshape(-1)
    gate_flat = route[:, TOP_K:2 * TOP_K].reshape(-1)
    n_asg = N * TOP_K
    onehot = (exp_flat[:, None] == jnp.arange(N_EXPERTS)[None, :]).astype(jnp.int32)
    csum = jnp.cumsum(onehot, axis=0)
    rank = jnp.take_along_axis(csum, exp_flat[:, None], axis=1)[:, 0] - 1
    counts = csum[-1]
    padded = (counts + blk - 1) // blk * blk
    pend = jnp.cumsum(padded)
    pstart = pend - padded
    dest = pstart[exp_flat] + rank
    n_blocks = n_asg // blk + N_EXPERTS
    n_rows = n_blocks * blk
    tok_flat = (jnp.arange(n_asg) // TOP_K).astype(jnp.int32)
    row_tok = jnp.full((n_rows,), N, jnp.int32).at[dest].set(tok_flat)
    row_gate = jnp.zeros((n_rows,), F32).at[dest].set(gate_flat)
    block_exp = jnp.clip(jnp.searchsorted(pend, jnp.arange(n_blocks) * blk, side='right'),
                         0, N_EXPERTS - 1).astype(jnp.int32)
    n_used = (pend[-1:] // blk).astype(jnp.int32)
    h_pad = jnp.concatenate([h2, jnp.zeros((1, D_MODEL), h2.dtype)], axis=0)
    xs = jnp.take(h_pad, row_tok, axis=0)
    yb = _moe_experts(xs, w_in, w_out, row_gate[:, None], block_exp, n_used)
    d2 = dest.reshape(N, TOP_K)
    ya = jnp.take(yb, d2[:, 0], axis=0)
    yc = jnp.take(yb, d2[:, 1], axis=0)
    return _combine(x, ya, yc, mod_l)


def _layout_w_in(w, has_vres):
    W = RWKV_WIDTH
    off_gd = 3 * W + RWKV_DECAY_LORA + RWKV_ICLR_LORA
    off_pool = off_gd + RWKV_GATE_LORA
    off_q = off_pool + POOL_WIDTH
    off_kv = off_q + MLA_Q_LORA
    off_kr = off_kv + MLA_KV_LORA
    n_base = off_kr + MLA_QK_ROPE
    d = w.shape[0]
    zeros = lambda n: jnp.zeros((d, n), w.dtype)
    vd = w[:, n_base:n_base + RWKV_VRES_LORA] if has_vres else zeros(RWKV_VRES_LORA)
    cols = [w[:, :off_gd], w[:, off_gd:off_pool], vd, zeros(ZR_COLS - off_pool - RWKV_VRES_LORA),
            w[:, off_pool:off_q], w[:, off_q:off_kv], w[:, off_kv:off_kr],
            zeros(MLA_QK_NOPE), w[:, off_kr:n_base], zeros(LANES - MLA_QK_DIM)]
    return jnp.concatenate(cols, axis=1).astype(BF16)


def _pad_heads(w, per_head, keep_from, keep_n):
    K = w.shape[0]
    wh = w.reshape(K, MLA_HEADS, per_head)[:, :, keep_from:keep_from + keep_n]
    wh = jnp.pad(wh, ((0, 0), (0, 0), (0, LANES - keep_n)))
    return wh.reshape(K, MLA_HEADS * LANES)


def kernel(x, c, positions, w_ada, b_ada, norm_gain, w_in_first, w_in_rest, mu_shift, mu_shift_v,
           rwkv_vec, rwkv_v0, rwkv_w2, rwkv_a2, rwkv_g2, rwkv_v2, pool_w, pool_scale,
           mla_q_lat_gain, mla_kv_lat_gain, mla_wq_up, mla_wkv_up, mla_qk_gain, w_out, ffn_w_in,
           ffn_w_out, moe_router, moe_w_in, moe_w_out):
    B, T, D = x.shape
    depth = w_ada.shape[0]
    W = RWKV_WIDTH
    mod = _adaln(c, w_ada, b_ada).reshape(depth, B, 6, D)
    pos3 = positions.reshape(B, T, 1)
    inv_freq = ROPE_BASE ** (-jnp.arange(0, MLA_QK_ROPE, 2, dtype=F32) / MLA_QK_ROPE)
    freq = jnp.concatenate([jnp.zeros((MLA_QK_NOPE,), F32), inv_freq, inv_freq,
                            jnp.zeros((LANES - MLA_QK_DIM,), F32)]).reshape(1, LANES)
    hid = np.arange(W) // HEAD_DIM
    bd64 = jnp.asarray(hid[:, None] == hid[None, :], BF16)
    bid = np.arange(MLA_HEADS * LANES) // LANES
    bd128 = jnp.asarray(bid[:, None] == bid[None, :], BF16)

    v_first = None
    for l in range(depth):
        has_vres = l > 0
        mod_l = mod[l]
        win = _layout_w_in(w_in_first if l == 0 else w_in_rest[l - 1], has_vres)
        poolw = jax.scipy.linalg.block_diag(*[pool_w[l, g] for g in range(len(POOL_WINDOWS))]).astype(BF16)
        wq = _pad_heads(mla_wq_up[l], MLA_QK_DIM, 0, MLA_QK_DIM).astype(BF16)
        wk = _pad_heads(mla_wkv_up[l], MLA_QK_NOPE + HEAD_DIM, 0, MLA_QK_NOPE).astype(BF16)
        wv = mla_wkv_up[l].reshape(MLA_KV_LORA, MLA_HEADS, MLA_QK_NOPE + HEAD_DIM)[:, :, MLA_QK_NOPE:]
        wv = wv.reshape(MLA_KV_LORA, MLA_HEADS * HEAD_DIM).astype(BF16)
        qkg = jnp.tile(jnp.pad(mla_qk_gain[l], ((0, 0), (0, LANES - MLA_QK_DIM))), (1, MLA_HEADS))
        zr, y_pool, q, k, v = _mixin(
            x, mod_l, norm_gain[l, 0].reshape(1, D), win, pos3, freq, poolw,
            pool_scale[l].reshape(1, -1), mla_q_lat_gain[l].reshape(1, -1),
            mla_kv_lat_gain[l].reshape(1, -1), wq, wk, wv, qkg, bd128)

        pad_mu = ZR_COLS - mu_shift.shape[1] - RWKV_VRES_LORA
        mu_v = mu_shift_v[l - 1] if has_vres else jnp.zeros((RWKV_VRES_LORA,), F32)
        mu = jnp.concatenate([mu_shift[l], mu_v, jnp.zeros((pad_mu,), F32)]).reshape(1, ZR_COLS)
        v0 = rwkv_v0[l - 1] if has_vres else jnp.zeros((W,), F32)
        vec8 = jnp.concatenate([rwkv_vec[l], v0[None]], axis=0)
        w2a2 = jax.scipy.linalg.block_diag(rwkv_w2[l], rwkv_a2[l]).astype(BF16)
        g2 = jnp.pad(rwkv_g2[l], ((0, 2 * LANES - RWKV_GATE_LORA), (0, 0)))
        if has_vres:
            v2 = jnp.pad(rwkv_v2[l - 1], ((RWKV_GATE_LORA, 2 * LANES - RWKV_GATE_LORA - RWKV_VRES_LORA), (0, 0)))
        else:
            v2 = jnp.zeros((2 * LANES, W), F32)
        g2v2 = jnp.concatenate([g2, v2], axis=1).astype(BF16)
        y_rwkv, v_first = _rwkv(zr, v_first, mu, vec8, w2a2, g2v2, bd64)

        y_mla = _attention(q, k, v)

        is_moe = (l % 2 == 1)
        router_p = None
        if is_moe:
            router_p = jnp.pad(moe_router[l // 2], ((0, 0), (0, LANES - N_EXPERTS)))
        outs = _mixout(x, y_rwkv, y_pool, y_mla, w_out[l].astype(BF16), mod_l,
                       norm_gain[l, 1].reshape(1, D), router_p)
        x_mid, h2 = outs[0], outs[1]
        h2f = h2.reshape(B * T, D)
        xf = x_mid.reshape(B * T, D)
        if is_moe:
            xo = _moe(h2f, outs[2].reshape(B * T, LANES), moe_w_in[l // 2].astype(BF16),
                      moe_w_out[l // 2].astype(BF16), xf, mod_l)
        else:
            xo = _ffn(h2f, ffn_w_in[l // 2].astype(BF16), ffn_w_out[l // 2].astype(BF16), xf, mod_l)
        x = xo.reshape(B, T, D)
    return x
```

```python
import functools

import numpy as np
import jax
import jax.numpy as jnp
from jax import lax
from jax.experimental import pallas as pl
from jax.experimental.pallas import tpu as pltpu

F32 = jnp.float32
BF16 = jnp.bfloat16

D_MODEL = 1024
HEAD_DIM = 64
RWKV_WIDTH = 512
RWKV_HEADS = 8
POOL_WIDTH = 256
POOL_WINDOWS = (2, 4, 8, 16)
POOL_HALO = 16
MLA_HEADS = 4
MLA_QK_NOPE = 64
MLA_QK_ROPE = 32
MLA_QK_DIM = 96
MLA_Q_LORA = 256
MLA_KV_LORA = 128
ROPE_BASE = 10000.0
RWKV_DECAY_LORA = 64
RWKV_ICLR_LORA = 64
RWKV_VRES_LORA = 32
RWKV_GATE_LORA = 160
RWKV_LNX_EPS = 64e-5
D_FF = 2816
N_EXPERTS = 8
TOP_K = 2
D_FF_EXPERT = 3584
NORM_EPS = 1e-6
NEG_INF = -1e30

LANES = 128
SUBLANES = 8
VMEM_LIMIT = 56 * 1024 * 1024

ZR_COLS = 1920
Z_POOL_OFF = ZR_COLS
Z_QLAT_OFF = Z_POOL_OFF + POOL_WIDTH
Z_KVLAT_OFF = Z_QLAT_OFF + MLA_Q_LORA
Z_KROPE_OFF = Z_KVLAT_OFF + MLA_KV_LORA
Z_COLS = Z_KROPE_OFF + LANES

TM_MIX = 512
WKV_CHUNK = 64
TQ = 512
TM_FFN = 512
TF_FFN = 1408
MOE_BLOCK = 1024
TF_MOE = 896

NN = (((1,), (0,)), ((), ()))
NT = (((1,), (1,)), ((), ()))


def _dot(a, b, dims=NN):
    return lax.dot_general(a, b, dims, preferred_element_type=F32)


def _split2(a):
    hi = a.astype(BF16)
    lo = (a - hi.astype(F32)).astype(BF16)
    return hi, lo


def _mm(a, b, dims=NN, passes=3):
    if passes == 1:
        return _dot(a.astype(BF16), b.astype(BF16), dims)
    ah, al = _split2(a)
    bh, bl = _split2(b)
    return _dot(ah, bh, dims) + (_dot(ah, bl, dims) + _dot(al, bh, dims))


def _mm_exact_rhs(a, b_bf16, dims=NN):
    a0 = a.astype(BF16)
    r1 = a - a0.astype(F32)
    a1 = r1.astype(BF16)
    a2 = (r1 - a1.astype(F32)).astype(BF16)
    return _dot(a0, b_bf16, dims) + (_dot(a1, b_bf16, dims) + _dot(a2, b_bf16, dims))


def _sigmoid(x):
    return 1.0 / (1.0 + jnp.exp(-x))


def _silu(x):
    return x * _sigmoid(x)


def _rms(x, eps=NORM_EPS):
    return x * lax.rsqrt(jnp.mean(x * x, axis=-1, keepdims=True) + eps)


def _cparams(sem):
    return pltpu.CompilerParams(dimension_semantics=sem, vmem_limit_bytes=VMEM_LIMIT)


def _adaln_kernel(c_ref, w_ref, b_ref, o_ref):
    ca = _silu(c_ref[...])
    o_ref[0] = _mm(ca, w_ref[0]) + b_ref[0]


def _adaln(c, w_ada, b_ada):
    L = w_ada.shape[0]
    B = c.shape[0]
    n = w_ada.shape[2] // D_MODEL
    return pl.pallas_call(
        _adaln_kernel,
        grid=(L, n),
        in_specs=[pl.BlockSpec((B, D_MODEL), lambda l, j: (0, 0)),
                  pl.BlockSpec((1, D_MODEL, D_MODEL), lambda l, j: (l, 0, j)),
                  pl.BlockSpec((1, 1, D_MODEL), lambda l, j: (l, 0, j))],
        out_specs=pl.BlockSpec((1, B, D_MODEL), lambda l, j: (l, 0, j)),
        out_shape=jax.ShapeDtypeStruct((L, B, n * D_MODEL), F32),
        compiler_params=_cparams(("arbitrary", "arbitrary")),
    )(c, w_ada, b_ada.reshape(L, 1, -1))


def _rope(x, cosf, sinf, lane):
    up = pltpu.roll(x, LANES - MLA_QK_ROPE // 2, axis=1)
    dn = pltpu.roll(x, MLA_QK_ROPE // 2, axis=1)
    rot = jnp.where(lane < MLA_QK_NOPE + MLA_QK_ROPE // 2, -up, dn)
    return x * cosf + rot * sinf


def _mixin_kernel(x_ref, mod_ref, gain_ref, win_ref, pos_ref, freq_ref, poolw_ref, pools_ref,
                  qg_ref, kvg_ref, wq_ref, wk_ref, wv_ref, qkg_ref, bd_ref,
                  zr_ref, yp_ref, q_ref, k_ref, v_ref, ubuf):
    i = pl.program_id(1)
    tm = x_ref.shape[1]
    x = x_ref[0]
    mod = mod_ref[0]
    h = _rms(x) * gain_ref[...] * (1.0 + mod[1:2]) + mod[0:1]
    z = _dot(h.astype(BF16), win_ref[...])
    zr_ref[0] = z[:, :ZR_COLS]

    @pl.when(i == 0)
    def _():
        ubuf[0:POOL_HALO, :] = jnp.zeros((POOL_HALO, POOL_WIDTH), F32)

    u = z[:, Z_POOL_OFF:Z_POOL_OFF + POOL_WIDTH]
    ubuf[POOL_HALO:, :] = u
    ue = ubuf[...]
    s2 = ue + pltpu.roll(ue, 1, axis=0)
    s4 = s2 + pltpu.roll(s2, 2, axis=0)
    s8 = s4 + pltpu.roll(s4, 4, axis=0)
    s16 = s8 + pltpu.roll(s8, 8, axis=0)
    ubuf[0:POOL_HALO, :] = u[tm - POOL_HALO:, :]
    lane_p = lax.broadcasted_iota(jnp.int32, (tm, POOL_WIDTH), 1)
    grp = lane_p // (POOL_WIDTH // len(POOL_WINDOWS))
    win_sum = jnp.where(grp == 0, s2[POOL_HALO:], jnp.where(grp == 1, s4[POOL_HALO:],
                        jnp.where(grp == 2, s8[POOL_HALO:], s16[POOL_HALO:])))
    win = jnp.where(grp == 0, 2, jnp.where(grp == 1, 4, jnp.where(grp == 2, 8, 16)))
    t_abs = i * tm + lax.broadcasted_iota(jnp.int32, (tm, POOL_WIDTH), 0)
    cnt = jnp.minimum(t_abs + 1, win).astype(F32)
    p = win_sum / cnt - u
    yp = _dot(p.astype(BF16), poolw_ref[...]) * pools_ref[...]
    yp_ref[0] = yp.astype(BF16)

    lane = lax.broadcasted_iota(jnp.int32, (tm, LANES), 1)
    in_rope = (lane >= MLA_QK_NOPE) & (lane < MLA_QK_DIM)
    ang = pos_ref[0].astype(F32) * freq_ref[...]
    cosf = jnp.where(in_rope, jnp.cos(ang), 1.0)
    sinf = jnp.where(in_rope, jnp.sin(ang), 0.0)

    q_lat = z[:, Z_QLAT_OFF:Z_QLAT_OFF + MLA_Q_LORA]
    kv_lat = z[:, Z_KVLAT_OFF:Z_KVLAT_OFF + MLA_KV_LORA]
    k_rope = z[:, Z_KROPE_OFF:Z_KROPE_OFF + LANES]
    qn = (_rms(q_lat) * qg_ref[...]).astype(BF16)
    kvn = (_rms(kv_lat) * kvg_ref[...]).astype(BF16)
    q = _dot(qn, wq_ref[...])
    kx = _dot(kvn, wk_ref[...])
    v = _dot(kvn, wv_ref[...])
    k_pe = _rope(k_rope, cosf, sinf, lane)
    qs, ks = [], []
    for hd in range(MLA_HEADS):
        sl = slice(hd * LANES, (hd + 1) * LANES)
        qs.append(_rope(q[:, sl], cosf, sinf, lane))
        ks.append(kx[:, sl] + k_pe)
    q = jnp.concatenate(qs, axis=1)
    k = jnp.concatenate(ks, axis=1)
    qss = _mm_exact_rhs(q * q, bd_ref[...]) * (1.0 / MLA_QK_DIM)
    kss = _mm_exact_rhs(k * k, bd_ref[...]) * (1.0 / MLA_QK_DIM)
    qkg = qkg_ref[...]
    q = q * lax.rsqrt(qss + NORM_EPS) * qkg[0:1] * (MLA_QK_DIM ** -0.5)
    k = k * lax.rsqrt(kss + NORM_EPS) * qkg[1:2]
    q_ref[0] = q.astype(BF16)
    k_ref[0] = k.astype(BF16)
    v_ref[0] = v.astype(BF16)


def _mixin(x, mod_l, gain, win, pos3, freq, poolw, pools, qg, kvg, wq, wk, wv, qkg, bd128):
    B, T, _ = x.shape
    tm = TM_MIX
    const = lambda shape: pl.BlockSpec(shape, lambda b, i: tuple(0 for _ in shape))
    tok = lambda w: pl.BlockSpec((1, tm, w), lambda b, i: (b, i, 0))
    return pl.pallas_call(
        _mixin_kernel,
        grid=(B, T // tm),
        in_specs=[tok(D_MODEL),
                  pl.BlockSpec((1, 6, D_MODEL), lambda b, i: (b, 0, 0)),
                  const((1, D_MODEL)), const((D_MODEL, Z_COLS)),
                  tok(1), const((1, LANES)),
                  const((POOL_WIDTH, POOL_WIDTH)), const((1, POOL_WIDTH)),
                  const((1, MLA_Q_LORA)), const((1, MLA_KV_LORA)),
                  const((MLA_Q_LORA, MLA_HEADS * LANES)), const((MLA_KV_LORA, MLA_HEADS * LANES)),
                  const((MLA_KV_LORA, MLA_HEADS * HEAD_DIM)), const((2, MLA_HEADS * LANES)),
                  const((MLA_HEADS * LANES, MLA_HEADS * LANES))],
        out_specs=[tok(ZR_COLS), tok(POOL_WIDTH), tok(MLA_HEADS * LANES), tok(MLA_HEADS * LANES),
                   tok(MLA_HEADS * HEAD_DIM)],
        out_shape=[jax.ShapeDtypeStruct((B, T, ZR_COLS), F32),
                   jax.ShapeDtypeStruct((B, T, POOL_WIDTH), BF16),
                   jax.ShapeDtypeStruct((B, T, MLA_HEADS * LANES), BF16),
                   jax.ShapeDtypeStruct((B, T, MLA_HEADS * LANES), BF16),
                   jax.ShapeDtypeStruct((B, T, MLA_HEADS * HEAD_DIM), BF16)],
        scratch_shapes=[pltpu.VMEM((POOL_HALO + tm, POOL_WIDTH), F32)],
        compiler_params=_cparams(("arbitrary", "arbitrary")),
    )(x, mod_l, gain, win, pos3, freq, poolw, pools, qg, kvg, wq, wk, wv, qkg, bd128)


WKV_PASSES_SCORE = 1
WKV_PASSES_INV = 1
WKV_PASSES_APPLY = 1
WKV_PASSES_STATE = 3
WKV_STEP_CHUNKS = 2


def _stack_heads(xp, lane):
    return jnp.concatenate([jnp.where(lane < HEAD_DIM, xp, 0.0),
                            jnp.where(lane >= HEAD_DIM, xp, 0.0)], axis=0)


def _wkv_prep(r, lw, k, v, kk, a, tri, masks):
    L = r[0].shape[0]
    n = 2 * L
    nc = len(r)
    each = lambda f, *ls: [f(*xs) for xs in zip(*ls)]
    lane = lax.broadcasted_iota(jnp.int32, (L, LANES), 1)
    stack = lambda x: _stack_heads(x, lane)
    cum = each(lambda x: _mm_exact_rhs_left(tri, x), lw)
    cum_last = each(lambda c: c[L - 1:L, :], cum)
    e_w = each(jnp.exp, cum)
    e_wm = each(lambda c, x: jnp.exp(c - x), cum, lw)
    e_iw = each(lambda c: jnp.exp(-c), cum)
    e_d = each(lambda cl, c: jnp.exp(cl - c), cum_last, cum)
    beta = each(lambda x, y: x * y, kk, a)
    r_t = each(lambda x, e: stack(x * e), r, e_w)
    a_t = each(lambda x, e: stack(-x * e), kk, e_wm)
    b_t = each(lambda x, e: stack(x * e), beta, e_iw)
    k_t = each(lambda x, e: stack(x * e), k, e_iw)
    b_d = each(lambda x, e: stack(x * e), beta, e_d)
    k_d = each(lambda x, e: stack(x * e), k, e_d)
    v_s = each(stack, v)
    g = each(lambda at, rt, bt, kt: _mm(jnp.concatenate([at, rt], axis=0),
                                        jnp.concatenate([bt, kt], axis=0), NT, WKV_PASSES_SCORE),
             a_t, r_t, b_t, k_t)
    strict, incl, levels = masks
    a_ab = each(lambda x: jnp.where(strict, x[:n, :n], 0.0), g)
    a_ak = each(lambda x: jnp.where(strict, x[:n, n:], 0.0), g)
    s_rb = each(lambda x: jnp.where(incl, x[n:, :n], 0.0), g)
    s_rk = each(lambda x: jnp.where(incl, x[n:, n:], 0.0), g)
    eye = jnp.where(levels[0][1], 1.0, 0.0)
    tinv = each(lambda x: eye + jnp.where(levels[0][0], x, 0.0), a_ab)
    for lvl_mask, _ in levels[1:]:
        et = each(lambda x, t: _mm(jnp.where(lvl_mask, x, 0.0), t, NN, WKV_PASSES_INV), a_ab, tinv)
        tinv = each(lambda t, x: t + _mm(t, x, NN, WKV_PASSES_INV), tinv, et)
    av = each(lambda x, y: _mm(x, y, NN, WKV_PASSES_APPLY), a_ak, v_s)
    tx = each(lambda t, x, y: _mm(t, jnp.concatenate([x, y], axis=1), NN, WKV_PASSES_APPLY),
              tinv, a_t, av)
    ra = each(lambda rt, s, x: rt + _mm(s, x[:, :LANES], NN, WKV_PASSES_APPLY), r_t, s_rb, tx)
    c2 = each(lambda sb, sk, x, vs: _mm(jnp.concatenate([sb, sk], axis=1),
                                        jnp.concatenate([x[:, LANES:], vs], axis=0),
                                        NN, WKV_PASSES_APPLY), s_rb, s_rk, tx, v_s)
    tb = each(lambda x, bd: _mm(x.T, bd, NN, WKV_PASSES_APPLY), tx, b_d)
    c3 = each(lambda x, vs, kd: x[LANES:] + _mm(vs.T, kd, NN, WKV_PASSES_APPLY), tb, v_s, k_d)
    return [(ra[i], c2[i], jnp.exp(cum_last[i]), tb[i][:LANES], c3[i]) for i in range(nc)]


def _mm_exact_rhs_left(tri_bf16, x):
    x0 = x.astype(BF16)
    r1 = x - x0.astype(F32)
    x1 = r1.astype(BF16)
    x2 = (r1 - x1.astype(F32)).astype(BF16)
    return _dot(tri_bf16, x0) + (_dot(tri_bf16, x1) + _dot(tri_bf16, x2))


def _wkv_masks(L):
    n = 2 * L
    row = lax.broadcasted_iota(jnp.int32, (n, n), 0)
    col = lax.broadcasted_iota(jnp.int32, (n, n), 1)
    strict = row > col
    incl = row >= col
    levels = []
    m = 1
    while m < L:
        same = (row // (2 * m)) == (col // (2 * m))
        lvl = same & ((row % (2 * m)) >= m) & ((col % (2 * m)) < m)
        levels.append((lvl, row == col))
        m *= 2
    return strict, incl, levels


def _rwkv_kernel(has_vres, *refs):
    if has_vres:
        (z_ref, vf_ref, mu_ref, vec_ref, w2a2_ref, g2v2_ref, bd_ref,
         y_ref, carry, state) = refs
    else:
        (z_ref, mu_ref, vec_ref, w2a2_ref, g2v2_ref, bd_ref,
         y_ref, vout_ref, carry, state) = refs
    c = pl.program_id(1)
    rows = z_ref.shape[1]
    L = WKV_CHUNK
    W = RWKV_WIDTH

    @pl.when(c == 0)
    def _():
        carry[...] = jnp.zeros(carry.shape, F32)
        state[...] = jnp.zeros(state.shape, F32)

    z = z_ref[0]
    row = lax.broadcasted_iota(jnp.int32, z.shape, 0)
    prev = jnp.where(row == 0, carry[SUBLANES - 1:SUBLANES, :], pltpu.roll(z, 1, axis=0))
    carry[...] = z[rows - SUBLANES:, :]
    zs = z + mu_ref[...] * (prev - z)
    r = zs[:, 0:W]
    k = zs[:, W:2 * W]
    v = zs[:, 2 * W:3 * W]
    wa = zs[:, 3 * W:3 * W + LANES]
    gb = zs[:, 3 * W + LANES:ZR_COLS]
    vec = vec_ref[...]
    w0, a0, k_k, k_a, r_k, ln_g, ln_b, v0 = (vec[j:j + 1] for j in range(8))

    lane_a = lax.broadcasted_iota(jnp.int32, wa.shape, 1)
    t1 = _dot(jnp.where(lane_a < RWKV_DECAY_LORA, jnp.tanh(wa), wa).astype(BF16), w2a2_ref[...])
    lane_g = lax.broadcasted_iota(jnp.int32, gb.shape, 1)
    t2 = _dot(jnp.where(lane_g < RWKV_GATE_LORA, _sigmoid(gb), gb).astype(BF16), g2v2_ref[...])
    xw = w0 + t1[:, :W]
    w_log = -(jnp.maximum(-xw, 0.0) + jnp.log1p(jnp.exp(-jnp.abs(xw)))) - 0.5
    lw = -jnp.exp(w_log)
    a = _sigmoid(a0 + t1[:, W:])
    g = t2[:, :W]
    if has_vres:
        v = v + (vf_ref[0] - v) * _sigmoid(v0 + t2[:, W:])
    else:
        vout_ref[0] = v
    kk = k * k_k
    bd = bd_ref[...]
    nrm = jnp.sqrt(_mm_exact_rhs(kk * kk, bd))
    kk = kk / jnp.maximum(nrm, 1e-12)
    k = k * (1.0 + (a - 1.0) * k_a)

    masks = _wkv_masks(L)
    rowt = lax.broadcasted_iota(jnp.int32, (L, L), 0)
    colt = lax.broadcasted_iota(jnp.int32, (L, L), 1)
    tri = jnp.where(rowt >= colt, 1.0, 0.0).astype(BF16)
    n_pairs = RWKV_HEADS // 2
    n_chunks = rows // L
    idx = [(ch, p) for ch in range(n_chunks) for p in range(n_pairs)]
    cut = lambda x: [x[ch * L:(ch + 1) * L, p * LANES:(p + 1) * LANES] for ch, p in idx]
    prep = dict(zip(idx, _wkv_prep(cut(r), cut(lw), cut(k), cut(v), cut(kk), cut(a), tri, masks)))
    S = [state[p] for p in range(n_pairs)]
    y_rows = []
    for ch in range(n_chunks):
        y_s = [_mm(prep[ch, p][0], S[p], NT, WKV_PASSES_STATE) + prep[ch, p][1] for p in range(n_pairs)]
        y_rows.append(jnp.concatenate([x[:L] + x[L:] for x in y_s], axis=1))
        S = [S[p] * prep[ch, p][2] + _mm(S[p], prep[ch, p][3], NN, WKV_PASSES_STATE) + prep[ch, p][4]
             for p in range(n_pairs)]
    for p in range(n_pairs):
        state[p] = S[p]
    y = jnp.concatenate(y_rows, axis=0)

    inv = 1.0 / HEAD_DIM
    mean = _mm_exact_rhs(y, bd) * inv
    yc = y - mean
    var = _mm_exact_rhs(yc * yc, bd) * inv
    yn = yc * lax.rsqrt(var + RWKV_LNX_EPS) * ln_g + ln_b
    bonus = _mm_exact_rhs(r * k * r_k, bd) * v
    y_ref[0] = ((yn + bonus) * g).astype(BF16)


def _rwkv(zr, v_first, mu, vec8, w2a2, g2v2, bd64):
    B, T, _ = zr.shape
    L = WKV_CHUNK * WKV_STEP_CHUNKS
    has_vres = v_first is not None
    const = lambda shape: pl.BlockSpec(shape, lambda b, c: tuple(0 for _ in shape))
    tok = lambda w: pl.BlockSpec((1, L, w), lambda b, c: (b, c, 0))
    in_specs = [tok(ZR_COLS)]
    args = [zr]
    if has_vres:
        in_specs.append(tok(RWKV_WIDTH))
        args.append(v_first)
    in_specs += [const((1, ZR_COLS)), const((8, RWKV_WIDTH)), const((LANES, 2 * RWKV_WIDTH)),
                 const((2 * LANES, 2 * RWKV_WIDTH)), const((RWKV_WIDTH, RWKV_WIDTH))]
    args += [mu, vec8, w2a2, g2v2, bd64]
    out_specs = [tok(RWKV_WIDTH)]
    out_shape = [jax.ShapeDtypeStruct((B, T, RWKV_WIDTH), BF16)]
    if not has_vres:
        out_specs.append(tok(RWKV_WIDTH))
        out_shape.append(jax.ShapeDtypeStruct((B, T, RWKV_WIDTH), F32))
    outs = pl.pallas_call(
        functools.partial(_rwkv_kernel, has_vres),
        grid=(B, T // L),
        in_specs=in_specs, out_specs=out_specs, out_shape=out_shape,
        scratch_shapes=[pltpu.VMEM((SUBLANES, ZR_COLS), F32),
                        pltpu.VMEM((RWKV_HEADS // 2, LANES, LANES), F32)],
        compiler_params=_cparams(("arbitrary", "arbitrary")),
    )(*args)
    return (outs[0], v_first) if has_vres else (outs[0], outs[1])


def _attn_step(q_ref, k_ref, v_ref, m_sc, l_sc, acc_sc, masked):
    tq = q_ref.shape[1]
    tk = k_ref.shape[1]
    lane = lax.broadcasted_iota(jnp.int32, (tq, LANES), 1)
    if masked:
        rowi = lax.broadcasted_iota(jnp.int32, (tq, tk), 0)
        coli = lax.broadcasted_iota(jnp.int32, (tq, tk), 1)
        keep = coli <= rowi
    for pr in range(MLA_HEADS // 2):
        vp = v_ref[0, :, pr * LANES:(pr + 1) * LANES]
        alphas, pvs = [], []
        for hh in range(2):
            hd = 2 * pr + hh
            q = q_ref[0, :, hd * LANES:(hd + 1) * LANES]
            k = k_ref[0, :, hd * LANES:(hd + 1) * LANES]
            s = _dot(q, k, NT)
            if masked:
                s = jnp.where(keep, s, NEG_INF)
            m_prev = m_sc[hd]
            m_new = jnp.maximum(m_prev, jnp.max(s, axis=-1, keepdims=True))
            alpha = jnp.exp(m_prev - m_new)
            p = jnp.exp(s - jnp.concatenate([m_new] * (tk // LANES), axis=1))
            l_sc[hd] = alpha * l_sc[hd] + jnp.sum(p, axis=-1, keepdims=True)
            m_sc[hd] = m_new
            alphas.append(alpha)
            pvs.append(_dot(p.astype(BF16), vp))
        first = lane < HEAD_DIM
        acc_sc[pr] = (acc_sc[pr] * jnp.where(first, alphas[0], alphas[1])
                      + jnp.where(first, pvs[0], pvs[1]))


def _attn_kernel(q_ref, k_ref, v_ref, o_ref, m_sc, l_sc, acc_sc):
    i = pl.program_id(1)
    j = pl.program_id(2)

    @pl.when(j == 0)
    def _():
        m_sc[...] = jnp.full(m_sc.shape, NEG_INF, F32)
        l_sc[...] = jnp.zeros(l_sc.shape, F32)
        acc_sc[...] = jnp.zeros(acc_sc.shape, F32)

    @pl.when(j < i)
    def _():
        _attn_step(q_ref, k_ref, v_ref, m_sc, l_sc, acc_sc, masked=False)

    @pl.when(j == i)
    def _():
        _attn_step(q_ref, k_ref, v_ref, m_sc, l_sc, acc_sc, masked=True)
        tq = q_ref.shape[1]
        lane = lax.broadcasted_iota(jnp.int32, (tq, LANES), 1)
        outs = []
        for pr in range(MLA_HEADS // 2):
            den = jnp.where(lane < HEAD_DIM, l_sc[2 * pr], l_sc[2 * pr + 1])
            outs.append(acc_sc[pr] / den)
        o_ref[0] = jnp.concatenate(outs, axis=1).astype(BF16)


def _attention(q, k, v):
    B, T, _ = q.shape
    nq = T // TQ
    return pl.pallas_call(
        _attn_kernel,
        grid=(B, nq, nq),
        in_specs=[pl.BlockSpec((1, TQ, MLA_HEADS * LANES), lambda b, i, j: (b, i, 0)),
                  pl.BlockSpec((1, TQ, MLA_HEADS * LANES), lambda b, i, j: (b, jnp.minimum(i, j), 0)),
                  pl.BlockSpec((1, TQ, MLA_HEADS * HEAD_DIM), lambda b, i, j: (b, jnp.minimum(i, j), 0))],
        out_specs=pl.BlockSpec((1, TQ, MLA_HEADS * HEAD_DIM), lambda b, i, j: (b, i, 0)),
        out_shape=jax.ShapeDtypeStruct((B, T, MLA_HEADS * HEAD_DIM), BF16),
        scratch_shapes=[pltpu.VMEM((MLA_HEADS, TQ, LANES), F32),
                        pltpu.VMEM((MLA_HEADS, TQ, LANES), F32),
                        pltpu.VMEM((MLA_HEADS // 2, TQ, LANES), F32)],
        compiler_params=_cparams(("arbitrary", "arbitrary", "arbitrary")),
    )(q, k, v)


def _mixout_kernel(has_router, *refs):
    if has_router:
        (x_ref, yr_ref, yp_ref, ym_ref, wo_ref, mod_ref, gain_ref, rt_ref,
         xo_ref, h_ref, route_ref) = refs
    else:
        x_ref, yr_ref, yp_ref, ym_ref, wo_ref, mod_ref, gain_ref, xo_ref, h_ref = refs
    mod = mod_ref[0]
    o1 = RWKV_WIDTH
    o2 = RWKV_WIDTH + POOL_WIDTH
    mix = (_dot(yr_ref[0], wo_ref[0:o1, :]) + _dot(yp_ref[0], wo_ref[o1:o2, :])
           + _dot(ym_ref[0], wo_ref[o2:, :]))
    x = x_ref[0] + mod[2:3] * mix
    xo_ref[0] = x
    h = _rms(x) * gain_ref[...] * (1.0 + mod[4:5]) + mod[3:4]
    h_ref[0] = h.astype(BF16)
    if has_router:
        logits = _mm(h, rt_ref[...])
        lane = lax.broadcasted_iota(jnp.int32, logits.shape, 1).astype(F32)
        lg = jnp.where(lane < N_EXPERTS, logits, -jnp.inf)
        m1 = jnp.max(lg, axis=-1, keepdims=True)
        i1 = jnp.min(jnp.where(lg == m1, lane, float(LANES)), axis=-1, keepdims=True)
        lg2 = jnp.where(lane == i1, -jnp.inf, lg)
        m2 = jnp.max(lg2, axis=-1, keepdims=True)
        i2 = jnp.min(jnp.where(lg2 == m2, lane, float(LANES)), axis=-1, keepdims=True)
        e2 = jnp.exp(m2 - m1)
        g1 = 1.0 / (1.0 + e2)
        g2 = e2 / (1.0 + e2)
        route_ref[0] = jnp.where(lane == 0, i1,
                                 jnp.where(lane == 1, i2,
                                           jnp.where(lane == 2, g1, jnp.where(lane == 3, g2, 0.0))))


def _mixout(x, yr, yp, ym, wo, mod_l, gain, router_p):
    B, T, _ = x.shape
    tm = TM_MIX
    has_router = router_p is not None
    const = lambda shape: pl.BlockSpec(shape, lambda b, i: tuple(0 for _ in shape))
    tok = lambda w: pl.BlockSpec((1, tm, w), lambda b, i: (b, i, 0))
    in_specs = [tok(D_MODEL), tok(RWKV_WIDTH), tok(POOL_WIDTH), tok(MLA_HEADS * HEAD_DIM),
                const((D_MODEL, D_MODEL)), pl.BlockSpec((1, 6, D_MODEL), lambda b, i: (b, 0, 0)),
                const((1, D_MODEL))]
    args = [x, yr, yp, ym, wo, mod_l, gain]
    out_specs = [tok(D_MODEL), tok(D_MODEL)]
    out_shape = [jax.ShapeDtypeStruct((B, T, D_MODEL), F32), jax.ShapeDtypeStruct((B, T, D_MODEL), BF16)]
    if has_router:
        in_specs.append(const((D_MODEL, LANES)))
        args.append(router_p)
        out_specs.append(tok(LANES))
        out_shape.append(jax.ShapeDtypeStruct((B, T, LANES), F32))
    return pl.pallas_call(
        functools.partial(_mixout_kernel, has_router),
        grid=(B, T // tm),
        in_specs=in_specs, out_specs=out_specs, out_shape=out_shape,
        compiler_params=_cparams(("arbitrary", "arbitrary")),
    )(*args)


def _ffn_kernel(h_ref, wg_ref, wu_ref, wo_ref, x_ref, mod_ref, o_ref, acc):
    j = pl.program_id(1)

    @pl.when(j == 0)
    def _():
        acc[...] = jnp.zeros(acc.shape, F32)

    h = h_ref[...]
    gg = _dot(h, wg_ref[...])
    uu = _dot(h, wu_ref[...])
    acc[...] += _dot((_silu(gg) * uu).astype(BF16), wo_ref[...])

    @pl.when(j == pl.num_programs(1) - 1)
    def _():
        o_ref[...] = x_ref[...] + mod_ref[0][5:6] * acc[...]


def _ffn(h2, w_in, w_out, x, mod_l):
    N = h2.shape[0]
    T = N // mod_l.shape[0]
    tm, tf = TM_FFN, TF_FFN
    nf = D_FF // tf
    per_b = T // tm
    return pl.pallas_call(
        _ffn_kernel,
        grid=(N // tm, nf),
        in_specs=[pl.BlockSpec((tm, D_MODEL), lambda i, j: (i, 0)),
                  pl.BlockSpec((D_MODEL, tf), lambda i, j: (0, j)),
                  pl.BlockSpec((D_MODEL, tf), lambda i, j: (0, j + nf)),
                  pl.BlockSpec((tf, D_MODEL), lambda i, j: (j, 0)),
                  pl.BlockSpec((tm, D_MODEL), lambda i, j: (i, 0)),
                  pl.BlockSpec((1, 6, D_MODEL), lambda i, j: (i // per_b, 0, 0))],
        out_specs=pl.BlockSpec((tm, D_MODEL), lambda i, j: (i, 0)),
        out_shape=jax.ShapeDtypeStruct((N, D_MODEL), F32),
        scratch_shapes=[pltpu.VMEM((tm, D_MODEL), F32)],
        compiler_params=_cparams(("arbitrary", "arbitrary")),
    )(h2, w_in, w_in, w_out, x, mod_l)


def _moe_kernel(be_ref, nb_ref, x_ref, wg_ref, wu_ref, wo_ref, gate_ref, o_ref, acc):
    i = pl.program_id(0)
    j = pl.program_id(1)

    @pl.when(j == 0)
    def _():
        acc[...] = jnp.zeros(acc.shape, F32)

    @pl.when(i < nb_ref[0])
    def _():
        x = x_ref[...]
        gg = _dot(x, wg_ref[0])
        uu = _dot(x, wu_ref[0])
        acc[...] += _dot((_silu(gg) * uu).astype(BF16), wo_ref[0])

    @pl.when(j == pl.num_programs(1) - 1)
    def _():
        o_ref[...] = acc[...] * gate_ref[...]


def _moe_experts(xs, w_in, w_out, row_gate, block_exp, n_used):
    n_rows = xs.shape[0]
    tm, tf = MOE_BLOCK, TF_MOE
    nf = D_FF_EXPERT // tf
    grid_spec = pltpu.PrefetchScalarGridSpec(
        num_scalar_prefetch=2,
        grid=(n_rows // tm, nf),
        in_specs=[pl.BlockSpec((tm, D_MODEL), lambda i, j, be, nb: (i, 0)),
                  pl.BlockSpec((1, D_MODEL, tf), lambda i, j, be, nb: (be[i], 0, j)),
                  pl.BlockSpec((1, D_MODEL, tf), lambda i, j, be, nb: (be[i], 0, j + nf)),
                  pl.BlockSpec((1, tf, D_MODEL), lambda i, j, be, nb: (be[i], j, 0)),
                  pl.BlockSpec((tm, 1), lambda i, j, be, nb: (i, 0))],
        out_specs=pl.BlockSpec((tm, D_MODEL), lambda i, j, be, nb: (i, 0)),
        scratch_shapes=[pltpu.VMEM((tm, D_MODEL), F32)])
    return pl.pallas_call(
        _moe_kernel,
        grid_spec=grid_spec,
        out_shape=jax.ShapeDtypeStruct((n_rows, D_MODEL), F32),
        compiler_params=_cparams(("arbitrary", "arbitrary")),
    )(block_exp, n_used, xs, w_in, w_in, w_out, row_gate)


def _combine_kernel(x_ref, ya_ref, yb_ref, mod_ref, o_ref):
    o_ref[...] = x_ref[...] + mod_ref[0][5:6] * (ya_ref[...] + yb_ref[...])


def _combine(x, ya, yb, mod_l):
    N = x.shape[0]
    T = N // mod_l.shape[0]
    tm = 1024
    per_b = T // tm
    tok = pl.BlockSpec((tm, D_MODEL), lambda i: (i, 0))
    return pl.pallas_call(
        _combine_kernel,
        grid=(N // tm,),
        in_specs=[tok, tok, tok, pl.BlockSpec((1, 6, D_MODEL), lambda i: (i // per_b, 0, 0))],
        out_specs=tok,
        out_shape=jax.ShapeDtypeStruct((N, D_MODEL), F32),
        compiler_params=_cparams(("arbitrary",)),
    )(x, ya, yb, mod_l)


def _moe(h2, route, w_in, w_out, x, mod_l):
    N = h2.shape[0]
    blk = MOE_BLOCK
    exp_flat = route[:, 0:TOP_K].astype(jnp.int32).reshape(-1)
    gate_flat = route[:, TOP_K:2 * TOP_K].reshape(-1)
    n_asg = N * TOP_K
    onehot = (exp_flat[:, None] == jnp.arange(N_EXPERTS)[None, :]).astype(jnp.int32)
    csum = jnp.cumsum(onehot, axis=0)
    rank = jnp.take_along_axis(csum, exp_flat[:, None], axis=1)[:, 0] - 1
    counts = csum[-1]
    padded = (counts + blk - 1) // blk * blk
    pend = jnp.cumsum(padded)
    pstart = pend - padded
    dest = pstart[exp_flat] + rank
    n_blocks = n_asg // blk + N_EXPERTS
    n_rows = n_blocks * blk
    tok_flat = (jnp.arange(n_asg) // TOP_K).astype(jnp.int32)
    row_tok = jnp.full((n_rows,), N, jnp.int32).at[dest].set(tok_flat)
    row_gate = jnp.zeros((n_rows,), F32).at[dest].set(gate_flat)
    block_exp = jnp.clip(jnp.searchsorted(pend, jnp.arange(n_blocks) * blk, side='right'),
                         0, N_EXPERTS - 1).astype(jnp.int32)
    n_used = (pend[-1:] // blk).astype(jnp.int32)
    h_pad = jnp.concatenate([h2, jnp.zeros((1, D_MODEL), h2.dtype)], axis=0)
    xs = jnp.take(h_pad, row_tok, axis=0)
    yb = _moe_experts(xs, w_in, w_out, row_gate[:, None], block_exp, n_used)
    d2 = dest.reshape(N, TOP_K)
    ya = jnp.take(yb, d2[:, 0], axis=0)
    yc = jnp.take(yb, d2[:, 1], axis=0)
    return _combine(x, ya, yc, mod_l)


def _layout_w_in(w, has_vres):
    W = RWKV_WIDTH
    off_gd = 3 * W + RWKV_DECAY_LORA + RWKV_ICLR_LORA
    off_pool = off_gd + RWKV_GATE_LORA
    off_q = off_pool + POOL_WIDTH
    off_kv = off_q + MLA_Q_LORA
    off_kr = off_kv + MLA_KV_LORA
    n_base = off_kr + MLA_QK_ROPE
    d = w.shape[0]
    zeros = lambda n: jnp.zeros((d, n), w.dtype)
    vd = w[:, n_base:n_base + RWKV_VRES_LORA] if has_vres else zeros(RWKV_VRES_LORA)
    cols = [w[:, :off_gd], w[:, off_gd:off_pool], vd, zeros(ZR_COLS - off_pool - RWKV_VRES_LORA),
            w[:, off_pool:off_q], w[:, off_q:off_kv], w[:, off_kv:off_kr],
            zeros(MLA_QK_NOPE), w[:, off_kr:n_base], zeros(LANES - MLA_QK_DIM)]
    return jnp.concatenate(cols, axis=1).astype(BF16)


def _pad_heads(w, per_head, keep_from, keep_n):
    K = w.shape[0]
    wh = w.reshape(K, MLA_HEADS, per_head)[:, :, keep_from:keep_from + keep_n]
    wh = jnp.pad(wh, ((0, 0), (0, 0), (0, LANES - keep_n)))
    return wh.reshape(K, MLA_HEADS * LANES)


def kernel(x, c, positions, w_ada, b_ada, norm_gain, w_in_first, w_in_rest, mu_shift, mu_shift_v,
           rwkv_vec, rwkv_v0, rwkv_w2, rwkv_a2, rwkv_g2, rwkv_v2, pool_w, pool_scale,
           mla_q_lat_gain, mla_kv_lat_gain, mla_wq_up, mla_wkv_up, mla_qk_gain, w_out, ffn_w_in,
           ffn_w_out, moe_router, moe_w_in, moe_w_out):
    B, T, D = x.shape
    depth = w_ada.shape[0]
    W = RWKV_WIDTH
    mod = _adaln(c, w_ada, b_ada).reshape(depth, B, 6, D)
    pos3 = positions.reshape(B, T, 1)
    inv_freq = ROPE_BASE ** (-jnp.arange(0, MLA_QK_ROPE, 2, dtype=F32) / MLA_QK_ROPE)
    freq = jnp.concatenate([jnp.zeros((MLA_QK_NOPE,), F32), inv_freq, inv_freq,
                            jnp.zeros((LANES - MLA_QK_DIM,), F32)]).reshape(1, LANES)
    hid = np.arange(W) // HEAD_DIM
    bd64 = jnp.asarray(hid[:, None] == hid[None, :], BF16)
    bid = np.arange(MLA_HEADS * LANES) // LANES
    bd128 = jnp.asarray(bid[:, None] == bid[None, :], BF16)

    v_first = None
    for l in range(depth):
        has_vres = l > 0
        mod_l = mod[l]
        win = _layout_w_in(w_in_first if l == 0 else w_in_rest[l - 1], has_vres)
        poolw = jax.scipy.linalg.block_diag(*[pool_w[l, g] for g in range(len(POOL_WINDOWS))]).astype(BF16)
        wq = _pad_heads(mla_wq_up[l], MLA_QK_DIM, 0, MLA_QK_DIM).astype(BF16)
        wk = _pad_heads(mla_wkv_up[l], MLA_QK_NOPE + HEAD_DIM, 0, MLA_QK_NOPE).astype(BF16)
        wv = mla_wkv_up[l].reshape(MLA_KV_LORA, MLA_HEADS, MLA_QK_NOPE + HEAD_DIM)[:, :, MLA_QK_NOPE:]
        wv = wv.reshape(MLA_KV_LORA, MLA_HEADS * HEAD_DIM).astype(BF16)
        qkg = jnp.tile(jnp.pad(mla_qk_gain[l], ((0, 0), (0, LANES - MLA_QK_DIM))), (1, MLA_HEADS))
        zr, y_pool, q, k, v = _mixin(
            x, mod_l, norm_gain[l, 0].reshape(1, D), win, pos3, freq, poolw,
            pool_scale[l].reshape(1, -1), mla_q_lat_gain[l].reshape(1, -1),
            mla_kv_lat_gain[l].reshape(1, -1), wq, wk, wv, qkg, bd128)

        pad_mu = ZR_COLS - mu_shift.shape[1] - RWKV_VRES_LORA
        mu_v = mu_shift_v[l - 1] if has_vres else jnp.zeros((RWKV_VRES_LORA,), F32)
        mu = jnp.concatenate([mu_shift[l], mu_v, jnp.zeros((pad_mu,), F32)]).reshape(1, ZR_COLS)
        v0 = rwkv_v0[l - 1] if has_vres else jnp.zeros((W,), F32)
        vec8 = jnp.concatenate([rwkv_vec[l], v0[None]], axis=0)
        w2a2 = jax.scipy.linalg.block_diag(rwkv_w2[l], rwkv_a2[l]).astype(BF16)
        g2 = jnp.pad(rwkv_g2[l], ((0, 2 * LANES - RWKV_GATE_LORA), (0, 0)))
        if has_vres:
            v2 = jnp.pad(rwkv_v2[l - 1], ((RWKV_GATE_LORA, 2 * LANES - RWKV_GATE_LORA - RWKV_VRES_LORA), (0, 0)))
        else:
            v2 = jnp.zeros((2 * LANES, W), F32)
        g2v2 = jnp.concatenate([g2, v2], axis=1).astype(BF16)
        y_rwkv, v_first = _rwkv(zr, v_first, mu, vec8, w2a2, g2v2, bd64)

        y_mla = _attention(q, k, v)

        is_moe = (l % 2 == 1)
        router_p = None
        if is_moe:
            router_p = jnp.pad(moe_router[l // 2], ((0, 0), (0, LANES - N_EXPERTS)))
        outs = _mixout(x, y_rwkv, y_pool, y_mla, w_out[l].astype(BF16), mod_l,
                       norm_gain[l, 1].reshape(1, D), router_p)
        x_mid, h2 = outs[0], outs[1]
        h2f = h2.reshape(B * T, D)
        xf = x_mid.reshape(B * T, D)
        if is_moe:
            xo = _moe(h2f, outs[2].reshape(B * T, LANES), moe_w_in[l // 2].astype(BF16),
                      moe_w_out[l // 2].astype(BF16), xf, mod_l)
        else:
            xo = _ffn(h2f, ffn_w_in[l // 2].astype(BF16), ffn_w_out[l // 2].astype(BF16), xf, mod_l)
        x = xo.reshape(B, T, D)
    return x
```

```python
import functools

import numpy as np
import jax
import jax.numpy as jnp
from jax import lax
from jax.experimental import pallas as pl
from jax.experimental.pallas import tpu as pltpu
from jax.experimental.pallas import tpu_sc as plsc

F32 = jnp.float32
BF16 = jnp.bfloat16

D_MODEL = 1024
HEAD_DIM = 64
RWKV_WIDTH = 512
RWKV_HEADS = 8
POOL_WIDTH = 256
POOL_WINDOWS = (2, 4, 8, 16)
POOL_HALO = 16
MLA_HEADS = 4
MLA_QK_NOPE = 64
MLA_QK_ROPE = 32
MLA_QK_DIM = 96
MLA_Q_LORA = 256
MLA_KV_LORA = 128
ROPE_BASE = 10000.0
RWKV_DECAY_LORA = 64
RWKV_ICLR_LORA = 64
RWKV_VRES_LORA = 32
RWKV_GATE_LORA = 160
RWKV_LNX_EPS = 64e-5
D_FF = 2816
N_EXPERTS = 8
TOP_K = 2
D_FF_EXPERT = 3584
NORM_EPS = 1e-6
NEG_INF = -1e30

LANES = 128
SUBLANES = 8
VMEM_LIMIT = 56 * 1024 * 1024

ZR_COLS = 1920
Z_POOL_OFF = ZR_COLS
Z_QLAT_OFF = Z_POOL_OFF + POOL_WIDTH
Z_KVLAT_OFF = Z_QLAT_OFF + MLA_Q_LORA
Z_KROPE_OFF = Z_KVLAT_OFF + MLA_KV_LORA
Z_COLS = Z_KROPE_OFF + LANES

TM_MIX = 512
WKV_CHUNK = 64
TQ = 512
TM_FFN = 512
TF_FFN = 1408
MOE_BLOCK = 1024
TF_MOE = 896
SC_WINDOW = 128
ROW_CHUNKS = 4
ROW_CHUNK_W = D_MODEL // ROW_CHUNKS
PACK_CHUNK_W = ROW_CHUNK_W // 2

NN = (((1,), (0,)), ((), ()))
NT = (((1,), (1,)), ((), ()))


def _dot(a, b, dims=NN):
    return lax.dot_general(a, b, dims, preferred_element_type=F32)


def _split2(a):
    hi = a.astype(BF16)
    lo = (a - hi.astype(F32)).astype(BF16)
    return hi, lo


def _mm(a, b, dims=NN, passes=3):
    if passes == 1:
        return _dot(a.astype(BF16), b.astype(BF16), dims)
    ah, al = _split2(a)
    bh, bl = _split2(b)
    return _dot(ah, bh, dims) + (_dot(ah, bl, dims) + _dot(al, bh, dims))


def _mm_exact_rhs(a, b_bf16, dims=NN):
    a0 = a.astype(BF16)
    r1 = a - a0.astype(F32)
    a1 = r1.astype(BF16)
    a2 = (r1 - a1.astype(F32)).astype(BF16)
    return _dot(a0, b_bf16, dims) + (_dot(a1, b_bf16, dims) + _dot(a2, b_bf16, dims))


def _pack_bf16_pairs(h):
    w = h.shape[1] // 2
    lo = lax.bitcast_convert_type(h[:, :w].astype(BF16).astype(F32), jnp.uint32)
    hi = lax.bitcast_convert_type(h[:, w:].astype(BF16).astype(F32), jnp.uint32)
    return (lo >> 16) | (hi & jnp.uint32(0xFFFF0000))


def _unpack_bf16_pairs(p):
    lo = lax.bitcast_convert_type(p << 16, F32)
    hi = lax.bitcast_convert_type(p & jnp.uint32(0xFFFF0000), F32)
    return jnp.concatenate([lo, hi], axis=1).astype(BF16)


def _sigmoid(x):
    return 1.0 / (1.0 + jnp.exp(-x))


def _silu(x):
    return x * _sigmoid(x)


def _rms(x, eps=NORM_EPS):
    return x * lax.rsqrt(jnp.mean(x * x, axis=-1, keepdims=True) + eps)


def _cparams(sem):
    return pltpu.CompilerParams(dimension_semantics=sem, vmem_limit_bytes=VMEM_LIMIT)


def _adaln_kernel(c_ref, w_ref, b_ref, o_ref):
    ca = _silu(c_ref[...])
    o_ref[0] = _mm(ca, w_ref[0]) + b_ref[0]


def _adaln(c, w_ada, b_ada):
    L = w_ada.shape[0]
    B = c.shape[0]
    n = w_ada.shape[2] // D_MODEL
    return pl.pallas_call(
        _adaln_kernel,
        grid=(L, n),
        in_specs=[pl.BlockSpec((B, D_MODEL), lambda l, j: (0, 0)),
                  pl.BlockSpec((1, D_MODEL, D_MODEL), lambda l, j: (l, 0, j)),
                  pl.BlockSpec((1, 1, D_MODEL), lambda l, j: (l, 0, j))],
        out_specs=pl.BlockSpec((1, B, D_MODEL), lambda l, j: (l, 0, j)),
        out_shape=jax.ShapeDtypeStruct((L, B, n * D_MODEL), F32),
        compiler_params=_cparams(("arbitrary", "arbitrary")),
    )(c, w_ada, b_ada.reshape(L, 1, -1))


def _rope(x, cosf, sinf, lane):
    up = pltpu.roll(x, LANES - MLA_QK_ROPE // 2, axis=1)
    dn = pltpu.roll(x, MLA_QK_ROPE // 2, axis=1)
    rot = jnp.where(lane < MLA_QK_NOPE + MLA_QK_ROPE // 2, -up, dn)
    return x * cosf + rot * sinf


def _mixin_kernel(x_ref, mod_ref, gain_ref, win_ref, pos_ref, freq_ref, poolw_ref, pools_ref,
                  qg_ref, kvg_ref, wq_ref, wk_ref, wv_ref, qkg_ref, bd_ref,
                  zr_ref, yp_ref, q_ref, k_ref, v_ref, ubuf):
    i = pl.program_id(1)
    tm = x_ref.shape[1]
    x = x_ref[0]
    mod = mod_ref[0]
    h = _rms(x) * gain_ref[...] * (1.0 + mod[1:2]) + mod[0:1]
    z = _dot(h.astype(BF16), win_ref[...])
    zr_ref[0] = z[:, :ZR_COLS]

    @pl.when(i == 0)
    def _():
        ubuf[0:POOL_HALO, :] = jnp.zeros((POOL_HALO, POOL_WIDTH), F32)

    u = z[:, Z_POOL_OFF:Z_POOL_OFF + POOL_WIDTH]
    ubuf[POOL_HALO:, :] = u
    ue = ubuf[...]
    s2 = ue + pltpu.roll(ue, 1, axis=0)
    s4 = s2 + pltpu.roll(s2, 2, axis=0)
    s8 = s4 + pltpu.roll(s4, 4, axis=0)
    s16 = s8 + pltpu.roll(s8, 8, axis=0)
    ubuf[0:POOL_HALO, :] = u[tm - POOL_HALO:, :]
    lane_p = lax.broadcasted_iota(jnp.int32, (tm, POOL_WIDTH), 1)
    grp = lane_p // (POOL_WIDTH // len(POOL_WINDOWS))
    win_sum = jnp.where(grp == 0, s2[POOL_HALO:], jnp.where(grp == 1, s4[POOL_HALO:],
                        jnp.where(grp == 2, s8[POOL_HALO:], s16[POOL_HALO:])))
    win = jnp.where(grp == 0, 2, jnp.where(grp == 1, 4, jnp.where(grp == 2, 8, 16)))
    t_abs = i * tm + lax.broadcasted_iota(jnp.int32, (tm, POOL_WIDTH), 0)
    cnt = jnp.minimum(t_abs + 1, win).astype(F32)
    p = win_sum / cnt - u
    yp = _dot(p.astype(BF16), poolw_ref[...]) * pools_ref[...]
    yp_ref[0] = yp.astype(BF16)

    lane = lax.broadcasted_iota(jnp.int32, (tm, LANES), 1)
    in_rope = (lane >= MLA_QK_NOPE) & (lane < MLA_QK_DIM)
    ang = pos_ref[0].astype(F32) * freq_ref[...]
    cosf = jnp.where(in_rope, jnp.cos(ang), 1.0)
    sinf = jnp.where(in_rope, jnp.sin(ang), 0.0)

    q_lat = z[:, Z_QLAT_OFF:Z_QLAT_OFF + MLA_Q_LORA]
    kv_lat = z[:, Z_KVLAT_OFF:Z_KVLAT_OFF + MLA_KV_LORA]
    k_rope = z[:, Z_KROPE_OFF:Z_KROPE_OFF + LANES]
    qn = (_rms(q_lat) * qg_ref[...]).astype(BF16)
    kvn = (_rms(kv_lat) * kvg_ref[...]).astype(BF16)
    q = _dot(qn, wq_ref[...])
    kx = _dot(kvn, wk_ref[...])
    v = _dot(kvn, wv_ref[...])
    k_pe = _rope(k_rope, cosf, sinf, lane)
    qs, ks = [], []
    for hd in range(MLA_HEADS):
        sl = slice(hd * LANES, (hd + 1) * LANES)
        qs.append(_rope(q[:, sl], cosf, sinf, lane))
        ks.append(kx[:, sl] + k_pe)
    q = jnp.concatenate(qs, axis=1)
    k = jnp.concatenate(ks, axis=1)
    qss = _mm_exact_rhs(q * q, bd_ref[...]) * (1.0 / MLA_QK_DIM)
    kss = _mm_exact_rhs(k * k, bd_ref[...]) * (1.0 / MLA_QK_DIM)
    qkg = qkg_ref[...]
    q = q * lax.rsqrt(qss + NORM_EPS) * qkg[0:1] * (MLA_QK_DIM ** -0.5)
    k = k * lax.rsqrt(kss + NORM_EPS) * qkg[1:2]
    q_ref[0] = q.astype(BF16)
    k_ref[0] = k.astype(BF16)
    v_ref[0] = v.astype(BF16)


def _mixin(x, mod_l, gain, win, pos3, freq, poolw, pools, qg, kvg, wq, wk, wv, qkg, bd128):
    B, T, _ = x.shape
    tm = TM_MIX
    const = lambda shape: pl.BlockSpec(shape, lambda b, i: tuple(0 for _ in shape))
    tok = lambda w: pl.BlockSpec((1, tm, w), lambda b, i: (b, i, 0))
    return pl.pallas_call(
        _mixin_kernel,
        grid=(B, T // tm),
        in_specs=[tok(D_MODEL),
                  pl.BlockSpec((1, 6, D_MODEL), lambda b, i: (b, 0, 0)),
                  const((1, D_MODEL)), const((D_MODEL, Z_COLS)),
                  tok(1), const((1, LANES)),
                  const((POOL_WIDTH, POOL_WIDTH)), const((1, POOL_WIDTH)),
                  const((1, MLA_Q_LORA)), const((1, MLA_KV_LORA)),
                  const((MLA_Q_LORA, MLA_HEADS * LANES)), const((MLA_KV_LORA, MLA_HEADS * LANES)),
                  const((MLA_KV_LORA, MLA_HEADS * HEAD_DIM)), const((2, MLA_HEADS * LANES)),
                  const((MLA_HEADS * LANES, MLA_HEADS * LANES))],
        out_specs=[tok(ZR_COLS), tok(POOL_WIDTH), tok(MLA_HEADS * LANES), tok(MLA_HEADS * LANES),
                   tok(MLA_HEADS * HEAD_DIM)],
        out_shape=[jax.ShapeDtypeStruct((B, T, ZR_COLS), F32),
                   jax.ShapeDtypeStruct((B, T, POOL_WIDTH), BF16),
                   jax.ShapeDtypeStruct((B, T, MLA_HEADS * LANES), BF16),
                   jax.ShapeDtypeStruct((B, T, MLA_HEADS * LANES), BF16),
                   jax.ShapeDtypeStruct((B, T, MLA_HEADS * HEAD_DIM), BF16)],
        scratch_shapes=[pltpu.VMEM((POOL_HALO + tm, POOL_WIDTH), F32)],
        compiler_params=_cparams(("arbitrary", "arbitrary")),
    )(x, mod_l, gain, win, pos3, freq, poolw, pools, qg, kvg, wq, wk, wv, qkg, bd128)


WKV_PASSES_SCORE = 1
WKV_PASSES_INV = 1
WKV_PASSES_APPLY = 1
WKV_PASSES_STATE = 3
WKV_STEP_CHUNKS = 2


def _stack_heads(xp, lane):
    return jnp.concatenate([jnp.where(lane < HEAD_DIM, xp, 0.0),
                            jnp.where(lane >= HEAD_DIM, xp, 0.0)], axis=0)


def _wkv_prep(r, lw, k, v, kk, a, tri, masks):
    L = r[0].shape[0]
    n = 2 * L
    nc = len(r)
    each = lambda f, *ls: [f(*xs) for xs in zip(*ls)]
    lane = lax.broadcasted_iota(jnp.int32, (L, LANES), 1)
    stack = lambda x: _stack_heads(x, lane)
    cum = each(lambda x: _mm_exact_rhs_left(tri, x), lw)
    cum_last = each(lambda c: c[L - 1:L, :], cum)
    e_w = each(jnp.exp, cum)
    e_wm = each(lambda c, x: jnp.exp(c - x), cum, lw)
    e_iw = each(lambda c: jnp.exp(-c), cum)
    e_d = each(lambda cl, c: jnp.exp(cl - c), cum_last, cum)
    beta = each(lambda x, y: x * y, kk, a)
    r_t = each(lambda x, e: stack(x * e), r, e_w)
    a_t = each(lambda x, e: stack(-x * e), kk, e_wm)
    b_t = each(lambda x, e: stack(x * e), beta, e_iw)
    k_t = each(lambda x, e: stack(x * e), k, e_iw)
    b_d = each(lambda x, e: stack(x * e), beta, e_d)
    k_d = each(lambda x, e: stack(x * e), k, e_d)
    v_s = each(stack, v)
    g = each(lambda at, rt, bt, kt: _mm(jnp.concatenate([at, rt], axis=0),
                                        jnp.concatenate([bt, kt], axis=0), NT, WKV_PASSES_SCORE),
             a_t, r_t, b_t, k_t)
    strict, incl, levels = masks
    a_ab = each(lambda x: jnp.where(strict, x[:n, :n], 0.0), g)
    a_ak = each(lambda x: jnp.where(strict, x[:n, n:], 0.0), g)
    s_rb = each(lambda x: jnp.where(incl, x[n:, :n], 0.0), g)
    s_rk = each(lambda x: jnp.where(incl, x[n:, n:], 0.0), g)
    eye = jnp.where(levels[0][1], 1.0, 0.0)
    tinv = each(lambda x: eye + jnp.where(levels[0][0], x, 0.0), a_ab)
    for lvl_mask, _ in levels[1:]:
        et = each(lambda x, t: _mm(jnp.where(lvl_mask, x, 0.0), t, NN, WKV_PASSES_INV), a_ab, tinv)
        tinv = each(lambda t, x: t + _mm(t, x, NN, WKV_PASSES_INV), tinv, et)
    av = each(lambda x, y: _mm(x, y, NN, WKV_PASSES_APPLY), a_ak, v_s)
    tx = each(lambda t, x, y: _mm(t, jnp.concatenate([x, y], axis=1), NN, WKV_PASSES_APPLY),
              tinv, a_t, av)
    ra = each(lambda rt, s, x: rt + _mm(s, x[:, :LANES], NN, WKV_PASSES_APPLY), r_t, s_rb, tx)
    c2 = each(lambda sb, sk, x, vs: _mm(jnp.concatenate([sb, sk], axis=1),
                                        jnp.concatenate([x[:, LANES:], vs], axis=0),
                                        NN, WKV_PASSES_APPLY), s_rb, s_rk, tx, v_s)
    tb = each(lambda x, bd: _mm(x.T, bd, NN, WKV_PASSES_APPLY), tx, b_d)
    c3 = each(lambda x, vs, kd: x[LANES:] + _mm(vs.T, kd, NN, WKV_PASSES_APPLY), tb, v_s, k_d)
    return [(ra[i], c2[i], jnp.exp(cum_last[i]), tb[i][:LANES], c3[i]) for i in range(nc)]


def _mm_exact_rhs_left(tri_bf16, x):
    x0 = x.astype(BF16)
    r1 = x - x0.astype(F32)
    x1 = r1.astype(BF16)
    x2 = (r1 - x1.astype(F32)).astype(BF16)
    return _dot(tri_bf16, x0) + (_dot(tri_bf16, x1) + _dot(tri_bf16, x2))


def _wkv_masks(L):
    n = 2 * L
    row = lax.broadcasted_iota(jnp.int32, (n, n), 0)
    col = lax.broadcasted_iota(jnp.int32, (n, n), 1)
    strict = row > col
    incl = row >= col
    levels = []
    m = 1
    while m < L:
        same = (row // (2 * m)) == (col // (2 * m))
        lvl = same & ((row % (2 * m)) >= m) & ((col % (2 * m)) < m)
        levels.append((lvl, row == col))
        m *= 2
    return strict, incl, levels


def _rwkv_kernel(has_vres, *refs):
    if has_vres:
        (z_ref, vf_ref, mu_ref, vec_ref, w2a2_ref, g2v2_ref, bd_ref,
         y_ref, carry, state) = refs
    else:
        (z_ref, mu_ref, vec_ref, w2a2_ref, g2v2_ref, bd_ref,
         y_ref, vout_ref, carry, state) = refs
    c = pl.program_id(1)
    rows = z_ref.shape[1]
    L = WKV_CHUNK
    W = RWKV_WIDTH

    @pl.when(c == 0)
    def _():
        carry[...] = jnp.zeros(carry.shape, F32)
        state[...] = jnp.zeros(state.shape, F32)

    z = z_ref[0]
    row = lax.broadcasted_iota(jnp.int32, z.shape, 0)
    prev = jnp.where(row == 0, carry[SUBLANES - 1:SUBLANES, :], pltpu.roll(z, 1, axis=0))
    carry[...] = z[rows - SUBLANES:, :]
    zs = z + mu_ref[...] * (prev - z)
    r = zs[:, 0:W]
    k = zs[:, W:2 * W]
    v = zs[:, 2 * W:3 * W]
    wa = zs[:, 3 * W:3 * W + LANES]
    gb = zs[:, 3 * W + LANES:ZR_COLS]
    vec = vec_ref[...]
    w0, a0, k_k, k_a, r_k, ln_g, ln_b, v0 = (vec[j:j + 1] for j in range(8))

    lane_a = lax.broadcasted_iota(jnp.int32, wa.shape, 1)
    t1 = _dot(jnp.where(lane_a < RWKV_DECAY_LORA, jnp.tanh(wa), wa).astype(BF16), w2a2_ref[...])
    lane_g = lax.broadcasted_iota(jnp.int32, gb.shape, 1)
    t2 = _dot(jnp.where(lane_g < RWKV_GATE_LORA, _sigmoid(gb), gb).astype(BF16), g2v2_ref[...])
    xw = w0 + t1[:, :W]
    w_log = -(jnp.maximum(-xw, 0.0) + jnp.log1p(jnp.exp(-jnp.abs(xw)))) - 0.5
    lw = -jnp.exp(w_log)
    a = _sigmoid(a0 + t1[:, W:])
    g = t2[:, :W]
    if has_vres:
        v = v + (vf_ref[0] - v) * _sigmoid(v0 + t2[:, W:])
    else:
        vout_ref[0] = v
    kk = k * k_k
    bd = bd_ref[...]
    nrm = jnp.sqrt(_mm_exact_rhs(kk * kk, bd))
    kk = kk / jnp.maximum(nrm, 1e-12)
    k = k * (1.0 + (a - 1.0) * k_a)

    masks = _wkv_masks(L)
    rowt = lax.broadcasted_iota(jnp.int32, (L, L), 0)
    colt = lax.broadcasted_iota(jnp.int32, (L, L), 1)
    tri = jnp.where(rowt >= colt, 1.0, 0.0).astype(BF16)
    n_pairs = RWKV_HEADS // 2
    n_chunks = rows // L
    idx = [(ch, p) for ch in range(n_chunks) for p in range(n_pairs)]
    cut = lambda x: [x[ch * L:(ch + 1) * L, p * LANES:(p + 1) * LANES] for ch, p in idx]
    prep = dict(zip(idx, _wkv_prep(cut(r), cut(lw), cut(k), cut(v), cut(kk), cut(a), tri, masks)))
    S = [state[p] for p in range(n_pairs)]
    y_rows = []
    for ch in range(n_chunks):
        y_s = [_mm(prep[ch, p][0], S[p], NT, WKV_PASSES_STATE) + prep[ch, p][1] for p in range(n_pairs)]
        y_rows.append(jnp.concatenate([x[:L] + x[L:] for x in y_s], axis=1))
        S = [S[p] * prep[ch, p][2] + _mm(S[p], prep[ch, p][3], NN, WKV_PASSES_STATE) + prep[ch, p][4]
             for p in range(n_pairs)]
    for p in range(n_pairs):
        state[p] = S[p]
    y = jnp.concatenate(y_rows, axis=0)

    inv = 1.0 / HEAD_DIM
    mean = _mm_exact_rhs(y, bd) * inv
    yc = y - mean
    var = _mm_exact_rhs(yc * yc, bd) * inv
    yn = yc * lax.rsqrt(var + RWKV_LNX_EPS) * ln_g + ln_b
    bonus = _mm_exact_rhs(r * k * r_k, bd) * v
    y_ref[0] = ((yn + bonus) * g).astype(BF16)


def _rwkv(zr, v_first, mu, vec8, w2a2, g2v2, bd64):
    B, T, _ = zr.shape
    L = WKV_CHUNK * WKV_STEP_CHUNKS
    has_vres = v_first is not None
    const = lambda shape: pl.BlockSpec(shape, lambda b, c: tuple(0 for _ in shape))
    tok = lambda w: pl.BlockSpec((1, L, w), lambda b, c: (b, c, 0))
    in_specs = [tok(ZR_COLS)]
    args = [zr]
    if has_vres:
        in_specs.append(tok(RWKV_WIDTH))
        args.append(v_first)
    in_specs += [const((1, ZR_COLS)), const((8, RWKV_WIDTH)), const((LANES, 2 * RWKV_WIDTH)),
                 const((2 * LANES, 2 * RWKV_WIDTH)), const((RWKV_WIDTH, RWKV_WIDTH))]
    args += [mu, vec8, w2a2, g2v2, bd64]
    out_specs = [tok(RWKV_WIDTH)]
    out_shape = [jax.ShapeDtypeStruct((B, T, RWKV_WIDTH), BF16)]
    if not has_vres:
        out_specs.append(tok(RWKV_WIDTH))
        out_shape.append(jax.ShapeDtypeStruct((B, T, RWKV_WIDTH), F32))
    outs = pl.pallas_call(
        functools.partial(_rwkv_kernel, has_vres),
        grid=(B, T // L),
        in_specs=in_specs, out_specs=out_specs, out_shape=out_shape,
        scratch_shapes=[pltpu.VMEM((SUBLANES, ZR_COLS), F32),
                        pltpu.VMEM((RWKV_HEADS // 2, LANES, LANES), F32)],
        compiler_params=_cparams(("arbitrary", "arbitrary")),
    )(*args)
    return (outs[0], v_first) if has_vres else (outs[0], outs[1])


def _attn_step(q_ref, k_ref, v_ref, m_sc, l_sc, acc_sc, masked):
    tq = q_ref.shape[1]
    tk = k_ref.shape[1]
    lane = lax.broadcasted_iota(jnp.int32, (tq, LANES), 1)
    if masked:
        rowi = lax.broadcasted_iota(jnp.int32, (tq, tk), 0)
        coli = lax.broadcasted_iota(jnp.int32, (tq, tk), 1)
        keep = coli <= rowi
    for pr in range(MLA_HEADS // 2):
        vp = v_ref[0, :, pr * LANES:(pr + 1) * LANES]
        alphas, pvs = [], []
        for hh in range(2):
            hd = 2 * pr + hh
            q = q_ref[0, :, hd * LANES:(hd + 1) * LANES]
            k = k_ref[0, :, hd * LANES:(hd + 1) * LANES]
            s = _dot(q, k, NT)
            if masked:
                s = jnp.where(keep, s, NEG_INF)
            m_prev = m_sc[hd]
            m_new = jnp.maximum(m_prev, jnp.max(s, axis=-1, keepdims=True))
            alpha = jnp.exp(m_prev - m_new)
            p = jnp.exp(s - jnp.concatenate([m_new] * (tk // LANES), axis=1))
            l_sc[hd] = alpha * l_sc[hd] + jnp.sum(p, axis=-1, keepdims=True)
            m_sc[hd] = m_new
            alphas.append(alpha)
            pvs.append(_dot(p.astype(BF16), vp))
        first = lane < HEAD_DIM
        acc_sc[pr] = (acc_sc[pr] * jnp.where(first, alphas[0], alphas[1])
                      + jnp.where(first, pvs[0], pvs[1]))


def _attn_kernel(q_ref, k_ref, v_ref, o_ref, m_sc, l_sc, acc_sc):
    i = pl.program_id(1)
    j = pl.program_id(2)

    @pl.when(j == 0)
    def _():
        m_sc[...] = jnp.full(m_sc.shape, NEG_INF, F32)
        l_sc[...] = jnp.zeros(l_sc.shape, F32)
        acc_sc[...] = jnp.zeros(acc_sc.shape, F32)

    @pl.when(j < i)
    def _():
        _attn_step(q_ref, k_ref, v_ref, m_sc, l_sc, acc_sc, masked=False)

    @pl.when(j == i)
    def _():
        _attn_step(q_ref, k_ref, v_ref, m_sc, l_sc, acc_sc, masked=True)
        tq = q_ref.shape[1]
        lane = lax.broadcasted_iota(jnp.int32, (tq, LANES), 1)
        outs = []
        for pr in range(MLA_HEADS // 2):
            den = jnp.where(lane < HEAD_DIM, l_sc[2 * pr], l_sc[2 * pr + 1])
            outs.append(acc_sc[pr] / den)
        o_ref[0] = jnp.concatenate(outs, axis=1).astype(BF16)


def _attention(q, k, v):
    B, T, _ = q.shape
    nq = T // TQ
    return pl.pallas_call(
        _attn_kernel,
        grid=(B, nq, nq),
        in_specs=[pl.BlockSpec((1, TQ, MLA_HEADS * LANES), lambda b, i, j: (b, i, 0)),
                  pl.BlockSpec((1, TQ, MLA_HEADS * LANES), lambda b, i, j: (b, jnp.minimum(i, j), 0)),
                  pl.BlockSpec((1, TQ, MLA_HEADS * HEAD_DIM), lambda b, i, j: (b, jnp.minimum(i, j), 0))],
        out_specs=pl.BlockSpec((1, TQ, MLA_HEADS * HEAD_DIM), lambda b, i, j: (b, i, 0)),
        out_shape=jax.ShapeDtypeStruct((B, T, MLA_HEADS * HEAD_DIM), BF16),
        scratch_shapes=[pltpu.VMEM((MLA_HEADS, TQ, LANES), F32),
                        pltpu.VMEM((MLA_HEADS, TQ, LANES), F32),
                        pltpu.VMEM((MLA_HEADS // 2, TQ, LANES), F32)],
        compiler_params=_cparams(("arbitrary", "arbitrary", "arbitrary")),
    )(q, k, v)


def _mixout_kernel(has_router, *refs):
    if has_router:
        (x_ref, yr_ref, yp_ref, ym_ref, wo_ref, mod_ref, gain_ref, rt_ref, tri_ref,
         xo_ref, h_ref, route_ref, cnt_ref, cnt_sc) = refs
    else:
        x_ref, yr_ref, yp_ref, ym_ref, wo_ref, mod_ref, gain_ref, xo_ref, h_ref = refs
    mod = mod_ref[0]
    o1 = RWKV_WIDTH
    o2 = RWKV_WIDTH + POOL_WIDTH
    mix = (_dot(yr_ref[0], wo_ref[0:o1, :]) + _dot(yp_ref[0], wo_ref[o1:o2, :])
           + _dot(ym_ref[0], wo_ref[o2:, :]))
    x = x_ref[0] + mod[2:3] * mix
    xo_ref[0] = x
    h = _rms(x) * gain_ref[...] * (1.0 + mod[4:5]) + mod[3:4]
    if not has_router:
        h_ref[0] = h.astype(BF16)
    else:
        hp = _pack_bf16_pairs(h)
        for ck in range(ROW_CHUNKS):
            h_ref[ck, 0] = hp[:, ck * PACK_CHUNK_W:(ck + 1) * PACK_CHUNK_W]
        logits = _mm(h, rt_ref[...])
        lane = lax.broadcasted_iota(jnp.int32, logits.shape, 1).astype(F32)
        lg = jnp.where(lane < N_EXPERTS, logits, -jnp.inf)
        m1 = jnp.max(lg, axis=-1, keepdims=True)
        i1 = jnp.min(jnp.where(lg == m1, lane, float(LANES)), axis=-1, keepdims=True)
        lg2 = jnp.where(lane == i1, -jnp.inf, lg)
        m2 = jnp.max(lg2, axis=-1, keepdims=True)
        i2 = jnp.min(jnp.where(lg2 == m2, lane, float(LANES)), axis=-1, keepdims=True)
        e2 = jnp.exp(m2 - m1)
        g1 = 1.0 / (1.0 + e2)
        g2 = e2 / (1.0 + e2)
        first = (pl.program_id(0) == 0) & (pl.program_id(1) == 0)

        @pl.when(first)
        def _():
            cnt_sc[...] = jnp.zeros(cnt_sc.shape, F32)

        hit1 = lane == i1
        hit2 = lane == i2
        onehot = jnp.where(hit1 | hit2, 1.0, 0.0)
        prefix = _dot(tri_ref[...], onehot.astype(BF16)) + cnt_sc[0:1, :]
        r1 = jnp.sum(jnp.where(hit1, prefix, 0.0), axis=-1, keepdims=True)
        r2 = jnp.sum(jnp.where(hit2, prefix, 0.0), axis=-1, keepdims=True)
        cnt_sc[...] = cnt_sc[...] + jnp.sum(onehot, axis=0, keepdims=True)
        cnt_ref[...] = cnt_sc[...]
        vals = (i1, i2, g1, g2, r1, r2)
        route = jnp.zeros(logits.shape, F32)
        for pos, val in enumerate(vals):
            route = jnp.where(lane == pos, val, route)
        route_ref[0] = route


def _mixout(x, yr, yp, ym, wo, mod_l, gain, router_p):
    B, T, _ = x.shape
    tm = TM_MIX
    has_router = router_p is not None
    const = lambda shape: pl.BlockSpec(shape, lambda b, i: tuple(0 for _ in shape))
    tok = lambda w: pl.BlockSpec((1, tm, w), lambda b, i: (b, i, 0))
    in_specs = [tok(D_MODEL), tok(RWKV_WIDTH), tok(POOL_WIDTH), tok(MLA_HEADS * HEAD_DIM),
                const((D_MODEL, D_MODEL)), pl.BlockSpec((1, 6, D_MODEL), lambda b, i: (b, 0, 0)),
                const((1, D_MODEL))]
    args = [x, yr, yp, ym, wo, mod_l, gain]
    out_specs = [tok(D_MODEL), tok(D_MODEL)]
    out_shape = [jax.ShapeDtypeStruct((B, T, D_MODEL), F32), jax.ShapeDtypeStruct((B, T, D_MODEL), BF16)]
    scratch = []
    if has_router:
        out_specs[1] = pl.BlockSpec((ROW_CHUNKS, 1, tm, PACK_CHUNK_W), lambda b, i: (0, b, i, 0))
        out_shape[1] = jax.ShapeDtypeStruct((ROW_CHUNKS, B, T, PACK_CHUNK_W), jnp.uint32)
        ids = np.arange(tm)
        tri = jnp.asarray(ids[:, None] > ids[None, :], BF16)
        in_specs += [const((D_MODEL, LANES)), const((tm, tm))]
        args += [router_p, tri]
        out_specs += [tok(LANES), const((SUBLANES, LANES))]
        out_shape += [jax.ShapeDtypeStruct((B, T, LANES), F32),
                      jax.ShapeDtypeStruct((SUBLANES, LANES), F32)]
        scratch = [pltpu.VMEM((SUBLANES, LANES), F32)]
    return pl.pallas_call(
        functools.partial(_mixout_kernel, has_router),
        grid=(B, T // tm),
        in_specs=in_specs, out_specs=out_specs, out_shape=out_shape, scratch_shapes=scratch,
        compiler_params=_cparams(("arbitrary", "arbitrary")),
    )(*args)


def _ffn_kernel(h_ref, wg_ref, wu_ref, wo_ref, x_ref, mod_ref, o_ref, acc):
    j = pl.program_id(1)

    @pl.when(j == 0)
    def _():
        acc[...] = jnp.zeros(acc.shape, F32)

    h = h_ref[...]
    gg = _dot(h, wg_ref[...])
    uu = _dot(h, wu_ref[...])
    acc[...] += _dot((_silu(gg) * uu).astype(BF16), wo_ref[...])

    @pl.when(j == pl.num_programs(1) - 1)
    def _():
        o_ref[...] = x_ref[...] + mod_ref[0][5:6] * acc[...]


def _ffn(h2, w_in, w_out, x, mod_l):
    N = h2.shape[0]
    T = N // mod_l.shape[0]
    tm, tf = TM_FFN, TF_FFN
    nf = D_FF // tf
    per_b = T // tm
    return pl.pallas_call(
        _ffn_kernel,
        grid=(N // tm, nf),
        in_specs=[pl.BlockSpec((tm, D_MODEL), lambda i, j: (i, 0)),
                  pl.BlockSpec((D_MODEL, tf), lambda i, j: (0, j)),
                  pl.BlockSpec((D_MODEL, tf), lambda i, j: (0, j + nf)),
                  pl.BlockSpec((tf, D_MODEL), lambda i, j: (j, 0)),
                  pl.BlockSpec((tm, D_MODEL), lambda i, j: (i, 0)),
                  pl.BlockSpec((1, 6, D_MODEL), lambda i, j: (i // per_b, 0, 0))],
        out_specs=pl.BlockSpec((tm, D_MODEL), lambda i, j: (i, 0)),
        out_shape=jax.ShapeDtypeStruct((N, D_MODEL), F32),
        scratch_shapes=[pltpu.VMEM((tm, D_MODEL), F32)],
        compiler_params=_cparams(("arbitrary", "arbitrary")),
    )(h2, w_in, w_in, w_out, x, mod_l)


def _moe_kernel(be_ref, nv_ref, x_ref, wg_ref, wu_ref, wo_ref, o_ref, acc, xm):
    i = pl.program_id(0)
    j = pl.program_id(1)

    @pl.when(j == 0)
    def _():
        acc[...] = jnp.zeros(acc.shape, F32)
        row = lax.broadcasted_iota(jnp.int32, (xm.shape[0], 1), 0)
        xp = jnp.concatenate([x_ref[ck] for ck in range(ROW_CHUNKS)], axis=1)
        xp = jnp.where(row < nv_ref[i], xp, jnp.uint32(0))
        xm[...] = _unpack_bf16_pairs(xp)

    @pl.when(nv_ref[i] > 0)
    def _():
        x = xm[...]
        gg = _dot(x, wg_ref[0])
        uu = _dot(x, wu_ref[0])
        acc[...] += _dot((_silu(gg) * uu).astype(BF16), wo_ref[0])

    @pl.when(j == pl.num_programs(1) - 1)
    def _():
        for ck in range(ROW_CHUNKS):
            o_ref[ck] = acc[:, ck * ROW_CHUNK_W:(ck + 1) * ROW_CHUNK_W]


def _moe_experts(xs, w_in, w_out, block_exp, n_valid):
    n_rows = xs.shape[1]
    tm, tf = MOE_BLOCK, TF_MOE
    nf = D_FF_EXPERT // tf
    rows_spec = pl.BlockSpec((ROW_CHUNKS, tm, ROW_CHUNK_W), lambda i, j, be, nv: (0, i, 0))
    grid_spec = pltpu.PrefetchScalarGridSpec(
        num_scalar_prefetch=2,
        grid=(n_rows // tm, nf),
        in_specs=[pl.BlockSpec((ROW_CHUNKS, tm, PACK_CHUNK_W), lambda i, j, be, nv: (0, i, 0)),
                  pl.BlockSpec((1, D_MODEL, tf), lambda i, j, be, nv: (be[i], 0, j)),
                  pl.BlockSpec((1, D_MODEL, tf), lambda i, j, be, nv: (be[i], 0, j + nf)),
                  pl.BlockSpec((1, tf, D_MODEL), lambda i, j, be, nv: (be[i], j, 0))],
        out_specs=rows_spec,
        scratch_shapes=[pltpu.VMEM((tm, D_MODEL), F32), pltpu.VMEM((tm, D_MODEL), BF16)])
    return pl.pallas_call(
        _moe_kernel,
        grid_spec=grid_spec,
        out_shape=jax.ShapeDtypeStruct((ROW_CHUNKS, n_rows, ROW_CHUNK_W), F32),
        compiler_params=_cparams(("arbitrary", "arbitrary")),
    )(block_exp, n_valid, xs, w_in, w_in, w_out)


def _sc_mesh():
    return plsc.VectorSubcoreMesh(core_axis_name="c", subcore_axis_name="s")


def _sc_scatter_rows(x, dest, n_rows):
    N, D = x.shape
    K = dest.shape[0]
    win = SC_WINDOW

    @pl.kernel(out_type=jax.ShapeDtypeStruct((n_rows, D), x.dtype), mesh=_sc_mesh(), scratch_types=[])
    def scatter(x_hbm, d_hbm, o_hbm):
        def body(x_vmem, *idx_vmem):
            for iv in idx_vmem:
                pltpu.sync_copy(x_vmem, o_hbm.at[iv.at[0]])

        pltpu.emit_pipeline(
            body,
            grid=(N // win,),
            in_specs=[pl.BlockSpec((win, D), lambda i: (i, 0))]
            + [pl.BlockSpec((1, win), functools.partial(lambda k, i: (k, i), k)) for k in range(K)],
            out_specs=[],
            core_axis_name=("c", "s"),
            dimension_semantics=(pltpu.PARALLEL,),
        )(x_hbm, *([d_hbm] * K))

    return scatter(x, dest)


def _sc_gather_rows(x, idx):
    n = idx.shape[0]
    D = x.shape[1]
    win = SC_WINDOW

    @pl.kernel(out_type=jax.ShapeDtypeStruct((n, D), x.dtype), mesh=_sc_mesh(), scratch_types=[])
    def gather(x_hbm, i_hbm, o_hbm):
        def body(i_vmem, o_vmem):
            pltpu.sync_copy(x_hbm.at[i_vmem.at[0]], o_vmem)

        pltpu.emit_pipeline(
            body,
            grid=(n // win,),
            in_specs=[pl.BlockSpec((1, win), lambda i: (0, i))],
            out_specs=[pl.BlockSpec((win, D), lambda i: (i, 0))],
            core_axis_name=("c", "s"),
            dimension_semantics=(pltpu.PARALLEL,),
        )(i_hbm, o_hbm)

    return gather(x, idx.reshape(1, n))


def _combine_kernel(x_ref, ya_ref, yb_ref, route_ref, mod_ref, o_ref):
    rt = route_ref[...]
    ya = jnp.concatenate([ya_ref[0, ck] for ck in range(ROW_CHUNKS)], axis=1)
    yb = jnp.concatenate([yb_ref[0, ck] for ck in range(ROW_CHUNKS)], axis=1)
    f = rt[:, 2:3] * ya + rt[:, 3:4] * yb
    o_ref[...] = x_ref[...] + mod_ref[0][5:6] * f


def _combine(x, y2, route, mod_l):
    N = x.shape[0]
    T = N // mod_l.shape[0]
    tm = 1024
    per_b = T // tm
    tok = pl.BlockSpec((tm, D_MODEL), lambda i: (i, 0))
    slot = lambda k: pl.BlockSpec((1, ROW_CHUNKS, tm, ROW_CHUNK_W), lambda i: (k, 0, i, 0))
    return pl.pallas_call(
        _combine_kernel,
        grid=(N // tm,),
        in_specs=[tok, slot(0), slot(1),
                  pl.BlockSpec((tm, LANES), lambda i: (i, 0)),
                  pl.BlockSpec((1, 6, D_MODEL), lambda i: (i // per_b, 0, 0))],
        out_specs=tok,
        out_shape=jax.ShapeDtypeStruct((N, D_MODEL), F32),
        compiler_params=_cparams(("arbitrary",)),
    )(x, y2, y2, route, mod_l)


def _moe(h2, route, counts, w_in, w_out, x, mod_l):
    N = x.shape[0]
    blk = MOE_BLOCK
    cnt = counts[0, :N_EXPERTS].astype(jnp.int32)
    padded = (cnt + blk - 1) // blk * blk
    pend = jnp.cumsum(padded)
    pstart = pend - padded
    e = route[:, 0:TOP_K].astype(jnp.int32)
    rank = route[:, 2 * TOP_K:3 * TOP_K].astype(jnp.int32)
    dest = (jnp.take(pstart, e) + rank).T
    n_blocks = N * TOP_K // blk + N_EXPERTS
    bstart = jnp.arange(n_blocks, dtype=jnp.int32) * blk
    block_exp = jnp.clip(jnp.searchsorted(pend, bstart, side='right'), 0, N_EXPERTS - 1).astype(jnp.int32)
    n_valid = jnp.clip(cnt[block_exp] - (bstart - pstart[block_exp]), 0, blk).astype(jnp.int32)
    n_rows = n_blocks * blk
    dest_ck = dest[:, None, :] + (jnp.arange(ROW_CHUNKS, dtype=jnp.int32) * n_rows)[None, :, None]
    xs = _sc_scatter_rows(h2, dest_ck.reshape(TOP_K, ROW_CHUNKS * N), ROW_CHUNKS * n_rows)
    yb = _moe_experts(xs.reshape(ROW_CHUNKS, n_rows, PACK_CHUNK_W), w_in, w_out, block_exp, n_valid)
    y2 = _sc_gather_rows(yb.reshape(ROW_CHUNKS * n_rows, ROW_CHUNK_W), dest_ck.reshape(-1))
    return _combine(x, y2.reshape(TOP_K, ROW_CHUNKS, N, ROW_CHUNK_W), route, mod_l)


def _layout_w_in(w, has_vres):
    W = RWKV_WIDTH
    off_gd = 3 * W + RWKV_DECAY_LORA + RWKV_ICLR_LORA
    off_pool = off_gd + RWKV_GATE_LORA
    off_q = off_pool + POOL_WIDTH
    off_kv = off_q + MLA_Q_LORA
    off_kr = off_kv + MLA_KV_LORA
    n_base = off_kr + MLA_QK_ROPE
    d = w.shape[0]
    zeros = lambda n: jnp.zeros((d, n), w.dtype)
    vd = w[:, n_base:n_base + RWKV_VRES_LORA] if has_vres else zeros(RWKV_VRES_LORA)
    cols = [w[:, :off_gd], w[:, off_gd:off_pool], vd, zeros(ZR_COLS - off_pool - RWKV_VRES_LORA),
            w[:, off_pool:off_q], w[:, off_q:off_kv], w[:, off_kv:off_kr],
            zeros(MLA_QK_NOPE), w[:, off_kr:n_base], zeros(LANES - MLA_QK_DIM)]
    return jnp.concatenate(cols, axis=1).astype(BF16)


def _pad_heads(w, per_head, keep_from, keep_n):
    K = w.shape[0]
    wh = w.reshape(K, MLA_HEADS, per_head)[:, :, keep_from:keep_from + keep_n]
    wh = jnp.pad(wh, ((0, 0), (0, 0), (0, LANES - keep_n)))
    return wh.reshape(K, MLA_HEADS * LANES)


def kernel(x, c, positions, w_ada, b_ada, norm_gain, w_in_first, w_in_rest, mu_shift, mu_shift_v,
           rwkv_vec, rwkv_v0, rwkv_w2, rwkv_a2, rwkv_g2, rwkv_v2, pool_w, pool_scale,
           mla_q_lat_gain, mla_kv_lat_gain, mla_wq_up, mla_wkv_up, mla_qk_gain, w_out, ffn_w_in,
           ffn_w_out, moe_router, moe_w_in, moe_w_out):
    B, T, D = x.shape
    depth = w_ada.shape[0]
    W = RWKV_WIDTH
    mod = _adaln(c, w_ada, b_ada).reshape(depth, B, 6, D)
    pos3 = positions.reshape(B, T, 1)
    inv_freq = ROPE_BASE ** (-jnp.arange(0, MLA_QK_ROPE, 2, dtype=F32) / MLA_QK_ROPE)
    freq = jnp.concatenate([jnp.zeros((MLA_QK_NOPE,), F32), inv_freq, inv_freq,
                            jnp.zeros((LANES - MLA_QK_DIM,), F32)]).reshape(1, LANES)
    hid = np.arange(W) // HEAD_DIM
    bd64 = jnp.asarray(hid[:, None] == hid[None, :], BF16)
    bid = np.arange(MLA_HEADS * LANES) // LANES
    bd128 = jnp.asarray(bid[:, None] == bid[None, :], BF16)

    v_first = None
    for l in range(depth):
        has_vres = l > 0
        mod_l = mod[l]
        win = _layout_w_in(w_in_first if l == 0 else w_in_rest[l - 1], has_vres)
        poolw = jax.scipy.linalg.block_diag(*[pool_w[l, g] for g in range(len(POOL_WINDOWS))]).astype(BF16)
        wq = _pad_heads(mla_wq_up[l], MLA_QK_DIM, 0, MLA_QK_DIM).astype(BF16)
        wk = _pad_heads(mla_wkv_up[l], MLA_QK_NOPE + HEAD_DIM, 0, MLA_QK_NOPE).astype(BF16)
        wv = mla_wkv_up[l].reshape(MLA_KV_LORA, MLA_HEADS, MLA_QK_NOPE + HEAD_DIM)[:, :, MLA_QK_NOPE:]
        wv = wv.reshape(MLA_KV_LORA, MLA_HEADS * HEAD_DIM).astype(BF16)
        qkg = jnp.tile(jnp.pad(mla_qk_gain[l], ((0, 0), (0, LANES - MLA_QK_DIM))), (1, MLA_HEADS))
        zr, y_pool, q, k, v = _mixin(
            x, mod_l, norm_gain[l, 0].reshape(1, D), win, pos3, freq, poolw,
            pool_scale[l].reshape(1, -1), mla_q_lat_gain[l].reshape(1, -1),
            mla_kv_lat_gain[l].reshape(1, -1), wq, wk, wv, qkg, bd128)

        pad_mu = ZR_COLS - mu_shift.shape[1] - RWKV_VRES_LORA
        mu_v = mu_shift_v[l - 1] if has_vres else jnp.zeros((RWKV_VRES_LORA,), F32)
        mu = jnp.concatenate([mu_shift[l], mu_v, jnp.zeros((pad_mu,), F32)]).reshape(1, ZR_COLS)
        v0 = rwkv_v0[l - 1] if has_vres else jnp.zeros((W,), F32)
        vec8 = jnp.concatenate([rwkv_vec[l], v0[None]], axis=0)
        w2a2 = jax.scipy.linalg.block_diag(rwkv_w2[l], rwkv_a2[l]).astype(BF16)
        g2 = jnp.pad(rwkv_g2[l], ((0, 2 * LANES - RWKV_GATE_LORA), (0, 0)))
        if has_vres:
            v2 = jnp.pad(rwkv_v2[l - 1], ((RWKV_GATE_LORA, 2 * LANES - RWKV_GATE_LORA - RWKV_VRES_LORA), (0, 0)))
        else:
            v2 = jnp.zeros((2 * LANES, W), F32)
        g2v2 = jnp.concatenate([g2, v2], axis=1).astype(BF16)
        y_rwkv, v_first = _rwkv(zr, v_first, mu, vec8, w2a2, g2v2, bd64)

        y_mla = _attention(q, k, v)

        is_moe = (l % 2 == 1)
        router_p = None
        if is_moe:
            router_p = jnp.pad(moe_router[l // 2], ((0, 0), (0, LANES - N_EXPERTS)))
        outs = _mixout(x, y_rwkv, y_pool, y_mla, w_out[l].astype(BF16), mod_l,
                       norm_gain[l, 1].reshape(1, D), router_p)
        x_mid, h2 = outs[0], outs[1]
        xf = x_mid.reshape(B * T, D)
        if is_moe:
            xo = _moe(h2.reshape(ROW_CHUNKS * B * T, PACK_CHUNK_W), outs[2].reshape(B * T, LANES), outs[3],
                      moe_w_in[l // 2].astype(BF16), moe_w_out[l // 2].astype(BF16), xf, mod_l)
        else:
            xo = _ffn(h2.reshape(B * T, D), ffn_w_in[l // 2].astype(BF16), ffn_w_out[l // 2].astype(BF16),
                      xf, mod_l)
        x = xo.reshape(B, T, D)
    return x
```

```python
import functools

import numpy as np
import jax
import jax.numpy as jnp
from jax import lax
from jax.experimental import pallas as pl
from jax.experimental.pallas import tpu as pltpu
from jax.experimental.pallas import tpu_sc as plsc

F32 = jnp.float32
BF16 = jnp.bfloat16

D_MODEL = 1024
HEAD_DIM = 64
RWKV_WIDTH = 512
RWKV_HEADS = 8
POOL_WIDTH = 256
POOL_WINDOWS = (2, 4, 8, 16)
POOL_HALO = 16
MLA_HEADS = 4
MLA_QK_NOPE = 64
MLA_QK_ROPE = 32
MLA_QK_DIM = 96
MLA_Q_LORA = 256
MLA_KV_LORA = 128
ROPE_BASE = 10000.0
RWKV_DECAY_LORA = 64
RWKV_ICLR_LORA = 64
RWKV_VRES_LORA = 32
RWKV_GATE_LORA = 160
RWKV_LNX_EPS = 64e-5
D_FF = 2816
N_EXPERTS = 8
TOP_K = 2
D_FF_EXPERT = 3584
NORM_EPS = 1e-6
NEG_INF = -1e30

LANES = 128
SUBLANES = 8
VMEM_LIMIT = 56 * 1024 * 1024

ZR_COLS = 1920
Z_POOL_OFF = ZR_COLS
Z_QLAT_OFF = Z_POOL_OFF + POOL_WIDTH
Z_KVLAT_OFF = Z_QLAT_OFF + MLA_Q_LORA
Z_KROPE_OFF = Z_KVLAT_OFF + MLA_KV_LORA
Z_COLS = Z_KROPE_OFF + LANES

TM_MIX = 512
WKV_CHUNK = 64
TQ = 512
TM_FFN = 512
TF_FFN = 1408
MOE_BLOCK = 1024
TF_MOE = 896
SC_WINDOW = 128
ROW_CHUNKS = 4
ROW_CHUNK_W = D_MODEL // ROW_CHUNKS
PACK_CHUNK_W = ROW_CHUNK_W // 2

SEGSUM_SPLITS = 1

NN = (((1,), (0,)), ((), ()))
NT = (((1,), (1,)), ((), ()))


def _dot(a, b, dims=NN):
    return lax.dot_general(a, b, dims, preferred_element_type=F32)


def _split2(a):
    hi = a.astype(BF16)
    lo = (a - hi.astype(F32)).astype(BF16)
    return hi, lo


def _mm(a, b, dims=NN, passes=3):
    if passes == 1:
        return _dot(a.astype(BF16), b.astype(BF16), dims)
    ah, al = _split2(a)
    bh, bl = _split2(b)
    return _dot(ah, bh, dims) + (_dot(ah, bl, dims) + _dot(al, bh, dims))


def _mm_exact_rhs(a, b_bf16, dims=NN, splits=SEGSUM_SPLITS):
    out = None
    rem = a
    for s in range(splits):
        part = rem.astype(BF16)
        term = _dot(part, b_bf16, dims)
        out = term if out is None else out + term
        if s + 1 < splits:
            rem = rem - part.astype(F32)
    return out


def _pack_bf16_pairs(h):
    w = h.shape[1] // 2
    lo = lax.bitcast_convert_type(h[:, :w].astype(BF16).astype(F32), jnp.uint32)
    hi = lax.bitcast_convert_type(h[:, w:].astype(BF16).astype(F32), jnp.uint32)
    return (lo >> 16) | (hi & jnp.uint32(0xFFFF0000))


def _unpack_bf16_pairs(p):
    lo = lax.bitcast_convert_type(p << 16, F32)
    hi = lax.bitcast_convert_type(p & jnp.uint32(0xFFFF0000), F32)
    return jnp.concatenate([lo, hi], axis=1).astype(BF16)


def _sigmoid(x):
    return 1.0 / (1.0 + jnp.exp(-x))


def _silu(x):
    return x * _sigmoid(x)


def _rms(x, eps=NORM_EPS):
    return x * lax.rsqrt(jnp.mean(x * x, axis=-1, keepdims=True) + eps)


def _cparams(sem):
    return pltpu.CompilerParams(dimension_semantics=sem, vmem_limit_bytes=VMEM_LIMIT)


def _adaln_kernel(c_ref, w_ref, b_ref, o_ref):
    ca = _silu(c_ref[...])
    o_ref[0] = _mm(ca, w_ref[0]) + b_ref[0]


def _adaln(c, w_ada, b_ada):
    L = w_ada.shape[0]
    B = c.shape[0]
    n = w_ada.shape[2] // D_MODEL
    return pl.pallas_call(
        _adaln_kernel,
        grid=(L, n),
        in_specs=[pl.BlockSpec((B, D_MODEL), lambda l, j: (0, 0)),
                  pl.BlockSpec((1, D_MODEL, D_MODEL), lambda l, j: (l, 0, j)),
                  pl.BlockSpec((1, 1, D_MODEL), lambda l, j: (l, 0, j))],
        out_specs=pl.BlockSpec((1, B, D_MODEL), lambda l, j: (l, 0, j)),
        out_shape=jax.ShapeDtypeStruct((L, B, n * D_MODEL), F32),
        compiler_params=_cparams(("arbitrary", "arbitrary")),
    )(c, w_ada, b_ada.reshape(L, 1, -1))


def _rope(x, cosf, sinf, lane):
    up = pltpu.roll(x, LANES - MLA_QK_ROPE // 2, axis=1)
    dn = pltpu.roll(x, MLA_QK_ROPE // 2, axis=1)
    rot = jnp.where(lane < MLA_QK_NOPE + MLA_QK_ROPE // 2, -up, dn)
    return x * cosf + rot * sinf


def _mixin_kernel(x_ref, mod_ref, gain_ref, win_ref, pos_ref, freq_ref, poolw_ref, pools_ref,
                  qg_ref, kvg_ref, wq_ref, wk_ref, wv_ref, qkg_ref, bd_ref,
                  zr_ref, yp_ref, q_ref, k_ref, v_ref, ubuf):
    i = pl.program_id(1)
    tm = x_ref.shape[1]
    x = x_ref[0]
    mod = mod_ref[0]
    h = _rms(x) * gain_ref[...] * (1.0 + mod[1:2]) + mod[0:1]
    z = _dot(h.astype(BF16), win_ref[...])
    zr_ref[0] = z[:, :ZR_COLS]

    @pl.when(i == 0)
    def _():
        ubuf[0:POOL_HALO, :] = jnp.zeros((POOL_HALO, POOL_WIDTH), F32)

    u = z[:, Z_POOL_OFF:Z_POOL_OFF + POOL_WIDTH]
    ubuf[POOL_HALO:, :] = u
    ue = ubuf[...]
    s2 = ue + pltpu.roll(ue, 1, axis=0)
    s4 = s2 + pltpu.roll(s2, 2, axis=0)
    s8 = s4 + pltpu.roll(s4, 4, axis=0)
    s16 = s8 + pltpu.roll(s8, 8, axis=0)
    ubuf[0:POOL_HALO, :] = u[tm - POOL_HALO:, :]
    lane_p = lax.broadcasted_iota(jnp.int32, (tm, POOL_WIDTH), 1)
    grp = lane_p // (POOL_WIDTH // len(POOL_WINDOWS))
    win_sum = jnp.where(grp == 0, s2[POOL_HALO:], jnp.where(grp == 1, s4[POOL_HALO:],
                        jnp.where(grp == 2, s8[POOL_HALO:], s16[POOL_HALO:])))
    win = jnp.where(grp == 0, 2, jnp.where(grp == 1, 4, jnp.where(grp == 2, 8, 16)))
    t_abs = i * tm + lax.broadcasted_iota(jnp.int32, (tm, POOL_WIDTH), 0)
    cnt = jnp.minimum(t_abs + 1, win).astype(F32)
    p = win_sum / cnt - u
    yp = _dot(p.astype(BF16), poolw_ref[...]) * pools_ref[...]
    yp_ref[0] = yp.astype(BF16)

    lane = lax.broadcasted_iota(jnp.int32, (tm, LANES), 1)
    in_rope = (lane >= MLA_QK_NOPE) & (lane < MLA_QK_DIM)
    ang = pos_ref[0].astype(F32) * freq_ref[...]
    cosf = jnp.where(in_rope, jnp.cos(ang), 1.0)
    sinf = jnp.where(in_rope, jnp.sin(ang), 0.0)

    q_lat = z[:, Z_QLAT_OFF:Z_QLAT_OFF + MLA_Q_LORA]
    kv_lat = z[:, Z_KVLAT_OFF:Z_KVLAT_OFF + MLA_KV_LORA]
    k_rope = z[:, Z_KROPE_OFF:Z_KROPE_OFF + LANES]
    qn = (_rms(q_lat) * qg_ref[...]).astype(BF16)
    kvn = (_rms(kv_lat) * kvg_ref[...]).astype(BF16)
    q = _dot(qn, wq_ref[...])
    kx = _dot(kvn, wk_ref[...])
    v = _dot(kvn, wv_ref[...])
    k_pe = _rope(k_rope, cosf, sinf, lane)
    qs, ks = [], []
    for hd in range(MLA_HEADS):
        sl = slice(hd * LANES, (hd + 1) * LANES)
        qs.append(_rope(q[:, sl], cosf, sinf, lane))
        ks.append(kx[:, sl] + k_pe)
    q = jnp.concatenate(qs, axis=1)
    k = jnp.concatenate(ks, axis=1)
    qss = _mm_exact_rhs(q * q, bd_ref[...]) * (1.0 / MLA_QK_DIM)
    kss = _mm_exact_rhs(k * k, bd_ref[...]) * (1.0 / MLA_QK_DIM)
    qkg = qkg_ref[...]
    q = q * lax.rsqrt(qss + NORM_EPS) * qkg[0:1] * (MLA_QK_DIM ** -0.5)
    k = k * lax.rsqrt(kss + NORM_EPS) * qkg[1:2]
    q_ref[0] = q.astype(BF16)
    k_ref[0] = k.astype(BF16)
    v_ref[0] = v.astype(BF16)


def _mixin(x, mod_l, gain, win, pos3, freq, poolw, pools, qg, kvg, wq, wk, wv, qkg, bd128):
    B, T, _ = x.shape
    tm = TM_MIX
    const = lambda shape: pl.BlockSpec(shape, lambda b, i: tuple(0 for _ in shape))
    tok = lambda w: pl.BlockSpec((1, tm, w), lambda b, i: (b, i, 0))
    return pl.pallas_call(
        _mixin_kernel,
        grid=(B, T // tm),
        in_specs=[tok(D_MODEL),
                  pl.BlockSpec((1, 6, D_MODEL), lambda b, i: (b, 0, 0)),
                  const((1, D_MODEL)), const((D_MODEL, Z_COLS)),
                  tok(1), const((1, LANES)),
                  const((POOL_WIDTH, POOL_WIDTH)), const((1, POOL_WIDTH)),
                  const((1, MLA_Q_LORA)), const((1, MLA_KV_LORA)),
                  const((MLA_Q_LORA, MLA_HEADS * LANES)), const((MLA_KV_LORA, MLA_HEADS * LANES)),
                  const((MLA_KV_LORA, MLA_HEADS * HEAD_DIM)), const((2, MLA_HEADS * LANES)),
                  const((MLA_HEADS * LANES, MLA_HEADS * LANES))],
        out_specs=[tok(ZR_COLS), tok(POOL_WIDTH), tok(MLA_HEADS * LANES), tok(MLA_HEADS * LANES),
                   tok(MLA_HEADS * HEAD_DIM)],
        out_shape=[jax.ShapeDtypeStruct((B, T, ZR_COLS), F32),
                   jax.ShapeDtypeStruct((B, T, POOL_WIDTH), BF16),
                   jax.ShapeDtypeStruct((B, T, MLA_HEADS * LANES), BF16),
                   jax.ShapeDtypeStruct((B, T, MLA_HEADS * LANES), BF16),
                   jax.ShapeDtypeStruct((B, T, MLA_HEADS * HEAD_DIM), BF16)],
        scratch_shapes=[pltpu.VMEM((POOL_HALO + tm, POOL_WIDTH), F32)],
        compiler_params=_cparams(("arbitrary", "arbitrary")),
    )(x, mod_l, gain, win, pos3, freq, poolw, pools, qg, kvg, wq, wk, wv, qkg, bd128)


WKV_PASSES_SCORE = 1
WKV_PASSES_INV = 1
WKV_PASSES_APPLY = 1
WKV_PASSES_STATE = 1
WKV_STEP_CHUNKS = 4


def _stack_heads(xp, lane):
    return jnp.concatenate([jnp.where(lane < HEAD_DIM, xp, 0.0),
                            jnp.where(lane >= HEAD_DIM, xp, 0.0)], axis=0)


def _wkv_prep(r, lw, k, v, kk, a, tri, masks):
    L = r[0].shape[0]
    n = 2 * L
    nc = len(r)
    each = lambda f, *ls: [f(*xs) for xs in zip(*ls)]
    lane = lax.broadcasted_iota(jnp.int32, (L, LANES), 1)
    stack = lambda x: _stack_heads(x, lane)
    cum = each(lambda x: _mm_exact_rhs_left(tri, x), lw)
    cum_last = each(lambda c: c[L - 1:L, :], cum)
    e_w = each(jnp.exp, cum)
    e_wm = each(lambda c, x: jnp.exp(c - x), cum, lw)
    e_iw = each(lambda c: jnp.exp(-c), cum)
    e_d = each(lambda cl, c: jnp.exp(cl - c), cum_last, cum)
    beta = each(lambda x, y: x * y, kk, a)
    r_t = each(lambda x, e: stack(x * e), r, e_w)
    a_t = each(lambda x, e: stack(-x * e), kk, e_wm)
    b_t = each(lambda x, e: stack(x * e), beta, e_iw)
    k_t = each(lambda x, e: stack(x * e), k, e_iw)
    b_d = each(lambda x, e: stack(x * e), beta, e_d)
    k_d = each(lambda x, e: stack(x * e), k, e_d)
    v_s = each(stack, v)
    g = each(lambda at, rt, bt, kt: _mm(jnp.concatenate([at, rt], axis=0),
                                        jnp.concatenate([bt, kt], axis=0), NT, WKV_PASSES_SCORE),
             a_t, r_t, b_t, k_t)
    strict, incl, levels = masks
    a_ab = each(lambda x: jnp.where(strict, x[:n, :n], 0.0), g)
    a_ak = each(lambda x: jnp.where(strict, x[:n, n:], 0.0), g)
    s_rb = each(lambda x: jnp.where(incl, x[n:, :n], 0.0), g)
    s_rk = each(lambda x: jnp.where(incl, x[n:, n:], 0.0), g)
    eye = jnp.where(levels[0][1], 1.0, 0.0)
    tinv = each(lambda x: eye + jnp.where(levels[0][0], x, 0.0), a_ab)
    for lvl_mask, _ in levels[1:]:
        et = each(lambda x, t: _mm(jnp.where(lvl_mask, x, 0.0), t, NN, WKV_PASSES_INV), a_ab, tinv)
        tinv = each(lambda t, x: t + _mm(t, x, NN, WKV_PASSES_INV), tinv, et)
    av = each(lambda x, y: _mm(x, y, NN, WKV_PASSES_APPLY), a_ak, v_s)
    tx = each(lambda t, x, y: _mm(t, jnp.concatenate([x, y], axis=1), NN, WKV_PASSES_APPLY),
              tinv, a_t, av)
    ra = each(lambda rt, s, x: rt + _mm(s, x[:, :LANES], NN, WKV_PASSES_APPLY), r_t, s_rb, tx)
    c2 = each(lambda sb, sk, x, vs: _mm(jnp.concatenate([sb, sk], axis=1),
                                        jnp.concatenate([x[:, LANES:], vs], axis=0),
                                        NN, WKV_PASSES_APPLY), s_rb, s_rk, tx, v_s)
    tb = each(lambda x, bd: _mm(x.T, bd, NN, WKV_PASSES_APPLY), tx, b_d)
    c3 = each(lambda x, vs, kd: x[LANES:] + _mm(vs.T, kd, NN, WKV_PASSES_APPLY), tb, v_s, k_d)
    return [(ra[i], c2[i], jnp.exp(cum_last[i]), tb[i][:LANES], c3[i]) for i in range(nc)]


def _mm_exact_rhs_left(tri_bf16, x):
    x0 = x.astype(BF16)
    r1 = x - x0.astype(F32)
    x1 = r1.astype(BF16)
    x2 = (r1 - x1.astype(F32)).astype(BF16)
    return _dot(tri_bf16, x0) + (_dot(tri_bf16, x1) + _dot(tri_bf16, x2))


def _wkv_masks(L):
    n = 2 * L
    row = lax.broadcasted_iota(jnp.int32, (n, n), 0)
    col = lax.broadcasted_iota(jnp.int32, (n, n), 1)
    strict = row > col
    incl = row >= col
    levels = []
    m = 1
    while m < L:
        same = (row // (2 * m)) == (col // (2 * m))
        lvl = same & ((row % (2 * m)) >= m) & ((col % (2 * m)) < m)
        levels.append((lvl, row == col))
        m *= 2
    return strict, incl, levels


def _rwkv_kernel(has_vres, *refs):
    if has_vres:
        (z_ref, vf_ref, mu_ref, vec_ref, w2a2_ref, g2v2_ref, bd_ref,
         y_ref, carry, state) = refs
    else:
        (z_ref, mu_ref, vec_ref, w2a2_ref, g2v2_ref, bd_ref,
         y_ref, vout_ref, carry, state) = refs
    c = pl.program_id(1)
    rows = z_ref.shape[1]
    L = WKV_CHUNK
    W = RWKV_WIDTH

    @pl.when(c == 0)
    def _():
        carry[...] = jnp.zeros(carry.shape, F32)
        state[...] = jnp.zeros(state.shape, F32)

    z = z_ref[0]
    row = lax.broadcasted_iota(jnp.int32, z.shape, 0)
    prev = jnp.where(row == 0, carry[SUBLANES - 1:SUBLANES, :], pltpu.roll(z, 1, axis=0))
    carry[...] = z[rows - SUBLANES:, :]
    zs = z + mu_ref[...] * (prev - z)
    r = zs[:, 0:W]
    k = zs[:, W:2 * W]
    v = zs[:, 2 * W:3 * W]
    wa = zs[:, 3 * W:3 * W + LANES]
    gb = zs[:, 3 * W + LANES:ZR_COLS]
    vec = vec_ref[...]
    w0, a0, k_k, k_a, r_k, ln_g, ln_b, v0 = (vec[j:j + 1] for j in range(8))

    lane_a = lax.broadcasted_iota(jnp.int32, wa.shape, 1)
    t1 = _dot(jnp.where(lane_a < RWKV_DECAY_LORA, jnp.tanh(wa), wa).astype(BF16), w2a2_ref[...])
    lane_g = lax.broadcasted_iota(jnp.int32, gb.shape, 1)
    t2 = _dot(jnp.where(lane_g < RWKV_GATE_LORA, _sigmoid(gb), gb).astype(BF16), g2v2_ref[...])
    xw = w0 + t1[:, :W]
    w_log = -(jnp.maximum(-xw, 0.0) + jnp.log1p(jnp.exp(-jnp.abs(xw)))) - 0.5
    lw = -jnp.exp(w_log)
    a = _sigmoid(a0 + t1[:, W:])
    g = t2[:, :W]
    if has_vres:
        v = v + (vf_ref[0] - v) * _sigmoid(v0 + t2[:, W:])
    else:
        vout_ref[0] = v
    kk = k * k_k
    bd = bd_ref[...]
    nrm = jnp.sqrt(_mm_exact_rhs(kk * kk, bd))
    kk = kk / jnp.maximum(nrm, 1e-12)
    k = k * (1.0 + (a - 1.0) * k_a)

    masks = _wkv_masks(L)
    rowt = lax.broadcasted_iota(jnp.int32, (L, L), 0)
    colt = lax.broadcasted_iota(jnp.int32, (L, L), 1)
    tri = jnp.where(rowt >= colt, 1.0, 0.0).astype(BF16)
    n_pairs = RWKV_HEADS // 2
    n_chunks = rows // L
    idx = [(ch, p) for ch in range(n_chunks) for p in range(n_pairs)]
    cut = lambda x: [x[ch * L:(ch + 1) * L, p * LANES:(p + 1) * LANES] for ch, p in idx]
    prep = dict(zip(idx, _wkv_prep(cut(r), cut(lw), cut(k), cut(v), cut(kk), cut(a), tri, masks)))
    S = [state[p] for p in range(n_pairs)]
    y_rows = []
    for ch in range(n_chunks):
        y_s = [_mm(prep[ch, p][0], S[p], NT, WKV_PASSES_STATE) + prep[ch, p][1] for p in range(n_pairs)]
        y_rows.append(jnp.concatenate([x[:L] + x[L:] for x in y_s], axis=1))
        S = [S[p] * prep[ch, p][2] + _mm(S[p], prep[ch, p][3], NN, WKV_PASSES_STATE) + prep[ch, p][4]
             for p in range(n_pairs)]
    for p in range(n_pairs):
        state[p] = S[p]
    y = jnp.concatenate(y_rows, axis=0)

    inv = 1.0 / HEAD_DIM
    mean = _mm_exact_rhs(y, bd) * inv
    yc = y - mean
    var = _mm_exact_rhs(yc * yc, bd) * inv
    yn = yc * lax.rsqrt(var + RWKV_LNX_EPS) * ln_g + ln_b
    bonus = _mm_exact_rhs(r * k * r_k, bd) * v
    y_ref[0] = ((yn + bonus) * g).astype(BF16)


def _rwkv(zr, v_first, mu, vec8, w2a2, g2v2, bd64):
    B, T, _ = zr.shape
    L = WKV_CHUNK * WKV_STEP_CHUNKS
    has_vres = v_first is not None
    const = lambda shape: pl.BlockSpec(shape, lambda b, c: tuple(0 for _ in shape))
    tok = lambda w: pl.BlockSpec((1, L, w), lambda b, c: (b, c, 0))
    in_specs = [tok(ZR_COLS)]
    args = [zr]
    if has_vres:
        in_specs.append(tok(RWKV_WIDTH))
        args.append(v_first)
    in_specs += [const((1, ZR_COLS)), const((8, RWKV_WIDTH)), const((LANES, 2 * RWKV_WIDTH)),
                 const((2 * LANES, 2 * RWKV_WIDTH)), const((RWKV_WIDTH, RWKV_WIDTH))]
    args += [mu, vec8, w2a2, g2v2, bd64]
    out_specs = [tok(RWKV_WIDTH)]
    out_shape = [jax.ShapeDtypeStruct((B, T, RWKV_WIDTH), BF16)]
    if not has_vres:
        out_specs.append(tok(RWKV_WIDTH))
        out_shape.append(jax.ShapeDtypeStruct((B, T, RWKV_WIDTH), F32))
    outs = pl.pallas_call(
        functools.partial(_rwkv_kernel, has_vres),
        grid=(B, T // L),
        in_specs=in_specs, out_specs=out_specs, out_shape=out_shape,
        scratch_shapes=[pltpu.VMEM((SUBLANES, ZR_COLS), F32),
                        pltpu.VMEM((RWKV_HEADS // 2, LANES, LANES), F32)],
        compiler_params=_cparams(("arbitrary", "arbitrary")),
    )(*args)
    return (outs[0], v_first) if has_vres else (outs[0], outs[1])


def _attn_step(q_ref, k_ref, v_ref, m_sc, l_sc, acc_sc, masked):
    tq = q_ref.shape[1]
    tk = k_ref.shape[1]
    lane = lax.broadcasted_iota(jnp.int32, (tq, LANES), 1)
    if masked:
        rowi = lax.broadcasted_iota(jnp.int32, (tq, tk), 0)
        coli = lax.broadcasted_iota(jnp.int32, (tq, tk), 1)
        keep = coli <= rowi
    for pr in range(MLA_HEADS // 2):
        vp = v_ref[0, :, pr * LANES:(pr + 1) * LANES]
        alphas, pvs = [], []
        for hh in range(2):
            hd = 2 * pr + hh
            q = q_ref[0, :, hd * LANES:(hd + 1) * LANES]
            k = k_ref[0, :, hd * LANES:(hd + 1) * LANES]
            s = _dot(q, k, NT)
            if masked:
                s = jnp.where(keep, s, NEG_INF)
            m_prev = m_sc[hd]
            m_new = jnp.maximum(m_prev, jnp.max(s, axis=-1, keepdims=True))
            alpha = jnp.exp(m_prev - m_new)
            p = jnp.exp(s - jnp.concatenate([m_new] * (tk // LANES), axis=1))
            l_sc[hd] = alpha * l_sc[hd] + jnp.sum(p, axis=-1, keepdims=True)
            m_sc[hd] = m_new
            alphas.append(alpha)
            pvs.append(_dot(p.astype(BF16), vp))
        first = lane < HEAD_DIM
        acc_sc[pr] = (acc_sc[pr] * jnp.where(first, alphas[0], alphas[1])
                      + jnp.where(first, pvs[0], pvs[1]))


def _attn_kernel(q_ref, k_ref, v_ref, o_ref, m_sc, l_sc, acc_sc):
    i = pl.program_id(1)
    j = pl.program_id(2)

    @pl.when(j == 0)
    def _():
        m_sc[...] = jnp.full(m_sc.shape, NEG_INF, F32)
        l_sc[...] = jnp.zeros(l_sc.shape, F32)
        acc_sc[...] = jnp.zeros(acc_sc.shape, F32)

    @pl.when(j < i)
    def _():
        _attn_step(q_ref, k_ref, v_ref, m_sc, l_sc, acc_sc, masked=False)

    @pl.when(j == i)
    def _():
        _attn_step(q_ref, k_ref, v_ref, m_sc, l_sc, acc_sc, masked=True)
        tq = q_ref.shape[1]
        lane = lax.broadcasted_iota(jnp.int32, (tq, LANES), 1)
        outs = []
        for pr in range(MLA_HEADS // 2):
            den = jnp.where(lane < HEAD_DIM, l_sc[2 * pr], l_sc[2 * pr + 1])
            outs.append(acc_sc[pr] / den)
        o_ref[0] = jnp.concatenate(outs, axis=1).astype(BF16)


def _attention(q, k, v):
    B, T, _ = q.shape
    nq = T // TQ
    return pl.pallas_call(
        _attn_kernel,
        grid=(B, nq, nq),
        in_specs=[pl.BlockSpec((1, TQ, MLA_HEADS * LANES), lambda b, i, j: (b, i, 0)),
                  pl.BlockSpec((1, TQ, MLA_HEADS * LANES), lambda b, i, j: (b, jnp.minimum(i, j), 0)),
                  pl.BlockSpec((1, TQ, MLA_HEADS * HEAD_DIM), lambda b, i, j: (b, jnp.minimum(i, j), 0))],
        out_specs=pl.BlockSpec((1, TQ, MLA_HEADS * HEAD_DIM), lambda b, i, j: (b, i, 0)),
        out_shape=jax.ShapeDtypeStruct((B, T, MLA_HEADS * HEAD_DIM), BF16),
        scratch_shapes=[pltpu.VMEM((MLA_HEADS, TQ, LANES), F32),
                        pltpu.VMEM((MLA_HEADS, TQ, LANES), F32),
                        pltpu.VMEM((MLA_HEADS // 2, TQ, LANES), F32)],
        compiler_params=_cparams(("arbitrary", "arbitrary", "arbitrary")),
    )(q, k, v)


def _mixout_kernel(has_router, *refs):
    if has_router:
        (x_ref, yr_ref, yp_ref, ym_ref, wo_ref, mod_ref, gain_ref, rt_ref, tri_ref,
         xo_ref, h_ref, route_ref, cnt_ref, cnt_sc) = refs
    else:
        x_ref, yr_ref, yp_ref, ym_ref, wo_ref, mod_ref, gain_ref, xo_ref, h_ref = refs
    mod = mod_ref[0]
    o1 = RWKV_WIDTH
    o2 = RWKV_WIDTH + POOL_WIDTH
    mix = (_dot(yr_ref[0], wo_ref[0:o1, :]) + _dot(yp_ref[0], wo_ref[o1:o2, :])
           + _dot(ym_ref[0], wo_ref[o2:, :]))
    x = x_ref[0] + mod[2:3] * mix
    xo_ref[0] = x
    h = _rms(x) * gain_ref[...] * (1.0 + mod[4:5]) + mod[3:4]
    if not has_router:
        h_ref[0] = h.astype(BF16)
    else:
        hp = _pack_bf16_pairs(h)
        for ck in range(ROW_CHUNKS):
            h_ref[ck, 0] = hp[:, ck * PACK_CHUNK_W:(ck + 1) * PACK_CHUNK_W]
        logits = _mm(h, rt_ref[...])
        lane = lax.broadcasted_iota(jnp.int32, logits.shape, 1).astype(F32)
        lg = jnp.where(lane < N_EXPERTS, logits, -jnp.inf)
        m1 = jnp.max(lg, axis=-1, keepdims=True)
        i1 = jnp.min(jnp.where(lg == m1, lane, float(LANES)), axis=-1, keepdims=True)
        lg2 = jnp.where(lane == i1, -jnp.inf, lg)
        m2 = jnp.max(lg2, axis=-1, keepdims=True)
        i2 = jnp.min(jnp.where(lg2 == m2, lane, float(LANES)), axis=-1, keepdims=True)
        e2 = jnp.exp(m2 - m1)
        g1 = 1.0 / (1.0 + e2)
        g2 = e2 / (1.0 + e2)
        first = (pl.program_id(0) == 0) & (pl.program_id(1) == 0)

        @pl.when(first)
        def _():
            cnt_sc[...] = jnp.zeros(cnt_sc.shape, F32)

        hit1 = lane == i1
        hit2 = lane == i2
        onehot = jnp.where(hit1 | hit2, 1.0, 0.0)
        prefix = _dot(tri_ref[...], onehot.astype(BF16)) + cnt_sc[0:1, :]
        r1 = jnp.sum(jnp.where(hit1, prefix, 0.0), axis=-1, keepdims=True)
        r2 = jnp.sum(jnp.where(hit2, prefix, 0.0), axis=-1, keepdims=True)
        cnt_sc[...] = cnt_sc[...] + jnp.sum(onehot, axis=0, keepdims=True)
        cnt_ref[...] = cnt_sc[...]
        vals = (i1, i2, g1, g2, r1, r2)
        route = jnp.zeros(logits.shape, F32)
        for pos, val in enumerate(vals):
            route = jnp.where(lane == pos, val, route)
        route_ref[0] = route


def _mixout(x, yr, yp, ym, wo, mod_l, gain, router_p):
    B, T, _ = x.shape
    tm = TM_MIX
    has_router = router_p is not None
    const = lambda shape: pl.BlockSpec(shape, lambda b, i: tuple(0 for _ in shape))
    tok = lambda w: pl.BlockSpec((1, tm, w), lambda b, i: (b, i, 0))
    in_specs = [tok(D_MODEL), tok(RWKV_WIDTH), tok(POOL_WIDTH), tok(MLA_HEADS * HEAD_DIM),
                const((D_MODEL, D_MODEL)), pl.BlockSpec((1, 6, D_MODEL), lambda b, i: (b, 0, 0)),
                const((1, D_MODEL))]
    args = [x, yr, yp, ym, wo, mod_l, gain]
    out_specs = [tok(D_MODEL), tok(D_MODEL)]
    out_shape = [jax.ShapeDtypeStruct((B, T, D_MODEL), F32), jax.ShapeDtypeStruct((B, T, D_MODEL), BF16)]
    scratch = []
    if has_router:
        out_specs[1] = pl.BlockSpec((ROW_CHUNKS, 1, tm, PACK_CHUNK_W), lambda b, i: (0, b, i, 0))
        out_shape[1] = jax.ShapeDtypeStruct((ROW_CHUNKS, B, T, PACK_CHUNK_W), jnp.uint32)
        ids = np.arange(tm)
        tri = jnp.asarray(ids[:, None] > ids[None, :], BF16)
        in_specs += [const((D_MODEL, LANES)), const((tm, tm))]
        args += [router_p, tri]
        out_specs += [tok(LANES), const((SUBLANES, LANES))]
        out_shape += [jax.ShapeDtypeStruct((B, T, LANES), F32),
                      jax.ShapeDtypeStruct((SUBLANES, LANES), F32)]
        scratch = [pltpu.VMEM((SUBLANES, LANES), F32)]
    return pl.pallas_call(
        functools.partial(_mixout_kernel, has_router),
        grid=(B, T // tm),
        in_specs=in_specs, out_specs=out_specs, out_shape=out_shape, scratch_shapes=scratch,
        compiler_params=_cparams(("arbitrary", "arbitrary")),
    )(*args)


def _ffn_kernel(h_ref, wg_ref, wu_ref, wo_ref, x_ref, mod_ref, o_ref, acc):
    j = pl.program_id(1)

    @pl.when(j == 0)
    def _():
        acc[...] = jnp.zeros(acc.shape, F32)

    h = h_ref[...]
    gg = _dot(h, wg_ref[...])
    uu = _dot(h, wu_ref[...])
    acc[...] += _dot((_silu(gg) * uu).astype(BF16), wo_ref[...])

    @pl.when(j == pl.num_programs(1) - 1)
    def _():
        o_ref[...] = x_ref[...] + mod_ref[0][5:6] * acc[...]


def _ffn(h2, w_in, w_out, x, mod_l):
    N = h2.shape[0]
    T = N // mod_l.shape[0]
    tm, tf = TM_FFN, TF_FFN
    nf = D_FF // tf
    per_b = T // tm
    return pl.pallas_call(
        _ffn_kernel,
        grid=(N // tm, nf),
        in_specs=[pl.BlockSpec((tm, D_MODEL), lambda i, j: (i, 0)),
                  pl.BlockSpec((D_MODEL, tf), lambda i, j: (0, j)),
                  pl.BlockSpec((D_MODEL, tf), lambda i, j: (0, j + nf)),
                  pl.BlockSpec((tf, D_MODEL), lambda i, j: (j, 0)),
                  pl.BlockSpec((tm, D_MODEL), lambda i, j: (i, 0)),
                  pl.BlockSpec((1, 6, D_MODEL), lambda i, j: (i // per_b, 0, 0))],
        out_specs=pl.BlockSpec((tm, D_MODEL), lambda i, j: (i, 0)),
        out_shape=jax.ShapeDtypeStruct((N, D_MODEL), F32),
        scratch_shapes=[pltpu.VMEM((tm, D_MODEL), F32)],
        compiler_params=_cparams(("arbitrary", "arbitrary")),
    )(h2, w_in, w_in, w_out, x, mod_l)


def _moe_kernel(be_ref, nv_ref, x_ref, wg_ref, wu_ref, wo_ref, o_ref, acc, xm):
    i = pl.program_id(0)
    j = pl.program_id(1)

    @pl.when(j == 0)
    def _():
        acc[...] = jnp.zeros(acc.shape, F32)
        row = lax.broadcasted_iota(jnp.int32, (xm.shape[0], 1), 0)
        xp = jnp.concatenate([x_ref[ck] for ck in range(ROW_CHUNKS)], axis=1)
        xp = jnp.where(row < nv_ref[i], xp, jnp.uint32(0))
        xm[...] = _unpack_bf16_pairs(xp)

    @pl.when(nv_ref[i] > 0)
    def _():
        x = xm[...]
        gg = _dot(x, wg_ref[0])
        uu = _dot(x, wu_ref[0])
        acc[...] += _dot((_silu(gg) * uu).astype(BF16), wo_ref[0])

    @pl.when(j == pl.num_programs(1) - 1)
    def _():
        for ck in range(ROW_CHUNKS):
            o_ref[ck] = acc[:, ck * ROW_CHUNK_W:(ck + 1) * ROW_CHUNK_W]


def _moe_experts(xs, w_in, w_out, block_exp, n_valid):
    n_rows = xs.shape[1]
    tm, tf = MOE_BLOCK, TF_MOE
    nf = D_FF_EXPERT // tf
    rows_spec = pl.BlockSpec((ROW_CHUNKS, tm, ROW_CHUNK_W), lambda i, j, be, nv: (0, i, 0))
    grid_spec = pltpu.PrefetchScalarGridSpec(
        num_scalar_prefetch=2,
        grid=(n_rows // tm, nf),
        in_specs=[pl.BlockSpec((ROW_CHUNKS, tm, PACK_CHUNK_W), lambda i, j, be, nv: (0, i, 0)),
                  pl.BlockSpec((1, D_MODEL, tf), lambda i, j, be, nv: (be[i], 0, j)),
                  pl.BlockSpec((1, D_MODEL, tf), lambda i, j, be, nv: (be[i], 0, j + nf)),
                  pl.BlockSpec((1, tf, D_MODEL), lambda i, j, be, nv: (be[i], j, 0))],
        out_specs=rows_spec,
        scratch_shapes=[pltpu.VMEM((tm, D_MODEL), F32), pltpu.VMEM((tm, D_MODEL), BF16)])
    return pl.pallas_call(
        _moe_kernel,
        grid_spec=grid_spec,
        out_shape=jax.ShapeDtypeStruct((ROW_CHUNKS, n_rows, ROW_CHUNK_W), F32),
        compiler_params=_cparams(("arbitrary", "arbitrary")),
    )(block_exp, n_valid, xs, w_in, w_in, w_out)


def _sc_mesh():
    return plsc.VectorSubcoreMesh(core_axis_name="c", subcore_axis_name="s")


def _sc_scatter_rows(x, dest, n_rows):
    N, D = x.shape
    K = dest.shape[0]
    win = SC_WINDOW

    @pl.kernel(out_type=jax.ShapeDtypeStruct((n_rows, D), x.dtype), mesh=_sc_mesh(), scratch_types=[])
    def scatter(x_hbm, d_hbm, o_hbm):
        def body(x_vmem, *idx_vmem):
            for iv in idx_vmem:
                pltpu.sync_copy(x_vmem, o_hbm.at[iv.at[0]])

        pltpu.emit_pipeline(
            body,
            grid=(N // win,),
            in_specs=[pl.BlockSpec((win, D), lambda i: (i, 0))]
            + [pl.BlockSpec((1, win), functools.partial(lambda k, i: (k, i), k)) for k in range(K)],
            out_specs=[],
            core_axis_name=("c", "s"),
            dimension_semantics=(pltpu.PARALLEL,),
        )(x_hbm, *([d_hbm] * K))

    return scatter(x, dest)


def _sc_gather_rows(x, idx):
    n = idx.shape[0]
    D = x.shape[1]
    win = SC_WINDOW

    @pl.kernel(out_type=jax.ShapeDtypeStruct((n, D), x.dtype), mesh=_sc_mesh(), scratch_types=[])
    def gather(x_hbm, i_hbm, o_hbm):
        def body(i_vmem, o_vmem):
            pltpu.sync_copy(x_hbm.at[i_vmem.at[0]], o_vmem)

        pltpu.emit_pipeline(
            body,
            grid=(n // win,),
            in_specs=[pl.BlockSpec((1, win), lambda i: (0, i))],
            out_specs=[pl.BlockSpec((win, D), lambda i: (i, 0))],
            core_axis_name=("c", "s"),
            dimension_semantics=(pltpu.PARALLEL,),
        )(i_hbm, o_hbm)

    return gather(x, idx.reshape(1, n))


def _combine_kernel(x_ref, ya_ref, yb_ref, route_ref, mod_ref, o_ref):
    rt = route_ref[...]
    ya = jnp.concatenate([ya_ref[0, ck] for ck in range(ROW_CHUNKS)], axis=1)
    yb = jnp.concatenate([yb_ref[0, ck] for ck in range(ROW_CHUNKS)], axis=1)
    f = rt[:, 2:3] * ya + rt[:, 3:4] * yb
    o_ref[...] = x_ref[...] + mod_ref[0][5:6] * f


def _combine(x, y2, route, mod_l):
    N = x.shape[0]
    T = N // mod_l.shape[0]
    tm = 1024
    per_b = T // tm
    tok = pl.BlockSpec((tm, D_MODEL), lambda i: (i, 0))
    slot = lambda k: pl.BlockSpec((1, ROW_CHUNKS, tm, ROW_CHUNK_W), lambda i: (k, 0, i, 0))
    return pl.pallas_call(
        _combine_kernel,
        grid=(N // tm,),
        in_specs=[tok, slot(0), slot(1),
                  pl.BlockSpec((tm, LANES), lambda i: (i, 0)),
                  pl.BlockSpec((1, 6, D_MODEL), lambda i: (i // per_b, 0, 0))],
        out_specs=tok,
        out_shape=jax.ShapeDtypeStruct((N, D_MODEL), F32),
        compiler_params=_cparams(("arbitrary",)),
    )(x, y2, y2, route, mod_l)


def _moe(h2, route, counts, w_in, w_out, x, mod_l):
    N = x.shape[0]
    blk = MOE_BLOCK
    cnt = counts[0, :N_EXPERTS].astype(jnp.int32)
    padded = (cnt + blk - 1) // blk * blk
    pend = jnp.cumsum(padded)
    pstart = pend - padded
    e = route[:, 0:TOP_K].astype(jnp.int32)
    rank = route[:, 2 * TOP_K:3 * TOP_K].astype(jnp.int32)
    dest = (jnp.take(pstart, e) + rank).T
    n_blocks = N * TOP_K // blk + N_EXPERTS
    bstart = jnp.arange(n_blocks, dtype=jnp.int32) * blk
    block_exp = jnp.clip(jnp.searchsorted(pend, bstart, side='right'), 0, N_EXPERTS - 1).astype(jnp.int32)
    n_valid = jnp.clip(cnt[block_exp] - (bstart - pstart[block_exp]), 0, blk).astype(jnp.int32)
    n_rows = n_blocks * blk
    dest_ck = dest[:, None, :] + (jnp.arange(ROW_CHUNKS, dtype=jnp.int32) * n_rows)[None, :, None]
    xs = _sc_scatter_rows(h2, dest_ck.reshape(TOP_K, ROW_CHUNKS * N), ROW_CHUNKS * n_rows)
    yb = _moe_experts(xs.reshape(ROW_CHUNKS, n_rows, PACK_CHUNK_W), w_in, w_out, block_exp, n_valid)
    y2 = _sc_gather_rows(yb.reshape(ROW_CHUNKS * n_rows, ROW_CHUNK_W), dest_ck.reshape(-1))
    return _combine(x, y2.reshape(TOP_K, ROW_CHUNKS, N, ROW_CHUNK_W), route, mod_l)


def _layout_w_in(w, has_vres):
    W = RWKV_WIDTH
    off_gd = 3 * W + RWKV_DECAY_LORA + RWKV_ICLR_LORA
    off_pool = off_gd + RWKV_GATE_LORA
    off_q = off_pool + POOL_WIDTH
    off_kv = off_q + MLA_Q_LORA
    off_kr = off_kv + MLA_KV_LORA
    n_base = off_kr + MLA_QK_ROPE
    d = w.shape[0]
    zeros = lambda n: jnp.zeros((d, n), w.dtype)
    vd = w[:, n_base:n_base + RWKV_VRES_LORA] if has_vres else zeros(RWKV_VRES_LORA)
    cols = [w[:, :off_gd], w[:, off_gd:off_pool], vd, zeros(ZR_COLS - off_pool - RWKV_VRES_LORA),
            w[:, off_pool:off_q], w[:, off_q:off_kv], w[:, off_kv:off_kr],
            zeros(MLA_QK_NOPE), w[:, off_kr:n_base], zeros(LANES - MLA_QK_DIM)]
    return jnp.concatenate(cols, axis=1).astype(BF16)


def _pad_heads(w, per_head, keep_from, keep_n):
    K = w.shape[0]
    wh = w.reshape(K, MLA_HEADS, per_head)[:, :, keep_from:keep_from + keep_n]
    wh = jnp.pad(wh, ((0, 0), (0, 0), (0, LANES - keep_n)))
    return wh.reshape(K, MLA_HEADS * LANES)


def kernel(x, c, positions, w_ada, b_ada, norm_gain, w_in_first, w_in_rest, mu_shift, mu_shift_v,
           rwkv_vec, rwkv_v0, rwkv_w2, rwkv_a2, rwkv_g2, rwkv_v2, pool_w, pool_scale,
           mla_q_lat_gain, mla_kv_lat_gain, mla_wq_up, mla_wkv_up, mla_qk_gain, w_out, ffn_w_in,
           ffn_w_out, moe_router, moe_w_in, moe_w_out):
    B, T, D = x.shape
    depth = w_ada.shape[0]
    W = RWKV_WIDTH
    mod = _adaln(c, w_ada, b_ada).reshape(depth, B, 6, D)
    pos3 = positions.reshape(B, T, 1)
    inv_freq = ROPE_BASE ** (-jnp.arange(0, MLA_QK_ROPE, 2, dtype=F32) / MLA_QK_ROPE)
    freq = jnp.concatenate([jnp.zeros((MLA_QK_NOPE,), F32), inv_freq, inv_freq,
                            jnp.zeros((LANES - MLA_QK_DIM,), F32)]).reshape(1, LANES)
    hid = np.arange(W) // HEAD_DIM
    bd64 = jnp.asarray(hid[:, None] == hid[None, :], BF16)
    bid = np.arange(MLA_HEADS * LANES) // LANES
    bd128 = jnp.asarray(bid[:, None] == bid[None, :], BF16)

    v_first = None
    for l in range(depth):
        has_vres = l > 0
        mod_l = mod[l]
        win = _layout_w_in(w_in_first if l == 0 else w_in_rest[l - 1], has_vres)
        poolw = jax.scipy.linalg.block_diag(*[pool_w[l, g] for g in range(len(POOL_WINDOWS))]).astype(BF16)
        wq = _pad_heads(mla_wq_up[l], MLA_QK_DIM, 0, MLA_QK_DIM).astype(BF16)
        wk = _pad_heads(mla_wkv_up[l], MLA_QK_NOPE + HEAD_DIM, 0, MLA_QK_NOPE).astype(BF16)
        wv = mla_wkv_up[l].reshape(MLA_KV_LORA, MLA_HEADS, MLA_QK_NOPE + HEAD_DIM)[:, :, MLA_QK_NOPE:]
        wv = wv.reshape(MLA_KV_LORA, MLA_HEADS * HEAD_DIM).astype(BF16)
        qkg = jnp.tile(jnp.pad(mla_qk_gain[l], ((0, 0), (0, LANES - MLA_QK_DIM))), (1, MLA_HEADS))
        zr, y_pool, q, k, v = _mixin(
            x, mod_l, norm_gain[l, 0].reshape(1, D), win, pos3, freq, poolw,
            pool_scale[l].reshape(1, -1), mla_q_lat_gain[l].reshape(1, -1),
            mla_kv_lat_gain[l].reshape(1, -1), wq, wk, wv, qkg, bd128)

        pad_mu = ZR_COLS - mu_shift.shape[1] - RWKV_VRES_LORA
        mu_v = mu_shift_v[l - 1] if has_vres else jnp.zeros((RWKV_VRES_LORA,), F32)
        mu = jnp.concatenate([mu_shift[l], mu_v, jnp.zeros((pad_mu,), F32)]).reshape(1, ZR_COLS)
        v0 = rwkv_v0[l - 1] if has_vres else jnp.zeros((W,), F32)
        vec8 = jnp.concatenate([rwkv_vec[l], v0[None]], axis=0)
        w2a2 = jax.scipy.linalg.block_diag(rwkv_w2[l], rwkv_a2[l]).astype(BF16)
        g2 = jnp.pad(rwkv_g2[l], ((0, 2 * LANES - RWKV_GATE_LORA), (0, 0)))
        if has_vres:
            v2 = jnp.pad(rwkv_v2[l - 1], ((RWKV_GATE_LORA, 2 * LANES - RWKV_GATE_LORA - RWKV_VRES_LORA), (0, 0)))
        else:
            v2 = jnp.zeros((2 * LANES, W), F32)
        g2v2 = jnp.concatenate([g2, v2], axis=1).astype(BF16)
        y_rwkv, v_first = _rwkv(zr, v_first, mu, vec8, w2a2, g2v2, bd64)

        y_mla = _attention(q, k, v)

        is_moe = (l % 2 == 1)
        router_p = None
        if is_moe:
            router_p = jnp.pad(moe_router[l // 2], ((0, 0), (0, LANES - N_EXPERTS)))
        outs = _mixout(x, y_rwkv, y_pool, y_mla, w_out[l].astype(BF16), mod_l,
                       norm_gain[l, 1].reshape(1, D), router_p)
        x_mid, h2 = outs[0], outs[1]
        xf = x_mid.reshape(B * T, D)
        if is_moe:
            xo = _moe(h2.reshape(ROW_CHUNKS * B * T, PACK_CHUNK_W), outs[2].reshape(B * T, LANES), outs[3],
                      moe_w_in[l // 2].astype(BF16), moe_w_out[l // 2].astype(BF16), xf, mod_l)
        else:
            xo = _ffn(h2.reshape(B * T, D), ffn_w_in[l // 2].astype(BF16), ffn_w_out[l // 2].astype(BF16),
                      xf, mod_l)
        x = xo.reshape(B, T, D)
    return x
```

```python
import functools

import numpy as np
import jax
import jax.numpy as jnp
from jax import lax
from jax.experimental import pallas as pl
from jax.experimental.pallas import tpu as pltpu
from jax.experimental.pallas import tpu_sc as plsc

F32 = jnp.float32
BF16 = jnp.bfloat16

D_MODEL = 1024
HEAD_DIM = 64
RWKV_WIDTH = 512
RWKV_HEADS = 8
POOL_WIDTH = 256
POOL_WINDOWS = (2, 4, 8, 16)
POOL_HALO = 16
MLA_HEADS = 4
MLA_QK_NOPE = 64
MLA_QK_ROPE = 32
MLA_QK_DIM = 96
MLA_Q_LORA = 256
MLA_KV_LORA = 128
ROPE_BASE = 10000.0
RWKV_DECAY_LORA = 64
RWKV_ICLR_LORA = 64
RWKV_VRES_LORA = 32
RWKV_GATE_LORA = 160
RWKV_LNX_EPS = 64e-5
D_FF = 2816
N_EXPERTS = 8
TOP_K = 2
D_FF_EXPERT = 3584
NORM_EPS = 1e-6
NEG_INF = -1e30

LANES = 128
SUBLANES = 8
VMEM_LIMIT = 56 * 1024 * 1024

ZR_COLS = 1920
Z_POOL_OFF = ZR_COLS
Z_QLAT_OFF = Z_POOL_OFF + POOL_WIDTH
Z_KVLAT_OFF = Z_QLAT_OFF + MLA_Q_LORA
Z_KROPE_OFF = Z_KVLAT_OFF + MLA_KV_LORA
Z_COLS = Z_KROPE_OFF + LANES

TM_MIX = 512
WKV_CHUNK = 64
TQ = 512
TM_FFN = 512
TF_FFN = 1408
MOE_BLOCK = 1024
TF_MOE = 896
SC_WINDOW = 128
ROW_CHUNKS = 4
ROW_CHUNK_W = D_MODEL // ROW_CHUNKS
PACK_CHUNK_W = ROW_CHUNK_W // 2

SEGSUM_SPLITS = 1

NN = (((1,), (0,)), ((), ()))
NT = (((1,), (1,)), ((), ()))


def _dot(a, b, dims=NN):
    return lax.dot_general(a, b, dims, preferred_element_type=F32)


def _split2(a):
    hi = a.astype(BF16)
    lo = (a - hi.astype(F32)).astype(BF16)
    return hi, lo


def _mm(a, b, dims=NN, passes=3):
    if passes == 1:
        return _dot(a.astype(BF16), b.astype(BF16), dims)
    ah, al = _split2(a)
    bh, bl = _split2(b)
    return _dot(ah, bh, dims) + (_dot(ah, bl, dims) + _dot(al, bh, dims))


def _mm_exact_rhs(a, b_bf16, dims=NN, splits=SEGSUM_SPLITS):
    out = None
    rem = a
    for s in range(splits):
        part = rem.astype(BF16)
        term = _dot(part, b_bf16, dims)
        out = term if out is None else out + term
        if s + 1 < splits:
            rem = rem - part.astype(F32)
    return out


def _pack_bf16_pairs(h):
    w = h.shape[1] // 2
    lo = lax.bitcast_convert_type(h[:, :w].astype(BF16).astype(F32), jnp.uint32)
    hi = lax.bitcast_convert_type(h[:, w:].astype(BF16).astype(F32), jnp.uint32)
    return (lo >> 16) | (hi & jnp.uint32(0xFFFF0000))


def _unpack_bf16_pairs(p):
    lo = lax.bitcast_convert_type(p << 16, F32)
    hi = lax.bitcast_convert_type(p & jnp.uint32(0xFFFF0000), F32)
    return jnp.concatenate([lo, hi], axis=1).astype(BF16)


def _sigmoid(x):
    return 1.0 / (1.0 + jnp.exp(-x))


def _silu(x):
    return x * _sigmoid(x)


def _rms(x, eps=NORM_EPS):
    return x * lax.rsqrt(jnp.mean(x * x, axis=-1, keepdims=True) + eps)


def _cparams(sem):
    return pltpu.CompilerParams(dimension_semantics=sem, vmem_limit_bytes=VMEM_LIMIT)


def _adaln_kernel(c_ref, w_ref, b_ref, o_ref):
    ca = _silu(c_ref[...])
    o_ref[0] = _mm(ca, w_ref[0]) + b_ref[0]


def _adaln(c, w_ada, b_ada):
    L = w_ada.shape[0]
    B = c.shape[0]
    n = w_ada.shape[2] // D_MODEL
    return pl.pallas_call(
        _adaln_kernel,
        grid=(L, n),
        in_specs=[pl.BlockSpec((B, D_MODEL), lambda l, j: (0, 0)),
                  pl.BlockSpec((1, D_MODEL, D_MODEL), lambda l, j: (l, 0, j)),
                  pl.BlockSpec((1, 1, D_MODEL), lambda l, j: (l, 0, j))],
        out_specs=pl.BlockSpec((1, B, D_MODEL), lambda l, j: (l, 0, j)),
        out_shape=jax.ShapeDtypeStruct((L, B, n * D_MODEL), F32),
        compiler_params=_cparams(("arbitrary", "arbitrary")),
    )(c, w_ada, b_ada.reshape(L, 1, -1))


def _rope(x, cosf, sinf, lane):
    up = pltpu.roll(x, LANES - MLA_QK_ROPE // 2, axis=1)
    dn = pltpu.roll(x, MLA_QK_ROPE // 2, axis=1)
    rot = jnp.where(lane < MLA_QK_NOPE + MLA_QK_ROPE // 2, -up, dn)
    return x * cosf + rot * sinf


def _mixin_kernel(x_ref, mod_ref, gain_ref, win_ref, pos_ref, freq_ref, poolw_ref, pools_ref,
                  qg_ref, kvg_ref, wq_ref, wk_ref, wv_ref, qkg_ref, bd_ref,
                  zr_ref, yp_ref, q_ref, k_ref, v_ref, ubuf):
    i = pl.program_id(1)
    tm = x_ref.shape[1]
    x = x_ref[0]
    mod = mod_ref[0]
    h = _rms(x) * gain_ref[...] * (1.0 + mod[1:2]) + mod[0:1]
    z = _dot(h.astype(BF16), win_ref[...])
    zr_ref[0] = z[:, :ZR_COLS]

    @pl.when(i == 0)
    def _():
        ubuf[0:POOL_HALO, :] = jnp.zeros((POOL_HALO, POOL_WIDTH), F32)

    u = z[:, Z_POOL_OFF:Z_POOL_OFF + POOL_WIDTH]
    ubuf[POOL_HALO:, :] = u
    ue = ubuf[...]
    s2 = ue + pltpu.roll(ue, 1, axis=0)
    s4 = s2 + pltpu.roll(s2, 2, axis=0)
    s8 = s4 + pltpu.roll(s4, 4, axis=0)
    s16 = s8 + pltpu.roll(s8, 8, axis=0)
    ubuf[0:POOL_HALO, :] = u[tm - POOL_HALO:, :]
    lane_p = lax.broadcasted_iota(jnp.int32, (tm, POOL_WIDTH), 1)
    grp = lane_p // (POOL_WIDTH // len(POOL_WINDOWS))
    win_sum = jnp.where(grp == 0, s2[POOL_HALO:], jnp.where(grp == 1, s4[POOL_HALO:],
                        jnp.where(grp == 2, s8[POOL_HALO:], s16[POOL_HALO:])))
    win = jnp.where(grp == 0, 2, jnp.where(grp == 1, 4, jnp.where(grp == 2, 8, 16)))
    t_abs = i * tm + lax.broadcasted_iota(jnp.int32, (tm, POOL_WIDTH), 0)
    cnt = jnp.minimum(t_abs + 1, win).astype(F32)
    p = win_sum / cnt - u
    yp = _dot(p.astype(BF16), poolw_ref[...]) * pools_ref[...]
    yp_ref[0] = yp.astype(BF16)

    lane = lax.broadcasted_iota(jnp.int32, (tm, LANES), 1)
    in_rope = (lane >= MLA_QK_NOPE) & (lane < MLA_QK_DIM)
    ang = pos_ref[0].astype(F32) * freq_ref[...]
    cosf = jnp.where(in_rope, jnp.cos(ang), 1.0)
    sinf = jnp.where(in_rope, jnp.sin(ang), 0.0)

    q_lat = z[:, Z_QLAT_OFF:Z_QLAT_OFF + MLA_Q_LORA]
    kv_lat = z[:, Z_KVLAT_OFF:Z_KVLAT_OFF + MLA_KV_LORA]
    k_rope = z[:, Z_KROPE_OFF:Z_KROPE_OFF + LANES]
    qn = (_rms(q_lat) * qg_ref[...]).astype(BF16)
    kvn = (_rms(kv_lat) * kvg_ref[...]).astype(BF16)
    q = _dot(qn, wq_ref[...])
    kx = _dot(kvn, wk_ref[...])
    v = _dot(kvn, wv_ref[...])
    k_pe = _rope(k_rope, cosf, sinf, lane)
    qs, ks = [], []
    for hd in range(MLA_HEADS):
        sl = slice(hd * LANES, (hd + 1) * LANES)
        qs.append(_rope(q[:, sl], cosf, sinf, lane))
        ks.append(kx[:, sl] + k_pe)
    q = jnp.concatenate(qs, axis=1)
    k = jnp.concatenate(ks, axis=1)
    qss = _mm_exact_rhs(q * q, bd_ref[...]) * (1.0 / MLA_QK_DIM)
    kss = _mm_exact_rhs(k * k, bd_ref[...]) * (1.0 / MLA_QK_DIM)
    qkg = qkg_ref[...]
    q = q * lax.rsqrt(qss + NORM_EPS) * qkg[0:1] * (MLA_QK_DIM ** -0.5)
    k = k * lax.rsqrt(kss + NORM_EPS) * qkg[1:2]
    q_ref[0] = q.astype(BF16)
    k_ref[0] = k.astype(BF16)
    v_ref[0] = v.astype(BF16)


def _mixin(x, mod_l, gain, win, pos3, freq, poolw, pools, qg, kvg, wq, wk, wv, qkg, bd128):
    B, T, _ = x.shape
    tm = TM_MIX
    const = lambda shape: pl.BlockSpec(shape, lambda b, i: tuple(0 for _ in shape))
    tok = lambda w: pl.BlockSpec((1, tm, w), lambda b, i: (b, i, 0))
    return pl.pallas_call(
        _mixin_kernel,
        grid=(B, T // tm),
        in_specs=[tok(D_MODEL),
                  pl.BlockSpec((1, 6, D_MODEL), lambda b, i: (b, 0, 0)),
                  const((1, D_MODEL)), const((D_MODEL, Z_COLS)),
                  tok(1), const((1, LANES)),
                  const((POOL_WIDTH, POOL_WIDTH)), const((1, POOL_WIDTH)),
                  const((1, MLA_Q_LORA)), const((1, MLA_KV_LORA)),
                  const((MLA_Q_LORA, MLA_HEADS * LANES)), const((MLA_KV_LORA, MLA_HEADS * LANES)),
                  const((MLA_KV_LORA, MLA_HEADS * HEAD_DIM)), const((2, MLA_HEADS * LANES)),
                  const((MLA_HEADS * LANES, MLA_HEADS * LANES))],
        out_specs=[tok(ZR_COLS), tok(POOL_WIDTH), tok(MLA_HEADS * LANES), tok(MLA_HEADS * LANES),
                   tok(MLA_HEADS * HEAD_DIM)],
        out_shape=[jax.ShapeDtypeStruct((B, T, ZR_COLS), F32),
                   jax.ShapeDtypeStruct((B, T, POOL_WIDTH), BF16),
                   jax.ShapeDtypeStruct((B, T, MLA_HEADS * LANES), BF16),
                   jax.ShapeDtypeStruct((B, T, MLA_HEADS * LANES), BF16),
                   jax.ShapeDtypeStruct((B, T, MLA_HEADS * HEAD_DIM), BF16)],
        scratch_shapes=[pltpu.VMEM((POOL_HALO + tm, POOL_WIDTH), F32)],
        compiler_params=_cparams(("arbitrary", "arbitrary")),
    )(x, mod_l, gain, win, pos3, freq, poolw, pools, qg, kvg, wq, wk, wv, qkg, bd128)


WKV_PASSES_SCORE = 1
WKV_PASSES_INV = 1
WKV_PASSES_APPLY = 1
WKV_PASSES_STATE = 1
WKV_STEP_CHUNKS = 4


def _stack_heads(xp, lane):
    return jnp.concatenate([jnp.where(lane < HEAD_DIM, xp, 0.0),
                            jnp.where(lane >= HEAD_DIM, xp, 0.0)], axis=0)


def _wkv_prep(r, lw, k, v, kk, a, tri, masks):
    L = r[0].shape[0]
    n = 2 * L
    nc = len(r)
    each = lambda f, *ls: [f(*xs) for xs in zip(*ls)]
    lane = lax.broadcasted_iota(jnp.int32, (L, LANES), 1)
    stack = lambda x: _stack_heads(x, lane)
    cum = each(lambda x: _mm_exact_rhs_left(tri, x), lw)
    cum_last = each(lambda c: c[L - 1:L, :], cum)
    e_w = each(jnp.exp, cum)
    e_wm = each(lambda c, x: jnp.exp(c - x), cum, lw)
    e_iw = each(lambda c: jnp.exp(-c), cum)
    e_d = each(lambda cl, c: jnp.exp(cl - c), cum_last, cum)
    beta = each(lambda x, y: x * y, kk, a)
    r_t = each(lambda x, e: stack(x * e), r, e_w)
    a_t = each(lambda x, e: stack(-x * e), kk, e_wm)
    b_t = each(lambda x, e: stack(x * e), beta, e_iw)
    k_t = each(lambda x, e: stack(x * e), k, e_iw)
    b_d = each(lambda x, e: stack(x * e), beta, e_d)
    k_d = each(lambda x, e: stack(x * e), k, e_d)
    v_s = each(stack, v)
    g = each(lambda at, rt, bt, kt: _mm(jnp.concatenate([at, rt], axis=0),
                                        jnp.concatenate([bt, kt], axis=0), NT, WKV_PASSES_SCORE),
             a_t, r_t, b_t, k_t)
    strict, incl, levels = masks
    a_ab = each(lambda x: jnp.where(strict, x[:n, :n], 0.0), g)
    a_ak = each(lambda x: jnp.where(strict, x[:n, n:], 0.0), g)
    s_rb = each(lambda x: jnp.where(incl, x[n:, :n], 0.0), g)
    s_rk = each(lambda x: jnp.where(incl, x[n:, n:], 0.0), g)
    eye = jnp.where(levels[0][1], 1.0, 0.0)
    tinv = each(lambda x: eye + jnp.where(levels[0][0], x, 0.0), a_ab)
    for lvl_mask, _ in levels[1:]:
        et = each(lambda x, t: _mm(jnp.where(lvl_mask, x, 0.0), t, NN, WKV_PASSES_INV), a_ab, tinv)
        tinv = each(lambda t, x: t + _mm(t, x, NN, WKV_PASSES_INV), tinv, et)
    av = each(lambda x, y: _mm(x, y, NN, WKV_PASSES_APPLY), a_ak, v_s)
    tx = each(lambda t, x, y: _mm(t, jnp.concatenate([x, y], axis=1), NN, WKV_PASSES_APPLY),
              tinv, a_t, av)
    ra = each(lambda rt, s, x: rt + _mm(s, x[:, :LANES], NN, WKV_PASSES_APPLY), r_t, s_rb, tx)
    c2 = each(lambda sb, sk, x, vs: _mm(jnp.concatenate([sb, sk], axis=1),
                                        jnp.concatenate([x[:, LANES:], vs], axis=0),
                                        NN, WKV_PASSES_APPLY), s_rb, s_rk, tx, v_s)
    tb = each(lambda x, bd: _mm(x.T, bd, NN, WKV_PASSES_APPLY), tx, b_d)
    c3 = each(lambda x, vs, kd: x[LANES:] + _mm(vs.T, kd, NN, WKV_PASSES_APPLY), tb, v_s, k_d)
    return [(ra[i], c2[i], jnp.exp(cum_last[i]), tb[i][:LANES], c3[i]) for i in range(nc)]


def _mm_exact_rhs_left(tri_bf16, x):
    x0 = x.astype(BF16)
    r1 = x - x0.astype(F32)
    x1 = r1.astype(BF16)
    x2 = (r1 - x1.astype(F32)).astype(BF16)
    return _dot(tri_bf16, x0) + (_dot(tri_bf16, x1) + _dot(tri_bf16, x2))


def _wkv_masks(L):
    n = 2 * L
    row = lax.broadcasted_iota(jnp.int32, (n, n), 0)
    col = lax.broadcasted_iota(jnp.int32, (n, n), 1)
    strict = row > col
    incl = row >= col
    levels = []
    m = 1
    while m < L:
        same = (row // (2 * m)) == (col // (2 * m))
        lvl = same & ((row % (2 * m)) >= m) & ((col % (2 * m)) < m)
        levels.append((lvl, row == col))
        m *= 2
    return strict, incl, levels


def _rwkv_kernel(has_vres, *refs):
    if has_vres:
        (z_ref, vf_ref, mu_ref, vec_ref, w2a2_ref, g2v2_ref, bd_ref,
         y_ref, carry, state) = refs
    else:
        (z_ref, mu_ref, vec_ref, w2a2_ref, g2v2_ref, bd_ref,
         y_ref, vout_ref, carry, state) = refs
    c = pl.program_id(1)
    rows = z_ref.shape[1]
    L = WKV_CHUNK
    W = RWKV_WIDTH

    @pl.when(c == 0)
    def _():
        carry[...] = jnp.zeros(carry.shape, F32)
        state[...] = jnp.zeros(state.shape, F32)

    z = z_ref[0]
    row = lax.broadcasted_iota(jnp.int32, z.shape, 0)
    prev = jnp.where(row == 0, carry[SUBLANES - 1:SUBLANES, :], pltpu.roll(z, 1, axis=0))
    carry[...] = z[rows - SUBLANES:, :]
    zs = z + mu_ref[...] * (prev - z)
    r = zs[:, 0:W]
    k = zs[:, W:2 * W]
    v = zs[:, 2 * W:3 * W]
    wa = zs[:, 3 * W:3 * W + LANES]
    gb = zs[:, 3 * W + LANES:ZR_COLS]
    vec = vec_ref[...]
    w0, a0, k_k, k_a, r_k, ln_g, ln_b, v0 = (vec[j:j + 1] for j in range(8))

    lane_a = lax.broadcasted_iota(jnp.int32, wa.shape, 1)
    t1 = _dot(jnp.where(lane_a < RWKV_DECAY_LORA, jnp.tanh(wa), wa).astype(BF16), w2a2_ref[...])
    lane_g = lax.broadcasted_iota(jnp.int32, gb.shape, 1)
    t2 = _dot(jnp.where(lane_g < RWKV_GATE_LORA, _sigmoid(gb), gb).astype(BF16), g2v2_ref[...])
    xw = w0 + t1[:, :W]
    w_log = -(jnp.maximum(-xw, 0.0) + jnp.log1p(jnp.exp(-jnp.abs(xw)))) - 0.5
    lw = -jnp.exp(w_log)
    a = _sigmoid(a0 + t1[:, W:])
    g = t2[:, :W]
    if has_vres:
        v = v + (vf_ref[0] - v) * _sigmoid(v0 + t2[:, W:])
    else:
        vout_ref[0] = v
    kk = k * k_k
    bd = bd_ref[...]
    nrm = jnp.sqrt(_mm_exact_rhs(kk * kk, bd))
    kk = kk / jnp.maximum(nrm, 1e-12)
    k = k * (1.0 + (a - 1.0) * k_a)

    masks = _wkv_masks(L)
    rowt = lax.broadcasted_iota(jnp.int32, (L, L), 0)
    colt = lax.broadcasted_iota(jnp.int32, (L, L), 1)
    tri = jnp.where(rowt >= colt, 1.0, 0.0).astype(BF16)
    n_pairs = RWKV_HEADS // 2
    n_chunks = rows // L
    idx = [(ch, p) for ch in range(n_chunks) for p in range(n_pairs)]
    cut = lambda x: [x[ch * L:(ch + 1) * L, p * LANES:(p + 1) * LANES] for ch, p in idx]
    prep = dict(zip(idx, _wkv_prep(cut(r), cut(lw), cut(k), cut(v), cut(kk), cut(a), tri, masks)))
    S = [state[p] for p in range(n_pairs)]
    y_rows = []
    for ch in range(n_chunks):
        y_s = [_mm(prep[ch, p][0], S[p], NT, WKV_PASSES_STATE) + prep[ch, p][1] for p in range(n_pairs)]
        y_rows.append(jnp.concatenate([x[:L] + x[L:] for x in y_s], axis=1))
        S = [S[p] * prep[ch, p][2] + _mm(S[p], prep[ch, p][3], NN, WKV_PASSES_STATE) + prep[ch, p][4]
             for p in range(n_pairs)]
    for p in range(n_pairs):
        state[p] = S[p]
    y = jnp.concatenate(y_rows, axis=0)

    inv = 1.0 / HEAD_DIM
    mean = _mm_exact_rhs(y, bd) * inv
    yc = y - mean
    var = _mm_exact_rhs(yc * yc, bd) * inv
    yn = yc * lax.rsqrt(var + RWKV_LNX_EPS) * ln_g + ln_b
    bonus = _mm_exact_rhs(r * k * r_k, bd) * v
    y_ref[0] = ((yn + bonus) * g).astype(BF16)


def _rwkv(zr, v_first, mu, vec8, w2a2, g2v2, bd64):
    B, T, _ = zr.shape
    L = WKV_CHUNK * WKV_STEP_CHUNKS
    has_vres = v_first is not None
    const = lambda shape: pl.BlockSpec(shape, lambda b, c: tuple(0 for _ in shape))
    tok = lambda w: pl.BlockSpec((1, L, w), lambda b, c: (b, c, 0))
    in_specs = [tok(ZR_COLS)]
    args = [zr]
    if has_vres:
        in_specs.append(tok(RWKV_WIDTH))
        args.append(v_first)
    in_specs += [const((1, ZR_COLS)), const((8, RWKV_WIDTH)), const((LANES, 2 * RWKV_WIDTH)),
                 const((2 * LANES, 2 * RWKV_WIDTH)), const((RWKV_WIDTH, RWKV_WIDTH))]
    args += [mu, vec8, w2a2, g2v2, bd64]
    out_specs = [tok(RWKV_WIDTH)]
    out_shape = [jax.ShapeDtypeStruct((B, T, RWKV_WIDTH), BF16)]
    if not has_vres:
        out_specs.append(tok(RWKV_WIDTH))
        out_shape.append(jax.ShapeDtypeStruct((B, T, RWKV_WIDTH), F32))
    outs = pl.pallas_call(
        functools.partial(_rwkv_kernel, has_vres),
        grid=(B, T // L),
        in_specs=in_specs, out_specs=out_specs, out_shape=out_shape,
        scratch_shapes=[pltpu.VMEM((SUBLANES, ZR_COLS), F32),
                        pltpu.VMEM((RWKV_HEADS // 2, LANES, LANES), F32)],
        compiler_params=_cparams(("arbitrary", "arbitrary")),
    )(*args)
    return (outs[0], v_first) if has_vres else (outs[0], outs[1])


def _attn_step(q_ref, k_ref, v_ref, m_sc, l_sc, acc_sc, masked):
    tq = q_ref.shape[1]
    tk = k_ref.shape[1]
    lane = lax.broadcasted_iota(jnp.int32, (tq, LANES), 1)
    if masked:
        rowi = lax.broadcasted_iota(jnp.int32, (tq, tk), 0)
        coli = lax.broadcasted_iota(jnp.int32, (tq, tk), 1)
        keep = coli <= rowi
    heads = range(MLA_HEADS)
    s = [_dot(q_ref[0, :, hd * LANES:(hd + 1) * LANES], k_ref[0, :, hd * LANES:(hd + 1) * LANES], NT)
         for hd in heads]
    if masked:
        s = [jnp.where(keep, x, NEG_INF) for x in s]
    m_prev = [m_sc[hd] for hd in heads]
    m_new = [jnp.maximum(m_prev[hd], jnp.max(s[hd], axis=-1, keepdims=True)) for hd in heads]
    alpha = [jnp.exp(m_prev[hd] - m_new[hd]) for hd in heads]
    p = [jnp.exp(s[hd] - jnp.concatenate([m_new[hd]] * (tk // LANES), axis=1)) for hd in heads]
    for hd in heads:
        l_sc[hd] = alpha[hd] * l_sc[hd] + jnp.sum(p[hd], axis=-1, keepdims=True)
        m_sc[hd] = m_new[hd]
    pv = [_dot(p[hd].astype(BF16), v_ref[0, :, (hd // 2) * LANES:(hd // 2 + 1) * LANES]) for hd in heads]
    first = lane < HEAD_DIM
    for pr in range(MLA_HEADS // 2):
        acc_sc[pr] = (acc_sc[pr] * jnp.where(first, alpha[2 * pr], alpha[2 * pr + 1])
                      + jnp.where(first, pv[2 * pr], pv[2 * pr + 1]))


def _attn_kernel(q_ref, k_ref, v_ref, o_ref, m_sc, l_sc, acc_sc):
    i = pl.program_id(1)
    j = pl.program_id(2)

    @pl.when(j == 0)
    def _():
        m_sc[...] = jnp.full(m_sc.shape, NEG_INF, F32)
        l_sc[...] = jnp.zeros(l_sc.shape, F32)
        acc_sc[...] = jnp.zeros(acc_sc.shape, F32)

    @pl.when(j < i)
    def _():
        _attn_step(q_ref, k_ref, v_ref, m_sc, l_sc, acc_sc, masked=False)

    @pl.when(j == i)
    def _():
        _attn_step(q_ref, k_ref, v_ref, m_sc, l_sc, acc_sc, masked=True)
        tq = q_ref.shape[1]
        lane = lax.broadcasted_iota(jnp.int32, (tq, LANES), 1)
        outs = []
        for pr in range(MLA_HEADS // 2):
            den = jnp.where(lane < HEAD_DIM, l_sc[2 * pr], l_sc[2 * pr + 1])
            outs.append(acc_sc[pr] / den)
        o_ref[0] = jnp.concatenate(outs, axis=1).astype(BF16)


def _attention(q, k, v):
    B, T, _ = q.shape
    nq = T // TQ
    return pl.pallas_call(
        _attn_kernel,
        grid=(B, nq, nq),
        in_specs=[pl.BlockSpec((1, TQ, MLA_HEADS * LANES), lambda b, i, j: (b, i, 0)),
                  pl.BlockSpec((1, TQ, MLA_HEADS * LANES), lambda b, i, j: (b, jnp.minimum(i, j), 0)),
                  pl.BlockSpec((1, TQ, MLA_HEADS * HEAD_DIM), lambda b, i, j: (b, jnp.minimum(i, j), 0))],
        out_specs=pl.BlockSpec((1, TQ, MLA_HEADS * HEAD_DIM), lambda b, i, j: (b, i, 0)),
        out_shape=jax.ShapeDtypeStruct((B, T, MLA_HEADS * HEAD_DIM), BF16),
        scratch_shapes=[pltpu.VMEM((MLA_HEADS, TQ, LANES), F32),
                        pltpu.VMEM((MLA_HEADS, TQ, LANES), F32),
                        pltpu.VMEM((MLA_HEADS // 2, TQ, LANES), F32)],
        compiler_params=_cparams(("arbitrary", "arbitrary", "arbitrary")),
    )(q, k, v)


def _mixout_kernel(has_router, *refs):
    if has_router:
        (x_ref, yr_ref, yp_ref, ym_ref, wo_ref, mod_ref, gain_ref, rt_ref, tri_ref,
         xo_ref, h_ref, route_ref, cnt_ref, cnt_sc) = refs
    else:
        x_ref, yr_ref, yp_ref, ym_ref, wo_ref, mod_ref, gain_ref, xo_ref, h_ref = refs
    mod = mod_ref[0]
    o1 = RWKV_WIDTH
    o2 = RWKV_WIDTH + POOL_WIDTH
    mix = (_dot(yr_ref[0], wo_ref[0:o1, :]) + _dot(yp_ref[0], wo_ref[o1:o2, :])
           + _dot(ym_ref[0], wo_ref[o2:, :]))
    x = x_ref[0] + mod[2:3] * mix
    xo_ref[0] = x
    h = _rms(x) * gain_ref[...] * (1.0 + mod[4:5]) + mod[3:4]
    if not has_router:
        h_ref[0] = h.astype(BF16)
    else:
        hp = _pack_bf16_pairs(h)
        for ck in range(ROW_CHUNKS):
            h_ref[ck, 0] = hp[:, ck * PACK_CHUNK_W:(ck + 1) * PACK_CHUNK_W]
        logits = _mm(h, rt_ref[...])
        lane = lax.broadcasted_iota(jnp.int32, logits.shape, 1).astype(F32)
        lg = jnp.where(lane < N_EXPERTS, logits, -jnp.inf)
        m1 = jnp.max(lg, axis=-1, keepdims=True)
        i1 = jnp.min(jnp.where(lg == m1, lane, float(LANES)), axis=-1, keepdims=True)
        lg2 = jnp.where(lane == i1, -jnp.inf, lg)
        m2 = jnp.max(lg2, axis=-1, keepdims=True)
        i2 = jnp.min(jnp.where(lg2 == m2, lane, float(LANES)), axis=-1, keepdims=True)
        e2 = jnp.exp(m2 - m1)
        g1 = 1.0 / (1.0 + e2)
        g2 = e2 / (1.0 + e2)
        first = (pl.program_id(0) == 0) & (pl.program_id(1) == 0)

        @pl.when(first)
        def _():
            cnt_sc[...] = jnp.zeros(cnt_sc.shape, F32)

        hit1 = lane == i1
        hit2 = lane == i2
        onehot = jnp.where(hit1 | hit2, 1.0, 0.0)
        prefix = _dot(tri_ref[...], onehot.astype(BF16)) + cnt_sc[0:1, :]
        r1 = jnp.sum(jnp.where(hit1, prefix, 0.0), axis=-1, keepdims=True)
        r2 = jnp.sum(jnp.where(hit2, prefix, 0.0), axis=-1, keepdims=True)
        cnt_sc[...] = cnt_sc[...] + jnp.sum(onehot, axis=0, keepdims=True)
        cnt_ref[...] = cnt_sc[...]
        vals = (i1, i2, g1, g2, r1, r2)
        route = jnp.zeros(logits.shape, F32)
        for pos, val in enumerate(vals):
            route = jnp.where(lane == pos, val, route)
        route_ref[0] = route


def _mixout(x, yr, yp, ym, wo, mod_l, gain, router_p):
    B, T, _ = x.shape
    tm = TM_MIX
    has_router = router_p is not None
    const = lambda shape: pl.BlockSpec(shape, lambda b, i: tuple(0 for _ in shape))
    tok = lambda w: pl.BlockSpec((1, tm, w), lambda b, i: (b, i, 0))
    in_specs = [tok(D_MODEL), tok(RWKV_WIDTH), tok(POOL_WIDTH), tok(MLA_HEADS * HEAD_DIM),
                const((D_MODEL, D_MODEL)), pl.BlockSpec((1, 6, D_MODEL), lambda b, i: (b, 0, 0)),
                const((1, D_MODEL))]
    args = [x, yr, yp, ym, wo, mod_l, gain]
    out_specs = [tok(D_MODEL), tok(D_MODEL)]
    out_shape = [jax.ShapeDtypeStruct((B, T, D_MODEL), F32), jax.ShapeDtypeStruct((B, T, D_MODEL), BF16)]
    scratch = []
    if has_router:
        out_specs[1] = pl.BlockSpec((ROW_CHUNKS, 1, tm, PACK_CHUNK_W), lambda b, i: (0, b, i, 0))
        out_shape[1] = jax.ShapeDtypeStruct((ROW_CHUNKS, B, T, PACK_CHUNK_W), jnp.uint32)
        ids = np.arange(tm)
        tri = jnp.asarray(ids[:, None] > ids[None, :], BF16)
        in_specs += [const((D_MODEL, LANES)), const((tm, tm))]
        args += [router_p, tri]
        out_specs += [tok(LANES), const((SUBLANES, LANES))]
        out_shape += [jax.ShapeDtypeStruct((B, T, LANES), F32),
                      jax.ShapeDtypeStruct((SUBLANES, LANES), F32)]
        scratch = [pltpu.VMEM((SUBLANES, LANES), F32)]
    return pl.pallas_call(
        functools.partial(_mixout_kernel, has_router),
        grid=(B, T // tm),
        in_specs=in_specs, out_specs=out_specs, out_shape=out_shape, scratch_shapes=scratch,
        compiler_params=_cparams(("arbitrary", "arbitrary")),
    )(*args)


def _ffn_kernel(h_ref, wg_ref, wu_ref, wo_ref, x_ref, mod_ref, o_ref, acc):
    j = pl.program_id(1)

    @pl.when(j == 0)
    def _():
        acc[...] = jnp.zeros(acc.shape, F32)

    h = h_ref[...]
    gg = _dot(h, wg_ref[...])
    uu = _dot(h, wu_ref[...])
    acc[...] += _dot((_silu(gg) * uu).astype(BF16), wo_ref[...])

    @pl.when(j == pl.num_programs(1) - 1)
    def _():
        o_ref[...] = x_ref[...] + mod_ref[0][5:6] * acc[...]


def _ffn(h2, w_in, w_out, x, mod_l):
    N = h2.shape[0]
    T = N // mod_l.shape[0]
    tm, tf = TM_FFN, TF_FFN
    nf = D_FF // tf
    per_b = T // tm
    return pl.pallas_call(
        _ffn_kernel,
        grid=(N // tm, nf),
        in_specs=[pl.BlockSpec((tm, D_MODEL), lambda i, j: (i, 0)),
                  pl.BlockSpec((D_MODEL, tf), lambda i, j: (0, j)),
                  pl.BlockSpec((D_MODEL, tf), lambda i, j: (0, j + nf)),
                  pl.BlockSpec((tf, D_MODEL), lambda i, j: (j, 0)),
                  pl.BlockSpec((tm, D_MODEL), lambda i, j: (i, 0)),
                  pl.BlockSpec((1, 6, D_MODEL), lambda i, j: (i // per_b, 0, 0))],
        out_specs=pl.BlockSpec((tm, D_MODEL), lambda i, j: (i, 0)),
        out_shape=jax.ShapeDtypeStruct((N, D_MODEL), F32),
        scratch_shapes=[pltpu.VMEM((tm, D_MODEL), F32)],
        compiler_params=_cparams(("arbitrary", "arbitrary")),
    )(h2, w_in, w_in, w_out, x, mod_l)


def _moe_kernel(be_ref, nv_ref, last_ref, x_ref, wg_ref, wu_ref, wo_ref, o_ref, acc, xm):
    i = pl.program_id(0)
    j = pl.program_id(1)

    @pl.when(i <= last_ref[0])
    def _():
        @pl.when(j == 0)
        def _():
            acc[...] = jnp.zeros(acc.shape, F32)
            row = lax.broadcasted_iota(jnp.int32, (xm.shape[0], 1), 0)
            xp = jnp.concatenate([x_ref[ck] for ck in range(ROW_CHUNKS)], axis=1)
            xp = jnp.where(row < nv_ref[i], xp, jnp.uint32(0))
            xm[...] = _unpack_bf16_pairs(xp)

        x = xm[...]
        gg = _dot(x, wg_ref[0].astype(BF16))
        uu = _dot(x, wu_ref[0].astype(BF16))
        acc[...] += _dot((_silu(gg) * uu).astype(BF16), wo_ref[0].astype(BF16))

        @pl.when(j == pl.num_programs(1) - 1)
        def _():
            for ck in range(ROW_CHUNKS):
                o_ref[ck] = acc[:, ck * ROW_CHUNK_W:(ck + 1) * ROW_CHUNK_W]


def _moe_experts(xs, w_in, w_out, block_exp, n_valid, last_blk):
    n_rows = xs.shape[1]
    tm, tf = MOE_BLOCK, TF_MOE
    nf = D_FF_EXPERT // tf
    blk = lambda i, last: jnp.minimum(i, last[0])
    chunk = lambda i, j, last: jnp.where(i <= last[0], j, nf - 1)
    grid_spec = pltpu.PrefetchScalarGridSpec(
        num_scalar_prefetch=3,
        grid=(n_rows // tm, nf),
        in_specs=[pl.BlockSpec((ROW_CHUNKS, tm, PACK_CHUNK_W),
                               lambda i, j, be, nv, last: (0, blk(i, last), 0)),
                  pl.BlockSpec((1, D_MODEL, tf),
                               lambda i, j, be, nv, last: (be[blk(i, last)], 0, chunk(i, j, last))),
                  pl.BlockSpec((1, D_MODEL, tf),
                               lambda i, j, be, nv, last: (be[blk(i, last)], 0, chunk(i, j, last) + nf)),
                  pl.BlockSpec((1, tf, D_MODEL),
                               lambda i, j, be, nv, last: (be[blk(i, last)], chunk(i, j, last), 0))],
        out_specs=pl.BlockSpec((ROW_CHUNKS, tm, ROW_CHUNK_W), lambda i, j, be, nv, last: (0, blk(i, last), 0)),
        scratch_shapes=[pltpu.VMEM((tm, D_MODEL), F32), pltpu.VMEM((tm, D_MODEL), BF16)])
    return pl.pallas_call(
        _moe_kernel,
        grid_spec=grid_spec,
        out_shape=jax.ShapeDtypeStruct((ROW_CHUNKS, n_rows, ROW_CHUNK_W), F32),
        compiler_params=_cparams(("arbitrary", "arbitrary")),
    )(block_exp, n_valid, last_blk, xs, w_in, w_in, w_out)


def _sc_mesh():
    return plsc.VectorSubcoreMesh(core_axis_name="c", subcore_axis_name="s")


def _sc_scatter_rows(x, dest, n_rows):
    N, D = x.shape
    K = dest.shape[0]
    win = SC_WINDOW

    @pl.kernel(out_type=jax.ShapeDtypeStruct((n_rows, D), x.dtype), mesh=_sc_mesh(), scratch_types=[])
    def scatter(x_hbm, d_hbm, o_hbm):
        def body(x_vmem, *idx_vmem):
            for iv in idx_vmem:
                pltpu.sync_copy(x_vmem, o_hbm.at[iv.at[0]])

        pltpu.emit_pipeline(
            body,
            grid=(N // win,),
            in_specs=[pl.BlockSpec((win, D), lambda i: (i, 0))]
            + [pl.BlockSpec((1, win), functools.partial(lambda k, i: (k, i), k)) for k in range(K)],
            out_specs=[],
            core_axis_name=("c", "s"),
            dimension_semantics=(pltpu.PARALLEL,),
        )(x_hbm, *([d_hbm] * K))

    return scatter(x, dest)


def _sc_gather_rows(x, idx):
    n = idx.shape[0]
    D = x.shape[1]
    win = SC_WINDOW

    @pl.kernel(out_type=jax.ShapeDtypeStruct((n, D), x.dtype), mesh=_sc_mesh(), scratch_types=[])
    def gather(x_hbm, i_hbm, o_hbm):
        def body(i_vmem, o_vmem):
            pltpu.sync_copy(x_hbm.at[i_vmem.at[0]], o_vmem)

        pltpu.emit_pipeline(
            body,
            grid=(n // win,),
            in_specs=[pl.BlockSpec((1, win), lambda i: (0, i))],
            out_specs=[pl.BlockSpec((win, D), lambda i: (i, 0))],
            core_axis_name=("c", "s"),
            dimension_semantics=(pltpu.PARALLEL,),
        )(i_hbm, o_hbm)

    return gather(x, idx.reshape(1, n))


def _combine_kernel(x_ref, ya_ref, yb_ref, route_ref, mod_ref, o_ref):
    rt = route_ref[...]
    ya = jnp.concatenate([ya_ref[0, ck] for ck in range(ROW_CHUNKS)], axis=1)
    yb = jnp.concatenate([yb_ref[0, ck] for ck in range(ROW_CHUNKS)], axis=1)
    f = rt[:, 2:3] * ya + rt[:, 3:4] * yb
    o_ref[...] = x_ref[...] + mod_ref[0][5:6] * f


def _combine(x, y2, route, mod_l):
    N = x.shape[0]
    T = N // mod_l.shape[0]
    tm = 1024
    per_b = T // tm
    tok = pl.BlockSpec((tm, D_MODEL), lambda i: (i, 0))
    slot = lambda k: pl.BlockSpec((1, ROW_CHUNKS, tm, ROW_CHUNK_W), lambda i: (k, 0, i, 0))
    return pl.pallas_call(
        _combine_kernel,
        grid=(N // tm,),
        in_specs=[tok, slot(0), slot(1),
                  pl.BlockSpec((tm, LANES), lambda i: (i, 0)),
                  pl.BlockSpec((1, 6, D_MODEL), lambda i: (i // per_b, 0, 0))],
        out_specs=tok,
        out_shape=jax.ShapeDtypeStruct((N, D_MODEL), F32),
        compiler_params=_cparams(("arbitrary",)),
    )(x, y2, y2, route, mod_l)


def _moe(h2, route, counts, w_in, w_out, x, mod_l):
    N = x.shape[0]
    blk = MOE_BLOCK
    cnt = counts[0, :N_EXPERTS].astype(jnp.int32)
    padded = (cnt + blk - 1) // blk * blk
    pend = jnp.cumsum(padded)
    pstart = pend - padded
    e = route[:, 0:TOP_K].astype(jnp.int32)
    rank = route[:, 2 * TOP_K:3 * TOP_K].astype(jnp.int32)
    dest = (jnp.take(pstart, e) + rank).T
    n_blocks = N * TOP_K // blk + N_EXPERTS
    bstart = jnp.arange(n_blocks, dtype=jnp.int32) * blk
    block_exp = jnp.clip(jnp.searchsorted(pend, bstart, side='right'), 0, N_EXPERTS - 1).astype(jnp.int32)
    n_valid = jnp.clip(cnt[block_exp] - (bstart - pstart[block_exp]), 0, blk).astype(jnp.int32)
    n_rows = n_blocks * blk
    dest_ck = dest[:, None, :] + (jnp.arange(ROW_CHUNKS, dtype=jnp.int32) * n_rows)[None, :, None]
    xs = _sc_scatter_rows(h2, dest_ck.reshape(TOP_K, ROW_CHUNKS * N), ROW_CHUNKS * n_rows)
    last_blk = (pend[-1:] // blk - 1).astype(jnp.int32)
    yb = _moe_experts(xs.reshape(ROW_CHUNKS, n_rows, PACK_CHUNK_W), w_in, w_out, block_exp, n_valid,
                      last_blk)
    y2 = _sc_gather_rows(yb.reshape(ROW_CHUNKS * n_rows, ROW_CHUNK_W), dest_ck.reshape(-1))
    return _combine(x, y2.reshape(TOP_K, ROW_CHUNKS, N, ROW_CHUNK_W), route, mod_l)


def _layout_w_in(w, has_vres):
    W = RWKV_WIDTH
    off_gd = 3 * W + RWKV_DECAY_LORA + RWKV_ICLR_LORA
    off_pool = off_gd + RWKV_GATE_LORA
    off_q = off_pool + POOL_WIDTH
    off_kv = off_q + MLA_Q_LORA
    off_kr = off_kv + MLA_KV_LORA
    n_base = off_kr + MLA_QK_ROPE
    d = w.shape[0]
    zeros = lambda n: jnp.zeros((d, n), w.dtype)
    vd = w[:, n_base:n_base + RWKV_VRES_LORA] if has_vres else zeros(RWKV_VRES_LORA)
    cols = [w[:, :off_gd], w[:, off_gd:off_pool], vd, zeros(ZR_COLS - off_pool - RWKV_VRES_LORA),
            w[:, off_pool:off_q], w[:, off_q:off_kv], w[:, off_kv:off_kr],
            zeros(MLA_QK_NOPE), w[:, off_kr:n_base], zeros(LANES - MLA_QK_DIM)]
    return jnp.concatenate(cols, axis=1).astype(BF16)


def _pad_heads(w, per_head, keep_from, keep_n):
    K = w.shape[0]
    wh = w.reshape(K, MLA_HEADS, per_head)[:, :, keep_from:keep_from + keep_n]
    wh = jnp.pad(wh, ((0, 0), (0, 0), (0, LANES - keep_n)))
    return wh.reshape(K, MLA_HEADS * LANES)


def kernel(x, c, positions, w_ada, b_ada, norm_gain, w_in_first, w_in_rest, mu_shift, mu_shift_v,
           rwkv_vec, rwkv_v0, rwkv_w2, rwkv_a2, rwkv_g2, rwkv_v2, pool_w, pool_scale,
           mla_q_lat_gain, mla_kv_lat_gain, mla_wq_up, mla_wkv_up, mla_qk_gain, w_out, ffn_w_in,
           ffn_w_out, moe_router, moe_w_in, moe_w_out):
    B, T, D = x.shape
    depth = w_ada.shape[0]
    W = RWKV_WIDTH
    mod = _adaln(c, w_ada, b_ada).reshape(depth, B, 6, D)
    pos3 = positions.reshape(B, T, 1)
    inv_freq = ROPE_BASE ** (-jnp.arange(0, MLA_QK_ROPE, 2, dtype=F32) / MLA_QK_ROPE)
    freq = jnp.concatenate([jnp.zeros((MLA_QK_NOPE,), F32), inv_freq, inv_freq,
                            jnp.zeros((LANES - MLA_QK_DIM,), F32)]).reshape(1, LANES)
    hid = np.arange(W) // HEAD_DIM
    bd64 = jnp.asarray(hid[:, None] == hid[None, :], BF16)
    bid = np.arange(MLA_HEADS * LANES) // LANES
    bd128 = jnp.asarray(bid[:, None] == bid[None, :], BF16)

    v_first = None
    for l in range(depth):
        has_vres = l > 0
        mod_l = mod[l]
        win = _layout_w_in(w_in_first if l == 0 else w_in_rest[l - 1], has_vres)
        poolw = jax.scipy.linalg.block_diag(*[pool_w[l, g] for g in range(len(POOL_WINDOWS))]).astype(BF16)
        wq = _pad_heads(mla_wq_up[l], MLA_QK_DIM, 0, MLA_QK_DIM).astype(BF16)
        wk = _pad_heads(mla_wkv_up[l], MLA_QK_NOPE + HEAD_DIM, 0, MLA_QK_NOPE).astype(BF16)
        wv = mla_wkv_up[l].reshape(MLA_KV_LORA, MLA_HEADS, MLA_QK_NOPE + HEAD_DIM)[:, :, MLA_QK_NOPE:]
        wv = wv.reshape(MLA_KV_LORA, MLA_HEADS * HEAD_DIM).astype(BF16)
        qkg = jnp.tile(jnp.pad(mla_qk_gain[l], ((0, 0), (0, LANES - MLA_QK_DIM))), (1, MLA_HEADS))
        zr, y_pool, q, k, v = _mixin(
            x, mod_l, norm_gain[l, 0].reshape(1, D), win, pos3, freq, poolw,
            pool_scale[l].reshape(1, -1), mla_q_lat_gain[l].reshape(1, -1),
            mla_kv_lat_gain[l].reshape(1, -1), wq, wk, wv, qkg, bd128)

        pad_mu = ZR_COLS - mu_shift.shape[1] - RWKV_VRES_LORA
        mu_v = mu_shift_v[l - 1] if has_vres else jnp.zeros((RWKV_VRES_LORA,), F32)
        mu = jnp.concatenate([mu_shift[l], mu_v, jnp.zeros((pad_mu,), F32)]).reshape(1, ZR_COLS)
        v0 = rwkv_v0[l - 1] if has_vres else jnp.zeros((W,), F32)
        vec8 = jnp.concatenate([rwkv_vec[l], v0[None]], axis=0)
        w2a2 = jax.scipy.linalg.block_diag(rwkv_w2[l], rwkv_a2[l]).astype(BF16)
        g2 = jnp.pad(rwkv_g2[l], ((0, 2 * LANES - RWKV_GATE_LORA), (0, 0)))
        if has_vres:
            v2 = jnp.pad(rwkv_v2[l - 1], ((RWKV_GATE_LORA, 2 * LANES - RWKV_GATE_LORA - RWKV_VRES_LORA), (0, 0)))
        else:
            v2 = jnp.zeros((2 * LANES, W), F32)
        g2v2 = jnp.concatenate([g2, v2], axis=1).astype(BF16)
        y_rwkv, v_first = _rwkv(zr, v_first, mu, vec8, w2a2, g2v2, bd64)

        y_mla = _attention(q, k, v)

        is_moe = (l % 2 == 1)
        router_p = None
        if is_moe:
            router_p = jnp.pad(moe_router[l // 2], ((0, 0), (0, LANES - N_EXPERTS)))
        outs = _mixout(x, y_rwkv, y_pool, y_mla, w_out[l].astype(BF16), mod_l,
                       norm_gain[l, 1].reshape(1, D), router_p)
        x_mid, h2 = outs[0], outs[1]
        xf = x_mid.reshape(B * T, D)
        if is_moe:
            xo = _moe(h2.reshape(ROW_CHUNKS * B * T, PACK_CHUNK_W), outs[2].reshape(B * T, LANES), outs[3],
                      moe_w_in[l // 2], moe_w_out[l // 2], xf, mod_l)
        else:
            xo = _ffn(h2.reshape(B * T, D), ffn_w_in[l // 2].astype(BF16), ffn_w_out[l // 2].astype(BF16),
                      xf, mod_l)
        x = xo.reshape(B, T, D)
    return x
```

```python
import functools

import numpy as np
import jax
import jax.numpy as jnp
from jax import lax
from jax.experimental import pallas as pl
from jax.experimental.pallas import tpu as pltpu
from jax.experimental.pallas import tpu_sc as plsc

F32 = jnp.float32
BF16 = jnp.bfloat16

D_MODEL = 1024
HEAD_DIM = 64
RWKV_WIDTH = 512
RWKV_HEADS = 8
POOL_WIDTH = 256
POOL_WINDOWS = (2, 4, 8, 16)
POOL_HALO = 16
MLA_HEADS = 4
MLA_QK_NOPE = 64
MLA_QK_ROPE = 32
MLA_QK_DIM = 96
MLA_Q_LORA = 256
MLA_KV_LORA = 128
ROPE_BASE = 10000.0
RWKV_DECAY_LORA = 64
RWKV_ICLR_LORA = 64
RWKV_VRES_LORA = 32
RWKV_GATE_LORA = 160
RWKV_LNX_EPS = 64e-5
D_FF = 2816
N_EXPERTS = 8
TOP_K = 2
D_FF_EXPERT = 3584
NORM_EPS = 1e-6
NEG_INF = -1e30

LANES = 128
SUBLANES = 8
VMEM_LIMIT = 56 * 1024 * 1024

ZR_COLS = 1920
Z_POOL_OFF = ZR_COLS
Z_QLAT_OFF = Z_POOL_OFF + POOL_WIDTH
Z_KVLAT_OFF = Z_QLAT_OFF + MLA_Q_LORA
Z_KROPE_OFF = Z_KVLAT_OFF + MLA_KV_LORA
Z_COLS = Z_KROPE_OFF + LANES

TM_MIX = 512
WKV_CHUNK = 64
TQ = 512
ATTN_BOUND_SLACK = 1.02
ATTN_BOUND_MAX = 40.0
TM_FFN = 512
TF_FFN = 1408
MOE_BLOCK = 1024
TF_MOE = 896
SC_WINDOW = 128
ROW_CHUNKS = 4
ROW_CHUNK_W = D_MODEL // ROW_CHUNKS
PACK_CHUNK_W = ROW_CHUNK_W // 2

SEGSUM_SPLITS = 1

NN = (((1,), (0,)), ((), ()))
NT = (((1,), (1,)), ((), ()))


def _dot(a, b, dims=NN):
    return lax.dot_general(a, b, dims, preferred_element_type=F32)


def _split2(a):
    hi = a.astype(BF16)
    lo = (a - hi.astype(F32)).astype(BF16)
    return hi, lo


def _mm(a, b, dims=NN, passes=3):
    if passes == 1:
        return _dot(a.astype(BF16), b.astype(BF16), dims)
    ah, al = _split2(a)
    bh, bl = _split2(b)
    return _dot(ah, bh, dims) + (_dot(ah, bl, dims) + _dot(al, bh, dims))


def _mm_exact_rhs(a, b_bf16, dims=NN, splits=SEGSUM_SPLITS):
    out = None
    rem = a
    for s in range(splits):
        part = rem.astype(BF16)
        term = _dot(part, b_bf16, dims)
        out = term if out is None else out + term
        if s + 1 < splits:
            rem = rem - part.astype(F32)
    return out


def _pack_bf16_pairs(h):
    w = h.shape[1] // 2
    lo = lax.bitcast_convert_type(h[:, :w].astype(BF16).astype(F32), jnp.uint32)
    hi = lax.bitcast_convert_type(h[:, w:].astype(BF16).astype(F32), jnp.uint32)
    return (lo >> 16) | (hi & jnp.uint32(0xFFFF0000))


def _unpack_bf16_pairs(p):
    lo = lax.bitcast_convert_type(p << 16, F32)
    hi = lax.bitcast_convert_type(p & jnp.uint32(0xFFFF0000), F32)
    return jnp.concatenate([lo, hi], axis=1).astype(BF16)


def _sigmoid(x):
    return 1.0 / (1.0 + jnp.exp(-x))


def _silu(x):
    return x * _sigmoid(x)


def _rms(x, eps=NORM_EPS):
    return x * lax.rsqrt(jnp.mean(x * x, axis=-1, keepdims=True) + eps)


def _cparams(sem):
    return pltpu.CompilerParams(dimension_semantics=sem, vmem_limit_bytes=VMEM_LIMIT)


def _adaln_kernel(c_ref, w_ref, b_ref, o_ref):
    ca = _silu(c_ref[...])
    o_ref[0] = _mm(ca, w_ref[0]) + b_ref[0]


def _adaln(c, w_ada, b_ada):
    L = w_ada.shape[0]
    B = c.shape[0]
    n = w_ada.shape[2] // D_MODEL
    return pl.pallas_call(
        _adaln_kernel,
        grid=(L, n),
        in_specs=[pl.BlockSpec((B, D_MODEL), lambda l, j: (0, 0)),
                  pl.BlockSpec((1, D_MODEL, D_MODEL), lambda l, j: (l, 0, j)),
                  pl.BlockSpec((1, 1, D_MODEL), lambda l, j: (l, 0, j))],
        out_specs=pl.BlockSpec((1, B, D_MODEL), lambda l, j: (l, 0, j)),
        out_shape=jax.ShapeDtypeStruct((L, B, n * D_MODEL), F32),
        compiler_params=_cparams(("arbitrary", "arbitrary")),
    )(c, w_ada, b_ada.reshape(L, 1, -1))


def _rope(x, cosf, sinf, lane):
    up = pltpu.roll(x, LANES - MLA_QK_ROPE // 2, axis=1)
    dn = pltpu.roll(x, MLA_QK_ROPE // 2, axis=1)
    rot = jnp.where(lane < MLA_QK_NOPE + MLA_QK_ROPE // 2, -up, dn)
    return x * cosf + rot * sinf


def _mixin_kernel(x_ref, mod_ref, gain_ref, win_ref, pos_ref, freq_ref, poolw_ref, pools_ref,
                  qg_ref, kvg_ref, wq_ref, wk_ref, wv_ref, qkg_ref, bd_ref,
                  zr_ref, yp_ref, q_ref, k_ref, v_ref, nrm_ref, ubuf):
    i = pl.program_id(1)
    tm = x_ref.shape[1]
    x = x_ref[0]
    mod = mod_ref[0]
    h = _rms(x) * gain_ref[...] * (1.0 + mod[1:2]) + mod[0:1]
    z = _dot(h.astype(BF16), win_ref[...])
    zr_ref[0] = z[:, :ZR_COLS]

    @pl.when(i == 0)
    def _():
        ubuf[0:POOL_HALO, :] = jnp.zeros((POOL_HALO, POOL_WIDTH), F32)

    u = z[:, Z_POOL_OFF:Z_POOL_OFF + POOL_WIDTH]
    ubuf[POOL_HALO:, :] = u
    ue = ubuf[...]
    s2 = ue + pltpu.roll(ue, 1, axis=0)
    s4 = s2 + pltpu.roll(s2, 2, axis=0)
    s8 = s4 + pltpu.roll(s4, 4, axis=0)
    s16 = s8 + pltpu.roll(s8, 8, axis=0)
    ubuf[0:POOL_HALO, :] = u[tm - POOL_HALO:, :]
    lane_p = lax.broadcasted_iota(jnp.int32, (tm, POOL_WIDTH), 1)
    grp = lane_p // (POOL_WIDTH // len(POOL_WINDOWS))
    win_sum = jnp.where(grp == 0, s2[POOL_HALO:], jnp.where(grp == 1, s4[POOL_HALO:],
                        jnp.where(grp == 2, s8[POOL_HALO:], s16[POOL_HALO:])))
    win = jnp.where(grp == 0, 2, jnp.where(grp == 1, 4, jnp.where(grp == 2, 8, 16)))
    t_abs = i * tm + lax.broadcasted_iota(jnp.int32, (tm, POOL_WIDTH), 0)
    cnt = jnp.minimum(t_abs + 1, win).astype(F32)
    p = win_sum / cnt - u
    yp = _dot(p.astype(BF16), poolw_ref[...]) * pools_ref[...]
    yp_ref[0] = yp.astype(BF16)

    lane = lax.broadcasted_iota(jnp.int32, (tm, LANES), 1)
    in_rope = (lane >= MLA_QK_NOPE) & (lane < MLA_QK_DIM)
    ang = pos_ref[0].astype(F32) * freq_ref[...]
    cosf = jnp.where(in_rope, jnp.cos(ang), 1.0)
    sinf = jnp.where(in_rope, jnp.sin(ang), 0.0)

    q_lat = z[:, Z_QLAT_OFF:Z_QLAT_OFF + MLA_Q_LORA]
    kv_lat = z[:, Z_KVLAT_OFF:Z_KVLAT_OFF + MLA_KV_LORA]
    k_rope = z[:, Z_KROPE_OFF:Z_KROPE_OFF + LANES]
    qn = (_rms(q_lat) * qg_ref[...]).astype(BF16)
    kvn = (_rms(kv_lat) * kvg_ref[...]).astype(BF16)
    q = _dot(qn, wq_ref[...])
    kx = _dot(kvn, wk_ref[...])
    v = _dot(kvn, wv_ref[...])
    k_pe = _rope(k_rope, cosf, sinf, lane)
    qs, ks = [], []
    for hd in range(MLA_HEADS):
        sl = slice(hd * LANES, (hd + 1) * LANES)
        qs.append(_rope(q[:, sl], cosf, sinf, lane))
        ks.append(kx[:, sl] + k_pe)
    q = jnp.concatenate(qs, axis=1)
    k = jnp.concatenate(ks, axis=1)
    qss = _mm_exact_rhs(q * q, bd_ref[...]) * (1.0 / MLA_QK_DIM)
    kss = _mm_exact_rhs(k * k, bd_ref[...]) * (1.0 / MLA_QK_DIM)
    qkg = qkg_ref[...]
    q = q * lax.rsqrt(qss + NORM_EPS) * qkg[0:1] * (MLA_QK_DIM ** -0.5)
    k = k * lax.rsqrt(kss + NORM_EPS) * qkg[1:2]
    qb = q.astype(BF16)
    kb = k.astype(BF16)
    q_ref[0] = qb
    k_ref[0] = kb
    v_ref[0] = v.astype(BF16)
    qf = qb.astype(F32)
    kf = kb.astype(F32)
    qn2 = jnp.max(_mm_exact_rhs(qf * qf, bd_ref[...]), axis=0, keepdims=True)
    kn2 = jnp.max(_mm_exact_rhs(kf * kf, bd_ref[...]), axis=0, keepdims=True)
    nrm_ref[0, 0] = jnp.concatenate([qn2, kn2], axis=0)


def _mixin(x, mod_l, gain, win, pos3, freq, poolw, pools, qg, kvg, wq, wk, wv, qkg, bd128):
    B, T, _ = x.shape
    tm = TM_MIX
    const = lambda shape: pl.BlockSpec(shape, lambda b, i: tuple(0 for _ in shape))
    tok = lambda w: pl.BlockSpec((1, tm, w), lambda b, i: (b, i, 0))
    return pl.pallas_call(
        _mixin_kernel,
        grid=(B, T // tm),
        in_specs=[tok(D_MODEL),
                  pl.BlockSpec((1, 6, D_MODEL), lambda b, i: (b, 0, 0)),
                  const((1, D_MODEL)), const((D_MODEL, Z_COLS)),
                  tok(1), const((1, LANES)),
                  const((POOL_WIDTH, POOL_WIDTH)), const((1, POOL_WIDTH)),
                  const((1, MLA_Q_LORA)), const((1, MLA_KV_LORA)),
                  const((MLA_Q_LORA, MLA_HEADS * LANES)), const((MLA_KV_LORA, MLA_HEADS * LANES)),
                  const((MLA_KV_LORA, MLA_HEADS * HEAD_DIM)), const((2, MLA_HEADS * LANES)),
                  const((MLA_HEADS * LANES, MLA_HEADS * LANES))],
        out_specs=[tok(ZR_COLS), tok(POOL_WIDTH), tok(MLA_HEADS * LANES), tok(MLA_HEADS * LANES),
                   tok(MLA_HEADS * HEAD_DIM),
                   pl.BlockSpec((1, 1, 2, MLA_HEADS * LANES), lambda b, i: (b, i, 0, 0))],
        out_shape=[jax.ShapeDtypeStruct((B, T, ZR_COLS), F32),
                   jax.ShapeDtypeStruct((B, T, POOL_WIDTH), BF16),
                   jax.ShapeDtypeStruct((B, T, MLA_HEADS * LANES), BF16),
                   jax.ShapeDtypeStruct((B, T, MLA_HEADS * LANES), BF16),
                   jax.ShapeDtypeStruct((B, T, MLA_HEADS * HEAD_DIM), BF16),
                   jax.ShapeDtypeStruct((B, T // tm, 2, MLA_HEADS * LANES), F32)],
        scratch_shapes=[pltpu.VMEM((POOL_HALO + tm, POOL_WIDTH), F32)],
        compiler_params=_cparams(("arbitrary", "arbitrary")),
    )(x, mod_l, gain, win, pos3, freq, poolw, pools, qg, kvg, wq, wk, wv, qkg, bd128)


WKV_PASSES_SCORE = 1
WKV_PASSES_INV = 1
WKV_PASSES_APPLY = 1
WKV_PASSES_STATE = 1
WKV_STEP_CHUNKS = 4


def _stack_heads(xp, lane):
    return jnp.concatenate([jnp.where(lane < HEAD_DIM, xp, 0.0),
                            jnp.where(lane >= HEAD_DIM, xp, 0.0)], axis=0)


def _wkv_prep(r, lw, k, v, kk, a, tri, masks):
    L = r[0].shape[0]
    n = 2 * L
    nc = len(r)
    each = lambda f, *ls: [f(*xs) for xs in zip(*ls)]
    lane = lax.broadcasted_iota(jnp.int32, (L, LANES), 1)
    stack = lambda x: _stack_heads(x, lane)
    cum = each(lambda x: _mm_exact_rhs_left(tri, x), lw)
    cum_last = each(lambda c: c[L - 1:L, :], cum)
    e_w = each(jnp.exp, cum)
    e_wm = each(lambda c, x: jnp.exp(c - x), cum, lw)
    e_iw = each(lambda c: jnp.exp(-c), cum)
    e_d = each(lambda cl, c: jnp.exp(cl - c), cum_last, cum)
    beta = each(lambda x, y: x * y, kk, a)
    r_t = each(lambda x, e: stack(x * e), r, e_w)
    a_t = each(lambda x, e: stack(-x * e), kk, e_wm)
    b_t = each(lambda x, e: stack(x * e), beta, e_iw)
    k_t = each(lambda x, e: stack(x * e), k, e_iw)
    b_d = each(lambda x, e: stack(x * e), beta, e_d)
    k_d = each(lambda x, e: stack(x * e), k, e_d)
    v_s = each(stack, v)
    g = each(lambda at, rt, bt, kt: _mm(jnp.concatenate([at, rt], axis=0),
                                        jnp.concatenate([bt, kt], axis=0), NT, WKV_PASSES_SCORE),
             a_t, r_t, b_t, k_t)
    strict, incl, levels = masks
    a_ab = each(lambda x: jnp.where(strict, x[:n, :n], 0.0), g)
    a_ak = each(lambda x: jnp.where(strict, x[:n, n:], 0.0), g)
    s_rb = each(lambda x: jnp.where(incl, x[n:, :n], 0.0), g)
    s_rk = each(lambda x: jnp.where(incl, x[n:, n:], 0.0), g)
    eye = jnp.where(levels[0][1], 1.0, 0.0)
    tinv = each(lambda x: eye + jnp.where(levels[0][0], x, 0.0), a_ab)
    for lvl_mask, _ in levels[1:]:
        et = each(lambda x, t: _mm(jnp.where(lvl_mask, x, 0.0), t, NN, WKV_PASSES_INV), a_ab, tinv)
        tinv = each(lambda t, x: t + _mm(t, x, NN, WKV_PASSES_INV), tinv, et)
    av = each(lambda x, y: _mm(x, y, NN, WKV_PASSES_APPLY), a_ak, v_s)
    tx = each(lambda t, x, y: _mm(t, jnp.concatenate([x, y], axis=1), NN, WKV_PASSES_APPLY),
              tinv, a_t, av)
    ra = each(lambda rt, s, x: rt + _mm(s, x[:, :LANES], NN, WKV_PASSES_APPLY), r_t, s_rb, tx)
    c2 = each(lambda sb, sk, x, vs: _mm(jnp.concatenate([sb, sk], axis=1),
                                        jnp.concatenate([x[:, LANES:], vs], axis=0),
                                        NN, WKV_PASSES_APPLY), s_rb, s_rk, tx, v_s)
    tb = each(lambda x, bd: _mm(x.T, bd, NN, WKV_PASSES_APPLY), tx, b_d)
    c3 = each(lambda x, vs, kd: x[LANES:] + _mm(vs.T, kd, NN, WKV_PASSES_APPLY), tb, v_s, k_d)
    return [(ra[i], c2[i], jnp.exp(cum_last[i]), tb[i][:LANES], c3[i]) for i in range(nc)]


def _mm_exact_rhs_left(tri_bf16, x):
    x0 = x.astype(BF16)
    r1 = x - x0.astype(F32)
    x1 = r1.astype(BF16)
    x2 = (r1 - x1.astype(F32)).astype(BF16)
    return _dot(tri_bf16, x0) + (_dot(tri_bf16, x1) + _dot(tri_bf16, x2))


def _wkv_masks(L):
    n = 2 * L
    row = lax.broadcasted_iota(jnp.int32, (n, n), 0)
    col = lax.broadcasted_iota(jnp.int32, (n, n), 1)
    strict = row > col
    incl = row >= col
    levels = []
    m = 1
    while m < L:
        same = (row // (2 * m)) == (col // (2 * m))
        lvl = same & ((row % (2 * m)) >= m) & ((col % (2 * m)) < m)
        levels.append((lvl, row == col))
        m *= 2
    return strict, incl, levels


def _rwkv_kernel(has_vres, *refs):
    if has_vres:
        (z_ref, vf_ref, mu_ref, vec_ref, w2a2_ref, g2v2_ref, bd_ref,
         y_ref, carry, state) = refs
    else:
        (z_ref, mu_ref, vec_ref, w2a2_ref, g2v2_ref, bd_ref,
         y_ref, vout_ref, carry, state) = refs
    c = pl.program_id(1)
    rows = z_ref.shape[1]
    L = WKV_CHUNK
    W = RWKV_WIDTH

    @pl.when(c == 0)
    def _():
        carry[...] = jnp.zeros(carry.shape, F32)
        state[...] = jnp.zeros(state.shape, F32)

    z = z_ref[0]
    row = lax.broadcasted_iota(jnp.int32, z.shape, 0)
    prev = jnp.where(row == 0, carry[SUBLANES - 1:SUBLANES, :], pltpu.roll(z, 1, axis=0))
    carry[...] = z[rows - SUBLANES:, :]
    zs = z + mu_ref[...] * (prev - z)
    r = zs[:, 0:W]
    k = zs[:, W:2 * W]
    v = zs[:, 2 * W:3 * W]
    wa = zs[:, 3 * W:3 * W + LANES]
    gb = zs[:, 3 * W + LANES:ZR_COLS]
    vec = vec_ref[...]
    w0, a0, k_k, k_a, r_k, ln_g, ln_b, v0 = (vec[j:j + 1] for j in range(8))

    lane_a = lax.broadcasted_iota(jnp.int32, wa.shape, 1)
    t1 = _dot(jnp.where(lane_a < RWKV_DECAY_LORA, jnp.tanh(wa), wa).astype(BF16), w2a2_ref[...])
    lane_g = lax.broadcasted_iota(jnp.int32, gb.shape, 1)
    t2 = _dot(jnp.where(lane_g < RWKV_GATE_LORA, _sigmoid(gb), gb).astype(BF16), g2v2_ref[...])
    xw = w0 + t1[:, :W]
    w_log = -(jnp.maximum(-xw, 0.0) + jnp.log1p(jnp.exp(-jnp.abs(xw)))) - 0.5
    lw = -jnp.exp(w_log)
    a = _sigmoid(a0 + t1[:, W:])
    g = t2[:, :W]
    if has_vres:
        v = v + (vf_ref[0] - v) * _sigmoid(v0 + t2[:, W:])
    else:
        vout_ref[0] = v
    kk = k * k_k
    bd = bd_ref[...]
    nrm = jnp.sqrt(_mm_exact_rhs(kk * kk, bd))
    kk = kk / jnp.maximum(nrm, 1e-12)
    k = k * (1.0 + (a - 1.0) * k_a)

    masks = _wkv_masks(L)
    rowt = lax.broadcasted_iota(jnp.int32, (L, L), 0)
    colt = lax.broadcasted_iota(jnp.int32, (L, L), 1)
    tri = jnp.where(rowt >= colt, 1.0, 0.0).astype(BF16)
    n_pairs = RWKV_HEADS // 2
    n_chunks = rows // L
    idx = [(ch, p) for ch in range(n_chunks) for p in range(n_pairs)]
    cut = lambda x: [x[ch * L:(ch + 1) * L, p * LANES:(p + 1) * LANES] for ch, p in idx]
    prep = dict(zip(idx, _wkv_prep(cut(r), cut(lw), cut(k), cut(v), cut(kk), cut(a), tri, masks)))
    S = [state[p] for p in range(n_pairs)]
    y_rows = []
    for ch in range(n_chunks):
        y_s = [_mm(prep[ch, p][0], S[p], NT, WKV_PASSES_STATE) + prep[ch, p][1] for p in range(n_pairs)]
        y_rows.append(jnp.concatenate([x[:L] + x[L:] for x in y_s], axis=1))
        S = [S[p] * prep[ch, p][2] + _mm(S[p], prep[ch, p][3], NN, WKV_PASSES_STATE) + prep[ch, p][4]
             for p in range(n_pairs)]
    for p in range(n_pairs):
        state[p] = S[p]
    y = jnp.concatenate(y_rows, axis=0)

    inv = 1.0 / HEAD_DIM
    mean = _mm_exact_rhs(y, bd) * inv
    yc = y - mean
    var = _mm_exact_rhs(yc * yc, bd) * inv
    yn = yc * lax.rsqrt(var + RWKV_LNX_EPS) * ln_g + ln_b
    bonus = _mm_exact_rhs(r * k * r_k, bd) * v
    y_ref[0] = ((yn + bonus) * g).astype(BF16)


def _rwkv(zr, v_first, mu, vec8, w2a2, g2v2, bd64):
    B, T, _ = zr.shape
    L = WKV_CHUNK * WKV_STEP_CHUNKS
    has_vres = v_first is not None
    const = lambda shape: pl.BlockSpec(shape, lambda b, c: tuple(0 for _ in shape))
    tok = lambda w: pl.BlockSpec((1, L, w), lambda b, c: (b, c, 0))
    in_specs = [tok(ZR_COLS)]
    args = [zr]
    if has_vres:
        in_specs.append(tok(RWKV_WIDTH))
        args.append(v_first)
    in_specs += [const((1, ZR_COLS)), const((8, RWKV_WIDTH)), const((LANES, 2 * RWKV_WIDTH)),
                 const((2 * LANES, 2 * RWKV_WIDTH)), const((RWKV_WIDTH, RWKV_WIDTH))]
    args += [mu, vec8, w2a2, g2v2, bd64]
    out_specs = [tok(RWKV_WIDTH)]
    out_shape = [jax.ShapeDtypeStruct((B, T, RWKV_WIDTH), BF16)]
    if not has_vres:
        out_specs.append(tok(RWKV_WIDTH))
        out_shape.append(jax.ShapeDtypeStruct((B, T, RWKV_WIDTH), F32))
    outs = pl.pallas_call(
        functools.partial(_rwkv_kernel, has_vres),
        grid=(B, T // L),
        in_specs=in_specs, out_specs=out_specs, out_shape=out_shape,
        scratch_shapes=[pltpu.VMEM((SUBLANES, ZR_COLS), F32),
                        pltpu.VMEM((RWKV_HEADS // 2, LANES, LANES), F32)],
        compiler_params=_cparams(("arbitrary", "arbitrary")),
    )(*args)
    return (outs[0], v_first) if has_vres else (outs[0], outs[1])


def _attn_step(q_ref, k_ref, v_ref, m_sc, l_sc, acc_sc, masked):
    tq = q_ref.shape[1]
    tk = k_ref.shape[1]
    lane = lax.broadcasted_iota(jnp.int32, (tq, LANES), 1)
    if masked:
        rowi = lax.broadcasted_iota(jnp.int32, (tq, tk), 0)
        coli = lax.broadcasted_iota(jnp.int32, (tq, tk), 1)
        keep = coli <= rowi
    heads = range(MLA_HEADS)
    s = [_dot(q_ref[0, :, hd * LANES:(hd + 1) * LANES], k_ref[0, :, hd * LANES:(hd + 1) * LANES], NT)
         for hd in heads]
    if masked:
        s = [jnp.where(keep, x, NEG_INF) for x in s]
    m_prev = [m_sc[hd] for hd in heads]
    m_new = [jnp.maximum(m_prev[hd], jnp.max(s[hd], axis=-1, keepdims=True)) for hd in heads]
    alpha = [jnp.exp(m_prev[hd] - m_new[hd]) for hd in heads]
    p = [jnp.exp(s[hd] - jnp.concatenate([m_new[hd]] * (tk // LANES), axis=1)) for hd in heads]
    for hd in heads:
        l_sc[hd] = alpha[hd] * l_sc[hd] + jnp.sum(p[hd], axis=-1, keepdims=True)
        m_sc[hd] = m_new[hd]
    pv = [_dot(p[hd].astype(BF16), v_ref[0, :, (hd // 2) * LANES:(hd // 2 + 1) * LANES]) for hd in heads]
    first = lane < HEAD_DIM
    for pr in range(MLA_HEADS // 2):
        acc_sc[pr] = (acc_sc[pr] * jnp.where(first, alpha[2 * pr], alpha[2 * pr + 1])
                      + jnp.where(first, pv[2 * pr], pv[2 * pr + 1]))


def _attn_step_bounded(q_ref, k_ref, v_ref, cb_ref, l_sc, acc_sc, masked):
    tq = q_ref.shape[1]
    tk = k_ref.shape[1]
    lane = lax.broadcasted_iota(jnp.int32, (tq, LANES), 1)
    if masked:
        rowi = lax.broadcasted_iota(jnp.int32, (tq, tk), 0)
        coli = lax.broadcasted_iota(jnp.int32, (tq, tk), 1)
        keep = coli <= rowi
    pv = []
    for hd in range(MLA_HEADS):
        s = _dot(q_ref[0, :, hd * LANES:(hd + 1) * LANES], k_ref[0, :, hd * LANES:(hd + 1) * LANES], NT)
        c = cb_ref[0, hd:hd + 1, :]
        p = jnp.exp(s - jnp.concatenate([c] * (tk // LANES), axis=1))
        if masked:
            p = jnp.where(keep, p, 0.0)
        part = p[:, 0:LANES]
        for t in range(1, tk // LANES):
            part = part + p[:, t * LANES:(t + 1) * LANES]
        l_sc[hd] = l_sc[hd] + part
        pv.append(_dot(p.astype(BF16), v_ref[0, :, (hd // 2) * LANES:(hd // 2 + 1) * LANES]))
    first = lane < HEAD_DIM
    for pr in range(MLA_HEADS // 2):
        acc_sc[pr] = acc_sc[pr] + jnp.where(first, pv[2 * pr], pv[2 * pr + 1])


def _attn_finish(o_ref, l_sc, acc_sc, lane_partial):
    tq = o_ref.shape[1]
    lane = lax.broadcasted_iota(jnp.int32, (tq, LANES), 1)
    outs = []
    for pr in range(MLA_HEADS // 2):
        la, lb = l_sc[2 * pr], l_sc[2 * pr + 1]
        if lane_partial:
            la = jnp.sum(la, axis=-1, keepdims=True)
            lb = jnp.sum(lb, axis=-1, keepdims=True)
        outs.append(acc_sc[pr] / jnp.where(lane < HEAD_DIM, la, lb))
    o_ref[0] = jnp.concatenate(outs, axis=1).astype(BF16)


def _attn_kernel(ok_ref, qi_ref, kj_ref, q_ref, k_ref, v_ref, cb_ref, o_ref, m_sc, l_sc, acc_sc):
    i = qi_ref[pl.program_id(1)]
    j = kj_ref[pl.program_id(1)]
    bounded = ok_ref[pl.program_id(0)] == 1
    exact = jnp.logical_not(bounded)

    @pl.when(j == 0)
    def _():
        m_sc[...] = jnp.full(m_sc.shape, NEG_INF, F32)
        l_sc[...] = jnp.zeros(l_sc.shape, F32)
        acc_sc[...] = jnp.zeros(acc_sc.shape, F32)

    @pl.when(bounded & (j < i))
    def _():
        _attn_step_bounded(q_ref, k_ref, v_ref, cb_ref, l_sc, acc_sc, masked=False)

    @pl.when(bounded & (j == i))
    def _():
        _attn_step_bounded(q_ref, k_ref, v_ref, cb_ref, l_sc, acc_sc, masked=True)
        _attn_finish(o_ref, l_sc, acc_sc, lane_partial=True)

    @pl.when(exact & (j < i))
    def _():
        _attn_step(q_ref, k_ref, v_ref, m_sc, l_sc, acc_sc, masked=False)

    @pl.when(exact & (j == i))
    def _():
        _attn_step(q_ref, k_ref, v_ref, m_sc, l_sc, acc_sc, masked=True)
        _attn_finish(o_ref, l_sc, acc_sc, lane_partial=False)


def _attention(q, k, v, norms):
    B, T, _ = q.shape
    nq = T // TQ
    mx = jnp.max(norms, axis=1).reshape(B, 2, MLA_HEADS, LANES)[..., 0]
    cb = jnp.sqrt(mx[:, 0] * mx[:, 1]) * ATTN_BOUND_SLACK
    ok = (jnp.max(cb, axis=1) <= ATTN_BOUND_MAX).astype(jnp.int32)
    cb = jnp.broadcast_to(cb[:, :, None], (B, MLA_HEADS, LANES))
    pairs = [(i, j) for i in range(nq) for j in range(i + 1)]
    qi = jnp.asarray([p[0] for p in pairs], jnp.int32)
    kj = jnp.asarray([p[1] for p in pairs], jnp.int32)
    grid_spec = pltpu.PrefetchScalarGridSpec(
        num_scalar_prefetch=3,
        grid=(B, len(pairs)),
        in_specs=[pl.BlockSpec((1, TQ, MLA_HEADS * LANES), lambda b, t, ok, qi, kj: (b, qi[t], 0)),
                  pl.BlockSpec((1, TQ, MLA_HEADS * LANES), lambda b, t, ok, qi, kj: (b, kj[t], 0)),
                  pl.BlockSpec((1, TQ, MLA_HEADS * HEAD_DIM), lambda b, t, ok, qi, kj: (b, kj[t], 0)),
                  pl.BlockSpec((1, MLA_HEADS, LANES), lambda b, t, ok, qi, kj: (b, 0, 0))],
        out_specs=pl.BlockSpec((1, TQ, MLA_HEADS * HEAD_DIM), lambda b, t, ok, qi, kj: (b, qi[t], 0)),
        scratch_shapes=[pltpu.VMEM((MLA_HEADS, TQ, LANES), F32),
                        pltpu.VMEM((MLA_HEADS, TQ, LANES), F32),
                        pltpu.VMEM((MLA_HEADS // 2, TQ, LANES), F32)])
    return pl.pallas_call(
        _attn_kernel,
        grid_spec=grid_spec,
        out_shape=jax.ShapeDtypeStruct((B, T, MLA_HEADS * HEAD_DIM), BF16),
        compiler_params=_cparams(("arbitrary", "arbitrary")),
    )(ok, qi, kj, q, k, v, cb)


def _mixout_kernel(has_router, *refs):
    if has_router:
        (x_ref, yr_ref, yp_ref, ym_ref, wo_ref, mod_ref, gain_ref, rt_ref, tri_ref,
         xo_ref, h_ref, route_ref, cnt_ref, cnt_sc) = refs
    else:
        x_ref, yr_ref, yp_ref, ym_ref, wo_ref, mod_ref, gain_ref, xo_ref, h_ref = refs
    mod = mod_ref[0]
    o1 = RWKV_WIDTH
    o2 = RWKV_WIDTH + POOL_WIDTH
    mix = (_dot(yr_ref[0], wo_ref[0:o1, :]) + _dot(yp_ref[0], wo_ref[o1:o2, :])
           + _dot(ym_ref[0], wo_ref[o2:, :]))
    x = x_ref[0] + mod[2:3] * mix
    xo_ref[0] = x
    h = _rms(x) * gain_ref[...] * (1.0 + mod[4:5]) + mod[3:4]
    if not has_router:
        h_ref[0] = h.astype(BF16)
    else:
        hp = _pack_bf16_pairs(h)
        for ck in range(ROW_CHUNKS):
            h_ref[ck, 0] = hp[:, ck * PACK_CHUNK_W:(ck + 1) * PACK_CHUNK_W]
        logits = _mm(h, rt_ref[...])
        lane = lax.broadcasted_iota(jnp.int32, logits.shape, 1).astype(F32)
        lg = jnp.where(lane < N_EXPERTS, logits, -jnp.inf)
        m1 = jnp.max(lg, axis=-1, keepdims=True)
        i1 = jnp.min(jnp.where(lg == m1, lane, float(LANES)), axis=-1, keepdims=True)
        lg2 = jnp.where(lane == i1, -jnp.inf, lg)
        m2 = jnp.max(lg2, axis=-1, keepdims=True)
        i2 = jnp.min(jnp.where(lg2 == m2, lane, float(LANES)), axis=-1, keepdims=True)
        e2 = jnp.exp(m2 - m1)
        g1 = 1.0 / (1.0 + e2)
        g2 = e2 / (1.0 + e2)
        first = (pl.program_id(0) == 0) & (pl.program_id(1) == 0)

        @pl.when(first)
        def _():
            cnt_sc[...] = jnp.zeros(cnt_sc.shape, F32)

        hit1 = lane == i1
        hit2 = lane == i2
        onehot = jnp.where(hit1 | hit2, 1.0, 0.0)
        prefix = _dot(tri_ref[...], onehot.astype(BF16)) + cnt_sc[0:1, :]
        r1 = jnp.sum(jnp.where(hit1, prefix, 0.0), axis=-1, keepdims=True)
        r2 = jnp.sum(jnp.where(hit2, prefix, 0.0), axis=-1, keepdims=True)
        cnt_sc[...] = cnt_sc[...] + jnp.sum(onehot, axis=0, keepdims=True)
        cnt_ref[...] = cnt_sc[...]
        vals = (i1, i2, g1, g2, r1, r2)
        route = jnp.zeros(logits.shape, F32)
        for pos, val in enumerate(vals):
            route = jnp.where(lane == pos, val, route)
        route_ref[0] = route


def _mixout(x, yr, yp, ym, wo, mod_l, gain, router_p):
    B, T, _ = x.shape
    tm = TM_MIX
    has_router = router_p is not None
    const = lambda shape: pl.BlockSpec(shape, lambda b, i: tuple(0 for _ in shape))
    tok = lambda w: pl.BlockSpec((1, tm, w), lambda b, i: (b, i, 0))
    in_specs = [tok(D_MODEL), tok(RWKV_WIDTH), tok(POOL_WIDTH), tok(MLA_HEADS * HEAD_DIM),
                const((D_MODEL, D_MODEL)), pl.BlockSpec((1, 6, D_MODEL), lambda b, i: (b, 0, 0)),
                const((1, D_MODEL))]
    args = [x, yr, yp, ym, wo, mod_l, gain]
    out_specs = [tok(D_MODEL), tok(D_MODEL)]
    out_shape = [jax.ShapeDtypeStruct((B, T, D_MODEL), F32), jax.ShapeDtypeStruct((B, T, D_MODEL), BF16)]
    scratch = []
    if has_router:
        out_specs[1] = pl.BlockSpec((ROW_CHUNKS, 1, tm, PACK_CHUNK_W), lambda b, i: (0, b, i, 0))
        out_shape[1] = jax.ShapeDtypeStruct((ROW_CHUNKS, B, T, PACK_CHUNK_W), jnp.uint32)
        ids = np.arange(tm)
        tri = jnp.asarray(ids[:, None] > ids[None, :], BF16)
        in_specs += [const((D_MODEL, LANES)), const((tm, tm))]
        args += [router_p, tri]
        out_specs += [tok(LANES), const((SUBLANES, LANES))]
        out_shape += [jax.ShapeDtypeStruct((B, T, LANES), F32),
                      jax.ShapeDtypeStruct((SUBLANES, LANES), F32)]
        scratch = [pltpu.VMEM((SUBLANES, LANES), F32)]
    return pl.pallas_call(
        functools.partial(_mixout_kernel, has_router),
        grid=(B, T // tm),
        in_specs=in_specs, out_specs=out_specs, out_shape=out_shape, scratch_shapes=scratch,
        compiler_params=_cparams(("arbitrary", "arbitrary")),
    )(*args)


def _ffn_kernel(h_ref, wg_ref, wu_ref, wo_ref, x_ref, mod_ref, o_ref, acc):
    j = pl.program_id(1)

    @pl.when(j == 0)
    def _():
        acc[...] = jnp.zeros(acc.shape, F32)

    h = h_ref[...]
    gg = _dot(h, wg_ref[...])
    uu = _dot(h, wu_ref[...])
    acc[...] += _dot((_silu(gg) * uu).astype(BF16), wo_ref[...])

    @pl.when(j == pl.num_programs(1) - 1)
    def _():
        o_ref[...] = x_ref[...] + mod_ref[0][5:6] * acc[...]


def _ffn(h2, w_in, w_out, x, mod_l):
    N = h2.shape[0]
    T = N // mod_l.shape[0]
    tm, tf = TM_FFN, TF_FFN
    nf = D_FF // tf
    per_b = T // tm
    return pl.pallas_call(
        _ffn_kernel,
        grid=(N // tm, nf),
        in_specs=[pl.BlockSpec((tm, D_MODEL), lambda i, j: (i, 0)),
                  pl.BlockSpec((D_MODEL, tf), lambda i, j: (0, j)),
                  pl.BlockSpec((D_MODEL, tf), lambda i, j: (0, j + nf)),
                  pl.BlockSpec((tf, D_MODEL), lambda i, j: (j, 0)),
                  pl.BlockSpec((tm, D_MODEL), lambda i, j: (i, 0)),
                  pl.BlockSpec((1, 6, D_MODEL), lambda i, j: (i // per_b, 0, 0))],
        out_specs=pl.BlockSpec((tm, D_MODEL), lambda i, j: (i, 0)),
        out_shape=jax.ShapeDtypeStruct((N, D_MODEL), F32),
        scratch_shapes=[pltpu.VMEM((tm, D_MODEL), F32)],
        compiler_params=_cparams(("arbitrary", "arbitrary")),
    )(h2, w_in, w_in, w_out, x, mod_l)


def _moe_kernel(be_ref, nv_ref, last_ref, x_ref, wg_ref, wu_ref, wo_ref, o_ref, acc, xm):
    i = pl.program_id(0)
    j = pl.program_id(1)

    @pl.when(i <= last_ref[0])
    def _():
        @pl.when(j == 0)
        def _():
            acc[...] = jnp.zeros(acc.shape, F32)
            row = lax.broadcasted_iota(jnp.int32, (xm.shape[0], 1), 0)
            xp = jnp.concatenate([x_ref[ck] for ck in range(ROW_CHUNKS)], axis=1)
            xp = jnp.where(row < nv_ref[i], xp, jnp.uint32(0))
            xm[...] = _unpack_bf16_pairs(xp)

        x = xm[...]
        gg = _dot(x, wg_ref[0].astype(BF16))
        uu = _dot(x, wu_ref[0].astype(BF16))
        acc[...] += _dot((_silu(gg) * uu).astype(BF16), wo_ref[0].astype(BF16))

        @pl.when(j == pl.num_programs(1) - 1)
        def _():
            for ck in range(ROW_CHUNKS):
                o_ref[ck] = acc[:, ck * ROW_CHUNK_W:(ck + 1) * ROW_CHUNK_W]


def _moe_experts(xs, w_in, w_out, block_exp, n_valid, last_blk):
    n_rows = xs.shape[1]
    tm, tf = MOE_BLOCK, TF_MOE
    nf = D_FF_EXPERT // tf
    blk = lambda i, last: jnp.minimum(i, last[0])
    chunk = lambda i, j, last: jnp.where(i <= last[0], j, nf - 1)
    grid_spec = pltpu.PrefetchScalarGridSpec(
        num_scalar_prefetch=3,
        grid=(n_rows // tm, nf),
        in_specs=[pl.BlockSpec((ROW_CHUNKS, tm, PACK_CHUNK_W),
                               lambda i, j, be, nv, last: (0, blk(i, last), 0)),
                  pl.BlockSpec((1, D_MODEL, tf),
                               lambda i, j, be, nv, last: (be[blk(i, last)], 0, chunk(i, j, last))),
                  pl.BlockSpec((1, D_MODEL, tf),
                               lambda i, j, be, nv, last: (be[blk(i, last)], 0, chunk(i, j, last) + nf)),
                  pl.BlockSpec((1, tf, D_MODEL),
                               lambda i, j, be, nv, last: (be[blk(i, last)], chunk(i, j, last), 0))],
        out_specs=pl.BlockSpec((ROW_CHUNKS, tm, ROW_CHUNK_W), lambda i, j, be, nv, last: (0, blk(i, last), 0)),
        scratch_shapes=[pltpu.VMEM((tm, D_MODEL), F32), pltpu.VMEM((tm, D_MODEL), BF16)])
    return pl.pallas_call(
        _moe_kernel,
        grid_spec=grid_spec,
        out_shape=jax.ShapeDtypeStruct((ROW_CHUNKS, n_rows, ROW_CHUNK_W), F32),
        compiler_params=_cparams(("arbitrary", "arbitrary")),
    )(block_exp, n_valid, last_blk, xs, w_in, w_in, w_out)


def _sc_mesh():
    return plsc.VectorSubcoreMesh(core_axis_name="c", subcore_axis_name="s")


def _sc_scatter_rows(x, dest, n_rows):
    N, D = x.shape
    K = dest.shape[0]
    win = SC_WINDOW

    @pl.kernel(out_type=jax.ShapeDtypeStruct((n_rows, D), x.dtype), mesh=_sc_mesh(), scratch_types=[])
    def scatter(x_hbm, d_hbm, o_hbm):
        def body(x_vmem, *idx_vmem):
            for iv in idx_vmem:
                pltpu.sync_copy(x_vmem, o_hbm.at[iv.at[0]])

        pltpu.emit_pipeline(
            body,
            grid=(N // win,),
            in_specs=[pl.BlockSpec((win, D), lambda i: (i, 0))]
            + [pl.BlockSpec((1, win), functools.partial(lambda k, i: (k, i), k)) for k in range(K)],
            out_specs=[],
            core_axis_name=("c", "s"),
            dimension_semantics=(pltpu.PARALLEL,),
        )(x_hbm, *([d_hbm] * K))

    return scatter(x, dest)


def _sc_gather_rows(x, idx):
    n = idx.shape[0]
    D = x.shape[1]
    win = SC_WINDOW

    @pl.kernel(out_type=jax.ShapeDtypeStruct((n, D), x.dtype), mesh=_sc_mesh(), scratch_types=[])
    def gather(x_hbm, i_hbm, o_hbm):
        def body(i_vmem, o_vmem):
            pltpu.sync_copy(x_hbm.at[i_vmem.at[0]], o_vmem)

        pltpu.emit_pipeline(
            body,
            grid=(n // win,),
            in_specs=[pl.BlockSpec((1, win), lambda i: (0, i))],
            out_specs=[pl.BlockSpec((win, D), lambda i: (i, 0))],
            core_axis_name=("c", "s"),
            dimension_semantics=(pltpu.PARALLEL,),
        )(i_hbm, o_hbm)

    return gather(x, idx.reshape(1, n))


def _combine_kernel(x_ref, ya_ref, yb_ref, route_ref, mod_ref, o_ref):
    rt = route_ref[...]
    ya = jnp.concatenate([ya_ref[0, ck] for ck in range(ROW_CHUNKS)], axis=1)
    yb = jnp.concatenate([yb_ref[0, ck] for ck in range(ROW_CHUNKS)], axis=1)
    f = rt[:, 2:3] * ya + rt[:, 3:4] * yb
    o_ref[...] = x_ref[...] + mod_ref[0][5:6] * f


def _combine(x, y2, route, mod_l):
    N = x.shape[0]
    T = N // mod_l.shape[0]
    tm = 1024
    per_b = T // tm
    tok = pl.BlockSpec((tm, D_MODEL), lambda i: (i, 0))
    slot = lambda k: pl.BlockSpec((1, ROW_CHUNKS, tm, ROW_CHUNK_W), lambda i: (k, 0, i, 0))
    return pl.pallas_call(
        _combine_kernel,
        grid=(N // tm,),
        in_specs=[tok, slot(0), slot(1),
                  pl.BlockSpec((tm, LANES), lambda i: (i, 0)),
                  pl.BlockSpec((1, 6, D_MODEL), lambda i: (i // per_b, 0, 0))],
        out_specs=tok,
        out_shape=jax.ShapeDtypeStruct((N, D_MODEL), F32),
        compiler_params=_cparams(("arbitrary",)),
    )(x, y2, y2, route, mod_l)


def _moe(h2, route, counts, w_in, w_out, x, mod_l):
    N = x.shape[0]
    blk = MOE_BLOCK
    cnt = counts[0, :N_EXPERTS].astype(jnp.int32)
    padded = (cnt + blk - 1) // blk * blk
    pend = jnp.cumsum(padded)
    pstart = pend - padded
    e = route[:, 0:TOP_K].astype(jnp.int32)
    rank = route[:, 2 * TOP_K:3 * TOP_K].astype(jnp.int32)
    dest = (jnp.take(pstart, e) + rank).T
    n_blocks = N * TOP_K // blk + N_EXPERTS
    bstart = jnp.arange(n_blocks, dtype=jnp.int32) * blk
    block_exp = jnp.clip(jnp.searchsorted(pend, bstart, side='right'), 0, N_EXPERTS - 1).astype(jnp.int32)
    n_valid = jnp.clip(cnt[block_exp] - (bstart - pstart[block_exp]), 0, blk).astype(jnp.int32)
    n_rows = n_blocks * blk
    dest_ck = dest[:, None, :] + (jnp.arange(ROW_CHUNKS, dtype=jnp.int32) * n_rows)[None, :, None]
    xs = _sc_scatter_rows(h2, dest_ck.reshape(TOP_K, ROW_CHUNKS * N), ROW_CHUNKS * n_rows)
    last_blk = (pend[-1:] // blk - 1).astype(jnp.int32)
    yb = _moe_experts(xs.reshape(ROW_CHUNKS, n_rows, PACK_CHUNK_W), w_in, w_out, block_exp, n_valid,
                      last_blk)
    y2 = _sc_gather_rows(yb.reshape(ROW_CHUNKS * n_rows, ROW_CHUNK_W), dest_ck.reshape(-1))
    return _combine(x, y2.reshape(TOP_K, ROW_CHUNKS, N, ROW_CHUNK_W), route, mod_l)


def _layout_w_in(w, has_vres):
    W = RWKV_WIDTH
    off_gd = 3 * W + RWKV_DECAY_LORA + RWKV_ICLR_LORA
    off_pool = off_gd + RWKV_GATE_LORA
    off_q = off_pool + POOL_WIDTH
    off_kv = off_q + MLA_Q_LORA
    off_kr = off_kv + MLA_KV_LORA
    n_base = off_kr + MLA_QK_ROPE
    d = w.shape[0]
    zeros = lambda n: jnp.zeros((d, n), w.dtype)
    vd = w[:, n_base:n_base + RWKV_VRES_LORA] if has_vres else zeros(RWKV_VRES_LORA)
    cols = [w[:, :off_gd], w[:, off_gd:off_pool], vd, zeros(ZR_COLS - off_pool - RWKV_VRES_LORA),
            w[:, off_pool:off_q], w[:, off_q:off_kv], w[:, off_kv:off_kr],
            zeros(MLA_QK_NOPE), w[:, off_kr:n_base], zeros(LANES - MLA_QK_DIM)]
    return jnp.concatenate(cols, axis=1).astype(BF16)


def _pad_heads(w, per_head, keep_from, keep_n):
    K = w.shape[0]
    wh = w.reshape(K, MLA_HEADS, per_head)[:, :, keep_from:keep_from + keep_n]
    wh = jnp.pad(wh, ((0, 0), (0, 0), (0, LANES - keep_n)))
    return wh.reshape(K, MLA_HEADS * LANES)


def kernel(x, c, positions, w_ada, b_ada, norm_gain, w_in_first, w_in_rest, mu_shift, mu_shift_v,
           rwkv_vec, rwkv_v0, rwkv_w2, rwkv_a2, rwkv_g2, rwkv_v2, pool_w, pool_scale,
           mla_q_lat_gain, mla_kv_lat_gain, mla_wq_up, mla_wkv_up, mla_qk_gain, w_out, ffn_w_in,
           ffn_w_out, moe_router, moe_w_in, moe_w_out):
    B, T, D = x.shape
    depth = w_ada.shape[0]
    W = RWKV_WIDTH
    mod = _adaln(c, w_ada, b_ada).reshape(depth, B, 6, D)
    pos3 = positions.reshape(B, T, 1)
    inv_freq = ROPE_BASE ** (-jnp.arange(0, MLA_QK_ROPE, 2, dtype=F32) / MLA_QK_ROPE)
    freq = jnp.concatenate([jnp.zeros((MLA_QK_NOPE,), F32), inv_freq, inv_freq,
                            jnp.zeros((LANES - MLA_QK_DIM,), F32)]).reshape(1, LANES)
    hid = np.arange(W) // HEAD_DIM
    bd64 = jnp.asarray(hid[:, None] == hid[None, :], BF16)
    bid = np.arange(MLA_HEADS * LANES) // LANES
    bd128 = jnp.asarray(bid[:, None] == bid[None, :], BF16)

    v_first = None
    for l in range(depth):
        has_vres = l > 0
        mod_l = mod[l]
        win = _layout_w_in(w_in_first if l == 0 else w_in_rest[l - 1], has_vres)
        poolw = jax.scipy.linalg.block_diag(*[pool_w[l, g] for g in range(len(POOL_WINDOWS))]).astype(BF16)
        wq = _pad_heads(mla_wq_up[l], MLA_QK_DIM, 0, MLA_QK_DIM).astype(BF16)
        wk = _pad_heads(mla_wkv_up[l], MLA_QK_NOPE + HEAD_DIM, 0, MLA_QK_NOPE).astype(BF16)
        wv = mla_wkv_up[l].reshape(MLA_KV_LORA, MLA_HEADS, MLA_QK_NOPE + HEAD_DIM)[:, :, MLA_QK_NOPE:]
        wv = wv.reshape(MLA_KV_LORA, MLA_HEADS * HEAD_DIM).astype(BF16)
        qkg = jnp.tile(jnp.pad(mla_qk_gain[l], ((0, 0), (0, LANES - MLA_QK_DIM))), (1, MLA_HEADS))
        zr, y_pool, q, k, v, qk_norms = _mixin(
            x, mod_l, norm_gain[l, 0].reshape(1, D), win, pos3, freq, poolw,
            pool_scale[l].reshape(1, -1), mla_q_lat_gain[l].reshape(1, -1),
            mla_kv_lat_gain[l].reshape(1, -1), wq, wk, wv, qkg, bd128)

        pad_mu = ZR_COLS - mu_shift.shape[1] - RWKV_VRES_LORA
        mu_v = mu_shift_v[l - 1] if has_vres else jnp.zeros((RWKV_VRES_LORA,), F32)
        mu = jnp.concatenate([mu_shift[l], mu_v, jnp.zeros((pad_mu,), F32)]).reshape(1, ZR_COLS)
        v0 = rwkv_v0[l - 1] if has_vres else jnp.zeros((W,), F32)
        vec8 = jnp.concatenate([rwkv_vec[l], v0[None]], axis=0)
        w2a2 = jax.scipy.linalg.block_diag(rwkv_w2[l], rwkv_a2[l]).astype(BF16)
        g2 = jnp.pad(rwkv_g2[l], ((0, 2 * LANES - RWKV_GATE_LORA), (0, 0)))
        if has_vres:
            v2 = jnp.pad(rwkv_v2[l - 1], ((RWKV_GATE_LORA, 2 * LANES - RWKV_GATE_LORA - RWKV_VRES_LORA), (0, 0)))
        else:
            v2 = jnp.zeros((2 * LANES, W), F32)
        g2v2 = jnp.concatenate([g2, v2], axis=1).astype(BF16)
        y_rwkv, v_first = _rwkv(zr, v_first, mu, vec8, w2a2, g2v2, bd64)

        y_mla = _attention(q, k, v, qk_norms)

        is_moe = (l % 2 == 1)
        router_p = None
        if is_moe:
            router_p = jnp.pad(moe_router[l // 2], ((0, 0), (0, LANES - N_EXPERTS)))
        outs = _mixout(x, y_rwkv, y_pool, y_mla, w_out[l].astype(BF16), mod_l,
                       norm_gain[l, 1].reshape(1, D), router_p)
        x_mid, h2 = outs[0], outs[1]
        xf = x_mid.reshape(B * T, D)
        if is_moe:
            xo = _moe(h2.reshape(ROW_CHUNKS * B * T, PACK_CHUNK_W), outs[2].reshape(B * T, LANES), outs[3],
                      moe_w_in[l // 2], moe_w_out[l // 2], xf, mod_l)
        else:
            xo = _ffn(h2.reshape(B * T, D), ffn_w_in[l // 2].astype(BF16), ffn_w_out[l // 2].astype(BF16),
                      xf, mod_l)
        x = xo.reshape(B, T, D)
    return x
```

```python
import functools

import numpy as np
import jax
import jax.numpy as jnp
from jax import lax
from jax.experimental import pallas as pl
from jax.experimental.pallas import tpu as pltpu
from jax.experimental.pallas import tpu_sc as plsc

F32 = jnp.float32
BF16 = jnp.bfloat16

D_MODEL = 1024
HEAD_DIM = 64
RWKV_WIDTH = 512
RWKV_HEADS = 8
POOL_WIDTH = 256
POOL_WINDOWS = (2, 4, 8, 16)
POOL_HALO = 16
MLA_HEADS = 4
MLA_QK_NOPE = 64
MLA_QK_ROPE = 32
MLA_QK_DIM = 96
MLA_Q_LORA = 256
MLA_KV_LORA = 128
ROPE_BASE = 10000.0
RWKV_DECAY_LORA = 64
RWKV_ICLR_LORA = 64
RWKV_VRES_LORA = 32
RWKV_GATE_LORA = 160
RWKV_LNX_EPS = 64e-5
D_FF = 2816
N_EXPERTS = 8
TOP_K = 2
D_FF_EXPERT = 3584
NORM_EPS = 1e-6
NEG_INF = -1e30

LANES = 128
SUBLANES = 8
VMEM_LIMIT = 56 * 1024 * 1024

ZR_COLS = 1920
Z_POOL_OFF = ZR_COLS
Z_QLAT_OFF = Z_POOL_OFF + POOL_WIDTH
Z_KVLAT_OFF = Z_QLAT_OFF + MLA_Q_LORA
Z_KROPE_OFF = Z_KVLAT_OFF + MLA_KV_LORA
Z_COLS = Z_KROPE_OFF + LANES

TM_MIX = 512
WKV_CHUNK = 64
TQ = 512
ATTN_BOUND_SLACK = 1.02
ATTN_BOUND_MAX = 40.0
TM_FFN = 512
TF_FFN = 1408
MOE_BLOCK = 1024
TF_MOE = 896
SC_WINDOW = 128
ROW_CHUNKS = 4
ROW_CHUNK_W = D_MODEL // ROW_CHUNKS
PACK_CHUNK_W = ROW_CHUNK_W // 2

SEGSUM_SPLITS = 1

NN = (((1,), (0,)), ((), ()))
NT = (((1,), (1,)), ((), ()))


def _dot(a, b, dims=NN):
    return lax.dot_general(a, b, dims, preferred_element_type=F32)


def _split2(a):
    hi = a.astype(BF16)
    lo = (a - hi.astype(F32)).astype(BF16)
    return hi, lo


def _mm(a, b, dims=NN, passes=3):
    if passes == 1:
        return _dot(a.astype(BF16), b.astype(BF16), dims)
    ah, al = _split2(a)
    bh, bl = _split2(b)
    return _dot(ah, bh, dims) + (_dot(ah, bl, dims) + _dot(al, bh, dims))


def _mm_exact_rhs(a, b_bf16, dims=NN, splits=SEGSUM_SPLITS):
    out = None
    rem = a
    for s in range(splits):
        part = rem.astype(BF16)
        term = _dot(part, b_bf16, dims)
        out = term if out is None else out + term
        if s + 1 < splits:
            rem = rem - part.astype(F32)
    return out


def _pack_bf16_pairs(h):
    w = h.shape[1] // 2
    lo = lax.bitcast_convert_type(h[:, :w].astype(BF16).astype(F32), jnp.uint32)
    hi = lax.bitcast_convert_type(h[:, w:].astype(BF16).astype(F32), jnp.uint32)
    return (lo >> 16) | (hi & jnp.uint32(0xFFFF0000))


def _unpack_bf16_pairs(p):
    lo = lax.bitcast_convert_type(p << 16, F32)
    hi = lax.bitcast_convert_type(p & jnp.uint32(0xFFFF0000), F32)
    return jnp.concatenate([lo, hi], axis=1).astype(BF16)


def _sigmoid(x):
    return 1.0 / (1.0 + jnp.exp(-x))


def _silu(x):
    return x * _sigmoid(x)


def _rms(x, eps=NORM_EPS):
    return x * lax.rsqrt(jnp.mean(x * x, axis=-1, keepdims=True) + eps)


def _cparams(sem):
    return pltpu.CompilerParams(dimension_semantics=sem, vmem_limit_bytes=VMEM_LIMIT)


def _adaln_kernel(c_ref, w_ref, b_ref, o_ref):
    ca = _silu(c_ref[...])
    o_ref[0] = _mm(ca, w_ref[0]) + b_ref[0]


def _adaln(c, w_ada, b_ada):
    L = w_ada.shape[0]
    B = c.shape[0]
    n = w_ada.shape[2] // D_MODEL
    return pl.pallas_call(
        _adaln_kernel,
        grid=(L, n),
        in_specs=[pl.BlockSpec((B, D_MODEL), lambda l, j: (0, 0)),
                  pl.BlockSpec((1, D_MODEL, D_MODEL), lambda l, j: (l, 0, j)),
                  pl.BlockSpec((1, 1, D_MODEL), lambda l, j: (l, 0, j))],
        out_specs=pl.BlockSpec((1, B, D_MODEL), lambda l, j: (l, 0, j)),
        out_shape=jax.ShapeDtypeStruct((L, B, n * D_MODEL), F32),
        compiler_params=_cparams(("arbitrary", "arbitrary")),
    )(c, w_ada, b_ada.reshape(L, 1, -1))


def _rope(x, cosf, sinf, lane):
    up = pltpu.roll(x, LANES - MLA_QK_ROPE // 2, axis=1)
    dn = pltpu.roll(x, MLA_QK_ROPE // 2, axis=1)
    rot = jnp.where(lane < MLA_QK_NOPE + MLA_QK_ROPE // 2, -up, dn)
    return x * cosf + rot * sinf


def _rope_kernel(pos_ref, freq_ref, cos_ref, sin_ref):
    tm = pos_ref.shape[1]
    lane = lax.broadcasted_iota(jnp.int32, (tm, LANES), 1)
    in_rope = (lane >= MLA_QK_NOPE) & (lane < MLA_QK_DIM)
    ang = pos_ref[0].astype(F32) * freq_ref[...]
    cos_ref[0] = jnp.where(in_rope, jnp.cos(ang), 1.0)
    sin_ref[0] = jnp.where(in_rope, jnp.sin(ang), 0.0)


def _rope_tables(pos3, freq):
    B, T, _ = pos3.shape
    tm = TM_MIX
    tok = lambda w: pl.BlockSpec((1, tm, w), lambda b, i: (b, i, 0))
    return pl.pallas_call(
        _rope_kernel,
        grid=(B, T // tm),
        in_specs=[tok(1), pl.BlockSpec((1, LANES), lambda b, i: (0, 0))],
        out_specs=[tok(LANES), tok(LANES)],
        out_shape=[jax.ShapeDtypeStruct((B, T, LANES), F32)] * 2,
        compiler_params=_cparams(("arbitrary", "arbitrary")),
    )(pos3, freq)


def _mixin_kernel(x_ref, mod_ref, gain_ref, win_ref, cos_ref, sin_ref, poolw_ref, pools_ref,
                  qg_ref, kvg_ref, wq_ref, wk_ref, wv_ref, qkg_ref, bd_ref,
                  zr_ref, yp_ref, q_ref, k_ref, v_ref, ubuf):
    i = pl.program_id(1)
    tm = x_ref.shape[1]
    x = x_ref[0]
    mod = mod_ref[0]
    h = _rms(x) * gain_ref[...] * (1.0 + mod[1:2]) + mod[0:1]
    z = _dot(h.astype(BF16), win_ref[...])
    zr_ref[0] = z[:, :ZR_COLS]

    @pl.when(i == 0)
    def _():
        ubuf[0:POOL_HALO, :] = jnp.zeros((POOL_HALO, POOL_WIDTH), F32)

    u = z[:, Z_POOL_OFF:Z_POOL_OFF + POOL_WIDTH]
    ubuf[POOL_HALO:, :] = u
    ue = ubuf[...]
    s2 = ue + pltpu.roll(ue, 1, axis=0)
    s4 = s2 + pltpu.roll(s2, 2, axis=0)
    s8 = s4 + pltpu.roll(s4, 4, axis=0)
    s16 = s8 + pltpu.roll(s8, 8, axis=0)
    ubuf[0:POOL_HALO, :] = u[tm - POOL_HALO:, :]
    lane_p = lax.broadcasted_iota(jnp.int32, (tm, POOL_WIDTH), 1)
    grp = lane_p // (POOL_WIDTH // len(POOL_WINDOWS))
    win_sum = jnp.where(grp == 0, s2[POOL_HALO:], jnp.where(grp == 1, s4[POOL_HALO:],
                        jnp.where(grp == 2, s8[POOL_HALO:], s16[POOL_HALO:])))
    win = jnp.where(grp == 0, 2, jnp.where(grp == 1, 4, jnp.where(grp == 2, 8, 16)))
    t_abs = i * tm + lax.broadcasted_iota(jnp.int32, (tm, POOL_WIDTH), 0)
    cnt = jnp.minimum(t_abs + 1, win).astype(F32)
    p = win_sum / cnt - u
    yp = _dot(p.astype(BF16), poolw_ref[...]) * pools_ref[...]
    yp_ref[0] = yp.astype(BF16)

    lane = lax.broadcasted_iota(jnp.int32, (tm, LANES), 1)
    cosf = cos_ref[0]
    sinf = sin_ref[0]

    q_lat = z[:, Z_QLAT_OFF:Z_QLAT_OFF + MLA_Q_LORA]
    kv_lat = z[:, Z_KVLAT_OFF:Z_KVLAT_OFF + MLA_KV_LORA]
    k_rope = z[:, Z_KROPE_OFF:Z_KROPE_OFF + LANES]
    qn = (_rms(q_lat) * qg_ref[...]).astype(BF16)
    kvn = (_rms(kv_lat) * kvg_ref[...]).astype(BF16)
    q = _dot(qn, wq_ref[...])
    kx = _dot(kvn, wk_ref[...])
    v = _dot(kvn, wv_ref[...])
    k_pe = _rope(k_rope, cosf, sinf, lane)
    qs, ks = [], []
    for hd in range(MLA_HEADS):
        sl = slice(hd * LANES, (hd + 1) * LANES)
        qs.append(_rope(q[:, sl], cosf, sinf, lane))
        ks.append(kx[:, sl] + k_pe)
    q = jnp.concatenate(qs, axis=1)
    k = jnp.concatenate(ks, axis=1)
    qss = _mm_exact_rhs(q * q, bd_ref[...]) * (1.0 / MLA_QK_DIM)
    kss = _mm_exact_rhs(k * k, bd_ref[...]) * (1.0 / MLA_QK_DIM)
    qkg = qkg_ref[...]
    q = q * lax.rsqrt(qss + NORM_EPS) * qkg[0:1] * (MLA_QK_DIM ** -0.5)
    k = k * lax.rsqrt(kss + NORM_EPS) * qkg[1:2]
    q_ref[0] = q.astype(BF16)
    k_ref[0] = k.astype(BF16)
    v_ref[0] = v.astype(BF16)


def _mixin(x, mod_l, gain, win, cosf, sinf, poolw, pools, qg, kvg, wq, wk, wv, qkg, bd128):
    B, T, _ = x.shape
    tm = TM_MIX
    const = lambda shape: pl.BlockSpec(shape, lambda b, i: tuple(0 for _ in shape))
    tok = lambda w: pl.BlockSpec((1, tm, w), lambda b, i: (b, i, 0))
    return pl.pallas_call(
        _mixin_kernel,
        grid=(B, T // tm),
        in_specs=[tok(D_MODEL),
                  pl.BlockSpec((1, 6, D_MODEL), lambda b, i: (b, 0, 0)),
                  const((1, D_MODEL)), const((D_MODEL, Z_COLS)),
                  tok(LANES), tok(LANES),
                  const((POOL_WIDTH, POOL_WIDTH)), const((1, POOL_WIDTH)),
                  const((1, MLA_Q_LORA)), const((1, MLA_KV_LORA)),
                  const((MLA_Q_LORA, MLA_HEADS * LANES)), const((MLA_KV_LORA, MLA_HEADS * LANES)),
                  const((MLA_KV_LORA, MLA_HEADS * HEAD_DIM)), const((2, MLA_HEADS * LANES)),
                  const((MLA_HEADS * LANES, MLA_HEADS * LANES))],
        out_specs=[tok(ZR_COLS), tok(POOL_WIDTH), tok(MLA_HEADS * LANES), tok(MLA_HEADS * LANES),
                   tok(MLA_HEADS * HEAD_DIM)],
        out_shape=[jax.ShapeDtypeStruct((B, T, ZR_COLS), F32),
                   jax.ShapeDtypeStruct((B, T, POOL_WIDTH), BF16),
                   jax.ShapeDtypeStruct((B, T, MLA_HEADS * LANES), BF16),
                   jax.ShapeDtypeStruct((B, T, MLA_HEADS * LANES), BF16),
                   jax.ShapeDtypeStruct((B, T, MLA_HEADS * HEAD_DIM), BF16)],
        scratch_shapes=[pltpu.VMEM((POOL_HALO + tm, POOL_WIDTH), F32)],
        compiler_params=_cparams(("arbitrary", "arbitrary")),
    )(x, mod_l, gain, win, cosf, sinf, poolw, pools, qg, kvg, wq, wk, wv, qkg, bd128)


WKV_PASSES_SCORE = 1
WKV_PASSES_INV = 1
WKV_PASSES_APPLY = 1
WKV_PASSES_STATE = 1
WKV_STEP_CHUNKS = 4


def _stack_heads(xp, lane):
    return jnp.concatenate([jnp.where(lane < HEAD_DIM, xp, 0.0),
                            jnp.where(lane >= HEAD_DIM, xp, 0.0)], axis=0)


def _wkv_prep(r, lw, k, v, kk, a, tri, masks):
    L = r[0].shape[0]
    n = 2 * L
    nc = len(r)
    each = lambda f, *ls: [f(*xs) for xs in zip(*ls)]
    lane = lax.broadcasted_iota(jnp.int32, (L, LANES), 1)
    stack = lambda x: _stack_heads(x, lane)
    cum = each(lambda x: _mm_exact_rhs_left(tri, x), lw)
    cum_last = each(lambda c: c[L - 1:L, :], cum)
    e_w = each(jnp.exp, cum)
    e_wm = each(lambda c, x: jnp.exp(c - x), cum, lw)
    e_iw = each(lambda c: jnp.exp(-c), cum)
    e_d = each(lambda cl, c: jnp.exp(cl - c), cum_last, cum)
    beta = each(lambda x, y: x * y, kk, a)
    r_t = each(lambda x, e: stack(x * e), r, e_w)
    a_t = each(lambda x, e: stack(-x * e), kk, e_wm)
    b_t = each(lambda x, e: stack(x * e), beta, e_iw)
    k_t = each(lambda x, e: stack(x * e), k, e_iw)
    b_d = each(lambda x, e: stack(x * e), beta, e_d)
    k_d = each(lambda x, e: stack(x * e), k, e_d)
    v_s = each(stack, v)
    yield
    g = each(lambda at, rt, bt, kt: _mm(jnp.concatenate([at, rt], axis=0),
                                        jnp.concatenate([bt, kt], axis=0), NT, WKV_PASSES_SCORE),
             a_t, r_t, b_t, k_t)
    strict, incl, levels = masks
    a_ab = each(lambda x: jnp.where(strict, x[:n, :n], 0.0), g)
    a_ak = each(lambda x: jnp.where(strict, x[:n, n:], 0.0), g)
    s_rb = each(lambda x: jnp.where(incl, x[n:, :n], 0.0), g)
    s_rk = each(lambda x: jnp.where(incl, x[n:, n:], 0.0), g)
    eye = jnp.where(levels[0][1], 1.0, 0.0)
    tinv = each(lambda x: eye + jnp.where(levels[0][0], x, 0.0), a_ab)
    yield
    for lvl_mask, _ in levels[1:]:
        et = each(lambda x, t: _mm(jnp.where(lvl_mask, x, 0.0), t, NN, WKV_PASSES_INV), a_ab, tinv)
        tinv = each(lambda t, x: t + _mm(t, x, NN, WKV_PASSES_INV), tinv, et)
        yield
    av = each(lambda x, y: _mm(x, y, NN, WKV_PASSES_APPLY), a_ak, v_s)
    tx = each(lambda t, x, y: _mm(t, jnp.concatenate([x, y], axis=1), NN, WKV_PASSES_APPLY),
              tinv, a_t, av)
    yield
    ra = each(lambda rt, s, x: rt + _mm(s, x[:, :LANES], NN, WKV_PASSES_APPLY), r_t, s_rb, tx)
    c2 = each(lambda sb, sk, x, vs: _mm(jnp.concatenate([sb, sk], axis=1),
                                        jnp.concatenate([x[:, LANES:], vs], axis=0),
                                        NN, WKV_PASSES_APPLY), s_rb, s_rk, tx, v_s)
    yield
    tb = each(lambda x, bd: _mm(x.T, bd, NN, WKV_PASSES_APPLY), tx, b_d)
    c3 = each(lambda x, vs, kd: x[LANES:] + _mm(vs.T, kd, NN, WKV_PASSES_APPLY), tb, v_s, k_d)
    return [(ra[i], c2[i], jnp.exp(cum_last[i]), tb[i][:LANES], c3[i]) for i in range(nc)]


def _mm_exact_rhs_left(tri_bf16, x):
    x0 = x.astype(BF16)
    r1 = x - x0.astype(F32)
    x1 = r1.astype(BF16)
    x2 = (r1 - x1.astype(F32)).astype(BF16)
    return _dot(tri_bf16, x0) + (_dot(tri_bf16, x1) + _dot(tri_bf16, x2))


def _wkv_masks(L):
    n = 2 * L
    row = lax.broadcasted_iota(jnp.int32, (n, n), 0)
    col = lax.broadcasted_iota(jnp.int32, (n, n), 1)
    strict = row > col
    incl = row >= col
    levels = []
    m = 1
    while m < L:
        same = (row // (2 * m)) == (col // (2 * m))
        lvl = same & ((row % (2 * m)) >= m) & ((col % (2 * m)) < m)
        levels.append((lvl, row == col))
        m *= 2
    return strict, incl, levels


def _rwkv_kernel(has_vres, *refs):
    if has_vres:
        (z_ref, vf_ref, mu_ref, vec_ref, w2a2_ref, g2v2_ref, bd_ref,
         y_ref, carry, state) = refs
    else:
        (z_ref, mu_ref, vec_ref, w2a2_ref, g2v2_ref, bd_ref,
         y_ref, vout_ref, carry, state) = refs
    c = pl.program_id(1)
    rows = z_ref.shape[1]
    L = WKV_CHUNK
    W = RWKV_WIDTH

    @pl.when(c == 0)
    def _():
        carry[...] = jnp.zeros(carry.shape, F32)
        state[...] = jnp.zeros(state.shape, F32)

    z = z_ref[0]
    row = lax.broadcasted_iota(jnp.int32, z.shape, 0)
    prev = jnp.where(row == 0, carry[SUBLANES - 1:SUBLANES, :], pltpu.roll(z, 1, axis=0))
    carry[...] = z[rows - SUBLANES:, :]
    zs_all = z + mu_ref[...] * (prev - z)
    vec = vec_ref[...]
    w0, a0, k_k, k_a, r_k, ln_g, ln_b, v0 = (vec[j:j + 1] for j in range(8))
    bd = bd_ref[...]
    masks = _wkv_masks(L)
    rowt = lax.broadcasted_iota(jnp.int32, (L, L), 0)
    colt = lax.broadcasted_iota(jnp.int32, (L, L), 1)
    tri = jnp.where(rowt >= colt, 1.0, 0.0).astype(BF16)
    n_pairs = RWKV_HEADS // 2
    S_now = [[state[p] for p in range(n_pairs)]]

    def group(r0, r1):
        zs = zs_all[r0:r1]
        r = zs[:, 0:W]
        k = zs[:, W:2 * W]
        v = zs[:, 2 * W:3 * W]
        wa = zs[:, 3 * W:3 * W + LANES]
        gb = zs[:, 3 * W + LANES:ZR_COLS]
        lane_a = lax.broadcasted_iota(jnp.int32, wa.shape, 1)
        t1 = _dot(jnp.where(lane_a < RWKV_DECAY_LORA, jnp.tanh(wa), wa).astype(BF16), w2a2_ref[...])
        lane_g = lax.broadcasted_iota(jnp.int32, gb.shape, 1)
        t2 = _dot(jnp.where(lane_g < RWKV_GATE_LORA, _sigmoid(gb), gb).astype(BF16), g2v2_ref[...])
        yield
        xw = w0 + t1[:, :W]
        w_log = -(jnp.maximum(-xw, 0.0) + jnp.log1p(jnp.exp(-jnp.abs(xw)))) - 0.5
        lw = -jnp.exp(w_log)
        a = _sigmoid(a0 + t1[:, W:])
        g = t2[:, :W]
        if has_vres:
            v = v + (vf_ref[0, r0:r1, :] - v) * _sigmoid(v0 + t2[:, W:])
        else:
            vout_ref[0, r0:r1, :] = v
        kk = k * k_k
        nrm = jnp.sqrt(_mm_exact_rhs(kk * kk, bd))
        kk = kk / jnp.maximum(nrm, 1e-12)
        k = k * (1.0 + (a - 1.0) * k_a)
        yield
        n_chunks = (r1 - r0) // L
        idx = [(ch, p) for ch in range(n_chunks) for p in range(n_pairs)]
        cut = lambda x: [x[ch * L:(ch + 1) * L, p * LANES:(p + 1) * LANES] for ch, p in idx]
        res = yield from _wkv_prep(cut(r), cut(lw), cut(k), cut(v), cut(kk), cut(a), tri, masks)
        prep = dict(zip(idx, res))
        yield
        S = S_now[0]
        y_rows = []
        for ch in range(n_chunks):
            y_s = [_mm(prep[ch, p][0], S[p], NT, WKV_PASSES_STATE) + prep[ch, p][1] for p in range(n_pairs)]
            y_rows.append(jnp.concatenate([x[:L] + x[L:] for x in y_s], axis=1))
            S = [S[p] * prep[ch, p][2] + _mm(S[p], prep[ch, p][3], NN, WKV_PASSES_STATE) + prep[ch, p][4]
                 for p in range(n_pairs)]
        S_now[0] = S
        y = jnp.concatenate(y_rows, axis=0)
        yield
        inv = 1.0 / HEAD_DIM
        mean = _mm_exact_rhs(y, bd) * inv
        yc = y - mean
        var = _mm_exact_rhs(yc * yc, bd) * inv
        yn = yc * lax.rsqrt(var + RWKV_LNX_EPS) * ln_g + ln_b
        bonus = _mm_exact_rhs(r * k * r_k, bd) * v
        y_ref[0, r0:r1, :] = ((yn + bonus) * g).astype(BF16)

    for _ in group(0, rows):
        pass
    for p in range(n_pairs):
        state[p] = S_now[0][p]


def _rwkv(zr, v_first, mu, vec8, w2a2, g2v2, bd64):
    B, T, _ = zr.shape
    L = WKV_CHUNK * WKV_STEP_CHUNKS
    has_vres = v_first is not None
    const = lambda shape: pl.BlockSpec(shape, lambda b, c: tuple(0 for _ in shape))
    tok = lambda w: pl.BlockSpec((1, L, w), lambda b, c: (b, c, 0))
    in_specs = [tok(ZR_COLS)]
    args = [zr]
    if has_vres:
        in_specs.append(tok(RWKV_WIDTH))
        args.append(v_first)
    in_specs += [const((1, ZR_COLS)), const((8, RWKV_WIDTH)), const((LANES, 2 * RWKV_WIDTH)),
                 const((2 * LANES, 2 * RWKV_WIDTH)), const((RWKV_WIDTH, RWKV_WIDTH))]
    args += [mu, vec8, w2a2, g2v2, bd64]
    out_specs = [tok(RWKV_WIDTH)]
    out_shape = [jax.ShapeDtypeStruct((B, T, RWKV_WIDTH), BF16)]
    if not has_vres:
        out_specs.append(tok(RWKV_WIDTH))
        out_shape.append(jax.ShapeDtypeStruct((B, T, RWKV_WIDTH), F32))
    outs = pl.pallas_call(
        functools.partial(_rwkv_kernel, has_vres),
        grid=(B, T // L),
        in_specs=in_specs, out_specs=out_specs, out_shape=out_shape,
        scratch_shapes=[pltpu.VMEM((SUBLANES, ZR_COLS), F32),
                        pltpu.VMEM((RWKV_HEADS // 2, LANES, LANES), F32)],
        compiler_params=_cparams(("arbitrary", "arbitrary")),
    )(*args)
    return (outs[0], v_first) if has_vres else (outs[0], outs[1])


def _attn_step(q_ref, k_ref, v_ref, m_sc, l_sc, acc_sc, masked):
    tq = q_ref.shape[1]
    tk = k_ref.shape[1]
    lane = lax.broadcasted_iota(jnp.int32, (tq, LANES), 1)
    if masked:
        rowi = lax.broadcasted_iota(jnp.int32, (tq, tk), 0)
        coli = lax.broadcasted_iota(jnp.int32, (tq, tk), 1)
        keep = coli <= rowi
    heads = range(MLA_HEADS)
    s = [_dot(q_ref[0, :, hd * LANES:(hd + 1) * LANES], k_ref[0, :, hd * LANES:(hd + 1) * LANES], NT)
         for hd in heads]
    if masked:
        s = [jnp.where(keep, x, NEG_INF) for x in s]
    m_prev = [m_sc[hd] for hd in heads]
    m_new = [jnp.maximum(m_prev[hd], jnp.max(s[hd], axis=-1, keepdims=True)) for hd in heads]
    alpha = [jnp.exp(m_prev[hd] - m_new[hd]) for hd in heads]
    p = [jnp.exp(s[hd] - jnp.concatenate([m_new[hd]] * (tk // LANES), axis=1)) for hd in heads]
    for hd in heads:
        l_sc[hd] = alpha[hd] * l_sc[hd] + jnp.sum(p[hd], axis=-1, keepdims=True)
        m_sc[hd] = m_new[hd]
    pv = [_dot(p[hd].astype(BF16), v_ref[0, :, (hd // 2) * LANES:(hd // 2 + 1) * LANES]) for hd in heads]
    first = lane < HEAD_DIM
    for pr in range(MLA_HEADS // 2):
        acc_sc[pr] = (acc_sc[pr] * jnp.where(first, alpha[2 * pr], alpha[2 * pr + 1])
                      + jnp.where(first, pv[2 * pr], pv[2 * pr + 1]))


def _attn_step_bounded(q_ref, k_ref, v_ref, cb_ref, l_sc, acc_sc, masked):
    tq = q_ref.shape[1]
    tk = k_ref.shape[1]
    lane = lax.broadcasted_iota(jnp.int32, (tq, LANES), 1)
    if masked:
        rowi = lax.broadcasted_iota(jnp.int32, (tq, tk), 0)
        coli = lax.broadcasted_iota(jnp.int32, (tq, tk), 1)
        keep = coli <= rowi
    pv = []
    for hd in range(MLA_HEADS):
        s = _dot(q_ref[0, :, hd * LANES:(hd + 1) * LANES], k_ref[0, :, hd * LANES:(hd + 1) * LANES], NT)
        c = cb_ref[0, hd:hd + 1, :]
        p = jnp.exp(s - jnp.concatenate([c] * (tk // LANES), axis=1))
        if masked:
            p = jnp.where(keep, p, 0.0)
        part = p[:, 0:LANES]
        for t in range(1, tk // LANES):
            part = part + p[:, t * LANES:(t + 1) * LANES]
        l_sc[hd] = l_sc[hd] + part
        pv.append(_dot(p.astype(BF16), v_ref[0, :, (hd // 2) * LANES:(hd // 2 + 1) * LANES]))
    first = lane < HEAD_DIM
    for pr in range(MLA_HEADS // 2):
        acc_sc[pr] = acc_sc[pr] + jnp.where(first, pv[2 * pr], pv[2 * pr + 1])


def _attn_finish(o_ref, l_sc, acc_sc, lane_partial):
    tq = o_ref.shape[1]
    lane = lax.broadcasted_iota(jnp.int32, (tq, LANES), 1)
    outs = []
    for pr in range(MLA_HEADS // 2):
        la, lb = l_sc[2 * pr], l_sc[2 * pr + 1]
        if lane_partial:
            la = jnp.sum(la, axis=-1, keepdims=True)
            lb = jnp.sum(lb, axis=-1, keepdims=True)
        outs.append(acc_sc[pr] / jnp.where(lane < HEAD_DIM, la, lb))
    o_ref[0] = jnp.concatenate(outs, axis=1).astype(BF16)


def _attn_kernel(ok_ref, qi_ref, kj_ref, q_ref, k_ref, v_ref, cb_ref, o_ref, m_sc, l_sc, acc_sc):
    i = qi_ref[pl.program_id(1)]
    j = kj_ref[pl.program_id(1)]
    bounded = ok_ref[pl.program_id(0)] == 1
    exact = jnp.logical_not(bounded)

    @pl.when(j == 0)
    def _():
        m_sc[...] = jnp.full(m_sc.shape, NEG_INF, F32)
        l_sc[...] = jnp.zeros(l_sc.shape, F32)
        acc_sc[...] = jnp.zeros(acc_sc.shape, F32)

    @pl.when(bounded & (j < i))
    def _():
        _attn_step_bounded(q_ref, k_ref, v_ref, cb_ref, l_sc, acc_sc, masked=False)

    @pl.when(bounded & (j == i))
    def _():
        _attn_step_bounded(q_ref, k_ref, v_ref, cb_ref, l_sc, acc_sc, masked=True)
        _attn_finish(o_ref, l_sc, acc_sc, lane_partial=True)

    @pl.when(exact & (j < i))
    def _():
        _attn_step(q_ref, k_ref, v_ref, m_sc, l_sc, acc_sc, masked=False)

    @pl.when(exact & (j == i))
    def _():
        _attn_step(q_ref, k_ref, v_ref, m_sc, l_sc, acc_sc, masked=True)
        _attn_finish(o_ref, l_sc, acc_sc, lane_partial=False)


def _attention(q, k, v, qk_gain):
    B, T, _ = q.shape
    nq = T // TQ
    gmax = jnp.max(jnp.abs(qk_gain), axis=1)
    c = gmax[0] * gmax[1] * (MLA_QK_DIM ** 0.5) * ATTN_BOUND_SLACK
    ok = jnp.broadcast_to((c <= ATTN_BOUND_MAX).astype(jnp.int32), (B,))
    cb = jnp.broadcast_to(c, (B, MLA_HEADS, LANES))
    pairs = [(i, j) for i in range(nq) for j in range(i + 1)]
    qi = jnp.asarray([p[0] for p in pairs], jnp.int32)
    kj = jnp.asarray([p[1] for p in pairs], jnp.int32)
    grid_spec = pltpu.PrefetchScalarGridSpec(
        num_scalar_prefetch=3,
        grid=(B, len(pairs)),
        in_specs=[pl.BlockSpec((1, TQ, MLA_HEADS * LANES), lambda b, t, ok, qi, kj: (b, qi[t], 0)),
                  pl.BlockSpec((1, TQ, MLA_HEADS * LANES), lambda b, t, ok, qi, kj: (b, kj[t], 0)),
                  pl.BlockSpec((1, TQ, MLA_HEADS * HEAD_DIM), lambda b, t, ok, qi, kj: (b, kj[t], 0)),
                  pl.BlockSpec((1, MLA_HEADS, LANES), lambda b, t, ok, qi, kj: (b, 0, 0))],
        out_specs=pl.BlockSpec((1, TQ, MLA_HEADS * HEAD_DIM), lambda b, t, ok, qi, kj: (b, qi[t], 0)),
        scratch_shapes=[pltpu.VMEM((MLA_HEADS, TQ, LANES), F32),
                        pltpu.VMEM((MLA_HEADS, TQ, LANES), F32),
                        pltpu.VMEM((MLA_HEADS // 2, TQ, LANES), F32)])
    return pl.pallas_call(
        _attn_kernel,
        grid_spec=grid_spec,
        out_shape=jax.ShapeDtypeStruct((B, T, MLA_HEADS * HEAD_DIM), BF16),
        compiler_params=_cparams(("arbitrary", "arbitrary")),
    )(ok, qi, kj, q, k, v, cb)


def _mixout_kernel(has_router, *refs):
    if has_router:
        (x_ref, yr_ref, yp_ref, ym_ref, wo_ref, mod_ref, gain_ref, rt_ref, tri_ref,
         xo_ref, h_ref, route_ref, cnt_ref, cnt_sc) = refs
    else:
        x_ref, yr_ref, yp_ref, ym_ref, wo_ref, mod_ref, gain_ref, xo_ref, h_ref = refs
    mod = mod_ref[0]
    o1 = RWKV_WIDTH
    o2 = RWKV_WIDTH + POOL_WIDTH
    mix = (_dot(yr_ref[0], wo_ref[0:o1, :]) + _dot(yp_ref[0], wo_ref[o1:o2, :])
           + _dot(ym_ref[0], wo_ref[o2:, :]))
    x = x_ref[0] + mod[2:3] * mix
    xo_ref[0] = x
    h = _rms(x) * gain_ref[...] * (1.0 + mod[4:5]) + mod[3:4]
    if not has_router:
        h_ref[0] = h.astype(BF16)
    else:
        hp = _pack_bf16_pairs(h)
        for ck in range(ROW_CHUNKS):
            h_ref[ck, 0] = hp[:, ck * PACK_CHUNK_W:(ck + 1) * PACK_CHUNK_W]
        logits = _mm(h, rt_ref[...])
        lane = lax.broadcasted_iota(jnp.int32, logits.shape, 1).astype(F32)
        lg = jnp.where(lane < N_EXPERTS, logits, -jnp.inf)
        m1 = jnp.max(lg, axis=-1, keepdims=True)
        i1 = jnp.min(jnp.where(lg == m1, lane, float(LANES)), axis=-1, keepdims=True)
        lg2 = jnp.where(lane == i1, -jnp.inf, lg)
        m2 = jnp.max(lg2, axis=-1, keepdims=True)
        i2 = jnp.min(jnp.where(lg2 == m2, lane, float(LANES)), axis=-1, keepdims=True)
        e2 = jnp.exp(m2 - m1)
        g1 = 1.0 / (1.0 + e2)
        g2 = e2 / (1.0 + e2)
        first = (pl.program_id(0) == 0) & (pl.program_id(1) == 0)

        @pl.when(first)
        def _():
            cnt_sc[...] = jnp.zeros(cnt_sc.shape, F32)

        hit1 = lane == i1
        hit2 = lane == i2
        onehot = jnp.where(hit1 | hit2, 1.0, 0.0)
        prefix = _dot(tri_ref[...], onehot.astype(BF16)) + cnt_sc[0:1, :]
        r1 = jnp.sum(jnp.where(hit1, prefix, 0.0), axis=-1, keepdims=True)
        r2 = jnp.sum(jnp.where(hit2, prefix, 0.0), axis=-1, keepdims=True)
        cnt_sc[...] = cnt_sc[...] + jnp.sum(onehot, axis=0, keepdims=True)
        cnt_ref[...] = cnt_sc[...]
        vals = (i1, i2, g1, g2, r1, r2)
        route = jnp.zeros(logits.shape, F32)
        for pos, val in enumerate(vals):
            route = jnp.where(lane == pos, val, route)
        route_ref[0] = route


def _mixout(x, yr, yp, ym, wo, mod_l, gain, router_p):
    B, T, _ = x.shape
    tm = TM_MIX
    has_router = router_p is not None
    const = lambda shape: pl.BlockSpec(shape, lambda b, i: tuple(0 for _ in shape))
    tok = lambda w: pl.BlockSpec((1, tm, w), lambda b, i: (b, i, 0))
    in_specs = [tok(D_MODEL), tok(RWKV_WIDTH), tok(POOL_WIDTH), tok(MLA_HEADS * HEAD_DIM),
                const((D_MODEL, D_MODEL)), pl.BlockSpec((1, 6, D_MODEL), lambda b, i: (b, 0, 0)),
                const((1, D_MODEL))]
    args = [x, yr, yp, ym, wo, mod_l, gain]
    out_specs = [tok(D_MODEL), tok(D_MODEL)]
    out_shape = [jax.ShapeDtypeStruct((B, T, D_MODEL), F32), jax.ShapeDtypeStruct((B, T, D_MODEL), BF16)]
    scratch = []
    if has_router:
        out_specs[1] = pl.BlockSpec((ROW_CHUNKS, 1, tm, PACK_CHUNK_W), lambda b, i: (0, b, i, 0))
        out_shape[1] = jax.ShapeDtypeStruct((ROW_CHUNKS, B, T, PACK_CHUNK_W), jnp.uint32)
        ids = np.arange(tm)
        tri = jnp.asarray(ids[:, None] > ids[None, :], BF16)
        in_specs += [const((D_MODEL, LANES)), const((tm, tm))]
        args += [router_p, tri]
        out_specs += [tok(LANES), const((SUBLANES, LANES))]
        out_shape += [jax.ShapeDtypeStruct((B, T, LANES), F32),
                      jax.ShapeDtypeStruct((SUBLANES, LANES), F32)]
        scratch = [pltpu.VMEM((SUBLANES, LANES), F32)]
    return pl.pallas_call(
        functools.partial(_mixout_kernel, has_router),
        grid=(B, T // tm),
        in_specs=in_specs, out_specs=out_specs, out_shape=out_shape, scratch_shapes=scratch,
        compiler_params=_cparams(("arbitrary", "arbitrary")),
    )(*args)


def _ffn_kernel(h_ref, wg_ref, wu_ref, wo_ref, x_ref, mod_ref, o_ref, acc):
    j = pl.program_id(1)

    @pl.when(j == 0)
    def _():
        acc[...] = jnp.zeros(acc.shape, F32)

    h = h_ref[...]
    gg = _dot(h, wg_ref[...])
    uu = _dot(h, wu_ref[...])
    acc[...] += _dot((_silu(gg) * uu).astype(BF16), wo_ref[...])

    @pl.when(j == pl.num_programs(1) - 1)
    def _():
        o_ref[...] = x_ref[...] + mod_ref[0][5:6] * acc[...]


def _ffn(h2, w_in, w_out, x, mod_l):
    N = h2.shape[0]
    T = N // mod_l.shape[0]
    tm, tf = TM_FFN, TF_FFN
    nf = D_FF // tf
    per_b = T // tm
    return pl.pallas_call(
        _ffn_kernel,
        grid=(N // tm, nf),
        in_specs=[pl.BlockSpec((tm, D_MODEL), lambda i, j: (i, 0)),
                  pl.BlockSpec((D_MODEL, tf), lambda i, j: (0, j)),
                  pl.BlockSpec((D_MODEL, tf), lambda i, j: (0, j + nf)),
                  pl.BlockSpec((tf, D_MODEL), lambda i, j: (j, 0)),
                  pl.BlockSpec((tm, D_MODEL), lambda i, j: (i, 0)),
                  pl.BlockSpec((1, 6, D_MODEL), lambda i, j: (i // per_b, 0, 0))],
        out_specs=pl.BlockSpec((tm, D_MODEL), lambda i, j: (i, 0)),
        out_shape=jax.ShapeDtypeStruct((N, D_MODEL), F32),
        scratch_shapes=[pltpu.VMEM((tm, D_MODEL), F32)],
        compiler_params=_cparams(("arbitrary", "arbitrary")),
    )(h2, w_in, w_in, w_out, x, mod_l)


def _moe_kernel(be_ref, nv_ref, last_ref, x_ref, wg_ref, wu_ref, wo_ref, o_ref, acc, xm):
    i = pl.program_id(0)
    j = pl.program_id(1)

    @pl.when(i <= last_ref[0])
    def _():
        @pl.when(j == 0)
        def _():
            acc[...] = jnp.zeros(acc.shape, F32)
            row = lax.broadcasted_iota(jnp.int32, (xm.shape[0], 1), 0)
            xp = jnp.concatenate([x_ref[ck] for ck in range(ROW_CHUNKS)], axis=1)
            xp = jnp.where(row < nv_ref[i], xp, jnp.uint32(0))
            xm[...] = _unpack_bf16_pairs(xp)

        x = xm[...]
        gg = _dot(x, wg_ref[0].astype(BF16))
        uu = _dot(x, wu_ref[0].astype(BF16))
        acc[...] += _dot((_silu(gg) * uu).astype(BF16), wo_ref[0].astype(BF16))

        @pl.when(j == pl.num_programs(1) - 1)
        def _():
            for ck in range(ROW_CHUNKS):
                o_ref[ck] = acc[:, ck * ROW_CHUNK_W:(ck + 1) * ROW_CHUNK_W]


def _moe_experts(xs, w_in, w_out, block_exp, n_valid, last_blk):
    n_rows = xs.shape[1]
    tm, tf = MOE_BLOCK, TF_MOE
    nf = D_FF_EXPERT // tf
    blk = lambda i, last: jnp.minimum(i, last[0])
    chunk = lambda i, j, last: jnp.where(i <= last[0], j, nf - 1)
    grid_spec = pltpu.PrefetchScalarGridSpec(
        num_scalar_prefetch=3,
        grid=(n_rows // tm, nf),
        in_specs=[pl.BlockSpec((ROW_CHUNKS, tm, PACK_CHUNK_W),
                               lambda i, j, be, nv, last: (0, blk(i, last), 0)),
                  pl.BlockSpec((1, D_MODEL, tf),
                               lambda i, j, be, nv, last: (be[blk(i, last)], 0, chunk(i, j, last))),
                  pl.BlockSpec((1, D_MODEL, tf),
                               lambda i, j, be, nv, last: (be[blk(i, last)], 0, chunk(i, j, last) + nf)),
                  pl.BlockSpec((1, tf, D_MODEL),
                               lambda i, j, be, nv, last: (be[blk(i, last)], chunk(i, j, last), 0))],
        out_specs=pl.BlockSpec((ROW_CHUNKS, tm, ROW_CHUNK_W), lambda i, j, be, nv, last: (0, blk(i, last), 0)),
        scratch_shapes=[pltpu.VMEM((tm, D_MODEL), F32), pltpu.VMEM((tm, D_MODEL), BF16)])
    return pl.pallas_call(
        _moe_kernel,
        grid_spec=grid_spec,
        out_shape=jax.ShapeDtypeStruct((ROW_CHUNKS, n_rows, ROW_CHUNK_W), F32),
        compiler_params=_cparams(("arbitrary", "arbitrary")),
    )(block_exp, n_valid, last_blk, xs, w_in, w_in, w_out)


def _sc_mesh():
    return plsc.VectorSubcoreMesh(core_axis_name="c", subcore_axis_name="s")


def _sc_scatter_rows(x, dest, n_rows):
    N, D = x.shape
    K = dest.shape[0]
    win = SC_WINDOW

    @pl.kernel(out_type=jax.ShapeDtypeStruct((n_rows, D), x.dtype), mesh=_sc_mesh(), scratch_types=[])
    def scatter(x_hbm, d_hbm, o_hbm):
        def body(x_vmem, *idx_vmem):
            for iv in idx_vmem:
                pltpu.sync_copy(x_vmem, o_hbm.at[iv.at[0]])

        pltpu.emit_pipeline(
            body,
            grid=(N // win,),
            in_specs=[pl.BlockSpec((win, D), lambda i: (i, 0))]
            + [pl.BlockSpec((1, win), functools.partial(lambda k, i: (k, i), k)) for k in range(K)],
            out_specs=[],
            core_axis_name=("c", "s"),
            dimension_semantics=(pltpu.PARALLEL,),
        )(x_hbm, *([d_hbm] * K))

    return scatter(x, dest)


def _sc_gather_rows(x, idx):
    n = idx.shape[0]
    D = x.shape[1]
    win = SC_WINDOW

    @pl.kernel(out_type=jax.ShapeDtypeStruct((n, D), x.dtype), mesh=_sc_mesh(), scratch_types=[])
    def gather(x_hbm, i_hbm, o_hbm):
        def body(i_vmem, o_vmem):
            pltpu.sync_copy(x_hbm.at[i_vmem.at[0]], o_vmem)

        pltpu.emit_pipeline(
            body,
            grid=(n // win,),
            in_specs=[pl.BlockSpec((1, win), lambda i: (0, i))],
            out_specs=[pl.BlockSpec((win, D), lambda i: (i, 0))],
            core_axis_name=("c", "s"),
            dimension_semantics=(pltpu.PARALLEL,),
        )(i_hbm, o_hbm)

    return gather(x, idx.reshape(1, n))


def _combine_kernel(x_ref, ya_ref, yb_ref, route_ref, mod_ref, o_ref):
    rt = route_ref[...]
    ya = jnp.concatenate([ya_ref[0, ck] for ck in range(ROW_CHUNKS)], axis=1)
    yb = jnp.concatenate([yb_ref[0, ck] for ck in range(ROW_CHUNKS)], axis=1)
    f = rt[:, 2:3] * ya + rt[:, 3:4] * yb
    o_ref[...] = x_ref[...] + mod_ref[0][5:6] * f


def _combine(x, y2, route, mod_l):
    N = x.shape[0]
    T = N // mod_l.shape[0]
    tm = 1024
    per_b = T // tm
    tok = pl.BlockSpec((tm, D_MODEL), lambda i: (i, 0))
    slot = lambda k: pl.BlockSpec((1, ROW_CHUNKS, tm, ROW_CHUNK_W), lambda i: (k, 0, i, 0))
    return pl.pallas_call(
        _combine_kernel,
        grid=(N // tm,),
        in_specs=[tok, slot(0), slot(1),
                  pl.BlockSpec((tm, LANES), lambda i: (i, 0)),
                  pl.BlockSpec((1, 6, D_MODEL), lambda i: (i // per_b, 0, 0))],
        out_specs=tok,
        out_shape=jax.ShapeDtypeStruct((N, D_MODEL), F32),
        compiler_params=_cparams(("arbitrary",)),
    )(x, y2, y2, route, mod_l)


def _moe(h2, route, counts, w_in, w_out, x, mod_l):
    N = x.shape[0]
    blk = MOE_BLOCK
    cnt = counts[0, :N_EXPERTS].astype(jnp.int32)
    padded = (cnt + blk - 1) // blk * blk
    pend = jnp.cumsum(padded)
    pstart = pend - padded
    e = route[:, 0:TOP_K].astype(jnp.int32)
    rank = route[:, 2 * TOP_K:3 * TOP_K].astype(jnp.int32)
    dest = (jnp.take(pstart, e) + rank).T
    n_blocks = N * TOP_K // blk + N_EXPERTS
    bstart = jnp.arange(n_blocks, dtype=jnp.int32) * blk
    block_exp = jnp.clip(jnp.searchsorted(pend, bstart, side='right'), 0, N_EXPERTS - 1).astype(jnp.int32)
    n_valid = jnp.clip(cnt[block_exp] - (bstart - pstart[block_exp]), 0, blk).astype(jnp.int32)
    n_rows = n_blocks * blk
    dest_ck = dest[:, None, :] + (jnp.arange(ROW_CHUNKS, dtype=jnp.int32) * n_rows)[None, :, None]
    xs = _sc_scatter_rows(h2, dest_ck.reshape(TOP_K, ROW_CHUNKS * N), ROW_CHUNKS * n_rows)
    last_blk = (pend[-1:] // blk - 1).astype(jnp.int32)
    yb = _moe_experts(xs.reshape(ROW_CHUNKS, n_rows, PACK_CHUNK_W), w_in, w_out, block_exp, n_valid,
                      last_blk)
    y2 = _sc_gather_rows(yb.reshape(ROW_CHUNKS * n_rows, ROW_CHUNK_W), dest_ck.reshape(-1))
    return _combine(x, y2.reshape(TOP_K, ROW_CHUNKS, N, ROW_CHUNK_W), route, mod_l)


def _layout_w_in(w, has_vres):
    W = RWKV_WIDTH
    off_gd = 3 * W + RWKV_DECAY_LORA + RWKV_ICLR_LORA
    off_pool = off_gd + RWKV_GATE_LORA
    off_q = off_pool + POOL_WIDTH
    off_kv = off_q + MLA_Q_LORA
    off_kr = off_kv + MLA_KV_LORA
    n_base = off_kr + MLA_QK_ROPE
    d = w.shape[0]
    zeros = lambda n: jnp.zeros((d, n), w.dtype)
    vd = w[:, n_base:n_base + RWKV_VRES_LORA] if has_vres else zeros(RWKV_VRES_LORA)
    cols = [w[:, :off_gd], w[:, off_gd:off_pool], vd, zeros(ZR_COLS - off_pool - RWKV_VRES_LORA),
            w[:, off_pool:off_q], w[:, off_q:off_kv], w[:, off_kv:off_kr],
            zeros(MLA_QK_NOPE), w[:, off_kr:n_base], zeros(LANES - MLA_QK_DIM)]
    return jnp.concatenate(cols, axis=1).astype(BF16)


def _pad_heads(w, per_head, keep_from, keep_n):
    K = w.shape[0]
    wh = w.reshape(K, MLA_HEADS, per_head)[:, :, keep_from:keep_from + keep_n]
    wh = jnp.pad(wh, ((0, 0), (0, 0), (0, LANES - keep_n)))
    return wh.reshape(K, MLA_HEADS * LANES)


def kernel(x, c, positions, w_ada, b_ada, norm_gain, w_in_first, w_in_rest, mu_shift, mu_shift_v,
           rwkv_vec, rwkv_v0, rwkv_w2, rwkv_a2, rwkv_g2, rwkv_v2, pool_w, pool_scale,
           mla_q_lat_gain, mla_kv_lat_gain, mla_wq_up, mla_wkv_up, mla_qk_gain, w_out, ffn_w_in,
           ffn_w_out, moe_router, moe_w_in, moe_w_out):
    B, T, D = x.shape
    depth = w_ada.shape[0]
    W = RWKV_WIDTH
    mod = _adaln(c, w_ada, b_ada).reshape(depth, B, 6, D)
    pos3 = positions.reshape(B, T, 1)
    inv_freq = ROPE_BASE ** (-jnp.arange(0, MLA_QK_ROPE, 2, dtype=F32) / MLA_QK_ROPE)
    freq = jnp.concatenate([jnp.zeros((MLA_QK_NOPE,), F32), inv_freq, inv_freq,
                            jnp.zeros((LANES - MLA_QK_DIM,), F32)]).reshape(1, LANES)
    cosf, sinf = _rope_tables(pos3, freq)
    hid = np.arange(W) // HEAD_DIM
    bd64 = jnp.asarray(hid[:, None] == hid[None, :], BF16)
    bid = np.arange(MLA_HEADS * LANES) // LANES
    bd128 = jnp.asarray(bid[:, None] == bid[None, :], BF16)

    v_first = None
    for l in range(depth):
        has_vres = l > 0
        mod_l = mod[l]
        win = _layout_w_in(w_in_first if l == 0 else w_in_rest[l - 1], has_vres)
        poolw = jax.scipy.linalg.block_diag(*[pool_w[l, g] for g in range(len(POOL_WINDOWS))]).astype(BF16)
        wq = _pad_heads(mla_wq_up[l], MLA_QK_DIM, 0, MLA_QK_DIM).astype(BF16)
        wk = _pad_heads(mla_wkv_up[l], MLA_QK_NOPE + HEAD_DIM, 0, MLA_QK_NOPE).astype(BF16)
        wv = mla_wkv_up[l].reshape(MLA_KV_LORA, MLA_HEADS, MLA_QK_NOPE + HEAD_DIM)[:, :, MLA_QK_NOPE:]
        wv = wv.reshape(MLA_KV_LORA, MLA_HEADS * HEAD_DIM).astype(BF16)
        qkg = jnp.tile(jnp.pad(mla_qk_gain[l], ((0, 0), (0, LANES - MLA_QK_DIM))), (1, MLA_HEADS))
        zr, y_pool, q, k, v = _mixin(
            x, mod_l, norm_gain[l, 0].reshape(1, D), win, cosf, sinf, poolw,
            pool_scale[l].reshape(1, -1), mla_q_lat_gain[l].reshape(1, -1),
            mla_kv_lat_gain[l].reshape(1, -1), wq, wk, wv, qkg, bd128)

        pad_mu = ZR_COLS - mu_shift.shape[1] - RWKV_VRES_LORA
        mu_v = mu_shift_v[l - 1] if has_vres else jnp.zeros((RWKV_VRES_LORA,), F32)
        mu = jnp.concatenate([mu_shift[l], mu_v, jnp.zeros((pad_mu,), F32)]).reshape(1, ZR_COLS)
        v0 = rwkv_v0[l - 1] if has_vres else jnp.zeros((W,), F32)
        vec8 = jnp.concatenate([rwkv_vec[l], v0[None]], axis=0)
        w2a2 = jax.scipy.linalg.block_diag(rwkv_w2[l], rwkv_a2[l]).astype(BF16)
        g2 = jnp.pad(rwkv_g2[l], ((0, 2 * LANES - RWKV_GATE_LORA), (0, 0)))
        if has_vres:
            v2 = jnp.pad(rwkv_v2[l - 1], ((RWKV_GATE_LORA, 2 * LANES - RWKV_GATE_LORA - RWKV_VRES_LORA), (0, 0)))
        else:
            v2 = jnp.zeros((2 * LANES, W), F32)
        g2v2 = jnp.concatenate([g2, v2], axis=1).astype(BF16)
        y_rwkv, v_first = _rwkv(zr, v_first, mu, vec8, w2a2, g2v2, bd64)

        y_mla = _attention(q, k, v, mla_qk_gain[l])

        is_moe = (l % 2 == 1)
        router_p = None
        if is_moe:
            router_p = jnp.pad(moe_router[l // 2], ((0, 0), (0, LANES - N_EXPERTS)))
        outs = _mixout(x, y_rwkv, y_pool, y_mla, w_out[l].astype(BF16), mod_l,
                       norm_gain[l, 1].reshape(1, D), router_p)
        x_mid, h2 = outs[0], outs[1]
        xf = x_mid.reshape(B * T, D)
        if is_moe:
            xo = _moe(h2.reshape(ROW_CHUNKS * B * T, PACK_CHUNK_W), outs[2].reshape(B * T, LANES), outs[3],
                      moe_w_in[l // 2], moe_w_out[l // 2], xf, mod_l)
        else:
            xo = _ffn(h2.reshape(B * T, D), ffn_w_in[l // 2].astype(BF16), ffn_w_out[l // 2].astype(BF16),
                      xf, mod_l)
        x = xo.reshape(B, T, D)
    return x
```

```python
import functools

import numpy as np
import jax
import jax.numpy as jnp
from jax import lax
from jax.experimental import pallas as pl
from jax.experimental.pallas import tpu as pltpu
from jax.experimental.pallas import tpu_sc as plsc

F32 = jnp.float32
BF16 = jnp.bfloat16

D_MODEL = 1024
HEAD_DIM = 64
RWKV_WIDTH = 512
RWKV_HEADS = 8
POOL_WIDTH = 256
POOL_WINDOWS = (2, 4, 8, 16)
POOL_HALO = 16
MLA_HEADS = 4
MLA_QK_NOPE = 64
MLA_QK_ROPE = 32
MLA_QK_DIM = 96
MLA_Q_LORA = 256
MLA_KV_LORA = 128
ROPE_BASE = 10000.0
RWKV_DECAY_LORA = 64
RWKV_ICLR_LORA = 64
RWKV_VRES_LORA = 32
RWKV_GATE_LORA = 160
RWKV_LNX_EPS = 64e-5
D_FF = 2816
N_EXPERTS = 8
TOP_K = 2
D_FF_EXPERT = 3584
NORM_EPS = 1e-6
NEG_INF = -1e30

LANES = 128
SUBLANES = 8
VMEM_LIMIT = 56 * 1024 * 1024

ZR_COLS = 1920
Z_POOL_OFF = ZR_COLS
Z_QLAT_OFF = Z_POOL_OFF + POOL_WIDTH
Z_KVLAT_OFF = Z_QLAT_OFF + MLA_Q_LORA
Z_KROPE_OFF = Z_KVLAT_OFF + MLA_KV_LORA
Z_COLS = Z_KROPE_OFF + LANES

TM_MIX = 512
WKV_CHUNK = 64
TQ = 512
ATTN_BOUND_SLACK = 1.02
ATTN_BOUND_MAX = 40.0
TM_FFN = 512
TF_FFN = 1408
MOE_BLOCK = 1024
TF_MOE = 512
COMBINE_PARTS = 4
SC_WINDOW = 128
ROW_CHUNKS = 4
ROW_CHUNK_W = D_MODEL // ROW_CHUNKS
PACK_CHUNK_W = ROW_CHUNK_W // 2

SEGSUM_SPLITS = 1

NN = (((1,), (0,)), ((), ()))
NT = (((1,), (1,)), ((), ()))


def _dot(a, b, dims=NN):
    return lax.dot_general(a, b, dims, preferred_element_type=F32)


def _split2(a):
    hi = a.astype(BF16)
    lo = (a - hi.astype(F32)).astype(BF16)
    return hi, lo


def _mm(a, b, dims=NN, passes=3):
    if passes == 1:
        return _dot(a.astype(BF16), b.astype(BF16), dims)
    ah, al = _split2(a)
    bh, bl = _split2(b)
    return _dot(ah, bh, dims) + (_dot(ah, bl, dims) + _dot(al, bh, dims))


def _mm_exact_rhs(a, b_bf16, dims=NN, splits=SEGSUM_SPLITS):
    out = None
    rem = a
    for s in range(splits):
        part = rem.astype(BF16)
        term = _dot(part, b_bf16, dims)
        out = term if out is None else out + term
        if s + 1 < splits:
            rem = rem - part.astype(F32)
    return out


def _pack_bf16_pairs(h):
    w = h.shape[1] // 2
    lo = lax.bitcast_convert_type(h[:, :w].astype(BF16).astype(F32), jnp.uint32)
    hi = lax.bitcast_convert_type(h[:, w:].astype(BF16).astype(F32), jnp.uint32)
    return (lo >> 16) | (hi & jnp.uint32(0xFFFF0000))


def _unpack_bf16_pairs(p):
    lo = lax.bitcast_convert_type(p << 16, F32)
    hi = lax.bitcast_convert_type(p & jnp.uint32(0xFFFF0000), F32)
    return jnp.concatenate([lo, hi], axis=1).astype(BF16)


def _sigmoid(x):
    return 1.0 / (1.0 + jnp.exp(-x))


def _silu(x):
    return x * _sigmoid(x)


def _rms(x, eps=NORM_EPS):
    return x * lax.rsqrt(jnp.mean(x * x, axis=-1, keepdims=True) + eps)


def _cparams(sem):
    return pltpu.CompilerParams(dimension_semantics=sem, vmem_limit_bytes=VMEM_LIMIT)


def _adaln_kernel(c_ref, w_ref, b_ref, o_ref):
    ca = _silu(c_ref[...])
    o_ref[0] = _mm(ca, w_ref[0]) + b_ref[0]


def _adaln(c, w_ada, b_ada):
    L = w_ada.shape[0]
    B = c.shape[0]
    n = w_ada.shape[2] // D_MODEL
    return pl.pallas_call(
        _adaln_kernel,
        grid=(L, n),
        in_specs=[pl.BlockSpec((B, D_MODEL), lambda l, j: (0, 0)),
                  pl.BlockSpec((1, D_MODEL, D_MODEL), lambda l, j: (l, 0, j)),
                  pl.BlockSpec((1, 1, D_MODEL), lambda l, j: (l, 0, j))],
        out_specs=pl.BlockSpec((1, B, D_MODEL), lambda l, j: (l, 0, j)),
        out_shape=jax.ShapeDtypeStruct((L, B, n * D_MODEL), F32),
        compiler_params=_cparams(("arbitrary", "arbitrary")),
    )(c, w_ada, b_ada.reshape(L, 1, -1))


def _rope(x, cosf, sinf, lane):
    up = pltpu.roll(x, LANES - MLA_QK_ROPE // 2, axis=1)
    dn = pltpu.roll(x, MLA_QK_ROPE // 2, axis=1)
    rot = jnp.where(lane < MLA_QK_NOPE + MLA_QK_ROPE // 2, -up, dn)
    return x * cosf + rot * sinf


def _rope_kernel(pos_ref, freq_ref, cos_ref, sin_ref):
    tm = pos_ref.shape[1]
    lane = lax.broadcasted_iota(jnp.int32, (tm, LANES), 1)
    in_rope = (lane >= MLA_QK_NOPE) & (lane < MLA_QK_DIM)
    ang = pos_ref[0].astype(F32) * freq_ref[...]
    cos_ref[0] = jnp.where(in_rope, jnp.cos(ang), 1.0)
    sin_ref[0] = jnp.where(in_rope, jnp.sin(ang), 0.0)


def _rope_tables(pos3, freq):
    B, T, _ = pos3.shape
    tm = TM_MIX
    tok = lambda w: pl.BlockSpec((1, tm, w), lambda b, i: (b, i, 0))
    return pl.pallas_call(
        _rope_kernel,
        grid=(B, T // tm),
        in_specs=[tok(1), pl.BlockSpec((1, LANES), lambda b, i: (0, 0))],
        out_specs=[tok(LANES), tok(LANES)],
        out_shape=[jax.ShapeDtypeStruct((B, T, LANES), F32)] * 2,
        compiler_params=_cparams(("arbitrary", "arbitrary")),
    )(pos3, freq)


def _mixin_kernel(x_ref, mod_ref, gain_ref, win_ref, cos_ref, sin_ref, poolw_ref, pools_ref,
                  qg_ref, kvg_ref, wq_ref, wk_ref, wv_ref, qkg_ref, bd_ref,
                  zr_ref, yp_ref, q_ref, k_ref, v_ref, ubuf):
    i = pl.program_id(1)
    tm = x_ref.shape[1]
    x = x_ref[0]
    mod = mod_ref[0]
    h = _rms(x) * gain_ref[...] * (1.0 + mod[1:2]) + mod[0:1]
    z = _dot(h.astype(BF16), win_ref[...])
    zr_ref[0] = z[:, :ZR_COLS]

    @pl.when(i == 0)
    def _():
        ubuf[0:POOL_HALO, :] = jnp.zeros((POOL_HALO, POOL_WIDTH), F32)

    u = z[:, Z_POOL_OFF:Z_POOL_OFF + POOL_WIDTH]
    ubuf[POOL_HALO:, :] = u
    ue = ubuf[...]
    s2 = ue + pltpu.roll(ue, 1, axis=0)
    s4 = s2 + pltpu.roll(s2, 2, axis=0)
    s8 = s4 + pltpu.roll(s4, 4, axis=0)
    s16 = s8 + pltpu.roll(s8, 8, axis=0)
    ubuf[0:POOL_HALO, :] = u[tm - POOL_HALO:, :]
    lane_p = lax.broadcasted_iota(jnp.int32, (tm, POOL_WIDTH), 1)
    grp = lane_p // (POOL_WIDTH // len(POOL_WINDOWS))
    win_sum = jnp.where(grp == 0, s2[POOL_HALO:], jnp.where(grp == 1, s4[POOL_HALO:],
                        jnp.where(grp == 2, s8[POOL_HALO:], s16[POOL_HALO:])))
    win = jnp.where(grp == 0, 2, jnp.where(grp == 1, 4, jnp.where(grp == 2, 8, 16)))
    t_abs = i * tm + lax.broadcasted_iota(jnp.int32, (tm, POOL_WIDTH), 0)
    cnt = jnp.minimum(t_abs + 1, win).astype(F32)
    p = win_sum / cnt - u
    yp = _dot(p.astype(BF16), poolw_ref[...]) * pools_ref[...]
    yp_ref[0] = yp.astype(BF16)

    lane = lax.broadcasted_iota(jnp.int32, (tm, LANES), 1)
    cosf = cos_ref[0]
    sinf = sin_ref[0]

    q_lat = z[:, Z_QLAT_OFF:Z_QLAT_OFF + MLA_Q_LORA]
    kv_lat = z[:, Z_KVLAT_OFF:Z_KVLAT_OFF + MLA_KV_LORA]
    k_rope = z[:, Z_KROPE_OFF:Z_KROPE_OFF + LANES]
    qn = (_rms(q_lat) * qg_ref[...]).astype(BF16)
    kvn = (_rms(kv_lat) * kvg_ref[...]).astype(BF16)
    q = _dot(qn, wq_ref[...])
    kx = _dot(kvn, wk_ref[...])
    v = _dot(kvn, wv_ref[...])
    k_pe = _rope(k_rope, cosf, sinf, lane)
    qs, ks = [], []
    for hd in range(MLA_HEADS):
        sl = slice(hd * LANES, (hd + 1) * LANES)
        qs.append(_rope(q[:, sl], cosf, sinf, lane))
        ks.append(kx[:, sl] + k_pe)
    q = jnp.concatenate(qs, axis=1)
    k = jnp.concatenate(ks, axis=1)
    qss = _mm_exact_rhs(q * q, bd_ref[...]) * (1.0 / MLA_QK_DIM)
    kss = _mm_exact_rhs(k * k, bd_ref[...]) * (1.0 / MLA_QK_DIM)
    qkg = qkg_ref[...]
    q = q * lax.rsqrt(qss + NORM_EPS) * qkg[0:1] * (MLA_QK_DIM ** -0.5)
    k = k * lax.rsqrt(kss + NORM_EPS) * qkg[1:2]
    q_ref[0] = q.astype(BF16)
    k_ref[0] = k.astype(BF16)
    v_ref[0] = v.astype(BF16)


def _mixin(x, mod_l, gain, win, cosf, sinf, poolw, pools, qg, kvg, wq, wk, wv, qkg, bd128):
    B, T, _ = x.shape
    tm = TM_MIX
    const = lambda shape: pl.BlockSpec(shape, lambda b, i: tuple(0 for _ in shape))
    tok = lambda w: pl.BlockSpec((1, tm, w), lambda b, i: (b, i, 0))
    return pl.pallas_call(
        _mixin_kernel,
        grid=(B, T // tm),
        in_specs=[tok(D_MODEL),
                  pl.BlockSpec((1, 6, D_MODEL), lambda b, i: (b, 0, 0)),
                  const((1, D_MODEL)), const((D_MODEL, Z_COLS)),
                  tok(LANES), tok(LANES),
                  const((POOL_WIDTH, POOL_WIDTH)), const((1, POOL_WIDTH)),
                  const((1, MLA_Q_LORA)), const((1, MLA_KV_LORA)),
                  const((MLA_Q_LORA, MLA_HEADS * LANES)), const((MLA_KV_LORA, MLA_HEADS * LANES)),
                  const((MLA_KV_LORA, MLA_HEADS * HEAD_DIM)), const((2, MLA_HEADS * LANES)),
                  const((MLA_HEADS * LANES, MLA_HEADS * LANES))],
        out_specs=[tok(ZR_COLS), tok(POOL_WIDTH), tok(MLA_HEADS * LANES), tok(MLA_HEADS * LANES),
                   tok(MLA_HEADS * HEAD_DIM)],
        out_shape=[jax.ShapeDtypeStruct((B, T, ZR_COLS), F32),
                   jax.ShapeDtypeStruct((B, T, POOL_WIDTH), BF16),
                   jax.ShapeDtypeStruct((B, T, MLA_HEADS * LANES), BF16),
                   jax.ShapeDtypeStruct((B, T, MLA_HEADS * LANES), BF16),
                   jax.ShapeDtypeStruct((B, T, MLA_HEADS * HEAD_DIM), BF16)],
        scratch_shapes=[pltpu.VMEM((POOL_HALO + tm, POOL_WIDTH), F32)],
        compiler_params=_cparams(("arbitrary", "arbitrary")),
    )(x, mod_l, gain, win, cosf, sinf, poolw, pools, qg, kvg, wq, wk, wv, qkg, bd128)


WKV_PASSES_SCORE = 1
WKV_PASSES_INV = 1
WKV_PASSES_APPLY = 1
WKV_PASSES_STATE = 1
WKV_STEP_CHUNKS = 4


def _stack_heads(xp, lane):
    return jnp.concatenate([jnp.where(lane < HEAD_DIM, xp, 0.0),
                            jnp.where(lane >= HEAD_DIM, xp, 0.0)], axis=0)


def _wkv_prep(r, lw, k, v, kk, a, tri, masks):
    L = r[0].shape[0]
    n = 2 * L
    nc = len(r)
    each = lambda f, *ls: [f(*xs) for xs in zip(*ls)]
    lane = lax.broadcasted_iota(jnp.int32, (L, LANES), 1)
    stack = lambda x: _stack_heads(x, lane)
    cum = each(lambda x: _mm_exact_rhs_left(tri, x), lw)
    cum_last = each(lambda c: c[L - 1:L, :], cum)
    e_w = each(jnp.exp, cum)
    e_wm = each(lambda c, x: jnp.exp(c - x), cum, lw)
    e_iw = each(lambda c: jnp.exp(-c), cum)
    e_d = each(lambda cl, c: jnp.exp(cl - c), cum_last, cum)
    beta = each(lambda x, y: x * y, kk, a)
    r_t = each(lambda x, e: stack(x * e), r, e_w)
    a_t = each(lambda x, e: stack(-x * e), kk, e_wm)
    b_t = each(lambda x, e: stack(x * e), beta, e_iw)
    k_t = each(lambda x, e: stack(x * e), k, e_iw)
    b_d = each(lambda x, e: stack(x * e), beta, e_d)
    k_d = each(lambda x, e: stack(x * e), k, e_d)
    v_s = each(stack, v)
    yield
    g = each(lambda at, rt, bt, kt: _mm(jnp.concatenate([at, rt], axis=0),
                                        jnp.concatenate([bt, kt], axis=0), NT, WKV_PASSES_SCORE),
             a_t, r_t, b_t, k_t)
    strict, incl, levels = masks
    a_ab = each(lambda x: jnp.where(strict, x[:n, :n], 0.0), g)
    a_ak = each(lambda x: jnp.where(strict, x[:n, n:], 0.0), g)
    s_rb = each(lambda x: jnp.where(incl, x[n:, :n], 0.0), g)
    s_rk = each(lambda x: jnp.where(incl, x[n:, n:], 0.0), g)
    eye = jnp.where(levels[0][1], 1.0, 0.0)
    tinv = each(lambda x: eye + jnp.where(levels[0][0], x, 0.0), a_ab)
    yield
    for lvl_mask, _ in levels[1:]:
        et = each(lambda x, t: _mm(jnp.where(lvl_mask, x, 0.0), t, NN, WKV_PASSES_INV), a_ab, tinv)
        tinv = each(lambda t, x: t + _mm(t, x, NN, WKV_PASSES_INV), tinv, et)
        yield
    av = each(lambda x, y: _mm(x, y, NN, WKV_PASSES_APPLY), a_ak, v_s)
    tx = each(lambda t, x, y: _mm(t, jnp.concatenate([x, y], axis=1), NN, WKV_PASSES_APPLY),
              tinv, a_t, av)
    yield
    ra = each(lambda rt, s, x: rt + _mm(s, x[:, :LANES], NN, WKV_PASSES_APPLY), r_t, s_rb, tx)
    c2 = each(lambda sb, sk, x, vs: _mm(jnp.concatenate([sb, sk], axis=1),
                                        jnp.concatenate([x[:, LANES:], vs], axis=0),
                                        NN, WKV_PASSES_APPLY), s_rb, s_rk, tx, v_s)
    yield
    tb = each(lambda x, bd: _mm(x.T, bd, NN, WKV_PASSES_APPLY), tx, b_d)
    c3 = each(lambda x, vs, kd: x[LANES:] + _mm(vs.T, kd, NN, WKV_PASSES_APPLY), tb, v_s, k_d)
    return [(ra[i], c2[i], jnp.exp(cum_last[i]), tb[i][:LANES], c3[i]) for i in range(nc)]


def _mm_exact_rhs_left(tri_bf16, x):
    x0 = x.astype(BF16)
    r1 = x - x0.astype(F32)
    x1 = r1.astype(BF16)
    x2 = (r1 - x1.astype(F32)).astype(BF16)
    return _dot(tri_bf16, x0) + (_dot(tri_bf16, x1) + _dot(tri_bf16, x2))


def _wkv_masks(L):
    n = 2 * L
    row = lax.broadcasted_iota(jnp.int32, (n, n), 0)
    col = lax.broadcasted_iota(jnp.int32, (n, n), 1)
    strict = row > col
    incl = row >= col
    levels = []
    m = 1
    while m < L:
        same = (row // (2 * m)) == (col // (2 * m))
        lvl = same & ((row % (2 * m)) >= m) & ((col % (2 * m)) < m)
        levels.append((lvl, row == col))
        m *= 2
    return strict, incl, levels


def _rwkv_kernel(has_vres, *refs):
    if has_vres:
        (z_ref, vf_ref, mu_ref, vec_ref, w2a2_ref, g2v2_ref, bd_ref,
         y_ref, carry, state) = refs
    else:
        (z_ref, mu_ref, vec_ref, w2a2_ref, g2v2_ref, bd_ref,
         y_ref, vout_ref, carry, state) = refs
    c = pl.program_id(1)
    rows = z_ref.shape[1]
    L = WKV_CHUNK
    W = RWKV_WIDTH

    @pl.when(c == 0)
    def _():
        carry[...] = jnp.zeros(carry.shape, F32)
        state[...] = jnp.zeros(state.shape, F32)

    z = z_ref[0]
    row = lax.broadcasted_iota(jnp.int32, z.shape, 0)
    prev = jnp.where(row == 0, carry[SUBLANES - 1:SUBLANES, :], pltpu.roll(z, 1, axis=0))
    carry[...] = z[rows - SUBLANES:, :]
    zs_all = z + mu_ref[...] * (prev - z)
    vec = vec_ref[...]
    w0, a0, k_k, k_a, r_k, ln_g, ln_b, v0 = (vec[j:j + 1] for j in range(8))
    bd = bd_ref[...]
    masks = _wkv_masks(L)
    rowt = lax.broadcasted_iota(jnp.int32, (L, L), 0)
    colt = lax.broadcasted_iota(jnp.int32, (L, L), 1)
    tri = jnp.where(rowt >= colt, 1.0, 0.0).astype(BF16)
    n_pairs = RWKV_HEADS // 2
    S_now = [[state[p] for p in range(n_pairs)]]

    def group(r0, r1):
        zs = zs_all[r0:r1]
        r = zs[:, 0:W]
        k = zs[:, W:2 * W]
        v = zs[:, 2 * W:3 * W]
        wa = zs[:, 3 * W:3 * W + LANES]
        gb = zs[:, 3 * W + LANES:ZR_COLS]
        lane_a = lax.broadcasted_iota(jnp.int32, wa.shape, 1)
        t1 = _dot(jnp.where(lane_a < RWKV_DECAY_LORA, jnp.tanh(wa), wa).astype(BF16), w2a2_ref[...])
        lane_g = lax.broadcasted_iota(jnp.int32, gb.shape, 1)
        t2 = _dot(jnp.where(lane_g < RWKV_GATE_LORA, _sigmoid(gb), gb).astype(BF16), g2v2_ref[...])
        yield
        xw = w0 + t1[:, :W]
        w_log = -(jnp.maximum(-xw, 0.0) + jnp.log1p(jnp.exp(-jnp.abs(xw)))) - 0.5
        lw = -jnp.exp(w_log)
        a = _sigmoid(a0 + t1[:, W:])
        g = t2[:, :W]
        if has_vres:
            v = v + (vf_ref[0, r0:r1, :] - v) * _sigmoid(v0 + t2[:, W:])
        else:
            vout_ref[0, r0:r1, :] = v
        kk = k * k_k
        nrm = jnp.sqrt(_mm_exact_rhs(kk * kk, bd))
        kk = kk / jnp.maximum(nrm, 1e-12)
        k = k * (1.0 + (a - 1.0) * k_a)
        yield
        n_chunks = (r1 - r0) // L
        idx = [(ch, p) for ch in range(n_chunks) for p in range(n_pairs)]
        cut = lambda x: [x[ch * L:(ch + 1) * L, p * LANES:(p + 1) * LANES] for ch, p in idx]
        res = yield from _wkv_prep(cut(r), cut(lw), cut(k), cut(v), cut(kk), cut(a), tri, masks)
        prep = dict(zip(idx, res))
        yield
        S = S_now[0]
        y_rows = []
        for ch in range(n_chunks):
            y_s = [_mm(prep[ch, p][0], S[p], NT, WKV_PASSES_STATE) + prep[ch, p][1] for p in range(n_pairs)]
            y_rows.append(jnp.concatenate([x[:L] + x[L:] for x in y_s], axis=1))
            S = [S[p] * prep[ch, p][2] + _mm(S[p], prep[ch, p][3], NN, WKV_PASSES_STATE) + prep[ch, p][4]
                 for p in range(n_pairs)]
        S_now[0] = S
        y = jnp.concatenate(y_rows, axis=0)
        yield
        inv = 1.0 / HEAD_DIM
        mean = _mm_exact_rhs(y, bd) * inv
        yc = y - mean
        var = _mm_exact_rhs(yc * yc, bd) * inv
        yn = yc * lax.rsqrt(var + RWKV_LNX_EPS) * ln_g + ln_b
        bonus = _mm_exact_rhs(r * k * r_k, bd) * v
        y_ref[0, r0:r1, :] = ((yn + bonus) * g).astype(BF16)

    for _ in group(0, rows):
        pass
    for p in range(n_pairs):
        state[p] = S_now[0][p]


def _rwkv(zr, v_first, mu, vec8, w2a2, g2v2, bd64):
    B, T, _ = zr.shape
    L = WKV_CHUNK * WKV_STEP_CHUNKS
    has_vres = v_first is not None
    const = lambda shape: pl.BlockSpec(shape, lambda b, c: tuple(0 for _ in shape))
    tok = lambda w: pl.BlockSpec((1, L, w), lambda b, c: (b, c, 0))
    in_specs = [tok(ZR_COLS)]
    args = [zr]
    if has_vres:
        in_specs.append(tok(RWKV_WIDTH))
        args.append(v_first)
    in_specs += [const((1, ZR_COLS)), const((8, RWKV_WIDTH)), const((LANES, 2 * RWKV_WIDTH)),
                 const((2 * LANES, 2 * RWKV_WIDTH)), const((RWKV_WIDTH, RWKV_WIDTH))]
    args += [mu, vec8, w2a2, g2v2, bd64]
    out_specs = [tok(RWKV_WIDTH)]
    out_shape = [jax.ShapeDtypeStruct((B, T, RWKV_WIDTH), BF16)]
    if not has_vres:
        out_specs.append(tok(RWKV_WIDTH))
        out_shape.append(jax.ShapeDtypeStruct((B, T, RWKV_WIDTH), F32))
    outs = pl.pallas_call(
        functools.partial(_rwkv_kernel, has_vres),
        grid=(B, T // L),
        in_specs=in_specs, out_specs=out_specs, out_shape=out_shape,
        scratch_shapes=[pltpu.VMEM((SUBLANES, ZR_COLS), F32),
                        pltpu.VMEM((RWKV_HEADS // 2, LANES, LANES), F32)],
        compiler_params=_cparams(("arbitrary", "arbitrary")),
    )(*args)
    return (outs[0], v_first) if has_vres else (outs[0], outs[1])


def _attn_step(q_ref, k_ref, v_ref, m_sc, l_sc, acc_sc, masked):
    tq = q_ref.shape[1]
    tk = k_ref.shape[1]
    lane = lax.broadcasted_iota(jnp.int32, (tq, LANES), 1)
    if masked:
        rowi = lax.broadcasted_iota(jnp.int32, (tq, tk), 0)
        coli = lax.broadcasted_iota(jnp.int32, (tq, tk), 1)
        keep = coli <= rowi
    heads = range(MLA_HEADS)
    s = [_dot(q_ref[0, :, hd * LANES:(hd + 1) * LANES], k_ref[0, :, hd * LANES:(hd + 1) * LANES], NT)
         for hd in heads]
    if masked:
        s = [jnp.where(keep, x, NEG_INF) for x in s]
    m_prev = [m_sc[hd] for hd in heads]
    m_new = [jnp.maximum(m_prev[hd], jnp.max(s[hd], axis=-1, keepdims=True)) for hd in heads]
    alpha = [jnp.exp(m_prev[hd] - m_new[hd]) for hd in heads]
    p = [jnp.exp(s[hd] - jnp.concatenate([m_new[hd]] * (tk // LANES), axis=1)) for hd in heads]
    for hd in heads:
        l_sc[hd] = alpha[hd] * l_sc[hd] + jnp.sum(p[hd], axis=-1, keepdims=True)
        m_sc[hd] = m_new[hd]
    pv = [_dot(p[hd].astype(BF16), v_ref[0, :, (hd // 2) * LANES:(hd // 2 + 1) * LANES]) for hd in heads]
    first = lane < HEAD_DIM
    for pr in range(MLA_HEADS // 2):
        acc_sc[pr] = (acc_sc[pr] * jnp.where(first, alpha[2 * pr], alpha[2 * pr + 1])
                      + jnp.where(first, pv[2 * pr], pv[2 * pr + 1]))


def _attn_step_bounded(q_ref, k_ref, v_ref, cb_ref, l_sc, acc_sc, masked):
    tq = q_ref.shape[1]
    tk = k_ref.shape[1]
    lane = lax.broadcasted_iota(jnp.int32, (tq, LANES), 1)
    if masked:
        rowi = lax.broadcasted_iota(jnp.int32, (tq, tk), 0)
        coli = lax.broadcasted_iota(jnp.int32, (tq, tk), 1)
        keep = coli <= rowi
    pv = []
    for hd in range(MLA_HEADS):
        s = _dot(q_ref[0, :, hd * LANES:(hd + 1) * LANES], k_ref[0, :, hd * LANES:(hd + 1) * LANES], NT)
        c = cb_ref[0, hd:hd + 1, :]
        p = jnp.exp(s - jnp.concatenate([c] * (tk // LANES), axis=1))
        if masked:
            p = jnp.where(keep, p, 0.0)
        part = p[:, 0:LANES]
        for t in range(1, tk // LANES):
            part = part + p[:, t * LANES:(t + 1) * LANES]
        l_sc[hd] = l_sc[hd] + part
        pv.append(_dot(p.astype(BF16), v_ref[0, :, (hd // 2) * LANES:(hd // 2 + 1) * LANES]))
    first = lane < HEAD_DIM
    for pr in range(MLA_HEADS // 2):
        acc_sc[pr] = acc_sc[pr] + jnp.where(first, pv[2 * pr], pv[2 * pr + 1])


def _attn_finish(o_ref, l_sc, acc_sc, lane_partial):
    tq = o_ref.shape[1]
    lane = lax.broadcasted_iota(jnp.int32, (tq, LANES), 1)
    outs = []
    for pr in range(MLA_HEADS // 2):
        la, lb = l_sc[2 * pr], l_sc[2 * pr + 1]
        if lane_partial:
            la = jnp.sum(la, axis=-1, keepdims=True)
            lb = jnp.sum(lb, axis=-1, keepdims=True)
        outs.append(acc_sc[pr] / jnp.where(lane < HEAD_DIM, la, lb))
    o_ref[0] = jnp.concatenate(outs, axis=1).astype(BF16)


def _attn_kernel(ok_ref, qi_ref, kj_ref, q_ref, k_ref, v_ref, cb_ref, o_ref, m_sc, l_sc, acc_sc):
    i = qi_ref[pl.program_id(1)]
    j = kj_ref[pl.program_id(1)]
    bounded = ok_ref[pl.program_id(0)] == 1
    exact = jnp.logical_not(bounded)

    @pl.when(j == 0)
    def _():
        m_sc[...] = jnp.full(m_sc.shape, NEG_INF, F32)
        l_sc[...] = jnp.zeros(l_sc.shape, F32)
        acc_sc[...] = jnp.zeros(acc_sc.shape, F32)

    @pl.when(bounded & (j < i))
    def _():
        _attn_step_bounded(q_ref, k_ref, v_ref, cb_ref, l_sc, acc_sc, masked=False)

    @pl.when(bounded & (j == i))
    def _():
        _attn_step_bounded(q_ref, k_ref, v_ref, cb_ref, l_sc, acc_sc, masked=True)
        _attn_finish(o_ref, l_sc, acc_sc, lane_partial=True)

    @pl.when(exact & (j < i))
    def _():
        _attn_step(q_ref, k_ref, v_ref, m_sc, l_sc, acc_sc, masked=False)

    @pl.when(exact & (j == i))
    def _():
        _attn_step(q_ref, k_ref, v_ref, m_sc, l_sc, acc_sc, masked=True)
        _attn_finish(o_ref, l_sc, acc_sc, lane_partial=False)


def _attention(q, k, v, qk_gain):
    B, T, _ = q.shape
    nq = T // TQ
    gmax = jnp.max(jnp.abs(qk_gain), axis=1)
    c = gmax[0] * gmax[1] * (MLA_QK_DIM ** 0.5) * ATTN_BOUND_SLACK
    ok = jnp.broadcast_to((c <= ATTN_BOUND_MAX).astype(jnp.int32), (B,))
    cb = jnp.broadcast_to(c, (B, MLA_HEADS, LANES))
    pairs = [(i, j) for i in range(nq) for j in range(i + 1)]
    qi = jnp.asarray([p[0] for p in pairs], jnp.int32)
    kj = jnp.asarray([p[1] for p in pairs], jnp.int32)
    grid_spec = pltpu.PrefetchScalarGridSpec(
        num_scalar_prefetch=3,
        grid=(B, len(pairs)),
        in_specs=[pl.BlockSpec((1, TQ, MLA_HEADS * LANES), lambda b, t, ok, qi, kj: (b, qi[t], 0)),
                  pl.BlockSpec((1, TQ, MLA_HEADS * LANES), lambda b, t, ok, qi, kj: (b, kj[t], 0)),
                  pl.BlockSpec((1, TQ, MLA_HEADS * HEAD_DIM), lambda b, t, ok, qi, kj: (b, kj[t], 0)),
                  pl.BlockSpec((1, MLA_HEADS, LANES), lambda b, t, ok, qi, kj: (b, 0, 0))],
        out_specs=pl.BlockSpec((1, TQ, MLA_HEADS * HEAD_DIM), lambda b, t, ok, qi, kj: (b, qi[t], 0)),
        scratch_shapes=[pltpu.VMEM((MLA_HEADS, TQ, LANES), F32),
                        pltpu.VMEM((MLA_HEADS, TQ, LANES), F32),
                        pltpu.VMEM((MLA_HEADS // 2, TQ, LANES), F32)])
    return pl.pallas_call(
        _attn_kernel,
        grid_spec=grid_spec,
        out_shape=jax.ShapeDtypeStruct((B, T, MLA_HEADS * HEAD_DIM), BF16),
        compiler_params=_cparams(("arbitrary", "arbitrary")),
    )(ok, qi, kj, q, k, v, cb)


def _mixout_kernel(has_router, *refs):
    if has_router:
        (x_ref, yr_ref, yp_ref, ym_ref, wo_ref, mod_ref, gain_ref, rt_ref, tri_ref,
         xo_ref, h_ref, route_ref, cnt_ref, cnt_sc) = refs
    else:
        x_ref, yr_ref, yp_ref, ym_ref, wo_ref, mod_ref, gain_ref, xo_ref, h_ref = refs
    mod = mod_ref[0]
    o1 = RWKV_WIDTH
    o2 = RWKV_WIDTH + POOL_WIDTH
    mix = (_dot(yr_ref[0], wo_ref[0:o1, :]) + _dot(yp_ref[0], wo_ref[o1:o2, :])
           + _dot(ym_ref[0], wo_ref[o2:, :]))
    x = x_ref[0] + mod[2:3] * mix
    xo_ref[0] = x
    h = _rms(x) * gain_ref[...] * (1.0 + mod[4:5]) + mod[3:4]
    if not has_router:
        h_ref[0] = h.astype(BF16)
    else:
        hp = _pack_bf16_pairs(h)
        for ck in range(ROW_CHUNKS):
            h_ref[ck, 0] = hp[:, ck * PACK_CHUNK_W:(ck + 1) * PACK_CHUNK_W]
        logits = _mm(h, rt_ref[...])
        lane = lax.broadcasted_iota(jnp.int32, logits.shape, 1).astype(F32)
        lg = jnp.where(lane < N_EXPERTS, logits, -jnp.inf)
        m1 = jnp.max(lg, axis=-1, keepdims=True)
        i1 = jnp.min(jnp.where(lg == m1, lane, float(LANES)), axis=-1, keepdims=True)
        lg2 = jnp.where(lane == i1, -jnp.inf, lg)
        m2 = jnp.max(lg2, axis=-1, keepdims=True)
        i2 = jnp.min(jnp.where(lg2 == m2, lane, float(LANES)), axis=-1, keepdims=True)
        e2 = jnp.exp(m2 - m1)
        g1 = 1.0 / (1.0 + e2)
        g2 = e2 / (1.0 + e2)
        first = (pl.program_id(0) == 0) & (pl.program_id(1) == 0)

        @pl.when(first)
        def _():
            cnt_sc[...] = jnp.zeros(cnt_sc.shape, F32)

        hit1 = lane == i1
        hit2 = lane == i2
        onehot = jnp.where(hit1 | hit2, 1.0, 0.0)
        prefix = _dot(tri_ref[...], onehot.astype(BF16)) + cnt_sc[0:1, :]
        r1 = jnp.sum(jnp.where(hit1, prefix, 0.0), axis=-1, keepdims=True)
        r2 = jnp.sum(jnp.where(hit2, prefix, 0.0), axis=-1, keepdims=True)
        cnt_sc[...] = cnt_sc[...] + jnp.sum(onehot, axis=0, keepdims=True)
        cnt_ref[...] = cnt_sc[...]
        vals = (i1, i2, g1, g2, r1, r2)
        route = jnp.zeros(logits.shape, F32)
        for pos, val in enumerate(vals):
            route = jnp.where(lane == pos, val, route)
        route_ref[0] = route


def _mixout(x, yr, yp, ym, wo, mod_l, gain, router_p):
    B, T, _ = x.shape
    tm = TM_MIX
    has_router = router_p is not None
    const = lambda shape: pl.BlockSpec(shape, lambda b, i: tuple(0 for _ in shape))
    tok = lambda w: pl.BlockSpec((1, tm, w), lambda b, i: (b, i, 0))
    in_specs = [tok(D_MODEL), tok(RWKV_WIDTH), tok(POOL_WIDTH), tok(MLA_HEADS * HEAD_DIM),
                const((D_MODEL, D_MODEL)), pl.BlockSpec((1, 6, D_MODEL), lambda b, i: (b, 0, 0)),
                const((1, D_MODEL))]
    args = [x, yr, yp, ym, wo, mod_l, gain]
    out_specs = [tok(D_MODEL), tok(D_MODEL)]
    out_shape = [jax.ShapeDtypeStruct((B, T, D_MODEL), F32), jax.ShapeDtypeStruct((B, T, D_MODEL), BF16)]
    scratch = []
    if has_router:
        out_specs[1] = pl.BlockSpec((ROW_CHUNKS, 1, tm, PACK_CHUNK_W), lambda b, i: (0, b, i, 0))
        out_shape[1] = jax.ShapeDtypeStruct((ROW_CHUNKS, B, T, PACK_CHUNK_W), jnp.uint32)
        ids = np.arange(tm)
        tri = jnp.asarray(ids[:, None] > ids[None, :], BF16)
        in_specs += [const((D_MODEL, LANES)), const((tm, tm))]
        args += [router_p, tri]
        out_specs += [tok(LANES), const((SUBLANES, LANES))]
        out_shape += [jax.ShapeDtypeStruct((B, T, LANES), F32),
                      jax.ShapeDtypeStruct((SUBLANES, LANES), F32)]
        scratch = [pltpu.VMEM((SUBLANES, LANES), F32)]
    return pl.pallas_call(
        functools.partial(_mixout_kernel, has_router),
        grid=(B, T // tm),
        in_specs=in_specs, out_specs=out_specs, out_shape=out_shape, scratch_shapes=scratch,
        compiler_params=_cparams(("arbitrary", "arbitrary")),
    )(*args)


def _ffn_kernel(h_ref, wg_ref, wu_ref, wo_ref, x_ref, mod_ref, o_ref, acc):
    j = pl.program_id(1)

    @pl.when(j == 0)
    def _():
        acc[...] = jnp.zeros(acc.shape, F32)

    h = h_ref[...]
    gg = _dot(h, wg_ref[...])
    uu = _dot(h, wu_ref[...])
    acc[...] += _dot((_silu(gg) * uu).astype(BF16), wo_ref[...])

    @pl.when(j == pl.num_programs(1) - 1)
    def _():
        o_ref[...] = x_ref[...] + mod_ref[0][5:6] * acc[...]


def _ffn(h2, w_in, w_out, x, mod_l):
    N = h2.shape[0]
    T = N // mod_l.shape[0]
    tm, tf = TM_FFN, TF_FFN
    nf = D_FF // tf
    per_b = T // tm
    return pl.pallas_call(
        _ffn_kernel,
        grid=(N // tm, nf),
        in_specs=[pl.BlockSpec((tm, D_MODEL), lambda i, j: (i, 0)),
                  pl.BlockSpec((D_MODEL, tf), lambda i, j: (0, j)),
                  pl.BlockSpec((D_MODEL, tf), lambda i, j: (0, j + nf)),
                  pl.BlockSpec((tf, D_MODEL), lambda i, j: (j, 0)),
                  pl.BlockSpec((tm, D_MODEL), lambda i, j: (i, 0)),
                  pl.BlockSpec((1, 6, D_MODEL), lambda i, j: (i // per_b, 0, 0))],
        out_specs=pl.BlockSpec((tm, D_MODEL), lambda i, j: (i, 0)),
        out_shape=jax.ShapeDtypeStruct((N, D_MODEL), F32),
        scratch_shapes=[pltpu.VMEM((tm, D_MODEL), F32)],
        compiler_params=_cparams(("arbitrary", "arbitrary")),
    )(h2, w_in, w_in, w_out, x, mod_l)


def _moe_kernel(be_ref, nv_ref, last_ref, x_ref, wg_ref, wu_ref, wo_ref, o_ref, acc, xm):
    i = pl.program_id(0)
    j = pl.program_id(1)

    @pl.when(i <= last_ref[0])
    def _():
        @pl.when(j == 0)
        def _():
            acc[...] = jnp.zeros(acc.shape, F32)
            row = lax.broadcasted_iota(jnp.int32, (xm.shape[0], 1), 0)
            xp = jnp.concatenate([x_ref[ck] for ck in range(ROW_CHUNKS)], axis=1)
            xp = jnp.where(row < nv_ref[i], xp, jnp.uint32(0))
            xm[...] = _unpack_bf16_pairs(xp)

        x = xm[...]
        gg = _dot(x, wg_ref[0].astype(BF16))
        uu = _dot(x, wu_ref[0].astype(BF16))
        acc[...] += _dot((_silu(gg) * uu).astype(BF16), wo_ref[0].astype(BF16))

        @pl.when(j == pl.num_programs(1) - 1)
        def _():
            for ck in range(ROW_CHUNKS):
                o_ref[ck] = acc[:, ck * ROW_CHUNK_W:(ck + 1) * ROW_CHUNK_W]


def _moe_experts(xs, w_in, w_out, block_exp, n_valid, last_blk):
    n_rows = xs.shape[1]
    tm, tf = MOE_BLOCK, TF_MOE
    nf = D_FF_EXPERT // tf
    blk = lambda i, last: jnp.minimum(i, last[0])
    chunk = lambda i, j, last: jnp.where(i <= last[0], j, nf - 1)
    grid_spec = pltpu.PrefetchScalarGridSpec(
        num_scalar_prefetch=3,
        grid=(n_rows // tm, nf),
        in_specs=[pl.BlockSpec((ROW_CHUNKS, tm, PACK_CHUNK_W),
                               lambda i, j, be, nv, last: (0, blk(i, last), 0)),
                  pl.BlockSpec((1, D_MODEL, tf),
                               lambda i, j, be, nv, last: (be[blk(i, last)], 0, chunk(i, j, last))),
                  pl.BlockSpec((1, D_MODEL, tf),
                               lambda i, j, be, nv, last: (be[blk(i, last)], 0, chunk(i, j, last) + nf)),
                  pl.BlockSpec((1, tf, D_MODEL),
                               lambda i, j, be, nv, last: (be[blk(i, last)], chunk(i, j, last), 0))],
        out_specs=pl.BlockSpec((ROW_CHUNKS, tm, ROW_CHUNK_W), lambda i, j, be, nv, last: (0, blk(i, last), 0)),
        scratch_shapes=[pltpu.VMEM((tm, D_MODEL), F32), pltpu.VMEM((tm, D_MODEL), BF16)])
    return pl.pallas_call(
        _moe_kernel,
        grid_spec=grid_spec,
        out_shape=jax.ShapeDtypeStruct((ROW_CHUNKS, n_rows, ROW_CHUNK_W), F32),
        compiler_params=_cparams(("arbitrary", "arbitrary")),
    )(block_exp, n_valid, last_blk, xs, w_in, w_in, w_out)


def _sc_mesh():
    return plsc.VectorSubcoreMesh(core_axis_name="c", subcore_axis_name="s")


def _sc_scatter_rows(x, dest, n_rows):
    N, D = x.shape
    K = dest.shape[0]
    win = SC_WINDOW

    @pl.kernel(out_type=jax.ShapeDtypeStruct((n_rows, D), x.dtype), mesh=_sc_mesh(), scratch_types=[])
    def scatter(x_hbm, d_hbm, o_hbm):
        def body(x_vmem, *idx_vmem):
            for iv in idx_vmem:
                pltpu.sync_copy(x_vmem, o_hbm.at[iv.at[0]])

        pltpu.emit_pipeline(
            body,
            grid=(N // win,),
            in_specs=[pl.BlockSpec((win, D), lambda i: (i, 0))]
            + [pl.BlockSpec((1, win), functools.partial(lambda k, i: (k, i), k)) for k in range(K)],
            out_specs=[],
            core_axis_name=("c", "s"),
            dimension_semantics=(pltpu.PARALLEL,),
        )(x_hbm, *([d_hbm] * K))

    return scatter(x, dest)


def _sc_gather_rows(x, idx):
    n = idx.shape[0]
    D = x.shape[1]
    win = SC_WINDOW

    @pl.kernel(out_type=jax.ShapeDtypeStruct((n, D), x.dtype), mesh=_sc_mesh(), scratch_types=[])
    def gather(x_hbm, i_hbm, o_hbm):
        def body(i_vmem, o_vmem):
            pltpu.sync_copy(x_hbm.at[i_vmem.at[0]], o_vmem)

        pltpu.emit_pipeline(
            body,
            grid=(n // win,),
            in_specs=[pl.BlockSpec((1, win), lambda i: (0, i))],
            out_specs=[pl.BlockSpec((win, D), lambda i: (i, 0))],
            core_axis_name=("c", "s"),
            dimension_semantics=(pltpu.PARALLEL,),
        )(i_hbm, o_hbm)

    return gather(x, idx.reshape(1, n))


def _combine_kernel(x_ref, ya_ref, yb_ref, route_ref, mod_ref, o_ref):
    rt = route_ref[...]
    ya = jnp.concatenate([ya_ref[0, ck] for ck in range(ROW_CHUNKS)], axis=1)
    yb = jnp.concatenate([yb_ref[0, ck] for ck in range(ROW_CHUNKS)], axis=1)
    f = rt[:, 2:3] * ya + rt[:, 3:4] * yb
    o_ref[...] = x_ref[...] + mod_ref[0][5:6] * f


def _combine_part_kernel(x_ref, ya_ref, yb_ref, route_ref, mod_ref, prev_ref, o_ref):
    del prev_ref
    _combine_kernel(x_ref, ya_ref, yb_ref, route_ref, mod_ref, o_ref)


def _combine(x, y2, route, mod_l, part, n_parts, prev):
    N = x.shape[0]
    T = N // mod_l.shape[0]
    tm = 1024
    per_b = T // tm
    steps = N // tm // n_parts
    off = part * steps
    tok = pl.BlockSpec((tm, D_MODEL), lambda i: (i + off, 0))
    slot = lambda k: pl.BlockSpec((1, ROW_CHUNKS, tm, ROW_CHUNK_W), lambda i: (k, 0, i, 0))
    in_specs = [tok, slot(0), slot(1),
                pl.BlockSpec((tm, LANES), lambda i: (i + off, 0)),
                pl.BlockSpec((1, 6, D_MODEL), lambda i: ((i + off) // per_b, 0, 0))]
    args = [x, y2, y2, route, mod_l]
    kern, alias = _combine_kernel, {}
    if prev is not None:
        in_specs.append(pl.BlockSpec(memory_space=pl.ANY))
        args.append(prev)
        kern, alias = _combine_part_kernel, {len(args) - 1: 0}
    return pl.pallas_call(
        kern,
        grid=(steps,),
        in_specs=in_specs,
        out_specs=tok,
        out_shape=jax.ShapeDtypeStruct((N, D_MODEL), F32),
        input_output_aliases=alias,
        compiler_params=_cparams(("arbitrary",)),
    )(*args)


def _moe(h2, route, counts, w_in, w_out, x, mod_l):
    N = x.shape[0]
    blk = MOE_BLOCK
    cnt = counts[0, :N_EXPERTS].astype(jnp.int32)
    padded = (cnt + blk - 1) // blk * blk
    pend = jnp.cumsum(padded)
    pstart = pend - padded
    e = route[:, 0:TOP_K].astype(jnp.int32)
    rank = route[:, 2 * TOP_K:3 * TOP_K].astype(jnp.int32)
    dest = (jnp.take(pstart, e) + rank).T
    n_blocks = N * TOP_K // blk + N_EXPERTS
    bstart = jnp.arange(n_blocks, dtype=jnp.int32) * blk
    block_exp = jnp.clip(jnp.searchsorted(pend, bstart, side='right'), 0, N_EXPERTS - 1).astype(jnp.int32)
    n_valid = jnp.clip(cnt[block_exp] - (bstart - pstart[block_exp]), 0, blk).astype(jnp.int32)
    n_rows = n_blocks * blk
    dest_ck = dest[:, None, :] + (jnp.arange(ROW_CHUNKS, dtype=jnp.int32) * n_rows)[None, :, None]
    xs = _sc_scatter_rows(h2, dest_ck.reshape(TOP_K, ROW_CHUNKS * N), ROW_CHUNKS * n_rows)
    last_blk = (pend[-1:] // blk - 1).astype(jnp.int32)
    yb = _moe_experts(xs.reshape(ROW_CHUNKS, n_rows, PACK_CHUNK_W), w_in, w_out, block_exp, n_valid,
                      last_blk)
    yb = yb.reshape(ROW_CHUNKS * n_rows, ROW_CHUNK_W)
    n_part = N // COMBINE_PARTS
    out = None
    for part in range(COMBINE_PARTS):
        idx = dest_ck[:, :, part * n_part:(part + 1) * n_part].reshape(-1)
        y2 = _sc_gather_rows(yb, idx).reshape(TOP_K, ROW_CHUNKS, n_part, ROW_CHUNK_W)
        out = _combine(x, y2, route, mod_l, part, COMBINE_PARTS, out)
    return out


def _layout_w_in(w, has_vres):
    W = RWKV_WIDTH
    off_gd = 3 * W + RWKV_DECAY_LORA + RWKV_ICLR_LORA
    off_pool = off_gd + RWKV_GATE_LORA
    off_q = off_pool + POOL_WIDTH
    off_kv = off_q + MLA_Q_LORA
    off_kr = off_kv + MLA_KV_LORA
    n_base = off_kr + MLA_QK_ROPE
    d = w.shape[0]
    zeros = lambda n: jnp.zeros((d, n), w.dtype)
    vd = w[:, n_base:n_base + RWKV_VRES_LORA] if has_vres else zeros(RWKV_VRES_LORA)
    cols = [w[:, :off_gd], w[:, off_gd:off_pool], vd, zeros(ZR_COLS - off_pool - RWKV_VRES_LORA),
            w[:, off_pool:off_q], w[:, off_q:off_kv], w[:, off_kv:off_kr],
            zeros(MLA_QK_NOPE), w[:, off_kr:n_base], zeros(LANES - MLA_QK_DIM)]
    return jnp.concatenate(cols, axis=1).astype(BF16)


def _pad_heads(w, per_head, keep_from, keep_n):
    K = w.shape[0]
    wh = w.reshape(K, MLA_HEADS, per_head)[:, :, keep_from:keep_from + keep_n]
    wh = jnp.pad(wh, ((0, 0), (0, 0), (0, LANES - keep_n)))
    return wh.reshape(K, MLA_HEADS * LANES)


def kernel(x, c, positions, w_ada, b_ada, norm_gain, w_in_first, w_in_rest, mu_shift, mu_shift_v,
           rwkv_vec, rwkv_v0, rwkv_w2, rwkv_a2, rwkv_g2, rwkv_v2, pool_w, pool_scale,
           mla_q_lat_gain, mla_kv_lat_gain, mla_wq_up, mla_wkv_up, mla_qk_gain, w_out, ffn_w_in,
           ffn_w_out, moe_router, moe_w_in, moe_w_out):
    B, T, D = x.shape
    depth = w_ada.shape[0]
    W = RWKV_WIDTH
    mod = _adaln(c, w_ada, b_ada).reshape(depth, B, 6, D)
    pos3 = positions.reshape(B, T, 1)
    inv_freq = ROPE_BASE ** (-jnp.arange(0, MLA_QK_ROPE, 2, dtype=F32) / MLA_QK_ROPE)
    freq = jnp.concatenate([jnp.zeros((MLA_QK_NOPE,), F32), inv_freq, inv_freq,
                            jnp.zeros((LANES - MLA_QK_DIM,), F32)]).reshape(1, LANES)
    cosf, sinf = _rope_tables(pos3, freq)
    hid = np.arange(W) // HEAD_DIM
    bd64 = jnp.asarray(hid[:, None] == hid[None, :], BF16)
    bid = np.arange(MLA_HEADS * LANES) // LANES
    bd128 = jnp.asarray(bid[:, None] == bid[None, :], BF16)

    v_first = None
    for l in range(depth):
        has_vres = l > 0
        mod_l = mod[l]
        win = _layout_w_in(w_in_first if l == 0 else w_in_rest[l - 1], has_vres)
        poolw = jax.scipy.linalg.block_diag(*[pool_w[l, g] for g in range(len(POOL_WINDOWS))]).astype(BF16)
        wq = _pad_heads(mla_wq_up[l], MLA_QK_DIM, 0, MLA_QK_DIM).astype(BF16)
        wk = _pad_heads(mla_wkv_up[l], MLA_QK_NOPE + HEAD_DIM, 0, MLA_QK_NOPE).astype(BF16)
        wv = mla_wkv_up[l].reshape(MLA_KV_LORA, MLA_HEADS, MLA_QK_NOPE + HEAD_DIM)[:, :, MLA_QK_NOPE:]
        wv = wv.reshape(MLA_KV_LORA, MLA_HEADS * HEAD_DIM).astype(BF16)
        qkg = jnp.tile(jnp.pad(mla_qk_gain[l], ((0, 0), (0, LANES - MLA_QK_DIM))), (1, MLA_HEADS))
        zr, y_pool, q, k, v = _mixin(
            x, mod_l, norm_gain[l, 0].reshape(1, D), win, cosf, sinf, poolw,
            pool_scale[l].reshape(1, -1), mla_q_lat_gain[l].reshape(1, -1),
            mla_kv_lat_gain[l].reshape(1, -1), wq, wk, wv, qkg, bd128)

        pad_mu = ZR_COLS - mu_shift.shape[1] - RWKV_VRES_LORA
        mu_v = mu_shift_v[l - 1] if has_vres else jnp.zeros((RWKV_VRES_LORA,), F32)
        mu = jnp.concatenate([mu_shift[l], mu_v, jnp.zeros((pad_mu,), F32)]).reshape(1, ZR_COLS)
        v0 = rwkv_v0[l - 1] if has_vres else jnp.zeros((W,), F32)
        vec8 = jnp.concatenate([rwkv_vec[l], v0[None]], axis=0)
        w2a2 = jax.scipy.linalg.block_diag(rwkv_w2[l], rwkv_a2[l]).astype(BF16)
        g2 = jnp.pad(rwkv_g2[l], ((0, 2 * LANES - RWKV_GATE_LORA), (0, 0)))
        if has_vres:
            v2 = jnp.pad(rwkv_v2[l - 1], ((RWKV_GATE_LORA, 2 * LANES - RWKV_GATE_LORA - RWKV_VRES_LORA), (0, 0)))
        else:
            v2 = jnp.zeros((2 * LANES, W), F32)
        g2v2 = jnp.concatenate([g2, v2], axis=1).astype(BF16)
        y_rwkv, v_first = _rwkv(zr, v_first, mu, vec8, w2a2, g2v2, bd64)

        y_mla = _attention(q, k, v, mla_qk_gain[l])

        is_moe = (l % 2 == 1)
        router_p = None
        if is_moe:
            router_p = jnp.pad(moe_router[l // 2], ((0, 0), (0, LANES - N_EXPERTS)))
        outs = _mixout(x, y_rwkv, y_pool, y_mla, w_out[l].astype(BF16), mod_l,
                       norm_gain[l, 1].reshape(1, D), router_p)
        x_mid, h2 = outs[0], outs[1]
        xf = x_mid.reshape(B * T, D)
        if is_moe:
            xo = _moe(h2.reshape(ROW_CHUNKS * B * T, PACK_CHUNK_W), outs[2].reshape(B * T, LANES), outs[3],
                      moe_w_in[l // 2], moe_w_out[l // 2], xf, mod_l)
        else:
            xo = _ffn(h2.reshape(B * T, D), ffn_w_in[l // 2].astype(BF16), ffn_w_out[l // 2].astype(BF16),
                      xf, mod_l)
        x = xo.reshape(B, T, D)
    return x
```

```python
import functools

import numpy as np
import jax
import jax.numpy as jnp
from jax import lax
from jax.experimental import pallas as pl
from jax.experimental.pallas import tpu as pltpu
from jax.experimental.pallas import tpu_sc as plsc

F32 = jnp.float32
BF16 = jnp.bfloat16

D_MODEL = 1024
HEAD_DIM = 64
RWKV_WIDTH = 512
RWKV_HEADS = 8
POOL_WIDTH = 256
POOL_WINDOWS = (2, 4, 8, 16)
POOL_HALO = 16
MLA_HEADS = 4
MLA_QK_NOPE = 64
MLA_QK_ROPE = 32
MLA_QK_DIM = 96
MLA_Q_LORA = 256
MLA_KV_LORA = 128
ROPE_BASE = 10000.0
RWKV_DECAY_LORA = 64
RWKV_ICLR_LORA = 64
RWKV_VRES_LORA = 32
RWKV_GATE_LORA = 160
RWKV_LNX_EPS = 64e-5
D_FF = 2816
N_EXPERTS = 8
TOP_K = 2
D_FF_EXPERT = 3584
NORM_EPS = 1e-6
NEG_INF = -1e30

LANES = 128
SUBLANES = 8
VMEM_LIMIT = 56 * 1024 * 1024

ZR_COLS = 1920
Z_POOL_OFF = ZR_COLS
Z_QLAT_OFF = Z_POOL_OFF + POOL_WIDTH
Z_KVLAT_OFF = Z_QLAT_OFF + MLA_Q_LORA
Z_KROPE_OFF = Z_KVLAT_OFF + MLA_KV_LORA
Z_COLS = Z_KROPE_OFF + LANES

TM_MIX = 512
WKV_CHUNK = 64
TQ = 512
ATTN_BOUND_SLACK = 1.02
ATTN_BOUND_MAX = 40.0
TM_FFN = 512
TF_FFN = 1408
MOE_BLOCK = 1024
TF_MOE = 512
SC_WINDOW = 128
ROW_CHUNKS = 4
ROW_CHUNK_W = D_MODEL // ROW_CHUNKS
PACK_CHUNK_W = ROW_CHUNK_W // 2

SEGSUM_SPLITS = 1

NN = (((1,), (0,)), ((), ()))
NT = (((1,), (1,)), ((), ()))


def _dot(a, b, dims=NN):
    return lax.dot_general(a, b, dims, preferred_element_type=F32)


def _split2(a):
    hi = a.astype(BF16)
    lo = (a - hi.astype(F32)).astype(BF16)
    return hi, lo


def _mm(a, b, dims=NN, passes=3):
    if passes == 1:
        return _dot(a.astype(BF16), b.astype(BF16), dims)
    ah, al = _split2(a)
    bh, bl = _split2(b)
    return _dot(ah, bh, dims) + (_dot(ah, bl, dims) + _dot(al, bh, dims))


def _mm_exact_rhs(a, b_bf16, dims=NN, splits=SEGSUM_SPLITS):
    out = None
    rem = a
    for s in range(splits):
        part = rem.astype(BF16)
        term = _dot(part, b_bf16, dims)
        out = term if out is None else out + term
        if s + 1 < splits:
            rem = rem - part.astype(F32)
    return out


def _pack_bf16_pairs(h):
    w = h.shape[1] // 2
    lo = lax.bitcast_convert_type(h[:, :w].astype(BF16).astype(F32), jnp.uint32)
    hi = lax.bitcast_convert_type(h[:, w:].astype(BF16).astype(F32), jnp.uint32)
    return (lo >> 16) | (hi & jnp.uint32(0xFFFF0000))


def _unpack_bf16_pairs(p):
    lo = lax.bitcast_convert_type(p << 16, F32)
    hi = lax.bitcast_convert_type(p & jnp.uint32(0xFFFF0000), F32)
    return jnp.concatenate([lo, hi], axis=1).astype(BF16)


def _sigmoid(x):
    return 1.0 / (1.0 + jnp.exp(-x))


def _silu(x):
    return x * _sigmoid(x)


def _rms(x, eps=NORM_EPS):
    return x * lax.rsqrt(jnp.mean(x * x, axis=-1, keepdims=True) + eps)


def _cparams(sem):
    return pltpu.CompilerParams(dimension_semantics=sem, vmem_limit_bytes=VMEM_LIMIT)


def _adaln_kernel(c_ref, w_ref, b_ref, o_ref):
    ca = _silu(c_ref[...])
    o_ref[0] = _mm(ca, w_ref[0]) + b_ref[0]


def _adaln(c, w_ada, b_ada):
    L = w_ada.shape[0]
    B = c.shape[0]
    n = w_ada.shape[2] // D_MODEL
    return pl.pallas_call(
        _adaln_kernel,
        grid=(L, n),
        in_specs=[pl.BlockSpec((B, D_MODEL), lambda l, j: (0, 0)),
                  pl.BlockSpec((1, D_MODEL, D_MODEL), lambda l, j: (l, 0, j)),
                  pl.BlockSpec((1, 1, D_MODEL), lambda l, j: (l, 0, j))],
        out_specs=pl.BlockSpec((1, B, D_MODEL), lambda l, j: (l, 0, j)),
        out_shape=jax.ShapeDtypeStruct((L, B, n * D_MODEL), F32),
        compiler_params=_cparams(("arbitrary", "arbitrary")),
    )(c, w_ada, b_ada.reshape(L, 1, -1))


def _rope(x, cosf, sinf, lane):
    up = pltpu.roll(x, LANES - MLA_QK_ROPE // 2, axis=1)
    dn = pltpu.roll(x, MLA_QK_ROPE // 2, axis=1)
    rot = jnp.where(lane < MLA_QK_NOPE + MLA_QK_ROPE // 2, -up, dn)
    return x * cosf + rot * sinf


def _rope_kernel(pos_ref, freq_ref, cos_ref, sin_ref):
    tm = pos_ref.shape[1]
    lane = lax.broadcasted_iota(jnp.int32, (tm, LANES), 1)
    in_rope = (lane >= MLA_QK_NOPE) & (lane < MLA_QK_DIM)
    ang = pos_ref[0].astype(F32) * freq_ref[...]
    cos_ref[0] = jnp.where(in_rope, jnp.cos(ang), 1.0)
    sin_ref[0] = jnp.where(in_rope, jnp.sin(ang), 0.0)


def _rope_tables(pos3, freq):
    B, T, _ = pos3.shape
    tm = TM_MIX
    tok = lambda w: pl.BlockSpec((1, tm, w), lambda b, i: (b, i, 0))
    return pl.pallas_call(
        _rope_kernel,
        grid=(B, T // tm),
        in_specs=[tok(1), pl.BlockSpec((1, LANES), lambda b, i: (0, 0))],
        out_specs=[tok(LANES), tok(LANES)],
        out_shape=[jax.ShapeDtypeStruct((B, T, LANES), F32)] * 2,
        compiler_params=_cparams(("arbitrary", "arbitrary")),
    )(pos3, freq)


def _mixin_kernel(x_ref, mod_ref, gain_ref, win_ref, cos_ref, sin_ref, poolw_ref, pools_ref,
                  qg_ref, kvg_ref, wq_ref, wk_ref, wv_ref, qkg_ref, bd_ref,
                  zr_ref, yp_ref, q_ref, k_ref, v_ref, ubuf):
    i = pl.program_id(1)
    tm = x_ref.shape[1]
    x = x_ref[0]
    mod = mod_ref[0]
    h = _rms(x) * gain_ref[...] * (1.0 + mod[1:2]) + mod[0:1]
    z = _dot(h.astype(BF16), win_ref[...])
    zr_ref[0] = z[:, :ZR_COLS]

    @pl.when(i == 0)
    def _():
        ubuf[0:POOL_HALO, :] = jnp.zeros((POOL_HALO, POOL_WIDTH), F32)

    u = z[:, Z_POOL_OFF:Z_POOL_OFF + POOL_WIDTH]
    ubuf[POOL_HALO:, :] = u
    ue = ubuf[...]
    s2 = ue + pltpu.roll(ue, 1, axis=0)
    s4 = s2 + pltpu.roll(s2, 2, axis=0)
    s8 = s4 + pltpu.roll(s4, 4, axis=0)
    s16 = s8 + pltpu.roll(s8, 8, axis=0)
    ubuf[0:POOL_HALO, :] = u[tm - POOL_HALO:, :]
    lane_p = lax.broadcasted_iota(jnp.int32, (tm, POOL_WIDTH), 1)
    grp = lane_p // (POOL_WIDTH // len(POOL_WINDOWS))
    win_sum = jnp.where(grp == 0, s2[POOL_HALO:], jnp.where(grp == 1, s4[POOL_HALO:],
                        jnp.where(grp == 2, s8[POOL_HALO:], s16[POOL_HALO:])))
    win = jnp.where(grp == 0, 2, jnp.where(grp == 1, 4, jnp.where(grp == 2, 8, 16)))
    t_abs = i * tm + lax.broadcasted_iota(jnp.int32, (tm, POOL_WIDTH), 0)
    cnt = jnp.minimum(t_abs + 1, win).astype(F32)
    p = win_sum / cnt - u
    yp = _dot(p.astype(BF16), poolw_ref[...]) * pools_ref[...]
    yp_ref[0] = yp.astype(BF16)

    lane = lax.broadcasted_iota(jnp.int32, (tm, LANES), 1)
    cosf = cos_ref[0]
    sinf = sin_ref[0]

    q_lat = z[:, Z_QLAT_OFF:Z_QLAT_OFF + MLA_Q_LORA]
    kv_lat = z[:, Z_KVLAT_OFF:Z_KVLAT_OFF + MLA_KV_LORA]
    k_rope = z[:, Z_KROPE_OFF:Z_KROPE_OFF + LANES]
    qn = (_rms(q_lat) * qg_ref[...]).astype(BF16)
    kvn = (_rms(kv_lat) * kvg_ref[...]).astype(BF16)
    q = _dot(qn, wq_ref[...])
    kx = _dot(kvn, wk_ref[...])
    v = _dot(kvn, wv_ref[...])
    k_pe = _rope(k_rope, cosf, sinf, lane)
    qs, ks = [], []
    for hd in range(MLA_HEADS):
        sl = slice(hd * LANES, (hd + 1) * LANES)
        qs.append(_rope(q[:, sl], cosf, sinf, lane))
        ks.append(kx[:, sl] + k_pe)
    q = jnp.concatenate(qs, axis=1)
    k = jnp.concatenate(ks, axis=1)
    qss = _mm_exact_rhs(q * q, bd_ref[...]) * (1.0 / MLA_QK_DIM)
    kss = _mm_exact_rhs(k * k, bd_ref[...]) * (1.0 / MLA_QK_DIM)
    qkg = qkg_ref[...]
    q = q * lax.rsqrt(qss + NORM_EPS) * qkg[0:1] * (MLA_QK_DIM ** -0.5)
    k = k * lax.rsqrt(kss + NORM_EPS) * qkg[1:2]
    q_ref[0] = q.astype(BF16)
    k_ref[0] = k.astype(BF16)
    v_ref[0] = v.astype(BF16)


def _mixin(x, mod_l, gain, win, cosf, sinf, poolw, pools, qg, kvg, wq, wk, wv, qkg, bd128):
    B, T, _ = x.shape
    tm = TM_MIX
    const = lambda shape: pl.BlockSpec(shape, lambda b, i: tuple(0 for _ in shape))
    tok = lambda w: pl.BlockSpec((1, tm, w), lambda b, i: (b, i, 0))
    return pl.pallas_call(
        _mixin_kernel,
        grid=(B, T // tm),
        in_specs=[tok(D_MODEL),
                  pl.BlockSpec((1, 6, D_MODEL), lambda b, i: (b, 0, 0)),
                  const((1, D_MODEL)), const((D_MODEL, Z_COLS)),
                  tok(LANES), tok(LANES),
                  const((POOL_WIDTH, POOL_WIDTH)), const((1, POOL_WIDTH)),
                  const((1, MLA_Q_LORA)), const((1, MLA_KV_LORA)),
                  const((MLA_Q_LORA, MLA_HEADS * LANES)), const((MLA_KV_LORA, MLA_HEADS * LANES)),
                  const((MLA_KV_LORA, MLA_HEADS * HEAD_DIM)), const((2, MLA_HEADS * LANES)),
                  const((MLA_HEADS * LANES, MLA_HEADS * LANES))],
        out_specs=[tok(ZR_COLS), tok(POOL_WIDTH), tok(MLA_HEADS * LANES), tok(MLA_HEADS * LANES),
                   tok(MLA_HEADS * HEAD_DIM)],
        out_shape=[jax.ShapeDtypeStruct((B, T, ZR_COLS), F32),
                   jax.ShapeDtypeStruct((B, T, POOL_WIDTH), BF16),
                   jax.ShapeDtypeStruct((B, T, MLA_HEADS * LANES), BF16),
                   jax.ShapeDtypeStruct((B, T, MLA_HEADS * LANES), BF16),
                   jax.ShapeDtypeStruct((B, T, MLA_HEADS * HEAD_DIM), BF16)],
        scratch_shapes=[pltpu.VMEM((POOL_HALO + tm, POOL_WIDTH), F32)],
        compiler_params=_cparams(("arbitrary", "arbitrary")),
    )(x, mod_l, gain, win, cosf, sinf, poolw, pools, qg, kvg, wq, wk, wv, qkg, bd128)


WKV_PASSES_SCORE = 1
WKV_PASSES_INV = 1
WKV_PASSES_APPLY = 1
WKV_PASSES_STATE = 1
WKV_STEP_CHUNKS = 4


def _stack_heads(xp, lane):
    return jnp.concatenate([jnp.where(lane < HEAD_DIM, xp, 0.0),
                            jnp.where(lane >= HEAD_DIM, xp, 0.0)], axis=0)


def _wkv_prep(r, lw, k, v, kk, a, tri, masks):
    L = r[0].shape[0]
    n = 2 * L
    nc = len(r)
    each = lambda f, *ls: [f(*xs) for xs in zip(*ls)]
    lane = lax.broadcasted_iota(jnp.int32, (L, LANES), 1)
    stack = lambda x: _stack_heads(x, lane)
    cum = each(lambda x: _mm_exact_rhs_left(tri, x), lw)
    cum_last = each(lambda c: c[L - 1:L, :], cum)
    e_w = each(jnp.exp, cum)
    e_wm = each(lambda c, x: jnp.exp(c - x), cum, lw)
    e_iw = each(lambda c: jnp.exp(-c), cum)
    e_d = each(lambda cl, c: jnp.exp(cl - c), cum_last, cum)
    beta = each(lambda x, y: x * y, kk, a)
    r_t = each(lambda x, e: stack(x * e), r, e_w)
    a_t = each(lambda x, e: stack(-x * e), kk, e_wm)
    b_t = each(lambda x, e: stack(x * e), beta, e_iw)
    k_t = each(lambda x, e: stack(x * e), k, e_iw)
    b_d = each(lambda x, e: stack(x * e), beta, e_d)
    k_d = each(lambda x, e: stack(x * e), k, e_d)
    v_s = each(stack, v)
    yield
    g = each(lambda at, rt, bt, kt: _mm(jnp.concatenate([at, rt], axis=0),
                                        jnp.concatenate([bt, kt], axis=0), NT, WKV_PASSES_SCORE),
             a_t, r_t, b_t, k_t)
    strict, incl, levels = masks
    a_ab = each(lambda x: jnp.where(strict, x[:n, :n], 0.0), g)
    a_ak = each(lambda x: jnp.where(strict, x[:n, n:], 0.0), g)
    s_rb = each(lambda x: jnp.where(incl, x[n:, :n], 0.0), g)
    s_rk = each(lambda x: jnp.where(incl, x[n:, n:], 0.0), g)
    eye = jnp.where(levels[0][1], 1.0, 0.0)
    tinv = each(lambda x: eye + jnp.where(levels[0][0], x, 0.0), a_ab)
    yield
    for lvl_mask, _ in levels[1:]:
        et = each(lambda x, t: _mm(jnp.where(lvl_mask, x, 0.0), t, NN, WKV_PASSES_INV), a_ab, tinv)
        tinv = each(lambda t, x: t + _mm(t, x, NN, WKV_PASSES_INV), tinv, et)
        yield
    av = each(lambda x, y: _mm(x, y, NN, WKV_PASSES_APPLY), a_ak, v_s)
    tx = each(lambda t, x, y: _mm(t, jnp.concatenate([x, y], axis=1), NN, WKV_PASSES_APPLY),
              tinv, a_t, av)
    yield
    ra = each(lambda rt, s, x: rt + _mm(s, x[:, :LANES], NN, WKV_PASSES_APPLY), r_t, s_rb, tx)
    c2 = each(lambda sb, sk, x, vs: _mm(jnp.concatenate([sb, sk], axis=1),
                                        jnp.concatenate([x[:, LANES:], vs], axis=0),
                                        NN, WKV_PASSES_APPLY), s_rb, s_rk, tx, v_s)
    yield
    tb = each(lambda x, bd: _mm(x.T, bd, NN, WKV_PASSES_APPLY), tx, b_d)
    c3 = each(lambda x, vs, kd: x[LANES:] + _mm(vs.T, kd, NN, WKV_PASSES_APPLY), tb, v_s, k_d)
    return [(ra[i], c2[i], jnp.exp(cum_last[i]), tb[i][:LANES], c3[i]) for i in range(nc)]


def _mm_exact_rhs_left(tri_bf16, x):
    x0 = x.astype(BF16)
    r1 = x - x0.astype(F32)
    x1 = r1.astype(BF16)
    x2 = (r1 - x1.astype(F32)).astype(BF16)
    return _dot(tri_bf16, x0) + (_dot(tri_bf16, x1) + _dot(tri_bf16, x2))


def _wkv_masks(L):
    n = 2 * L
    row = lax.broadcasted_iota(jnp.int32, (n, n), 0)
    col = lax.broadcasted_iota(jnp.int32, (n, n), 1)
    strict = row > col
    incl = row >= col
    levels = []
    m = 1
    while m < L:
        same = (row // (2 * m)) == (col // (2 * m))
        lvl = same & ((row % (2 * m)) >= m) & ((col % (2 * m)) < m)
        levels.append((lvl, row == col))
        m *= 2
    return strict, incl, levels


def _rwkv_kernel(has_vres, *refs):
    if has_vres:
        (z_ref, vf_ref, mu_ref, vec_ref, w2a2_ref, g2v2_ref, bd_ref,
         y_ref, carry, state) = refs
    else:
        (z_ref, mu_ref, vec_ref, w2a2_ref, g2v2_ref, bd_ref,
         y_ref, vout_ref, carry, state) = refs
    c = pl.program_id(1)
    rows = z_ref.shape[1]
    L = WKV_CHUNK
    W = RWKV_WIDTH

    @pl.when(c == 0)
    def _():
        carry[...] = jnp.zeros(carry.shape, F32)
        state[...] = jnp.zeros(state.shape, F32)

    z = z_ref[0]
    row = lax.broadcasted_iota(jnp.int32, z.shape, 0)
    prev = jnp.where(row == 0, carry[SUBLANES - 1:SUBLANES, :], pltpu.roll(z, 1, axis=0))
    carry[...] = z[rows - SUBLANES:, :]
    zs_all = z + mu_ref[...] * (prev - z)
    vec = vec_ref[...]
    w0, a0, k_k, k_a, r_k, ln_g, ln_b, v0 = (vec[j:j + 1] for j in range(8))
    bd = bd_ref[...]
    masks = _wkv_masks(L)
    rowt = lax.broadcasted_iota(jnp.int32, (L, L), 0)
    colt = lax.broadcasted_iota(jnp.int32, (L, L), 1)
    tri = jnp.where(rowt >= colt, 1.0, 0.0).astype(BF16)
    n_pairs = RWKV_HEADS // 2
    S_now = [[state[p] for p in range(n_pairs)]]

    def group(r0, r1):
        zs = zs_all[r0:r1]
        r = zs[:, 0:W]
        k = zs[:, W:2 * W]
        v = zs[:, 2 * W:3 * W]
        wa = zs[:, 3 * W:3 * W + LANES]
        gb = zs[:, 3 * W + LANES:ZR_COLS]
        lane_a = lax.broadcasted_iota(jnp.int32, wa.shape, 1)
        t1 = _dot(jnp.where(lane_a < RWKV_DECAY_LORA, jnp.tanh(wa), wa).astype(BF16), w2a2_ref[...])
        lane_g = lax.broadcasted_iota(jnp.int32, gb.shape, 1)
        t2 = _dot(jnp.where(lane_g < RWKV_GATE_LORA, _sigmoid(gb), gb).astype(BF16), g2v2_ref[...])
        yield
        xw = w0 + t1[:, :W]
        w_log = -(jnp.maximum(-xw, 0.0) + jnp.log1p(jnp.exp(-jnp.abs(xw)))) - 0.5
        lw = -jnp.exp(w_log)
        a = _sigmoid(a0 + t1[:, W:])
        g = t2[:, :W]
        if has_vres:
            v = v + (vf_ref[0, r0:r1, :] - v) * _sigmoid(v0 + t2[:, W:])
        else:
            vout_ref[0, r0:r1, :] = v
        kk = k * k_k
        nrm = jnp.sqrt(_mm_exact_rhs(kk * kk, bd))
        kk = kk / jnp.maximum(nrm, 1e-12)
        k = k * (1.0 + (a - 1.0) * k_a)
        yield
        n_chunks = (r1 - r0) // L
        idx = [(ch, p) for ch in range(n_chunks) for p in range(n_pairs)]
        cut = lambda x: [x[ch * L:(ch + 1) * L, p * LANES:(p + 1) * LANES] for ch, p in idx]
        res = yield from _wkv_prep(cut(r), cut(lw), cut(k), cut(v), cut(kk), cut(a), tri, masks)
        prep = dict(zip(idx, res))
        yield
        S = S_now[0]
        y_rows = []
        for ch in range(n_chunks):
            y_s = [_mm(prep[ch, p][0], S[p], NT, WKV_PASSES_STATE) + prep[ch, p][1] for p in range(n_pairs)]
            y_rows.append(jnp.concatenate([x[:L] + x[L:] for x in y_s], axis=1))
            S = [S[p] * prep[ch, p][2] + _mm(S[p], prep[ch, p][3], NN, WKV_PASSES_STATE) + prep[ch, p][4]
                 for p in range(n_pairs)]
        S_now[0] = S
        y = jnp.concatenate(y_rows, axis=0)
        yield
        inv = 1.0 / HEAD_DIM
        mean = _mm_exact_rhs(y, bd) * inv
        yc = y - mean
        var = _mm_exact_rhs(yc * yc, bd) * inv
        yn = yc * lax.rsqrt(var + RWKV_LNX_EPS) * ln_g + ln_b
        bonus = _mm_exact_rhs(r * k * r_k, bd) * v
        y_ref[0, r0:r1, :] = ((yn + bonus) * g).astype(BF16)

    for _ in group(0, rows):
        pass
    for p in range(n_pairs):
        state[p] = S_now[0][p]


def _rwkv(zr, v_first, mu, vec8, w2a2, g2v2, bd64):
    B, T, _ = zr.shape
    L = WKV_CHUNK * WKV_STEP_CHUNKS
    has_vres = v_first is not None
    const = lambda shape: pl.BlockSpec(shape, lambda b, c: tuple(0 for _ in shape))
    tok = lambda w: pl.BlockSpec((1, L, w), lambda b, c: (b, c, 0))
    in_specs = [tok(ZR_COLS)]
    args = [zr]
    if has_vres:
        in_specs.append(tok(RWKV_WIDTH))
        args.append(v_first)
    in_specs += [const((1, ZR_COLS)), const((8, RWKV_WIDTH)), const((LANES, 2 * RWKV_WIDTH)),
                 const((2 * LANES, 2 * RWKV_WIDTH)), const((RWKV_WIDTH, RWKV_WIDTH))]
    args += [mu, vec8, w2a2, g2v2, bd64]
    out_specs = [tok(RWKV_WIDTH)]
    out_shape = [jax.ShapeDtypeStruct((B, T, RWKV_WIDTH), BF16)]
    if not has_vres:
        out_specs.append(tok(RWKV_WIDTH))
        out_shape.append(jax.ShapeDtypeStruct((B, T, RWKV_WIDTH), F32))
    outs = pl.pallas_call(
        functools.partial(_rwkv_kernel, has_vres),
        grid=(B, T // L),
        in_specs=in_specs, out_specs=out_specs, out_shape=out_shape,
        scratch_shapes=[pltpu.VMEM((SUBLANES, ZR_COLS), F32),
                        pltpu.VMEM((RWKV_HEADS // 2, LANES, LANES), F32)],
        compiler_params=_cparams(("arbitrary", "arbitrary")),
    )(*args)
    return (outs[0], v_first) if has_vres else (outs[0], outs[1])


def _attn_step(q_ref, k_ref, v_ref, m_sc, l_sc, acc_sc, masked):
    tq = q_ref.shape[1]
    tk = k_ref.shape[1]
    lane = lax.broadcasted_iota(jnp.int32, (tq, LANES), 1)
    if masked:
        rowi = lax.broadcasted_iota(jnp.int32, (tq, tk), 0)
        coli = lax.broadcasted_iota(jnp.int32, (tq, tk), 1)
        keep = coli <= rowi
    heads = range(MLA_HEADS)
    s = [_dot(q_ref[0, :, hd * LANES:(hd + 1) * LANES], k_ref[0, :, hd * LANES:(hd + 1) * LANES], NT)
         for hd in heads]
    if masked:
        s = [jnp.where(keep, x, NEG_INF) for x in s]
    m_prev = [m_sc[hd] for hd in heads]
    m_new = [jnp.maximum(m_prev[hd], jnp.max(s[hd], axis=-1, keepdims=True)) for hd in heads]
    alpha = [jnp.exp(m_prev[hd] - m_new[hd]) for hd in heads]
    p = [jnp.exp(s[hd] - jnp.concatenate([m_new[hd]] * (tk // LANES), axis=1)) for hd in heads]
    for hd in heads:
        l_sc[hd] = alpha[hd] * l_sc[hd] + jnp.sum(p[hd], axis=-1, keepdims=True)
        m_sc[hd] = m_new[hd]
    pv = [_dot(p[hd].astype(BF16), v_ref[0, :, (hd // 2) * LANES:(hd // 2 + 1) * LANES]) for hd in heads]
    first = lane < HEAD_DIM
    for pr in range(MLA_HEADS // 2):
        acc_sc[pr] = (acc_sc[pr] * jnp.where(first, alpha[2 * pr], alpha[2 * pr + 1])
                      + jnp.where(first, pv[2 * pr], pv[2 * pr + 1]))


def _attn_step_bounded(q_ref, k_ref, v_ref, cb_ref, l_sc, acc_sc, masked):
    tq = q_ref.shape[1]
    tk = k_ref.shape[1]
    lane = lax.broadcasted_iota(jnp.int32, (tq, LANES), 1)
    if masked:
        rowi = lax.broadcasted_iota(jnp.int32, (tq, tk), 0)
        coli = lax.broadcasted_iota(jnp.int32, (tq, tk), 1)
        keep = coli <= rowi
    pv = []
    for hd in range(MLA_HEADS):
        s = _dot(q_ref[0, :, hd * LANES:(hd + 1) * LANES], k_ref[0, :, hd * LANES:(hd + 1) * LANES], NT)
        c = cb_ref[0, hd:hd + 1, :]
        p = jnp.exp(s - jnp.concatenate([c] * (tk // LANES), axis=1))
        if masked:
            p = jnp.where(keep, p, 0.0)
        part = p[:, 0:LANES]
        for t in range(1, tk // LANES):
            part = part + p[:, t * LANES:(t + 1) * LANES]
        l_sc[hd] = l_sc[hd] + part
        pv.append(_dot(p.astype(BF16), v_ref[0, :, (hd // 2) * LANES:(hd // 2 + 1) * LANES]))
    first = lane < HEAD_DIM
    for pr in range(MLA_HEADS // 2):
        acc_sc[pr] = acc_sc[pr] + jnp.where(first, pv[2 * pr], pv[2 * pr + 1])


def _attn_finish(o_ref, l_sc, acc_sc, lane_partial):
    tq = o_ref.shape[1]
    lane = lax.broadcasted_iota(jnp.int32, (tq, LANES), 1)
    outs = []
    for pr in range(MLA_HEADS // 2):
        la, lb = l_sc[2 * pr], l_sc[2 * pr + 1]
        if lane_partial:
            la = jnp.sum(la, axis=-1, keepdims=True)
            lb = jnp.sum(lb, axis=-1, keepdims=True)
        outs.append(acc_sc[pr] / jnp.where(lane < HEAD_DIM, la, lb))
    o_ref[0] = jnp.concatenate(outs, axis=1).astype(BF16)


def _attn_kernel(ok_ref, qi_ref, kj_ref, q_ref, k_ref, v_ref, cb_ref, o_ref, m_sc, l_sc, acc_sc):
    i = qi_ref[pl.program_id(1)]
    j = kj_ref[pl.program_id(1)]
    bounded = ok_ref[pl.program_id(0)] == 1
    exact = jnp.logical_not(bounded)

    @pl.when(j == 0)
    def _():
        m_sc[...] = jnp.full(m_sc.shape, NEG_INF, F32)
        l_sc[...] = jnp.zeros(l_sc.shape, F32)
        acc_sc[...] = jnp.zeros(acc_sc.shape, F32)

    @pl.when(bounded & (j < i))
    def _():
        _attn_step_bounded(q_ref, k_ref, v_ref, cb_ref, l_sc, acc_sc, masked=False)

    @pl.when(bounded & (j == i))
    def _():
        _attn_step_bounded(q_ref, k_ref, v_ref, cb_ref, l_sc, acc_sc, masked=True)
        _attn_finish(o_ref, l_sc, acc_sc, lane_partial=True)

    @pl.when(exact & (j < i))
    def _():
        _attn_step(q_ref, k_ref, v_ref, m_sc, l_sc, acc_sc, masked=False)

    @pl.when(exact & (j == i))
    def _():
        _attn_step(q_ref, k_ref, v_ref, m_sc, l_sc, acc_sc, masked=True)
        _attn_finish(o_ref, l_sc, acc_sc, lane_partial=False)


def _attention(q, k, v, qk_gain):
    B, T, _ = q.shape
    nq = T // TQ
    gmax = jnp.max(jnp.abs(qk_gain), axis=1)
    c = gmax[0] * gmax[1] * (MLA_QK_DIM ** 0.5) * ATTN_BOUND_SLACK
    ok = jnp.broadcast_to((c <= ATTN_BOUND_MAX).astype(jnp.int32), (B,))
    cb = jnp.broadcast_to(c, (B, MLA_HEADS, LANES))
    pairs = [(i, j) for i in range(nq) for j in range(i + 1)]
    qi = jnp.asarray([p[0] for p in pairs], jnp.int32)
    kj = jnp.asarray([p[1] for p in pairs], jnp.int32)
    grid_spec = pltpu.PrefetchScalarGridSpec(
        num_scalar_prefetch=3,
        grid=(B, len(pairs)),
        in_specs=[pl.BlockSpec((1, TQ, MLA_HEADS * LANES), lambda b, t, ok, qi, kj: (b, qi[t], 0)),
                  pl.BlockSpec((1, TQ, MLA_HEADS * LANES), lambda b, t, ok, qi, kj: (b, kj[t], 0)),
                  pl.BlockSpec((1, TQ, MLA_HEADS * HEAD_DIM), lambda b, t, ok, qi, kj: (b, kj[t], 0)),
                  pl.BlockSpec((1, MLA_HEADS, LANES), lambda b, t, ok, qi, kj: (b, 0, 0))],
        out_specs=pl.BlockSpec((1, TQ, MLA_HEADS * HEAD_DIM), lambda b, t, ok, qi, kj: (b, qi[t], 0)),
        scratch_shapes=[pltpu.VMEM((MLA_HEADS, TQ, LANES), F32),
                        pltpu.VMEM((MLA_HEADS, TQ, LANES), F32),
                        pltpu.VMEM((MLA_HEADS // 2, TQ, LANES), F32)])
    return pl.pallas_call(
        _attn_kernel,
        grid_spec=grid_spec,
        out_shape=jax.ShapeDtypeStruct((B, T, MLA_HEADS * HEAD_DIM), BF16),
        compiler_params=_cparams(("arbitrary", "arbitrary")),
    )(ok, qi, kj, q, k, v, cb)


def _mixout_kernel(has_router, *refs):
    if has_router:
        (x_ref, yr_ref, yp_ref, ym_ref, wo_ref, mod_ref, gain_ref, rt_ref, tri_ref,
         xo_ref, h_ref, route_ref, cnt_ref, cnt_sc) = refs
    else:
        x_ref, yr_ref, yp_ref, ym_ref, wo_ref, mod_ref, gain_ref, xo_ref, h_ref = refs
    mod = mod_ref[0]
    o1 = RWKV_WIDTH
    o2 = RWKV_WIDTH + POOL_WIDTH
    mix = (_dot(yr_ref[0], wo_ref[0:o1, :]) + _dot(yp_ref[0], wo_ref[o1:o2, :])
           + _dot(ym_ref[0], wo_ref[o2:, :]))
    x = x_ref[0] + mod[2:3] * mix
    xo_ref[0] = x
    h = _rms(x) * gain_ref[...] * (1.0 + mod[4:5]) + mod[3:4]
    if not has_router:
        h_ref[0] = h.astype(BF16)
    else:
        hp = _pack_bf16_pairs(h)
        for ck in range(ROW_CHUNKS):
            h_ref[ck, 0] = hp[:, ck * PACK_CHUNK_W:(ck + 1) * PACK_CHUNK_W]
        logits = _dot(h.astype(BF16), rt_ref[...])
        lane = lax.broadcasted_iota(jnp.int32, logits.shape, 1).astype(F32)
        lg = jnp.where(lane < N_EXPERTS, logits, -jnp.inf)
        m1 = jnp.max(lg, axis=-1, keepdims=True)
        i1 = jnp.min(jnp.where(lg == m1, lane, float(LANES)), axis=-1, keepdims=True)
        lg2 = jnp.where(lane == i1, -jnp.inf, lg)
        m2 = jnp.max(lg2, axis=-1, keepdims=True)
        i2 = jnp.min(jnp.where(lg2 == m2, lane, float(LANES)), axis=-1, keepdims=True)
        e2 = jnp.exp(m2 - m1)
        g1 = 1.0 / (1.0 + e2)
        g2 = e2 / (1.0 + e2)
        first = (pl.program_id(0) == 0) & (pl.program_id(1) == 0)

        @pl.when(first)
        def _():
            cnt_sc[...] = jnp.zeros(cnt_sc.shape, F32)

        hit1 = lane == i1
        hit2 = lane == i2
        onehot = jnp.where(hit1 | hit2, 1.0, 0.0)
        prefix = _dot(tri_ref[...], onehot.astype(BF16)) + cnt_sc[0:1, :]
        r1 = jnp.sum(jnp.where(hit1, prefix, 0.0), axis=-1, keepdims=True)
        r2 = jnp.sum(jnp.where(hit2, prefix, 0.0), axis=-1, keepdims=True)
        cnt_sc[...] = cnt_sc[...] + jnp.sum(onehot, axis=0, keepdims=True)
        cnt_ref[...] = cnt_sc[...]
        vals = (i1, i2, g1, g2, r1, r2)
        route = jnp.zeros(logits.shape, F32)
        for pos, val in enumerate(vals):
            route = jnp.where(lane == pos, val, route)
        route_ref[0] = route


def _mixout(x, yr, yp, ym, wo, mod_l, gain, router_p):
    B, T, _ = x.shape
    tm = TM_MIX
    has_router = router_p is not None
    const = lambda shape: pl.BlockSpec(shape, lambda b, i: tuple(0 for _ in shape))
    tok = lambda w: pl.BlockSpec((1, tm, w), lambda b, i: (b, i, 0))
    in_specs = [tok(D_MODEL), tok(RWKV_WIDTH), tok(POOL_WIDTH), tok(MLA_HEADS * HEAD_DIM),
                const((D_MODEL, D_MODEL)), pl.BlockSpec((1, 6, D_MODEL), lambda b, i: (b, 0, 0)),
                const((1, D_MODEL))]
    args = [x, yr, yp, ym, wo, mod_l, gain]
    out_specs = [tok(D_MODEL), tok(D_MODEL)]
    out_shape = [jax.ShapeDtypeStruct((B, T, D_MODEL), F32), jax.ShapeDtypeStruct((B, T, D_MODEL), BF16)]
    scratch = []
    if has_router:
        out_specs[1] = pl.BlockSpec((ROW_CHUNKS, 1, tm, PACK_CHUNK_W), lambda b, i: (0, b, i, 0))
        out_shape[1] = jax.ShapeDtypeStruct((ROW_CHUNKS, B, T, PACK_CHUNK_W), jnp.uint32)
        ids = np.arange(tm)
        tri = jnp.asarray(ids[:, None] > ids[None, :], BF16)
        in_specs += [const((D_MODEL, LANES)), const((tm, tm))]
        args += [router_p, tri]
        out_specs += [tok(LANES), const((SUBLANES, LANES))]
        out_shape += [jax.ShapeDtypeStruct((B, T, LANES), F32),
                      jax.ShapeDtypeStruct((SUBLANES, LANES), F32)]
        scratch = [pltpu.VMEM((SUBLANES, LANES), F32)]
    return pl.pallas_call(
        functools.partial(_mixout_kernel, has_router),
        grid=(B, T // tm),
        in_specs=in_specs, out_specs=out_specs, out_shape=out_shape, scratch_shapes=scratch,
        compiler_params=_cparams(("arbitrary", "arbitrary")),
    )(*args)


def _ffn_kernel(h_ref, wg_ref, wu_ref, wo_ref, x_ref, mod_ref, o_ref, acc):
    j = pl.program_id(1)

    @pl.when(j == 0)
    def _():
        acc[...] = jnp.zeros(acc.shape, F32)

    h = h_ref[...]
    gg = _dot(h, wg_ref[...])
    uu = _dot(h, wu_ref[...])
    acc[...] += _dot((_silu(gg) * uu).astype(BF16), wo_ref[...])

    @pl.when(j == pl.num_programs(1) - 1)
    def _():
        o_ref[...] = x_ref[...] + mod_ref[0][5:6] * acc[...]


def _ffn(h2, w_in, w_out, x, mod_l):
    N = h2.shape[0]
    T = N // mod_l.shape[0]
    tm, tf = TM_FFN, TF_FFN
    nf = D_FF // tf
    per_b = T // tm
    return pl.pallas_call(
        _ffn_kernel,
        grid=(N // tm, nf),
        in_specs=[pl.BlockSpec((tm, D_MODEL), lambda i, j: (i, 0)),
                  pl.BlockSpec((D_MODEL, tf), lambda i, j: (0, j)),
                  pl.BlockSpec((D_MODEL, tf), lambda i, j: (0, j + nf)),
                  pl.BlockSpec((tf, D_MODEL), lambda i, j: (j, 0)),
                  pl.BlockSpec((tm, D_MODEL), lambda i, j: (i, 0)),
                  pl.BlockSpec((1, 6, D_MODEL), lambda i, j: (i // per_b, 0, 0))],
        out_specs=pl.BlockSpec((tm, D_MODEL), lambda i, j: (i, 0)),
        out_shape=jax.ShapeDtypeStruct((N, D_MODEL), F32),
        scratch_shapes=[pltpu.VMEM((tm, D_MODEL), F32)],
        compiler_params=_cparams(("arbitrary", "arbitrary")),
    )(h2, w_in, w_in, w_out, x, mod_l)


def _moe_kernel(be_ref, nv_ref, last_ref, x_ref, wg_ref, wu_ref, wo_ref, o_ref, acc, xm):
    i = pl.program_id(0)
    j = pl.program_id(1)

    @pl.when(i <= last_ref[0])
    def _():
        @pl.when(j == 0)
        def _():
            acc[...] = jnp.zeros(acc.shape, F32)
            row = lax.broadcasted_iota(jnp.int32, (xm.shape[0], 1), 0)
            xp = jnp.concatenate([x_ref[ck] for ck in range(ROW_CHUNKS)], axis=1)
            xp = jnp.where(row < nv_ref[i], xp, jnp.uint32(0))
            xm[...] = _unpack_bf16_pairs(xp)

        x = xm[...]
        gg = _dot(x, wg_ref[0].astype(BF16))
        uu = _dot(x, wu_ref[0].astype(BF16))
        acc[...] += _dot((_silu(gg) * uu).astype(BF16), wo_ref[0].astype(BF16))

        @pl.when(j == pl.num_programs(1) - 1)
        def _():
            yp = _pack_bf16_pairs(acc[...])
            for ck in range(ROW_CHUNKS):
                o_ref[ck] = yp[:, ck * PACK_CHUNK_W:(ck + 1) * PACK_CHUNK_W]


def _moe_experts(xs, w_in, w_out, block_exp, n_valid, last_blk):
    n_rows = xs.shape[1]
    tm, tf = MOE_BLOCK, TF_MOE
    nf = D_FF_EXPERT // tf
    blk = lambda i, last: jnp.minimum(i, last[0])
    chunk = lambda i, j, last: jnp.where(i <= last[0], j, nf - 1)
    grid_spec = pltpu.PrefetchScalarGridSpec(
        num_scalar_prefetch=3,
        grid=(n_rows // tm, nf),
        in_specs=[pl.BlockSpec((ROW_CHUNKS, tm, PACK_CHUNK_W),
                               lambda i, j, be, nv, last: (0, blk(i, last), 0)),
                  pl.BlockSpec((1, D_MODEL, tf),
                               lambda i, j, be, nv, last: (be[blk(i, last)], 0, chunk(i, j, last))),
                  pl.BlockSpec((1, D_MODEL, tf),
                               lambda i, j, be, nv, last: (be[blk(i, last)], 0, chunk(i, j, last) + nf)),
                  pl.BlockSpec((1, tf, D_MODEL),
                               lambda i, j, be, nv, last: (be[blk(i, last)], chunk(i, j, last), 0))],
        out_specs=pl.BlockSpec((ROW_CHUNKS, tm, PACK_CHUNK_W), lambda i, j, be, nv, last: (0, blk(i, last), 0)),
        scratch_shapes=[pltpu.VMEM((tm, D_MODEL), F32), pltpu.VMEM((tm, D_MODEL), BF16)])
    return pl.pallas_call(
        _moe_kernel,
        grid_spec=grid_spec,
        out_shape=jax.ShapeDtypeStruct((ROW_CHUNKS, n_rows, PACK_CHUNK_W), jnp.uint32),
        compiler_params=_cparams(("arbitrary", "arbitrary")),
    )(block_exp, n_valid, last_blk, xs, w_in, w_in, w_out)


def _sc_mesh():
    return plsc.VectorSubcoreMesh(core_axis_name="c", subcore_axis_name="s")


def _sc_scatter_rows(x, dest, n_rows):
    N, D = x.shape
    K = dest.shape[0]
    win = SC_WINDOW

    @pl.kernel(out_type=jax.ShapeDtypeStruct((n_rows, D), x.dtype), mesh=_sc_mesh(), scratch_types=[])
    def scatter(x_hbm, d_hbm, o_hbm):
        def body(x_vmem, *idx_vmem):
            for iv in idx_vmem:
                pltpu.sync_copy(x_vmem, o_hbm.at[iv.at[0]])

        pltpu.emit_pipeline(
            body,
            grid=(N // win,),
            in_specs=[pl.BlockSpec((win, D), lambda i: (i, 0))]
            + [pl.BlockSpec((1, win), functools.partial(lambda k, i: (k, i), k)) for k in range(K)],
            out_specs=[],
            core_axis_name=("c", "s"),
            dimension_semantics=(pltpu.PARALLEL,),
        )(x_hbm, *([d_hbm] * K))

    return scatter(x, dest)


def _sc_gather_rows(x, idx):
    n = idx.shape[0]
    D = x.shape[1]
    win = SC_WINDOW

    @pl.kernel(out_type=jax.ShapeDtypeStruct((n, D), x.dtype), mesh=_sc_mesh(), scratch_types=[])
    def gather(x_hbm, i_hbm, o_hbm):
        def body(i_vmem, o_vmem):
            pltpu.sync_copy(x_hbm.at[i_vmem.at[0]], o_vmem)

        pltpu.emit_pipeline(
            body,
            grid=(n // win,),
            in_specs=[pl.BlockSpec((1, win), lambda i: (0, i))],
            out_specs=[pl.BlockSpec((win, D), lambda i: (i, 0))],
            core_axis_name=("c", "s"),
            dimension_semantics=(pltpu.PARALLEL,),
        )(i_hbm, o_hbm)

    return gather(x, idx.reshape(1, n))


def _combine_kernel(x_ref, ya_ref, yb_ref, route_ref, mod_ref, o_ref):
    rt = route_ref[...]
    ya = _unpack_bf16_pairs(jnp.concatenate([ya_ref[0, ck] for ck in range(ROW_CHUNKS)], axis=1))
    yb = _unpack_bf16_pairs(jnp.concatenate([yb_ref[0, ck] for ck in range(ROW_CHUNKS)], axis=1))
    f = rt[:, 2:3] * ya.astype(F32) + rt[:, 3:4] * yb.astype(F32)
    o_ref[...] = x_ref[...] + mod_ref[0][5:6] * f


def _combine(x, y2, route, mod_l):
    N = x.shape[0]
    T = N // mod_l.shape[0]
    tm = 1024
    per_b = T // tm
    tok = pl.BlockSpec((tm, D_MODEL), lambda i: (i, 0))
    slot = lambda k: pl.BlockSpec((1, ROW_CHUNKS, tm, PACK_CHUNK_W), lambda i: (k, 0, i, 0))
    return pl.pallas_call(
        _combine_kernel,
        grid=(N // tm,),
        in_specs=[tok, slot(0), slot(1),
                  pl.BlockSpec((tm, LANES), lambda i: (i, 0)),
                  pl.BlockSpec((1, 6, D_MODEL), lambda i: (i // per_b, 0, 0))],
        out_specs=tok,
        out_shape=jax.ShapeDtypeStruct((N, D_MODEL), F32),
        compiler_params=_cparams(("arbitrary",)),
    )(x, y2, y2, route, mod_l)


def _moe(h2, route, counts, w_in, w_out, x, mod_l):
    N = x.shape[0]
    blk = MOE_BLOCK
    cnt = counts[0, :N_EXPERTS].astype(jnp.int32)
    padded = (cnt + blk - 1) // blk * blk
    pend = jnp.cumsum(padded)
    pstart = pend - padded
    e = route[:, 0:TOP_K].astype(jnp.int32)
    rank = route[:, 2 * TOP_K:3 * TOP_K].astype(jnp.int32)
    dest = (jnp.take(pstart, e) + rank).T
    n_blocks = N * TOP_K // blk + N_EXPERTS
    bstart = jnp.arange(n_blocks, dtype=jnp.int32) * blk
    block_exp = jnp.clip(jnp.searchsorted(pend, bstart, side='right'), 0, N_EXPERTS - 1).astype(jnp.int32)
    n_valid = jnp.clip(cnt[block_exp] - (bstart - pstart[block_exp]), 0, blk).astype(jnp.int32)
    n_rows = n_blocks * blk
    dest_ck = dest[:, None, :] + (jnp.arange(ROW_CHUNKS, dtype=jnp.int32) * n_rows)[None, :, None]
    xs = _sc_scatter_rows(h2, dest_ck.reshape(TOP_K, ROW_CHUNKS * N), ROW_CHUNKS * n_rows)
    last_blk = (pend[-1:] // blk - 1).astype(jnp.int32)
    yb = _moe_experts(xs.reshape(ROW_CHUNKS, n_rows, PACK_CHUNK_W), w_in, w_out, block_exp, n_valid,
                      last_blk)
    y2 = _sc_gather_rows(yb.reshape(ROW_CHUNKS * n_rows, PACK_CHUNK_W), dest_ck.reshape(-1))
    return _combine(x, y2.reshape(TOP_K, ROW_CHUNKS, N, PACK_CHUNK_W), route, mod_l)


def _layout_w_in(w, has_vres):
    W = RWKV_WIDTH
    off_gd = 3 * W + RWKV_DECAY_LORA + RWKV_ICLR_LORA
    off_pool = off_gd + RWKV_GATE_LORA
    off_q = off_pool + POOL_WIDTH
    off_kv = off_q + MLA_Q_LORA
    off_kr = off_kv + MLA_KV_LORA
    n_base = off_kr + MLA_QK_ROPE
    d = w.shape[0]
    zeros = lambda n: jnp.zeros((d, n), w.dtype)
    vd = w[:, n_base:n_base + RWKV_VRES_LORA] if has_vres else zeros(RWKV_VRES_LORA)
    cols = [w[:, :off_gd], w[:, off_gd:off_pool], vd, zeros(ZR_COLS - off_pool - RWKV_VRES_LORA),
            w[:, off_pool:off_q], w[:, off_q:off_kv], w[:, off_kv:off_kr],
            zeros(MLA_QK_NOPE), w[:, off_kr:n_base], zeros(LANES - MLA_QK_DIM)]
    return jnp.concatenate(cols, axis=1).astype(BF16)


def _pad_heads(w, per_head, keep_from, keep_n):
    K = w.shape[0]
    wh = w.reshape(K, MLA_HEADS, per_head)[:, :, keep_from:keep_from + keep_n]
    wh = jnp.pad(wh, ((0, 0), (0, 0), (0, LANES - keep_n)))
    return wh.reshape(K, MLA_HEADS * LANES)


def kernel(x, c, positions, w_ada, b_ada, norm_gain, w_in_first, w_in_rest, mu_shift, mu_shift_v,
           rwkv_vec, rwkv_v0, rwkv_w2, rwkv_a2, rwkv_g2, rwkv_v2, pool_w, pool_scale,
           mla_q_lat_gain, mla_kv_lat_gain, mla_wq_up, mla_wkv_up, mla_qk_gain, w_out, ffn_w_in,
           ffn_w_out, moe_router, moe_w_in, moe_w_out):
    B, T, D = x.shape
    depth = w_ada.shape[0]
    W = RWKV_WIDTH
    mod = _adaln(c, w_ada, b_ada).reshape(depth, B, 6, D)
    pos3 = positions.reshape(B, T, 1)
    inv_freq = ROPE_BASE ** (-jnp.arange(0, MLA_QK_ROPE, 2, dtype=F32) / MLA_QK_ROPE)
    freq = jnp.concatenate([jnp.zeros((MLA_QK_NOPE,), F32), inv_freq, inv_freq,
                            jnp.zeros((LANES - MLA_QK_DIM,), F32)]).reshape(1, LANES)
    cosf, sinf = _rope_tables(pos3, freq)
    hid = np.arange(W) // HEAD_DIM
    bd64 = jnp.asarray(hid[:, None] == hid[None, :], BF16)
    bid = np.arange(MLA_HEADS * LANES) // LANES
    bd128 = jnp.asarray(bid[:, None] == bid[None, :], BF16)

    v_first = None
    for l in range(depth):
        has_vres = l > 0
        mod_l = mod[l]
        win = _layout_w_in(w_in_first if l == 0 else w_in_rest[l - 1], has_vres)
        poolw = jax.scipy.linalg.block_diag(*[pool_w[l, g] for g in range(len(POOL_WINDOWS))]).astype(BF16)
        wq = _pad_heads(mla_wq_up[l], MLA_QK_DIM, 0, MLA_QK_DIM).astype(BF16)
        wk = _pad_heads(mla_wkv_up[l], MLA_QK_NOPE + HEAD_DIM, 0, MLA_QK_NOPE).astype(BF16)
        wv = mla_wkv_up[l].reshape(MLA_KV_LORA, MLA_HEADS, MLA_QK_NOPE + HEAD_DIM)[:, :, MLA_QK_NOPE:]
        wv = wv.reshape(MLA_KV_LORA, MLA_HEADS * HEAD_DIM).astype(BF16)
        qkg = jnp.tile(jnp.pad(mla_qk_gain[l], ((0, 0), (0, LANES - MLA_QK_DIM))), (1, MLA_HEADS))
        zr, y_pool, q, k, v = _mixin(
            x, mod_l, norm_gain[l, 0].reshape(1, D), win, cosf, sinf, poolw,
            pool_scale[l].reshape(1, -1), mla_q_lat_gain[l].reshape(1, -1),
            mla_kv_lat_gain[l].reshape(1, -1), wq, wk, wv, qkg, bd128)

        pad_mu = ZR_COLS - mu_shift.shape[1] - RWKV_VRES_LORA
        mu_v = mu_shift_v[l - 1] if has_vres else jnp.zeros((RWKV_VRES_LORA,), F32)
        mu = jnp.concatenate([mu_shift[l], mu_v, jnp.zeros((pad_mu,), F32)]).reshape(1, ZR_COLS)
        v0 = rwkv_v0[l - 1] if has_vres else jnp.zeros((W,), F32)
        vec8 = jnp.concatenate([rwkv_vec[l], v0[None]], axis=0)
        w2a2 = jax.scipy.linalg.block_diag(rwkv_w2[l], rwkv_a2[l]).astype(BF16)
        g2 = jnp.pad(rwkv_g2[l], ((0, 2 * LANES - RWKV_GATE_LORA), (0, 0)))
        if has_vres:
            v2 = jnp.pad(rwkv_v2[l - 1], ((RWKV_GATE_LORA, 2 * LANES - RWKV_GATE_LORA - RWKV_VRES_LORA), (0, 0)))
        else:
            v2 = jnp.zeros((2 * LANES, W), F32)
        g2v2 = jnp.concatenate([g2, v2], axis=1).astype(BF16)
        y_rwkv, v_first = _rwkv(zr, v_first, mu, vec8, w2a2, g2v2, bd64)

        y_mla = _attention(q, k, v, mla_qk_gain[l])

        is_moe = (l % 2 == 1)
        router_p = None
        if is_moe:
            router_p = jnp.pad(moe_router[l // 2], ((0, 0), (0, LANES - N_EXPERTS))).astype(BF16)
        outs = _mixout(x, y_rwkv, y_pool, y_mla, w_out[l].astype(BF16), mod_l,
                       norm_gain[l, 1].reshape(1, D), router_p)
        x_mid, h2 = outs[0], outs[1]
        xf = x_mid.reshape(B * T, D)
        if is_moe:
            xo = _moe(h2.reshape(ROW_CHUNKS * B * T, PACK_CHUNK_W), outs[2].reshape(B * T, LANES), outs[3],
                      moe_w_in[l // 2], moe_w_out[l // 2], xf, mod_l)
        else:
            xo = _ffn(h2.reshape(B * T, D), ffn_w_in[l // 2].astype(BF16), ffn_w_out[l // 2].astype(BF16),
                      xf, mod_l)
        x = xo.reshape(B, T, D)
    return x
```

```python
import functools

import numpy as np
import jax
import jax.numpy as jnp
from jax import lax
from jax.experimental import pallas as pl
from jax.experimental.pallas import tpu as pltpu
from jax.experimental.pallas import tpu_sc as plsc

F32 = jnp.float32
BF16 = jnp.bfloat16

D_MODEL = 1024
HEAD_DIM = 64
RWKV_WIDTH = 512
RWKV_HEADS = 8
POOL_WIDTH = 256
POOL_WINDOWS = (2, 4, 8, 16)
POOL_HALO = 16
MLA_HEADS = 4
MLA_QK_NOPE = 64
MLA_QK_ROPE = 32
MLA_QK_DIM = 96
MLA_Q_LORA = 256
MLA_KV_LORA = 128
ROPE_BASE = 10000.0
RWKV_DECAY_LORA = 64
RWKV_ICLR_LORA = 64
RWKV_VRES_LORA = 32
RWKV_GATE_LORA = 160
RWKV_LNX_EPS = 64e-5
D_FF = 2816
N_EXPERTS = 8
TOP_K = 2
D_FF_EXPERT = 3584
NORM_EPS = 1e-6
NEG_INF = -1e30

LANES = 128
SUBLANES = 8
VMEM_LIMIT = 56 * 1024 * 1024

ZR_COLS = 1920
Z_POOL_OFF = ZR_COLS
Z_QLAT_OFF = Z_POOL_OFF + POOL_WIDTH
Z_KVLAT_OFF = Z_QLAT_OFF + MLA_Q_LORA
Z_KROPE_OFF = Z_KVLAT_OFF + MLA_KV_LORA
Z_COLS = Z_KROPE_OFF + LANES

MIXIN_PIECE_W = 512
TM_MIX = 512
WKV_CHUNK = 64
TQ = 512
ATTN_BOUND_SLACK = 1.02
ATTN_BOUND_MAX = 40.0
TM_FFN = 512
TF_FFN = 1408
MOE_BLOCK = 1024
TF_MOE = 512
SC_WINDOW = 128
ROW_CHUNKS = 4
ROW_CHUNK_W = D_MODEL // ROW_CHUNKS
PACK_CHUNK_W = ROW_CHUNK_W // 2

SEGSUM_SPLITS = 1

NN = (((1,), (0,)), ((), ()))
NT = (((1,), (1,)), ((), ()))


def _dot(a, b, dims=NN):
    return lax.dot_general(a, b, dims, preferred_element_type=F32)


def _split2(a):
    hi = a.astype(BF16)
    lo = (a - hi.astype(F32)).astype(BF16)
    return hi, lo


def _mm(a, b, dims=NN, passes=3):
    if passes == 1:
        return _dot(a.astype(BF16), b.astype(BF16), dims)
    ah, al = _split2(a)
    bh, bl = _split2(b)
    return _dot(ah, bh, dims) + (_dot(ah, bl, dims) + _dot(al, bh, dims))


def _mm_exact_rhs(a, b_bf16, dims=NN, splits=SEGSUM_SPLITS):
    out = None
    rem = a
    for s in range(splits):
        part = rem.astype(BF16)
        term = _dot(part, b_bf16, dims)
        out = term if out is None else out + term
        if s + 1 < splits:
            rem = rem - part.astype(F32)
    return out


def _pack_bf16_pairs(h):
    w = h.shape[1] // 2
    lo = lax.bitcast_convert_type(h[:, :w].astype(BF16).astype(F32), jnp.uint32)
    hi = lax.bitcast_convert_type(h[:, w:].astype(BF16).astype(F32), jnp.uint32)
    return (lo >> 16) | (hi & jnp.uint32(0xFFFF0000))


def _unpack_bf16_pairs(p):
    lo = lax.bitcast_convert_type(p << 16, F32)
    hi = lax.bitcast_convert_type(p & jnp.uint32(0xFFFF0000), F32)
    return jnp.concatenate([lo, hi], axis=1).astype(BF16)


def _sigmoid(x):
    return 1.0 / (1.0 + jnp.exp(-x))


def _silu(x):
    return x * _sigmoid(x)


def _rms(x, eps=NORM_EPS):
    return x * lax.rsqrt(jnp.mean(x * x, axis=-1, keepdims=True) + eps)


def _cparams(sem):
    return pltpu.CompilerParams(dimension_semantics=sem, vmem_limit_bytes=VMEM_LIMIT)


def _adaln_kernel(c_ref, w_ref, b_ref, o_ref):
    ca = _silu(c_ref[...])
    o_ref[0] = _mm(ca, w_ref[0]) + b_ref[0]


def _adaln(c, w_ada, b_ada):
    L = w_ada.shape[0]
    B = c.shape[0]
    n = w_ada.shape[2] // D_MODEL
    return pl.pallas_call(
        _adaln_kernel,
        grid=(L, n),
        in_specs=[pl.BlockSpec((B, D_MODEL), lambda l, j: (0, 0)),
                  pl.BlockSpec((1, D_MODEL, D_MODEL), lambda l, j: (l, 0, j)),
                  pl.BlockSpec((1, 1, D_MODEL), lambda l, j: (l, 0, j))],
        out_specs=pl.BlockSpec((1, B, D_MODEL), lambda l, j: (l, 0, j)),
        out_shape=jax.ShapeDtypeStruct((L, B, n * D_MODEL), F32),
        compiler_params=_cparams(("arbitrary", "arbitrary")),
    )(c, w_ada, b_ada.reshape(L, 1, -1))


def _rope(x, cosf, sinf, lane):
    up = pltpu.roll(x, LANES - MLA_QK_ROPE // 2, axis=1)
    dn = pltpu.roll(x, MLA_QK_ROPE // 2, axis=1)
    rot = jnp.where(lane < MLA_QK_NOPE + MLA_QK_ROPE // 2, -up, dn)
    return x * cosf + rot * sinf


def _rope_kernel(pos_ref, freq_ref, cos_ref, sin_ref):
    tm = pos_ref.shape[1]
    lane = lax.broadcasted_iota(jnp.int32, (tm, LANES), 1)
    in_rope = (lane >= MLA_QK_NOPE) & (lane < MLA_QK_DIM)
    ang = pos_ref[0].astype(F32) * freq_ref[...]
    cos_ref[0] = jnp.where(in_rope, jnp.cos(ang), 1.0)
    sin_ref[0] = jnp.where(in_rope, jnp.sin(ang), 0.0)


def _rope_tables(pos3, freq):
    B, T, _ = pos3.shape
    tm = TM_MIX
    tok = lambda w: pl.BlockSpec((1, tm, w), lambda b, i: (b, i, 0))
    return pl.pallas_call(
        _rope_kernel,
        grid=(B, T // tm),
        in_specs=[tok(1), pl.BlockSpec((1, LANES), lambda b, i: (0, 0))],
        out_specs=[tok(LANES), tok(LANES)],
        out_shape=[jax.ShapeDtypeStruct((B, T, LANES), F32)] * 2,
        compiler_params=_cparams(("arbitrary", "arbitrary")),
    )(pos3, freq)


def _mixin_kernel(x_ref, mod_ref, gain_ref, win_ref, cos_ref, sin_ref, poolw_ref, pools_ref,
                  qg_ref, kvg_ref, wq_ref, wk_ref, wv_ref, qkg_ref, bd_ref,
                  zr_ref, yp_ref, q_ref, k_ref, v_ref, ubuf):
    i = pl.program_id(1)
    tm = x_ref.shape[1]
    x = x_ref[0]
    mod = mod_ref[0]
    @pl.when(i == 0)
    def _():
        ubuf[0:POOL_HALO, :] = jnp.zeros((POOL_HALO, POOL_WIDTH), F32)

    h = (_rms(x) * gain_ref[...] * (1.0 + mod[1:2]) + mod[0:1]).astype(BF16)
    zb = _dot(h, win_ref[:, ZR_COLS:])
    zcol = lambda off, w: zb[:, off - ZR_COLS:off - ZR_COLS + w]
    def project_piece(n):
        cols = slice(n * MIXIN_PIECE_W, min((n + 1) * MIXIN_PIECE_W, ZR_COLS))
        zr_ref[0, :, cols] = _dot(h, win_ref[:, cols])

    project_piece(0)
    u = zcol(Z_POOL_OFF, POOL_WIDTH)
    ubuf[POOL_HALO:, :] = u
    ue = ubuf[...]
    s2 = ue + pltpu.roll(ue, 1, axis=0)
    s4 = s2 + pltpu.roll(s2, 2, axis=0)
    s8 = s4 + pltpu.roll(s4, 4, axis=0)
    s16 = s8 + pltpu.roll(s8, 8, axis=0)
    ubuf[0:POOL_HALO, :] = u[tm - POOL_HALO:, :]
    lane_p = lax.broadcasted_iota(jnp.int32, (tm, POOL_WIDTH), 1)
    grp = lane_p // (POOL_WIDTH // len(POOL_WINDOWS))
    win_sum = jnp.where(grp == 0, s2[POOL_HALO:], jnp.where(grp == 1, s4[POOL_HALO:],
                        jnp.where(grp == 2, s8[POOL_HALO:], s16[POOL_HALO:])))
    win = jnp.where(grp == 0, 2, jnp.where(grp == 1, 4, jnp.where(grp == 2, 8, 16)))
    t_abs = i * tm + lax.broadcasted_iota(jnp.int32, (tm, POOL_WIDTH), 0)
    cnt = jnp.minimum(t_abs + 1, win).astype(F32)
    p = win_sum / cnt - u
    yp = _dot(p.astype(BF16), poolw_ref[...]) * pools_ref[...]
    yp_ref[0] = yp.astype(BF16)
    project_piece(1)

    lane = lax.broadcasted_iota(jnp.int32, (tm, LANES), 1)
    cosf = cos_ref[0]
    sinf = sin_ref[0]

    q_lat = zcol(Z_QLAT_OFF, MLA_Q_LORA)
    kv_lat = zcol(Z_KVLAT_OFF, MLA_KV_LORA)
    k_rope = zcol(Z_KROPE_OFF, LANES)
    qn = (_rms(q_lat) * qg_ref[...]).astype(BF16)
    kvn = (_rms(kv_lat) * kvg_ref[...]).astype(BF16)
    q = _dot(qn, wq_ref[...])
    kx = _dot(kvn, wk_ref[...])
    v = _dot(kvn, wv_ref[...])
    v_ref[0] = v.astype(BF16)
    project_piece(2)
    k_pe = _rope(k_rope, cosf, sinf, lane)
    qs, ks = [], []
    for hd in range(MLA_HEADS):
        sl = slice(hd * LANES, (hd + 1) * LANES)
        qs.append(_rope(q[:, sl], cosf, sinf, lane))
        ks.append(kx[:, sl] + k_pe)
    q = jnp.concatenate(qs, axis=1)
    k = jnp.concatenate(ks, axis=1)
    project_piece(3)
    qss = _mm_exact_rhs(q * q, bd_ref[...]) * (1.0 / MLA_QK_DIM)
    kss = _mm_exact_rhs(k * k, bd_ref[...]) * (1.0 / MLA_QK_DIM)
    qkg = qkg_ref[...]
    q = q * lax.rsqrt(qss + NORM_EPS) * qkg[0:1] * (MLA_QK_DIM ** -0.5)
    k = k * lax.rsqrt(kss + NORM_EPS) * qkg[1:2]
    q_ref[0] = q.astype(BF16)
    k_ref[0] = k.astype(BF16)


def _mixin(x, mod_l, gain, win, cosf, sinf, poolw, pools, qg, kvg, wq, wk, wv, qkg, bd128):
    B, T, _ = x.shape
    tm = TM_MIX
    const = lambda shape: pl.BlockSpec(shape, lambda b, i: tuple(0 for _ in shape))
    tok = lambda w: pl.BlockSpec((1, tm, w), lambda b, i: (b, i, 0))
    return pl.pallas_call(
        _mixin_kernel,
        grid=(B, T // tm),
        in_specs=[tok(D_MODEL),
                  pl.BlockSpec((1, 6, D_MODEL), lambda b, i: (b, 0, 0)),
                  const((1, D_MODEL)), const((D_MODEL, Z_COLS)),
                  tok(LANES), tok(LANES),
                  const((POOL_WIDTH, POOL_WIDTH)), const((1, POOL_WIDTH)),
                  const((1, MLA_Q_LORA)), const((1, MLA_KV_LORA)),
                  const((MLA_Q_LORA, MLA_HEADS * LANES)), const((MLA_KV_LORA, MLA_HEADS * LANES)),
                  const((MLA_KV_LORA, MLA_HEADS * HEAD_DIM)), const((2, MLA_HEADS * LANES)),
                  const((MLA_HEADS * LANES, MLA_HEADS * LANES))],
        out_specs=[tok(ZR_COLS), tok(POOL_WIDTH), tok(MLA_HEADS * LANES), tok(MLA_HEADS * LANES),
                   tok(MLA_HEADS * HEAD_DIM)],
        out_shape=[jax.ShapeDtypeStruct((B, T, ZR_COLS), F32),
                   jax.ShapeDtypeStruct((B, T, POOL_WIDTH), BF16),
                   jax.ShapeDtypeStruct((B, T, MLA_HEADS * LANES), BF16),
                   jax.ShapeDtypeStruct((B, T, MLA_HEADS * LANES), BF16),
                   jax.ShapeDtypeStruct((B, T, MLA_HEADS * HEAD_DIM), BF16)],
        scratch_shapes=[pltpu.VMEM((POOL_HALO + tm, POOL_WIDTH), F32)],
        compiler_params=_cparams(("arbitrary", "arbitrary")),
    )(x, mod_l, gain, win, cosf, sinf, poolw, pools, qg, kvg, wq, wk, wv, qkg, bd128)


WKV_PASSES_SCORE = 1
WKV_PASSES_INV = 1
WKV_PASSES_APPLY = 1
WKV_PASSES_STATE = 1
WKV_STEP_CHUNKS = 4


def _stack_heads(xp, lane):
    return jnp.concatenate([jnp.where(lane < HEAD_DIM, xp, 0.0),
                            jnp.where(lane >= HEAD_DIM, xp, 0.0)], axis=0)


def _wkv_prep(r, lw, k, v, kk, a, tri, masks):
    L = r[0].shape[0]
    n = 2 * L
    nc = len(r)
    each = lambda f, *ls: [f(*xs) for xs in zip(*ls)]
    lane = lax.broadcasted_iota(jnp.int32, (L, LANES), 1)
    stack = lambda x: _stack_heads(x, lane)
    cum = each(lambda x: _mm_exact_rhs_left(tri, x), lw)
    cum_last = each(lambda c: c[L - 1:L, :], cum)
    e_w = each(jnp.exp, cum)
    e_wm = each(lambda c, x: jnp.exp(c - x), cum, lw)
    e_iw = each(lambda c: jnp.exp(-c), cum)
    e_d = each(lambda cl, c: jnp.exp(cl - c), cum_last, cum)
    beta = each(lambda x, y: x * y, kk, a)
    r_t = each(lambda x, e: stack(x * e), r, e_w)
    a_t = each(lambda x, e: stack(-x * e), kk, e_wm)
    b_t = each(lambda x, e: stack(x * e), beta, e_iw)
    k_t = each(lambda x, e: stack(x * e), k, e_iw)
    b_d = each(lambda x, e: stack(x * e), beta, e_d)
    k_d = each(lambda x, e: stack(x * e), k, e_d)
    v_s = each(stack, v)
    yield
    g = each(lambda at, rt, bt, kt: _mm(jnp.concatenate([at, rt], axis=0),
                                        jnp.concatenate([bt, kt], axis=0), NT, WKV_PASSES_SCORE),
             a_t, r_t, b_t, k_t)
    strict, incl, levels = masks
    a_ab = each(lambda x: jnp.where(strict, x[:n, :n], 0.0), g)
    a_ak = each(lambda x: jnp.where(strict, x[:n, n:], 0.0), g)
    s_rb = each(lambda x: jnp.where(incl, x[n:, :n], 0.0), g)
    s_rk = each(lambda x: jnp.where(incl, x[n:, n:], 0.0), g)
    eye = jnp.where(levels[0][1], 1.0, 0.0)
    tinv = each(lambda x: eye + jnp.where(levels[0][0], x, 0.0), a_ab)
    yield
    for lvl_mask, _ in levels[1:]:
        et = each(lambda x, t: _mm(jnp.where(lvl_mask, x, 0.0), t, NN, WKV_PASSES_INV), a_ab, tinv)
        tinv = each(lambda t, x: t + _mm(t, x, NN, WKV_PASSES_INV), tinv, et)
        yield
    av = each(lambda x, y: _mm(x, y, NN, WKV_PASSES_APPLY), a_ak, v_s)
    tx = each(lambda t, x, y: _mm(t, jnp.concatenate([x, y], axis=1), NN, WKV_PASSES_APPLY),
              tinv, a_t, av)
    yield
    ra = each(lambda rt, s, x: rt + _mm(s, x[:, :LANES], NN, WKV_PASSES_APPLY), r_t, s_rb, tx)
    c2 = each(lambda sb, sk, x, vs: _mm(jnp.concatenate([sb, sk], axis=1),
                                        jnp.concatenate([x[:, LANES:], vs], axis=0),
                                        NN, WKV_PASSES_APPLY), s_rb, s_rk, tx, v_s)
    yield
    tb = each(lambda x, bd: _mm(x.T, bd, NN, WKV_PASSES_APPLY), tx, b_d)
    c3 = each(lambda x, vs, kd: x[LANES:] + _mm(vs.T, kd, NN, WKV_PASSES_APPLY), tb, v_s, k_d)
    return [(ra[i], c2[i], jnp.exp(cum_last[i]), tb[i][:LANES], c3[i]) for i in range(nc)]


def _mm_exact_rhs_left(tri_bf16, x):
    x0 = x.astype(BF16)
    r1 = x - x0.astype(F32)
    x1 = r1.astype(BF16)
    x2 = (r1 - x1.astype(F32)).astype(BF16)
    return _dot(tri_bf16, x0) + (_dot(tri_bf16, x1) + _dot(tri_bf16, x2))


def _wkv_masks(L):
    n = 2 * L
    row = lax.broadcasted_iota(jnp.int32, (n, n), 0)
    col = lax.broadcasted_iota(jnp.int32, (n, n), 1)
    strict = row > col
    incl = row >= col
    levels = []
    m = 1
    while m < L:
        same = (row // (2 * m)) == (col // (2 * m))
        lvl = same & ((row % (2 * m)) >= m) & ((col % (2 * m)) < m)
        levels.append((lvl, row == col))
        m *= 2
    return strict, incl, levels


def _rwkv_kernel(has_vres, *refs):
    if has_vres:
        (z_ref, vf_ref, mu_ref, vec_ref, w2a2_ref, g2v2_ref, bd_ref,
         y_ref, carry, state) = refs
    else:
        (z_ref, mu_ref, vec_ref, w2a2_ref, g2v2_ref, bd_ref,
         y_ref, vout_ref, carry, state) = refs
    c = pl.program_id(1)
    rows = z_ref.shape[1]
    L = WKV_CHUNK
    W = RWKV_WIDTH

    @pl.when(c == 0)
    def _():
        carry[...] = jnp.zeros(carry.shape, F32)
        state[...] = jnp.zeros(state.shape, F32)

    z = z_ref[0]
    row = lax.broadcasted_iota(jnp.int32, z.shape, 0)
    prev = jnp.where(row == 0, carry[SUBLANES - 1:SUBLANES, :], pltpu.roll(z, 1, axis=0))
    carry[...] = z[rows - SUBLANES:, :]
    zs_all = z + mu_ref[...] * (prev - z)
    vec = vec_ref[...]
    w0, a0, k_k, k_a, r_k, ln_g, ln_b, v0 = (vec[j:j + 1] for j in range(8))
    bd = bd_ref[...]
    masks = _wkv_masks(L)
    rowt = lax.broadcasted_iota(jnp.int32, (L, L), 0)
    colt = lax.broadcasted_iota(jnp.int32, (L, L), 1)
    tri = jnp.where(rowt >= colt, 1.0, 0.0).astype(BF16)
    n_pairs = RWKV_HEADS // 2
    S_now = [[state[p] for p in range(n_pairs)]]

    def group(r0, r1):
        zs = zs_all[r0:r1]
        r = zs[:, 0:W]
        k = zs[:, W:2 * W]
        v = zs[:, 2 * W:3 * W]
        wa = zs[:, 3 * W:3 * W + LANES]
        gb = zs[:, 3 * W + LANES:ZR_COLS]
        lane_a = lax.broadcasted_iota(jnp.int32, wa.shape, 1)
        t1 = _dot(jnp.where(lane_a < RWKV_DECAY_LORA, jnp.tanh(wa), wa).astype(BF16), w2a2_ref[...])
        lane_g = lax.broadcasted_iota(jnp.int32, gb.shape, 1)
        t2 = _dot(jnp.where(lane_g < RWKV_GATE_LORA, _sigmoid(gb), gb).astype(BF16), g2v2_ref[...])
        yield
        xw = w0 + t1[:, :W]
        w_log = -(jnp.maximum(-xw, 0.0) + jnp.log1p(jnp.exp(-jnp.abs(xw)))) - 0.5
        lw = -jnp.exp(w_log)
        a = _sigmoid(a0 + t1[:, W:])
        g = t2[:, :W]
        if has_vres:
            v = v + (vf_ref[0, r0:r1, :] - v) * _sigmoid(v0 + t2[:, W:])
        else:
            vout_ref[0, r0:r1, :] = v
        kk = k * k_k
        nrm = jnp.sqrt(_mm_exact_rhs(kk * kk, bd))
        kk = kk / jnp.maximum(nrm, 1e-12)
        k = k * (1.0 + (a - 1.0) * k_a)
        yield
        n_chunks = (r1 - r0) // L
        idx = [(ch, p) for ch in range(n_chunks) for p in range(n_pairs)]
        cut = lambda x: [x[ch * L:(ch + 1) * L, p * LANES:(p + 1) * LANES] for ch, p in idx]
        res = yield from _wkv_prep(cut(r), cut(lw), cut(k), cut(v), cut(kk), cut(a), tri, masks)
        prep = dict(zip(idx, res))
        yield
        S = S_now[0]
        y_rows = []
        for ch in range(n_chunks):
            y_s = [_mm(prep[ch, p][0], S[p], NT, WKV_PASSES_STATE) + prep[ch, p][1] for p in range(n_pairs)]
            y_rows.append(jnp.concatenate([x[:L] + x[L:] for x in y_s], axis=1))
            S = [S[p] * prep[ch, p][2] + _mm(S[p], prep[ch, p][3], NN, WKV_PASSES_STATE) + prep[ch, p][4]
                 for p in range(n_pairs)]
        S_now[0] = S
        y = jnp.concatenate(y_rows, axis=0)
        yield
        inv = 1.0 / HEAD_DIM
        mean = _mm_exact_rhs(y, bd) * inv
        yc = y - mean
        var = _mm_exact_rhs(yc * yc, bd) * inv
        yn = yc * lax.rsqrt(var + RWKV_LNX_EPS) * ln_g + ln_b
        bonus = _mm_exact_rhs(r * k * r_k, bd) * v
        y_ref[0, r0:r1, :] = ((yn + bonus) * g).astype(BF16)

    for _ in group(0, rows):
        pass
    for p in range(n_pairs):
        state[p] = S_now[0][p]


def _rwkv(zr, v_first, mu, vec8, w2a2, g2v2, bd64):
    B, T, _ = zr.shape
    L = WKV_CHUNK * WKV_STEP_CHUNKS
    has_vres = v_first is not None
    const = lambda shape: pl.BlockSpec(shape, lambda b, c: tuple(0 for _ in shape))
    tok = lambda w: pl.BlockSpec((1, L, w), lambda b, c: (b, c, 0))
    in_specs = [tok(ZR_COLS)]
    args = [zr]
    if has_vres:
        in_specs.append(tok(RWKV_WIDTH))
        args.append(v_first)
    in_specs += [const((1, ZR_COLS)), const((8, RWKV_WIDTH)), const((LANES, 2 * RWKV_WIDTH)),
                 const((2 * LANES, 2 * RWKV_WIDTH)), const((RWKV_WIDTH, RWKV_WIDTH))]
    args += [mu, vec8, w2a2, g2v2, bd64]
    out_specs = [tok(RWKV_WIDTH)]
    out_shape = [jax.ShapeDtypeStruct((B, T, RWKV_WIDTH), BF16)]
    if not has_vres:
        out_specs.append(tok(RWKV_WIDTH))
        out_shape.append(jax.ShapeDtypeStruct((B, T, RWKV_WIDTH), F32))
    outs = pl.pallas_call(
        functools.partial(_rwkv_kernel, has_vres),
        grid=(B, T // L),
        in_specs=in_specs, out_specs=out_specs, out_shape=out_shape,
        scratch_shapes=[pltpu.VMEM((SUBLANES, ZR_COLS), F32),
                        pltpu.VMEM((RWKV_HEADS // 2, LANES, LANES), F32)],
        compiler_params=_cparams(("arbitrary", "arbitrary")),
    )(*args)
    return (outs[0], v_first) if has_vres else (outs[0], outs[1])


def _attn_step(q_ref, k_ref, v_ref, m_sc, l_sc, acc_sc, masked):
    tq = q_ref.shape[1]
    tk = k_ref.shape[1]
    lane = lax.broadcasted_iota(jnp.int32, (tq, LANES), 1)
    if masked:
        rowi = lax.broadcasted_iota(jnp.int32, (tq, tk), 0)
        coli = lax.broadcasted_iota(jnp.int32, (tq, tk), 1)
        keep = coli <= rowi
    heads = range(MLA_HEADS)
    s = [_dot(q_ref[0, :, hd * LANES:(hd + 1) * LANES], k_ref[0, :, hd * LANES:(hd + 1) * LANES], NT)
         for hd in heads]
    if masked:
        s = [jnp.where(keep, x, NEG_INF) for x in s]
    m_prev = [m_sc[hd] for hd in heads]
    m_new = [jnp.maximum(m_prev[hd], jnp.max(s[hd], axis=-1, keepdims=True)) for hd in heads]
    alpha = [jnp.exp(m_prev[hd] - m_new[hd]) for hd in heads]
    p = [jnp.exp(s[hd] - jnp.concatenate([m_new[hd]] * (tk // LANES), axis=1)) for hd in heads]
    for hd in heads:
        l_sc[hd] = alpha[hd] * l_sc[hd] + jnp.sum(p[hd], axis=-1, keepdims=True)
        m_sc[hd] = m_new[hd]
    pv = [_dot(p[hd].astype(BF16), v_ref[0, :, (hd // 2) * LANES:(hd // 2 + 1) * LANES]) for hd in heads]
    first = lane < HEAD_DIM
    for pr in range(MLA_HEADS // 2):
        acc_sc[pr] = (acc_sc[pr] * jnp.where(first, alpha[2 * pr], alpha[2 * pr + 1])
                      + jnp.where(first, pv[2 * pr], pv[2 * pr + 1]))


def _attn_step_bounded(q_ref, k_ref, v_ref, cb_ref, l_sc, acc_sc, masked):
    tq = q_ref.shape[1]
    tk = k_ref.shape[1]
    lane = lax.broadcasted_iota(jnp.int32, (tq, LANES), 1)
    if masked:
        rowi = lax.broadcasted_iota(jnp.int32, (tq, tk), 0)
        coli = lax.broadcasted_iota(jnp.int32, (tq, tk), 1)
        keep = coli <= rowi
    pv = []
    for hd in range(MLA_HEADS):
        s = _dot(q_ref[0, :, hd * LANES:(hd + 1) * LANES], k_ref[0, :, hd * LANES:(hd + 1) * LANES], NT)
        c = cb_ref[0, hd:hd + 1, :]
        p = jnp.exp(s - jnp.concatenate([c] * (tk // LANES), axis=1))
        if masked:
            p = jnp.where(keep, p, 0.0)
        part = p[:, 0:LANES]
        for t in range(1, tk // LANES):
            part = part + p[:, t * LANES:(t + 1) * LANES]
        l_sc[hd] = l_sc[hd] + part
        pv.append(_dot(p.astype(BF16), v_ref[0, :, (hd // 2) * LANES:(hd // 2 + 1) * LANES]))
    first = lane < HEAD_DIM
    for pr in range(MLA_HEADS // 2):
        acc_sc[pr] = acc_sc[pr] + jnp.where(first, pv[2 * pr], pv[2 * pr + 1])


def _attn_finish(o_ref, l_sc, acc_sc, lane_partial):
    tq = o_ref.shape[1]
    lane = lax.broadcasted_iota(jnp.int32, (tq, LANES), 1)
    outs = []
    for pr in range(MLA_HEADS // 2):
        la, lb = l_sc[2 * pr], l_sc[2 * pr + 1]
        if lane_partial:
            la = jnp.sum(la, axis=-1, keepdims=True)
            lb = jnp.sum(lb, axis=-1, keepdims=True)
        outs.append(acc_sc[pr] / jnp.where(lane < HEAD_DIM, la, lb))
    o_ref[0] = jnp.concatenate(outs, axis=1).astype(BF16)


def _attn_kernel(ok_ref, qi_ref, kj_ref, q_ref, k_ref, v_ref, cb_ref, o_ref, m_sc, l_sc, acc_sc):
    i = qi_ref[pl.program_id(1)]
    j = kj_ref[pl.program_id(1)]
    bounded = ok_ref[pl.program_id(0)] == 1
    exact = jnp.logical_not(bounded)

    @pl.when(j == 0)
    def _():
        m_sc[...] = jnp.full(m_sc.shape, NEG_INF, F32)
        l_sc[...] = jnp.zeros(l_sc.shape, F32)
        acc_sc[...] = jnp.zeros(acc_sc.shape, F32)

    @pl.when(bounded & (j < i))
    def _():
        _attn_step_bounded(q_ref, k_ref, v_ref, cb_ref, l_sc, acc_sc, masked=False)

    @pl.when(bounded & (j == i))
    def _():
        _attn_step_bounded(q_ref, k_ref, v_ref, cb_ref, l_sc, acc_sc, masked=True)
        _attn_finish(o_ref, l_sc, acc_sc, lane_partial=True)

    @pl.when(exact & (j < i))
    def _():
        _attn_step(q_ref, k_ref, v_ref, m_sc, l_sc, acc_sc, masked=False)

    @pl.when(exact & (j == i))
    def _():
        _attn_step(q_ref, k_ref, v_ref, m_sc, l_sc, acc_sc, masked=True)
        _attn_finish(o_ref, l_sc, acc_sc, lane_partial=False)


def _attention(q, k, v, qk_gain):
    B, T, _ = q.shape
    nq = T // TQ
    gmax = jnp.max(jnp.abs(qk_gain), axis=1)
    c = gmax[0] * gmax[1] * (MLA_QK_DIM ** 0.5) * ATTN_BOUND_SLACK
    ok = jnp.broadcast_to((c <= ATTN_BOUND_MAX).astype(jnp.int32), (B,))
    cb = jnp.broadcast_to(c, (B, MLA_HEADS, LANES))
    pairs = [(i, j) for i in range(nq) for j in range(i + 1)]
    qi = jnp.asarray([p[0] for p in pairs], jnp.int32)
    kj = jnp.asarray([p[1] for p in pairs], jnp.int32)
    grid_spec = pltpu.PrefetchScalarGridSpec(
        num_scalar_prefetch=3,
        grid=(B, len(pairs)),
        in_specs=[pl.BlockSpec((1, TQ, MLA_HEADS * LANES), lambda b, t, ok, qi, kj: (b, qi[t], 0)),
                  pl.BlockSpec((1, TQ, MLA_HEADS * LANES), lambda b, t, ok, qi, kj: (b, kj[t], 0)),
                  pl.BlockSpec((1, TQ, MLA_HEADS * HEAD_DIM), lambda b, t, ok, qi, kj: (b, kj[t], 0)),
                  pl.BlockSpec((1, MLA_HEADS, LANES), lambda b, t, ok, qi, kj: (b, 0, 0))],
        out_specs=pl.BlockSpec((1, TQ, MLA_HEADS * HEAD_DIM), lambda b, t, ok, qi, kj: (b, qi[t], 0)),
        scratch_shapes=[pltpu.VMEM((MLA_HEADS, TQ, LANES), F32),
                        pltpu.VMEM((MLA_HEADS, TQ, LANES), F32),
                        pltpu.VMEM((MLA_HEADS // 2, TQ, LANES), F32)])
    return pl.pallas_call(
        _attn_kernel,
        grid_spec=grid_spec,
        out_shape=jax.ShapeDtypeStruct((B, T, MLA_HEADS * HEAD_DIM), BF16),
        compiler_params=_cparams(("arbitrary", "arbitrary")),
    )(ok, qi, kj, q, k, v, cb)


def _mixout_kernel(has_router, *refs):
    if has_router:
        (x_ref, yr_ref, yp_ref, ym_ref, wo_ref, mod_ref, gain_ref, rt_ref, tri_ref,
         xo_ref, h_ref, route_ref, cnt_ref, cnt_sc) = refs
    else:
        x_ref, yr_ref, yp_ref, ym_ref, wo_ref, mod_ref, gain_ref, xo_ref, h_ref = refs
    mod = mod_ref[0]
    o1 = RWKV_WIDTH
    o2 = RWKV_WIDTH + POOL_WIDTH
    mix = (_dot(yr_ref[0], wo_ref[0:o1, :]) + _dot(yp_ref[0], wo_ref[o1:o2, :])
           + _dot(ym_ref[0], wo_ref[o2:, :]))
    x = x_ref[0] + mod[2:3] * mix
    xo_ref[0] = x
    h = _rms(x) * gain_ref[...] * (1.0 + mod[4:5]) + mod[3:4]
    if not has_router:
        h_ref[0] = h.astype(BF16)
    else:
        hp = _pack_bf16_pairs(h)
        for ck in range(ROW_CHUNKS):
            h_ref[ck, 0] = hp[:, ck * PACK_CHUNK_W:(ck + 1) * PACK_CHUNK_W]
        logits = _dot(h.astype(BF16), rt_ref[...])
        lane = lax.broadcasted_iota(jnp.int32, logits.shape, 1).astype(F32)
        lg = jnp.where(lane < N_EXPERTS, logits, -jnp.inf)
        m1 = jnp.max(lg, axis=-1, keepdims=True)
        i1 = jnp.min(jnp.where(lg == m1, lane, float(LANES)), axis=-1, keepdims=True)
        lg2 = jnp.where(lane == i1, -jnp.inf, lg)
        m2 = jnp.max(lg2, axis=-1, keepdims=True)
        i2 = jnp.min(jnp.where(lg2 == m2, lane, float(LANES)), axis=-1, keepdims=True)
        e2 = jnp.exp(m2 - m1)
        g1 = 1.0 / (1.0 + e2)
        g2 = e2 / (1.0 + e2)
        first = (pl.program_id(0) == 0) & (pl.program_id(1) == 0)

        @pl.when(first)
        def _():
            cnt_sc[...] = jnp.zeros(cnt_sc.shape, F32)

        hit1 = lane == i1
        hit2 = lane == i2
        onehot = jnp.where(hit1 | hit2, 1.0, 0.0)
        prefix = _dot(tri_ref[...], onehot.astype(BF16)) + cnt_sc[0:1, :]
        r1 = jnp.sum(jnp.where(hit1, prefix, 0.0), axis=-1, keepdims=True)
        r2 = jnp.sum(jnp.where(hit2, prefix, 0.0), axis=-1, keepdims=True)
        cnt_sc[...] = cnt_sc[...] + jnp.sum(onehot, axis=0, keepdims=True)
        cnt_ref[...] = cnt_sc[...]
        vals = (i1, i2, g1, g2, r1, r2)
        route = jnp.zeros(logits.shape, F32)
        for pos, val in enumerate(vals):
            route = jnp.where(lane == pos, val, route)
        route_ref[0] = route


def _mixout(x, yr, yp, ym, wo, mod_l, gain, router_p):
    B, T, _ = x.shape
    tm = TM_MIX
    has_router = router_p is not None
    const = lambda shape: pl.BlockSpec(shape, lambda b, i: tuple(0 for _ in shape))
    tok = lambda w: pl.BlockSpec((1, tm, w), lambda b, i: (b, i, 0))
    in_specs = [tok(D_MODEL), tok(RWKV_WIDTH), tok(POOL_WIDTH), tok(MLA_HEADS * HEAD_DIM),
                const((D_MODEL, D_MODEL)), pl.BlockSpec((1, 6, D_MODEL), lambda b, i: (b, 0, 0)),
                const((1, D_MODEL))]
    args = [x, yr, yp, ym, wo, mod_l, gain]
    out_specs = [tok(D_MODEL), tok(D_MODEL)]
    out_shape = [jax.ShapeDtypeStruct((B, T, D_MODEL), F32), jax.ShapeDtypeStruct((B, T, D_MODEL), BF16)]
    scratch = []
    if has_router:
        out_specs[1] = pl.BlockSpec((ROW_CHUNKS, 1, tm, PACK_CHUNK_W), lambda b, i: (0, b, i, 0))
        out_shape[1] = jax.ShapeDtypeStruct((ROW_CHUNKS, B, T, PACK_CHUNK_W), jnp.uint32)
        ids = np.arange(tm)
        tri = jnp.asarray(ids[:, None] > ids[None, :], BF16)
        in_specs += [const((D_MODEL, LANES)), const((tm, tm))]
        args += [router_p, tri]
        out_specs += [tok(LANES), const((SUBLANES, LANES))]
        out_shape += [jax.ShapeDtypeStruct((B, T, LANES), F32),
                      jax.ShapeDtypeStruct((SUBLANES, LANES), F32)]
        scratch = [pltpu.VMEM((SUBLANES, LANES), F32)]
    return pl.pallas_call(
        functools.partial(_mixout_kernel, has_router),
        grid=(B, T // tm),
        in_specs=in_specs, out_specs=out_specs, out_shape=out_shape, scratch_shapes=scratch,
        compiler_params=_cparams(("arbitrary", "arbitrary")),
    )(*args)


def _ffn_kernel(h_ref, wg_ref, wu_ref, wo_ref, x_ref, mod_ref, o_ref, acc):
    j = pl.program_id(1)

    @pl.when(j == 0)
    def _():
        acc[...] = jnp.zeros(acc.shape, F32)

    h = h_ref[...]
    gg = _dot(h, wg_ref[...])
    uu = _dot(h, wu_ref[...])
    acc[...] += _dot((_silu(gg) * uu).astype(BF16), wo_ref[...])

    @pl.when(j == pl.num_programs(1) - 1)
    def _():
        o_ref[...] = x_ref[...] + mod_ref[0][5:6] * acc[...]


def _ffn(h2, w_in, w_out, x, mod_l):
    N = h2.shape[0]
    T = N // mod_l.shape[0]
    tm, tf = TM_FFN, TF_FFN
    nf = D_FF // tf
    per_b = T // tm
    return pl.pallas_call(
        _ffn_kernel,
        grid=(N // tm, nf),
        in_specs=[pl.BlockSpec((tm, D_MODEL), lambda i, j: (i, 0)),
                  pl.BlockSpec((D_MODEL, tf), lambda i, j: (0, j)),
                  pl.BlockSpec((D_MODEL, tf), lambda i, j: (0, j + nf)),
                  pl.BlockSpec((tf, D_MODEL), lambda i, j: (j, 0)),
                  pl.BlockSpec((tm, D_MODEL), lambda i, j: (i, 0)),
                  pl.BlockSpec((1, 6, D_MODEL), lambda i, j: (i // per_b, 0, 0))],
        out_specs=pl.BlockSpec((tm, D_MODEL), lambda i, j: (i, 0)),
        out_shape=jax.ShapeDtypeStruct((N, D_MODEL), F32),
        scratch_shapes=[pltpu.VMEM((tm, D_MODEL), F32)],
        compiler_params=_cparams(("arbitrary", "arbitrary")),
    )(h2, w_in, w_in, w_out, x, mod_l)


def _moe_kernel(be_ref, nv_ref, last_ref, x_ref, wg_ref, wu_ref, wo_ref, o_ref, acc, xm):
    i = pl.program_id(0)
    j = pl.program_id(1)

    @pl.when(i <= last_ref[0])
    def _():
        @pl.when(j == 0)
        def _():
            acc[...] = jnp.zeros(acc.shape, F32)
            row = lax.broadcasted_iota(jnp.int32, (xm.shape[0], 1), 0)
            xp = jnp.concatenate([x_ref[ck] for ck in range(ROW_CHUNKS)], axis=1)
            xp = jnp.where(row < nv_ref[i], xp, jnp.uint32(0))
            xm[...] = _unpack_bf16_pairs(xp)

        x = xm[...]
        gg = _dot(x, wg_ref[0].astype(BF16))
        uu = _dot(x, wu_ref[0].astype(BF16))
        acc[...] += _dot((_silu(gg) * uu).astype(BF16), wo_ref[0].astype(BF16))

        @pl.when(j == pl.num_programs(1) - 1)
        def _():
            yp = _pack_bf16_pairs(acc[...])
            for ck in range(ROW_CHUNKS):
                o_ref[ck] = yp[:, ck * PACK_CHUNK_W:(ck + 1) * PACK_CHUNK_W]


def _moe_experts(xs, w_in, w_out, block_exp, n_valid, last_blk):
    n_rows = xs.shape[1]
    tm, tf = MOE_BLOCK, TF_MOE
    nf = D_FF_EXPERT // tf
    blk = lambda i, last: jnp.minimum(i, last[0])
    chunk = lambda i, j, last: jnp.where(i <= last[0], j, nf - 1)
    grid_spec = pltpu.PrefetchScalarGridSpec(
        num_scalar_prefetch=3,
        grid=(n_rows // tm, nf),
        in_specs=[pl.BlockSpec((ROW_CHUNKS, tm, PACK_CHUNK_W),
                               lambda i, j, be, nv, last: (0, blk(i, last), 0)),
                  pl.BlockSpec((1, D_MODEL, tf),
                               lambda i, j, be, nv, last: (be[blk(i, last)], 0, chunk(i, j, last))),
                  pl.BlockSpec((1, D_MODEL, tf),
                               lambda i, j, be, nv, last: (be[blk(i, last)], 0, chunk(i, j, last) + nf)),
                  pl.BlockSpec((1, tf, D_MODEL),
                               lambda i, j, be, nv, last: (be[blk(i, last)], chunk(i, j, last), 0))],
        out_specs=pl.BlockSpec((ROW_CHUNKS, tm, PACK_CHUNK_W), lambda i, j, be, nv, last: (0, blk(i, last), 0)),
        scratch_shapes=[pltpu.VMEM((tm, D_MODEL), F32), pltpu.VMEM((tm, D_MODEL), BF16)])
    return pl.pallas_call(
        _moe_kernel,
        grid_spec=grid_spec,
        out_shape=jax.ShapeDtypeStruct((ROW_CHUNKS, n_rows, PACK_CHUNK_W), jnp.uint32),
        compiler_params=_cparams(("arbitrary", "arbitrary")),
    )(block_exp, n_valid, last_blk, xs, w_in, w_in, w_out)


def _sc_mesh():
    return plsc.VectorSubcoreMesh(core_axis_name="c", subcore_axis_name="s")


def _sc_scatter_rows(x, dest, n_rows):
    N, D = x.shape
    K = dest.shape[0]
    win = SC_WINDOW

    @pl.kernel(out_type=jax.ShapeDtypeStruct((n_rows, D), x.dtype), mesh=_sc_mesh(), scratch_types=[])
    def scatter(x_hbm, d_hbm, o_hbm):
        def body(x_vmem, *idx_vmem):
            for iv in idx_vmem:
                pltpu.sync_copy(x_vmem, o_hbm.at[iv.at[0]])

        pltpu.emit_pipeline(
            body,
            grid=(N // win,),
            in_specs=[pl.BlockSpec((win, D), lambda i: (i, 0))]
            + [pl.BlockSpec((1, win), functools.partial(lambda k, i: (k, i), k)) for k in range(K)],
            out_specs=[],
            core_axis_name=("c", "s"),
            dimension_semantics=(pltpu.PARALLEL,),
        )(x_hbm, *([d_hbm] * K))

    return scatter(x, dest)


def _sc_gather_rows(x, idx):
    n = idx.shape[0]
    D = x.shape[1]
    win = SC_WINDOW

    @pl.kernel(out_type=jax.ShapeDtypeStruct((n, D), x.dtype), mesh=_sc_mesh(), scratch_types=[])
    def gather(x_hbm, i_hbm, o_hbm):
        def body(i_vmem, o_vmem):
            pltpu.sync_copy(x_hbm.at[i_vmem.at[0]], o_vmem)

        pltpu.emit_pipeline(
            body,
            grid=(n // win,),
            in_specs=[pl.BlockSpec((1, win), lambda i: (0, i))],
            out_specs=[pl.BlockSpec((win, D), lambda i: (i, 0))],
            core_axis_name=("c", "s"),
            dimension_semantics=(pltpu.PARALLEL,),
        )(i_hbm, o_hbm)

    return gather(x, idx.reshape(1, n))


def _combine_kernel(x_ref, ya_ref, yb_ref, route_ref, mod_ref, o_ref):
    rt = route_ref[...]
    ya = _unpack_bf16_pairs(jnp.concatenate([ya_ref[0, ck] for ck in range(ROW_CHUNKS)], axis=1))
    yb = _unpack_bf16_pairs(jnp.concatenate([yb_ref[0, ck] for ck in range(ROW_CHUNKS)], axis=1))
    f = rt[:, 2:3] * ya.astype(F32) + rt[:, 3:4] * yb.astype(F32)
    o_ref[...] = x_ref[...] + mod_ref[0][5:6] * f


def _combine(x, y2, route, mod_l):
    N = x.shape[0]
    T = N // mod_l.shape[0]
    tm = 1024
    per_b = T // tm
    tok = pl.BlockSpec((tm, D_MODEL), lambda i: (i, 0))
    slot = lambda k: pl.BlockSpec((1, ROW_CHUNKS, tm, PACK_CHUNK_W), lambda i: (k, 0, i, 0))
    return pl.pallas_call(
        _combine_kernel,
        grid=(N // tm,),
        in_specs=[tok, slot(0), slot(1),
                  pl.BlockSpec((tm, LANES), lambda i: (i, 0)),
                  pl.BlockSpec((1, 6, D_MODEL), lambda i: (i // per_b, 0, 0))],
        out_specs=tok,
        out_shape=jax.ShapeDtypeStruct((N, D_MODEL), F32),
        compiler_params=_cparams(("arbitrary",)),
    )(x, y2, y2, route, mod_l)


def _moe(h2, route, counts, w_in, w_out, x, mod_l):
    N = x.shape[0]
    blk = MOE_BLOCK
    cnt = counts[0, :N_EXPERTS].astype(jnp.int32)
    padded = (cnt + blk - 1) // blk * blk
    pend = jnp.cumsum(padded)
    pstart = pend - padded
    e = route[:, 0:TOP_K].astype(jnp.int32)
    rank = route[:, 2 * TOP_K:3 * TOP_K].astype(jnp.int32)
    dest = (jnp.take(pstart, e) + rank).T
    n_blocks = N * TOP_K // blk + N_EXPERTS
    bstart = jnp.arange(n_blocks, dtype=jnp.int32) * blk
    block_exp = jnp.minimum(jnp.sum((bstart[:, None] >= pend[None, :]).astype(jnp.int32), axis=1),
                            N_EXPERTS - 1)
    n_valid = jnp.clip(cnt[block_exp] - (bstart - pstart[block_exp]), 0, blk).astype(jnp.int32)
    n_rows = n_blocks * blk
    dest_ck = dest[:, None, :] + (jnp.arange(ROW_CHUNKS, dtype=jnp.int32) * n_rows)[None, :, None]
    xs = _sc_scatter_rows(h2, dest_ck.reshape(TOP_K, ROW_CHUNKS * N), ROW_CHUNKS * n_rows)
    last_blk = (pend[-1:] // blk - 1).astype(jnp.int32)
    yb = _moe_experts(xs.reshape(ROW_CHUNKS, n_rows, PACK_CHUNK_W), w_in, w_out, block_exp, n_valid,
                      last_blk)
    y2 = _sc_gather_rows(yb.reshape(ROW_CHUNKS * n_rows, PACK_CHUNK_W), dest_ck.reshape(-1))
    return _combine(x, y2.reshape(TOP_K, ROW_CHUNKS, N, PACK_CHUNK_W), route, mod_l)


def _layout_w_in(w, has_vres):
    W = RWKV_WIDTH
    off_gd = 3 * W + RWKV_DECAY_LORA + RWKV_ICLR_LORA
    off_pool = off_gd + RWKV_GATE_LORA
    off_q = off_pool + POOL_WIDTH
    off_kv = off_q + MLA_Q_LORA
    off_kr = off_kv + MLA_KV_LORA
    n_base = off_kr + MLA_QK_ROPE
    d = w.shape[0]
    zeros = lambda n: jnp.zeros((d, n), w.dtype)
    vd = w[:, n_base:n_base + RWKV_VRES_LORA] if has_vres else zeros(RWKV_VRES_LORA)
    cols = [w[:, :off_gd], w[:, off_gd:off_pool], vd, zeros(ZR_COLS - off_pool - RWKV_VRES_LORA),
            w[:, off_pool:off_q], w[:, off_q:off_kv], w[:, off_kv:off_kr],
            zeros(MLA_QK_NOPE), w[:, off_kr:n_base], zeros(LANES - MLA_QK_DIM)]
    return jnp.concatenate(cols, axis=1).astype(BF16)


def _pad_heads(w, per_head, keep_from, keep_n):
    K = w.shape[0]
    wh = w.reshape(K, MLA_HEADS, per_head)[:, :, keep_from:keep_from + keep_n]
    wh = jnp.pad(wh, ((0, 0), (0, 0), (0, LANES - keep_n)))
    return wh.reshape(K, MLA_HEADS * LANES)


def kernel(x, c, positions, w_ada, b_ada, norm_gain, w_in_first, w_in_rest, mu_shift, mu_shift_v,
           rwkv_vec, rwkv_v0, rwkv_w2, rwkv_a2, rwkv_g2, rwkv_v2, pool_w, pool_scale,
           mla_q_lat_gain, mla_kv_lat_gain, mla_wq_up, mla_wkv_up, mla_qk_gain, w_out, ffn_w_in,
           ffn_w_out, moe_router, moe_w_in, moe_w_out):
    B, T, D = x.shape
    depth = w_ada.shape[0]
    W = RWKV_WIDTH
    mod = _adaln(c, w_ada, b_ada).reshape(depth, B, 6, D)
    pos3 = positions.reshape(B, T, 1)
    inv_freq = ROPE_BASE ** (-jnp.arange(0, MLA_QK_ROPE, 2, dtype=F32) / MLA_QK_ROPE)
    freq = jnp.concatenate([jnp.zeros((MLA_QK_NOPE,), F32), inv_freq, inv_freq,
                            jnp.zeros((LANES - MLA_QK_DIM,), F32)]).reshape(1, LANES)
    cosf, sinf = _rope_tables(pos3, freq)
    hid = np.arange(W) // HEAD_DIM
    bd64 = jnp.asarray(hid[:, None] == hid[None, :], BF16)
    bid = np.arange(MLA_HEADS * LANES) // LANES
    bd128 = jnp.asarray(bid[:, None] == bid[None, :], BF16)

    v_first = None
    for l in range(depth):
        has_vres = l > 0
        mod_l = mod[l]
        win = _layout_w_in(w_in_first if l == 0 else w_in_rest[l - 1], has_vres)
        poolw = jax.scipy.linalg.block_diag(*[pool_w[l, g] for g in range(len(POOL_WINDOWS))]).astype(BF16)
        wq = _pad_heads(mla_wq_up[l], MLA_QK_DIM, 0, MLA_QK_DIM).astype(BF16)
        wk = _pad_heads(mla_wkv_up[l], MLA_QK_NOPE + HEAD_DIM, 0, MLA_QK_NOPE).astype(BF16)
        wv = mla_wkv_up[l].reshape(MLA_KV_LORA, MLA_HEADS, MLA_QK_NOPE + HEAD_DIM)[:, :, MLA_QK_NOPE:]
        wv = wv.reshape(MLA_KV_LORA, MLA_HEADS * HEAD_DIM).astype(BF16)
        qkg = jnp.tile(jnp.pad(mla_qk_gain[l], ((0, 0), (0, LANES - MLA_QK_DIM))), (1, MLA_HEADS))
        zr, y_pool, q, k, v = _mixin(
            x, mod_l, norm_gain[l, 0].reshape(1, D), win, cosf, sinf, poolw,
            pool_scale[l].reshape(1, -1), mla_q_lat_gain[l].reshape(1, -1),
            mla_kv_lat_gain[l].reshape(1, -1), wq, wk, wv, qkg, bd128)

        pad_mu = ZR_COLS - mu_shift.shape[1] - RWKV_VRES_LORA
        mu_v = mu_shift_v[l - 1] if has_vres else jnp.zeros((RWKV_VRES_LORA,), F32)
        mu = jnp.concatenate([mu_shift[l], mu_v, jnp.zeros((pad_mu,), F32)]).reshape(1, ZR_COLS)
        v0 = rwkv_v0[l - 1] if has_vres else jnp.zeros((W,), F32)
        vec8 = jnp.concatenate([rwkv_vec[l], v0[None]], axis=0)
        w2a2 = jax.scipy.linalg.block_diag(rwkv_w2[l], rwkv_a2[l]).astype(BF16)
        g2 = jnp.pad(rwkv_g2[l], ((0, 2 * LANES - RWKV_GATE_LORA), (0, 0)))
        if has_vres:
            v2 = jnp.pad(rwkv_v2[l - 1], ((RWKV_GATE_LORA, 2 * LANES - RWKV_GATE_LORA - RWKV_VRES_LORA), (0, 0)))
        else:
            v2 = jnp.zeros((2 * LANES, W), F32)
        g2v2 = jnp.concatenate([g2, v2], axis=1).astype(BF16)
        y_rwkv, v_first = _rwkv(zr, v_first, mu, vec8, w2a2, g2v2, bd64)

        y_mla = _attention(q, k, v, mla_qk_gain[l])

        is_moe = (l % 2 == 1)
        router_p = None
        if is_moe:
            router_p = jnp.pad(moe_router[l // 2], ((0, 0), (0, LANES - N_EXPERTS))).astype(BF16)
        outs = _mixout(x, y_rwkv, y_pool, y_mla, w_out[l].astype(BF16), mod_l,
                       norm_gain[l, 1].reshape(1, D), router_p)
        x_mid, h2 = outs[0], outs[1]
        xf = x_mid.reshape(B * T, D)
        if is_moe:
            xo = _moe(h2.reshape(ROW_CHUNKS * B * T, PACK_CHUNK_W), outs[2].reshape(B * T, LANES), outs[3],
                      moe_w_in[l // 2], moe_w_out[l // 2], xf, mod_l)
        else:
            xo = _ffn(h2.reshape(B * T, D), ffn_w_in[l // 2].astype(BF16), ffn_w_out[l // 2].astype(BF16),
                      xf, mod_l)
        x = xo.reshape(B, T, D)
    return x
```

```python
import functools

import numpy as np
import jax
import jax.numpy as jnp
from jax import lax
from jax.experimental import pallas as pl
from jax.experimental.pallas import tpu as pltpu
from jax.experimental.pallas import tpu_sc as plsc

F32 = jnp.float32
BF16 = jnp.bfloat16

D_MODEL = 1024
HEAD_DIM = 64
RWKV_WIDTH = 512
RWKV_HEADS = 8
POOL_WIDTH = 256
POOL_WINDOWS = (2, 4, 8, 16)
POOL_HALO = 16
MLA_HEADS = 4
MLA_QK_NOPE = 64
MLA_QK_ROPE = 32
MLA_QK_DIM = 96
MLA_Q_LORA = 256
MLA_KV_LORA = 128
ROPE_BASE = 10000.0
RWKV_DECAY_LORA = 64
RWKV_ICLR_LORA = 64
RWKV_VRES_LORA = 32
RWKV_GATE_LORA = 160
RWKV_LNX_EPS = 64e-5
D_FF = 2816
N_EXPERTS = 8
TOP_K = 2
D_FF_EXPERT = 3584
NORM_EPS = 1e-6
NEG_INF = -1e30

LANES = 128
SUBLANES = 8
VMEM_LIMIT = 56 * 1024 * 1024

ZR_COLS = 1920
Z_POOL_OFF = ZR_COLS
Z_QLAT_OFF = Z_POOL_OFF + POOL_WIDTH
Z_KVLAT_OFF = Z_QLAT_OFF + MLA_Q_LORA
Z_KROPE_OFF = Z_KVLAT_OFF + MLA_KV_LORA
Z_COLS = Z_KROPE_OFF + LANES

MIXIN_PIECE_W = 512
TM_MIX = 512
WKV_CHUNK = 64
TQ = 512
ATTN_BOUND_SLACK = 1.02
ATTN_BOUND_MAX = 40.0
TM_FFN = 512
TF_FFN = 1408
MOE_BLOCK = 1024
TF_MOE = 512
SC_WINDOW = 128
ROW_CHUNKS = 4
ROW_CHUNK_W = D_MODEL // ROW_CHUNKS
PACK_CHUNK_W = ROW_CHUNK_W // 2

SEGSUM_SPLITS = 1

NN = (((1,), (0,)), ((), ()))
NT = (((1,), (1,)), ((), ()))


def _dot(a, b, dims=NN):
    return lax.dot_general(a, b, dims, preferred_element_type=F32)


def _split2(a):
    hi = a.astype(BF16)
    lo = (a - hi.astype(F32)).astype(BF16)
    return hi, lo


def _mm(a, b, dims=NN, passes=3):
    if passes == 1:
        return _dot(a.astype(BF16), b.astype(BF16), dims)
    ah, al = _split2(a)
    bh, bl = _split2(b)
    return _dot(ah, bh, dims) + (_dot(ah, bl, dims) + _dot(al, bh, dims))


def _mm_exact_rhs(a, b_bf16, dims=NN, splits=SEGSUM_SPLITS):
    out = None
    rem = a
    for s in range(splits):
        part = rem.astype(BF16)
        term = _dot(part, b_bf16, dims)
        out = term if out is None else out + term
        if s + 1 < splits:
            rem = rem - part.astype(F32)
    return out


def _pack_bf16_pairs(h):
    w = h.shape[1] // 2
    lo = lax.bitcast_convert_type(h[:, :w].astype(BF16).astype(F32), jnp.uint32)
    hi = lax.bitcast_convert_type(h[:, w:].astype(BF16).astype(F32), jnp.uint32)
    return (lo >> 16) | (hi & jnp.uint32(0xFFFF0000))


def _unpack_bf16_pairs(p):
    lo = lax.bitcast_convert_type(p << 16, F32)
    hi = lax.bitcast_convert_type(p & jnp.uint32(0xFFFF0000), F32)
    return jnp.concatenate([lo, hi], axis=1).astype(BF16)


def _sigmoid(x):
    return 1.0 / (1.0 + jnp.exp(-x))


def _silu(x):
    return x * _sigmoid(x)


def _rms(x, eps=NORM_EPS):
    return x * lax.rsqrt(jnp.mean(x * x, axis=-1, keepdims=True) + eps)


def _cparams(sem):
    return pltpu.CompilerParams(dimension_semantics=sem, vmem_limit_bytes=VMEM_LIMIT)


def _adaln_kernel(c_ref, w_ref, b_ref, o_ref):
    ca = _silu(c_ref[...])
    o_ref[0] = _mm(ca, w_ref[0]) + b_ref[0]


def _adaln(c, w_ada, b_ada):
    L = w_ada.shape[0]
    B = c.shape[0]
    n = w_ada.shape[2] // D_MODEL
    return pl.pallas_call(
        _adaln_kernel,
        grid=(L, n),
        in_specs=[pl.BlockSpec((B, D_MODEL), lambda l, j: (0, 0)),
                  pl.BlockSpec((1, D_MODEL, D_MODEL), lambda l, j: (l, 0, j)),
                  pl.BlockSpec((1, 1, D_MODEL), lambda l, j: (l, 0, j))],
        out_specs=pl.BlockSpec((1, B, D_MODEL), lambda l, j: (l, 0, j)),
        out_shape=jax.ShapeDtypeStruct((L, B, n * D_MODEL), F32),
        compiler_params=_cparams(("arbitrary", "arbitrary")),
    )(c, w_ada, b_ada.reshape(L, 1, -1))


def _rope(x, cosf, sinf, lane):
    up = pltpu.roll(x, LANES - MLA_QK_ROPE // 2, axis=1)
    dn = pltpu.roll(x, MLA_QK_ROPE // 2, axis=1)
    rot = jnp.where(lane < MLA_QK_NOPE + MLA_QK_ROPE // 2, -up, dn)
    return x * cosf + rot * sinf


def _rope_kernel(pos_ref, freq_ref, cos_ref, sin_ref):
    tm = pos_ref.shape[1]
    lane = lax.broadcasted_iota(jnp.int32, (tm, LANES), 1)
    in_rope = (lane >= MLA_QK_NOPE) & (lane < MLA_QK_DIM)
    ang = pos_ref[0].astype(F32) * freq_ref[...]
    cos_ref[0] = jnp.where(in_rope, jnp.cos(ang), 1.0)
    sin_ref[0] = jnp.where(in_rope, jnp.sin(ang), 0.0)


def _rope_tables(pos3, freq):
    B, T, _ = pos3.shape
    tm = TM_MIX
    tok = lambda w: pl.BlockSpec((1, tm, w), lambda b, i: (b, i, 0))
    return pl.pallas_call(
        _rope_kernel,
        grid=(B, T // tm),
        in_specs=[tok(1), pl.BlockSpec((1, LANES), lambda b, i: (0, 0))],
        out_specs=[tok(LANES), tok(LANES)],
        out_shape=[jax.ShapeDtypeStruct((B, T, LANES), F32)] * 2,
        compiler_params=_cparams(("arbitrary", "arbitrary")),
    )(pos3, freq)


def _mixin_kernel(x_ref, mod_ref, gain_ref, win_ref, cos_ref, sin_ref, poolw_ref, pools_ref,
                  qg_ref, kvg_ref, wq_ref, wk_ref, wv_ref, qkg_ref, bd_ref,
                  zr_ref, yp_ref, q_ref, k_ref, v_ref, ubuf):
    i = pl.program_id(1)
    tm = x_ref.shape[1]
    x = x_ref[0]
    mod = mod_ref[0]
    @pl.when(i == 0)
    def _():
        ubuf[0:POOL_HALO, :] = jnp.zeros((POOL_HALO, POOL_WIDTH), F32)

    h = (_rms(x) * gain_ref[...] * (1.0 + mod[1:2]) + mod[0:1]).astype(BF16)
    zb = _dot(h, win_ref[:, ZR_COLS:])
    zcol = lambda off, w: zb[:, off - ZR_COLS:off - ZR_COLS + w]
    def project_piece(n):
        cols = slice(n * MIXIN_PIECE_W, min((n + 1) * MIXIN_PIECE_W, ZR_COLS))
        zr_ref[0, :, cols] = _dot(h, win_ref[:, cols])

    project_piece(0)
    u = zcol(Z_POOL_OFF, POOL_WIDTH)
    ubuf[POOL_HALO:, :] = u
    ue = ubuf[...]
    s2 = ue + pltpu.roll(ue, 1, axis=0)
    s4 = s2 + pltpu.roll(s2, 2, axis=0)
    s8 = s4 + pltpu.roll(s4, 4, axis=0)
    s16 = s8 + pltpu.roll(s8, 8, axis=0)
    ubuf[0:POOL_HALO, :] = u[tm - POOL_HALO:, :]
    lane_p = lax.broadcasted_iota(jnp.int32, (tm, POOL_WIDTH), 1)
    grp = lane_p // (POOL_WIDTH // len(POOL_WINDOWS))
    win_sum = jnp.where(grp == 0, s2[POOL_HALO:], jnp.where(grp == 1, s4[POOL_HALO:],
                        jnp.where(grp == 2, s8[POOL_HALO:], s16[POOL_HALO:])))
    win = jnp.where(grp == 0, 2, jnp.where(grp == 1, 4, jnp.where(grp == 2, 8, 16)))
    t_abs = i * tm + lax.broadcasted_iota(jnp.int32, (tm, POOL_WIDTH), 0)
    cnt = jnp.minimum(t_abs + 1, win).astype(F32)
    p = win_sum / cnt - u
    yp = _dot(p.astype(BF16), poolw_ref[...]) * pools_ref[...]
    yp_ref[0] = yp.astype(BF16)
    project_piece(1)

    lane = lax.broadcasted_iota(jnp.int32, (tm, LANES), 1)
    cosf = cos_ref[0]
    sinf = sin_ref[0]

    q_lat = zcol(Z_QLAT_OFF, MLA_Q_LORA)
    kv_lat = zcol(Z_KVLAT_OFF, MLA_KV_LORA)
    k_rope = zcol(Z_KROPE_OFF, LANES)
    qn = (_rms(q_lat) * qg_ref[...]).astype(BF16)
    kvn = (_rms(kv_lat) * kvg_ref[...]).astype(BF16)
    q = _dot(qn, wq_ref[...])
    kx = _dot(kvn, wk_ref[...])
    v = _dot(kvn, wv_ref[...])
    v_ref[0] = v.astype(BF16)
    project_piece(2)
    k_pe = _rope(k_rope, cosf, sinf, lane)
    qs, ks = [], []
    for hd in range(MLA_HEADS):
        sl = slice(hd * LANES, (hd + 1) * LANES)
        qs.append(_rope(q[:, sl], cosf, sinf, lane))
        ks.append(kx[:, sl] + k_pe)
    q = jnp.concatenate(qs, axis=1)
    k = jnp.concatenate(ks, axis=1)
    project_piece(3)
    qss = _mm_exact_rhs(q * q, bd_ref[...]) * (1.0 / MLA_QK_DIM)
    kss = _mm_exact_rhs(k * k, bd_ref[...]) * (1.0 / MLA_QK_DIM)
    qkg = qkg_ref[...]
    q = q * lax.rsqrt(qss + NORM_EPS) * qkg[0:1] * (MLA_QK_DIM ** -0.5)
    k = k * lax.rsqrt(kss + NORM_EPS) * qkg[1:2]
    q_ref[0] = q.astype(BF16)
    k_ref[0] = k.astype(BF16)


def _mixin(x, mod_l, gain, win, cosf, sinf, poolw, pools, qg, kvg, wq, wk, wv, qkg, bd128):
    B, T, _ = x.shape
    tm = TM_MIX
    const = lambda shape: pl.BlockSpec(shape, lambda b, i: tuple(0 for _ in shape))
    tok = lambda w: pl.BlockSpec((1, tm, w), lambda b, i: (b, i, 0))
    return pl.pallas_call(
        _mixin_kernel,
        grid=(B, T // tm),
        in_specs=[tok(D_MODEL),
                  pl.BlockSpec((1, 6, D_MODEL), lambda b, i: (b, 0, 0)),
                  const((1, D_MODEL)), const((D_MODEL, Z_COLS)),
                  tok(LANES), tok(LANES),
                  const((POOL_WIDTH, POOL_WIDTH)), const((1, POOL_WIDTH)),
                  const((1, MLA_Q_LORA)), const((1, MLA_KV_LORA)),
                  const((MLA_Q_LORA, MLA_HEADS * LANES)), const((MLA_KV_LORA, MLA_HEADS * LANES)),
                  const((MLA_KV_LORA, MLA_HEADS * HEAD_DIM)), const((2, MLA_HEADS * LANES)),
                  const((MLA_HEADS * LANES, MLA_HEADS * LANES))],
        out_specs=[tok(ZR_COLS), tok(POOL_WIDTH), tok(MLA_HEADS * LANES), tok(MLA_HEADS * LANES),
                   tok(MLA_HEADS * HEAD_DIM)],
        out_shape=[jax.ShapeDtypeStruct((B, T, ZR_COLS), F32),
                   jax.ShapeDtypeStruct((B, T, POOL_WIDTH), BF16),
                   jax.ShapeDtypeStruct((B, T, MLA_HEADS * LANES), BF16),
                   jax.ShapeDtypeStruct((B, T, MLA_HEADS * LANES), BF16),
                   jax.ShapeDtypeStruct((B, T, MLA_HEADS * HEAD_DIM), BF16)],
        scratch_shapes=[pltpu.VMEM((POOL_HALO + tm, POOL_WIDTH), F32)],
        compiler_params=_cparams(("arbitrary", "arbitrary")),
    )(x, mod_l, gain, win, cosf, sinf, poolw, pools, qg, kvg, wq, wk, wv, qkg, bd128)


WKV_PASSES_SCORE = 1
WKV_PASSES_INV = 1
WKV_PASSES_APPLY = 1
WKV_PASSES_STATE = 1
WKV_STEP_CHUNKS = 1
WKV_STEP_SEQS = 4


def _stack_heads(xp, lane):
    return jnp.concatenate([jnp.where(lane < HEAD_DIM, xp, 0.0),
                            jnp.where(lane >= HEAD_DIM, xp, 0.0)], axis=0)


def _wkv_prep(r, lw, k, v, kk, a, tri, masks):
    L = r[0].shape[0]
    n = 2 * L
    nc = len(r)
    each = lambda f, *ls: [f(*xs) for xs in zip(*ls)]
    lane = lax.broadcasted_iota(jnp.int32, (L, LANES), 1)
    stack = lambda x: _stack_heads(x, lane)
    cum = each(lambda x: _mm_exact_rhs_left(tri, x), lw)
    cum_last = each(lambda c: c[L - 1:L, :], cum)
    e_w = each(jnp.exp, cum)
    e_wm = each(lambda c, x: jnp.exp(c - x), cum, lw)
    e_iw = each(lambda c: jnp.exp(-c), cum)
    e_d = each(lambda cl, c: jnp.exp(cl - c), cum_last, cum)
    beta = each(lambda x, y: x * y, kk, a)
    r_t = each(lambda x, e: stack(x * e), r, e_w)
    a_t = each(lambda x, e: stack(-x * e), kk, e_wm)
    b_t = each(lambda x, e: stack(x * e), beta, e_iw)
    k_t = each(lambda x, e: stack(x * e), k, e_iw)
    b_d = each(lambda x, e: stack(x * e), beta, e_d)
    k_d = each(lambda x, e: stack(x * e), k, e_d)
    v_s = each(stack, v)
    yield
    g = each(lambda at, rt, bt, kt: _mm(jnp.concatenate([at, rt], axis=0),
                                        jnp.concatenate([bt, kt], axis=0), NT, WKV_PASSES_SCORE),
             a_t, r_t, b_t, k_t)
    strict, incl, levels = masks
    a_ab = each(lambda x: jnp.where(strict, x[:n, :n], 0.0), g)
    a_ak = each(lambda x: jnp.where(strict, x[:n, n:], 0.0), g)
    s_rb = each(lambda x: jnp.where(incl, x[n:, :n], 0.0), g)
    s_rk = each(lambda x: jnp.where(incl, x[n:, n:], 0.0), g)
    eye = jnp.where(levels[0][1], 1.0, 0.0)
    tinv = each(lambda x: eye + jnp.where(levels[0][0], x, 0.0), a_ab)
    yield
    for lvl_mask, _ in levels[1:]:
        et = each(lambda x, t: _mm(jnp.where(lvl_mask, x, 0.0), t, NN, WKV_PASSES_INV), a_ab, tinv)
        tinv = each(lambda t, x: t + _mm(t, x, NN, WKV_PASSES_INV), tinv, et)
        yield
    av = each(lambda x, y: _mm(x, y, NN, WKV_PASSES_APPLY), a_ak, v_s)
    tx = each(lambda t, x, y: _mm(t, jnp.concatenate([x, y], axis=1), NN, WKV_PASSES_APPLY),
              tinv, a_t, av)
    yield
    ra = each(lambda rt, s, x: rt + _mm(s, x[:, :LANES], NN, WKV_PASSES_APPLY), r_t, s_rb, tx)
    c2 = each(lambda sb, sk, x, vs: _mm(jnp.concatenate([sb, sk], axis=1),
                                        jnp.concatenate([x[:, LANES:], vs], axis=0),
                                        NN, WKV_PASSES_APPLY), s_rb, s_rk, tx, v_s)
    yield
    tb = each(lambda x, bd: _mm(x.T, bd, NN, WKV_PASSES_APPLY), tx, b_d)
    c3 = each(lambda x, vs, kd: x[LANES:] + _mm(vs.T, kd, NN, WKV_PASSES_APPLY), tb, v_s, k_d)
    return [(ra[i], c2[i], jnp.exp(cum_last[i]), tb[i][:LANES], c3[i]) for i in range(nc)]


def _mm_exact_rhs_left(tri_bf16, x):
    x0 = x.astype(BF16)
    r1 = x - x0.astype(F32)
    x1 = r1.astype(BF16)
    x2 = (r1 - x1.astype(F32)).astype(BF16)
    return _dot(tri_bf16, x0) + (_dot(tri_bf16, x1) + _dot(tri_bf16, x2))


def _wkv_masks(L):
    n = 2 * L
    row = lax.broadcasted_iota(jnp.int32, (n, n), 0)
    col = lax.broadcasted_iota(jnp.int32, (n, n), 1)
    strict = row > col
    incl = row >= col
    levels = []
    m = 1
    while m < L:
        same = (row // (2 * m)) == (col // (2 * m))
        lvl = same & ((row % (2 * m)) >= m) & ((col % (2 * m)) < m)
        levels.append((lvl, row == col))
        m *= 2
    return strict, incl, levels


def _rwkv_kernel(has_vres, *refs):
    if has_vres:
        (z_ref, vf_ref, mu_ref, vec_ref, w2a2_ref, g2v2_ref, bd_ref,
         y_ref, carry, state) = refs
    else:
        (z_ref, mu_ref, vec_ref, w2a2_ref, g2v2_ref, bd_ref,
         y_ref, vout_ref, carry, state) = refs
    c = pl.program_id(1)
    n_seq, seq_rows = z_ref.shape[0], z_ref.shape[1]
    rows = n_seq * seq_rows
    L = WKV_CHUNK
    W = RWKV_WIDTH

    @pl.when(c == 0)
    def _():
        carry[...] = jnp.zeros(carry.shape, F32)
        state[...] = jnp.zeros(state.shape, F32)

    flat = lambda ref: jnp.concatenate([ref[s] for s in range(n_seq)], axis=0)
    z = flat(z_ref)
    row = lax.broadcasted_iota(jnp.int32, z.shape, 0)
    prev = pltpu.roll(z, 1, axis=0)
    for s in range(n_seq):
        prev = jnp.where(row == s * seq_rows, carry[s, SUBLANES - 1:SUBLANES, :], prev)
        carry[s] = z[(s + 1) * seq_rows - SUBLANES:(s + 1) * seq_rows, :]
    zs_all = z + mu_ref[...] * (prev - z)
    if has_vres:
        vf_all = flat(vf_ref)
    vec = vec_ref[...]
    w0, a0, k_k, k_a, r_k, ln_g, ln_b, v0 = (vec[j:j + 1] for j in range(8))
    bd = bd_ref[...]
    masks = _wkv_masks(L)
    rowt = lax.broadcasted_iota(jnp.int32, (L, L), 0)
    colt = lax.broadcasted_iota(jnp.int32, (L, L), 1)
    tri = jnp.where(rowt >= colt, 1.0, 0.0).astype(BF16)
    n_pairs = RWKV_HEADS // 2
    S_now = [{(s, p): state[s * n_pairs + p] for s in range(n_seq) for p in range(n_pairs)}]

    def group(r0, r1):
        zs = zs_all[r0:r1]
        r = zs[:, 0:W]
        k = zs[:, W:2 * W]
        v = zs[:, 2 * W:3 * W]
        wa = zs[:, 3 * W:3 * W + LANES]
        gb = zs[:, 3 * W + LANES:ZR_COLS]
        lane_a = lax.broadcasted_iota(jnp.int32, wa.shape, 1)
        t1 = _dot(jnp.where(lane_a < RWKV_DECAY_LORA, jnp.tanh(wa), wa).astype(BF16), w2a2_ref[...])
        lane_g = lax.broadcasted_iota(jnp.int32, gb.shape, 1)
        t2 = _dot(jnp.where(lane_g < RWKV_GATE_LORA, _sigmoid(gb), gb).astype(BF16), g2v2_ref[...])
        yield
        xw = w0 + t1[:, :W]
        w_log = -(jnp.maximum(-xw, 0.0) + jnp.log1p(jnp.exp(-jnp.abs(xw)))) - 0.5
        lw = -jnp.exp(w_log)
        a = _sigmoid(a0 + t1[:, W:])
        g = t2[:, :W]
        if has_vres:
            v = v + (vf_all[r0:r1] - v) * _sigmoid(v0 + t2[:, W:])
        else:
            for s in range(n_seq):
                vout_ref[s] = v[s * seq_rows:(s + 1) * seq_rows]
        kk = k * k_k
        nrm = jnp.sqrt(_mm_exact_rhs(kk * kk, bd))
        kk = kk / jnp.maximum(nrm, 1e-12)
        k = k * (1.0 + (a - 1.0) * k_a)
        yield
        n_chunks = seq_rows // L
        idx = [(s, ch, p) for s in range(n_seq) for ch in range(n_chunks) for p in range(n_pairs)]
        cut = lambda x: [x[s * seq_rows + ch * L:s * seq_rows + (ch + 1) * L, p * LANES:(p + 1) * LANES]
                         for s, ch, p in idx]
        res = yield from _wkv_prep(cut(r), cut(lw), cut(k), cut(v), cut(kk), cut(a), tri, masks)
        prep = dict(zip(idx, res))
        yield
        sp = [(s, p) for s in range(n_seq) for p in range(n_pairs)]
        S = S_now[0]
        y_blk = {}
        for ch in range(n_chunks):
            y_s = {(s, p): _mm(prep[s, ch, p][0], S[s, p], NT, WKV_PASSES_STATE) + prep[s, ch, p][1]
                   for s, p in sp}
            for s in range(n_seq):
                y_blk[s, ch] = jnp.concatenate([y_s[s, p][:L] + y_s[s, p][L:] for p in range(n_pairs)], axis=1)
            S = {(s, p): S[s, p] * prep[s, ch, p][2] + _mm(S[s, p], prep[s, ch, p][3], NN, WKV_PASSES_STATE)
                 + prep[s, ch, p][4] for s, p in sp}
        S_now[0] = S
        y = jnp.concatenate([y_blk[s, ch] for s in range(n_seq) for ch in range(n_chunks)], axis=0)
        yield
        inv = 1.0 / HEAD_DIM
        mean = _mm_exact_rhs(y, bd) * inv
        yc = y - mean
        var = _mm_exact_rhs(yc * yc, bd) * inv
        yn = yc * lax.rsqrt(var + RWKV_LNX_EPS) * ln_g + ln_b
        bonus = _mm_exact_rhs(r * k * r_k, bd) * v
        out = ((yn + bonus) * g).astype(BF16)
        for s in range(n_seq):
            y_ref[s] = out[s * seq_rows:(s + 1) * seq_rows]

    for _ in group(0, rows):
        pass
    for s in range(n_seq):
        for p in range(n_pairs):
            state[s * n_pairs + p] = S_now[0][s, p]


def _rwkv(zr, v_first, mu, vec8, w2a2, g2v2, bd64):
    B, T, _ = zr.shape
    L = WKV_CHUNK * WKV_STEP_CHUNKS
    ns = WKV_STEP_SEQS
    has_vres = v_first is not None
    const = lambda shape: pl.BlockSpec(shape, lambda b, c: tuple(0 for _ in shape))
    tok = lambda w: pl.BlockSpec((ns, L, w), lambda b, c: (b, c, 0))
    in_specs = [tok(ZR_COLS)]
    args = [zr]
    if has_vres:
        in_specs.append(tok(RWKV_WIDTH))
        args.append(v_first)
    in_specs += [const((1, ZR_COLS)), const((8, RWKV_WIDTH)), const((LANES, 2 * RWKV_WIDTH)),
                 const((2 * LANES, 2 * RWKV_WIDTH)), const((RWKV_WIDTH, RWKV_WIDTH))]
    args += [mu, vec8, w2a2, g2v2, bd64]
    out_specs = [tok(RWKV_WIDTH)]
    out_shape = [jax.ShapeDtypeStruct((B, T, RWKV_WIDTH), BF16)]
    if not has_vres:
        out_specs.append(tok(RWKV_WIDTH))
        out_shape.append(jax.ShapeDtypeStruct((B, T, RWKV_WIDTH), F32))
    outs = pl.pallas_call(
        functools.partial(_rwkv_kernel, has_vres),
        grid=(B // ns, T // L),
        in_specs=in_specs, out_specs=out_specs, out_shape=out_shape,
        scratch_shapes=[pltpu.VMEM((ns, SUBLANES, ZR_COLS), F32),
                        pltpu.VMEM((ns * (RWKV_HEADS // 2), LANES, LANES), F32)],
        compiler_params=_cparams(("arbitrary", "arbitrary")),
    )(*args)
    return (outs[0], v_first) if has_vres else (outs[0], outs[1])


def _attn_step(q_ref, k_ref, v_ref, m_sc, l_sc, acc_sc, masked):
    tq = q_ref.shape[1]
    tk = k_ref.shape[1]
    lane = lax.broadcasted_iota(jnp.int32, (tq, LANES), 1)
    if masked:
        rowi = lax.broadcasted_iota(jnp.int32, (tq, tk), 0)
        coli = lax.broadcasted_iota(jnp.int32, (tq, tk), 1)
        keep = coli <= rowi
    heads = range(MLA_HEADS)
    s = [_dot(q_ref[0, :, hd * LANES:(hd + 1) * LANES], k_ref[0, :, hd * LANES:(hd + 1) * LANES], NT)
         for hd in heads]
    if masked:
        s = [jnp.where(keep, x, NEG_INF) for x in s]
    m_prev = [m_sc[hd] for hd in heads]
    m_new = [jnp.maximum(m_prev[hd], jnp.max(s[hd], axis=-1, keepdims=True)) for hd in heads]
    alpha = [jnp.exp(m_prev[hd] - m_new[hd]) for hd in heads]
    p = [jnp.exp(s[hd] - jnp.concatenate([m_new[hd]] * (tk // LANES), axis=1)) for hd in heads]
    for hd in heads:
        l_sc[hd] = alpha[hd] * l_sc[hd] + jnp.sum(p[hd], axis=-1, keepdims=True)
        m_sc[hd] = m_new[hd]
    pv = [_dot(p[hd].astype(BF16), v_ref[0, :, (hd // 2) * LANES:(hd // 2 + 1) * LANES]) for hd in heads]
    first = lane < HEAD_DIM
    for pr in range(MLA_HEADS // 2):
        acc_sc[pr] = (acc_sc[pr] * jnp.where(first, alpha[2 * pr], alpha[2 * pr + 1])
                      + jnp.where(first, pv[2 * pr], pv[2 * pr + 1]))


def _attn_step_bounded(q_ref, k_ref, v_ref, cb_ref, l_sc, acc_sc, masked):
    tq = q_ref.shape[1]
    tk = k_ref.shape[1]
    lane = lax.broadcasted_iota(jnp.int32, (tq, LANES), 1)
    if masked:
        rowi = lax.broadcasted_iota(jnp.int32, (tq, tk), 0)
        coli = lax.broadcasted_iota(jnp.int32, (tq, tk), 1)
        keep = coli <= rowi
    pv = []
    for hd in range(MLA_HEADS):
        s = _dot(q_ref[0, :, hd * LANES:(hd + 1) * LANES], k_ref[0, :, hd * LANES:(hd + 1) * LANES], NT)
        c = cb_ref[0, hd:hd + 1, :]
        p = jnp.exp(s - jnp.concatenate([c] * (tk // LANES), axis=1))
        if masked:
            p = jnp.where(keep, p, 0.0)
        part = p[:, 0:LANES]
        for t in range(1, tk // LANES):
            part = part + p[:, t * LANES:(t + 1) * LANES]
        l_sc[hd] = l_sc[hd] + part
        pv.append(_dot(p.astype(BF16), v_ref[0, :, (hd // 2) * LANES:(hd // 2 + 1) * LANES]))
    first = lane < HEAD_DIM
    for pr in range(MLA_HEADS // 2):
        acc_sc[pr] = acc_sc[pr] + jnp.where(first, pv[2 * pr], pv[2 * pr + 1])


def _attn_finish(o_ref, l_sc, acc_sc, lane_partial):
    tq = o_ref.shape[1]
    lane = lax.broadcasted_iota(jnp.int32, (tq, LANES), 1)
    outs = []
    for pr in range(MLA_HEADS // 2):
        la, lb = l_sc[2 * pr], l_sc[2 * pr + 1]
        if lane_partial:
            la = jnp.sum(la, axis=-1, keepdims=True)
            lb = jnp.sum(lb, axis=-1, keepdims=True)
        outs.append(acc_sc[pr] / jnp.where(lane < HEAD_DIM, la, lb))
    o_ref[0] = jnp.concatenate(outs, axis=1).astype(BF16)


def _attn_kernel(ok_ref, qi_ref, kj_ref, q_ref, k_ref, v_ref, cb_ref, o_ref, m_sc, l_sc, acc_sc):
    i = qi_ref[pl.program_id(1)]
    j = kj_ref[pl.program_id(1)]
    bounded = ok_ref[pl.program_id(0)] == 1
    exact = jnp.logical_not(bounded)

    @pl.when(j == 0)
    def _():
        m_sc[...] = jnp.full(m_sc.shape, NEG_INF, F32)
        l_sc[...] = jnp.zeros(l_sc.shape, F32)
        acc_sc[...] = jnp.zeros(acc_sc.shape, F32)

    @pl.when(bounded & (j < i))
    def _():
        _attn_step_bounded(q_ref, k_ref, v_ref, cb_ref, l_sc, acc_sc, masked=False)

    @pl.when(bounded & (j == i))
    def _():
        _attn_step_bounded(q_ref, k_ref, v_ref, cb_ref, l_sc, acc_sc, masked=True)
        _attn_finish(o_ref, l_sc, acc_sc, lane_partial=True)

    @pl.when(exact & (j < i))
    def _():
        _attn_step(q_ref, k_ref, v_ref, m_sc, l_sc, acc_sc, masked=False)

    @pl.when(exact & (j == i))
    def _():
        _attn_step(q_ref, k_ref, v_ref, m_sc, l_sc, acc_sc, masked=True)
        _attn_finish(o_ref, l_sc, acc_sc, lane_partial=False)


def _attention(q, k, v, qk_gain):
    B, T, _ = q.shape
    nq = T // TQ
    gmax = jnp.max(jnp.abs(qk_gain), axis=1)
    c = gmax[0] * gmax[1] * (MLA_QK_DIM ** 0.5) * ATTN_BOUND_SLACK
    ok = jnp.broadcast_to((c <= ATTN_BOUND_MAX).astype(jnp.int32), (B,))
    cb = jnp.broadcast_to(c, (B, MLA_HEADS, LANES))
    pairs = [(i, j) for i in range(nq) for j in range(i + 1)]
    qi = jnp.asarray([p[0] for p in pairs], jnp.int32)
    kj = jnp.asarray([p[1] for p in pairs], jnp.int32)
    grid_spec = pltpu.PrefetchScalarGridSpec(
        num_scalar_prefetch=3,
        grid=(B, len(pairs)),
        in_specs=[pl.BlockSpec((1, TQ, MLA_HEADS * LANES), lambda b, t, ok, qi, kj: (b, qi[t], 0)),
                  pl.BlockSpec((1, TQ, MLA_HEADS * LANES), lambda b, t, ok, qi, kj: (b, kj[t], 0)),
                  pl.BlockSpec((1, TQ, MLA_HEADS * HEAD_DIM), lambda b, t, ok, qi, kj: (b, kj[t], 0)),
                  pl.BlockSpec((1, MLA_HEADS, LANES), lambda b, t, ok, qi, kj: (b, 0, 0))],
        out_specs=pl.BlockSpec((1, TQ, MLA_HEADS * HEAD_DIM), lambda b, t, ok, qi, kj: (b, qi[t], 0)),
        scratch_shapes=[pltpu.VMEM((MLA_HEADS, TQ, LANES), F32),
                        pltpu.VMEM((MLA_HEADS, TQ, LANES), F32),
                        pltpu.VMEM((MLA_HEADS // 2, TQ, LANES), F32)])
    return pl.pallas_call(
        _attn_kernel,
        grid_spec=grid_spec,
        out_shape=jax.ShapeDtypeStruct((B, T, MLA_HEADS * HEAD_DIM), BF16),
        compiler_params=_cparams(("arbitrary", "arbitrary")),
    )(ok, qi, kj, q, k, v, cb)


def _mixout_kernel(has_router, *refs):
    if has_router:
        (x_ref, yr_ref, yp_ref, ym_ref, wo_ref, mod_ref, gain_ref, rt_ref, tri_ref,
         xo_ref, h_ref, route_ref, cnt_ref, cnt_sc) = refs
    else:
        x_ref, yr_ref, yp_ref, ym_ref, wo_ref, mod_ref, gain_ref, xo_ref, h_ref = refs
    mod = mod_ref[0]
    o1 = RWKV_WIDTH
    o2 = RWKV_WIDTH + POOL_WIDTH
    mix = (_dot(yr_ref[0], wo_ref[0:o1, :]) + _dot(yp_ref[0], wo_ref[o1:o2, :])
           + _dot(ym_ref[0], wo_ref[o2:, :]))
    x = x_ref[0] + mod[2:3] * mix
    xo_ref[0] = x
    h = _rms(x) * gain_ref[...] * (1.0 + mod[4:5]) + mod[3:4]
    if not has_router:
        h_ref[0] = h.astype(BF16)
    else:
        hp = _pack_bf16_pairs(h)
        for ck in range(ROW_CHUNKS):
            h_ref[ck, 0] = hp[:, ck * PACK_CHUNK_W:(ck + 1) * PACK_CHUNK_W]
        logits = _dot(h.astype(BF16), rt_ref[...])
        lane = lax.broadcasted_iota(jnp.int32, logits.shape, 1).astype(F32)
        lg = jnp.where(lane < N_EXPERTS, logits, -jnp.inf)
        m1 = jnp.max(lg, axis=-1, keepdims=True)
        i1 = jnp.min(jnp.where(lg == m1, lane, float(LANES)), axis=-1, keepdims=True)
        lg2 = jnp.where(lane == i1, -jnp.inf, lg)
        m2 = jnp.max(lg2, axis=-1, keepdims=True)
        i2 = jnp.min(jnp.where(lg2 == m2, lane, float(LANES)), axis=-1, keepdims=True)
        e2 = jnp.exp(m2 - m1)
        g1 = 1.0 / (1.0 + e2)
        g2 = e2 / (1.0 + e2)
        first = (pl.program_id(0) == 0) & (pl.program_id(1) == 0)

        @pl.when(first)
        def _():
            cnt_sc[...] = jnp.zeros(cnt_sc.shape, F32)

        hit1 = lane == i1
        hit2 = lane == i2
        onehot = jnp.where(hit1 | hit2, 1.0, 0.0)
        prefix = _dot(tri_ref[...], onehot.astype(BF16)) + cnt_sc[0:1, :]
        r1 = jnp.sum(jnp.where(hit1, prefix, 0.0), axis=-1, keepdims=True)
        r2 = jnp.sum(jnp.where(hit2, prefix, 0.0), axis=-1, keepdims=True)
        cnt_sc[...] = cnt_sc[...] + jnp.sum(onehot, axis=0, keepdims=True)
        cnt_ref[...] = cnt_sc[...]
        vals = (i1, i2, g1, g2, r1, r2)
        route = jnp.zeros(logits.shape, F32)
        for pos, val in enumerate(vals):
            route = jnp.where(lane == pos, val, route)
        route_ref[0] = route


def _mixout(x, yr, yp, ym, wo, mod_l, gain, router_p):
    B, T, _ = x.shape
    tm = TM_MIX
    has_router = router_p is not None
    const = lambda shape: pl.BlockSpec(shape, lambda b, i: tuple(0 for _ in shape))
    tok = lambda w: pl.BlockSpec((1, tm, w), lambda b, i: (b, i, 0))
    in_specs = [tok(D_MODEL), tok(RWKV_WIDTH), tok(POOL_WIDTH), tok(MLA_HEADS * HEAD_DIM),
                const((D_MODEL, D_MODEL)), pl.BlockSpec((1, 6, D_MODEL), lambda b, i: (b, 0, 0)),
                const((1, D_MODEL))]
    args = [x, yr, yp, ym, wo, mod_l, gain]
    out_specs = [tok(D_MODEL), tok(D_MODEL)]
    out_shape = [jax.ShapeDtypeStruct((B, T, D_MODEL), F32), jax.ShapeDtypeStruct((B, T, D_MODEL), BF16)]
    scratch = []
    if has_router:
        out_specs[1] = pl.BlockSpec((ROW_CHUNKS, 1, tm, PACK_CHUNK_W), lambda b, i: (0, b, i, 0))
        out_shape[1] = jax.ShapeDtypeStruct((ROW_CHUNKS, B, T, PACK_CHUNK_W), jnp.uint32)
        ids = np.arange(tm)
        tri = jnp.asarray(ids[:, None] > ids[None, :], BF16)
        in_specs += [const((D_MODEL, LANES)), const((tm, tm))]
        args += [router_p, tri]
        out_specs += [tok(LANES), const((SUBLANES, LANES))]
        out_shape += [jax.ShapeDtypeStruct((B, T, LANES), F32),
                      jax.ShapeDtypeStruct((SUBLANES, LANES), F32)]
        scratch = [pltpu.VMEM((SUBLANES, LANES), F32)]
    return pl.pallas_call(
        functools.partial(_mixout_kernel, has_router),
        grid=(B, T // tm),
        in_specs=in_specs, out_specs=out_specs, out_shape=out_shape, scratch_shapes=scratch,
        compiler_params=_cparams(("arbitrary", "arbitrary")),
    )(*args)


def _ffn_kernel(h_ref, wg_ref, wu_ref, wo_ref, x_ref, mod_ref, o_ref, acc):
    j = pl.program_id(1)

    @pl.when(j == 0)
    def _():
        acc[...] = jnp.zeros(acc.shape, F32)

    h = h_ref[...]
    gg = _dot(h, wg_ref[...])
    uu = _dot(h, wu_ref[...])
    acc[...] += _dot((_silu(gg) * uu).astype(BF16), wo_ref[...])

    @pl.when(j == pl.num_programs(1) - 1)
    def _():
        o_ref[...] = x_ref[...] + mod_ref[0][5:6] * acc[...]


def _ffn(h2, w_in, w_out, x, mod_l):
    N = h2.shape[0]
    T = N // mod_l.shape[0]
    tm, tf = TM_FFN, TF_FFN
    nf = D_FF // tf
    per_b = T // tm
    return pl.pallas_call(
        _ffn_kernel,
        grid=(N // tm, nf),
        in_specs=[pl.BlockSpec((tm, D_MODEL), lambda i, j: (i, 0)),
                  pl.BlockSpec((D_MODEL, tf), lambda i, j: (0, j)),
                  pl.BlockSpec((D_MODEL, tf), lambda i, j: (0, j + nf)),
                  pl.BlockSpec((tf, D_MODEL), lambda i, j: (j, 0)),
                  pl.BlockSpec((tm, D_MODEL), lambda i, j: (i, 0)),
                  pl.BlockSpec((1, 6, D_MODEL), lambda i, j: (i // per_b, 0, 0))],
        out_specs=pl.BlockSpec((tm, D_MODEL), lambda i, j: (i, 0)),
        out_shape=jax.ShapeDtypeStruct((N, D_MODEL), F32),
        scratch_shapes=[pltpu.VMEM((tm, D_MODEL), F32)],
        compiler_params=_cparams(("arbitrary", "arbitrary")),
    )(h2, w_in, w_in, w_out, x, mod_l)


def _moe_kernel(be_ref, nv_ref, last_ref, x_ref, wg_ref, wu_ref, wo_ref, o_ref, acc, xm):
    i = pl.program_id(0)
    j = pl.program_id(1)

    @pl.when(i <= last_ref[0])
    def _():
        @pl.when(j == 0)
        def _():
            acc[...] = jnp.zeros(acc.shape, F32)
            row = lax.broadcasted_iota(jnp.int32, (xm.shape[0], 1), 0)
            xp = jnp.concatenate([x_ref[ck] for ck in range(ROW_CHUNKS)], axis=1)
            xp = jnp.where(row < nv_ref[i], xp, jnp.uint32(0))
            xm[...] = _unpack_bf16_pairs(xp)

        x = xm[...]
        gg = _dot(x, wg_ref[0].astype(BF16))
        uu = _dot(x, wu_ref[0].astype(BF16))
        acc[...] += _dot((_silu(gg) * uu).astype(BF16), wo_ref[0].astype(BF16))

        @pl.when(j == pl.num_programs(1) - 1)
        def _():
            yp = _pack_bf16_pairs(acc[...])
            for ck in range(ROW_CHUNKS):
                o_ref[ck] = yp[:, ck * PACK_CHUNK_W:(ck + 1) * PACK_CHUNK_W]


def _moe_experts(xs, w_in, w_out, block_exp, n_valid, last_blk):
    n_rows = xs.shape[1]
    tm, tf = MOE_BLOCK, TF_MOE
    nf = D_FF_EXPERT // tf
    blk = lambda i, last: jnp.minimum(i, last[0])
    chunk = lambda i, j, last: jnp.where(i <= last[0], j, nf - 1)
    grid_spec = pltpu.PrefetchScalarGridSpec(
        num_scalar_prefetch=3,
        grid=(n_rows // tm, nf),
        in_specs=[pl.BlockSpec((ROW_CHUNKS, tm, PACK_CHUNK_W),
                               lambda i, j, be, nv, last: (0, blk(i, last), 0)),
                  pl.BlockSpec((1, D_MODEL, tf),
                               lambda i, j, be, nv, last: (be[blk(i, last)], 0, chunk(i, j, last))),
                  pl.BlockSpec((1, D_MODEL, tf),
                               lambda i, j, be, nv, last: (be[blk(i, last)], 0, chunk(i, j, last) + nf)),
                  pl.BlockSpec((1, tf, D_MODEL),
                               lambda i, j, be, nv, last: (be[blk(i, last)], chunk(i, j, last), 0))],
        out_specs=pl.BlockSpec((ROW_CHUNKS, tm, PACK_CHUNK_W), lambda i, j, be, nv, last: (0, blk(i, last), 0)),
        scratch_shapes=[pltpu.VMEM((tm, D_MODEL), F32), pltpu.VMEM((tm, D_MODEL), BF16)])
    return pl.pallas_call(
        _moe_kernel,
        grid_spec=grid_spec,
        out_shape=jax.ShapeDtypeStruct((ROW_CHUNKS, n_rows, PACK_CHUNK_W), jnp.uint32),
        compiler_params=_cparams(("arbitrary", "arbitrary")),
    )(block_exp, n_valid, last_blk, xs, w_in, w_in, w_out)


def _sc_mesh():
    return plsc.VectorSubcoreMesh(core_axis_name="c", subcore_axis_name="s")


def _sc_scatter_rows(x, dest, n_rows):
    N, D = x.shape
    K = dest.shape[0]
    win = SC_WINDOW

    @pl.kernel(out_type=jax.ShapeDtypeStruct((n_rows, D), x.dtype), mesh=_sc_mesh(), scratch_types=[])
    def scatter(x_hbm, d_hbm, o_hbm):
        def body(x_vmem, *idx_vmem):
            for iv in idx_vmem:
                pltpu.sync_copy(x_vmem, o_hbm.at[iv.at[0]])

        pltpu.emit_pipeline(
            body,
            grid=(N // win,),
            in_specs=[pl.BlockSpec((win, D), lambda i: (i, 0))]
            + [pl.BlockSpec((1, win), functools.partial(lambda k, i: (k, i), k)) for k in range(K)],
            out_specs=[],
            core_axis_name=("c", "s"),
            dimension_semantics=(pltpu.PARALLEL,),
        )(x_hbm, *([d_hbm] * K))

    return scatter(x, dest)


def _sc_gather_rows(x, idx):
    n = idx.shape[0]
    D = x.shape[1]
    win = SC_WINDOW

    @pl.kernel(out_type=jax.ShapeDtypeStruct((n, D), x.dtype), mesh=_sc_mesh(), scratch_types=[])
    def gather(x_hbm, i_hbm, o_hbm):
        def body(i_vmem, o_vmem):
            pltpu.sync_copy(x_hbm.at[i_vmem.at[0]], o_vmem)

        pltpu.emit_pipeline(
            body,
            grid=(n // win,),
            in_specs=[pl.BlockSpec((1, win), lambda i: (0, i))],
            out_specs=[pl.BlockSpec((win, D), lambda i: (i, 0))],
            core_axis_name=("c", "s"),
            dimension_semantics=(pltpu.PARALLEL,),
        )(i_hbm, o_hbm)

    return gather(x, idx.reshape(1, n))


def _combine_kernel(x_ref, ya_ref, yb_ref, route_ref, mod_ref, o_ref):
    rt = route_ref[...]
    ya = _unpack_bf16_pairs(jnp.concatenate([ya_ref[0, ck] for ck in range(ROW_CHUNKS)], axis=1))
    yb = _unpack_bf16_pairs(jnp.concatenate([yb_ref[0, ck] for ck in range(ROW_CHUNKS)], axis=1))
    f = rt[:, 2:3] * ya.astype(F32) + rt[:, 3:4] * yb.astype(F32)
    o_ref[...] = x_ref[...] + mod_ref[0][5:6] * f


def _combine(x, y2, route, mod_l):
    N = x.shape[0]
    T = N // mod_l.shape[0]
    tm = 1024
    per_b = T // tm
    tok = pl.BlockSpec((tm, D_MODEL), lambda i: (i, 0))
    slot = lambda k: pl.BlockSpec((1, ROW_CHUNKS, tm, PACK_CHUNK_W), lambda i: (k, 0, i, 0))
    return pl.pallas_call(
        _combine_kernel,
        grid=(N // tm,),
        in_specs=[tok, slot(0), slot(1),
                  pl.BlockSpec((tm, LANES), lambda i: (i, 0)),
                  pl.BlockSpec((1, 6, D_MODEL), lambda i: (i // per_b, 0, 0))],
        out_specs=tok,
        out_shape=jax.ShapeDtypeStruct((N, D_MODEL), F32),
        compiler_params=_cparams(("arbitrary",)),
    )(x, y2, y2, route, mod_l)


def _moe(h2, route, counts, w_in, w_out, x, mod_l):
    N = x.shape[0]
    blk = MOE_BLOCK
    cnt = counts[0, :N_EXPERTS].astype(jnp.int32)
    padded = (cnt + blk - 1) // blk * blk
    pend = jnp.cumsum(padded)
    pstart = pend - padded
    e = route[:, 0:TOP_K].astype(jnp.int32)
    rank = route[:, 2 * TOP_K:3 * TOP_K].astype(jnp.int32)
    dest = (jnp.take(pstart, e) + rank).T
    n_blocks = N * TOP_K // blk + N_EXPERTS
    bstart = jnp.arange(n_blocks, dtype=jnp.int32) * blk
    block_exp = jnp.minimum(jnp.sum((bstart[:, None] >= pend[None, :]).astype(jnp.int32), axis=1),
                            N_EXPERTS - 1)
    n_valid = jnp.clip(cnt[block_exp] - (bstart - pstart[block_exp]), 0, blk).astype(jnp.int32)
    n_rows = n_blocks * blk
    dest_ck = dest[:, None, :] + (jnp.arange(ROW_CHUNKS, dtype=jnp.int32) * n_rows)[None, :, None]
    xs = _sc_scatter_rows(h2, dest_ck.reshape(TOP_K, ROW_CHUNKS * N), ROW_CHUNKS * n_rows)
    last_blk = (pend[-1:] // blk - 1).astype(jnp.int32)
    yb = _moe_experts(xs.reshape(ROW_CHUNKS, n_rows, PACK_CHUNK_W), w_in, w_out, block_exp, n_valid,
                      last_blk)
    y2 = _sc_gather_rows(yb.reshape(ROW_CHUNKS * n_rows, PACK_CHUNK_W), dest_ck.reshape(-1))
    return _combine(x, y2.reshape(TOP_K, ROW_CHUNKS, N, PACK_CHUNK_W), route, mod_l)


def _layout_w_in(w, has_vres):
    W = RWKV_WIDTH
    off_gd = 3 * W + RWKV_DECAY_LORA + RWKV_ICLR_LORA
    off_pool = off_gd + RWKV_GATE_LORA
    off_q = off_pool + POOL_WIDTH
    off_kv = off_q + MLA_Q_LORA
    off_kr = off_kv + MLA_KV_LORA
    n_base = off_kr + MLA_QK_ROPE
    d = w.shape[0]
    zeros = lambda n: jnp.zeros((d, n), w.dtype)
    vd = w[:, n_base:n_base + RWKV_VRES_LORA] if has_vres else zeros(RWKV_VRES_LORA)
    cols = [w[:, :off_gd], w[:, off_gd:off_pool], vd, zeros(ZR_COLS - off_pool - RWKV_VRES_LORA),
            w[:, off_pool:off_q], w[:, off_q:off_kv], w[:, off_kv:off_kr],
            zeros(MLA_QK_NOPE), w[:, off_kr:n_base], zeros(LANES - MLA_QK_DIM)]
    return jnp.concatenate(cols, axis=1).astype(BF16)


def _pad_heads(w, per_head, keep_from, keep_n):
    K = w.shape[0]
    wh = w.reshape(K, MLA_HEADS, per_head)[:, :, keep_from:keep_from + keep_n]
    wh = jnp.pad(wh, ((0, 0), (0, 0), (0, LANES - keep_n)))
    return wh.reshape(K, MLA_HEADS * LANES)


def kernel(x, c, positions, w_ada, b_ada, norm_gain, w_in_first, w_in_rest, mu_shift, mu_shift_v,
           rwkv_vec, rwkv_v0, rwkv_w2, rwkv_a2, rwkv_g2, rwkv_v2, pool_w, pool_scale,
           mla_q_lat_gain, mla_kv_lat_gain, mla_wq_up, mla_wkv_up, mla_qk_gain, w_out, ffn_w_in,
           ffn_w_out, moe_router, moe_w_in, moe_w_out):
    B, T, D = x.shape
    depth = w_ada.shape[0]
    W = RWKV_WIDTH
    mod = _adaln(c, w_ada, b_ada).reshape(depth, B, 6, D)
    pos3 = positions.reshape(B, T, 1)
    inv_freq = ROPE_BASE ** (-jnp.arange(0, MLA_QK_ROPE, 2, dtype=F32) / MLA_QK_ROPE)
    freq = jnp.concatenate([jnp.zeros((MLA_QK_NOPE,), F32), inv_freq, inv_freq,
                            jnp.zeros((LANES - MLA_QK_DIM,), F32)]).reshape(1, LANES)
    cosf, sinf = _rope_tables(pos3, freq)
    hid = np.arange(W) // HEAD_DIM
    bd64 = jnp.asarray(hid[:, None] == hid[None, :], BF16)
    bid = np.arange(MLA_HEADS * LANES) // LANES
    bd128 = jnp.asarray(bid[:, None] == bid[None, :], BF16)

    v_first = None
    for l in range(depth):
        has_vres = l > 0
        mod_l = mod[l]
        win = _layout_w_in(w_in_first if l == 0 else w_in_rest[l - 1], has_vres)
        poolw = jax.scipy.linalg.block_diag(*[pool_w[l, g] for g in range(len(POOL_WINDOWS))]).astype(BF16)
        wq = _pad_heads(mla_wq_up[l], MLA_QK_DIM, 0, MLA_QK_DIM).astype(BF16)
        wk = _pad_heads(mla_wkv_up[l], MLA_QK_NOPE + HEAD_DIM, 0, MLA_QK_NOPE).astype(BF16)
        wv = mla_wkv_up[l].reshape(MLA_KV_LORA, MLA_HEADS, MLA_QK_NOPE + HEAD_DIM)[:, :, MLA_QK_NOPE:]
        wv = wv.reshape(MLA_KV_LORA, MLA_HEADS * HEAD_DIM).astype(BF16)
        qkg = jnp.tile(jnp.pad(mla_qk_gain[l], ((0, 0), (0, LANES - MLA_QK_DIM))), (1, MLA_HEADS))
        zr, y_pool, q, k, v = _mixin(
            x, mod_l, norm_gain[l, 0].reshape(1, D), win, cosf, sinf, poolw,
            pool_scale[l].reshape(1, -1), mla_q_lat_gain[l].reshape(1, -1),
            mla_kv_lat_gain[l].reshape(1, -1), wq, wk, wv, qkg, bd128)

        pad_mu = ZR_COLS - mu_shift.shape[1] - RWKV_VRES_LORA
        mu_v = mu_shift_v[l - 1] if has_vres else jnp.zeros((RWKV_VRES_LORA,), F32)
        mu = jnp.concatenate([mu_shift[l], mu_v, jnp.zeros((pad_mu,), F32)]).reshape(1, ZR_COLS)
        v0 = rwkv_v0[l - 1] if has_vres else jnp.zeros((W,), F32)
        vec8 = jnp.concatenate([rwkv_vec[l], v0[None]], axis=0)
        w2a2 = jax.scipy.linalg.block_diag(rwkv_w2[l], rwkv_a2[l]).astype(BF16)
        g2 = jnp.pad(rwkv_g2[l], ((0, 2 * LANES - RWKV_GATE_LORA), (0, 0)))
        if has_vres:
            v2 = jnp.pad(rwkv_v2[l - 1], ((RWKV_GATE_LORA, 2 * LANES - RWKV_GATE_LORA - RWKV_VRES_LORA), (0, 0)))
        else:
            v2 = jnp.zeros((2 * LANES, W), F32)
        g2v2 = jnp.concatenate([g2, v2], axis=1).astype(BF16)
        y_rwkv, v_first = _rwkv(zr, v_first, mu, vec8, w2a2, g2v2, bd64)

        y_mla = _attention(q, k, v, mla_qk_gain[l])

        is_moe = (l % 2 == 1)
        router_p = None
        if is_moe:
            router_p = jnp.pad(moe_router[l // 2], ((0, 0), (0, LANES - N_EXPERTS))).astype(BF16)
        outs = _mixout(x, y_rwkv, y_pool, y_mla, w_out[l].astype(BF16), mod_l,
                       norm_gain[l, 1].reshape(1, D), router_p)
        x_mid, h2 = outs[0], outs[1]
        xf = x_mid.reshape(B * T, D)
        if is_moe:
            xo = _moe(h2.reshape(ROW_CHUNKS * B * T, PACK_CHUNK_W), outs[2].reshape(B * T, LANES), outs[3],
                      moe_w_in[l // 2], moe_w_out[l // 2], xf, mod_l)
        else:
            xo = _ffn(h2.reshape(B * T, D), ffn_w_in[l // 2].astype(BF16), ffn_w_out[l // 2].astype(BF16),
                      xf, mod_l)
        x = xo.reshape(B, T, D)
    return x
```

```python
import functools

import numpy as np
import jax
import jax.numpy as jnp
from jax import lax
from jax.experimental import pallas as pl
from jax.experimental.pallas import tpu as pltpu
from jax.experimental.pallas import tpu_sc as plsc

F32 = jnp.float32
BF16 = jnp.bfloat16

D_MODEL = 1024
HEAD_DIM = 64
RWKV_WIDTH = 512
RWKV_HEADS = 8
POOL_WIDTH = 256
POOL_WINDOWS = (2, 4, 8, 16)
POOL_HALO = 16
MLA_HEADS = 4
MLA_QK_NOPE = 64
MLA_QK_ROPE = 32
MLA_QK_DIM = 96
MLA_Q_LORA = 256
MLA_KV_LORA = 128
ROPE_BASE = 10000.0
RWKV_DECAY_LORA = 64
RWKV_ICLR_LORA = 64
RWKV_VRES_LORA = 32
RWKV_GATE_LORA = 160
RWKV_LNX_EPS = 64e-5
D_FF = 2816
N_EXPERTS = 8
TOP_K = 2
D_FF_EXPERT = 3584
NORM_EPS = 1e-6
NEG_INF = -1e30

LANES = 128
SUBLANES = 8
VMEM_LIMIT = 56 * 1024 * 1024

ZR_COLS = 1920
Z_POOL_OFF = ZR_COLS
Z_QLAT_OFF = Z_POOL_OFF + POOL_WIDTH
Z_KVLAT_OFF = Z_QLAT_OFF + MLA_Q_LORA
Z_KROPE_OFF = Z_KVLAT_OFF + MLA_KV_LORA
Z_COLS = Z_KROPE_OFF + LANES

MIXIN_PIECE_W = 512
TM_MIX = 512
WKV_CHUNK = 64
TQ = 512
ATTN_BOUND_SLACK = 1.02
ATTN_BOUND_MAX = 40.0
TM_FFN = 512
TF_FFN = 1408
MOE_BLOCK = 1024
TF_MOE = 512
SC_WINDOW = 128
ROW_CHUNKS = 4
ROW_CHUNK_W = D_MODEL // ROW_CHUNKS
PACK_CHUNK_W = ROW_CHUNK_W // 2

SEGSUM_SPLITS = 1

NN = (((1,), (0,)), ((), ()))
NT = (((1,), (1,)), ((), ()))


def _dot(a, b, dims=NN):
    return lax.dot_general(a, b, dims, preferred_element_type=F32)


def _split2(a):
    hi = a.astype(BF16)
    lo = (a - hi.astype(F32)).astype(BF16)
    return hi, lo


def _mm(a, b, dims=NN, passes=3):
    if passes == 1:
        return _dot(a.astype(BF16), b.astype(BF16), dims)
    ah, al = _split2(a)
    bh, bl = _split2(b)
    return _dot(ah, bh, dims) + (_dot(ah, bl, dims) + _dot(al, bh, dims))


def _mm_exact_rhs(a, b_bf16, dims=NN, splits=SEGSUM_SPLITS):
    out = None
    rem = a
    for s in range(splits):
        part = rem.astype(BF16)
        term = _dot(part, b_bf16, dims)
        out = term if out is None else out + term
        if s + 1 < splits:
            rem = rem - part.astype(F32)
    return out


def _pack_bf16_pairs(h):
    w = h.shape[1] // 2
    lo = lax.bitcast_convert_type(h[:, :w].astype(BF16).astype(F32), jnp.uint32)
    hi = lax.bitcast_convert_type(h[:, w:].astype(BF16).astype(F32), jnp.uint32)
    return (lo >> 16) | (hi & jnp.uint32(0xFFFF0000))


def _unpack_bf16_pairs(p):
    lo = lax.bitcast_convert_type(p << 16, F32)
    hi = lax.bitcast_convert_type(p & jnp.uint32(0xFFFF0000), F32)
    return jnp.concatenate([lo, hi], axis=1).astype(BF16)


def _sigmoid(x):
    return 1.0 / (1.0 + jnp.exp(-x))


def _silu(x):
    return x * _sigmoid(x)


def _rms(x, eps=NORM_EPS):
    return x * lax.rsqrt(jnp.mean(x * x, axis=-1, keepdims=True) + eps)


def _cparams(sem):
    return pltpu.CompilerParams(dimension_semantics=sem, vmem_limit_bytes=VMEM_LIMIT)


def _adaln_kernel(c_ref, w_ref, b_ref, o_ref):
    ca = _silu(c_ref[...])
    o_ref[0] = _mm(ca, w_ref[0]) + b_ref[0]


def _adaln(c, w_ada, b_ada):
    L = w_ada.shape[0]
    B = c.shape[0]
    n = w_ada.shape[2] // D_MODEL
    return pl.pallas_call(
        _adaln_kernel,
        grid=(L, n),
        in_specs=[pl.BlockSpec((B, D_MODEL), lambda l, j: (0, 0)),
                  pl.BlockSpec((1, D_MODEL, D_MODEL), lambda l, j: (l, 0, j)),
                  pl.BlockSpec((1, 1, D_MODEL), lambda l, j: (l, 0, j))],
        out_specs=pl.BlockSpec((1, B, D_MODEL), lambda l, j: (l, 0, j)),
        out_shape=jax.ShapeDtypeStruct((L, B, n * D_MODEL), F32),
        compiler_params=_cparams(("arbitrary", "arbitrary")),
    )(c, w_ada, b_ada.reshape(L, 1, -1))


def _rope(x, cosf, sinf, lane):
    up = pltpu.roll(x, LANES - MLA_QK_ROPE // 2, axis=1)
    dn = pltpu.roll(x, MLA_QK_ROPE // 2, axis=1)
    rot = jnp.where(lane < MLA_QK_NOPE + MLA_QK_ROPE // 2, -up, dn)
    return x * cosf + rot * sinf


def _rope_kernel(pos_ref, freq_ref, cos_ref, sin_ref):
    tm = pos_ref.shape[1]
    lane = lax.broadcasted_iota(jnp.int32, (tm, LANES), 1)
    in_rope = (lane >= MLA_QK_NOPE) & (lane < MLA_QK_DIM)
    ang = pos_ref[0].astype(F32) * freq_ref[...]
    cos_ref[0] = jnp.where(in_rope, jnp.cos(ang), 1.0)
    sin_ref[0] = jnp.where(in_rope, jnp.sin(ang), 0.0)


def _rope_tables(pos3, freq):
    B, T, _ = pos3.shape
    tm = TM_MIX
    tok = lambda w: pl.BlockSpec((1, tm, w), lambda b, i: (b, i, 0))
    return pl.pallas_call(
        _rope_kernel,
        grid=(B, T // tm),
        in_specs=[tok(1), pl.BlockSpec((1, LANES), lambda b, i: (0, 0))],
        out_specs=[tok(LANES), tok(LANES)],
        out_shape=[jax.ShapeDtypeStruct((B, T, LANES), F32)] * 2,
        compiler_params=_cparams(("arbitrary", "arbitrary")),
    )(pos3, freq)


def _mixin_kernel(x_ref, mod_ref, gain_ref, win_ref, cos_ref, sin_ref, poolw_ref, pools_ref,
                  qg_ref, kvg_ref, wq_ref, wk_ref, wv_ref, qkg_ref, bd_ref,
                  zr_ref, yp_ref, q_ref, k_ref, v_ref, ubuf):
    i = pl.program_id(1)
    tm = x_ref.shape[1]
    x = x_ref[0]
    mod = mod_ref[0]
    @pl.when(i == 0)
    def _():
        ubuf[0:POOL_HALO, :] = jnp.zeros((POOL_HALO, POOL_WIDTH), F32)

    h = (_rms(x) * gain_ref[...] * (1.0 + mod[1:2]) + mod[0:1]).astype(BF16)
    zb = _dot(h, win_ref[:, ZR_COLS:])
    zcol = lambda off, w: zb[:, off - ZR_COLS:off - ZR_COLS + w]
    def project_piece(n):
        cols = slice(n * MIXIN_PIECE_W, min((n + 1) * MIXIN_PIECE_W, ZR_COLS))
        zr_ref[0, :, cols] = _dot(h, win_ref[:, cols])

    project_piece(0)
    u = zcol(Z_POOL_OFF, POOL_WIDTH)
    ubuf[POOL_HALO:, :] = u
    ue = ubuf[...]
    s2 = ue + pltpu.roll(ue, 1, axis=0)
    s4 = s2 + pltpu.roll(s2, 2, axis=0)
    s8 = s4 + pltpu.roll(s4, 4, axis=0)
    s16 = s8 + pltpu.roll(s8, 8, axis=0)
    ubuf[0:POOL_HALO, :] = u[tm - POOL_HALO:, :]
    lane_p = lax.broadcasted_iota(jnp.int32, (tm, POOL_WIDTH), 1)
    grp = lane_p // (POOL_WIDTH // len(POOL_WINDOWS))
    win_sum = jnp.where(grp == 0, s2[POOL_HALO:], jnp.where(grp == 1, s4[POOL_HALO:],
                        jnp.where(grp == 2, s8[POOL_HALO:], s16[POOL_HALO:])))
    win = jnp.where(grp == 0, 2, jnp.where(grp == 1, 4, jnp.where(grp == 2, 8, 16)))
    t_abs = i * tm + lax.broadcasted_iota(jnp.int32, (tm, POOL_WIDTH), 0)
    cnt = jnp.minimum(t_abs + 1, win).astype(F32)
    p = win_sum / cnt - u
    yp = _dot(p.astype(BF16), poolw_ref[...]) * pools_ref[...]
    yp_ref[0] = yp.astype(BF16)
    project_piece(1)

    lane = lax.broadcasted_iota(jnp.int32, (tm, LANES), 1)
    cosf = cos_ref[0]
    sinf = sin_ref[0]

    q_lat = zcol(Z_QLAT_OFF, MLA_Q_LORA)
    kv_lat = zcol(Z_KVLAT_OFF, MLA_KV_LORA)
    k_rope = zcol(Z_KROPE_OFF, LANES)
    qn = (_rms(q_lat) * qg_ref[...]).astype(BF16)
    kvn = (_rms(kv_lat) * kvg_ref[...]).astype(BF16)
    q = _dot(qn, wq_ref[...])
    kx = _dot(kvn, wk_ref[...])
    v = _dot(kvn, wv_ref[...])
    v_ref[0] = v.astype(BF16)
    project_piece(2)
    k_pe = _rope(k_rope, cosf, sinf, lane)
    qs, ks = [], []
    for hd in range(MLA_HEADS):
        sl = slice(hd * LANES, (hd + 1) * LANES)
        qs.append(_rope(q[:, sl], cosf, sinf, lane))
        ks.append(kx[:, sl] + k_pe)
    q = jnp.concatenate(qs, axis=1)
    k = jnp.concatenate(ks, axis=1)
    project_piece(3)
    qss = _mm_exact_rhs(q * q, bd_ref[...]) * (1.0 / MLA_QK_DIM)
    kss = _mm_exact_rhs(k * k, bd_ref[...]) * (1.0 / MLA_QK_DIM)
    qkg = qkg_ref[...]
    q = q * lax.rsqrt(qss + NORM_EPS) * qkg[0:1] * (MLA_QK_DIM ** -0.5)
    k = k * lax.rsqrt(kss + NORM_EPS) * qkg[1:2]
    q_ref[0] = q.astype(BF16)
    k_ref[0] = k.astype(BF16)


def _mixin(x, mod_l, gain, win, cosf, sinf, poolw, pools, qg, kvg, wq, wk, wv, qkg, bd128):
    B, T, _ = x.shape
    tm = TM_MIX
    const = lambda shape: pl.BlockSpec(shape, lambda b, i: tuple(0 for _ in shape))
    tok = lambda w: pl.BlockSpec((1, tm, w), lambda b, i: (b, i, 0))
    return pl.pallas_call(
        _mixin_kernel,
        grid=(B, T // tm),
        in_specs=[tok(D_MODEL),
                  pl.BlockSpec((1, 6, D_MODEL), lambda b, i: (b, 0, 0)),
                  const((1, D_MODEL)), const((D_MODEL, Z_COLS)),
                  tok(LANES), tok(LANES),
                  const((POOL_WIDTH, POOL_WIDTH)), const((1, POOL_WIDTH)),
                  const((1, MLA_Q_LORA)), const((1, MLA_KV_LORA)),
                  const((MLA_Q_LORA, MLA_HEADS * LANES)), const((MLA_KV_LORA, MLA_HEADS * LANES)),
                  const((MLA_KV_LORA, MLA_HEADS * HEAD_DIM)), const((2, MLA_HEADS * LANES)),
                  const((MLA_HEADS * LANES, MLA_HEADS * LANES))],
        out_specs=[tok(ZR_COLS), tok(POOL_WIDTH), tok(MLA_HEADS * LANES), tok(MLA_HEADS * LANES),
                   tok(MLA_HEADS * HEAD_DIM)],
        out_shape=[jax.ShapeDtypeStruct((B, T, ZR_COLS), F32),
                   jax.ShapeDtypeStruct((B, T, POOL_WIDTH), BF16),
                   jax.ShapeDtypeStruct((B, T, MLA_HEADS * LANES), BF16),
                   jax.ShapeDtypeStruct((B, T, MLA_HEADS * LANES), BF16),
                   jax.ShapeDtypeStruct((B, T, MLA_HEADS * HEAD_DIM), BF16)],
        scratch_shapes=[pltpu.VMEM((POOL_HALO + tm, POOL_WIDTH), F32)],
        compiler_params=_cparams(("arbitrary", "arbitrary")),
    )(x, mod_l, gain, win, cosf, sinf, poolw, pools, qg, kvg, wq, wk, wv, qkg, bd128)


WKV_PASSES_SCORE = 1
WKV_PASSES_INV = 1
WKV_PASSES_APPLY = 1
WKV_PASSES_STATE = 1
WKV_STEP_CHUNKS = 1
WKV_STEP_SEQS = 4


def _stack_heads(xp, lane):
    return jnp.concatenate([jnp.where(lane < HEAD_DIM, xp, 0.0),
                            jnp.where(lane >= HEAD_DIM, xp, 0.0)], axis=0)


def _wkv_prep(r, lw, k, v, kk, a, tri, masks):
    L = r[0].shape[0]
    n = 2 * L
    nc = len(r)
    each = lambda f, *ls: [f(*xs) for xs in zip(*ls)]
    lane = lax.broadcasted_iota(jnp.int32, (L, LANES), 1)
    stack = lambda x: _stack_heads(x, lane)
    cum = each(lambda x: _mm_exact_rhs_left(tri, x), lw)
    cum_last = each(lambda c: c[L - 1:L, :], cum)
    e_w = each(jnp.exp, cum)
    e_wm = each(lambda c, x: jnp.exp(c - x), cum, lw)
    e_iw = each(lambda c: jnp.exp(-c), cum)
    e_d = each(lambda cl, c: jnp.exp(cl - c), cum_last, cum)
    beta = each(lambda x, y: x * y, kk, a)
    r_t = each(lambda x, e: stack(x * e), r, e_w)
    a_t = each(lambda x, e: stack(-x * e), kk, e_wm)
    b_t = each(lambda x, e: stack(x * e), beta, e_iw)
    k_t = each(lambda x, e: stack(x * e), k, e_iw)
    b_d = each(lambda x, e: stack(x * e), beta, e_d)
    k_d = each(lambda x, e: stack(x * e), k, e_d)
    v_s = each(stack, v)
    yield
    g = each(lambda at, rt, bt, kt: _mm(jnp.concatenate([at, rt], axis=0),
                                        jnp.concatenate([bt, kt], axis=0), NT, WKV_PASSES_SCORE),
             a_t, r_t, b_t, k_t)
    strict, incl, levels = masks
    a_ab = each(lambda x: jnp.where(strict, x[:n, :n], 0.0), g)
    a_ak = each(lambda x: jnp.where(strict, x[:n, n:], 0.0), g)
    s_rb = each(lambda x: jnp.where(incl, x[n:, :n], 0.0), g)
    s_rk = each(lambda x: jnp.where(incl, x[n:, n:], 0.0), g)
    eye = jnp.where(levels[0][1], 1.0, 0.0)
    tinv = each(lambda x: eye + jnp.where(levels[0][0], x, 0.0), a_ab)
    yield
    for lvl_mask, _ in levels[1:]:
        et = each(lambda x, t: _mm(jnp.where(lvl_mask, x, 0.0), t, NN, WKV_PASSES_INV), a_ab, tinv)
        tinv = each(lambda t, x: t + _mm(t, x, NN, WKV_PASSES_INV), tinv, et)
        yield
    av = each(lambda x, y: _mm(x, y, NN, WKV_PASSES_APPLY), a_ak, v_s)
    tx = each(lambda t, x, y: _mm(t, jnp.concatenate([x, y], axis=1), NN, WKV_PASSES_APPLY),
              tinv, a_t, av)
    yield
    ra = each(lambda rt, s, x: rt + _mm(s, x[:, :LANES], NN, WKV_PASSES_APPLY), r_t, s_rb, tx)
    c2 = each(lambda sb, sk, x, vs: _mm(jnp.concatenate([sb, sk], axis=1),
                                        jnp.concatenate([x[:, LANES:], vs], axis=0),
                                        NN, WKV_PASSES_APPLY), s_rb, s_rk, tx, v_s)
    yield
    tb = each(lambda x, bd: _mm(x.T, bd, NN, WKV_PASSES_APPLY), tx, b_d)
    c3 = each(lambda x, vs, kd: x[LANES:] + _mm(vs.T, kd, NN, WKV_PASSES_APPLY), tb, v_s, k_d)
    return [(ra[i], c2[i], jnp.exp(cum_last[i]), tb[i][:LANES], c3[i]) for i in range(nc)]


def _mm_exact_rhs_left(tri_bf16, x):
    x0 = x.astype(BF16)
    r1 = x - x0.astype(F32)
    x1 = r1.astype(BF16)
    x2 = (r1 - x1.astype(F32)).astype(BF16)
    return _dot(tri_bf16, x0) + (_dot(tri_bf16, x1) + _dot(tri_bf16, x2))


def _wkv_masks(L):
    n = 2 * L
    row = lax.broadcasted_iota(jnp.int32, (n, n), 0)
    col = lax.broadcasted_iota(jnp.int32, (n, n), 1)
    strict = row > col
    incl = row >= col
    levels = []
    m = 1
    while m < L:
        same = (row // (2 * m)) == (col // (2 * m))
        lvl = same & ((row % (2 * m)) >= m) & ((col % (2 * m)) < m)
        levels.append((lvl, row == col))
        m *= 2
    return strict, incl, levels


def _rwkv_kernel(has_vres, *refs):
    if has_vres:
        (z_ref, vf_ref, mu_ref, vec_ref, w2a2_ref, g2v2_ref, bd_ref,
         y_ref, carry, state) = refs
    else:
        (z_ref, mu_ref, vec_ref, w2a2_ref, g2v2_ref, bd_ref,
         y_ref, vout_ref, carry, state) = refs
    c = pl.program_id(1)
    n_seq, seq_rows = z_ref.shape[0], z_ref.shape[1]
    rows = n_seq * seq_rows
    L = WKV_CHUNK
    W = RWKV_WIDTH

    @pl.when(c == 0)
    def _():
        carry[...] = jnp.zeros(carry.shape, F32)
        state[...] = jnp.zeros(state.shape, F32)

    flat = lambda ref: jnp.concatenate([ref[s] for s in range(n_seq)], axis=0)
    z = flat(z_ref)
    row = lax.broadcasted_iota(jnp.int32, z.shape, 0)
    prev = pltpu.roll(z, 1, axis=0)
    for s in range(n_seq):
        prev = jnp.where(row == s * seq_rows, carry[s, SUBLANES - 1:SUBLANES, :], prev)
        carry[s] = z[(s + 1) * seq_rows - SUBLANES:(s + 1) * seq_rows, :]
    zs_all = z + mu_ref[...] * (prev - z)
    if has_vres:
        vf_all = flat(vf_ref)
    vec = vec_ref[...]
    w0, a0, k_k, k_a, r_k, ln_g, ln_b, v0 = (vec[j:j + 1] for j in range(8))
    bd = bd_ref[...]
    masks = _wkv_masks(L)
    rowt = lax.broadcasted_iota(jnp.int32, (L, L), 0)
    colt = lax.broadcasted_iota(jnp.int32, (L, L), 1)
    tri = jnp.where(rowt >= colt, 1.0, 0.0).astype(BF16)
    n_pairs = RWKV_HEADS // 2
    S_now = [{(s, p): state[s * n_pairs + p] for s in range(n_seq) for p in range(n_pairs)}]

    def group(r0, r1):
        zs = zs_all[r0:r1]
        r = zs[:, 0:W]
        k = zs[:, W:2 * W]
        v = zs[:, 2 * W:3 * W]
        wa = zs[:, 3 * W:3 * W + LANES]
        gb = zs[:, 3 * W + LANES:ZR_COLS]
        lane_a = lax.broadcasted_iota(jnp.int32, wa.shape, 1)
        t1 = _dot(jnp.where(lane_a < RWKV_DECAY_LORA, jnp.tanh(wa), wa).astype(BF16), w2a2_ref[...])
        lane_g = lax.broadcasted_iota(jnp.int32, gb.shape, 1)
        t2 = _dot(jnp.where(lane_g < RWKV_GATE_LORA, _sigmoid(gb), gb).astype(BF16), g2v2_ref[...])
        yield
        xw = w0 + t1[:, :W]
        w_log = -(jnp.maximum(-xw, 0.0) + jnp.log(1.0 + jnp.exp(-jnp.abs(xw)))) - 0.5
        lw = -jnp.exp(w_log)
        a = _sigmoid(a0 + t1[:, W:])
        g = t2[:, :W]
        if has_vres:
            v = v + (vf_all[r0:r1] - v) * _sigmoid(v0 + t2[:, W:])
        else:
            for s in range(n_seq):
                vout_ref[s] = v[s * seq_rows:(s + 1) * seq_rows]
        kk = k * k_k
        kk = kk * jnp.minimum(lax.rsqrt(_mm_exact_rhs(kk * kk, bd)), 1e12)
        k = k * (1.0 + (a - 1.0) * k_a)
        yield
        n_chunks = seq_rows // L
        idx = [(s, ch, p) for s in range(n_seq) for ch in range(n_chunks) for p in range(n_pairs)]
        cut = lambda x: [x[s * seq_rows + ch * L:s * seq_rows + (ch + 1) * L, p * LANES:(p + 1) * LANES]
                         for s, ch, p in idx]
        res = yield from _wkv_prep(cut(r), cut(lw), cut(k), cut(v), cut(kk), cut(a), tri, masks)
        prep = dict(zip(idx, res))
        yield
        sp = [(s, p) for s in range(n_seq) for p in range(n_pairs)]
        S = S_now[0]
        y_blk = {}
        for ch in range(n_chunks):
            y_s = {(s, p): _mm(prep[s, ch, p][0], S[s, p], NT, WKV_PASSES_STATE) + prep[s, ch, p][1]
                   for s, p in sp}
            for s in range(n_seq):
                y_blk[s, ch] = jnp.concatenate([y_s[s, p][:L] + y_s[s, p][L:] for p in range(n_pairs)], axis=1)
            S = {(s, p): S[s, p] * prep[s, ch, p][2] + _mm(S[s, p], prep[s, ch, p][3], NN, WKV_PASSES_STATE)
                 + prep[s, ch, p][4] for s, p in sp}
        S_now[0] = S
        y = jnp.concatenate([y_blk[s, ch] for s in range(n_seq) for ch in range(n_chunks)], axis=0)
        yield
        inv = 1.0 / HEAD_DIM
        mean = _mm_exact_rhs(y, bd) * inv
        yc = y - mean
        var = _mm_exact_rhs(yc * yc, bd) * inv
        yn = yc * lax.rsqrt(var + RWKV_LNX_EPS) * ln_g + ln_b
        bonus = _mm_exact_rhs(r * k * r_k, bd) * v
        out = ((yn + bonus) * g).astype(BF16)
        for s in range(n_seq):
            y_ref[s] = out[s * seq_rows:(s + 1) * seq_rows]

    for _ in group(0, rows):
        pass
    for s in range(n_seq):
        for p in range(n_pairs):
            state[s * n_pairs + p] = S_now[0][s, p]


def _rwkv(zr, v_first, mu, vec8, w2a2, g2v2, bd64):
    B, T, _ = zr.shape
    L = WKV_CHUNK * WKV_STEP_CHUNKS
    ns = WKV_STEP_SEQS
    has_vres = v_first is not None
    const = lambda shape: pl.BlockSpec(shape, lambda b, c: tuple(0 for _ in shape))
    tok = lambda w: pl.BlockSpec((ns, L, w), lambda b, c: (b, c, 0))
    in_specs = [tok(ZR_COLS)]
    args = [zr]
    if has_vres:
        in_specs.append(tok(RWKV_WIDTH))
        args.append(v_first)
    in_specs += [const((1, ZR_COLS)), const((8, RWKV_WIDTH)), const((LANES, 2 * RWKV_WIDTH)),
                 const((2 * LANES, 2 * RWKV_WIDTH)), const((RWKV_WIDTH, RWKV_WIDTH))]
    args += [mu, vec8, w2a2, g2v2, bd64]
    out_specs = [tok(RWKV_WIDTH)]
    out_shape = [jax.ShapeDtypeStruct((B, T, RWKV_WIDTH), BF16)]
    if not has_vres:
        out_specs.append(tok(RWKV_WIDTH))
        out_shape.append(jax.ShapeDtypeStruct((B, T, RWKV_WIDTH), F32))
    outs = pl.pallas_call(
        functools.partial(_rwkv_kernel, has_vres),
        grid=(B // ns, T // L),
        in_specs=in_specs, out_specs=out_specs, out_shape=out_shape,
        scratch_shapes=[pltpu.VMEM((ns, SUBLANES, ZR_COLS), F32),
                        pltpu.VMEM((ns * (RWKV_HEADS // 2), LANES, LANES), F32)],
        compiler_params=_cparams(("arbitrary", "arbitrary")),
    )(*args)
    return (outs[0], v_first) if has_vres else (outs[0], outs[1])


def _attn_step(q_ref, k_ref, v_ref, m_sc, l_sc, acc_sc, masked):
    tq = q_ref.shape[1]
    tk = k_ref.shape[1]
    lane = lax.broadcasted_iota(jnp.int32, (tq, LANES), 1)
    if masked:
        rowi = lax.broadcasted_iota(jnp.int32, (tq, tk), 0)
        coli = lax.broadcasted_iota(jnp.int32, (tq, tk), 1)
        keep = coli <= rowi
    heads = range(MLA_HEADS)
    s = [_dot(q_ref[0, :, hd * LANES:(hd + 1) * LANES], k_ref[0, :, hd * LANES:(hd + 1) * LANES], NT)
         for hd in heads]
    if masked:
        s = [jnp.where(keep, x, NEG_INF) for x in s]
    m_prev = [m_sc[hd] for hd in heads]
    m_new = [jnp.maximum(m_prev[hd], jnp.max(s[hd], axis=-1, keepdims=True)) for hd in heads]
    alpha = [jnp.exp(m_prev[hd] - m_new[hd]) for hd in heads]
    p = [jnp.exp(s[hd] - jnp.concatenate([m_new[hd]] * (tk // LANES), axis=1)) for hd in heads]
    for hd in heads:
        l_sc[hd] = alpha[hd] * l_sc[hd] + jnp.sum(p[hd], axis=-1, keepdims=True)
        m_sc[hd] = m_new[hd]
    pv = [_dot(p[hd].astype(BF16), v_ref[0, :, (hd // 2) * LANES:(hd // 2 + 1) * LANES]) for hd in heads]
    first = lane < HEAD_DIM
    for pr in range(MLA_HEADS // 2):
        acc_sc[pr] = (acc_sc[pr] * jnp.where(first, alpha[2 * pr], alpha[2 * pr + 1])
                      + jnp.where(first, pv[2 * pr], pv[2 * pr + 1]))


def _attn_step_bounded(q_ref, k_ref, v_ref, cb_ref, l_sc, acc_sc, masked):
    tq = q_ref.shape[1]
    tk = k_ref.shape[1]
    lane = lax.broadcasted_iota(jnp.int32, (tq, LANES), 1)
    if masked:
        rowi = lax.broadcasted_iota(jnp.int32, (tq, tk), 0)
        coli = lax.broadcasted_iota(jnp.int32, (tq, tk), 1)
        keep = coli <= rowi
    pv = []
    for hd in range(MLA_HEADS):
        s = _dot(q_ref[0, :, hd * LANES:(hd + 1) * LANES], k_ref[0, :, hd * LANES:(hd + 1) * LANES], NT)
        c = cb_ref[0, hd:hd + 1, :]
        p = jnp.exp(s - jnp.concatenate([c] * (tk // LANES), axis=1))
        if masked:
            p = jnp.where(keep, p, 0.0)
        part = p[:, 0:LANES]
        for t in range(1, tk // LANES):
            part = part + p[:, t * LANES:(t + 1) * LANES]
        l_sc[hd] = l_sc[hd] + part
        pv.append(_dot(p.astype(BF16), v_ref[0, :, (hd // 2) * LANES:(hd // 2 + 1) * LANES]))
    first = lane < HEAD_DIM
    for pr in range(MLA_HEADS // 2):
        acc_sc[pr] = acc_sc[pr] + jnp.where(first, pv[2 * pr], pv[2 * pr + 1])


def _attn_finish(o_ref, l_sc, acc_sc, lane_partial):
    tq = o_ref.shape[1]
    lane = lax.broadcasted_iota(jnp.int32, (tq, LANES), 1)
    outs = []
    for pr in range(MLA_HEADS // 2):
        la, lb = l_sc[2 * pr], l_sc[2 * pr + 1]
        if lane_partial:
            la = jnp.sum(la, axis=-1, keepdims=True)
            lb = jnp.sum(lb, axis=-1, keepdims=True)
        outs.append(acc_sc[pr] / jnp.where(lane < HEAD_DIM, la, lb))
    o_ref[0] = jnp.concatenate(outs, axis=1).astype(BF16)


def _attn_kernel(ok_ref, qi_ref, kj_ref, q_ref, k_ref, v_ref, cb_ref, o_ref, m_sc, l_sc, acc_sc):
    i = qi_ref[pl.program_id(1)]
    j = kj_ref[pl.program_id(1)]
    bounded = ok_ref[pl.program_id(0)] == 1
    exact = jnp.logical_not(bounded)

    @pl.when(j == 0)
    def _():
        m_sc[...] = jnp.full(m_sc.shape, NEG_INF, F32)
        l_sc[...] = jnp.zeros(l_sc.shape, F32)
        acc_sc[...] = jnp.zeros(acc_sc.shape, F32)

    @pl.when(bounded & (j < i))
    def _():
        _attn_step_bounded(q_ref, k_ref, v_ref, cb_ref, l_sc, acc_sc, masked=False)

    @pl.when(bounded & (j == i))
    def _():
        _attn_step_bounded(q_ref, k_ref, v_ref, cb_ref, l_sc, acc_sc, masked=True)
        _attn_finish(o_ref, l_sc, acc_sc, lane_partial=True)

    @pl.when(exact & (j < i))
    def _():
        _attn_step(q_ref, k_ref, v_ref, m_sc, l_sc, acc_sc, masked=False)

    @pl.when(exact & (j == i))
    def _():
        _attn_step(q_ref, k_ref, v_ref, m_sc, l_sc, acc_sc, masked=True)
        _attn_finish(o_ref, l_sc, acc_sc, lane_partial=False)


def _attention(q, k, v, qk_gain):
    B, T, _ = q.shape
    nq = T // TQ
    gmax = jnp.max(jnp.abs(qk_gain), axis=1)
    c = gmax[0] * gmax[1] * (MLA_QK_DIM ** 0.5) * ATTN_BOUND_SLACK
    ok = jnp.broadcast_to((c <= ATTN_BOUND_MAX).astype(jnp.int32), (B,))
    cb = jnp.broadcast_to(c, (B, MLA_HEADS, LANES))
    pairs = [(i, j) for i in range(nq) for j in range(i + 1)]
    qi = jnp.asarray([p[0] for p in pairs], jnp.int32)
    kj = jnp.asarray([p[1] for p in pairs], jnp.int32)
    grid_spec = pltpu.PrefetchScalarGridSpec(
        num_scalar_prefetch=3,
        grid=(B, len(pairs)),
        in_specs=[pl.BlockSpec((1, TQ, MLA_HEADS * LANES), lambda b, t, ok, qi, kj: (b, qi[t], 0)),
                  pl.BlockSpec((1, TQ, MLA_HEADS * LANES), lambda b, t, ok, qi, kj: (b, kj[t], 0)),
                  pl.BlockSpec((1, TQ, MLA_HEADS * HEAD_DIM), lambda b, t, ok, qi, kj: (b, kj[t], 0)),
                  pl.BlockSpec((1, MLA_HEADS, LANES), lambda b, t, ok, qi, kj: (b, 0, 0))],
        out_specs=pl.BlockSpec((1, TQ, MLA_HEADS * HEAD_DIM), lambda b, t, ok, qi, kj: (b, qi[t], 0)),
        scratch_shapes=[pltpu.VMEM((MLA_HEADS, TQ, LANES), F32),
                        pltpu.VMEM((MLA_HEADS, TQ, LANES), F32),
                        pltpu.VMEM((MLA_HEADS // 2, TQ, LANES), F32)])
    return pl.pallas_call(
        _attn_kernel,
        grid_spec=grid_spec,
        out_shape=jax.ShapeDtypeStruct((B, T, MLA_HEADS * HEAD_DIM), BF16),
        compiler_params=_cparams(("arbitrary", "arbitrary")),
    )(ok, qi, kj, q, k, v, cb)


def _mixout_kernel(has_router, *refs):
    if has_router:
        (x_ref, yr_ref, yp_ref, ym_ref, wo_ref, mod_ref, gain_ref, rt_ref, tri_ref,
         xo_ref, h_ref, route_ref, cnt_ref, cnt_sc) = refs
    else:
        x_ref, yr_ref, yp_ref, ym_ref, wo_ref, mod_ref, gain_ref, xo_ref, h_ref = refs
    mod = mod_ref[0]
    o1 = RWKV_WIDTH
    o2 = RWKV_WIDTH + POOL_WIDTH
    mix = (_dot(yr_ref[0], wo_ref[0:o1, :]) + _dot(yp_ref[0], wo_ref[o1:o2, :])
           + _dot(ym_ref[0], wo_ref[o2:, :]))
    x = x_ref[0] + mod[2:3] * mix
    xo_ref[0] = x
    h = _rms(x) * gain_ref[...] * (1.0 + mod[4:5]) + mod[3:4]
    if not has_router:
        h_ref[0] = h.astype(BF16)
    else:
        hp = _pack_bf16_pairs(h)
        for ck in range(ROW_CHUNKS):
            h_ref[ck, 0] = hp[:, ck * PACK_CHUNK_W:(ck + 1) * PACK_CHUNK_W]
        logits = _dot(h.astype(BF16), rt_ref[...])
        lane = lax.broadcasted_iota(jnp.int32, logits.shape, 1).astype(F32)
        lg = jnp.where(lane < N_EXPERTS, logits, -jnp.inf)
        m1 = jnp.max(lg, axis=-1, keepdims=True)
        i1 = jnp.min(jnp.where(lg == m1, lane, float(LANES)), axis=-1, keepdims=True)
        lg2 = jnp.where(lane == i1, -jnp.inf, lg)
        m2 = jnp.max(lg2, axis=-1, keepdims=True)
        i2 = jnp.min(jnp.where(lg2 == m2, lane, float(LANES)), axis=-1, keepdims=True)
        e2 = jnp.exp(m2 - m1)
        g1 = 1.0 / (1.0 + e2)
        g2 = e2 / (1.0 + e2)
        first = (pl.program_id(0) == 0) & (pl.program_id(1) == 0)

        @pl.when(first)
        def _():
            cnt_sc[...] = jnp.zeros(cnt_sc.shape, F32)

        hit1 = lane == i1
        hit2 = lane == i2
        onehot = jnp.where(hit1 | hit2, 1.0, 0.0)
        prefix = _dot(tri_ref[...], onehot.astype(BF16)) + cnt_sc[0:1, :]
        r1 = jnp.sum(jnp.where(hit1, prefix, 0.0), axis=-1, keepdims=True)
        r2 = jnp.sum(jnp.where(hit2, prefix, 0.0), axis=-1, keepdims=True)
        cnt_sc[...] = cnt_sc[...] + jnp.sum(onehot, axis=0, keepdims=True)
        cnt_ref[...] = cnt_sc[...]
        vals = (i1, i2, g1, g2, r1, r2)
        route = jnp.zeros(logits.shape, F32)
        for pos, val in enumerate(vals):
            route = jnp.where(lane == pos, val, route)
        route_ref[0] = route


def _mixout(x, yr, yp, ym, wo, mod_l, gain, router_p):
    B, T, _ = x.shape
    tm = TM_MIX
    has_router = router_p is not None
    const = lambda shape: pl.BlockSpec(shape, lambda b, i: tuple(0 for _ in shape))
    tok = lambda w: pl.BlockSpec((1, tm, w), lambda b, i: (b, i, 0))
    in_specs = [tok(D_MODEL), tok(RWKV_WIDTH), tok(POOL_WIDTH), tok(MLA_HEADS * HEAD_DIM),
                const((D_MODEL, D_MODEL)), pl.BlockSpec((1, 6, D_MODEL), lambda b, i: (b, 0, 0)),
                const((1, D_MODEL))]
    args = [x, yr, yp, ym, wo, mod_l, gain]
    out_specs = [tok(D_MODEL), tok(D_MODEL)]
    out_shape = [jax.ShapeDtypeStruct((B, T, D_MODEL), F32), jax.ShapeDtypeStruct((B, T, D_MODEL), BF16)]
    scratch = []
    if has_router:
        out_specs[1] = pl.BlockSpec((ROW_CHUNKS, 1, tm, PACK_CHUNK_W), lambda b, i: (0, b, i, 0))
        out_shape[1] = jax.ShapeDtypeStruct((ROW_CHUNKS, B, T, PACK_CHUNK_W), jnp.uint32)
        ids = np.arange(tm)
        tri = jnp.asarray(ids[:, None] > ids[None, :], BF16)
        in_specs += [const((D_MODEL, LANES)), const((tm, tm))]
        args += [router_p, tri]
        out_specs += [tok(LANES), const((SUBLANES, LANES))]
        out_shape += [jax.ShapeDtypeStruct((B, T, LANES), F32),
                      jax.ShapeDtypeStruct((SUBLANES, LANES), F32)]
        scratch = [pltpu.VMEM((SUBLANES, LANES), F32)]
    return pl.pallas_call(
        functools.partial(_mixout_kernel, has_router),
        grid=(B, T // tm),
        in_specs=in_specs, out_specs=out_specs, out_shape=out_shape, scratch_shapes=scratch,
        compiler_params=_cparams(("arbitrary", "arbitrary")),
    )(*args)


def _ffn_kernel(h_ref, wg_ref, wu_ref, wo_ref, x_ref, mod_ref, o_ref, acc):
    j = pl.program_id(1)

    @pl.when(j == 0)
    def _():
        acc[...] = jnp.zeros(acc.shape, F32)

    h = h_ref[...]
    gg = _dot(h, wg_ref[...])
    uu = _dot(h, wu_ref[...])
    acc[...] += _dot((_silu(gg) * uu).astype(BF16), wo_ref[...])

    @pl.when(j == pl.num_programs(1) - 1)
    def _():
        o_ref[...] = x_ref[...] + mod_ref[0][5:6] * acc[...]


def _ffn(h2, w_in, w_out, x, mod_l):
    N = h2.shape[0]
    T = N // mod_l.shape[0]
    tm, tf = TM_FFN, TF_FFN
    nf = D_FF // tf
    per_b = T // tm
    return pl.pallas_call(
        _ffn_kernel,
        grid=(N // tm, nf),
        in_specs=[pl.BlockSpec((tm, D_MODEL), lambda i, j: (i, 0)),
                  pl.BlockSpec((D_MODEL, tf), lambda i, j: (0, j)),
                  pl.BlockSpec((D_MODEL, tf), lambda i, j: (0, j + nf)),
                  pl.BlockSpec((tf, D_MODEL), lambda i, j: (j, 0)),
                  pl.BlockSpec((tm, D_MODEL), lambda i, j: (i, 0)),
                  pl.BlockSpec((1, 6, D_MODEL), lambda i, j: (i // per_b, 0, 0))],
        out_specs=pl.BlockSpec((tm, D_MODEL), lambda i, j: (i, 0)),
        out_shape=jax.ShapeDtypeStruct((N, D_MODEL), F32),
        scratch_shapes=[pltpu.VMEM((tm, D_MODEL), F32)],
        compiler_params=_cparams(("arbitrary", "arbitrary")),
    )(h2, w_in, w_in, w_out, x, mod_l)


def _moe_kernel(be_ref, nv_ref, last_ref, x_ref, wg_ref, wu_ref, wo_ref, o_ref, acc, xm):
    i = pl.program_id(0)
    j = pl.program_id(1)

    @pl.when(i <= last_ref[0])
    def _():
        @pl.when(j == 0)
        def _():
            acc[...] = jnp.zeros(acc.shape, F32)
            row = lax.broadcasted_iota(jnp.int32, (xm.shape[0], 1), 0)
            xp = jnp.concatenate([x_ref[ck] for ck in range(ROW_CHUNKS)], axis=1)
            xp = jnp.where(row < nv_ref[i], xp, jnp.uint32(0))
            xm[...] = _unpack_bf16_pairs(xp)

        x = xm[...]
        gg = _dot(x, wg_ref[0].astype(BF16))
        uu = _dot(x, wu_ref[0].astype(BF16))
        acc[...] += _dot((_silu(gg) * uu).astype(BF16), wo_ref[0].astype(BF16))

        @pl.when(j == pl.num_programs(1) - 1)
        def _():
            yp = _pack_bf16_pairs(acc[...])
            for ck in range(ROW_CHUNKS):
                o_ref[ck] = yp[:, ck * PACK_CHUNK_W:(ck + 1) * PACK_CHUNK_W]


def _moe_experts(xs, w_in, w_out, block_exp, n_valid, last_blk):
    n_rows = xs.shape[1]
    tm, tf = MOE_BLOCK, TF_MOE
    nf = D_FF_EXPERT // tf
    blk = lambda i, last: jnp.minimum(i, last[0])
    chunk = lambda i, j, last: jnp.where(i <= last[0], j, nf - 1)
    grid_spec = pltpu.PrefetchScalarGridSpec(
        num_scalar_prefetch=3,
        grid=(n_rows // tm, nf),
        in_specs=[pl.BlockSpec((ROW_CHUNKS, tm, PACK_CHUNK_W),
                               lambda i, j, be, nv, last: (0, blk(i, last), 0)),
                  pl.BlockSpec((1, D_MODEL, tf),
                               lambda i, j, be, nv, last: (be[blk(i, last)], 0, chunk(i, j, last))),
                  pl.BlockSpec((1, D_MODEL, tf),
                               lambda i, j, be, nv, last: (be[blk(i, last)], 0, chunk(i, j, last) + nf)),
                  pl.BlockSpec((1, tf, D_MODEL),
                               lambda i, j, be, nv, last: (be[blk(i, last)], chunk(i, j, last), 0))],
        out_specs=pl.BlockSpec((ROW_CHUNKS, tm, PACK_CHUNK_W), lambda i, j, be, nv, last: (0, blk(i, last), 0)),
        scratch_shapes=[pltpu.VMEM((tm, D_MODEL), F32), pltpu.VMEM((tm, D_MODEL), BF16)])
    return pl.pallas_call(
        _moe_kernel,
        grid_spec=grid_spec,
        out_shape=jax.ShapeDtypeStruct((ROW_CHUNKS, n_rows, PACK_CHUNK_W), jnp.uint32),
        compiler_params=_cparams(("arbitrary", "arbitrary")),
    )(block_exp, n_valid, last_blk, xs, w_in, w_in, w_out)


def _sc_mesh():
    return plsc.VectorSubcoreMesh(core_axis_name="c", subcore_axis_name="s")


def _sc_scatter_rows(x, dest, n_rows):
    N, D = x.shape
    K = dest.shape[0]
    win = SC_WINDOW

    @pl.kernel(out_type=jax.ShapeDtypeStruct((n_rows, D), x.dtype), mesh=_sc_mesh(), scratch_types=[])
    def scatter(x_hbm, d_hbm, o_hbm):
        def body(x_vmem, *idx_vmem):
            for iv in idx_vmem:
                pltpu.sync_copy(x_vmem, o_hbm.at[iv.at[0]])

        pltpu.emit_pipeline(
            body,
            grid=(N // win,),
            in_specs=[pl.BlockSpec((win, D), lambda i: (i, 0))]
            + [pl.BlockSpec((1, win), functools.partial(lambda k, i: (k, i), k)) for k in range(K)],
            out_specs=[],
            core_axis_name=("c", "s"),
            dimension_semantics=(pltpu.PARALLEL,),
        )(x_hbm, *([d_hbm] * K))

    return scatter(x, dest)


def _sc_gather_rows(x, idx):
    n = idx.shape[0]
    D = x.shape[1]
    win = SC_WINDOW

    @pl.kernel(out_type=jax.ShapeDtypeStruct((n, D), x.dtype), mesh=_sc_mesh(), scratch_types=[])
    def gather(x_hbm, i_hbm, o_hbm):
        def body(i_vmem, o_vmem):
            pltpu.sync_copy(x_hbm.at[i_vmem.at[0]], o_vmem)

        pltpu.emit_pipeline(
            body,
            grid=(n // win,),
            in_specs=[pl.BlockSpec((1, win), lambda i: (0, i))],
            out_specs=[pl.BlockSpec((win, D), lambda i: (i, 0))],
            core_axis_name=("c", "s"),
            dimension_semantics=(pltpu.PARALLEL,),
        )(i_hbm, o_hbm)

    return gather(x, idx.reshape(1, n))


def _combine_kernel(x_ref, ya_ref, yb_ref, route_ref, mod_ref, o_ref):
    rt = route_ref[...]
    ya = _unpack_bf16_pairs(jnp.concatenate([ya_ref[0, ck] for ck in range(ROW_CHUNKS)], axis=1))
    yb = _unpack_bf16_pairs(jnp.concatenate([yb_ref[0, ck] for ck in range(ROW_CHUNKS)], axis=1))
    f = rt[:, 2:3] * ya.astype(F32) + rt[:, 3:4] * yb.astype(F32)
    o_ref[...] = x_ref[...] + mod_ref[0][5:6] * f


def _combine(x, y2, route, mod_l):
    N = x.shape[0]
    T = N // mod_l.shape[0]
    tm = 1024
    per_b = T // tm
    tok = pl.BlockSpec((tm, D_MODEL), lambda i: (i, 0))
    slot = lambda k: pl.BlockSpec((1, ROW_CHUNKS, tm, PACK_CHUNK_W), lambda i: (k, 0, i, 0))
    return pl.pallas_call(
        _combine_kernel,
        grid=(N // tm,),
        in_specs=[tok, slot(0), slot(1),
                  pl.BlockSpec((tm, LANES), lambda i: (i, 0)),
                  pl.BlockSpec((1, 6, D_MODEL), lambda i: (i // per_b, 0, 0))],
        out_specs=tok,
        out_shape=jax.ShapeDtypeStruct((N, D_MODEL), F32),
        compiler_params=_cparams(("arbitrary",)),
    )(x, y2, y2, route, mod_l)


def _moe(h2, route, counts, w_in, w_out, x, mod_l):
    N = x.shape[0]
    blk = MOE_BLOCK
    cnt = counts[0, :N_EXPERTS].astype(jnp.int32)
    padded = (cnt + blk - 1) // blk * blk
    pend = jnp.cumsum(padded)
    pstart = pend - padded
    e = route[:, 0:TOP_K].astype(jnp.int32)
    rank = route[:, 2 * TOP_K:3 * TOP_K].astype(jnp.int32)
    dest = (jnp.take(pstart, e) + rank).T
    n_blocks = N * TOP_K // blk + N_EXPERTS
    bstart = jnp.arange(n_blocks, dtype=jnp.int32) * blk
    block_exp = jnp.minimum(jnp.sum((bstart[:, None] >= pend[None, :]).astype(jnp.int32), axis=1),
                            N_EXPERTS - 1)
    n_valid = jnp.clip(cnt[block_exp] - (bstart - pstart[block_exp]), 0, blk).astype(jnp.int32)
    n_rows = n_blocks * blk
    dest_ck = dest[:, None, :] + (jnp.arange(ROW_CHUNKS, dtype=jnp.int32) * n_rows)[None, :, None]
    xs = _sc_scatter_rows(h2, dest_ck.reshape(TOP_K, ROW_CHUNKS * N), ROW_CHUNKS * n_rows)
    last_blk = (pend[-1:] // blk - 1).astype(jnp.int32)
    yb = _moe_experts(xs.reshape(ROW_CHUNKS, n_rows, PACK_CHUNK_W), w_in, w_out, block_exp, n_valid,
                      last_blk)
    y2 = _sc_gather_rows(yb.reshape(ROW_CHUNKS * n_rows, PACK_CHUNK_W), dest_ck.reshape(-1))
    return _combine(x, y2.reshape(TOP_K, ROW_CHUNKS, N, PACK_CHUNK_W), route, mod_l)


def _layout_w_in(w, has_vres):
    W = RWKV_WIDTH
    off_gd = 3 * W + RWKV_DECAY_LORA + RWKV_ICLR_LORA
    off_pool = off_gd + RWKV_GATE_LORA
    off_q = off_pool + POOL_WIDTH
    off_kv = off_q + MLA_Q_LORA
    off_kr = off_kv + MLA_KV_LORA
    n_base = off_kr + MLA_QK_ROPE
    d = w.shape[0]
    zeros = lambda n: jnp.zeros((d, n), w.dtype)
    vd = w[:, n_base:n_base + RWKV_VRES_LORA] if has_vres else zeros(RWKV_VRES_LORA)
    cols = [w[:, :off_gd], w[:, off_gd:off_pool], vd, zeros(ZR_COLS - off_pool - RWKV_VRES_LORA),
            w[:, off_pool:off_q], w[:, off_q:off_kv], w[:, off_kv:off_kr],
            zeros(MLA_QK_NOPE), w[:, off_kr:n_base], zeros(LANES - MLA_QK_DIM)]
    return jnp.concatenate(cols, axis=1).astype(BF16)


def _pad_heads(w, per_head, keep_from, keep_n):
    K = w.shape[0]
    wh = w.reshape(K, MLA_HEADS, per_head)[:, :, keep_from:keep_from + keep_n]
    wh = jnp.pad(wh, ((0, 0), (0, 0), (0, LANES - keep_n)))
    return wh.reshape(K, MLA_HEADS * LANES)


def kernel(x, c, positions, w_ada, b_ada, norm_gain, w_in_first, w_in_rest, mu_shift, mu_shift_v,
           rwkv_vec, rwkv_v0, rwkv_w2, rwkv_a2, rwkv_g2, rwkv_v2, pool_w, pool_scale,
           mla_q_lat_gain, mla_kv_lat_gain, mla_wq_up, mla_wkv_up, mla_qk_gain, w_out, ffn_w_in,
           ffn_w_out, moe_router, moe_w_in, moe_w_out):
    B, T, D = x.shape
    depth = w_ada.shape[0]
    W = RWKV_WIDTH
    mod = _adaln(c, w_ada, b_ada).reshape(depth, B, 6, D)
    pos3 = positions.reshape(B, T, 1)
    inv_freq = ROPE_BASE ** (-jnp.arange(0, MLA_QK_ROPE, 2, dtype=F32) / MLA_QK_ROPE)
    freq = jnp.concatenate([jnp.zeros((MLA_QK_NOPE,), F32), inv_freq, inv_freq,
                            jnp.zeros((LANES - MLA_QK_DIM,), F32)]).reshape(1, LANES)
    cosf, sinf = _rope_tables(pos3, freq)
    hid = np.arange(W) // HEAD_DIM
    bd64 = jnp.asarray(hid[:, None] == hid[None, :], BF16)
    bid = np.arange(MLA_HEADS * LANES) // LANES
    bd128 = jnp.asarray(bid[:, None] == bid[None, :], BF16)

    v_first = None
    for l in range(depth):
        has_vres = l > 0
        mod_l = mod[l]
        win = _layout_w_in(w_in_first if l == 0 else w_in_rest[l - 1], has_vres)
        poolw = jax.scipy.linalg.block_diag(*[pool_w[l, g] for g in range(len(POOL_WINDOWS))]).astype(BF16)
        wq = _pad_heads(mla_wq_up[l], MLA_QK_DIM, 0, MLA_QK_DIM).astype(BF16)
        wk = _pad_heads(mla_wkv_up[l], MLA_QK_NOPE + HEAD_DIM, 0, MLA_QK_NOPE).astype(BF16)
        wv = mla_wkv_up[l].reshape(MLA_KV_LORA, MLA_HEADS, MLA_QK_NOPE + HEAD_DIM)[:, :, MLA_QK_NOPE:]
        wv = wv.reshape(MLA_KV_LORA, MLA_HEADS * HEAD_DIM).astype(BF16)
        qkg = jnp.tile(jnp.pad(mla_qk_gain[l], ((0, 0), (0, LANES - MLA_QK_DIM))), (1, MLA_HEADS))
        zr, y_pool, q, k, v = _mixin(
            x, mod_l, norm_gain[l, 0].reshape(1, D), win, cosf, sinf, poolw,
            pool_scale[l].reshape(1, -1), mla_q_lat_gain[l].reshape(1, -1),
            mla_kv_lat_gain[l].reshape(1, -1), wq, wk, wv, qkg, bd128)

        pad_mu = ZR_COLS - mu_shift.shape[1] - RWKV_VRES_LORA
        mu_v = mu_shift_v[l - 1] if has_vres else jnp.zeros((RWKV_VRES_LORA,), F32)
        mu = jnp.concatenate([mu_shift[l], mu_v, jnp.zeros((pad_mu,), F32)]).reshape(1, ZR_COLS)
        v0 = rwkv_v0[l - 1] if has_vres else jnp.zeros((W,), F32)
        vec8 = jnp.concatenate([rwkv_vec[l], v0[None]], axis=0)
        w2a2 = jax.scipy.linalg.block_diag(rwkv_w2[l], rwkv_a2[l]).astype(BF16)
        g2 = jnp.pad(rwkv_g2[l], ((0, 2 * LANES - RWKV_GATE_LORA), (0, 0)))
        if has_vres:
            v2 = jnp.pad(rwkv_v2[l - 1], ((RWKV_GATE_LORA, 2 * LANES - RWKV_GATE_LORA - RWKV_VRES_LORA), (0, 0)))
        else:
            v2 = jnp.zeros((2 * LANES, W), F32)
        g2v2 = jnp.concatenate([g2, v2], axis=1).astype(BF16)
        y_rwkv, v_first = _rwkv(zr, v_first, mu, vec8, w2a2, g2v2, bd64)

        y_mla = _attention(q, k, v, mla_qk_gain[l])

        is_moe = (l % 2 == 1)
        router_p = None
        if is_moe:
            router_p = jnp.pad(moe_router[l // 2], ((0, 0), (0, LANES - N_EXPERTS))).astype(BF16)
        outs = _mixout(x, y_rwkv, y_pool, y_mla, w_out[l].astype(BF16), mod_l,
                       norm_gain[l, 1].reshape(1, D), router_p)
        x_mid, h2 = outs[0], outs[1]
        xf = x_mid.reshape(B * T, D)
        if is_moe:
            xo = _moe(h2.reshape(ROW_CHUNKS * B * T, PACK_CHUNK_W), outs[2].reshape(B * T, LANES), outs[3],
                      moe_w_in[l // 2], moe_w_out[l // 2], xf, mod_l)
        else:
            xo = _ffn(h2.reshape(B * T, D), ffn_w_in[l // 2].astype(BF16), ffn_w_out[l // 2].astype(BF16),
                      xf, mod_l)
        x = xo.reshape(B, T, D)
    return x
```

```python
import functools

import numpy as np
import jax
import jax.numpy as jnp
from jax import lax
from jax.experimental import pallas as pl
from jax.experimental.pallas import tpu as pltpu
from jax.experimental.pallas import tpu_sc as plsc

F32 = jnp.float32
BF16 = jnp.bfloat16

D_MODEL = 1024
HEAD_DIM = 64
RWKV_WIDTH = 512
RWKV_HEADS = 8
POOL_WIDTH = 256
POOL_WINDOWS = (2, 4, 8, 16)
POOL_HALO = 16
MLA_HEADS = 4
MLA_QK_NOPE = 64
MLA_QK_ROPE = 32
MLA_QK_DIM = 96
MLA_Q_LORA = 256
MLA_KV_LORA = 128
ROPE_BASE = 10000.0
RWKV_DECAY_LORA = 64
RWKV_ICLR_LORA = 64
RWKV_VRES_LORA = 32
RWKV_GATE_LORA = 160
RWKV_LNX_EPS = 64e-5
D_FF = 2816
N_EXPERTS = 8
TOP_K = 2
D_FF_EXPERT = 3584
NORM_EPS = 1e-6
NEG_INF = -1e30

LANES = 128
SUBLANES = 8
VMEM_LIMIT = 56 * 1024 * 1024

ZR_COLS = 1920
Z_POOL_OFF = ZR_COLS
Z_QLAT_OFF = Z_POOL_OFF + POOL_WIDTH
Z_KVLAT_OFF = Z_QLAT_OFF + MLA_Q_LORA
Z_KROPE_OFF = Z_KVLAT_OFF + MLA_KV_LORA
Z_COLS = Z_KROPE_OFF + LANES

MIXIN_PIECE_W = 512
TM_MIX = 512
WKV_CHUNK = 64
TQ = 512
ATTN_BOUND_SLACK = 1.02
ATTN_BOUND_MAX = 40.0
TM_FFN = 1024
TF_FFN = 1408
MOE_BLOCK = 1024
TF_MOE = 512
SC_WINDOW = 128
ROW_CHUNKS = 4
ROW_CHUNK_W = D_MODEL // ROW_CHUNKS
PACK_CHUNK_W = ROW_CHUNK_W // 2

SEGSUM_SPLITS = 1

NN = (((1,), (0,)), ((), ()))
NT = (((1,), (1,)), ((), ()))


def _dot(a, b, dims=NN):
    return lax.dot_general(a, b, dims, preferred_element_type=F32)


def _split2(a):
    hi = a.astype(BF16)
    lo = (a - hi.astype(F32)).astype(BF16)
    return hi, lo


def _mm(a, b, dims=NN, passes=3):
    if passes == 1:
        return _dot(a.astype(BF16), b.astype(BF16), dims)
    ah, al = _split2(a)
    bh, bl = _split2(b)
    return _dot(ah, bh, dims) + (_dot(ah, bl, dims) + _dot(al, bh, dims))


def _mm_exact_rhs(a, b_bf16, dims=NN, splits=SEGSUM_SPLITS):
    out = None
    rem = a
    for s in range(splits):
        part = rem.astype(BF16)
        term = _dot(part, b_bf16, dims)
        out = term if out is None else out + term
        if s + 1 < splits:
            rem = rem - part.astype(F32)
    return out


def _pack_bf16_pairs(h):
    w = h.shape[1] // 2
    lo = lax.bitcast_convert_type(h[:, :w].astype(BF16).astype(F32), jnp.uint32)
    hi = lax.bitcast_convert_type(h[:, w:].astype(BF16).astype(F32), jnp.uint32)
    return (lo >> 16) | (hi & jnp.uint32(0xFFFF0000))


def _unpack_bf16_pairs(p):
    lo = lax.bitcast_convert_type(p << 16, F32)
    hi = lax.bitcast_convert_type(p & jnp.uint32(0xFFFF0000), F32)
    return jnp.concatenate([lo, hi], axis=1).astype(BF16)


def _sigmoid(x):
    return 1.0 / (1.0 + jnp.exp(-x))


def _silu(x):
    return x * _sigmoid(x)


def _rms(x, eps=NORM_EPS):
    return x * lax.rsqrt(jnp.mean(x * x, axis=-1, keepdims=True) + eps)


def _cparams(sem):
    return pltpu.CompilerParams(dimension_semantics=sem, vmem_limit_bytes=VMEM_LIMIT)


def _adaln_kernel(c_ref, w_ref, b_ref, o_ref):
    ca = _silu(c_ref[...])
    o_ref[0] = _mm(ca, w_ref[0]) + b_ref[0]


def _adaln(c, w_ada, b_ada):
    L = w_ada.shape[0]
    B = c.shape[0]
    n = w_ada.shape[2] // D_MODEL
    return pl.pallas_call(
        _adaln_kernel,
        grid=(L, n),
        in_specs=[pl.BlockSpec((B, D_MODEL), lambda l, j: (0, 0)),
                  pl.BlockSpec((1, D_MODEL, D_MODEL), lambda l, j: (l, 0, j)),
                  pl.BlockSpec((1, 1, D_MODEL), lambda l, j: (l, 0, j))],
        out_specs=pl.BlockSpec((1, B, D_MODEL), lambda l, j: (l, 0, j)),
        out_shape=jax.ShapeDtypeStruct((L, B, n * D_MODEL), F32),
        compiler_params=_cparams(("arbitrary", "arbitrary")),
    )(c, w_ada, b_ada.reshape(L, 1, -1))


def _rope(x, cosf, sinf, lane):
    up = pltpu.roll(x, LANES - MLA_QK_ROPE // 2, axis=1)
    dn = pltpu.roll(x, MLA_QK_ROPE // 2, axis=1)
    rot = jnp.where(lane < MLA_QK_NOPE + MLA_QK_ROPE // 2, -up, dn)
    return x * cosf + rot * sinf


def _rope_kernel(pos_ref, freq_ref, cos_ref, sin_ref):
    tm = pos_ref.shape[1]
    lane = lax.broadcasted_iota(jnp.int32, (tm, LANES), 1)
    in_rope = (lane >= MLA_QK_NOPE) & (lane < MLA_QK_DIM)
    ang = pos_ref[0].astype(F32) * freq_ref[...]
    cos_ref[0] = jnp.where(in_rope, jnp.cos(ang), 1.0)
    sin_ref[0] = jnp.where(in_rope, jnp.sin(ang), 0.0)


def _rope_tables(pos3, freq):
    B, T, _ = pos3.shape
    tm = TM_MIX
    tok = lambda w: pl.BlockSpec((1, tm, w), lambda b, i: (b, i, 0))
    return pl.pallas_call(
        _rope_kernel,
        grid=(B, T // tm),
        in_specs=[tok(1), pl.BlockSpec((1, LANES), lambda b, i: (0, 0))],
        out_specs=[tok(LANES), tok(LANES)],
        out_shape=[jax.ShapeDtypeStruct((B, T, LANES), F32)] * 2,
        compiler_params=_cparams(("arbitrary", "arbitrary")),
    )(pos3, freq)


def _mixin_kernel(x_ref, mod_ref, gain_ref, win_ref, cos_ref, sin_ref, poolw_ref, pools_ref,
                  qg_ref, kvg_ref, wq_ref, wk_ref, wv_ref, qkg_ref, bd_ref,
                  zr_ref, yp_ref, q_ref, k_ref, v_ref, ubuf):
    i = pl.program_id(1)
    tm = x_ref.shape[1]
    x = x_ref[0]
    mod = mod_ref[0]
    @pl.when(i == 0)
    def _():
        ubuf[0:POOL_HALO, :] = jnp.zeros((POOL_HALO, POOL_WIDTH), F32)

    h = (_rms(x) * gain_ref[...] * (1.0 + mod[1:2]) + mod[0:1]).astype(BF16)
    zb = _dot(h, win_ref[:, ZR_COLS:])
    zcol = lambda off, w: zb[:, off - ZR_COLS:off - ZR_COLS + w]
    def project_piece(n):
        cols = slice(n * MIXIN_PIECE_W, min((n + 1) * MIXIN_PIECE_W, ZR_COLS))
        zr_ref[0, :, cols] = _dot(h, win_ref[:, cols])

    project_piece(0)
    u = zcol(Z_POOL_OFF, POOL_WIDTH)
    ubuf[POOL_HALO:, :] = u
    ue = ubuf[...]
    s2 = ue + pltpu.roll(ue, 1, axis=0)
    s4 = s2 + pltpu.roll(s2, 2, axis=0)
    s8 = s4 + pltpu.roll(s4, 4, axis=0)
    s16 = s8 + pltpu.roll(s8, 8, axis=0)
    ubuf[0:POOL_HALO, :] = u[tm - POOL_HALO:, :]
    lane_p = lax.broadcasted_iota(jnp.int32, (tm, POOL_WIDTH), 1)
    grp = lane_p // (POOL_WIDTH // len(POOL_WINDOWS))
    win_sum = jnp.where(grp == 0, s2[POOL_HALO:], jnp.where(grp == 1, s4[POOL_HALO:],
                        jnp.where(grp == 2, s8[POOL_HALO:], s16[POOL_HALO:])))
    win = jnp.where(grp == 0, 2, jnp.where(grp == 1, 4, jnp.where(grp == 2, 8, 16)))
    t_abs = i * tm + lax.broadcasted_iota(jnp.int32, (tm, POOL_WIDTH), 0)
    cnt = jnp.minimum(t_abs + 1, win).astype(F32)
    p = win_sum / cnt - u
    yp = _dot(p.astype(BF16), poolw_ref[...]) * pools_ref[...]
    yp_ref[0] = yp.astype(BF16)
    project_piece(1)

    lane = lax.broadcasted_iota(jnp.int32, (tm, LANES), 1)
    cosf = cos_ref[0]
    sinf = sin_ref[0]

    q_lat = zcol(Z_QLAT_OFF, MLA_Q_LORA)
    kv_lat = zcol(Z_KVLAT_OFF, MLA_KV_LORA)
    k_rope = zcol(Z_KROPE_OFF, LANES)
    qn = (_rms(q_lat) * qg_ref[...]).astype(BF16)
    kvn = (_rms(kv_lat) * kvg_ref[...]).astype(BF16)
    q = _dot(qn, wq_ref[...])
    kx = _dot(kvn, wk_ref[...])
    v = _dot(kvn, wv_ref[...])
    v_ref[0] = v.astype(BF16)
    project_piece(2)
    k_pe = _rope(k_rope, cosf, sinf, lane)
    qs, ks = [], []
    for hd in range(MLA_HEADS):
        sl = slice(hd * LANES, (hd + 1) * LANES)
        qs.append(_rope(q[:, sl], cosf, sinf, lane))
        ks.append(kx[:, sl] + k_pe)
    q = jnp.concatenate(qs, axis=1)
    k = jnp.concatenate(ks, axis=1)
    project_piece(3)
    qss = _mm_exact_rhs(q * q, bd_ref[...]) * (1.0 / MLA_QK_DIM)
    kss = _mm_exact_rhs(k * k, bd_ref[...]) * (1.0 / MLA_QK_DIM)
    qkg = qkg_ref[...]
    q = q * lax.rsqrt(qss + NORM_EPS) * qkg[0:1] * (MLA_QK_DIM ** -0.5)
    k = k * lax.rsqrt(kss + NORM_EPS) * qkg[1:2]
    q_ref[0] = q.astype(BF16)
    k_ref[0] = k.astype(BF16)


def _mixin(x, mod_l, gain, win, cosf, sinf, poolw, pools, qg, kvg, wq, wk, wv, qkg, bd128):
    B, T, _ = x.shape
    tm = TM_MIX
    const = lambda shape: pl.BlockSpec(shape, lambda b, i: tuple(0 for _ in shape))
    tok = lambda w: pl.BlockSpec((1, tm, w), lambda b, i: (b, i, 0))
    return pl.pallas_call(
        _mixin_kernel,
        grid=(B, T // tm),
        in_specs=[tok(D_MODEL),
                  pl.BlockSpec((1, 6, D_MODEL), lambda b, i: (b, 0, 0)),
                  const((1, D_MODEL)), const((D_MODEL, Z_COLS)),
                  tok(LANES), tok(LANES),
                  const((POOL_WIDTH, POOL_WIDTH)), const((1, POOL_WIDTH)),
                  const((1, MLA_Q_LORA)), const((1, MLA_KV_LORA)),
                  const((MLA_Q_LORA, MLA_HEADS * LANES)), const((MLA_KV_LORA, MLA_HEADS * LANES)),
                  const((MLA_KV_LORA, MLA_HEADS * HEAD_DIM)), const((2, MLA_HEADS * LANES)),
                  const((MLA_HEADS * LANES, MLA_HEADS * LANES))],
        out_specs=[tok(ZR_COLS), tok(POOL_WIDTH), tok(MLA_HEADS * LANES), tok(MLA_HEADS * LANES),
                   tok(MLA_HEADS * HEAD_DIM)],
        out_shape=[jax.ShapeDtypeStruct((B, T, ZR_COLS), F32),
                   jax.ShapeDtypeStruct((B, T, POOL_WIDTH), BF16),
                   jax.ShapeDtypeStruct((B, T, MLA_HEADS * LANES), BF16),
                   jax.ShapeDtypeStruct((B, T, MLA_HEADS * LANES), BF16),
                   jax.ShapeDtypeStruct((B, T, MLA_HEADS * HEAD_DIM), BF16)],
        scratch_shapes=[pltpu.VMEM((POOL_HALO + tm, POOL_WIDTH), F32)],
        compiler_params=_cparams(("arbitrary", "arbitrary")),
    )(x, mod_l, gain, win, cosf, sinf, poolw, pools, qg, kvg, wq, wk, wv, qkg, bd128)


WKV_PASSES_SCORE = 1
WKV_PASSES_INV = 1
WKV_PASSES_APPLY = 1
WKV_PASSES_STATE = 1
WKV_STEP_CHUNKS = 1
WKV_STEP_SEQS = 4


def _stack_heads(xp, lane):
    return jnp.concatenate([jnp.where(lane < HEAD_DIM, xp, 0.0),
                            jnp.where(lane >= HEAD_DIM, xp, 0.0)], axis=0)


def _wkv_prep(r, lw, k, v, kk, a, tri, masks):
    L = r[0].shape[0]
    n = 2 * L
    nc = len(r)
    each = lambda f, *ls: [f(*xs) for xs in zip(*ls)]
    lane = lax.broadcasted_iota(jnp.int32, (L, LANES), 1)
    stack = lambda x: _stack_heads(x, lane)
    cum = each(lambda x: _mm_exact_rhs_left(tri, x), lw)
    cum_last = each(lambda c: c[L - 1:L, :], cum)
    e_w = each(jnp.exp, cum)
    e_wm = each(lambda c, x: jnp.exp(c - x), cum, lw)
    e_iw = each(lambda c: jnp.exp(-c), cum)
    e_d = each(lambda cl, c: jnp.exp(cl - c), cum_last, cum)
    beta = each(lambda x, y: x * y, kk, a)
    r_t = each(lambda x, e: stack(x * e), r, e_w)
    a_t = each(lambda x, e: stack(-x * e), kk, e_wm)
    b_t = each(lambda x, e: stack(x * e), beta, e_iw)
    k_t = each(lambda x, e: stack(x * e), k, e_iw)
    b_d = each(lambda x, e: stack(x * e), beta, e_d)
    k_d = each(lambda x, e: stack(x * e), k, e_d)
    v_s = each(stack, v)
    yield
    g = each(lambda at, rt, bt, kt: _mm(jnp.concatenate([at, rt], axis=0),
                                        jnp.concatenate([bt, kt], axis=0), NT, WKV_PASSES_SCORE),
             a_t, r_t, b_t, k_t)
    strict, incl, levels = masks
    a_ab = each(lambda x: jnp.where(strict, x[:n, :n], 0.0), g)
    a_ak = each(lambda x: jnp.where(strict, x[:n, n:], 0.0), g)
    s_rb = each(lambda x: jnp.where(incl, x[n:, :n], 0.0), g)
    s_rk = each(lambda x: jnp.where(incl, x[n:, n:], 0.0), g)
    eye = jnp.where(levels[0][1], 1.0, 0.0)
    tinv = each(lambda x: eye + jnp.where(levels[0][0], x, 0.0), a_ab)
    yield
    for lvl_mask, _ in levels[1:]:
        et = each(lambda x, t: _mm(jnp.where(lvl_mask, x, 0.0), t, NN, WKV_PASSES_INV), a_ab, tinv)
        tinv = each(lambda t, x: t + _mm(t, x, NN, WKV_PASSES_INV), tinv, et)
        yield
    av = each(lambda x, y: _mm(x, y, NN, WKV_PASSES_APPLY), a_ak, v_s)
    tx = each(lambda t, x, y: _mm(t, jnp.concatenate([x, y], axis=1), NN, WKV_PASSES_APPLY),
              tinv, a_t, av)
    yield
    ra = each(lambda rt, s, x: rt + _mm(s, x[:, :LANES], NN, WKV_PASSES_APPLY), r_t, s_rb, tx)
    c2 = each(lambda sb, sk, x, vs: _mm(jnp.concatenate([sb, sk], axis=1),
                                        jnp.concatenate([x[:, LANES:], vs], axis=0),
                                        NN, WKV_PASSES_APPLY), s_rb, s_rk, tx, v_s)
    yield
    tb = each(lambda x, bd: _mm(x.T, bd, NN, WKV_PASSES_APPLY), tx, b_d)
    c3 = each(lambda x, vs, kd: x[LANES:] + _mm(vs.T, kd, NN, WKV_PASSES_APPLY), tb, v_s, k_d)
    return [(ra[i], c2[i], jnp.exp(cum_last[i]), tb[i][:LANES], c3[i]) for i in range(nc)]


def _mm_exact_rhs_left(tri_bf16, x):
    x0 = x.astype(BF16)
    r1 = x - x0.astype(F32)
    x1 = r1.astype(BF16)
    x2 = (r1 - x1.astype(F32)).astype(BF16)
    return _dot(tri_bf16, x0) + (_dot(tri_bf16, x1) + _dot(tri_bf16, x2))


def _wkv_masks(L):
    n = 2 * L
    row = lax.broadcasted_iota(jnp.int32, (n, n), 0)
    col = lax.broadcasted_iota(jnp.int32, (n, n), 1)
    strict = row > col
    incl = row >= col
    levels = []
    m = 1
    while m < L:
        same = (row // (2 * m)) == (col // (2 * m))
        lvl = same & ((row % (2 * m)) >= m) & ((col % (2 * m)) < m)
        levels.append((lvl, row == col))
        m *= 2
    return strict, incl, levels


def _rwkv_kernel(has_vres, *refs):
    if has_vres:
        (z_ref, vf_ref, mu_ref, vec_ref, w2a2_ref, g2v2_ref, bd_ref,
         y_ref, carry, state) = refs
    else:
        (z_ref, mu_ref, vec_ref, w2a2_ref, g2v2_ref, bd_ref,
         y_ref, vout_ref, carry, state) = refs
    c = pl.program_id(1)
    n_seq, seq_rows = z_ref.shape[0], z_ref.shape[1]
    rows = n_seq * seq_rows
    L = WKV_CHUNK
    W = RWKV_WIDTH

    @pl.when(c == 0)
    def _():
        carry[...] = jnp.zeros(carry.shape, F32)
        state[...] = jnp.zeros(state.shape, F32)

    flat = lambda ref: jnp.concatenate([ref[s] for s in range(n_seq)], axis=0)
    z = flat(z_ref)
    row = lax.broadcasted_iota(jnp.int32, z.shape, 0)
    prev = pltpu.roll(z, 1, axis=0)
    for s in range(n_seq):
        prev = jnp.where(row == s * seq_rows, carry[s, SUBLANES - 1:SUBLANES, :], prev)
        carry[s] = z[(s + 1) * seq_rows - SUBLANES:(s + 1) * seq_rows, :]
    zs_all = z + mu_ref[...] * (prev - z)
    if has_vres:
        vf_all = flat(vf_ref)
    vec = vec_ref[...]
    w0, a0, k_k, k_a, r_k, ln_g, ln_b, v0 = (vec[j:j + 1] for j in range(8))
    bd = bd_ref[...]
    masks = _wkv_masks(L)
    rowt = lax.broadcasted_iota(jnp.int32, (L, L), 0)
    colt = lax.broadcasted_iota(jnp.int32, (L, L), 1)
    tri = jnp.where(rowt >= colt, 1.0, 0.0).astype(BF16)
    n_pairs = RWKV_HEADS // 2
    S_now = [{(s, p): state[s * n_pairs + p] for s in range(n_seq) for p in range(n_pairs)}]

    def group(r0, r1):
        zs = zs_all[r0:r1]
        r = zs[:, 0:W]
        k = zs[:, W:2 * W]
        v = zs[:, 2 * W:3 * W]
        wa = zs[:, 3 * W:3 * W + LANES]
        gb = zs[:, 3 * W + LANES:ZR_COLS]
        lane_a = lax.broadcasted_iota(jnp.int32, wa.shape, 1)
        t1 = _dot(jnp.where(lane_a < RWKV_DECAY_LORA, jnp.tanh(wa), wa).astype(BF16), w2a2_ref[...])
        lane_g = lax.broadcasted_iota(jnp.int32, gb.shape, 1)
        t2 = _dot(jnp.where(lane_g < RWKV_GATE_LORA, _sigmoid(gb), gb).astype(BF16), g2v2_ref[...])
        yield
        xw = w0 + t1[:, :W]
        w_log = -(jnp.maximum(-xw, 0.0) + jnp.log(1.0 + jnp.exp(-jnp.abs(xw)))) - 0.5
        lw = -jnp.exp(w_log)
        a = _sigmoid(a0 + t1[:, W:])
        g = t2[:, :W]
        if has_vres:
            v = v + (vf_all[r0:r1] - v) * _sigmoid(v0 + t2[:, W:])
        else:
            for s in range(n_seq):
                vout_ref[s] = v[s * seq_rows:(s + 1) * seq_rows]
        kk = k * k_k
        kk = kk * jnp.minimum(lax.rsqrt(_mm_exact_rhs(kk * kk, bd)), 1e12)
        k = k * (1.0 + (a - 1.0) * k_a)
        yield
        n_chunks = seq_rows // L
        idx = [(s, ch, p) for s in range(n_seq) for ch in range(n_chunks) for p in range(n_pairs)]
        cut = lambda x: [x[s * seq_rows + ch * L:s * seq_rows + (ch + 1) * L, p * LANES:(p + 1) * LANES]
                         for s, ch, p in idx]
        res = yield from _wkv_prep(cut(r), cut(lw), cut(k), cut(v), cut(kk), cut(a), tri, masks)
        prep = dict(zip(idx, res))
        yield
        sp = [(s, p) for s in range(n_seq) for p in range(n_pairs)]
        S = S_now[0]
        y_blk = {}
        for ch in range(n_chunks):
            y_s = {(s, p): _mm(prep[s, ch, p][0], S[s, p], NT, WKV_PASSES_STATE) + prep[s, ch, p][1]
                   for s, p in sp}
            for s in range(n_seq):
                y_blk[s, ch] = jnp.concatenate([y_s[s, p][:L] + y_s[s, p][L:] for p in range(n_pairs)], axis=1)
            S = {(s, p): S[s, p] * prep[s, ch, p][2] + _mm(S[s, p], prep[s, ch, p][3], NN, WKV_PASSES_STATE)
                 + prep[s, ch, p][4] for s, p in sp}
        S_now[0] = S
        y = jnp.concatenate([y_blk[s, ch] for s in range(n_seq) for ch in range(n_chunks)], axis=0)
        yield
        inv = 1.0 / HEAD_DIM
        mean = _mm_exact_rhs(y, bd) * inv
        yc = y - mean
        var = _mm_exact_rhs(yc * yc, bd) * inv
        yn = yc * lax.rsqrt(var + RWKV_LNX_EPS) * ln_g + ln_b
        bonus = _mm_exact_rhs(r * k * r_k, bd) * v
        out = ((yn + bonus) * g).astype(BF16)
        for s in range(n_seq):
            y_ref[s] = out[s * seq_rows:(s + 1) * seq_rows]

    for _ in group(0, rows):
        pass
    for s in range(n_seq):
        for p in range(n_pairs):
            state[s * n_pairs + p] = S_now[0][s, p]


def _rwkv(zr, v_first, mu, vec8, w2a2, g2v2, bd64):
    B, T, _ = zr.shape
    L = WKV_CHUNK * WKV_STEP_CHUNKS
    ns = WKV_STEP_SEQS
    has_vres = v_first is not None
    const = lambda shape: pl.BlockSpec(shape, lambda b, c: tuple(0 for _ in shape))
    tok = lambda w: pl.BlockSpec((ns, L, w), lambda b, c: (b, c, 0))
    in_specs = [tok(ZR_COLS)]
    args = [zr]
    if has_vres:
        in_specs.append(tok(RWKV_WIDTH))
        args.append(v_first)
    in_specs += [const((1, ZR_COLS)), const((8, RWKV_WIDTH)), const((LANES, 2 * RWKV_WIDTH)),
                 const((2 * LANES, 2 * RWKV_WIDTH)), const((RWKV_WIDTH, RWKV_WIDTH))]
    args += [mu, vec8, w2a2, g2v2, bd64]
    out_specs = [tok(RWKV_WIDTH)]
    out_shape = [jax.ShapeDtypeStruct((B, T, RWKV_WIDTH), BF16)]
    if not has_vres:
        out_specs.append(tok(RWKV_WIDTH))
        out_shape.append(jax.ShapeDtypeStruct((B, T, RWKV_WIDTH), F32))
    outs = pl.pallas_call(
        functools.partial(_rwkv_kernel, has_vres),
        grid=(B // ns, T // L),
        in_specs=in_specs, out_specs=out_specs, out_shape=out_shape,
        scratch_shapes=[pltpu.VMEM((ns, SUBLANES, ZR_COLS), F32),
                        pltpu.VMEM((ns * (RWKV_HEADS // 2), LANES, LANES), F32)],
        compiler_params=_cparams(("arbitrary", "arbitrary")),
    )(*args)
    return (outs[0], v_first) if has_vres else (outs[0], outs[1])


def _attn_step(q_ref, k_ref, v_ref, m_sc, l_sc, acc_sc, masked):
    tq = q_ref.shape[1]
    tk = k_ref.shape[1]
    lane = lax.broadcasted_iota(jnp.int32, (tq, LANES), 1)
    if masked:
        rowi = lax.broadcasted_iota(jnp.int32, (tq, tk), 0)
        coli = lax.broadcasted_iota(jnp.int32, (tq, tk), 1)
        keep = coli <= rowi
    heads = range(MLA_HEADS)
    s = [_dot(q_ref[0, :, hd * LANES:(hd + 1) * LANES], k_ref[0, :, hd * LANES:(hd + 1) * LANES], NT)
         for hd in heads]
    if masked:
        s = [jnp.where(keep, x, NEG_INF) for x in s]
    m_prev = [m_sc[hd] for hd in heads]
    m_new = [jnp.maximum(m_prev[hd], jnp.max(s[hd], axis=-1, keepdims=True)) for hd in heads]
    alpha = [jnp.exp(m_prev[hd] - m_new[hd]) for hd in heads]
    p = [jnp.exp(s[hd] - jnp.concatenate([m_new[hd]] * (tk // LANES), axis=1)) for hd in heads]
    for hd in heads:
        l_sc[hd] = alpha[hd] * l_sc[hd] + jnp.sum(p[hd], axis=-1, keepdims=True)
        m_sc[hd] = m_new[hd]
    pv = [_dot(p[hd].astype(BF16), v_ref[0, :, (hd // 2) * LANES:(hd // 2 + 1) * LANES]) for hd in heads]
    first = lane < HEAD_DIM
    for pr in range(MLA_HEADS // 2):
        acc_sc[pr] = (acc_sc[pr] * jnp.where(first, alpha[2 * pr], alpha[2 * pr + 1])
                      + jnp.where(first, pv[2 * pr], pv[2 * pr + 1]))


def _attn_step_bounded(q_ref, k_ref, v_ref, cb_ref, l_sc, acc_sc, masked):
    tq = q_ref.shape[1]
    tk = k_ref.shape[1]
    lane = lax.broadcasted_iota(jnp.int32, (tq, LANES), 1)
    if masked:
        rowi = lax.broadcasted_iota(jnp.int32, (tq, tk), 0)
        coli = lax.broadcasted_iota(jnp.int32, (tq, tk), 1)
        keep = coli <= rowi
    pv = []
    for hd in range(MLA_HEADS):
        s = _dot(q_ref[0, :, hd * LANES:(hd + 1) * LANES], k_ref[0, :, hd * LANES:(hd + 1) * LANES], NT)
        c = cb_ref[0, hd:hd + 1, :]
        p = jnp.exp(s - jnp.concatenate([c] * (tk // LANES), axis=1))
        if masked:
            p = jnp.where(keep, p, 0.0)
        part = p[:, 0:LANES]
        for t in range(1, tk // LANES):
            part = part + p[:, t * LANES:(t + 1) * LANES]
        l_sc[hd] = l_sc[hd] + part
        pv.append(_dot(p.astype(BF16), v_ref[0, :, (hd // 2) * LANES:(hd // 2 + 1) * LANES]))
    first = lane < HEAD_DIM
    for pr in range(MLA_HEADS // 2):
        acc_sc[pr] = acc_sc[pr] + jnp.where(first, pv[2 * pr], pv[2 * pr + 1])


def _attn_finish(o_ref, l_sc, acc_sc, lane_partial):
    tq = o_ref.shape[1]
    lane = lax.broadcasted_iota(jnp.int32, (tq, LANES), 1)
    outs = []
    for pr in range(MLA_HEADS // 2):
        la, lb = l_sc[2 * pr], l_sc[2 * pr + 1]
        if lane_partial:
            la = jnp.sum(la, axis=-1, keepdims=True)
            lb = jnp.sum(lb, axis=-1, keepdims=True)
        outs.append(acc_sc[pr] / jnp.where(lane < HEAD_DIM, la, lb))
    o_ref[0] = jnp.concatenate(outs, axis=1).astype(BF16)


def _attn_kernel(ok_ref, qi_ref, kj_ref, q_ref, k_ref, v_ref, cb_ref, o_ref, m_sc, l_sc, acc_sc):
    i = qi_ref[pl.program_id(1)]
    j = kj_ref[pl.program_id(1)]
    bounded = ok_ref[pl.program_id(0)] == 1
    exact = jnp.logical_not(bounded)

    @pl.when(j == 0)
    def _():
        m_sc[...] = jnp.full(m_sc.shape, NEG_INF, F32)
        l_sc[...] = jnp.zeros(l_sc.shape, F32)
        acc_sc[...] = jnp.zeros(acc_sc.shape, F32)

    @pl.when(bounded & (j < i))
    def _():
        _attn_step_bounded(q_ref, k_ref, v_ref, cb_ref, l_sc, acc_sc, masked=False)

    @pl.when(bounded & (j == i))
    def _():
        _attn_step_bounded(q_ref, k_ref, v_ref, cb_ref, l_sc, acc_sc, masked=True)
        _attn_finish(o_ref, l_sc, acc_sc, lane_partial=True)

    @pl.when(exact & (j < i))
    def _():
        _attn_step(q_ref, k_ref, v_ref, m_sc, l_sc, acc_sc, masked=False)

    @pl.when(exact & (j == i))
    def _():
        _attn_step(q_ref, k_ref, v_ref, m_sc, l_sc, acc_sc, masked=True)
        _attn_finish(o_ref, l_sc, acc_sc, lane_partial=False)


def _attention(q, k, v, qk_gain):
    B, T, _ = q.shape
    nq = T // TQ
    gmax = jnp.max(jnp.abs(qk_gain), axis=1)
    c = gmax[0] * gmax[1] * (MLA_QK_DIM ** 0.5) * ATTN_BOUND_SLACK
    ok = jnp.broadcast_to((c <= ATTN_BOUND_MAX).astype(jnp.int32), (B,))
    cb = jnp.broadcast_to(c, (B, MLA_HEADS, LANES))
    pairs = [(i, j) for i in range(nq) for j in range(i + 1)]
    qi = jnp.asarray([p[0] for p in pairs], jnp.int32)
    kj = jnp.asarray([p[1] for p in pairs], jnp.int32)
    grid_spec = pltpu.PrefetchScalarGridSpec(
        num_scalar_prefetch=3,
        grid=(B, len(pairs)),
        in_specs=[pl.BlockSpec((1, TQ, MLA_HEADS * LANES), lambda b, t, ok, qi, kj: (b, qi[t], 0)),
                  pl.BlockSpec((1, TQ, MLA_HEADS * LANES), lambda b, t, ok, qi, kj: (b, kj[t], 0)),
                  pl.BlockSpec((1, TQ, MLA_HEADS * HEAD_DIM), lambda b, t, ok, qi, kj: (b, kj[t], 0)),
                  pl.BlockSpec((1, MLA_HEADS, LANES), lambda b, t, ok, qi, kj: (b, 0, 0))],
        out_specs=pl.BlockSpec((1, TQ, MLA_HEADS * HEAD_DIM), lambda b, t, ok, qi, kj: (b, qi[t], 0)),
        scratch_shapes=[pltpu.VMEM((MLA_HEADS, TQ, LANES), F32),
                        pltpu.VMEM((MLA_HEADS, TQ, LANES), F32),
                        pltpu.VMEM((MLA_HEADS // 2, TQ, LANES), F32)])
    return pl.pallas_call(
        _attn_kernel,
        grid_spec=grid_spec,
        out_shape=jax.ShapeDtypeStruct((B, T, MLA_HEADS * HEAD_DIM), BF16),
        compiler_params=_cparams(("arbitrary", "arbitrary")),
    )(ok, qi, kj, q, k, v, cb)


def _mixout_kernel(has_router, *refs):
    if has_router:
        (x_ref, yr_ref, yp_ref, ym_ref, wo_ref, mod_ref, gain_ref, rt_ref, tri_ref,
         xo_ref, h_ref, route_ref, cnt_ref, cnt_sc) = refs
    else:
        x_ref, yr_ref, yp_ref, ym_ref, wo_ref, mod_ref, gain_ref, xo_ref, h_ref = refs
    mod = mod_ref[0]
    o1 = RWKV_WIDTH
    o2 = RWKV_WIDTH + POOL_WIDTH
    mix = (_dot(yr_ref[0], wo_ref[0:o1, :]) + _dot(yp_ref[0], wo_ref[o1:o2, :])
           + _dot(ym_ref[0], wo_ref[o2:, :]))
    x = x_ref[0] + mod[2:3] * mix
    xo_ref[0] = x
    h = _rms(x) * gain_ref[...] * (1.0 + mod[4:5]) + mod[3:4]
    if not has_router:
        h_ref[0] = h.astype(BF16)
    else:
        hp = _pack_bf16_pairs(h)
        for ck in range(ROW_CHUNKS):
            h_ref[ck, 0] = hp[:, ck * PACK_CHUNK_W:(ck + 1) * PACK_CHUNK_W]
        logits = _dot(h.astype(BF16), rt_ref[...])
        lane = lax.broadcasted_iota(jnp.int32, logits.shape, 1).astype(F32)
        lg = jnp.where(lane < N_EXPERTS, logits, -jnp.inf)
        m1 = jnp.max(lg, axis=-1, keepdims=True)
        i1 = jnp.min(jnp.where(lg == m1, lane, float(LANES)), axis=-1, keepdims=True)
        lg2 = jnp.where(lane == i1, -jnp.inf, lg)
        m2 = jnp.max(lg2, axis=-1, keepdims=True)
        i2 = jnp.min(jnp.where(lg2 == m2, lane, float(LANES)), axis=-1, keepdims=True)
        e2 = jnp.exp(m2 - m1)
        g1 = 1.0 / (1.0 + e2)
        g2 = e2 / (1.0 + e2)
        first = (pl.program_id(0) == 0) & (pl.program_id(1) == 0)

        @pl.when(first)
        def _():
            cnt_sc[...] = jnp.zeros(cnt_sc.shape, F32)

        hit1 = lane == i1
        hit2 = lane == i2
        onehot = jnp.where(hit1 | hit2, 1.0, 0.0)
        prefix = _dot(tri_ref[...], onehot.astype(BF16)) + cnt_sc[0:1, :]
        r1 = jnp.sum(jnp.where(hit1, prefix, 0.0), axis=-1, keepdims=True)
        r2 = jnp.sum(jnp.where(hit2, prefix, 0.0), axis=-1, keepdims=True)
        cnt_sc[...] = cnt_sc[...] + jnp.sum(onehot, axis=0, keepdims=True)
        cnt_ref[...] = cnt_sc[...]
        vals = (i1, i2, g1, g2, r1, r2)
        route = jnp.zeros(logits.shape, F32)
        for pos, val in enumerate(vals):
            route = jnp.where(lane == pos, val, route)
        route_ref[0] = route


def _mixout(x, yr, yp, ym, wo, mod_l, gain, router_p):
    B, T, _ = x.shape
    tm = TM_MIX
    has_router = router_p is not None
    const = lambda shape: pl.BlockSpec(shape, lambda b, i: tuple(0 for _ in shape))
    tok = lambda w: pl.BlockSpec((1, tm, w), lambda b, i: (b, i, 0))
    in_specs = [tok(D_MODEL), tok(RWKV_WIDTH), tok(POOL_WIDTH), tok(MLA_HEADS * HEAD_DIM),
                const((D_MODEL, D_MODEL)), pl.BlockSpec((1, 6, D_MODEL), lambda b, i: (b, 0, 0)),
                const((1, D_MODEL))]
    args = [x, yr, yp, ym, wo, mod_l, gain]
    out_specs = [tok(D_MODEL), tok(D_MODEL)]
    out_shape = [jax.ShapeDtypeStruct((B, T, D_MODEL), F32), jax.ShapeDtypeStruct((B, T, D_MODEL), BF16)]
    scratch = []
    if has_router:
        out_specs[1] = pl.BlockSpec((ROW_CHUNKS, 1, tm, PACK_CHUNK_W), lambda b, i: (0, b, i, 0))
        out_shape[1] = jax.ShapeDtypeStruct((ROW_CHUNKS, B, T, PACK_CHUNK_W), jnp.uint32)
        ids = np.arange(tm)
        tri = jnp.asarray(ids[:, None] > ids[None, :], BF16)
        in_specs += [const((D_MODEL, LANES)), const((tm, tm))]
        args += [router_p, tri]
        out_specs += [tok(LANES), const((SUBLANES, LANES))]
        out_shape += [jax.ShapeDtypeStruct((B, T, LANES), F32),
                      jax.ShapeDtypeStruct((SUBLANES, LANES), F32)]
        scratch = [pltpu.VMEM((SUBLANES, LANES), F32)]
    return pl.pallas_call(
        functools.partial(_mixout_kernel, has_router),
        grid=(B, T // tm),
        in_specs=in_specs, out_specs=out_specs, out_shape=out_shape, scratch_shapes=scratch,
        compiler_params=_cparams(("arbitrary", "arbitrary")),
    )(*args)


def _ffn_kernel(h_ref, wg_ref, wu_ref, wo_ref, x_ref, mod_ref, o_ref, acc):
    j = pl.program_id(1)

    @pl.when(j == 0)
    def _():
        acc[...] = jnp.zeros(acc.shape, F32)

    h = h_ref[...]
    gg = _dot(h, wg_ref[...])
    uu = _dot(h, wu_ref[...])
    acc[...] += _dot((_silu(gg) * uu).astype(BF16), wo_ref[...])

    @pl.when(j == pl.num_programs(1) - 1)
    def _():
        o_ref[...] = x_ref[...] + mod_ref[0][5:6] * acc[...]


def _ffn(h2, w_in, w_out, x, mod_l):
    N = h2.shape[0]
    T = N // mod_l.shape[0]
    tm, tf = TM_FFN, TF_FFN
    nf = D_FF // tf
    per_b = T // tm
    return pl.pallas_call(
        _ffn_kernel,
        grid=(N // tm, nf),
        in_specs=[pl.BlockSpec((tm, D_MODEL), lambda i, j: (i, 0)),
                  pl.BlockSpec((D_MODEL, tf), lambda i, j: (0, j)),
                  pl.BlockSpec((D_MODEL, tf), lambda i, j: (0, j + nf)),
                  pl.BlockSpec((tf, D_MODEL), lambda i, j: (j, 0)),
                  pl.BlockSpec((tm, D_MODEL), lambda i, j: (i, 0)),
                  pl.BlockSpec((1, 6, D_MODEL), lambda i, j: (i // per_b, 0, 0))],
        out_specs=pl.BlockSpec((tm, D_MODEL), lambda i, j: (i, 0)),
        out_shape=jax.ShapeDtypeStruct((N, D_MODEL), F32),
        scratch_shapes=[pltpu.VMEM((tm, D_MODEL), F32)],
        compiler_params=_cparams(("arbitrary", "arbitrary")),
    )(h2, w_in, w_in, w_out, x, mod_l)


def _moe_kernel(be_ref, nv_ref, last_ref, x_ref, wg_ref, wu_ref, wo_ref, o_ref, acc, xm):
    i = pl.program_id(0)
    j = pl.program_id(1)

    @pl.when(i <= last_ref[0])
    def _():
        @pl.when(j == 0)
        def _():
            acc[...] = jnp.zeros(acc.shape, F32)
            row = lax.broadcasted_iota(jnp.int32, (xm.shape[0], 1), 0)
            xp = jnp.concatenate([x_ref[ck] for ck in range(ROW_CHUNKS)], axis=1)
            xp = jnp.where(row < nv_ref[i], xp, jnp.uint32(0))
            xm[...] = _unpack_bf16_pairs(xp)

        def swiglu_rows(n_rows):
            x = xm[0:n_rows, :]
            gg = _dot(x, wg_ref[0].astype(BF16))
            uu = _dot(x, wu_ref[0].astype(BF16))
            acc[0:n_rows, :] += _dot((_silu(gg) * uu).astype(BF16), wo_ref[0].astype(BF16))

        half = xm.shape[0] // 2

        @pl.when(nv_ref[i] > half)
        def _():
            swiglu_rows(xm.shape[0])

        @pl.when(nv_ref[i] <= half)
        def _():
            swiglu_rows(half)

        @pl.when(j == pl.num_programs(1) - 1)
        def _():
            yp = _pack_bf16_pairs(acc[...])
            for ck in range(ROW_CHUNKS):
                o_ref[ck] = yp[:, ck * PACK_CHUNK_W:(ck + 1) * PACK_CHUNK_W]


def _moe_experts(xs, w_in, w_out, block_exp, n_valid, last_blk):
    n_rows = xs.shape[1]
    tm, tf = MOE_BLOCK, TF_MOE
    nf = D_FF_EXPERT // tf
    blk = lambda i, last: jnp.minimum(i, last[0])
    chunk = lambda i, j, last: jnp.where(i <= last[0], j, nf - 1)
    grid_spec = pltpu.PrefetchScalarGridSpec(
        num_scalar_prefetch=3,
        grid=(n_rows // tm, nf),
        in_specs=[pl.BlockSpec((ROW_CHUNKS, tm, PACK_CHUNK_W),
                               lambda i, j, be, nv, last: (0, blk(i, last), 0)),
                  pl.BlockSpec((1, D_MODEL, tf),
                               lambda i, j, be, nv, last: (be[blk(i, last)], 0, chunk(i, j, last))),
                  pl.BlockSpec((1, D_MODEL, tf),
                               lambda i, j, be, nv, last: (be[blk(i, last)], 0, chunk(i, j, last) + nf)),
                  pl.BlockSpec((1, tf, D_MODEL),
                               lambda i, j, be, nv, last: (be[blk(i, last)], chunk(i, j, last), 0))],
        out_specs=pl.BlockSpec((ROW_CHUNKS, tm, PACK_CHUNK_W), lambda i, j, be, nv, last: (0, blk(i, last), 0)),
        scratch_shapes=[pltpu.VMEM((tm, D_MODEL), F32), pltpu.VMEM((tm, D_MODEL), BF16)])
    return pl.pallas_call(
        _moe_kernel,
        grid_spec=grid_spec,
        out_shape=jax.ShapeDtypeStruct((ROW_CHUNKS, n_rows, PACK_CHUNK_W), jnp.uint32),
        compiler_params=_cparams(("arbitrary", "arbitrary")),
    )(block_exp, n_valid, last_blk, xs, w_in, w_in, w_out)


def _sc_mesh():
    return plsc.VectorSubcoreMesh(core_axis_name="c", subcore_axis_name="s")


def _sc_scatter_rows(x, dest, n_rows):
    N, D = x.shape
    K = dest.shape[0]
    win = SC_WINDOW

    @pl.kernel(out_type=jax.ShapeDtypeStruct((n_rows, D), x.dtype), mesh=_sc_mesh(), scratch_types=[])
    def scatter(x_hbm, d_hbm, o_hbm):
        def body(x_vmem, *idx_vmem):
            for iv in idx_vmem:
                pltpu.sync_copy(x_vmem, o_hbm.at[iv.at[0]])

        pltpu.emit_pipeline(
            body,
            grid=(N // win,),
            in_specs=[pl.BlockSpec((win, D), lambda i: (i, 0))]
            + [pl.BlockSpec((1, win), functools.partial(lambda k, i: (k, i), k)) for k in range(K)],
            out_specs=[],
            core_axis_name=("c", "s"),
            dimension_semantics=(pltpu.PARALLEL,),
        )(x_hbm, *([d_hbm] * K))

    return scatter(x, dest)


def _sc_gather_rows(x, idx):
    n = idx.shape[0]
    D = x.shape[1]
    win = SC_WINDOW

    @pl.kernel(out_type=jax.ShapeDtypeStruct((n, D), x.dtype), mesh=_sc_mesh(), scratch_types=[])
    def gather(x_hbm, i_hbm, o_hbm):
        def body(i_vmem, o_vmem):
            pltpu.sync_copy(x_hbm.at[i_vmem.at[0]], o_vmem)

        pltpu.emit_pipeline(
            body,
            grid=(n // win,),
            in_specs=[pl.BlockSpec((1, win), lambda i: (0, i))],
            out_specs=[pl.BlockSpec((win, D), lambda i: (i, 0))],
            core_axis_name=("c", "s"),
            dimension_semantics=(pltpu.PARALLEL,),
        )(i_hbm, o_hbm)

    return gather(x, idx.reshape(1, n))


def _combine_kernel(x_ref, ya_ref, yb_ref, route_ref, mod_ref, o_ref):
    rt = route_ref[...]
    ya = _unpack_bf16_pairs(jnp.concatenate([ya_ref[0, ck] for ck in range(ROW_CHUNKS)], axis=1))
    yb = _unpack_bf16_pairs(jnp.concatenate([yb_ref[0, ck] for ck in range(ROW_CHUNKS)], axis=1))
    f = rt[:, 2:3] * ya.astype(F32) + rt[:, 3:4] * yb.astype(F32)
    o_ref[...] = x_ref[...] + mod_ref[0][5:6] * f


def _combine(x, y2, route, mod_l):
    N = x.shape[0]
    T = N // mod_l.shape[0]
    tm = 1024
    per_b = T // tm
    tok = pl.BlockSpec((tm, D_MODEL), lambda i: (i, 0))
    slot = lambda k: pl.BlockSpec((1, ROW_CHUNKS, tm, PACK_CHUNK_W), lambda i: (k, 0, i, 0))
    return pl.pallas_call(
        _combine_kernel,
        grid=(N // tm,),
        in_specs=[tok, slot(0), slot(1),
                  pl.BlockSpec((tm, LANES), lambda i: (i, 0)),
                  pl.BlockSpec((1, 6, D_MODEL), lambda i: (i // per_b, 0, 0))],
        out_specs=tok,
        out_shape=jax.ShapeDtypeStruct((N, D_MODEL), F32),
        compiler_params=_cparams(("arbitrary",)),
    )(x, y2, y2, route, mod_l)


def _moe(h2, route, counts, w_in, w_out, x, mod_l):
    N = x.shape[0]
    blk = MOE_BLOCK
    cnt = counts[0, :N_EXPERTS].astype(jnp.int32)
    padded = (cnt + blk - 1) // blk * blk
    pend = jnp.cumsum(padded)
    pstart = pend - padded
    e = route[:, 0:TOP_K].astype(jnp.int32)
    rank = route[:, 2 * TOP_K:3 * TOP_K].astype(jnp.int32)
    dest = (jnp.take(pstart, e) + rank).T
    n_blocks = N * TOP_K // blk + N_EXPERTS
    bstart = jnp.arange(n_blocks, dtype=jnp.int32) * blk
    block_exp = jnp.minimum(jnp.sum((bstart[:, None] >= pend[None, :]).astype(jnp.int32), axis=1),
                            N_EXPERTS - 1)
    n_valid = jnp.clip(cnt[block_exp] - (bstart - pstart[block_exp]), 0, blk).astype(jnp.int32)
    n_rows = n_blocks * blk
    dest_ck = dest[:, None, :] + (jnp.arange(ROW_CHUNKS, dtype=jnp.int32) * n_rows)[None, :, None]
    xs = _sc_scatter_rows(h2, dest_ck.reshape(TOP_K, ROW_CHUNKS * N), ROW_CHUNKS * n_rows)
    last_blk = (pend[-1:] // blk - 1).astype(jnp.int32)
    yb = _moe_experts(xs.reshape(ROW_CHUNKS, n_rows, PACK_CHUNK_W), w_in, w_out, block_exp, n_valid,
                      last_blk)
    y2 = _sc_gather_rows(yb.reshape(ROW_CHUNKS * n_rows, PACK_CHUNK_W), dest_ck.reshape(-1))
    return _combine(x, y2.reshape(TOP_K, ROW_CHUNKS, N, PACK_CHUNK_W), route, mod_l)


def _layout_w_in(w, has_vres):
    W = RWKV_WIDTH
    off_gd = 3 * W + RWKV_DECAY_LORA + RWKV_ICLR_LORA
    off_pool = off_gd + RWKV_GATE_LORA
    off_q = off_pool + POOL_WIDTH
    off_kv = off_q + MLA_Q_LORA
    off_kr = off_kv + MLA_KV_LORA
    n_base = off_kr + MLA_QK_ROPE
    d = w.shape[0]
    zeros = lambda n: jnp.zeros((d, n), w.dtype)
    vd = w[:, n_base:n_base + RWKV_VRES_LORA] if has_vres else zeros(RWKV_VRES_LORA)
    cols = [w[:, :off_gd], w[:, off_gd:off_pool], vd, zeros(ZR_COLS - off_pool - RWKV_VRES_LORA),
            w[:, off_pool:off_q], w[:, off_q:off_kv], w[:, off_kv:off_kr],
            zeros(MLA_QK_NOPE), w[:, off_kr:n_base], zeros(LANES - MLA_QK_DIM)]
    return jnp.concatenate(cols, axis=1).astype(BF16)


def _pad_heads(w, per_head, keep_from, keep_n):
    K = w.shape[0]
    wh = w.reshape(K, MLA_HEADS, per_head)[:, :, keep_from:keep_from + keep_n]
    wh = jnp.pad(wh, ((0, 0), (0, 0), (0, LANES - keep_n)))
    return wh.reshape(K, MLA_HEADS * LANES)


def kernel(x, c, positions, w_ada, b_ada, norm_gain, w_in_first, w_in_rest, mu_shift, mu_shift_v,
           rwkv_vec, rwkv_v0, rwkv_w2, rwkv_a2, rwkv_g2, rwkv_v2, pool_w, pool_scale,
           mla_q_lat_gain, mla_kv_lat_gain, mla_wq_up, mla_wkv_up, mla_qk_gain, w_out, ffn_w_in,
           ffn_w_out, moe_router, moe_w_in, moe_w_out):
    B, T, D = x.shape
    depth = w_ada.shape[0]
    W = RWKV_WIDTH
    mod = _adaln(c, w_ada, b_ada).reshape(depth, B, 6, D)
    pos3 = positions.reshape(B, T, 1)
    inv_freq = ROPE_BASE ** (-jnp.arange(0, MLA_QK_ROPE, 2, dtype=F32) / MLA_QK_ROPE)
    freq = jnp.concatenate([jnp.zeros((MLA_QK_NOPE,), F32), inv_freq, inv_freq,
                            jnp.zeros((LANES - MLA_QK_DIM,), F32)]).reshape(1, LANES)
    cosf, sinf = _rope_tables(pos3, freq)
    hid = np.arange(W) // HEAD_DIM
    bd64 = jnp.asarray(hid[:, None] == hid[None, :], BF16)
    bid = np.arange(MLA_HEADS * LANES) // LANES
    bd128 = jnp.asarray(bid[:, None] == bid[None, :], BF16)

    v_first = None
    for l in range(depth):
        has_vres = l > 0
        mod_l = mod[l]
        win = _layout_w_in(w_in_first if l == 0 else w_in_rest[l - 1], has_vres)
        poolw = jax.scipy.linalg.block_diag(*[pool_w[l, g] for g in range(len(POOL_WINDOWS))]).astype(BF16)
        wq = _pad_heads(mla_wq_up[l], MLA_QK_DIM, 0, MLA_QK_DIM).astype(BF16)
        wk = _pad_heads(mla_wkv_up[l], MLA_QK_NOPE + HEAD_DIM, 0, MLA_QK_NOPE).astype(BF16)
        wv = mla_wkv_up[l].reshape(MLA_KV_LORA, MLA_HEADS, MLA_QK_NOPE + HEAD_DIM)[:, :, MLA_QK_NOPE:]
        wv = wv.reshape(MLA_KV_LORA, MLA_HEADS * HEAD_DIM).astype(BF16)
        qkg = jnp.tile(jnp.pad(mla_qk_gain[l], ((0, 0), (0, LANES - MLA_QK_DIM))), (1, MLA_HEADS))
        zr, y_pool, q, k, v = _mixin(
            x, mod_l, norm_gain[l, 0].reshape(1, D), win, cosf, sinf, poolw,
            pool_scale[l].reshape(1, -1), mla_q_lat_gain[l].reshape(1, -1),
            mla_kv_lat_gain[l].reshape(1, -1), wq, wk, wv, qkg, bd128)

        pad_mu = ZR_COLS - mu_shift.shape[1] - RWKV_VRES_LORA
        mu_v = mu_shift_v[l - 1] if has_vres else jnp.zeros((RWKV_VRES_LORA,), F32)
        mu = jnp.concatenate([mu_shift[l], mu_v, jnp.zeros((pad_mu,), F32)]).reshape(1, ZR_COLS)
        v0 = rwkv_v0[l - 1] if has_vres else jnp.zeros((W,), F32)
        vec8 = jnp.concatenate([rwkv_vec[l], v0[None]], axis=0)
        w2a2 = jax.scipy.linalg.block_diag(rwkv_w2[l], rwkv_a2[l]).astype(BF16)
        g2 = jnp.pad(rwkv_g2[l], ((0, 2 * LANES - RWKV_GATE_LORA), (0, 0)))
        if has_vres:
            v2 = jnp.pad(rwkv_v2[l - 1], ((RWKV_GATE_LORA, 2 * LANES - RWKV_GATE_LORA - RWKV_VRES_LORA), (0, 0)))
        else:
            v2 = jnp.zeros((2 * LANES, W), F32)
        g2v2 = jnp.concatenate([g2, v2], axis=1).astype(BF16)
        y_rwkv, v_first = _rwkv(zr, v_first, mu, vec8, w2a2, g2v2, bd64)

        y_mla = _attention(q, k, v, mla_qk_gain[l])

        is_moe = (l % 2 == 1)
        router_p = None
        if is_moe:
            router_p = jnp.pad(moe_router[l // 2], ((0, 0), (0, LANES - N_EXPERTS))).astype(BF16)
        outs = _mixout(x, y_rwkv, y_pool, y_mla, w_out[l].astype(BF16), mod_l,
                       norm_gain[l, 1].reshape(1, D), router_p)
        x_mid, h2 = outs[0], outs[1]
        xf = x_mid.reshape(B * T, D)
        if is_moe:
            xo = _moe(h2.reshape(ROW_CHUNKS * B * T, PACK_CHUNK_W), outs[2].reshape(B * T, LANES), outs[3],
                      moe_w_in[l // 2], moe_w_out[l // 2], xf, mod_l)
        else:
            xo = _ffn(h2.reshape(B * T, D), ffn_w_in[l // 2].astype(BF16), ffn_w_out[l // 2].astype(BF16),
                      xf, mod_l)
        x = xo.reshape(B, T, D)
    return x
```

```python
import functools

import numpy as np
import jax
import jax.numpy as jnp
from jax import lax
from jax.experimental import pallas as pl
from jax.experimental.pallas import tpu as pltpu
from jax.experimental.pallas import tpu_sc as plsc

F32 = jnp.float32
BF16 = jnp.bfloat16

D_MODEL = 1024
HEAD_DIM = 64
RWKV_WIDTH = 512
RWKV_HEADS = 8
POOL_WIDTH = 256
POOL_WINDOWS = (2, 4, 8, 16)
POOL_HALO = 16
MLA_HEADS = 4
MLA_QK_NOPE = 64
MLA_QK_ROPE = 32
MLA_QK_DIM = 96
MLA_Q_LORA = 256
MLA_KV_LORA = 128
ROPE_BASE = 10000.0
RWKV_DECAY_LORA = 64
RWKV_ICLR_LORA = 64
RWKV_VRES_LORA = 32
RWKV_GATE_LORA = 160
RWKV_LNX_EPS = 64e-5
D_FF = 2816
N_EXPERTS = 8
TOP_K = 2
D_FF_EXPERT = 3584
NORM_EPS = 1e-6
NEG_INF = -1e30

LANES = 128
SUBLANES = 8
VMEM_LIMIT = 56 * 1024 * 1024

ZR_COLS = 1920
Z_POOL_OFF = ZR_COLS
Z_QLAT_OFF = Z_POOL_OFF + POOL_WIDTH
Z_KVLAT_OFF = Z_QLAT_OFF + MLA_Q_LORA
Z_KROPE_OFF = Z_KVLAT_OFF + MLA_KV_LORA
Z_COLS = Z_KROPE_OFF + LANES

MIXIN_PIECE_W = 512
TM_MIX = 512
WKV_CHUNK = 64
TQ = 512
ATTN_BOUND_SLACK = 1.02
ATTN_BOUND_MAX = 40.0
TM_FFN = 1024
TF_FFN = 1408
MOE_BLOCK = 1024
TF_MOE = 512
SC_WINDOW = 128
ROW_CHUNKS = 4
ROW_CHUNK_W = D_MODEL // ROW_CHUNKS
PACK_CHUNK_W = ROW_CHUNK_W // 2

SEGSUM_SPLITS = 1

NN = (((1,), (0,)), ((), ()))
NT = (((1,), (1,)), ((), ()))


def _dot(a, b, dims=NN):
    return lax.dot_general(a, b, dims, preferred_element_type=F32)


def _split2(a):
    hi = a.astype(BF16)
    lo = (a - hi.astype(F32)).astype(BF16)
    return hi, lo


def _mm(a, b, dims=NN, passes=3):
    if passes == 1:
        return _dot(a.astype(BF16), b.astype(BF16), dims)
    ah, al = _split2(a)
    bh, bl = _split2(b)
    return _dot(ah, bh, dims) + (_dot(ah, bl, dims) + _dot(al, bh, dims))


def _mm_exact_rhs(a, b_bf16, splits=SEGSUM_SPLITS):
    m, w = a.shape
    nb = w // LANES
    stacked = jnp.concatenate([a[:, i * LANES:(i + 1) * LANES] for i in range(nb)], axis=0)
    out = None
    rem = stacked
    for s in range(splits):
        part = rem.astype(BF16)
        term = _dot(part, b_bf16)
        out = term if out is None else out + term
        if s + 1 < splits:
            rem = rem - part.astype(F32)
    return jnp.concatenate([out[i * m:(i + 1) * m] for i in range(nb)], axis=1)


def _pack_bf16_pairs(h):
    w = h.shape[1] // 2
    lo = lax.bitcast_convert_type(h[:, :w].astype(BF16).astype(F32), jnp.uint32)
    hi = lax.bitcast_convert_type(h[:, w:].astype(BF16).astype(F32), jnp.uint32)
    return (lo >> 16) | (hi & jnp.uint32(0xFFFF0000))


def _unpack_bf16_pairs(p):
    lo = lax.bitcast_convert_type(p << 16, F32)
    hi = lax.bitcast_convert_type(p & jnp.uint32(0xFFFF0000), F32)
    return jnp.concatenate([lo, hi], axis=1).astype(BF16)


def _sigmoid(x):
    return 1.0 / (1.0 + jnp.exp(-x))


def _silu(x):
    return x * _sigmoid(x)


def _rms(x, eps=NORM_EPS):
    return x * lax.rsqrt(jnp.mean(x * x, axis=-1, keepdims=True) + eps)


def _cparams(sem):
    return pltpu.CompilerParams(dimension_semantics=sem, vmem_limit_bytes=VMEM_LIMIT)


def _adaln_kernel(c_ref, w_ref, b_ref, o_ref):
    ca = _silu(c_ref[...])
    o_ref[0] = _mm(ca, w_ref[0]) + b_ref[0]


def _adaln(c, w_ada, b_ada):
    L = w_ada.shape[0]
    B = c.shape[0]
    n = w_ada.shape[2] // D_MODEL
    return pl.pallas_call(
        _adaln_kernel,
        grid=(L, n),
        in_specs=[pl.BlockSpec((B, D_MODEL), lambda l, j: (0, 0)),
                  pl.BlockSpec((1, D_MODEL, D_MODEL), lambda l, j: (l, 0, j)),
                  pl.BlockSpec((1, 1, D_MODEL), lambda l, j: (l, 0, j))],
        out_specs=pl.BlockSpec((1, B, D_MODEL), lambda l, j: (l, 0, j)),
        out_shape=jax.ShapeDtypeStruct((L, B, n * D_MODEL), F32),
        compiler_params=_cparams(("arbitrary", "arbitrary")),
    )(c, w_ada, b_ada.reshape(L, 1, -1))


def _rope(x, cosf, sinf, lane):
    up = pltpu.roll(x, LANES - MLA_QK_ROPE // 2, axis=1)
    dn = pltpu.roll(x, MLA_QK_ROPE // 2, axis=1)
    rot = jnp.where(lane < MLA_QK_NOPE + MLA_QK_ROPE // 2, -up, dn)
    return x * cosf + rot * sinf


def _rope_kernel(pos_ref, freq_ref, cos_ref, sin_ref):
    tm = pos_ref.shape[1]
    lane = lax.broadcasted_iota(jnp.int32, (tm, LANES), 1)
    in_rope = (lane >= MLA_QK_NOPE) & (lane < MLA_QK_DIM)
    ang = pos_ref[0].astype(F32) * freq_ref[...]
    cos_ref[0] = jnp.where(in_rope, jnp.cos(ang), 1.0)
    sin_ref[0] = jnp.where(in_rope, jnp.sin(ang), 0.0)


def _rope_tables(pos3, freq):
    B, T, _ = pos3.shape
    tm = TM_MIX
    tok = lambda w: pl.BlockSpec((1, tm, w), lambda b, i: (b, i, 0))
    return pl.pallas_call(
        _rope_kernel,
        grid=(B, T // tm),
        in_specs=[tok(1), pl.BlockSpec((1, LANES), lambda b, i: (0, 0))],
        out_specs=[tok(LANES), tok(LANES)],
        out_shape=[jax.ShapeDtypeStruct((B, T, LANES), F32)] * 2,
        compiler_params=_cparams(("arbitrary", "arbitrary")),
    )(pos3, freq)


def _mixin_kernel(x_ref, mod_ref, gain_ref, win_ref, cos_ref, sin_ref, poolw_ref, pools_ref,
                  qg_ref, kvg_ref, wq_ref, wk_ref, wv_ref, qkg_ref, bd_ref,
                  zr_ref, yp_ref, q_ref, k_ref, v_ref, ubuf):
    i = pl.program_id(1)
    tm = x_ref.shape[1]
    x = x_ref[0]
    mod = mod_ref[0]
    @pl.when(i == 0)
    def _():
        ubuf[0:POOL_HALO, :] = jnp.zeros((POOL_HALO, POOL_WIDTH), F32)

    h = (_rms(x) * gain_ref[...] * (1.0 + mod[1:2]) + mod[0:1]).astype(BF16)
    zb = _dot(h, win_ref[:, ZR_COLS:])
    zcol = lambda off, w: zb[:, off - ZR_COLS:off - ZR_COLS + w]
    def project_piece(n):
        cols = slice(n * MIXIN_PIECE_W, min((n + 1) * MIXIN_PIECE_W, ZR_COLS))
        zr_ref[0, :, cols] = _dot(h, win_ref[:, cols])

    project_piece(0)
    u = zcol(Z_POOL_OFF, POOL_WIDTH)
    ubuf[POOL_HALO:, :] = u
    ue = ubuf[...]
    s2 = ue + pltpu.roll(ue, 1, axis=0)
    s4 = s2 + pltpu.roll(s2, 2, axis=0)
    s8 = s4 + pltpu.roll(s4, 4, axis=0)
    s16 = s8 + pltpu.roll(s8, 8, axis=0)
    ubuf[0:POOL_HALO, :] = u[tm - POOL_HALO:, :]
    lane_p = lax.broadcasted_iota(jnp.int32, (tm, POOL_WIDTH), 1)
    grp = lane_p // (POOL_WIDTH // len(POOL_WINDOWS))
    win_sum = jnp.where(grp == 0, s2[POOL_HALO:], jnp.where(grp == 1, s4[POOL_HALO:],
                        jnp.where(grp == 2, s8[POOL_HALO:], s16[POOL_HALO:])))
    win = jnp.where(grp == 0, 2, jnp.where(grp == 1, 4, jnp.where(grp == 2, 8, 16)))
    t_abs = i * tm + lax.broadcasted_iota(jnp.int32, (tm, POOL_WIDTH), 0)
    cnt = jnp.minimum(t_abs + 1, win).astype(F32)
    p = win_sum / cnt - u
    yp = _dot(p.astype(BF16), poolw_ref[...]) * pools_ref[...]
    yp_ref[0] = yp.astype(BF16)
    project_piece(1)

    lane = lax.broadcasted_iota(jnp.int32, (tm, LANES), 1)
    cosf = cos_ref[0]
    sinf = sin_ref[0]

    q_lat = zcol(Z_QLAT_OFF, MLA_Q_LORA)
    kv_lat = zcol(Z_KVLAT_OFF, MLA_KV_LORA)
    k_rope = zcol(Z_KROPE_OFF, LANES)
    qn = (_rms(q_lat) * qg_ref[...]).astype(BF16)
    kvn = (_rms(kv_lat) * kvg_ref[...]).astype(BF16)
    q = _dot(qn, wq_ref[...])
    kx = _dot(kvn, wk_ref[...])
    v = _dot(kvn, wv_ref[...])
    v_ref[0] = v.astype(BF16)
    project_piece(2)
    k_pe = _rope(k_rope, cosf, sinf, lane)
    qs, ks = [], []
    for hd in range(MLA_HEADS):
        sl = slice(hd * LANES, (hd + 1) * LANES)
        qs.append(_rope(q[:, sl], cosf, sinf, lane))
        ks.append(kx[:, sl] + k_pe)
    q = jnp.concatenate(qs, axis=1)
    k = jnp.concatenate(ks, axis=1)
    project_piece(3)
    qss = _mm_exact_rhs(q * q, bd_ref[...]) * (1.0 / MLA_QK_DIM)
    kss = _mm_exact_rhs(k * k, bd_ref[...]) * (1.0 / MLA_QK_DIM)
    qkg = qkg_ref[...]
    q = q * lax.rsqrt(qss + NORM_EPS) * qkg[0:1] * (MLA_QK_DIM ** -0.5)
    k = k * lax.rsqrt(kss + NORM_EPS) * qkg[1:2]
    q_ref[0] = q.astype(BF16)
    k_ref[0] = k.astype(BF16)


def _mixin(x, mod_l, gain, win, cosf, sinf, poolw, pools, qg, kvg, wq, wk, wv, qkg, bd128):
    B, T, _ = x.shape
    tm = TM_MIX
    const = lambda shape: pl.BlockSpec(shape, lambda b, i: tuple(0 for _ in shape))
    tok = lambda w: pl.BlockSpec((1, tm, w), lambda b, i: (b, i, 0))
    return pl.pallas_call(
        _mixin_kernel,
        grid=(B, T // tm),
        in_specs=[tok(D_MODEL),
                  pl.BlockSpec((1, 6, D_MODEL), lambda b, i: (b, 0, 0)),
                  const((1, D_MODEL)), const((D_MODEL, Z_COLS)),
                  tok(LANES), tok(LANES),
                  const((POOL_WIDTH, POOL_WIDTH)), const((1, POOL_WIDTH)),
                  const((1, MLA_Q_LORA)), const((1, MLA_KV_LORA)),
                  const((MLA_Q_LORA, MLA_HEADS * LANES)), const((MLA_KV_LORA, MLA_HEADS * LANES)),
                  const((MLA_KV_LORA, MLA_HEADS * HEAD_DIM)), const((2, MLA_HEADS * LANES)),
                  const((LANES, LANES))],
        out_specs=[tok(ZR_COLS), tok(POOL_WIDTH), tok(MLA_HEADS * LANES), tok(MLA_HEADS * LANES),
                   tok(MLA_HEADS * HEAD_DIM)],
        out_shape=[jax.ShapeDtypeStruct((B, T, ZR_COLS), F32),
                   jax.ShapeDtypeStruct((B, T, POOL_WIDTH), BF16),
                   jax.ShapeDtypeStruct((B, T, MLA_HEADS * LANES), BF16),
                   jax.ShapeDtypeStruct((B, T, MLA_HEADS * LANES), BF16),
                   jax.ShapeDtypeStruct((B, T, MLA_HEADS * HEAD_DIM), BF16)],
        scratch_shapes=[pltpu.VMEM((POOL_HALO + tm, POOL_WIDTH), F32)],
        compiler_params=_cparams(("arbitrary", "arbitrary")),
    )(x, mod_l, gain, win, cosf, sinf, poolw, pools, qg, kvg, wq, wk, wv, qkg, bd128)


WKV_PASSES_SCORE = 1
WKV_PASSES_INV = 1
WKV_PASSES_APPLY = 1
WKV_PASSES_STATE = 1
WKV_STEP_CHUNKS = 1
WKV_STEP_SEQS = 4


def _stack_heads(xp, lane):
    return jnp.concatenate([jnp.where(lane < HEAD_DIM, xp, 0.0),
                            jnp.where(lane >= HEAD_DIM, xp, 0.0)], axis=0)


def _wkv_prep(r, lw, k, v, kk, a, tri, masks):
    L = r[0].shape[0]
    n = 2 * L
    nc = len(r)
    each = lambda f, *ls: [f(*xs) for xs in zip(*ls)]
    lane = lax.broadcasted_iota(jnp.int32, (L, LANES), 1)
    stack = lambda x: _stack_heads(x, lane)
    cum = each(lambda x: _mm_exact_rhs_left(tri, x), lw)
    cum_last = each(lambda c: c[L - 1:L, :], cum)
    e_w = each(jnp.exp, cum)
    e_wm = each(lambda c, x: jnp.exp(c - x), cum, lw)
    e_iw = each(lambda c: jnp.exp(-c), cum)
    e_d = each(lambda cl, c: jnp.exp(cl - c), cum_last, cum)
    beta = each(lambda x, y: x * y, kk, a)
    r_t = each(lambda x, e: stack(x * e), r, e_w)
    a_t = each(lambda x, e: stack(-x * e), kk, e_wm)
    b_t = each(lambda x, e: stack(x * e), beta, e_iw)
    k_t = each(lambda x, e: stack(x * e), k, e_iw)
    b_d = each(lambda x, e: stack(x * e), beta, e_d)
    k_d = each(lambda x, e: stack(x * e), k, e_d)
    v_s = each(stack, v)
    yield
    g = each(lambda at, rt, bt, kt: _mm(jnp.concatenate([at, rt], axis=0),
                                        jnp.concatenate([bt, kt], axis=0), NT, WKV_PASSES_SCORE),
             a_t, r_t, b_t, k_t)
    strict, incl, levels = masks
    a_ab = each(lambda x: jnp.where(strict, x[:n, :n], 0.0), g)
    a_ak = each(lambda x: jnp.where(strict, x[:n, n:], 0.0), g)
    s_rb = each(lambda x: jnp.where(incl, x[n:, :n], 0.0), g)
    s_rk = each(lambda x: jnp.where(incl, x[n:, n:], 0.0), g)
    eye = jnp.where(levels[0][1], 1.0, 0.0)
    tinv = each(lambda x: eye + jnp.where(levels[0][0], x, 0.0), a_ab)
    yield
    for lvl_mask, _ in levels[1:]:
        et = each(lambda x, t: _mm(jnp.where(lvl_mask, x, 0.0), t, NN, WKV_PASSES_INV), a_ab, tinv)
        tinv = each(lambda t, x: t + _mm(t, x, NN, WKV_PASSES_INV), tinv, et)
        yield
    av = each(lambda x, y: _mm(x, y, NN, WKV_PASSES_APPLY), a_ak, v_s)
    tx = each(lambda t, x, y: _mm(t, jnp.concatenate([x, y], axis=1), NN, WKV_PASSES_APPLY),
              tinv, a_t, av)
    yield
    ra = each(lambda rt, s, x: rt + _mm(s, x[:, :LANES], NN, WKV_PASSES_APPLY), r_t, s_rb, tx)
    c2 = each(lambda sb, sk, x, vs: _mm(jnp.concatenate([sb, sk], axis=1),
                                        jnp.concatenate([x[:, LANES:], vs], axis=0),
                                        NN, WKV_PASSES_APPLY), s_rb, s_rk, tx, v_s)
    yield
    tb = each(lambda x, bd: _mm(x.T, bd, NN, WKV_PASSES_APPLY), tx, b_d)
    c3 = each(lambda x, vs, kd: x[LANES:] + _mm(vs.T, kd, NN, WKV_PASSES_APPLY), tb, v_s, k_d)
    return [(ra[i], c2[i], jnp.exp(cum_last[i]), tb[i][:LANES], c3[i]) for i in range(nc)]


def _mm_exact_rhs_left(tri_bf16, x):
    x0 = x.astype(BF16)
    r1 = x - x0.astype(F32)
    x1 = r1.astype(BF16)
    x2 = (r1 - x1.astype(F32)).astype(BF16)
    return _dot(tri_bf16, x0) + (_dot(tri_bf16, x1) + _dot(tri_bf16, x2))


def _wkv_masks(L):
    n = 2 * L
    row = lax.broadcasted_iota(jnp.int32, (n, n), 0)
    col = lax.broadcasted_iota(jnp.int32, (n, n), 1)
    strict = row > col
    incl = row >= col
    levels = []
    m = 1
    while m < L:
        same = (row // (2 * m)) == (col // (2 * m))
        lvl = same & ((row % (2 * m)) >= m) & ((col % (2 * m)) < m)
        levels.append((lvl, row == col))
        m *= 2
    return strict, incl, levels


def _rwkv_kernel(has_vres, *refs):
    if has_vres:
        (z_ref, vf_ref, mu_ref, vec_ref, w2a2_ref, g2v2_ref, bd_ref,
         y_ref, carry, state) = refs
    else:
        (z_ref, mu_ref, vec_ref, w2a2_ref, g2v2_ref, bd_ref,
         y_ref, vout_ref, carry, state) = refs
    c = pl.program_id(1)
    n_seq, seq_rows = z_ref.shape[0], z_ref.shape[1]
    rows = n_seq * seq_rows
    L = WKV_CHUNK
    W = RWKV_WIDTH

    @pl.when(c == 0)
    def _():
        carry[...] = jnp.zeros(carry.shape, F32)
        state[...] = jnp.zeros(state.shape, F32)

    flat = lambda ref: jnp.concatenate([ref[s] for s in range(n_seq)], axis=0)
    z = flat(z_ref)
    row = lax.broadcasted_iota(jnp.int32, z.shape, 0)
    prev = pltpu.roll(z, 1, axis=0)
    for s in range(n_seq):
        prev = jnp.where(row == s * seq_rows, carry[s, SUBLANES - 1:SUBLANES, :], prev)
        carry[s] = z[(s + 1) * seq_rows - SUBLANES:(s + 1) * seq_rows, :]
    zs_all = z + mu_ref[...] * (prev - z)
    if has_vres:
        vf_all = flat(vf_ref)
    vec = vec_ref[...]
    w0, a0, k_k, k_a, r_k, ln_g, ln_b, v0 = (vec[j:j + 1] for j in range(8))
    bd = bd_ref[...]
    masks = _wkv_masks(L)
    rowt = lax.broadcasted_iota(jnp.int32, (L, L), 0)
    colt = lax.broadcasted_iota(jnp.int32, (L, L), 1)
    tri = jnp.where(rowt >= colt, 1.0, 0.0).astype(BF16)
    n_pairs = RWKV_HEADS // 2
    S_now = [{(s, p): state[s * n_pairs + p] for s in range(n_seq) for p in range(n_pairs)}]

    def group(r0, r1):
        zs = zs_all[r0:r1]
        r = zs[:, 0:W]
        k = zs[:, W:2 * W]
        v = zs[:, 2 * W:3 * W]
        wa = zs[:, 3 * W:3 * W + LANES]
        gb = zs[:, 3 * W + LANES:ZR_COLS]
        lane_a = lax.broadcasted_iota(jnp.int32, wa.shape, 1)
        t1 = _dot(jnp.where(lane_a < RWKV_DECAY_LORA, jnp.tanh(wa), wa).astype(BF16), w2a2_ref[...])
        lane_g = lax.broadcasted_iota(jnp.int32, gb.shape, 1)
        t2 = _dot(jnp.where(lane_g < RWKV_GATE_LORA, _sigmoid(gb), gb).astype(BF16), g2v2_ref[...])
        yield
        xw = w0 + t1[:, :W]
        w_log = -(jnp.maximum(-xw, 0.0) + jnp.log(1.0 + jnp.exp(-jnp.abs(xw)))) - 0.5
        lw = -jnp.exp(w_log)
        a = _sigmoid(a0 + t1[:, W:])
        g = t2[:, :W]
        if has_vres:
            v = v + (vf_all[r0:r1] - v) * _sigmoid(v0 + t2[:, W:])
        else:
            for s in range(n_seq):
                vout_ref[s] = v[s * seq_rows:(s + 1) * seq_rows]
        kk = k * k_k
        kk = kk * jnp.minimum(lax.rsqrt(_mm_exact_rhs(kk * kk, bd)), 1e12)
        k = k * (1.0 + (a - 1.0) * k_a)
        yield
        n_chunks = seq_rows // L
        idx = [(s, ch, p) for s in range(n_seq) for ch in range(n_chunks) for p in range(n_pairs)]
        cut = lambda x: [x[s * seq_rows + ch * L:s * seq_rows + (ch + 1) * L, p * LANES:(p + 1) * LANES]
                         for s, ch, p in idx]
        res = yield from _wkv_prep(cut(r), cut(lw), cut(k), cut(v), cut(kk), cut(a), tri, masks)
        prep = dict(zip(idx, res))
        yield
        sp = [(s, p) for s in range(n_seq) for p in range(n_pairs)]
        S = S_now[0]
        y_blk = {}
        for ch in range(n_chunks):
            y_s = {(s, p): _mm(prep[s, ch, p][0], S[s, p], NT, WKV_PASSES_STATE) + prep[s, ch, p][1]
                   for s, p in sp}
            for s in range(n_seq):
                y_blk[s, ch] = jnp.concatenate([y_s[s, p][:L] + y_s[s, p][L:] for p in range(n_pairs)], axis=1)
            S = {(s, p): S[s, p] * prep[s, ch, p][2] + _mm(S[s, p], prep[s, ch, p][3], NN, WKV_PASSES_STATE)
                 + prep[s, ch, p][4] for s, p in sp}
        S_now[0] = S
        y = jnp.concatenate([y_blk[s, ch] for s in range(n_seq) for ch in range(n_chunks)], axis=0)
        yield
        inv = 1.0 / HEAD_DIM
        mean = _mm_exact_rhs(y, bd) * inv
        yc = y - mean
        var = _mm_exact_rhs(yc * yc, bd) * inv
        yn = yc * lax.rsqrt(var + RWKV_LNX_EPS) * ln_g + ln_b
        bonus = _mm_exact_rhs(r * k * r_k, bd) * v
        out = ((yn + bonus) * g).astype(BF16)
        for s in range(n_seq):
            y_ref[s] = out[s * seq_rows:(s + 1) * seq_rows]

    for _ in group(0, rows):
        pass
    for s in range(n_seq):
        for p in range(n_pairs):
            state[s * n_pairs + p] = S_now[0][s, p]


def _rwkv(zr, v_first, mu, vec8, w2a2, g2v2, bd64):
    B, T, _ = zr.shape
    L = WKV_CHUNK * WKV_STEP_CHUNKS
    ns = WKV_STEP_SEQS
    has_vres = v_first is not None
    const = lambda shape: pl.BlockSpec(shape, lambda b, c: tuple(0 for _ in shape))
    tok = lambda w: pl.BlockSpec((ns, L, w), lambda b, c: (b, c, 0))
    in_specs = [tok(ZR_COLS)]
    args = [zr]
    if has_vres:
        in_specs.append(tok(RWKV_WIDTH))
        args.append(v_first)
    in_specs += [const((1, ZR_COLS)), const((8, RWKV_WIDTH)), const((LANES, 2 * RWKV_WIDTH)),
                 const((2 * LANES, 2 * RWKV_WIDTH)), const((LANES, LANES))]
    args += [mu, vec8, w2a2, g2v2, bd64]
    out_specs = [tok(RWKV_WIDTH)]
    out_shape = [jax.ShapeDtypeStruct((B, T, RWKV_WIDTH), BF16)]
    if not has_vres:
        out_specs.append(tok(RWKV_WIDTH))
        out_shape.append(jax.ShapeDtypeStruct((B, T, RWKV_WIDTH), F32))
    outs = pl.pallas_call(
        functools.partial(_rwkv_kernel, has_vres),
        grid=(B // ns, T // L),
        in_specs=in_specs, out_specs=out_specs, out_shape=out_shape,
        scratch_shapes=[pltpu.VMEM((ns, SUBLANES, ZR_COLS), F32),
                        pltpu.VMEM((ns * (RWKV_HEADS // 2), LANES, LANES), F32)],
        compiler_params=_cparams(("arbitrary", "arbitrary")),
    )(*args)
    return (outs[0], v_first) if has_vres else (outs[0], outs[1])


def _attn_step(q_ref, k_ref, v_ref, m_sc, l_sc, acc_sc, masked):
    tq = q_ref.shape[1]
    tk = k_ref.shape[1]
    lane = lax.broadcasted_iota(jnp.int32, (tq, LANES), 1)
    if masked:
        rowi = lax.broadcasted_iota(jnp.int32, (tq, tk), 0)
        coli = lax.broadcasted_iota(jnp.int32, (tq, tk), 1)
        keep = coli <= rowi
    heads = range(MLA_HEADS)
    s = [_dot(q_ref[0, :, hd * LANES:(hd + 1) * LANES], k_ref[0, :, hd * LANES:(hd + 1) * LANES], NT)
         for hd in heads]
    if masked:
        s = [jnp.where(keep, x, NEG_INF) for x in s]
    m_prev = [m_sc[hd] for hd in heads]
    m_new = [jnp.maximum(m_prev[hd], jnp.max(s[hd], axis=-1, keepdims=True)) for hd in heads]
    alpha = [jnp.exp(m_prev[hd] - m_new[hd]) for hd in heads]
    p = [jnp.exp(s[hd] - jnp.concatenate([m_new[hd]] * (tk // LANES), axis=1)) for hd in heads]
    for hd in heads:
        l_sc[hd] = alpha[hd] * l_sc[hd] + jnp.sum(p[hd], axis=-1, keepdims=True)
        m_sc[hd] = m_new[hd]
    pv = [_dot(p[hd].astype(BF16), v_ref[0, :, (hd // 2) * LANES:(hd // 2 + 1) * LANES]) for hd in heads]
    first = lane < HEAD_DIM
    for pr in range(MLA_HEADS // 2):
        acc_sc[pr] = (acc_sc[pr] * jnp.where(first, alpha[2 * pr], alpha[2 * pr + 1])
                      + jnp.where(first, pv[2 * pr], pv[2 * pr + 1]))


def _attn_step_bounded(q_ref, k_ref, v_ref, cb_ref, l_sc, acc_sc, masked):
    tq = q_ref.shape[1]
    tk = k_ref.shape[1]
    lane = lax.broadcasted_iota(jnp.int32, (tq, LANES), 1)
    if masked:
        rowi = lax.broadcasted_iota(jnp.int32, (tq, tk), 0)
        coli = lax.broadcasted_iota(jnp.int32, (tq, tk), 1)
        keep = coli <= rowi
    ps = []
    for hd in range(MLA_HEADS):
        s = _dot(q_ref[0, :, hd * LANES:(hd + 1) * LANES], k_ref[0, :, hd * LANES:(hd + 1) * LANES], NT)
        c = cb_ref[0, hd:hd + 1, :]
        p = jnp.exp(s - jnp.concatenate([c] * (tk // LANES), axis=1))
        if masked:
            p = jnp.where(keep, p, 0.0)
        part = p[:, 0:LANES]
        for t in range(1, tk // LANES):
            part = part + p[:, t * LANES:(t + 1) * LANES]
        l_sc[hd] = l_sc[hd] + part
        ps.append(p.astype(BF16))
    first = lane < HEAD_DIM
    for pr in range(MLA_HEADS // 2):
        pv = _dot(jnp.concatenate(ps[2 * pr:2 * pr + 2], axis=0), v_ref[0, :, pr * LANES:(pr + 1) * LANES])
        acc_sc[pr] = acc_sc[pr] + jnp.where(first, pv[:tq], pv[tq:])


def _attn_finish(o_ref, l_sc, acc_sc, lane_partial):
    tq = o_ref.shape[1]
    lane = lax.broadcasted_iota(jnp.int32, (tq, LANES), 1)
    outs = []
    for pr in range(MLA_HEADS // 2):
        la, lb = l_sc[2 * pr], l_sc[2 * pr + 1]
        if lane_partial:
            la = jnp.sum(la, axis=-1, keepdims=True)
            lb = jnp.sum(lb, axis=-1, keepdims=True)
        outs.append(acc_sc[pr] / jnp.where(lane < HEAD_DIM, la, lb))
    o_ref[0] = jnp.concatenate(outs, axis=1).astype(BF16)


def _attn_kernel(ok_ref, qi_ref, kj_ref, q_ref, k_ref, v_ref, cb_ref, o_ref, m_sc, l_sc, acc_sc):
    i = qi_ref[pl.program_id(1)]
    j = kj_ref[pl.program_id(1)]
    bounded = ok_ref[pl.program_id(0)] == 1
    exact = jnp.logical_not(bounded)

    @pl.when(j == 0)
    def _():
        m_sc[...] = jnp.full(m_sc.shape, NEG_INF, F32)
        l_sc[...] = jnp.zeros(l_sc.shape, F32)
        acc_sc[...] = jnp.zeros(acc_sc.shape, F32)

    @pl.when(bounded & (j < i))
    def _():
        _attn_step_bounded(q_ref, k_ref, v_ref, cb_ref, l_sc, acc_sc, masked=False)

    @pl.when(bounded & (j == i))
    def _():
        _attn_step_bounded(q_ref, k_ref, v_ref, cb_ref, l_sc, acc_sc, masked=True)
        _attn_finish(o_ref, l_sc, acc_sc, lane_partial=True)

    @pl.when(exact & (j < i))
    def _():
        _attn_step(q_ref, k_ref, v_ref, m_sc, l_sc, acc_sc, masked=False)

    @pl.when(exact & (j == i))
    def _():
        _attn_step(q_ref, k_ref, v_ref, m_sc, l_sc, acc_sc, masked=True)
        _attn_finish(o_ref, l_sc, acc_sc, lane_partial=False)


def _attention(q, k, v, qk_gain):
    B, T, _ = q.shape
    nq = T // TQ
    gmax = jnp.max(jnp.abs(qk_gain), axis=1)
    c = gmax[0] * gmax[1] * (MLA_QK_DIM ** 0.5) * ATTN_BOUND_SLACK
    ok = jnp.broadcast_to((c <= ATTN_BOUND_MAX).astype(jnp.int32), (B,))
    cb = jnp.broadcast_to(c, (B, MLA_HEADS, LANES))
    pairs = [(i, j) for i in range(nq) for j in range(i + 1)]
    qi = jnp.asarray([p[0] for p in pairs], jnp.int32)
    kj = jnp.asarray([p[1] for p in pairs], jnp.int32)
    grid_spec = pltpu.PrefetchScalarGridSpec(
        num_scalar_prefetch=3,
        grid=(B, len(pairs)),
        in_specs=[pl.BlockSpec((1, TQ, MLA_HEADS * LANES), lambda b, t, ok, qi, kj: (b, qi[t], 0)),
                  pl.BlockSpec((1, TQ, MLA_HEADS * LANES), lambda b, t, ok, qi, kj: (b, kj[t], 0)),
                  pl.BlockSpec((1, TQ, MLA_HEADS * HEAD_DIM), lambda b, t, ok, qi, kj: (b, kj[t], 0)),
                  pl.BlockSpec((1, MLA_HEADS, LANES), lambda b, t, ok, qi, kj: (b, 0, 0))],
        out_specs=pl.BlockSpec((1, TQ, MLA_HEADS * HEAD_DIM), lambda b, t, ok, qi, kj: (b, qi[t], 0)),
        scratch_shapes=[pltpu.VMEM((MLA_HEADS, TQ, LANES), F32),
                        pltpu.VMEM((MLA_HEADS, TQ, LANES), F32),
                        pltpu.VMEM((MLA_HEADS // 2, TQ, LANES), F32)])
    return pl.pallas_call(
        _attn_kernel,
        grid_spec=grid_spec,
        out_shape=jax.ShapeDtypeStruct((B, T, MLA_HEADS * HEAD_DIM), BF16),
        compiler_params=_cparams(("arbitrary", "arbitrary")),
    )(ok, qi, kj, q, k, v, cb)


def _mixout_kernel(has_router, *refs):
    if has_router:
        (x_ref, yr_ref, yp_ref, ym_ref, wo_ref, mod_ref, gain_ref, rt_ref, tri_ref,
         xo_ref, h_ref, route_ref, cnt_ref, cnt_sc) = refs
    else:
        x_ref, yr_ref, yp_ref, ym_ref, wo_ref, mod_ref, gain_ref, xo_ref, h_ref = refs
    mod = mod_ref[0]
    o1 = RWKV_WIDTH
    o2 = RWKV_WIDTH + POOL_WIDTH
    mix = (_dot(yr_ref[0], wo_ref[0:o1, :]) + _dot(yp_ref[0], wo_ref[o1:o2, :])
           + _dot(ym_ref[0], wo_ref[o2:, :]))
    x = x_ref[0] + mod[2:3] * mix
    xo_ref[0] = x
    h = _rms(x) * gain_ref[...] * (1.0 + mod[4:5]) + mod[3:4]
    if not has_router:
        h_ref[0] = h.astype(BF16)
    else:
        hp = _pack_bf16_pairs(h)
        for ck in range(ROW_CHUNKS):
            h_ref[ck, 0] = hp[:, ck * PACK_CHUNK_W:(ck + 1) * PACK_CHUNK_W]
        logits = _dot(h.astype(BF16), rt_ref[...])
        lane = lax.broadcasted_iota(jnp.int32, logits.shape, 1).astype(F32)
        lg = jnp.where(lane < N_EXPERTS, logits, -jnp.inf)
        m1 = jnp.max(lg, axis=-1, keepdims=True)
        i1 = jnp.min(jnp.where(lg == m1, lane, float(LANES)), axis=-1, keepdims=True)
        lg2 = jnp.where(lane == i1, -jnp.inf, lg)
        m2 = jnp.max(lg2, axis=-1, keepdims=True)
        i2 = jnp.min(jnp.where(lg2 == m2, lane, float(LANES)), axis=-1, keepdims=True)
        e2 = jnp.exp(m2 - m1)
        g1 = 1.0 / (1.0 + e2)
        g2 = e2 / (1.0 + e2)
        first = (pl.program_id(0) == 0) & (pl.program_id(1) == 0)

        @pl.when(first)
        def _():
            cnt_sc[...] = jnp.zeros(cnt_sc.shape, F32)

        hit1 = lane == i1
        hit2 = lane == i2
        onehot = jnp.where(hit1 | hit2, 1.0, 0.0)
        prefix = _dot(tri_ref[...], onehot.astype(BF16)) + cnt_sc[0:1, :]
        r1 = jnp.sum(jnp.where(hit1, prefix, 0.0), axis=-1, keepdims=True)
        r2 = jnp.sum(jnp.where(hit2, prefix, 0.0), axis=-1, keepdims=True)
        cnt_sc[...] = cnt_sc[...] + jnp.sum(onehot, axis=0, keepdims=True)
        cnt_ref[...] = cnt_sc[...]
        vals = (i1, i2, g1, g2, r1, r2)
        route = jnp.zeros(logits.shape, F32)
        for pos, val in enumerate(vals):
            route = jnp.where(lane == pos, val, route)
        route_ref[0] = route


def _mixout(x, yr, yp, ym, wo, mod_l, gain, router_p):
    B, T, _ = x.shape
    tm = TM_MIX
    has_router = router_p is not None
    const = lambda shape: pl.BlockSpec(shape, lambda b, i: tuple(0 for _ in shape))
    tok = lambda w: pl.BlockSpec((1, tm, w), lambda b, i: (b, i, 0))
    in_specs = [tok(D_MODEL), tok(RWKV_WIDTH), tok(POOL_WIDTH), tok(MLA_HEADS * HEAD_DIM),
                const((D_MODEL, D_MODEL)), pl.BlockSpec((1, 6, D_MODEL), lambda b, i: (b, 0, 0)),
                const((1, D_MODEL))]
    args = [x, yr, yp, ym, wo, mod_l, gain]
    out_specs = [tok(D_MODEL), tok(D_MODEL)]
    out_shape = [jax.ShapeDtypeStruct((B, T, D_MODEL), F32), jax.ShapeDtypeStruct((B, T, D_MODEL), BF16)]
    scratch = []
    if has_router:
        out_specs[1] = pl.BlockSpec((ROW_CHUNKS, 1, tm, PACK_CHUNK_W), lambda b, i: (0, b, i, 0))
        out_shape[1] = jax.ShapeDtypeStruct((ROW_CHUNKS, B, T, PACK_CHUNK_W), jnp.uint32)
        ids = np.arange(tm)
        tri = jnp.asarray(ids[:, None] > ids[None, :], BF16)
        in_specs += [const((D_MODEL, LANES)), const((tm, tm))]
        args += [router_p, tri]
        out_specs += [tok(LANES), const((SUBLANES, LANES))]
        out_shape += [jax.ShapeDtypeStruct((B, T, LANES), F32),
                      jax.ShapeDtypeStruct((SUBLANES, LANES), F32)]
        scratch = [pltpu.VMEM((SUBLANES, LANES), F32)]
    return pl.pallas_call(
        functools.partial(_mixout_kernel, has_router),
        grid=(B, T // tm),
        in_specs=in_specs, out_specs=out_specs, out_shape=out_shape, scratch_shapes=scratch,
        compiler_params=_cparams(("arbitrary", "arbitrary")),
    )(*args)


def _ffn_kernel(h_ref, wg_ref, wu_ref, wo_ref, x_ref, mod_ref, o_ref, acc):
    j = pl.program_id(1)

    @pl.when(j == 0)
    def _():
        acc[...] = jnp.zeros(acc.shape, F32)

    h = h_ref[...]
    gg = _dot(h, wg_ref[...])
    uu = _dot(h, wu_ref[...])
    acc[...] += _dot((_silu(gg) * uu).astype(BF16), wo_ref[...])

    @pl.when(j == pl.num_programs(1) - 1)
    def _():
        o_ref[...] = x_ref[...] + mod_ref[0][5:6] * acc[...]


def _ffn(h2, w_in, w_out, x, mod_l):
    N = h2.shape[0]
    T = N // mod_l.shape[0]
    tm, tf = TM_FFN, TF_FFN
    nf = D_FF // tf
    per_b = T // tm
    return pl.pallas_call(
        _ffn_kernel,
        grid=(N // tm, nf),
        in_specs=[pl.BlockSpec((tm, D_MODEL), lambda i, j: (i, 0)),
                  pl.BlockSpec((D_MODEL, tf), lambda i, j: (0, j)),
                  pl.BlockSpec((D_MODEL, tf), lambda i, j: (0, j + nf)),
                  pl.BlockSpec((tf, D_MODEL), lambda i, j: (j, 0)),
                  pl.BlockSpec((tm, D_MODEL), lambda i, j: (i, 0)),
                  pl.BlockSpec((1, 6, D_MODEL), lambda i, j: (i // per_b, 0, 0))],
        out_specs=pl.BlockSpec((tm, D_MODEL), lambda i, j: (i, 0)),
        out_shape=jax.ShapeDtypeStruct((N, D_MODEL), F32),
        scratch_shapes=[pltpu.VMEM((tm, D_MODEL), F32)],
        compiler_params=_cparams(("arbitrary", "arbitrary")),
    )(h2, w_in, w_in, w_out, x, mod_l)


def _moe_kernel(be_ref, nv_ref, last_ref, x_ref, wg_ref, wu_ref, wo_ref, o_ref, acc, xm):
    i = pl.program_id(0)
    j = pl.program_id(1)

    @pl.when(i <= last_ref[0])
    def _():
        @pl.when(j == 0)
        def _():
            acc[...] = jnp.zeros(acc.shape, F32)
            row = lax.broadcasted_iota(jnp.int32, (xm.shape[0], 1), 0)
            xp = jnp.concatenate([x_ref[ck] for ck in range(ROW_CHUNKS)], axis=1)
            xp = jnp.where(row < nv_ref[i], xp, jnp.uint32(0))
            xm[...] = _unpack_bf16_pairs(xp)

        def swiglu_rows(n_rows):
            x = xm[0:n_rows, :]
            gg = _dot(x, wg_ref[0].astype(BF16))
            uu = _dot(x, wu_ref[0].astype(BF16))
            acc[0:n_rows, :] += _dot((_silu(gg) * uu).astype(BF16), wo_ref[0].astype(BF16))

        half = xm.shape[0] // 2

        @pl.when(nv_ref[i] > half)
        def _():
            swiglu_rows(xm.shape[0])

        @pl.when(nv_ref[i] <= half)
        def _():
            swiglu_rows(half)

        @pl.when(j == pl.num_programs(1) - 1)
        def _():
            yp = _pack_bf16_pairs(acc[...])
            for ck in range(ROW_CHUNKS):
                o_ref[ck] = yp[:, ck * PACK_CHUNK_W:(ck + 1) * PACK_CHUNK_W]


def _moe_experts(xs, w_in, w_out, block_exp, n_valid, last_blk):
    n_rows = xs.shape[1]
    tm, tf = MOE_BLOCK, TF_MOE
    nf = D_FF_EXPERT // tf
    blk = lambda i, last: jnp.minimum(i, last[0])
    chunk = lambda i, j, last: jnp.where(i <= last[0], j, nf - 1)
    grid_spec = pltpu.PrefetchScalarGridSpec(
        num_scalar_prefetch=3,
        grid=(n_rows // tm, nf),
        in_specs=[pl.BlockSpec((ROW_CHUNKS, tm, PACK_CHUNK_W),
                               lambda i, j, be, nv, last: (0, blk(i, last), 0)),
                  pl.BlockSpec((1, D_MODEL, tf),
                               lambda i, j, be, nv, last: (be[blk(i, last)], 0, chunk(i, j, last))),
                  pl.BlockSpec((1, D_MODEL, tf),
                               lambda i, j, be, nv, last: (be[blk(i, last)], 0, chunk(i, j, last) + nf)),
                  pl.BlockSpec((1, tf, D_MODEL),
                               lambda i, j, be, nv, last: (be[blk(i, last)], chunk(i, j, last), 0))],
        out_specs=pl.BlockSpec((ROW_CHUNKS, tm, PACK_CHUNK_W), lambda i, j, be, nv, last: (0, blk(i, last), 0)),
        scratch_shapes=[pltpu.VMEM((tm, D_MODEL), F32), pltpu.VMEM((tm, D_MODEL), BF16)])
    return pl.pallas_call(
        _moe_kernel,
        grid_spec=grid_spec,
        out_shape=jax.ShapeDtypeStruct((ROW_CHUNKS, n_rows, PACK_CHUNK_W), jnp.uint32),
        compiler_params=_cparams(("arbitrary", "arbitrary")),
    )(block_exp, n_valid, last_blk, xs, w_in, w_in, w_out)


def _sc_mesh():
    return plsc.VectorSubcoreMesh(core_axis_name="c", subcore_axis_name="s")


def _sc_scatter_rows(x, dest, n_rows):
    N, D = x.shape
    K = dest.shape[0]
    win = SC_WINDOW

    @pl.kernel(out_type=jax.ShapeDtypeStruct((n_rows, D), x.dtype), mesh=_sc_mesh(), scratch_types=[])
    def scatter(x_hbm, d_hbm, o_hbm):
        def body(x_vmem, *idx_vmem):
            for iv in idx_vmem:
                pltpu.sync_copy(x_vmem, o_hbm.at[iv.at[0]])

        pltpu.emit_pipeline(
            body,
            grid=(N // win,),
            in_specs=[pl.BlockSpec((win, D), lambda i: (i, 0))]
            + [pl.BlockSpec((1, win), functools.partial(lambda k, i: (k, i), k)) for k in range(K)],
            out_specs=[],
            core_axis_name=("c", "s"),
            dimension_semantics=(pltpu.PARALLEL,),
        )(x_hbm, *([d_hbm] * K))

    return scatter(x, dest)


def _sc_gather_rows(x, idx):
    n = idx.shape[0]
    D = x.shape[1]
    win = SC_WINDOW

    @pl.kernel(out_type=jax.ShapeDtypeStruct((n, D), x.dtype), mesh=_sc_mesh(), scratch_types=[])
    def gather(x_hbm, i_hbm, o_hbm):
        def body(i_vmem, o_vmem):
            pltpu.sync_copy(x_hbm.at[i_vmem.at[0]], o_vmem)

        pltpu.emit_pipeline(
            body,
            grid=(n // win,),
            in_specs=[pl.BlockSpec((1, win), lambda i: (0, i))],
            out_specs=[pl.BlockSpec((win, D), lambda i: (i, 0))],
            core_axis_name=("c", "s"),
            dimension_semantics=(pltpu.PARALLEL,),
        )(i_hbm, o_hbm)

    return gather(x, idx.reshape(1, n))


def _combine_kernel(x_ref, ya_ref, yb_ref, route_ref, mod_ref, o_ref):
    rt = route_ref[...]
    ya = _unpack_bf16_pairs(jnp.concatenate([ya_ref[0, ck] for ck in range(ROW_CHUNKS)], axis=1))
    yb = _unpack_bf16_pairs(jnp.concatenate([yb_ref[0, ck] for ck in range(ROW_CHUNKS)], axis=1))
    f = rt[:, 2:3] * ya.astype(F32) + rt[:, 3:4] * yb.astype(F32)
    o_ref[...] = x_ref[...] + mod_ref[0][5:6] * f


def _combine(x, y2, route, mod_l):
    N = x.shape[0]
    T = N // mod_l.shape[0]
    tm = 1024
    per_b = T // tm
    tok = pl.BlockSpec((tm, D_MODEL), lambda i: (i, 0))
    slot = lambda k: pl.BlockSpec((1, ROW_CHUNKS, tm, PACK_CHUNK_W), lambda i: (k, 0, i, 0))
    return pl.pallas_call(
        _combine_kernel,
        grid=(N // tm,),
        in_specs=[tok, slot(0), slot(1),
                  pl.BlockSpec((tm, LANES), lambda i: (i, 0)),
                  pl.BlockSpec((1, 6, D_MODEL), lambda i: (i // per_b, 0, 0))],
        out_specs=tok,
        out_shape=jax.ShapeDtypeStruct((N, D_MODEL), F32),
        compiler_params=_cparams(("arbitrary",)),
    )(x, y2, y2, route, mod_l)


def _moe(h2, route, counts, w_in, w_out, x, mod_l):
    N = x.shape[0]
    blk = MOE_BLOCK
    cnt = counts[0, :N_EXPERTS].astype(jnp.int32)
    padded = (cnt + blk - 1) // blk * blk
    pend = jnp.cumsum(padded)
    pstart = pend - padded
    e = route[:, 0:TOP_K].astype(jnp.int32)
    rank = route[:, 2 * TOP_K:3 * TOP_K].astype(jnp.int32)
    dest = (jnp.take(pstart, e) + rank).T
    n_blocks = N * TOP_K // blk + N_EXPERTS
    bstart = jnp.arange(n_blocks, dtype=jnp.int32) * blk
    block_exp = jnp.minimum(jnp.sum((bstart[:, None] >= pend[None, :]).astype(jnp.int32), axis=1),
                            N_EXPERTS - 1)
    n_valid = jnp.clip(cnt[block_exp] - (bstart - pstart[block_exp]), 0, blk).astype(jnp.int32)
    n_rows = n_blocks * blk
    dest_ck = dest[:, None, :] + (jnp.arange(ROW_CHUNKS, dtype=jnp.int32) * n_rows)[None, :, None]
    xs = _sc_scatter_rows(h2, dest_ck.reshape(TOP_K, ROW_CHUNKS * N), ROW_CHUNKS * n_rows)
    last_blk = (pend[-1:] // blk - 1).astype(jnp.int32)
    yb = _moe_experts(xs.reshape(ROW_CHUNKS, n_rows, PACK_CHUNK_W), w_in, w_out, block_exp, n_valid,
                      last_blk)
    y2 = _sc_gather_rows(yb.reshape(ROW_CHUNKS * n_rows, PACK_CHUNK_W), dest_ck.reshape(-1))
    return _combine(x, y2.reshape(TOP_K, ROW_CHUNKS, N, PACK_CHUNK_W), route, mod_l)


def _layout_w_in(w, has_vres):
    W = RWKV_WIDTH
    off_gd = 3 * W + RWKV_DECAY_LORA + RWKV_ICLR_LORA
    off_pool = off_gd + RWKV_GATE_LORA
    off_q = off_pool + POOL_WIDTH
    off_kv = off_q + MLA_Q_LORA
    off_kr = off_kv + MLA_KV_LORA
    n_base = off_kr + MLA_QK_ROPE
    d = w.shape[0]
    zeros = lambda n: jnp.zeros((d, n), w.dtype)
    vd = w[:, n_base:n_base + RWKV_VRES_LORA] if has_vres else zeros(RWKV_VRES_LORA)
    cols = [w[:, :off_gd], w[:, off_gd:off_pool], vd, zeros(ZR_COLS - off_pool - RWKV_VRES_LORA),
            w[:, off_pool:off_q], w[:, off_q:off_kv], w[:, off_kv:off_kr],
            zeros(MLA_QK_NOPE), w[:, off_kr:n_base], zeros(LANES - MLA_QK_DIM)]
    return jnp.concatenate(cols, axis=1).astype(BF16)


def _pad_heads(w, per_head, keep_from, keep_n):
    K = w.shape[0]
    wh = w.reshape(K, MLA_HEADS, per_head)[:, :, keep_from:keep_from + keep_n]
    wh = jnp.pad(wh, ((0, 0), (0, 0), (0, LANES - keep_n)))
    return wh.reshape(K, MLA_HEADS * LANES)


def kernel(x, c, positions, w_ada, b_ada, norm_gain, w_in_first, w_in_rest, mu_shift, mu_shift_v,
           rwkv_vec, rwkv_v0, rwkv_w2, rwkv_a2, rwkv_g2, rwkv_v2, pool_w, pool_scale,
           mla_q_lat_gain, mla_kv_lat_gain, mla_wq_up, mla_wkv_up, mla_qk_gain, w_out, ffn_w_in,
           ffn_w_out, moe_router, moe_w_in, moe_w_out):
    B, T, D = x.shape
    depth = w_ada.shape[0]
    W = RWKV_WIDTH
    mod = _adaln(c, w_ada, b_ada).reshape(depth, B, 6, D)
    pos3 = positions.reshape(B, T, 1)
    inv_freq = ROPE_BASE ** (-jnp.arange(0, MLA_QK_ROPE, 2, dtype=F32) / MLA_QK_ROPE)
    freq = jnp.concatenate([jnp.zeros((MLA_QK_NOPE,), F32), inv_freq, inv_freq,
                            jnp.zeros((LANES - MLA_QK_DIM,), F32)]).reshape(1, LANES)
    cosf, sinf = _rope_tables(pos3, freq)
    hid = np.arange(LANES) // HEAD_DIM
    bd64 = jnp.asarray(hid[:, None] == hid[None, :], BF16)
    bd128 = jnp.ones((LANES, LANES), BF16)

    v_first = None
    for l in range(depth):
        has_vres = l > 0
        mod_l = mod[l]
        win = _layout_w_in(w_in_first if l == 0 else w_in_rest[l - 1], has_vres)
        poolw = jax.scipy.linalg.block_diag(*[pool_w[l, g] for g in range(len(POOL_WINDOWS))]).astype(BF16)
        wq = _pad_heads(mla_wq_up[l], MLA_QK_DIM, 0, MLA_QK_DIM).astype(BF16)
        wk = _pad_heads(mla_wkv_up[l], MLA_QK_NOPE + HEAD_DIM, 0, MLA_QK_NOPE).astype(BF16)
        wv = mla_wkv_up[l].reshape(MLA_KV_LORA, MLA_HEADS, MLA_QK_NOPE + HEAD_DIM)[:, :, MLA_QK_NOPE:]
        wv = wv.reshape(MLA_KV_LORA, MLA_HEADS * HEAD_DIM).astype(BF16)
        qkg = jnp.tile(jnp.pad(mla_qk_gain[l], ((0, 0), (0, LANES - MLA_QK_DIM))), (1, MLA_HEADS))
        zr, y_pool, q, k, v = _mixin(
            x, mod_l, norm_gain[l, 0].reshape(1, D), win, cosf, sinf, poolw,
            pool_scale[l].reshape(1, -1), mla_q_lat_gain[l].reshape(1, -1),
            mla_kv_lat_gain[l].reshape(1, -1), wq, wk, wv, qkg, bd128)

        pad_mu = ZR_COLS - mu_shift.shape[1] - RWKV_VRES_LORA
        mu_v = mu_shift_v[l - 1] if has_vres else jnp.zeros((RWKV_VRES_LORA,), F32)
        mu = jnp.concatenate([mu_shift[l], mu_v, jnp.zeros((pad_mu,), F32)]).reshape(1, ZR_COLS)
        v0 = rwkv_v0[l - 1] if has_vres else jnp.zeros((W,), F32)
        vec8 = jnp.concatenate([rwkv_vec[l], v0[None]], axis=0)
        w2a2 = jax.scipy.linalg.block_diag(rwkv_w2[l], rwkv_a2[l]).astype(BF16)
        g2 = jnp.pad(rwkv_g2[l], ((0, 2 * LANES - RWKV_GATE_LORA), (0, 0)))
        if has_vres:
            v2 = jnp.pad(rwkv_v2[l - 1], ((RWKV_GATE_LORA, 2 * LANES - RWKV_GATE_LORA - RWKV_VRES_LORA), (0, 0)))
        else:
            v2 = jnp.zeros((2 * LANES, W), F32)
        g2v2 = jnp.concatenate([g2, v2], axis=1).astype(BF16)
        y_rwkv, v_first = _rwkv(zr, v_first, mu, vec8, w2a2, g2v2, bd64)

        y_mla = _attention(q, k, v, mla_qk_gain[l])

        is_moe = (l % 2 == 1)
        router_p = None
        if is_moe:
            router_p = jnp.pad(moe_router[l // 2], ((0, 0), (0, LANES - N_EXPERTS))).astype(BF16)
        outs = _mixout(x, y_rwkv, y_pool, y_mla, w_out[l].astype(BF16), mod_l,
                       norm_gain[l, 1].reshape(1, D), router_p)
        x_mid, h2 = outs[0], outs[1]
        xf = x_mid.reshape(B * T, D)
        if is_moe:
            xo = _moe(h2.reshape(ROW_CHUNKS * B * T, PACK_CHUNK_W), outs[2].reshape(B * T, LANES), outs[3],
                      moe_w_in[l // 2], moe_w_out[l // 2], xf, mod_l)
        else:
            xo = _ffn(h2.reshape(B * T, D), ffn_w_in[l // 2].astype(BF16), ffn_w_out[l // 2].astype(BF16),
                      xf, mod_l)
        x = xo.reshape(B, T, D)
    return x
```

```python
import functools

import numpy as np
import jax
import jax.numpy as jnp
from jax import lax
from jax.experimental import pallas as pl
from jax.experimental.pallas import tpu as pltpu
from jax.experimental.pallas import tpu_sc as plsc

F32 = jnp.float32
BF16 = jnp.bfloat16

D_MODEL = 1024
HEAD_DIM = 64
RWKV_WIDTH = 512
RWKV_HEADS = 8
POOL_WIDTH = 256
POOL_WINDOWS = (2, 4, 8, 16)
POOL_HALO = 16
MLA_HEADS = 4
MLA_QK_NOPE = 64
MLA_QK_ROPE = 32
MLA_QK_DIM = 96
MLA_Q_LORA = 256
MLA_KV_LORA = 128
ROPE_BASE = 10000.0
RWKV_DECAY_LORA = 64
RWKV_ICLR_LORA = 64
RWKV_VRES_LORA = 32
RWKV_GATE_LORA = 160
RWKV_LNX_EPS = 64e-5
D_FF = 2816
N_EXPERTS = 8
TOP_K = 2
D_FF_EXPERT = 3584
NORM_EPS = 1e-6
NEG_INF = -1e30

LANES = 128
SUBLANES = 8
VMEM_LIMIT = 56 * 1024 * 1024

ZR_COLS = 1920
Z_POOL_OFF = ZR_COLS
Z_QLAT_OFF = Z_POOL_OFF + POOL_WIDTH
Z_KVLAT_OFF = Z_QLAT_OFF + MLA_Q_LORA
Z_KROPE_OFF = Z_KVLAT_OFF + MLA_KV_LORA
Z_COLS = Z_KROPE_OFF + LANES

MIXIN_PIECE_W = 512
TM_MIX = 512
WKV_CHUNK = 64
TQ = 512
ATTN_BOUND_SLACK = 1.02
ATTN_BOUND_MAX = 40.0
TM_FFN = 1024
TF_FFN = 1408
MOE_BLOCK = 1024
TF_MOE = 512
SC_WINDOW = 128
ROW_CHUNKS = 4
ROW_CHUNK_W = D_MODEL // ROW_CHUNKS
PACK_CHUNK_W = ROW_CHUNK_W // 2

SEGSUM_SPLITS = 1

NN = (((1,), (0,)), ((), ()))
NT = (((1,), (1,)), ((), ()))


def _dot(a, b, dims=NN):
    return lax.dot_general(a, b, dims, preferred_element_type=F32)


def _split2(a):
    hi = a.astype(BF16)
    lo = (a - hi.astype(F32)).astype(BF16)
    return hi, lo


def _mm(a, b, dims=NN, passes=3):
    if passes == 1:
        return _dot(a.astype(BF16), b.astype(BF16), dims)
    ah, al = _split2(a)
    bh, bl = _split2(b)
    return _dot(ah, bh, dims) + (_dot(ah, bl, dims) + _dot(al, bh, dims))


def _mm_exact_rhs(a, b_bf16, splits=SEGSUM_SPLITS):
    m, w = a.shape
    nb = w // LANES
    stacked = jnp.concatenate([a[:, i * LANES:(i + 1) * LANES] for i in range(nb)], axis=0)
    out = None
    rem = stacked
    for s in range(splits):
        part = rem.astype(BF16)
        term = _dot(part, b_bf16)
        out = term if out is None else out + term
        if s + 1 < splits:
            rem = rem - part.astype(F32)
    return jnp.concatenate([out[i * m:(i + 1) * m] for i in range(nb)], axis=1)


def _pack_bf16_pairs(h):
    w = h.shape[1] // 2
    lo = lax.bitcast_convert_type(h[:, :w].astype(BF16).astype(F32), jnp.uint32)
    hi = lax.bitcast_convert_type(h[:, w:].astype(BF16).astype(F32), jnp.uint32)
    return (lo >> 16) | (hi & jnp.uint32(0xFFFF0000))


def _unpack_bf16_pairs(p):
    lo = lax.bitcast_convert_type(p << 16, F32)
    hi = lax.bitcast_convert_type(p & jnp.uint32(0xFFFF0000), F32)
    return jnp.concatenate([lo, hi], axis=1).astype(BF16)


def _sigmoid(x):
    return 1.0 / (1.0 + jnp.exp(-x))


def _silu(x):
    return x * _sigmoid(x)


def _rms(x, eps=NORM_EPS):
    return x * lax.rsqrt(jnp.mean(x * x, axis=-1, keepdims=True) + eps)


def _cparams(sem):
    return pltpu.CompilerParams(dimension_semantics=sem, vmem_limit_bytes=VMEM_LIMIT)


def _adaln_kernel(c_ref, w_ref, b_ref, o_ref):
    ca = _silu(c_ref[...])
    o_ref[0] = _mm(ca, w_ref[0]) + b_ref[0]


def _adaln(c, w_ada, b_ada):
    L = w_ada.shape[0]
    B = c.shape[0]
    n = w_ada.shape[2] // D_MODEL
    return pl.pallas_call(
        _adaln_kernel,
        grid=(L, n),
        in_specs=[pl.BlockSpec((B, D_MODEL), lambda l, j: (0, 0)),
                  pl.BlockSpec((1, D_MODEL, D_MODEL), lambda l, j: (l, 0, j)),
                  pl.BlockSpec((1, 1, D_MODEL), lambda l, j: (l, 0, j))],
        out_specs=pl.BlockSpec((1, B, D_MODEL), lambda l, j: (l, 0, j)),
        out_shape=jax.ShapeDtypeStruct((L, B, n * D_MODEL), F32),
        compiler_params=_cparams(("arbitrary", "arbitrary")),
    )(c, w_ada, b_ada.reshape(L, 1, -1))


def _rope(x, cosf, sinf, lane):
    up = pltpu.roll(x, LANES - MLA_QK_ROPE // 2, axis=1)
    dn = pltpu.roll(x, MLA_QK_ROPE // 2, axis=1)
    rot = jnp.where(lane < MLA_QK_NOPE + MLA_QK_ROPE // 2, -up, dn)
    return x * cosf + rot * sinf


def _rope_kernel(pos_ref, freq_ref, cos_ref, sin_ref):
    tm = pos_ref.shape[1]
    lane = lax.broadcasted_iota(jnp.int32, (tm, LANES), 1)
    in_rope = (lane >= MLA_QK_NOPE) & (lane < MLA_QK_DIM)
    ang = pos_ref[0].astype(F32) * freq_ref[...]
    cos_ref[0] = jnp.where(in_rope, jnp.cos(ang), 1.0)
    sin_ref[0] = jnp.where(in_rope, jnp.sin(ang), 0.0)


def _rope_tables(pos3, freq):
    B, T, _ = pos3.shape
    tm = TM_MIX
    tok = lambda w: pl.BlockSpec((1, tm, w), lambda b, i: (b, i, 0))
    return pl.pallas_call(
        _rope_kernel,
        grid=(B, T // tm),
        in_specs=[tok(1), pl.BlockSpec((1, LANES), lambda b, i: (0, 0))],
        out_specs=[tok(LANES), tok(LANES)],
        out_shape=[jax.ShapeDtypeStruct((B, T, LANES), F32)] * 2,
        compiler_params=_cparams(("arbitrary", "arbitrary")),
    )(pos3, freq)


def _mixin_kernel(x_ref, mod_ref, gain_ref, win_ref, cos_ref, sin_ref, poolw_ref, pools_ref,
                  qg_ref, kvg_ref, wq_ref, wk_ref, wv_ref, qkg_ref, bd_ref,
                  zr_ref, yp_ref, q_ref, k_ref, v_ref, ubuf):
    i = pl.program_id(1)
    tm = x_ref.shape[1]
    x = x_ref[0]
    mod = mod_ref[0]
    @pl.when(i == 0)
    def _():
        ubuf[0:POOL_HALO, :] = jnp.zeros((POOL_HALO, POOL_WIDTH), F32)

    h = (_rms(x) * gain_ref[...] * (1.0 + mod[1:2]) + mod[0:1]).astype(BF16)
    zb = _dot(h, win_ref[:, ZR_COLS:])
    zcol = lambda off, w: zb[:, off - ZR_COLS:off - ZR_COLS + w]
    def project_piece(n):
        cols = slice(n * MIXIN_PIECE_W, min((n + 1) * MIXIN_PIECE_W, ZR_COLS))
        zr_ref[0, :, cols] = _dot(h, win_ref[:, cols])

    project_piece(0)
    u = zcol(Z_POOL_OFF, POOL_WIDTH)
    ubuf[POOL_HALO:, :] = u
    ue = ubuf[...]
    s2 = ue + pltpu.roll(ue, 1, axis=0)
    s4 = s2 + pltpu.roll(s2, 2, axis=0)
    s8 = s4 + pltpu.roll(s4, 4, axis=0)
    s16 = s8 + pltpu.roll(s8, 8, axis=0)
    ubuf[0:POOL_HALO, :] = u[tm - POOL_HALO:, :]
    lane_p = lax.broadcasted_iota(jnp.int32, (tm, POOL_WIDTH), 1)
    grp = lane_p // (POOL_WIDTH // len(POOL_WINDOWS))
    win_sum = jnp.where(grp == 0, s2[POOL_HALO:], jnp.where(grp == 1, s4[POOL_HALO:],
                        jnp.where(grp == 2, s8[POOL_HALO:], s16[POOL_HALO:])))
    win = jnp.where(grp == 0, 2, jnp.where(grp == 1, 4, jnp.where(grp == 2, 8, 16)))
    t_abs = i * tm + lax.broadcasted_iota(jnp.int32, (tm, POOL_WIDTH), 0)
    cnt = jnp.minimum(t_abs + 1, win).astype(F32)
    p = win_sum / cnt - u
    yp = _dot(p.astype(BF16), poolw_ref[...]) * pools_ref[...]
    yp_ref[0] = yp.astype(BF16)
    project_piece(1)

    lane = lax.broadcasted_iota(jnp.int32, (tm, LANES), 1)
    cosf = cos_ref[0]
    sinf = sin_ref[0]

    q_lat = zcol(Z_QLAT_OFF, MLA_Q_LORA)
    kv_lat = zcol(Z_KVLAT_OFF, MLA_KV_LORA)
    k_rope = zcol(Z_KROPE_OFF, LANES)
    qn = (_rms(q_lat) * qg_ref[...]).astype(BF16)
    kvn = (_rms(kv_lat) * kvg_ref[...]).astype(BF16)
    q = _dot(qn, wq_ref[...])
    kx = _dot(kvn, wk_ref[...])
    v = _dot(kvn, wv_ref[...])
    v_ref[0] = v.astype(BF16)
    project_piece(2)
    k_pe = _rope(k_rope, cosf, sinf, lane)
    qs, ks = [], []
    for hd in range(MLA_HEADS):
        sl = slice(hd * LANES, (hd + 1) * LANES)
        qs.append(_rope(q[:, sl], cosf, sinf, lane))
        ks.append(kx[:, sl] + k_pe)
    q = jnp.concatenate(qs, axis=1)
    k = jnp.concatenate(ks, axis=1)
    project_piece(3)
    qss = _mm_exact_rhs(q * q, bd_ref[...]) * (1.0 / MLA_QK_DIM)
    kss = _mm_exact_rhs(k * k, bd_ref[...]) * (1.0 / MLA_QK_DIM)
    qkg = qkg_ref[...]
    q = q * lax.rsqrt(qss + NORM_EPS) * qkg[0:1] * (MLA_QK_DIM ** -0.5)
    k = k * lax.rsqrt(kss + NORM_EPS) * qkg[1:2]
    q_ref[0] = q.astype(BF16)
    k_ref[0] = k.astype(BF16)


def _mixin(x, mod_l, gain, win, cosf, sinf, poolw, pools, qg, kvg, wq, wk, wv, qkg, bd128):
    B, T, _ = x.shape
    tm = TM_MIX
    const = lambda shape: pl.BlockSpec(shape, lambda b, i: tuple(0 for _ in shape))
    tok = lambda w: pl.BlockSpec((1, tm, w), lambda b, i: (b, i, 0))
    return pl.pallas_call(
        _mixin_kernel,
        grid=(B, T // tm),
        in_specs=[tok(D_MODEL),
                  pl.BlockSpec((1, 6, D_MODEL), lambda b, i: (b, 0, 0)),
                  const((1, D_MODEL)), const((D_MODEL, Z_COLS)),
                  tok(LANES), tok(LANES),
                  const((POOL_WIDTH, POOL_WIDTH)), const((1, POOL_WIDTH)),
                  const((1, MLA_Q_LORA)), const((1, MLA_KV_LORA)),
                  const((MLA_Q_LORA, MLA_HEADS * LANES)), const((MLA_KV_LORA, MLA_HEADS * LANES)),
                  const((MLA_KV_LORA, MLA_HEADS * HEAD_DIM)), const((2, MLA_HEADS * LANES)),
                  const((LANES, LANES))],
        out_specs=[tok(ZR_COLS), tok(POOL_WIDTH), tok(MLA_HEADS * LANES), tok(MLA_HEADS * LANES),
                   tok(MLA_HEADS * HEAD_DIM)],
        out_shape=[jax.ShapeDtypeStruct((B, T, ZR_COLS), F32),
                   jax.ShapeDtypeStruct((B, T, POOL_WIDTH), BF16),
                   jax.ShapeDtypeStruct((B, T, MLA_HEADS * LANES), BF16),
                   jax.ShapeDtypeStruct((B, T, MLA_HEADS * LANES), BF16),
                   jax.ShapeDtypeStruct((B, T, MLA_HEADS * HEAD_DIM), BF16)],
        scratch_shapes=[pltpu.VMEM((POOL_HALO + tm, POOL_WIDTH), F32)],
        compiler_params=_cparams(("arbitrary", "arbitrary")),
    )(x, mod_l, gain, win, cosf, sinf, poolw, pools, qg, kvg, wq, wk, wv, qkg, bd128)


WKV_PASSES_SCORE = 1
WKV_PASSES_INV = 1
WKV_PASSES_APPLY = 1
WKV_PASSES_STATE = 1
WKV_STEP_CHUNKS = 1
WKV_STEP_SEQS = 4


def _stack_heads(xp, lane):
    return jnp.concatenate([jnp.where(lane < HEAD_DIM, xp, 0.0),
                            jnp.where(lane >= HEAD_DIM, xp, 0.0)], axis=0)


def _wkv_prep(r, lw, k, v, kk, a, tri, masks):
    L = r[0].shape[0]
    nc = len(r)
    each = lambda f, *ls: [f(*xs) for xs in zip(*ls)]
    lane = lax.broadcasted_iota(jnp.int32, (L, LANES), 1)
    stack = lambda x: _stack_heads(x, lane)
    cum = each(lambda x: _mm_exact_rhs_left(tri, x), lw)
    cum_last = each(lambda c: c[L - 1:L, :], cum)
    e_w = each(jnp.exp, cum)
    e_wm = each(lambda c, x: jnp.exp(c - x), cum, lw)
    e_iw = each(lambda c: jnp.exp(-c), cum)
    e_d = each(lambda cl, c: jnp.exp(cl - c), cum_last, cum)
    beta = each(lambda x, y: x * y, kk, a)
    r_f = each(lambda x, e: x * e, r, e_w)
    a_f = each(lambda x, e: -x * e, kk, e_wm)
    a_s = each(stack, a_f)
    b_s = each(lambda x, e: stack(x * e), beta, e_iw)
    k_s = each(lambda x, e: stack(x * e), k, e_iw)
    b_d = each(lambda x, e: stack(x * e), beta, e_d)
    k_d = each(lambda x, e: stack(x * e), k, e_d)
    v_s = each(stack, v)
    yield
    g = each(lambda af, rf, bs, ks: _mm(jnp.concatenate([af, rf], axis=0),
                                        jnp.concatenate([bs, ks], axis=0), NT, WKV_PASSES_SCORE),
             a_f, r_f, b_s, k_s)
    strict, incl, levels = masks
    a_ab = each(lambda x: jnp.where(strict, x[:L, :LANES], 0.0), g)
    a_ak = each(lambda x: jnp.where(strict, x[:L, LANES:], 0.0), g)
    s_rb = each(lambda x: jnp.where(incl, x[L:, :LANES], 0.0), g)
    s_rk = each(lambda x: jnp.where(incl, x[L:, LANES:], 0.0), g)
    eye = jnp.where(levels[0][1], 1.0, 0.0)
    tinv = each(lambda x: eye + jnp.where(levels[0][0], x, 0.0), a_ab)
    yield
    for lvl_mask, _ in levels[1:]:
        et = each(lambda x, t: _mm(jnp.where(lvl_mask, x, 0.0), stack(t), NN, WKV_PASSES_INV), a_ab, tinv)
        tinv = each(lambda t, x: t + _mm(t, stack(x), NN, WKV_PASSES_INV), tinv, et)
        yield
    av = each(lambda x, y: _mm(x, y, NN, WKV_PASSES_APPLY), a_ak, v_s)
    tx = each(lambda t, x, y: _mm(t, jnp.concatenate([x, stack(y)], axis=1), NN, WKV_PASSES_APPLY),
              tinv, a_s, av)
    ta_s = each(lambda x: stack(x[:, :LANES]), tx)
    c1_s = each(lambda x: stack(x[:, LANES:]), tx)
    yield
    ra = each(lambda rf, s, x: rf + _mm(s, x, NN, WKV_PASSES_APPLY), r_f, s_rb, ta_s)
    c2 = each(lambda sb, sk, x, vs: _mm(jnp.concatenate([sb, sk], axis=1),
                                        jnp.concatenate([x, vs], axis=0),
                                        NN, WKV_PASSES_APPLY), s_rb, s_rk, c1_s, v_s)
    yield
    tb = each(lambda x, y, bd: _mm(jnp.concatenate([x, y], axis=1).T, bd, NN, WKV_PASSES_APPLY),
              ta_s, c1_s, b_d)
    c3 = each(lambda x, vs, kd: x[LANES:] + _mm(vs.T, kd, NN, WKV_PASSES_APPLY), tb, v_s, k_d)
    return [(ra[i], c2[i], jnp.exp(cum_last[i]), tb[i][:LANES], c3[i]) for i in range(nc)]


def _mm_exact_rhs_left(tri_bf16, x):
    x0 = x.astype(BF16)
    r1 = x - x0.astype(F32)
    x1 = r1.astype(BF16)
    x2 = (r1 - x1.astype(F32)).astype(BF16)
    return _dot(tri_bf16, x0) + (_dot(tri_bf16, x1) + _dot(tri_bf16, x2))


def _wkv_masks(L):
    row = lax.broadcasted_iota(jnp.int32, (L, 2 * L), 0)
    col = lax.broadcasted_iota(jnp.int32, (L, 2 * L), 1) % L
    strict = row > col
    incl = row >= col
    levels = []
    m = 1
    while m < L:
        same = (row // (2 * m)) == (col // (2 * m))
        lvl = same & ((row % (2 * m)) >= m) & ((col % (2 * m)) < m)
        levels.append((lvl, row == col))
        m *= 2
    return strict, incl, levels


def _rwkv_kernel(has_vres, *refs):
    if has_vres:
        (z_ref, vf_ref, mu_ref, vec_ref, w2a2_ref, g2v2_ref, bd_ref,
         y_ref, carry, state) = refs
    else:
        (z_ref, mu_ref, vec_ref, w2a2_ref, g2v2_ref, bd_ref,
         y_ref, vout_ref, carry, state) = refs
    c = pl.program_id(1)
    n_seq, seq_rows = z_ref.shape[0], z_ref.shape[1]
    rows = n_seq * seq_rows
    L = WKV_CHUNK
    W = RWKV_WIDTH

    @pl.when(c == 0)
    def _():
        carry[...] = jnp.zeros(carry.shape, F32)
        state[...] = jnp.zeros(state.shape, F32)

    flat = lambda ref: jnp.concatenate([ref[s] for s in range(n_seq)], axis=0)
    z = flat(z_ref)
    row = lax.broadcasted_iota(jnp.int32, z.shape, 0)
    prev = pltpu.roll(z, 1, axis=0)
    for s in range(n_seq):
        prev = jnp.where(row == s * seq_rows, carry[s, SUBLANES - 1:SUBLANES, :], prev)
        carry[s] = z[(s + 1) * seq_rows - SUBLANES:(s + 1) * seq_rows, :]
    zs_all = z + mu_ref[...] * (prev - z)
    if has_vres:
        vf_all = flat(vf_ref)
    vec = vec_ref[...]
    w0, a0, k_k, k_a, r_k, ln_g, ln_b, v0 = (vec[j:j + 1] for j in range(8))
    bd = bd_ref[...]
    masks = _wkv_masks(L)
    rowt = lax.broadcasted_iota(jnp.int32, (L, L), 0)
    colt = lax.broadcasted_iota(jnp.int32, (L, L), 1)
    tri = jnp.where(rowt >= colt, 1.0, 0.0).astype(BF16)
    n_pairs = RWKV_HEADS // 2
    S_now = [{(s, p): state[s * n_pairs + p] for s in range(n_seq) for p in range(n_pairs)}]

    def group(r0, r1):
        zs = zs_all[r0:r1]
        r = zs[:, 0:W]
        k = zs[:, W:2 * W]
        v = zs[:, 2 * W:3 * W]
        wa = zs[:, 3 * W:3 * W + LANES]
        gb = zs[:, 3 * W + LANES:ZR_COLS]
        lane_a = lax.broadcasted_iota(jnp.int32, wa.shape, 1)
        t1 = _dot(jnp.where(lane_a < RWKV_DECAY_LORA, jnp.tanh(wa), wa).astype(BF16), w2a2_ref[...])
        lane_g = lax.broadcasted_iota(jnp.int32, gb.shape, 1)
        t2 = _dot(jnp.where(lane_g < RWKV_GATE_LORA, _sigmoid(gb), gb).astype(BF16), g2v2_ref[...])
        yield
        xw = w0 + t1[:, :W]
        w_log = -(jnp.maximum(-xw, 0.0) + jnp.log(1.0 + jnp.exp(-jnp.abs(xw)))) - 0.5
        lw = -jnp.exp(w_log)
        a = _sigmoid(a0 + t1[:, W:])
        g = t2[:, :W]
        if has_vres:
            v = v + (vf_all[r0:r1] - v) * _sigmoid(v0 + t2[:, W:])
        else:
            for s in range(n_seq):
                vout_ref[s] = v[s * seq_rows:(s + 1) * seq_rows]
        kk = k * k_k
        kk = kk * jnp.minimum(lax.rsqrt(_mm_exact_rhs(kk * kk, bd)), 1e12)
        k = k * (1.0 + (a - 1.0) * k_a)
        yield
        n_chunks = seq_rows // L
        idx = [(s, ch, p) for s in range(n_seq) for ch in range(n_chunks) for p in range(n_pairs)]
        cut = lambda x: [x[s * seq_rows + ch * L:s * seq_rows + (ch + 1) * L, p * LANES:(p + 1) * LANES]
                         for s, ch, p in idx]
        res = yield from _wkv_prep(cut(r), cut(lw), cut(k), cut(v), cut(kk), cut(a), tri, masks)
        prep = dict(zip(idx, res))
        yield
        sp = [(s, p) for s in range(n_seq) for p in range(n_pairs)]
        S = S_now[0]
        y_blk = {}
        for ch in range(n_chunks):
            y_s = {(s, p): _mm(prep[s, ch, p][0], S[s, p], NT, WKV_PASSES_STATE) + prep[s, ch, p][1]
                   for s, p in sp}
            for s in range(n_seq):
                y_blk[s, ch] = jnp.concatenate([y_s[s, p] for p in range(n_pairs)], axis=1)
            S = {(s, p): S[s, p] * prep[s, ch, p][2] + _mm(S[s, p], prep[s, ch, p][3], NN, WKV_PASSES_STATE)
                 + prep[s, ch, p][4] for s, p in sp}
        S_now[0] = S
        y = jnp.concatenate([y_blk[s, ch] for s in range(n_seq) for ch in range(n_chunks)], axis=0)
        yield
        inv = 1.0 / HEAD_DIM
        mean = _mm_exact_rhs(y, bd) * inv
        yc = y - mean
        var = _mm_exact_rhs(yc * yc, bd) * inv
        yn = yc * lax.rsqrt(var + RWKV_LNX_EPS) * ln_g + ln_b
        bonus = _mm_exact_rhs(r * k * r_k, bd) * v
        out = ((yn + bonus) * g).astype(BF16)
        for s in range(n_seq):
            y_ref[s] = out[s * seq_rows:(s + 1) * seq_rows]

    for _ in group(0, rows):
        pass
    for s in range(n_seq):
        for p in range(n_pairs):
            state[s * n_pairs + p] = S_now[0][s, p]


def _rwkv(zr, v_first, mu, vec8, w2a2, g2v2, bd64):
    B, T, _ = zr.shape
    L = WKV_CHUNK * WKV_STEP_CHUNKS
    ns = WKV_STEP_SEQS
    has_vres = v_first is not None
    const = lambda shape: pl.BlockSpec(shape, lambda b, c: tuple(0 for _ in shape))
    tok = lambda w: pl.BlockSpec((ns, L, w), lambda b, c: (b, c, 0))
    in_specs = [tok(ZR_COLS)]
    args = [zr]
    if has_vres:
        in_specs.append(tok(RWKV_WIDTH))
        args.append(v_first)
    in_specs += [const((1, ZR_COLS)), const((8, RWKV_WIDTH)), const((LANES, 2 * RWKV_WIDTH)),
                 const((2 * LANES, 2 * RWKV_WIDTH)), const((LANES, LANES))]
    args += [mu, vec8, w2a2, g2v2, bd64]
    out_specs = [tok(RWKV_WIDTH)]
    out_shape = [jax.ShapeDtypeStruct((B, T, RWKV_WIDTH), BF16)]
    if not has_vres:
        out_specs.append(tok(RWKV_WIDTH))
        out_shape.append(jax.ShapeDtypeStruct((B, T, RWKV_WIDTH), F32))
    outs = pl.pallas_call(
        functools.partial(_rwkv_kernel, has_vres),
        grid=(B // ns, T // L),
        in_specs=in_specs, out_specs=out_specs, out_shape=out_shape,
        scratch_shapes=[pltpu.VMEM((ns, SUBLANES, ZR_COLS), F32),
                        pltpu.VMEM((ns * (RWKV_HEADS // 2), LANES, LANES), F32)],
        compiler_params=_cparams(("arbitrary", "arbitrary")),
    )(*args)
    return (outs[0], v_first) if has_vres else (outs[0], outs[1])


def _attn_step(q_ref, k_ref, v_ref, m_sc, l_sc, acc_sc, masked):
    tq = q_ref.shape[1]
    tk = k_ref.shape[1]
    lane = lax.broadcasted_iota(jnp.int32, (tq, LANES), 1)
    if masked:
        rowi = lax.broadcasted_iota(jnp.int32, (tq, tk), 0)
        coli = lax.broadcasted_iota(jnp.int32, (tq, tk), 1)
        keep = coli <= rowi
    heads = range(MLA_HEADS)
    s = [_dot(q_ref[0, :, hd * LANES:(hd + 1) * LANES], k_ref[0, :, hd * LANES:(hd + 1) * LANES], NT)
         for hd in heads]
    if masked:
        s = [jnp.where(keep, x, NEG_INF) for x in s]
    m_prev = [m_sc[hd] for hd in heads]
    m_new = [jnp.maximum(m_prev[hd], jnp.max(s[hd], axis=-1, keepdims=True)) for hd in heads]
    alpha = [jnp.exp(m_prev[hd] - m_new[hd]) for hd in heads]
    p = [jnp.exp(s[hd] - jnp.concatenate([m_new[hd]] * (tk // LANES), axis=1)) for hd in heads]
    for hd in heads:
        l_sc[hd] = alpha[hd] * l_sc[hd] + jnp.sum(p[hd], axis=-1, keepdims=True)
        m_sc[hd] = m_new[hd]
    pv = [_dot(p[hd].astype(BF16), v_ref[0, :, (hd // 2) * LANES:(hd // 2 + 1) * LANES]) for hd in heads]
    first = lane < HEAD_DIM
    for pr in range(MLA_HEADS // 2):
        acc_sc[pr] = (acc_sc[pr] * jnp.where(first, alpha[2 * pr], alpha[2 * pr + 1])
                      + jnp.where(first, pv[2 * pr], pv[2 * pr + 1]))


def _attn_step_bounded(q_ref, k_ref, v_ref, cb_ref, l_sc, acc_sc, masked):
    tq = q_ref.shape[1]
    tk = k_ref.shape[1]
    lane = lax.broadcasted_iota(jnp.int32, (tq, LANES), 1)
    if masked:
        rowi = lax.broadcasted_iota(jnp.int32, (tq, tk), 0)
        coli = lax.broadcasted_iota(jnp.int32, (tq, tk), 1)
        keep = coli <= rowi
    ps = []
    for hd in range(MLA_HEADS):
        s = _dot(q_ref[0, :, hd * LANES:(hd + 1) * LANES], k_ref[0, :, hd * LANES:(hd + 1) * LANES], NT)
        c = cb_ref[0, hd:hd + 1, :]
        p = jnp.exp(s - jnp.concatenate([c] * (tk // LANES), axis=1))
        if masked:
            p = jnp.where(keep, p, 0.0)
        part = p[:, 0:LANES]
        for t in range(1, tk // LANES):
            part = part + p[:, t * LANES:(t + 1) * LANES]
        l_sc[hd] = l_sc[hd] + part
        ps.append(p.astype(BF16))
    first = lane < HEAD_DIM
    for pr in range(MLA_HEADS // 2):
        pv = _dot(jnp.concatenate(ps[2 * pr:2 * pr + 2], axis=0), v_ref[0, :, pr * LANES:(pr + 1) * LANES])
        acc_sc[pr] = acc_sc[pr] + jnp.where(first, pv[:tq], pv[tq:])


def _attn_finish(o_ref, l_sc, acc_sc, lane_partial):
    tq = o_ref.shape[1]
    lane = lax.broadcasted_iota(jnp.int32, (tq, LANES), 1)
    outs = []
    for pr in range(MLA_HEADS // 2):
        la, lb = l_sc[2 * pr], l_sc[2 * pr + 1]
        if lane_partial:
            la = jnp.sum(la, axis=-1, keepdims=True)
            lb = jnp.sum(lb, axis=-1, keepdims=True)
        outs.append(acc_sc[pr] / jnp.where(lane < HEAD_DIM, la, lb))
    o_ref[0] = jnp.concatenate(outs, axis=1).astype(BF16)


def _attn_kernel(ok_ref, qi_ref, kj_ref, q_ref, k_ref, v_ref, cb_ref, o_ref, m_sc, l_sc, acc_sc):
    i = qi_ref[pl.program_id(1)]
    j = kj_ref[pl.program_id(1)]
    bounded = ok_ref[pl.program_id(0)] == 1
    exact = jnp.logical_not(bounded)

    @pl.when(j == 0)
    def _():
        m_sc[...] = jnp.full(m_sc.shape, NEG_INF, F32)
        l_sc[...] = jnp.zeros(l_sc.shape, F32)
        acc_sc[...] = jnp.zeros(acc_sc.shape, F32)

    @pl.when(bounded & (j < i))
    def _():
        _attn_step_bounded(q_ref, k_ref, v_ref, cb_ref, l_sc, acc_sc, masked=False)

    @pl.when(bounded & (j == i))
    def _():
        _attn_step_bounded(q_ref, k_ref, v_ref, cb_ref, l_sc, acc_sc, masked=True)
        _attn_finish(o_ref, l_sc, acc_sc, lane_partial=True)

    @pl.when(exact & (j < i))
    def _():
        _attn_step(q_ref, k_ref, v_ref, m_sc, l_sc, acc_sc, masked=False)

    @pl.when(exact & (j == i))
    def _():
        _attn_step(q_ref, k_ref, v_ref, m_sc, l_sc, acc_sc, masked=True)
        _attn_finish(o_ref, l_sc, acc_sc, lane_partial=False)


def _attention(q, k, v, qk_gain):
    B, T, _ = q.shape
    nq = T // TQ
    gmax = jnp.max(jnp.abs(qk_gain), axis=1)
    c = gmax[0] * gmax[1] * (MLA_QK_DIM ** 0.5) * ATTN_BOUND_SLACK
    ok = jnp.broadcast_to((c <= ATTN_BOUND_MAX).astype(jnp.int32), (B,))
    cb = jnp.broadcast_to(c, (B, MLA_HEADS, LANES))
    pairs = [(i, j) for i in range(nq) for j in range(i + 1)]
    qi = jnp.asarray([p[0] for p in pairs], jnp.int32)
    kj = jnp.asarray([p[1] for p in pairs], jnp.int32)
    grid_spec = pltpu.PrefetchScalarGridSpec(
        num_scalar_prefetch=3,
        grid=(B, len(pairs)),
        in_specs=[pl.BlockSpec((1, TQ, MLA_HEADS * LANES), lambda b, t, ok, qi, kj: (b, qi[t], 0)),
                  pl.BlockSpec((1, TQ, MLA_HEADS * LANES), lambda b, t, ok, qi, kj: (b, kj[t], 0)),
                  pl.BlockSpec((1, TQ, MLA_HEADS * HEAD_DIM), lambda b, t, ok, qi, kj: (b, kj[t], 0)),
                  pl.BlockSpec((1, MLA_HEADS, LANES), lambda b, t, ok, qi, kj: (b, 0, 0))],
        out_specs=pl.BlockSpec((1, TQ, MLA_HEADS * HEAD_DIM), lambda b, t, ok, qi, kj: (b, qi[t], 0)),
        scratch_shapes=[pltpu.VMEM((MLA_HEADS, TQ, LANES), F32),
                        pltpu.VMEM((MLA_HEADS, TQ, LANES), F32),
                        pltpu.VMEM((MLA_HEADS // 2, TQ, LANES), F32)])
    return pl.pallas_call(
        _attn_kernel,
        grid_spec=grid_spec,
        out_shape=jax.ShapeDtypeStruct((B, T, MLA_HEADS * HEAD_DIM), BF16),
        compiler_params=_cparams(("arbitrary", "arbitrary")),
    )(ok, qi, kj, q, k, v, cb)


def _mixout_kernel(has_router, *refs):
    if has_router:
        (x_ref, yr_ref, yp_ref, ym_ref, wo_ref, mod_ref, gain_ref, rt_ref, tri_ref,
         xo_ref, h_ref, route_ref, cnt_ref, cnt_sc) = refs
    else:
        x_ref, yr_ref, yp_ref, ym_ref, wo_ref, mod_ref, gain_ref, xo_ref, h_ref = refs
    mod = mod_ref[0]
    o1 = RWKV_WIDTH
    o2 = RWKV_WIDTH + POOL_WIDTH
    mix = (_dot(yr_ref[0], wo_ref[0:o1, :]) + _dot(yp_ref[0], wo_ref[o1:o2, :])
           + _dot(ym_ref[0], wo_ref[o2:, :]))
    x = x_ref[0] + mod[2:3] * mix
    xo_ref[0] = x
    h = _rms(x) * gain_ref[...] * (1.0 + mod[4:5]) + mod[3:4]
    if not has_router:
        h_ref[0] = h.astype(BF16)
    else:
        hp = _pack_bf16_pairs(h)
        for ck in range(ROW_CHUNKS):
            h_ref[ck, 0] = hp[:, ck * PACK_CHUNK_W:(ck + 1) * PACK_CHUNK_W]
        logits = _dot(h.astype(BF16), rt_ref[...])
        lane = lax.broadcasted_iota(jnp.int32, logits.shape, 1).astype(F32)
        lg = jnp.where(lane < N_EXPERTS, logits, -jnp.inf)
        m1 = jnp.max(lg, axis=-1, keepdims=True)
        i1 = jnp.min(jnp.where(lg == m1, lane, float(LANES)), axis=-1, keepdims=True)
        lg2 = jnp.where(lane == i1, -jnp.inf, lg)
        m2 = jnp.max(lg2, axis=-1, keepdims=True)
        i2 = jnp.min(jnp.where(lg2 == m2, lane, float(LANES)), axis=-1, keepdims=True)
        e2 = jnp.exp(m2 - m1)
        g1 = 1.0 / (1.0 + e2)
        g2 = e2 / (1.0 + e2)
        first = (pl.program_id(0) == 0) & (pl.program_id(1) == 0)

        @pl.when(first)
        def _():
            cnt_sc[...] = jnp.zeros(cnt_sc.shape, F32)

        hit1 = lane == i1
        hit2 = lane == i2
        onehot = jnp.where(hit1 | hit2, 1.0, 0.0)
        prefix = _dot(tri_ref[...], onehot.astype(BF16)) + cnt_sc[0:1, :]
        r1 = jnp.sum(jnp.where(hit1, prefix, 0.0), axis=-1, keepdims=True)
        r2 = jnp.sum(jnp.where(hit2, prefix, 0.0), axis=-1, keepdims=True)
        cnt_sc[...] = cnt_sc[...] + jnp.sum(onehot, axis=0, keepdims=True)
        cnt_ref[...] = cnt_sc[...]
        vals = (i1, i2, g1, g2, r1, r2)
        route = jnp.zeros(logits.shape, F32)
        for pos, val in enumerate(vals):
            route = jnp.where(lane == pos, val, route)
        route_ref[0] = route


def _mixout(x, yr, yp, ym, wo, mod_l, gain, router_p):
    B, T, _ = x.shape
    tm = TM_MIX
    has_router = router_p is not None
    const = lambda shape: pl.BlockSpec(shape, lambda b, i: tuple(0 for _ in shape))
    tok = lambda w: pl.BlockSpec((1, tm, w), lambda b, i: (b, i, 0))
    in_specs = [tok(D_MODEL), tok(RWKV_WIDTH), tok(POOL_WIDTH), tok(MLA_HEADS * HEAD_DIM),
                const((D_MODEL, D_MODEL)), pl.BlockSpec((1, 6, D_MODEL), lambda b, i: (b, 0, 0)),
                const((1, D_MODEL))]
    args = [x, yr, yp, ym, wo, mod_l, gain]
    out_specs = [tok(D_MODEL), tok(D_MODEL)]
    out_shape = [jax.ShapeDtypeStruct((B, T, D_MODEL), F32), jax.ShapeDtypeStruct((B, T, D_MODEL), BF16)]
    scratch = []
    if has_router:
        out_specs[1] = pl.BlockSpec((ROW_CHUNKS, 1, tm, PACK_CHUNK_W), lambda b, i: (0, b, i, 0))
        out_shape[1] = jax.ShapeDtypeStruct((ROW_CHUNKS, B, T, PACK_CHUNK_W), jnp.uint32)
        ids = np.arange(tm)
        tri = jnp.asarray(ids[:, None] > ids[None, :], BF16)
        in_specs += [const((D_MODEL, LANES)), const((tm, tm))]
        args += [router_p, tri]
        out_specs += [tok(LANES), const((SUBLANES, LANES))]
        out_shape += [jax.ShapeDtypeStruct((B, T, LANES), F32),
                      jax.ShapeDtypeStruct((SUBLANES, LANES), F32)]
        scratch = [pltpu.VMEM((SUBLANES, LANES), F32)]
    return pl.pallas_call(
        functools.partial(_mixout_kernel, has_router),
        grid=(B, T // tm),
        in_specs=in_specs, out_specs=out_specs, out_shape=out_shape, scratch_shapes=scratch,
        compiler_params=_cparams(("arbitrary", "arbitrary")),
    )(*args)


def _ffn_kernel(h_ref, wg_ref, wu_ref, wo_ref, x_ref, mod_ref, o_ref, acc):
    j = pl.program_id(1)

    @pl.when(j == 0)
    def _():
        acc[...] = jnp.zeros(acc.shape, F32)

    h = h_ref[...]
    gg = _dot(h, wg_ref[...])
    uu = _dot(h, wu_ref[...])
    acc[...] += _dot((_silu(gg) * uu).astype(BF16), wo_ref[...])

    @pl.when(j == pl.num_programs(1) - 1)
    def _():
        o_ref[...] = x_ref[...] + mod_ref[0][5:6] * acc[...]


def _ffn(h2, w_in, w_out, x, mod_l):
    N = h2.shape[0]
    T = N // mod_l.shape[0]
    tm, tf = TM_FFN, TF_FFN
    nf = D_FF // tf
    per_b = T // tm
    return pl.pallas_call(
        _ffn_kernel,
        grid=(N // tm, nf),
        in_specs=[pl.BlockSpec((tm, D_MODEL), lambda i, j: (i, 0)),
                  pl.BlockSpec((D_MODEL, tf), lambda i, j: (0, j)),
                  pl.BlockSpec((D_MODEL, tf), lambda i, j: (0, j + nf)),
                  pl.BlockSpec((tf, D_MODEL), lambda i, j: (j, 0)),
                  pl.BlockSpec((tm, D_MODEL), lambda i, j: (i, 0)),
                  pl.BlockSpec((1, 6, D_MODEL), lambda i, j: (i // per_b, 0, 0))],
        out_specs=pl.BlockSpec((tm, D_MODEL), lambda i, j: (i, 0)),
        out_shape=jax.ShapeDtypeStruct((N, D_MODEL), F32),
        scratch_shapes=[pltpu.VMEM((tm, D_MODEL), F32)],
        compiler_params=_cparams(("arbitrary", "arbitrary")),
    )(h2, w_in, w_in, w_out, x, mod_l)


def _moe_kernel(be_ref, nv_ref, last_ref, x_ref, wg_ref, wu_ref, wo_ref, o_ref, acc, xm):
    i = pl.program_id(0)
    j = pl.program_id(1)

    @pl.when(i <= last_ref[0])
    def _():
        @pl.when(j == 0)
        def _():
            acc[...] = jnp.zeros(acc.shape, F32)
            row = lax.broadcasted_iota(jnp.int32, (xm.shape[0], 1), 0)
            xp = jnp.concatenate([x_ref[ck] for ck in range(ROW_CHUNKS)], axis=1)
            xp = jnp.where(row < nv_ref[i], xp, jnp.uint32(0))
            xm[...] = _unpack_bf16_pairs(xp)

        def swiglu_rows(n_rows):
            x = xm[0:n_rows, :]
            gg = _dot(x, wg_ref[0].astype(BF16))
            uu = _dot(x, wu_ref[0].astype(BF16))
            acc[0:n_rows, :] += _dot((_silu(gg) * uu).astype(BF16), wo_ref[0].astype(BF16))

        half = xm.shape[0] // 2

        @pl.when(nv_ref[i] > half)
        def _():
            swiglu_rows(xm.shape[0])

        @pl.when(nv_ref[i] <= half)
        def _():
            swiglu_rows(half)

        @pl.when(j == pl.num_programs(1) - 1)
        def _():
            yp = _pack_bf16_pairs(acc[...])
            for ck in range(ROW_CHUNKS):
                o_ref[ck] = yp[:, ck * PACK_CHUNK_W:(ck + 1) * PACK_CHUNK_W]


def _moe_experts(xs, w_in, w_out, block_exp, n_valid, last_blk):
    n_rows = xs.shape[1]
    tm, tf = MOE_BLOCK, TF_MOE
    nf = D_FF_EXPERT // tf
    blk = lambda i, last: jnp.minimum(i, last[0])
    chunk = lambda i, j, last: jnp.where(i <= last[0], j, nf - 1)
    grid_spec = pltpu.PrefetchScalarGridSpec(
        num_scalar_prefetch=3,
        grid=(n_rows // tm, nf),
        in_specs=[pl.BlockSpec((ROW_CHUNKS, tm, PACK_CHUNK_W),
                               lambda i, j, be, nv, last: (0, blk(i, last), 0)),
                  pl.BlockSpec((1, D_MODEL, tf),
                               lambda i, j, be, nv, last: (be[blk(i, last)], 0, chunk(i, j, last))),
                  pl.BlockSpec((1, D_MODEL, tf),
                               lambda i, j, be, nv, last: (be[blk(i, last)], 0, chunk(i, j, last) + nf)),
                  pl.BlockSpec((1, tf, D_MODEL),
                               lambda i, j, be, nv, last: (be[blk(i, last)], chunk(i, j, last), 0))],
        out_specs=pl.BlockSpec((ROW_CHUNKS, tm, PACK_CHUNK_W), lambda i, j, be, nv, last: (0, blk(i, last), 0)),
        scratch_shapes=[pltpu.VMEM((tm, D_MODEL), F32), pltpu.VMEM((tm, D_MODEL), BF16)])
    return pl.pallas_call(
        _moe_kernel,
        grid_spec=grid_spec,
        out_shape=jax.ShapeDtypeStruct((ROW_CHUNKS, n_rows, PACK_CHUNK_W), jnp.uint32),
        compiler_params=_cparams(("arbitrary", "arbitrary")),
    )(block_exp, n_valid, last_blk, xs, w_in, w_in, w_out)


def _sc_mesh():
    return plsc.VectorSubcoreMesh(core_axis_name="c", subcore_axis_name="s")


def _sc_scatter_rows(x, dest, n_rows):
    N, D = x.shape
    K = dest.shape[0]
    win = SC_WINDOW

    @pl.kernel(out_type=jax.ShapeDtypeStruct((n_rows, D), x.dtype), mesh=_sc_mesh(), scratch_types=[])
    def scatter(x_hbm, d_hbm, o_hbm):
        def body(x_vmem, *idx_vmem):
            for iv in idx_vmem:
                pltpu.sync_copy(x_vmem, o_hbm.at[iv.at[0]])

        pltpu.emit_pipeline(
            body,
            grid=(N // win,),
            in_specs=[pl.BlockSpec((win, D), lambda i: (i, 0))]
            + [pl.BlockSpec((1, win), functools.partial(lambda k, i: (k, i), k)) for k in range(K)],
            out_specs=[],
            core_axis_name=("c", "s"),
            dimension_semantics=(pltpu.PARALLEL,),
        )(x_hbm, *([d_hbm] * K))

    return scatter(x, dest)


def _sc_gather_rows(x, idx):
    n = idx.shape[0]
    D = x.shape[1]
    win = SC_WINDOW

    @pl.kernel(out_type=jax.ShapeDtypeStruct((n, D), x.dtype), mesh=_sc_mesh(), scratch_types=[])
    def gather(x_hbm, i_hbm, o_hbm):
        def body(i_vmem, o_vmem):
            pltpu.sync_copy(x_hbm.at[i_vmem.at[0]], o_vmem)

        pltpu.emit_pipeline(
            body,
            grid=(n // win,),
            in_specs=[pl.BlockSpec((1, win), lambda i: (0, i))],
            out_specs=[pl.BlockSpec((win, D), lambda i: (i, 0))],
            core_axis_name=("c", "s"),
            dimension_semantics=(pltpu.PARALLEL,),
        )(i_hbm, o_hbm)

    return gather(x, idx.reshape(1, n))


def _combine_kernel(x_ref, ya_ref, yb_ref, route_ref, mod_ref, o_ref):
    rt = route_ref[...]
    ya = _unpack_bf16_pairs(jnp.concatenate([ya_ref[0, ck] for ck in range(ROW_CHUNKS)], axis=1))
    yb = _unpack_bf16_pairs(jnp.concatenate([yb_ref[0, ck] for ck in range(ROW_CHUNKS)], axis=1))
    f = rt[:, 2:3] * ya.astype(F32) + rt[:, 3:4] * yb.astype(F32)
    o_ref[...] = x_ref[...] + mod_ref[0][5:6] * f


def _combine(x, y2, route, mod_l):
    N = x.shape[0]
    T = N // mod_l.shape[0]
    tm = 1024
    per_b = T // tm
    tok = pl.BlockSpec((tm, D_MODEL), lambda i: (i, 0))
    slot = lambda k: pl.BlockSpec((1, ROW_CHUNKS, tm, PACK_CHUNK_W), lambda i: (k, 0, i, 0))
    return pl.pallas_call(
        _combine_kernel,
        grid=(N // tm,),
        in_specs=[tok, slot(0), slot(1),
                  pl.BlockSpec((tm, LANES), lambda i: (i, 0)),
                  pl.BlockSpec((1, 6, D_MODEL), lambda i: (i // per_b, 0, 0))],
        out_specs=tok,
        out_shape=jax.ShapeDtypeStruct((N, D_MODEL), F32),
        compiler_params=_cparams(("arbitrary",)),
    )(x, y2, y2, route, mod_l)


def _moe(h2, route, counts, w_in, w_out, x, mod_l):
    N = x.shape[0]
    blk = MOE_BLOCK
    cnt = counts[0, :N_EXPERTS].astype(jnp.int32)
    padded = (cnt + blk - 1) // blk * blk
    pend = jnp.cumsum(padded)
    pstart = pend - padded
    e = route[:, 0:TOP_K].astype(jnp.int32)
    rank = route[:, 2 * TOP_K:3 * TOP_K].astype(jnp.int32)
    dest = (jnp.take(pstart, e) + rank).T
    n_blocks = N * TOP_K // blk + N_EXPERTS
    bstart = jnp.arange(n_blocks, dtype=jnp.int32) * blk
    block_exp = jnp.minimum(jnp.sum((bstart[:, None] >= pend[None, :]).astype(jnp.int32), axis=1),
                            N_EXPERTS - 1)
    n_valid = jnp.clip(cnt[block_exp] - (bstart - pstart[block_exp]), 0, blk).astype(jnp.int32)
    n_rows = n_blocks * blk
    dest_ck = dest[:, None, :] + (jnp.arange(ROW_CHUNKS, dtype=jnp.int32) * n_rows)[None, :, None]
    xs = _sc_scatter_rows(h2, dest_ck.reshape(TOP_K, ROW_CHUNKS * N), ROW_CHUNKS * n_rows)
    last_blk = (pend[-1:] // blk - 1).astype(jnp.int32)
    yb = _moe_experts(xs.reshape(ROW_CHUNKS, n_rows, PACK_CHUNK_W), w_in, w_out, block_exp, n_valid,
                      last_blk)
    y2 = _sc_gather_rows(yb.reshape(ROW_CHUNKS * n_rows, PACK_CHUNK_W), dest_ck.reshape(-1))
    return _combine(x, y2.reshape(TOP_K, ROW_CHUNKS, N, PACK_CHUNK_W), route, mod_l)


def _layout_w_in(w, has_vres):
    W = RWKV_WIDTH
    off_gd = 3 * W + RWKV_DECAY_LORA + RWKV_ICLR_LORA
    off_pool = off_gd + RWKV_GATE_LORA
    off_q = off_pool + POOL_WIDTH
    off_kv = off_q + MLA_Q_LORA
    off_kr = off_kv + MLA_KV_LORA
    n_base = off_kr + MLA_QK_ROPE
    d = w.shape[0]
    zeros = lambda n: jnp.zeros((d, n), w.dtype)
    vd = w[:, n_base:n_base + RWKV_VRES_LORA] if has_vres else zeros(RWKV_VRES_LORA)
    cols = [w[:, :off_gd], w[:, off_gd:off_pool], vd, zeros(ZR_COLS - off_pool - RWKV_VRES_LORA),
            w[:, off_pool:off_q], w[:, off_q:off_kv], w[:, off_kv:off_kr],
            zeros(MLA_QK_NOPE), w[:, off_kr:n_base], zeros(LANES - MLA_QK_DIM)]
    return jnp.concatenate(cols, axis=1).astype(BF16)


def _pad_heads(w, per_head, keep_from, keep_n):
    K = w.shape[0]
    wh = w.reshape(K, MLA_HEADS, per_head)[:, :, keep_from:keep_from + keep_n]
    wh = jnp.pad(wh, ((0, 0), (0, 0), (0, LANES - keep_n)))
    return wh.reshape(K, MLA_HEADS * LANES)


def kernel(x, c, positions, w_ada, b_ada, norm_gain, w_in_first, w_in_rest, mu_shift, mu_shift_v,
           rwkv_vec, rwkv_v0, rwkv_w2, rwkv_a2, rwkv_g2, rwkv_v2, pool_w, pool_scale,
           mla_q_lat_gain, mla_kv_lat_gain, mla_wq_up, mla_wkv_up, mla_qk_gain, w_out, ffn_w_in,
           ffn_w_out, moe_router, moe_w_in, moe_w_out):
    B, T, D = x.shape
    depth = w_ada.shape[0]
    W = RWKV_WIDTH
    mod = _adaln(c, w_ada, b_ada).reshape(depth, B, 6, D)
    pos3 = positions.reshape(B, T, 1)
    inv_freq = ROPE_BASE ** (-jnp.arange(0, MLA_QK_ROPE, 2, dtype=F32) / MLA_QK_ROPE)
    freq = jnp.concatenate([jnp.zeros((MLA_QK_NOPE,), F32), inv_freq, inv_freq,
                            jnp.zeros((LANES - MLA_QK_DIM,), F32)]).reshape(1, LANES)
    cosf, sinf = _rope_tables(pos3, freq)
    hid = np.arange(LANES) // HEAD_DIM
    bd64 = jnp.asarray(hid[:, None] == hid[None, :], BF16)
    bd128 = jnp.ones((LANES, LANES), BF16)

    v_first = None
    for l in range(depth):
        has_vres = l > 0
        mod_l = mod[l]
        win = _layout_w_in(w_in_first if l == 0 else w_in_rest[l - 1], has_vres)
        poolw = jax.scipy.linalg.block_diag(*[pool_w[l, g] for g in range(len(POOL_WINDOWS))]).astype(BF16)
        wq = _pad_heads(mla_wq_up[l], MLA_QK_DIM, 0, MLA_QK_DIM).astype(BF16)
        wk = _pad_heads(mla_wkv_up[l], MLA_QK_NOPE + HEAD_DIM, 0, MLA_QK_NOPE).astype(BF16)
        wv = mla_wkv_up[l].reshape(MLA_KV_LORA, MLA_HEADS, MLA_QK_NOPE + HEAD_DIM)[:, :, MLA_QK_NOPE:]
        wv = wv.reshape(MLA_KV_LORA, MLA_HEADS * HEAD_DIM).astype(BF16)
        qkg = jnp.tile(jnp.pad(mla_qk_gain[l], ((0, 0), (0, LANES - MLA_QK_DIM))), (1, MLA_HEADS))
        zr, y_pool, q, k, v = _mixin(
            x, mod_l, norm_gain[l, 0].reshape(1, D), win, cosf, sinf, poolw,
            pool_scale[l].reshape(1, -1), mla_q_lat_gain[l].reshape(1, -1),
            mla_kv_lat_gain[l].reshape(1, -1), wq, wk, wv, qkg, bd128)

        pad_mu = ZR_COLS - mu_shift.shape[1] - RWKV_VRES_LORA
        mu_v = mu_shift_v[l - 1] if has_vres else jnp.zeros((RWKV_VRES_LORA,), F32)
        mu = jnp.concatenate([mu_shift[l], mu_v, jnp.zeros((pad_mu,), F32)]).reshape(1, ZR_COLS)
        v0 = rwkv_v0[l - 1] if has_vres else jnp.zeros((W,), F32)
        vec8 = jnp.concatenate([rwkv_vec[l], v0[None]], axis=0)
        w2a2 = jax.scipy.linalg.block_diag(rwkv_w2[l], rwkv_a2[l]).astype(BF16)
        g2 = jnp.pad(rwkv_g2[l], ((0, 2 * LANES - RWKV_GATE_LORA), (0, 0)))
        if has_vres:
            v2 = jnp.pad(rwkv_v2[l - 1], ((RWKV_GATE_LORA, 2 * LANES - RWKV_GATE_LORA - RWKV_VRES_LORA), (0, 0)))
        else:
            v2 = jnp.zeros((2 * LANES, W), F32)
        g2v2 = jnp.concatenate([g2, v2], axis=1).astype(BF16)
        y_rwkv, v_first = _rwkv(zr, v_first, mu, vec8, w2a2, g2v2, bd64)

        y_mla = _attention(q, k, v, mla_qk_gain[l])

        is_moe = (l % 2 == 1)
        router_p = None
        if is_moe:
            router_p = jnp.pad(moe_router[l // 2], ((0, 0), (0, LANES - N_EXPERTS))).astype(BF16)
        outs = _mixout(x, y_rwkv, y_pool, y_mla, w_out[l].astype(BF16), mod_l,
                       norm_gain[l, 1].reshape(1, D), router_p)
        x_mid, h2 = outs[0], outs[1]
        xf = x_mid.reshape(B * T, D)
        if is_moe:
            xo = _moe(h2.reshape(ROW_CHUNKS * B * T, PACK_CHUNK_W), outs[2].reshape(B * T, LANES), outs[3],
                      moe_w_in[l // 2], moe_w_out[l // 2], xf, mod_l)
        else:
            xo = _ffn(h2.reshape(B * T, D), ffn_w_in[l // 2].astype(BF16), ffn_w_out[l // 2].astype(BF16),
                      xf, mod_l)
        x = xo.reshape(B, T, D)
    return x
```

```python
import functools

import numpy as np
import jax
import jax.numpy as jnp
from jax import lax
from jax.experimental import pallas as pl
from jax.experimental.pallas import tpu as pltpu
from jax.experimental.pallas import tpu_sc as plsc

F32 = jnp.float32
BF16 = jnp.bfloat16

D_MODEL = 1024
HEAD_DIM = 64
RWKV_WIDTH = 512
RWKV_HEADS = 8
POOL_WIDTH = 256
POOL_WINDOWS = (2, 4, 8, 16)
POOL_HALO = 16
MLA_HEADS = 4
MLA_QK_NOPE = 64
MLA_QK_ROPE = 32
MLA_QK_DIM = 96
MLA_Q_LORA = 256
MLA_KV_LORA = 128
ROPE_BASE = 10000.0
RWKV_DECAY_LORA = 64
RWKV_ICLR_LORA = 64
RWKV_VRES_LORA = 32
RWKV_GATE_LORA = 160
RWKV_LNX_EPS = 64e-5
D_FF = 2816
N_EXPERTS = 8
TOP_K = 2
D_FF_EXPERT = 3584
NORM_EPS = 1e-6
NEG_INF = -1e30

LANES = 128
SUBLANES = 8
VMEM_LIMIT = 56 * 1024 * 1024

ZR_COLS = 1920
Z_POOL_OFF = ZR_COLS
Z_QLAT_OFF = Z_POOL_OFF + POOL_WIDTH
Z_KVLAT_OFF = Z_QLAT_OFF + MLA_Q_LORA
Z_KROPE_OFF = Z_KVLAT_OFF + MLA_KV_LORA
Z_COLS = Z_KROPE_OFF + LANES

MIXIN_PIECE_W = 512
TM_MIX = 512
WKV_CHUNK = 64
TQ = 512
ATTN_BOUND_SLACK = 1.02
ATTN_BOUND_MAX = 40.0
TM_FFN = 1024
TF_FFN = 1408
MOE_BLOCK = 1024
TF_MOE = 512
SC_WINDOW = 128
ROW_CHUNKS = 4
ROW_CHUNK_W = D_MODEL // ROW_CHUNKS
PACK_CHUNK_W = ROW_CHUNK_W // 2

SEGSUM_SPLITS = 1

NN = (((1,), (0,)), ((), ()))
NT = (((1,), (1,)), ((), ()))


def _dot(a, b, dims=NN):
    return lax.dot_general(a, b, dims, preferred_element_type=F32)


def _split2(a):
    hi = a.astype(BF16)
    lo = (a - hi.astype(F32)).astype(BF16)
    return hi, lo


def _mm(a, b, dims=NN, passes=3):
    if passes == 1:
        return _dot(a.astype(BF16), b.astype(BF16), dims)
    ah, al = _split2(a)
    bh, bl = _split2(b)
    return _dot(ah, bh, dims) + (_dot(ah, bl, dims) + _dot(al, bh, dims))


def _mm_exact_rhs(a, b_bf16, splits=SEGSUM_SPLITS):
    m, w = a.shape
    nb = w // LANES
    stacked = jnp.concatenate([a[:, i * LANES:(i + 1) * LANES] for i in range(nb)], axis=0)
    out = None
    rem = stacked
    for s in range(splits):
        part = rem.astype(BF16)
        term = _dot(part, b_bf16)
        out = term if out is None else out + term
        if s + 1 < splits:
            rem = rem - part.astype(F32)
    return jnp.concatenate([out[i * m:(i + 1) * m] for i in range(nb)], axis=1)


def _pack_bf16_pairs(h):
    w = h.shape[1] // 2
    lo = lax.bitcast_convert_type(h[:, :w].astype(BF16).astype(F32), jnp.uint32)
    hi = lax.bitcast_convert_type(h[:, w:].astype(BF16).astype(F32), jnp.uint32)
    return (lo >> 16) | (hi & jnp.uint32(0xFFFF0000))


def _unpack_bf16_pairs(p):
    lo = lax.bitcast_convert_type(p << 16, F32)
    hi = lax.bitcast_convert_type(p & jnp.uint32(0xFFFF0000), F32)
    return jnp.concatenate([lo, hi], axis=1).astype(BF16)


def _sigmoid(x):
    return 1.0 / (1.0 + jnp.exp(-x))


def _silu(x):
    return x * _sigmoid(x)


def _rms(x, eps=NORM_EPS):
    return x * lax.rsqrt(jnp.mean(x * x, axis=-1, keepdims=True) + eps)


def _cparams(sem):
    return pltpu.CompilerParams(dimension_semantics=sem, vmem_limit_bytes=VMEM_LIMIT)


def _adaln_kernel(c_ref, w_ref, b_ref, o_ref):
    ca = _silu(c_ref[...])
    o_ref[0] = _mm(ca, w_ref[0]) + b_ref[0]


def _adaln(c, w_ada, b_ada):
    L = w_ada.shape[0]
    B = c.shape[0]
    n = w_ada.shape[2] // D_MODEL
    return pl.pallas_call(
        _adaln_kernel,
        grid=(L, n),
        in_specs=[pl.BlockSpec((B, D_MODEL), lambda l, j: (0, 0)),
                  pl.BlockSpec((1, D_MODEL, D_MODEL), lambda l, j: (l, 0, j)),
                  pl.BlockSpec((1, 1, D_MODEL), lambda l, j: (l, 0, j))],
        out_specs=pl.BlockSpec((1, B, D_MODEL), lambda l, j: (l, 0, j)),
        out_shape=jax.ShapeDtypeStruct((L, B, n * D_MODEL), F32),
        compiler_params=_cparams(("arbitrary", "arbitrary")),
    )(c, w_ada, b_ada.reshape(L, 1, -1))


def _rope(x, cosf, sinf, lane):
    up = pltpu.roll(x, LANES - MLA_QK_ROPE // 2, axis=1)
    dn = pltpu.roll(x, MLA_QK_ROPE // 2, axis=1)
    rot = jnp.where(lane < MLA_QK_NOPE + MLA_QK_ROPE // 2, -up, dn)
    return x * cosf + rot * sinf


def _rope_kernel(pos_ref, freq_ref, cos_ref, sin_ref):
    tm = pos_ref.shape[1]
    lane = lax.broadcasted_iota(jnp.int32, (tm, LANES), 1)
    in_rope = (lane >= MLA_QK_NOPE) & (lane < MLA_QK_DIM)
    ang = pos_ref[0].astype(F32) * freq_ref[...]
    cos_ref[0] = jnp.where(in_rope, jnp.cos(ang), 1.0)
    sin_ref[0] = jnp.where(in_rope, jnp.sin(ang), 0.0)


def _rope_tables(pos3, freq):
    B, T, _ = pos3.shape
    tm = TM_MIX
    tok = lambda w: pl.BlockSpec((1, tm, w), lambda b, i: (b, i, 0))
    return pl.pallas_call(
        _rope_kernel,
        grid=(B, T // tm),
        in_specs=[tok(1), pl.BlockSpec((1, LANES), lambda b, i: (0, 0))],
        out_specs=[tok(LANES), tok(LANES)],
        out_shape=[jax.ShapeDtypeStruct((B, T, LANES), F32)] * 2,
        compiler_params=_cparams(("arbitrary", "arbitrary")),
    )(pos3, freq)


def _mixin_kernel(x_ref, mod_ref, gain_ref, win_ref, cos_ref, sin_ref, poolw_ref, pools_ref,
                  qg_ref, kvg_ref, wq_ref, wk_ref, wv_ref, qkg_ref, bd_ref,
                  zr_ref, yp_ref, q_ref, k_ref, v_ref, ubuf):
    i = pl.program_id(1)
    tm = x_ref.shape[1]
    x = x_ref[0]
    mod = mod_ref[0]
    @pl.when(i == 0)
    def _():
        ubuf[0:POOL_HALO, :] = jnp.zeros((POOL_HALO, POOL_WIDTH), F32)

    h = (_rms(x) * gain_ref[...] * (1.0 + mod[1:2]) + mod[0:1]).astype(BF16)
    zb = _dot(h, win_ref[:, ZR_COLS:])
    zcol = lambda off, w: zb[:, off - ZR_COLS:off - ZR_COLS + w]
    def project_piece(n):
        cols = slice(n * MIXIN_PIECE_W, min((n + 1) * MIXIN_PIECE_W, ZR_COLS))
        zr_ref[0, :, cols] = _dot(h, win_ref[:, cols])

    project_piece(0)
    u = zcol(Z_POOL_OFF, POOL_WIDTH)
    ubuf[POOL_HALO:, :] = u
    ue = ubuf[...]
    s2 = ue + pltpu.roll(ue, 1, axis=0)
    s4 = s2 + pltpu.roll(s2, 2, axis=0)
    s8 = s4 + pltpu.roll(s4, 4, axis=0)
    s16 = s8 + pltpu.roll(s8, 8, axis=0)
    ubuf[0:POOL_HALO, :] = u[tm - POOL_HALO:, :]
    lane_p = lax.broadcasted_iota(jnp.int32, (tm, POOL_WIDTH), 1)
    grp = lane_p // (POOL_WIDTH // len(POOL_WINDOWS))
    win_sum = jnp.where(grp == 0, s2[POOL_HALO:], jnp.where(grp == 1, s4[POOL_HALO:],
                        jnp.where(grp == 2, s8[POOL_HALO:], s16[POOL_HALO:])))
    win = jnp.where(grp == 0, 2, jnp.where(grp == 1, 4, jnp.where(grp == 2, 8, 16)))
    t_abs = i * tm + lax.broadcasted_iota(jnp.int32, (tm, POOL_WIDTH), 0)
    cnt = jnp.minimum(t_abs + 1, win).astype(F32)
    p = win_sum / cnt - u
    yp = _dot(p.astype(BF16), poolw_ref[...]) * pools_ref[...]
    yp_ref[0] = yp.astype(BF16)
    project_piece(1)

    lane = lax.broadcasted_iota(jnp.int32, (tm, LANES), 1)
    cosf = cos_ref[0]
    sinf = sin_ref[0]

    q_lat = zcol(Z_QLAT_OFF, MLA_Q_LORA)
    kv_lat = zcol(Z_KVLAT_OFF, MLA_KV_LORA)
    k_rope = zcol(Z_KROPE_OFF, LANES)
    qn = (_rms(q_lat) * qg_ref[...]).astype(BF16)
    kvn = (_rms(kv_lat) * kvg_ref[...]).astype(BF16)
    q = _dot(qn, wq_ref[...])
    kx = _dot(kvn, wk_ref[...])
    v = _dot(kvn, wv_ref[...])
    v_ref[0] = v.astype(BF16)
    project_piece(2)
    k_pe = _rope(k_rope, cosf, sinf, lane)
    qs, ks = [], []
    for hd in range(MLA_HEADS):
        sl = slice(hd * LANES, (hd + 1) * LANES)
        qs.append(_rope(q[:, sl], cosf, sinf, lane))
        ks.append(kx[:, sl] + k_pe)
    q = jnp.concatenate(qs, axis=1)
    k = jnp.concatenate(ks, axis=1)
    project_piece(3)
    qss = _mm_exact_rhs(q * q, bd_ref[...]) * (1.0 / MLA_QK_DIM)
    kss = _mm_exact_rhs(k * k, bd_ref[...]) * (1.0 / MLA_QK_DIM)
    qkg = qkg_ref[...]
    q = q * lax.rsqrt(qss + NORM_EPS) * qkg[0:1] * (MLA_QK_DIM ** -0.5)
    k = k * lax.rsqrt(kss + NORM_EPS) * qkg[1:2]
    q_ref[0] = q.astype(BF16)
    k_ref[0] = k.astype(BF16)


def _mixin(x, mod_l, gain, win, cosf, sinf, poolw, pools, qg, kvg, wq, wk, wv, qkg, bd128):
    B, T, _ = x.shape
    tm = TM_MIX
    const = lambda shape: pl.BlockSpec(shape, lambda b, i: tuple(0 for _ in shape))
    tok = lambda w: pl.BlockSpec((1, tm, w), lambda b, i: (b, i, 0))
    return pl.pallas_call(
        _mixin_kernel,
        grid=(B, T // tm),
        in_specs=[tok(D_MODEL),
                  pl.BlockSpec((1, 6, D_MODEL), lambda b, i: (b, 0, 0)),
                  const((1, D_MODEL)), const((D_MODEL, Z_COLS)),
                  tok(LANES), tok(LANES),
                  const((POOL_WIDTH, POOL_WIDTH)), const((1, POOL_WIDTH)),
                  const((1, MLA_Q_LORA)), const((1, MLA_KV_LORA)),
                  const((MLA_Q_LORA, MLA_HEADS * LANES)), const((MLA_KV_LORA, MLA_HEADS * LANES)),
                  const((MLA_KV_LORA, MLA_HEADS * HEAD_DIM)), const((2, MLA_HEADS * LANES)),
                  const((LANES, LANES))],
        out_specs=[tok(ZR_COLS), tok(POOL_WIDTH), tok(MLA_HEADS * LANES), tok(MLA_HEADS * LANES),
                   tok(MLA_HEADS * HEAD_DIM)],
        out_shape=[jax.ShapeDtypeStruct((B, T, ZR_COLS), F32),
                   jax.ShapeDtypeStruct((B, T, POOL_WIDTH), BF16),
                   jax.ShapeDtypeStruct((B, T, MLA_HEADS * LANES), BF16),
                   jax.ShapeDtypeStruct((B, T, MLA_HEADS * LANES), BF16),
                   jax.ShapeDtypeStruct((B, T, MLA_HEADS * HEAD_DIM), BF16)],
        scratch_shapes=[pltpu.VMEM((POOL_HALO + tm, POOL_WIDTH), F32)],
        compiler_params=_cparams(("arbitrary", "arbitrary")),
    )(x, mod_l, gain, win, cosf, sinf, poolw, pools, qg, kvg, wq, wk, wv, qkg, bd128)


WKV_PASSES_SCORE = 1
WKV_PASSES_INV = 1
WKV_PASSES_APPLY = 1
WKV_PASSES_STATE = 1
WKV_STEP_CHUNKS = 1
WKV_STEP_SEQS = 8
WKV_GROUPS = 2
WKV_GROUP_LEAD = 4


def _stack_heads(xp, lane):
    return jnp.concatenate([jnp.where(lane < HEAD_DIM, xp, 0.0),
                            jnp.where(lane >= HEAD_DIM, xp, 0.0)], axis=0)


def _wkv_prep(r, lw, k, v, kk, a, tri, masks):
    L = r[0].shape[0]
    nc = len(r)
    each = lambda f, *ls: [f(*xs) for xs in zip(*ls)]
    lane = lax.broadcasted_iota(jnp.int32, (L, LANES), 1)
    stack = lambda x: _stack_heads(x, lane)
    cum = each(lambda x: _mm_exact_rhs_left(tri, x), lw)
    cum_last = each(lambda c: c[L - 1:L, :], cum)
    e_w = each(jnp.exp, cum)
    e_wm = each(lambda c, x: jnp.exp(c - x), cum, lw)
    e_iw = each(lambda c: jnp.exp(-c), cum)
    e_d = each(lambda cl, c: jnp.exp(cl - c), cum_last, cum)
    beta = each(lambda x, y: x * y, kk, a)
    r_f = each(lambda x, e: x * e, r, e_w)
    a_f = each(lambda x, e: -x * e, kk, e_wm)
    a_s = each(stack, a_f)
    b_s = each(lambda x, e: stack(x * e), beta, e_iw)
    k_s = each(lambda x, e: stack(x * e), k, e_iw)
    b_d = each(lambda x, e: stack(x * e), beta, e_d)
    k_d = each(lambda x, e: stack(x * e), k, e_d)
    v_s = each(stack, v)
    yield
    g = each(lambda af, rf, bs, ks: _mm(jnp.concatenate([af, rf], axis=0),
                                        jnp.concatenate([bs, ks], axis=0), NT, WKV_PASSES_SCORE),
             a_f, r_f, b_s, k_s)
    strict, incl, levels = masks
    a_ab = each(lambda x: jnp.where(strict, x[:L, :LANES], 0.0), g)
    a_ak = each(lambda x: jnp.where(strict, x[:L, LANES:], 0.0), g)
    s_rb = each(lambda x: jnp.where(incl, x[L:, :LANES], 0.0), g)
    s_rk = each(lambda x: jnp.where(incl, x[L:, LANES:], 0.0), g)
    eye = jnp.where(levels[0][1], 1.0, 0.0)
    tinv = each(lambda x: eye + jnp.where(levels[0][0], x, 0.0), a_ab)
    yield
    for lvl_mask, _ in levels[1:]:
        et = each(lambda x, t: _mm(jnp.where(lvl_mask, x, 0.0), stack(t), NN, WKV_PASSES_INV), a_ab, tinv)
        tinv = each(lambda t, x: t + _mm(t, stack(x), NN, WKV_PASSES_INV), tinv, et)
        yield
    av = each(lambda x, y: _mm(x, y, NN, WKV_PASSES_APPLY), a_ak, v_s)
    tx = each(lambda t, x, y: _mm(t, jnp.concatenate([x, stack(y)], axis=1), NN, WKV_PASSES_APPLY),
              tinv, a_s, av)
    ta_s = each(lambda x: stack(x[:, :LANES]), tx)
    c1_s = each(lambda x: stack(x[:, LANES:]), tx)
    yield
    ra = each(lambda rf, s, x: rf + _mm(s, x, NN, WKV_PASSES_APPLY), r_f, s_rb, ta_s)
    c2 = each(lambda sb, sk, x, vs: _mm(jnp.concatenate([sb, sk], axis=1),
                                        jnp.concatenate([x, vs], axis=0),
                                        NN, WKV_PASSES_APPLY), s_rb, s_rk, c1_s, v_s)
    yield
    tb = each(lambda x, y, bd: _mm(jnp.concatenate([x, y], axis=1).T, bd, NN, WKV_PASSES_APPLY),
              ta_s, c1_s, b_d)
    c3 = each(lambda x, vs, kd: x[LANES:] + _mm(vs.T, kd, NN, WKV_PASSES_APPLY), tb, v_s, k_d)
    return [(ra[i], c2[i], jnp.exp(cum_last[i]), tb[i][:LANES], c3[i]) for i in range(nc)]


def _mm_exact_rhs_left(tri_bf16, x):
    x0 = x.astype(BF16)
    r1 = x - x0.astype(F32)
    x1 = r1.astype(BF16)
    x2 = (r1 - x1.astype(F32)).astype(BF16)
    return _dot(tri_bf16, x0) + (_dot(tri_bf16, x1) + _dot(tri_bf16, x2))


def _wkv_masks(L):
    row = lax.broadcasted_iota(jnp.int32, (L, 2 * L), 0)
    col = lax.broadcasted_iota(jnp.int32, (L, 2 * L), 1) % L
    strict = row > col
    incl = row >= col
    levels = []
    m = 1
    while m < L:
        same = (row // (2 * m)) == (col // (2 * m))
        lvl = same & ((row % (2 * m)) >= m) & ((col % (2 * m)) < m)
        levels.append((lvl, row == col))
        m *= 2
    return strict, incl, levels


def _rwkv_kernel(has_vres, *refs):
    if has_vres:
        (z_ref, vf_ref, mu_ref, vec_ref, w2a2_ref, g2v2_ref, bd_ref,
         y_ref, carry, state) = refs
    else:
        (z_ref, mu_ref, vec_ref, w2a2_ref, g2v2_ref, bd_ref,
         y_ref, vout_ref, carry, state) = refs
    c = pl.program_id(1)
    n_seq, seq_rows = z_ref.shape[0], z_ref.shape[1]
    rows = n_seq * seq_rows
    L = WKV_CHUNK
    W = RWKV_WIDTH

    @pl.when(c == 0)
    def _():
        carry[...] = jnp.zeros(carry.shape, F32)
        state[...] = jnp.zeros(state.shape, F32)

    flat = lambda ref: jnp.concatenate([ref[s] for s in range(n_seq)], axis=0)
    z = flat(z_ref)
    row = lax.broadcasted_iota(jnp.int32, z.shape, 0)
    prev = pltpu.roll(z, 1, axis=0)
    for s in range(n_seq):
        prev = jnp.where(row == s * seq_rows, carry[s, SUBLANES - 1:SUBLANES, :], prev)
        carry[s] = z[(s + 1) * seq_rows - SUBLANES:(s + 1) * seq_rows, :]
    zs_all = z + mu_ref[...] * (prev - z)
    if has_vres:
        vf_all = flat(vf_ref)
    vec = vec_ref[...]
    w0, a0, k_k, k_a, r_k, ln_g, ln_b, v0 = (vec[j:j + 1] for j in range(8))
    bd = bd_ref[...]
    masks = _wkv_masks(L)
    rowt = lax.broadcasted_iota(jnp.int32, (L, L), 0)
    colt = lax.broadcasted_iota(jnp.int32, (L, L), 1)
    tri = jnp.where(rowt >= colt, 1.0, 0.0).astype(BF16)
    n_pairs = RWKV_HEADS // 2
    S_now = [{(s, p): state[s * n_pairs + p] for s in range(n_seq) for p in range(n_pairs)}]

    def group(s0, s1):
        seqs = range(s0, s1)
        r0, r1 = s0 * seq_rows, s1 * seq_rows
        zs = zs_all[r0:r1]
        r = zs[:, 0:W]
        k = zs[:, W:2 * W]
        v = zs[:, 2 * W:3 * W]
        wa = zs[:, 3 * W:3 * W + LANES]
        gb = zs[:, 3 * W + LANES:ZR_COLS]
        lane_a = lax.broadcasted_iota(jnp.int32, wa.shape, 1)
        t1 = _dot(jnp.where(lane_a < RWKV_DECAY_LORA, jnp.tanh(wa), wa).astype(BF16), w2a2_ref[...])
        lane_g = lax.broadcasted_iota(jnp.int32, gb.shape, 1)
        t2 = _dot(jnp.where(lane_g < RWKV_GATE_LORA, _sigmoid(gb), gb).astype(BF16), g2v2_ref[...])
        yield
        xw = w0 + t1[:, :W]
        w_log = -(jnp.maximum(-xw, 0.0) + jnp.log(1.0 + jnp.exp(-jnp.abs(xw)))) - 0.5
        lw = -jnp.exp(w_log)
        a = _sigmoid(a0 + t1[:, W:])
        g = t2[:, :W]
        if has_vres:
            v = v + (vf_all[r0:r1] - v) * _sigmoid(v0 + t2[:, W:])
        else:
            for s in seqs:
                vout_ref[s] = v[(s - s0) * seq_rows:(s - s0 + 1) * seq_rows]
        kk = k * k_k
        kk = kk * jnp.minimum(lax.rsqrt(_mm_exact_rhs(kk * kk, bd)), 1e12)
        k = k * (1.0 + (a - 1.0) * k_a)
        yield
        n_chunks = seq_rows // L
        idx = [(s, ch, p) for s in seqs for ch in range(n_chunks) for p in range(n_pairs)]
        row0 = lambda s, ch: (s - s0) * seq_rows + ch * L
        cut = lambda x: [x[row0(s, ch):row0(s, ch) + L, p * LANES:(p + 1) * LANES] for s, ch, p in idx]
        res = yield from _wkv_prep(cut(r), cut(lw), cut(k), cut(v), cut(kk), cut(a), tri, masks)
        prep = dict(zip(idx, res))
        yield
        sp = [(s, p) for s in seqs for p in range(n_pairs)]
        S = {key: S_now[0][key] for key in sp}
        y_blk = {}
        for ch in range(n_chunks):
            y_s = {(s, p): _mm(prep[s, ch, p][0], S[s, p], NT, WKV_PASSES_STATE) + prep[s, ch, p][1]
                   for s, p in sp}
            for s in seqs:
                y_blk[s, ch] = jnp.concatenate([y_s[s, p] for p in range(n_pairs)], axis=1)
            S = {(s, p): S[s, p] * prep[s, ch, p][2] + _mm(S[s, p], prep[s, ch, p][3], NN, WKV_PASSES_STATE)
                 + prep[s, ch, p][4] for s, p in sp}
        S_now[0].update(S)
        y = jnp.concatenate([y_blk[s, ch] for s in seqs for ch in range(n_chunks)], axis=0)
        yield
        inv = 1.0 / HEAD_DIM
        mean = _mm_exact_rhs(y, bd) * inv
        yc = y - mean
        var = _mm_exact_rhs(yc * yc, bd) * inv
        yn = yc * lax.rsqrt(var + RWKV_LNX_EPS) * ln_g + ln_b
        bonus = _mm_exact_rhs(r * k * r_k, bd) * v
        out = ((yn + bonus) * g).astype(BF16)
        for s in seqs:
            y_ref[s] = out[(s - s0) * seq_rows:(s - s0 + 1) * seq_rows]

    per = n_seq // WKV_GROUPS
    gens = [group(gi * per, (gi + 1) * per) for gi in range(WKV_GROUPS)]
    alive = [True] * WKV_GROUPS
    tick = 0
    while any(alive):
        for gi in range(WKV_GROUPS):
            if alive[gi] and tick >= gi * WKV_GROUP_LEAD:
                try:
                    next(gens[gi])
                except StopIteration:
                    alive[gi] = False
        tick += 1
    for s in range(n_seq):
        for p in range(n_pairs):
            state[s * n_pairs + p] = S_now[0][s, p]


def _rwkv(zr, v_first, mu, vec8, w2a2, g2v2, bd64):
    B, T, _ = zr.shape
    L = WKV_CHUNK * WKV_STEP_CHUNKS
    ns = WKV_STEP_SEQS
    has_vres = v_first is not None
    const = lambda shape: pl.BlockSpec(shape, lambda b, c: tuple(0 for _ in shape))
    tok = lambda w: pl.BlockSpec((ns, L, w), lambda b, c: (b, c, 0))
    in_specs = [tok(ZR_COLS)]
    args = [zr]
    if has_vres:
        in_specs.append(tok(RWKV_WIDTH))
        args.append(v_first)
    in_specs += [const((1, ZR_COLS)), const((8, RWKV_WIDTH)), const((LANES, 2 * RWKV_WIDTH)),
                 const((2 * LANES, 2 * RWKV_WIDTH)), const((LANES, LANES))]
    args += [mu, vec8, w2a2, g2v2, bd64]
    out_specs = [tok(RWKV_WIDTH)]
    out_shape = [jax.ShapeDtypeStruct((B, T, RWKV_WIDTH), BF16)]
    if not has_vres:
        out_specs.append(tok(RWKV_WIDTH))
        out_shape.append(jax.ShapeDtypeStruct((B, T, RWKV_WIDTH), F32))
    outs = pl.pallas_call(
        functools.partial(_rwkv_kernel, has_vres),
        grid=(B // ns, T // L),
        in_specs=in_specs, out_specs=out_specs, out_shape=out_shape,
        scratch_shapes=[pltpu.VMEM((ns, SUBLANES, ZR_COLS), F32),
                        pltpu.VMEM((ns * (RWKV_HEADS // 2), LANES, LANES), F32)],
        compiler_params=_cparams(("arbitrary", "arbitrary")),
    )(*args)
    return (outs[0], v_first) if has_vres else (outs[0], outs[1])


def _attn_step(q_ref, k_ref, v_ref, m_sc, l_sc, acc_sc, masked):
    tq = q_ref.shape[1]
    tk = k_ref.shape[1]
    lane = lax.broadcasted_iota(jnp.int32, (tq, LANES), 1)
    if masked:
        rowi = lax.broadcasted_iota(jnp.int32, (tq, tk), 0)
        coli = lax.broadcasted_iota(jnp.int32, (tq, tk), 1)
        keep = coli <= rowi
    heads = range(MLA_HEADS)
    s = [_dot(q_ref[0, :, hd * LANES:(hd + 1) * LANES], k_ref[0, :, hd * LANES:(hd + 1) * LANES], NT)
         for hd in heads]
    if masked:
        s = [jnp.where(keep, x, NEG_INF) for x in s]
    m_prev = [m_sc[hd] for hd in heads]
    m_new = [jnp.maximum(m_prev[hd], jnp.max(s[hd], axis=-1, keepdims=True)) for hd in heads]
    alpha = [jnp.exp(m_prev[hd] - m_new[hd]) for hd in heads]
    p = [jnp.exp(s[hd] - jnp.concatenate([m_new[hd]] * (tk // LANES), axis=1)) for hd in heads]
    for hd in heads:
        l_sc[hd] = alpha[hd] * l_sc[hd] + jnp.sum(p[hd], axis=-1, keepdims=True)
        m_sc[hd] = m_new[hd]
    pv = [_dot(p[hd].astype(BF16), v_ref[0, :, (hd // 2) * LANES:(hd // 2 + 1) * LANES]) for hd in heads]
    first = lane < HEAD_DIM
    for pr in range(MLA_HEADS // 2):
        acc_sc[pr] = (acc_sc[pr] * jnp.where(first, alpha[2 * pr], alpha[2 * pr + 1])
                      + jnp.where(first, pv[2 * pr], pv[2 * pr + 1]))


def _attn_step_bounded(q_ref, k_ref, v_ref, cb_ref, l_sc, acc_sc, masked):
    tq = q_ref.shape[1]
    tk = k_ref.shape[1]
    lane = lax.broadcasted_iota(jnp.int32, (tq, LANES), 1)
    if masked:
        rowi = lax.broadcasted_iota(jnp.int32, (tq, tk), 0)
        coli = lax.broadcasted_iota(jnp.int32, (tq, tk), 1)
        keep = coli <= rowi
    ps = []
    for hd in range(MLA_HEADS):
        s = _dot(q_ref[0, :, hd * LANES:(hd + 1) * LANES], k_ref[0, :, hd * LANES:(hd + 1) * LANES], NT)
        c = cb_ref[0, hd:hd + 1, :]
        p = jnp.exp(s - jnp.concatenate([c] * (tk // LANES), axis=1))
        if masked:
            p = jnp.where(keep, p, 0.0)
        part = p[:, 0:LANES]
        for t in range(1, tk // LANES):
            part = part + p[:, t * LANES:(t + 1) * LANES]
        l_sc[hd] = l_sc[hd] + part
        ps.append(p.astype(BF16))
    first = lane < HEAD_DIM
    for pr in range(MLA_HEADS // 2):
        pv = _dot(jnp.concatenate(ps[2 * pr:2 * pr + 2], axis=0), v_ref[0, :, pr * LANES:(pr + 1) * LANES])
        acc_sc[pr] = acc_sc[pr] + jnp.where(first, pv[:tq], pv[tq:])


def _attn_finish(o_ref, l_sc, acc_sc, lane_partial):
    tq = o_ref.shape[1]
    lane = lax.broadcasted_iota(jnp.int32, (tq, LANES), 1)
    outs = []
    for pr in range(MLA_HEADS // 2):
        la, lb = l_sc[2 * pr], l_sc[2 * pr + 1]
        if lane_partial:
            la = jnp.sum(la, axis=-1, keepdims=True)
            lb = jnp.sum(lb, axis=-1, keepdims=True)
        outs.append(acc_sc[pr] / jnp.where(lane < HEAD_DIM, la, lb))
    o_ref[0] = jnp.concatenate(outs, axis=1).astype(BF16)


def _attn_kernel(ok_ref, qi_ref, kj_ref, q_ref, k_ref, v_ref, cb_ref, o_ref, m_sc, l_sc, acc_sc):
    i = qi_ref[pl.program_id(1)]
    j = kj_ref[pl.program_id(1)]
    bounded = ok_ref[pl.program_id(0)] == 1
    exact = jnp.logical_not(bounded)

    @pl.when(j == 0)
    def _():
        m_sc[...] = jnp.full(m_sc.shape, NEG_INF, F32)
        l_sc[...] = jnp.zeros(l_sc.shape, F32)
        acc_sc[...] = jnp.zeros(acc_sc.shape, F32)

    @pl.when(bounded & (j < i))
    def _():
        _attn_step_bounded(q_ref, k_ref, v_ref, cb_ref, l_sc, acc_sc, masked=False)

    @pl.when(bounded & (j == i))
    def _():
        _attn_step_bounded(q_ref, k_ref, v_ref, cb_ref, l_sc, acc_sc, masked=True)
        _attn_finish(o_ref, l_sc, acc_sc, lane_partial=True)

    @pl.when(exact & (j < i))
    def _():
        _attn_step(q_ref, k_ref, v_ref, m_sc, l_sc, acc_sc, masked=False)

    @pl.when(exact & (j == i))
    def _():
        _attn_step(q_ref, k_ref, v_ref, m_sc, l_sc, acc_sc, masked=True)
        _attn_finish(o_ref, l_sc, acc_sc, lane_partial=False)


def _attention(q, k, v, qk_gain):
    B, T, _ = q.shape
    nq = T // TQ
    gmax = jnp.max(jnp.abs(qk_gain), axis=1)
    c = gmax[0] * gmax[1] * (MLA_QK_DIM ** 0.5) * ATTN_BOUND_SLACK
    ok = jnp.broadcast_to((c <= ATTN_BOUND_MAX).astype(jnp.int32), (B,))
    cb = jnp.broadcast_to(c, (B, MLA_HEADS, LANES))
    pairs = [(i, j) for i in range(nq) for j in range(i + 1)]
    qi = jnp.asarray([p[0] for p in pairs], jnp.int32)
    kj = jnp.asarray([p[1] for p in pairs], jnp.int32)
    grid_spec = pltpu.PrefetchScalarGridSpec(
        num_scalar_prefetch=3,
        grid=(B, len(pairs)),
        in_specs=[pl.BlockSpec((1, TQ, MLA_HEADS * LANES), lambda b, t, ok, qi, kj: (b, qi[t], 0)),
                  pl.BlockSpec((1, TQ, MLA_HEADS * LANES), lambda b, t, ok, qi, kj: (b, kj[t], 0)),
                  pl.BlockSpec((1, TQ, MLA_HEADS * HEAD_DIM), lambda b, t, ok, qi, kj: (b, kj[t], 0)),
                  pl.BlockSpec((1, MLA_HEADS, LANES), lambda b, t, ok, qi, kj: (b, 0, 0))],
        out_specs=pl.BlockSpec((1, TQ, MLA_HEADS * HEAD_DIM), lambda b, t, ok, qi, kj: (b, qi[t], 0)),
        scratch_shapes=[pltpu.VMEM((MLA_HEADS, TQ, LANES), F32),
                        pltpu.VMEM((MLA_HEADS, TQ, LANES), F32),
                        pltpu.VMEM((MLA_HEADS // 2, TQ, LANES), F32)])
    return pl.pallas_call(
        _attn_kernel,
        grid_spec=grid_spec,
        out_shape=jax.ShapeDtypeStruct((B, T, MLA_HEADS * HEAD_DIM), BF16),
        compiler_params=_cparams(("arbitrary", "arbitrary")),
    )(ok, qi, kj, q, k, v, cb)


def _mixout_kernel(has_router, *refs):
    if has_router:
        (x_ref, yr_ref, yp_ref, ym_ref, wo_ref, mod_ref, gain_ref, rt_ref, tri_ref,
         xo_ref, h_ref, route_ref, cnt_ref, cnt_sc) = refs
    else:
        x_ref, yr_ref, yp_ref, ym_ref, wo_ref, mod_ref, gain_ref, xo_ref, h_ref = refs
    mod = mod_ref[0]
    o1 = RWKV_WIDTH
    o2 = RWKV_WIDTH + POOL_WIDTH
    mix = (_dot(yr_ref[0], wo_ref[0:o1, :]) + _dot(yp_ref[0], wo_ref[o1:o2, :])
           + _dot(ym_ref[0], wo_ref[o2:, :]))
    x = x_ref[0] + mod[2:3] * mix
    xo_ref[0] = x
    h = _rms(x) * gain_ref[...] * (1.0 + mod[4:5]) + mod[3:4]
    if not has_router:
        h_ref[0] = h.astype(BF16)
    else:
        hp = _pack_bf16_pairs(h)
        for ck in range(ROW_CHUNKS):
            h_ref[ck, 0] = hp[:, ck * PACK_CHUNK_W:(ck + 1) * PACK_CHUNK_W]
        logits = _dot(h.astype(BF16), rt_ref[...])
        lane = lax.broadcasted_iota(jnp.int32, logits.shape, 1).astype(F32)
        lg = jnp.where(lane < N_EXPERTS, logits, -jnp.inf)
        m1 = jnp.max(lg, axis=-1, keepdims=True)
        i1 = jnp.min(jnp.where(lg == m1, lane, float(LANES)), axis=-1, keepdims=True)
        lg2 = jnp.where(lane == i1, -jnp.inf, lg)
        m2 = jnp.max(lg2, axis=-1, keepdims=True)
        i2 = jnp.min(jnp.where(lg2 == m2, lane, float(LANES)), axis=-1, keepdims=True)
        e2 = jnp.exp(m2 - m1)
        g1 = 1.0 / (1.0 + e2)
        g2 = e2 / (1.0 + e2)
        first = (pl.program_id(0) == 0) & (pl.program_id(1) == 0)

        @pl.when(first)
        def _():
            cnt_sc[...] = jnp.zeros(cnt_sc.shape, F32)

        hit1 = lane == i1
        hit2 = lane == i2
        onehot = jnp.where(hit1 | hit2, 1.0, 0.0)
        prefix = _dot(tri_ref[...], onehot.astype(BF16)) + cnt_sc[0:1, :]
        r1 = jnp.sum(jnp.where(hit1, prefix, 0.0), axis=-1, keepdims=True)
        r2 = jnp.sum(jnp.where(hit2, prefix, 0.0), axis=-1, keepdims=True)
        cnt_sc[...] = cnt_sc[...] + jnp.sum(onehot, axis=0, keepdims=True)
        cnt_ref[...] = cnt_sc[...]
        vals = (i1, i2, g1, g2, r1, r2)
        route = jnp.zeros(logits.shape, F32)
        for pos, val in enumerate(vals):
            route = jnp.where(lane == pos, val, route)
        route_ref[0] = route


def _mixout(x, yr, yp, ym, wo, mod_l, gain, router_p):
    B, T, _ = x.shape
    tm = TM_MIX
    has_router = router_p is not None
    const = lambda shape: pl.BlockSpec(shape, lambda b, i: tuple(0 for _ in shape))
    tok = lambda w: pl.BlockSpec((1, tm, w), lambda b, i: (b, i, 0))
    in_specs = [tok(D_MODEL), tok(RWKV_WIDTH), tok(POOL_WIDTH), tok(MLA_HEADS * HEAD_DIM),
                const((D_MODEL, D_MODEL)), pl.BlockSpec((1, 6, D_MODEL), lambda b, i: (b, 0, 0)),
                const((1, D_MODEL))]
    args = [x, yr, yp, ym, wo, mod_l, gain]
    out_specs = [tok(D_MODEL), tok(D_MODEL)]
    out_shape = [jax.ShapeDtypeStruct((B, T, D_MODEL), F32), jax.ShapeDtypeStruct((B, T, D_MODEL), BF16)]
    scratch = []
    if has_router:
        out_specs[1] = pl.BlockSpec((ROW_CHUNKS, 1, tm, PACK_CHUNK_W), lambda b, i: (0, b, i, 0))
        out_shape[1] = jax.ShapeDtypeStruct((ROW_CHUNKS, B, T, PACK_CHUNK_W), jnp.uint32)
        ids = np.arange(tm)
        tri = jnp.asarray(ids[:, None] > ids[None, :], BF16)
        in_specs += [const((D_MODEL, LANES)), const((tm, tm))]
        args += [router_p, tri]
        out_specs += [tok(LANES), const((SUBLANES, LANES))]
        out_shape += [jax.ShapeDtypeStruct((B, T, LANES), F32),
                      jax.ShapeDtypeStruct((SUBLANES, LANES), F32)]
        scratch = [pltpu.VMEM((SUBLANES, LANES), F32)]
    return pl.pallas_call(
        functools.partial(_mixout_kernel, has_router),
        grid=(B, T // tm),
        in_specs=in_specs, out_specs=out_specs, out_shape=out_shape, scratch_shapes=scratch,
        compiler_params=_cparams(("arbitrary", "arbitrary")),
    )(*args)


def _ffn_kernel(h_ref, wg_ref, wu_ref, wo_ref, x_ref, mod_ref, o_ref, acc):
    j = pl.program_id(1)

    @pl.when(j == 0)
    def _():
        acc[...] = jnp.zeros(acc.shape, F32)

    h = h_ref[...]
    gg = _dot(h, wg_ref[...])
    uu = _dot(h, wu_ref[...])
    acc[...] += _dot((_silu(gg) * uu).astype(BF16), wo_ref[...])

    @pl.when(j == pl.num_programs(1) - 1)
    def _():
        o_ref[...] = x_ref[...] + mod_ref[0][5:6] * acc[...]


def _ffn(h2, w_in, w_out, x, mod_l):
    N = h2.shape[0]
    T = N // mod_l.shape[0]
    tm, tf = TM_FFN, TF_FFN
    nf = D_FF // tf
    per_b = T // tm
    return pl.pallas_call(
        _ffn_kernel,
        grid=(N // tm, nf),
        in_specs=[pl.BlockSpec((tm, D_MODEL), lambda i, j: (i, 0)),
                  pl.BlockSpec((D_MODEL, tf), lambda i, j: (0, j)),
                  pl.BlockSpec((D_MODEL, tf), lambda i, j: (0, j + nf)),
                  pl.BlockSpec((tf, D_MODEL), lambda i, j: (j, 0)),
                  pl.BlockSpec((tm, D_MODEL), lambda i, j: (i, 0)),
                  pl.BlockSpec((1, 6, D_MODEL), lambda i, j: (i // per_b, 0, 0))],
        out_specs=pl.BlockSpec((tm, D_MODEL), lambda i, j: (i, 0)),
        out_shape=jax.ShapeDtypeStruct((N, D_MODEL), F32),
        scratch_shapes=[pltpu.VMEM((tm, D_MODEL), F32)],
        compiler_params=_cparams(("arbitrary", "arbitrary")),
    )(h2, w_in, w_in, w_out, x, mod_l)


def _moe_kernel(be_ref, nv_ref, last_ref, x_ref, wg_ref, wu_ref, wo_ref, o_ref, acc, xm):
    i = pl.program_id(0)
    j = pl.program_id(1)

    @pl.when(i <= last_ref[0])
    def _():
        @pl.when(j == 0)
        def _():
            acc[...] = jnp.zeros(acc.shape, F32)
            row = lax.broadcasted_iota(jnp.int32, (xm.shape[0], 1), 0)
            xp = jnp.concatenate([x_ref[ck] for ck in range(ROW_CHUNKS)], axis=1)
            xp = jnp.where(row < nv_ref[i], xp, jnp.uint32(0))
            xm[...] = _unpack_bf16_pairs(xp)

        def swiglu_rows(n_rows):
            x = xm[0:n_rows, :]
            gg = _dot(x, wg_ref[0].astype(BF16))
            uu = _dot(x, wu_ref[0].astype(BF16))
            acc[0:n_rows, :] += _dot((_silu(gg) * uu).astype(BF16), wo_ref[0].astype(BF16))

        half = xm.shape[0] // 2

        @pl.when(nv_ref[i] > half)
        def _():
            swiglu_rows(xm.shape[0])

        @pl.when(nv_ref[i] <= half)
        def _():
            swiglu_rows(half)

        @pl.when(j == pl.num_programs(1) - 1)
        def _():
            yp = _pack_bf16_pairs(acc[...])
            for ck in range(ROW_CHUNKS):
                o_ref[ck] = yp[:, ck * PACK_CHUNK_W:(ck + 1) * PACK_CHUNK_W]


def _moe_experts(xs, w_in, w_out, block_exp, n_valid, last_blk):
    n_rows = xs.shape[1]
    tm, tf = MOE_BLOCK, TF_MOE
    nf = D_FF_EXPERT // tf
    blk = lambda i, last: jnp.minimum(i, last[0])
    chunk = lambda i, j, last: jnp.where(i <= last[0], j, nf - 1)
    grid_spec = pltpu.PrefetchScalarGridSpec(
        num_scalar_prefetch=3,
        grid=(n_rows // tm, nf),
        in_specs=[pl.BlockSpec((ROW_CHUNKS, tm, PACK_CHUNK_W),
                               lambda i, j, be, nv, last: (0, blk(i, last), 0)),
                  pl.BlockSpec((1, D_MODEL, tf),
                               lambda i, j, be, nv, last: (be[blk(i, last)], 0, chunk(i, j, last))),
                  pl.BlockSpec((1, D_MODEL, tf),
                               lambda i, j, be, nv, last: (be[blk(i, last)], 0, chunk(i, j, last) + nf)),
                  pl.BlockSpec((1, tf, D_MODEL),
                               lambda i, j, be, nv, last: (be[blk(i, last)], chunk(i, j, last), 0))],
        out_specs=pl.BlockSpec((ROW_CHUNKS, tm, PACK_CHUNK_W), lambda i, j, be, nv, last: (0, blk(i, last), 0)),
        scratch_shapes=[pltpu.VMEM((tm, D_MODEL), F32), pltpu.VMEM((tm, D_MODEL), BF16)])
    return pl.pallas_call(
        _moe_kernel,
        grid_spec=grid_spec,
        out_shape=jax.ShapeDtypeStruct((ROW_CHUNKS, n_rows, PACK_CHUNK_W), jnp.uint32),
        compiler_params=_cparams(("arbitrary", "arbitrary")),
    )(block_exp, n_valid, last_blk, xs, w_in, w_in, w_out)


def _sc_mesh():
    return plsc.VectorSubcoreMesh(core_axis_name="c", subcore_axis_name="s")


def _sc_scatter_rows(x, dest, n_rows):
    N, D = x.shape
    K = dest.shape[0]
    win = SC_WINDOW

    @pl.kernel(out_type=jax.ShapeDtypeStruct((n_rows, D), x.dtype), mesh=_sc_mesh(), scratch_types=[])
    def scatter(x_hbm, d_hbm, o_hbm):
        def body(x_vmem, *idx_vmem):
            for iv in idx_vmem:
                pltpu.sync_copy(x_vmem, o_hbm.at[iv.at[0]])

        pltpu.emit_pipeline(
            body,
            grid=(N // win,),
            in_specs=[pl.BlockSpec((win, D), lambda i: (i, 0))]
            + [pl.BlockSpec((1, win), functools.partial(lambda k, i: (k, i), k)) for k in range(K)],
            out_specs=[],
            core_axis_name=("c", "s"),
            dimension_semantics=(pltpu.PARALLEL,),
        )(x_hbm, *([d_hbm] * K))

    return scatter(x, dest)


def _sc_gather_rows(x, idx):
    n = idx.shape[0]
    D = x.shape[1]
    win = SC_WINDOW

    @pl.kernel(out_type=jax.ShapeDtypeStruct((n, D), x.dtype), mesh=_sc_mesh(), scratch_types=[])
    def gather(x_hbm, i_hbm, o_hbm):
        def body(i_vmem, o_vmem):
            pltpu.sync_copy(x_hbm.at[i_vmem.at[0]], o_vmem)

        pltpu.emit_pipeline(
            body,
            grid=(n // win,),
            in_specs=[pl.BlockSpec((1, win), lambda i: (0, i))],
            out_specs=[pl.BlockSpec((win, D), lambda i: (i, 0))],
            core_axis_name=("c", "s"),
            dimension_semantics=(pltpu.PARALLEL,),
        )(i_hbm, o_hbm)

    return gather(x, idx.reshape(1, n))


def _combine_kernel(x_ref, ya_ref, yb_ref, route_ref, mod_ref, o_ref):
    rt = route_ref[...]
    ya = _unpack_bf16_pairs(jnp.concatenate([ya_ref[0, ck] for ck in range(ROW_CHUNKS)], axis=1))
    yb = _unpack_bf16_pairs(jnp.concatenate([yb_ref[0, ck] for ck in range(ROW_CHUNKS)], axis=1))
    f = rt[:, 2:3] * ya.astype(F32) + rt[:, 3:4] * yb.astype(F32)
    o_ref[...] = x_ref[...] + mod_ref[0][5:6] * f


def _combine(x, y2, route, mod_l):
    N = x.shape[0]
    T = N // mod_l.shape[0]
    tm = 1024
    per_b = T // tm
    tok = pl.BlockSpec((tm, D_MODEL), lambda i: (i, 0))
    slot = lambda k: pl.BlockSpec((1, ROW_CHUNKS, tm, PACK_CHUNK_W), lambda i: (k, 0, i, 0))
    return pl.pallas_call(
        _combine_kernel,
        grid=(N // tm,),
        in_specs=[tok, slot(0), slot(1),
                  pl.BlockSpec((tm, LANES), lambda i: (i, 0)),
                  pl.BlockSpec((1, 6, D_MODEL), lambda i: (i // per_b, 0, 0))],
        out_specs=tok,
        out_shape=jax.ShapeDtypeStruct((N, D_MODEL), F32),
        compiler_params=_cparams(("arbitrary",)),
    )(x, y2, y2, route, mod_l)


def _moe(h2, route, counts, w_in, w_out, x, mod_l):
    N = x.shape[0]
    blk = MOE_BLOCK
    cnt = counts[0, :N_EXPERTS].astype(jnp.int32)
    padded = (cnt + blk - 1) // blk * blk
    pend = jnp.cumsum(padded)
    pstart = pend - padded
    e = route[:, 0:TOP_K].astype(jnp.int32)
    rank = route[:, 2 * TOP_K:3 * TOP_K].astype(jnp.int32)
    dest = (jnp.take(pstart, e) + rank).T
    n_blocks = N * TOP_K // blk + N_EXPERTS
    bstart = jnp.arange(n_blocks, dtype=jnp.int32) * blk
    block_exp = jnp.minimum(jnp.sum((bstart[:, None] >= pend[None, :]).astype(jnp.int32), axis=1),
                            N_EXPERTS - 1)
    n_valid = jnp.clip(cnt[block_exp] - (bstart - pstart[block_exp]), 0, blk).astype(jnp.int32)
    n_rows = n_blocks * blk
    dest_ck = dest[:, None, :] + (jnp.arange(ROW_CHUNKS, dtype=jnp.int32) * n_rows)[None, :, None]
    xs = _sc_scatter_rows(h2, dest_ck.reshape(TOP_K, ROW_CHUNKS * N), ROW_CHUNKS * n_rows)
    last_blk = (pend[-1:] // blk - 1).astype(jnp.int32)
    yb = _moe_experts(xs.reshape(ROW_CHUNKS, n_rows, PACK_CHUNK_W), w_in, w_out, block_exp, n_valid,
                      last_blk)
    y2 = _sc_gather_rows(yb.reshape(ROW_CHUNKS * n_rows, PACK_CHUNK_W), dest_ck.reshape(-1))
    return _combine(x, y2.reshape(TOP_K, ROW_CHUNKS, N, PACK_CHUNK_W), route, mod_l)


def _layout_w_in(w, has_vres):
    W = RWKV_WIDTH
    off_gd = 3 * W + RWKV_DECAY_LORA + RWKV_ICLR_LORA
    off_pool = off_gd + RWKV_GATE_LORA
    off_q = off_pool + POOL_WIDTH
    off_kv = off_q + MLA_Q_LORA
    off_kr = off_kv + MLA_KV_LORA
    n_base = off_kr + MLA_QK_ROPE
    d = w.shape[0]
    zeros = lambda n: jnp.zeros((d, n), w.dtype)
    vd = w[:, n_base:n_base + RWKV_VRES_LORA] if has_vres else zeros(RWKV_VRES_LORA)
    cols = [w[:, :off_gd], w[:, off_gd:off_pool], vd, zeros(ZR_COLS - off_pool - RWKV_VRES_LORA),
            w[:, off_pool:off_q], w[:, off_q:off_kv], w[:, off_kv:off_kr],
            zeros(MLA_QK_NOPE), w[:, off_kr:n_base], zeros(LANES - MLA_QK_DIM)]
    return jnp.concatenate(cols, axis=1).astype(BF16)


def _pad_heads(w, per_head, keep_from, keep_n):
    K = w.shape[0]
    wh = w.reshape(K, MLA_HEADS, per_head)[:, :, keep_from:keep_from + keep_n]
    wh = jnp.pad(wh, ((0, 0), (0, 0), (0, LANES - keep_n)))
    return wh.reshape(K, MLA_HEADS * LANES)


def kernel(x, c, positions, w_ada, b_ada, norm_gain, w_in_first, w_in_rest, mu_shift, mu_shift_v,
           rwkv_vec, rwkv_v0, rwkv_w2, rwkv_a2, rwkv_g2, rwkv_v2, pool_w, pool_scale,
           mla_q_lat_gain, mla_kv_lat_gain, mla_wq_up, mla_wkv_up, mla_qk_gain, w_out, ffn_w_in,
           ffn_w_out, moe_router, moe_w_in, moe_w_out):
    B, T, D = x.shape
    depth = w_ada.shape[0]
    W = RWKV_WIDTH
    mod = _adaln(c, w_ada, b_ada).reshape(depth, B, 6, D)
    pos3 = positions.reshape(B, T, 1)
    inv_freq = ROPE_BASE ** (-jnp.arange(0, MLA_QK_ROPE, 2, dtype=F32) / MLA_QK_ROPE)
    freq = jnp.concatenate([jnp.zeros((MLA_QK_NOPE,), F32), inv_freq, inv_freq,
                            jnp.zeros((LANES - MLA_QK_DIM,), F32)]).reshape(1, LANES)
    cosf, sinf = _rope_tables(pos3, freq)
    hid = np.arange(LANES) // HEAD_DIM
    bd64 = jnp.asarray(hid[:, None] == hid[None, :], BF16)
    bd128 = jnp.ones((LANES, LANES), BF16)

    v_first = None
    for l in range(depth):
        has_vres = l > 0
        mod_l = mod[l]
        win = _layout_w_in(w_in_first if l == 0 else w_in_rest[l - 1], has_vres)
        poolw = jax.scipy.linalg.block_diag(*[pool_w[l, g] for g in range(len(POOL_WINDOWS))]).astype(BF16)
        wq = _pad_heads(mla_wq_up[l], MLA_QK_DIM, 0, MLA_QK_DIM).astype(BF16)
        wk = _pad_heads(mla_wkv_up[l], MLA_QK_NOPE + HEAD_DIM, 0, MLA_QK_NOPE).astype(BF16)
        wv = mla_wkv_up[l].reshape(MLA_KV_LORA, MLA_HEADS, MLA_QK_NOPE + HEAD_DIM)[:, :, MLA_QK_NOPE:]
        wv = wv.reshape(MLA_KV_LORA, MLA_HEADS * HEAD_DIM).astype(BF16)
        qkg = jnp.tile(jnp.pad(mla_qk_gain[l], ((0, 0), (0, LANES - MLA_QK_DIM))), (1, MLA_HEADS))
        zr, y_pool, q, k, v = _mixin(
            x, mod_l, norm_gain[l, 0].reshape(1, D), win, cosf, sinf, poolw,
            pool_scale[l].reshape(1, -1), mla_q_lat_gain[l].reshape(1, -1),
            mla_kv_lat_gain[l].reshape(1, -1), wq, wk, wv, qkg, bd128)

        pad_mu = ZR_COLS - mu_shift.shape[1] - RWKV_VRES_LORA
        mu_v = mu_shift_v[l - 1] if has_vres else jnp.zeros((RWKV_VRES_LORA,), F32)
        mu = jnp.concatenate([mu_shift[l], mu_v, jnp.zeros((pad_mu,), F32)]).reshape(1, ZR_COLS)
        v0 = rwkv_v0[l - 1] if has_vres else jnp.zeros((W,), F32)
        vec8 = jnp.concatenate([rwkv_vec[l], v0[None]], axis=0)
        w2a2 = jax.scipy.linalg.block_diag(rwkv_w2[l], rwkv_a2[l]).astype(BF16)
        g2 = jnp.pad(rwkv_g2[l], ((0, 2 * LANES - RWKV_GATE_LORA), (0, 0)))
        if has_vres:
            v2 = jnp.pad(rwkv_v2[l - 1], ((RWKV_GATE_LORA, 2 * LANES - RWKV_GATE_LORA - RWKV_VRES_LORA), (0, 0)))
        else:
            v2 = jnp.zeros((2 * LANES, W), F32)
        g2v2 = jnp.concatenate([g2, v2], axis=1).astype(BF16)
        y_rwkv, v_first = _rwkv(zr, v_first, mu, vec8, w2a2, g2v2, bd64)

        y_mla = _attention(q, k, v, mla_qk_gain[l])

        is_moe = (l % 2 == 1)
        router_p = None
        if is_moe:
            router_p = jnp.pad(moe_router[l // 2], ((0, 0), (0, LANES - N_EXPERTS))).astype(BF16)
        outs = _mixout(x, y_rwkv, y_pool, y_mla, w_out[l].astype(BF16), mod_l,
                       norm_gain[l, 1].reshape(1, D), router_p)
        x_mid, h2 = outs[0], outs[1]
        xf = x_mid.reshape(B * T, D)
        if is_moe:
            xo = _moe(h2.reshape(ROW_CHUNKS * B * T, PACK_CHUNK_W), outs[2].reshape(B * T, LANES), outs[3],
                      moe_w_in[l // 2], moe_w_out[l // 2], xf, mod_l)
        else:
            xo = _ffn(h2.reshape(B * T, D), ffn_w_in[l // 2].astype(BF16), ffn_w_out[l // 2].astype(BF16),
                      xf, mod_l)
        x = xo.reshape(B, T, D)
    return x
```

```python
import functools

import numpy as np
import jax
import jax.numpy as jnp
from jax import lax
from jax.experimental import pallas as pl
from jax.experimental.pallas import tpu as pltpu
from jax.experimental.pallas import tpu_sc as plsc

F32 = jnp.float32
BF16 = jnp.bfloat16

D_MODEL = 1024
HEAD_DIM = 64
RWKV_WIDTH = 512
RWKV_HEADS = 8
POOL_WIDTH = 256
POOL_WINDOWS = (2, 4, 8, 16)
POOL_HALO = 16
MLA_HEADS = 4
MLA_QK_NOPE = 64
MLA_QK_ROPE = 32
MLA_QK_DIM = 96
MLA_Q_LORA = 256
MLA_KV_LORA = 128
ROPE_BASE = 10000.0
RWKV_DECAY_LORA = 64
RWKV_ICLR_LORA = 64
RWKV_VRES_LORA = 32
RWKV_GATE_LORA = 160
RWKV_LNX_EPS = 64e-5
D_FF = 2816
N_EXPERTS = 8
TOP_K = 2
D_FF_EXPERT = 3584
NORM_EPS = 1e-6
NEG_INF = -1e30

LANES = 128
SUBLANES = 8
VMEM_LIMIT = 56 * 1024 * 1024

ZR_COLS = 1920
Z_POOL_OFF = ZR_COLS
Z_QLAT_OFF = Z_POOL_OFF + POOL_WIDTH
Z_KVLAT_OFF = Z_QLAT_OFF + MLA_Q_LORA
Z_KROPE_OFF = Z_KVLAT_OFF + MLA_KV_LORA
Z_COLS = Z_KROPE_OFF + LANES

MIXIN_PIECE_W = 512
TM_MIX = 512
TM_OUT = 1024
WKV_CHUNK = 64
TQ = 512
ATTN_BOUND_SLACK = 1.02
ATTN_BOUND_MAX = 40.0
TM_FFN = 1024
TF_FFN = 1408
MOE_BLOCK = 1024
TF_MOE = 512
SC_WINDOW = 128
ROW_CHUNKS = 4
ROW_CHUNK_W = D_MODEL // ROW_CHUNKS
PACK_CHUNK_W = ROW_CHUNK_W // 2

SEGSUM_SPLITS = 1

NN = (((1,), (0,)), ((), ()))
NT = (((1,), (1,)), ((), ()))


def _dot(a, b, dims=NN):
    return lax.dot_general(a, b, dims, preferred_element_type=F32)


def _split2(a):
    hi = a.astype(BF16)
    lo = (a - hi.astype(F32)).astype(BF16)
    return hi, lo


def _mm(a, b, dims=NN, passes=3):
    if passes == 1:
        return _dot(a.astype(BF16), b.astype(BF16), dims)
    ah, al = _split2(a)
    bh, bl = _split2(b)
    return _dot(ah, bh, dims) + (_dot(ah, bl, dims) + _dot(al, bh, dims))


def _mm_exact_rhs(a, b_bf16, splits=SEGSUM_SPLITS):
    m, w = a.shape
    nb = w // LANES
    stacked = jnp.concatenate([a[:, i * LANES:(i + 1) * LANES] for i in range(nb)], axis=0)
    out = None
    rem = stacked
    for s in range(splits):
        part = rem.astype(BF16)
        term = _dot(part, b_bf16)
        out = term if out is None else out + term
        if s + 1 < splits:
            rem = rem - part.astype(F32)
    return jnp.concatenate([out[i * m:(i + 1) * m] for i in range(nb)], axis=1)


def _pack_bf16_pairs(h):
    w = h.shape[1] // 2
    lo = lax.bitcast_convert_type(h[:, :w].astype(BF16).astype(F32), jnp.uint32)
    hi = lax.bitcast_convert_type(h[:, w:].astype(BF16).astype(F32), jnp.uint32)
    return (lo >> 16) | (hi & jnp.uint32(0xFFFF0000))


def _unpack_bf16_pairs(p):
    lo = lax.bitcast_convert_type(p << 16, F32)
    hi = lax.bitcast_convert_type(p & jnp.uint32(0xFFFF0000), F32)
    return jnp.concatenate([lo, hi], axis=1).astype(BF16)


def _sigmoid(x):
    return 1.0 / (1.0 + jnp.exp(-x))


def _silu(x):
    return x * _sigmoid(x)


def _rms(x, eps=NORM_EPS):
    return x * lax.rsqrt(jnp.mean(x * x, axis=-1, keepdims=True) + eps)


def _cparams(sem):
    return pltpu.CompilerParams(dimension_semantics=sem, vmem_limit_bytes=VMEM_LIMIT)


def _adaln_kernel(c_ref, w_ref, b_ref, o_ref):
    ca = _silu(c_ref[...])
    o_ref[0] = _mm(ca, w_ref[0]) + b_ref[0]


def _adaln(c, w_ada, b_ada):
    L = w_ada.shape[0]
    B = c.shape[0]
    n = w_ada.shape[2] // D_MODEL
    return pl.pallas_call(
        _adaln_kernel,
        grid=(L, n),
        in_specs=[pl.BlockSpec((B, D_MODEL), lambda l, j: (0, 0)),
                  pl.BlockSpec((1, D_MODEL, D_MODEL), lambda l, j: (l, 0, j)),
                  pl.BlockSpec((1, 1, D_MODEL), lambda l, j: (l, 0, j))],
        out_specs=pl.BlockSpec((1, B, D_MODEL), lambda l, j: (l, 0, j)),
        out_shape=jax.ShapeDtypeStruct((L, B, n * D_MODEL), F32),
        compiler_params=_cparams(("arbitrary", "arbitrary")),
    )(c, w_ada, b_ada.reshape(L, 1, -1))


def _rope(x, cosf, sinf, lane):
    up = pltpu.roll(x, LANES - MLA_QK_ROPE // 2, axis=1)
    dn = pltpu.roll(x, MLA_QK_ROPE // 2, axis=1)
    rot = jnp.where(lane < MLA_QK_NOPE + MLA_QK_ROPE // 2, -up, dn)
    return x * cosf + rot * sinf


def _rope_kernel(pos_ref, freq_ref, cos_ref, sin_ref):
    ang = pos_ref[0].astype(F32) * freq_ref[...]
    cos_ref[0] = jnp.cos(ang)
    sin_ref[0] = jnp.sin(ang)


def _rope_tables(positions, inv_freq):
    B, T = positions.shape
    nf = inv_freq.shape[0]
    per_row = LANES // nf
    pos_p = jnp.repeat(positions, nf, axis=1).reshape(B, T // per_row, LANES)
    freq_p = jnp.tile(inv_freq, per_row).reshape(1, LANES)
    blk = pl.BlockSpec((1, T // per_row, LANES), lambda b: (b, 0, 0))
    cos_p, sin_p = pl.pallas_call(
        _rope_kernel,
        grid=(B,),
        in_specs=[blk, pl.BlockSpec((1, LANES), lambda b: (0, 0))],
        out_specs=[blk, blk],
        out_shape=[jax.ShapeDtypeStruct((B, T // per_row, LANES), F32)] * 2,
        compiler_params=_cparams(("arbitrary",)),
    )(pos_p, freq_p)

    def spread(tab, fill):
        t16 = tab.reshape(B, T, nf)
        return jnp.concatenate([jnp.full((B, T, MLA_QK_NOPE), fill, F32), t16, t16,
                                jnp.full((B, T, LANES - MLA_QK_DIM), fill, F32)], axis=-1)

    return spread(cos_p, 1.0), spread(sin_p, 0.0)


def _mixin_kernel(x_ref, mod_ref, gain_ref, win_ref, cos_ref, sin_ref, poolw_ref, pools_ref,
                  qg_ref, kvg_ref, wq_ref, wk_ref, wv_ref, qkg_ref, bd_ref,
                  zr_ref, yp_ref, q_ref, k_ref, v_ref, ubuf):
    i = pl.program_id(1)
    tm = x_ref.shape[1]
    x = x_ref[0]
    mod = mod_ref[0]
    @pl.when(i == 0)
    def _():
        ubuf[0:POOL_HALO, :] = jnp.zeros((POOL_HALO, POOL_WIDTH), F32)

    h = (_rms(x) * gain_ref[...] * (1.0 + mod[1:2]) + mod[0:1]).astype(BF16)
    zb = _dot(h, win_ref[:, ZR_COLS:])
    zcol = lambda off, w: zb[:, off - ZR_COLS:off - ZR_COLS + w]
    def project_piece(n):
        cols = slice(n * MIXIN_PIECE_W, min((n + 1) * MIXIN_PIECE_W, ZR_COLS))
        zr_ref[0, :, cols] = _dot(h, win_ref[:, cols])

    project_piece(0)
    u = zcol(Z_POOL_OFF, POOL_WIDTH)
    ubuf[POOL_HALO:, :] = u
    ue = ubuf[...]
    s2 = ue + pltpu.roll(ue, 1, axis=0)
    s4 = s2 + pltpu.roll(s2, 2, axis=0)
    s8 = s4 + pltpu.roll(s4, 4, axis=0)
    s16 = s8 + pltpu.roll(s8, 8, axis=0)
    ubuf[0:POOL_HALO, :] = u[tm - POOL_HALO:, :]
    lane_p = lax.broadcasted_iota(jnp.int32, (tm, POOL_WIDTH), 1)
    grp = lane_p // (POOL_WIDTH // len(POOL_WINDOWS))
    win_sum = jnp.where(grp == 0, s2[POOL_HALO:], jnp.where(grp == 1, s4[POOL_HALO:],
                        jnp.where(grp == 2, s8[POOL_HALO:], s16[POOL_HALO:])))
    win = jnp.where(grp == 0, 2, jnp.where(grp == 1, 4, jnp.where(grp == 2, 8, 16)))
    t_abs = i * tm + lax.broadcasted_iota(jnp.int32, (tm, POOL_WIDTH), 0)
    cnt = jnp.minimum(t_abs + 1, win).astype(F32)
    p = win_sum / cnt - u
    yp = _dot(p.astype(BF16), poolw_ref[...]) * pools_ref[...]
    yp_ref[0] = yp.astype(BF16)
    project_piece(1)

    lane = lax.broadcasted_iota(jnp.int32, (tm, LANES), 1)
    cosf = cos_ref[0]
    sinf = sin_ref[0]

    q_lat = zcol(Z_QLAT_OFF, MLA_Q_LORA)
    kv_lat = zcol(Z_KVLAT_OFF, MLA_KV_LORA)
    k_rope = zcol(Z_KROPE_OFF, LANES)
    qn = (_rms(q_lat) * qg_ref[...]).astype(BF16)
    kvn = (_rms(kv_lat) * kvg_ref[...]).astype(BF16)
    q = _dot(qn, wq_ref[...])
    kx = _dot(kvn, wk_ref[...])
    v = _dot(kvn, wv_ref[...])
    v_ref[0] = v.astype(BF16)
    project_piece(2)
    k_pe = _rope(k_rope, cosf, sinf, lane)
    qs, ks = [], []
    for hd in range(MLA_HEADS):
        sl = slice(hd * LANES, (hd + 1) * LANES)
        qs.append(_rope(q[:, sl], cosf, sinf, lane))
        ks.append(kx[:, sl] + k_pe)
    q = jnp.concatenate(qs, axis=1)
    k = jnp.concatenate(ks, axis=1)
    project_piece(3)
    qss = _mm_exact_rhs(q * q, bd_ref[...]) * (1.0 / MLA_QK_DIM)
    kss = _mm_exact_rhs(k * k, bd_ref[...]) * (1.0 / MLA_QK_DIM)
    qkg = qkg_ref[...]
    q = q * lax.rsqrt(qss + NORM_EPS) * qkg[0:1] * (MLA_QK_DIM ** -0.5)
    k = k * lax.rsqrt(kss + NORM_EPS) * qkg[1:2]
    q_ref[0] = q.astype(BF16)
    k_ref[0] = k.astype(BF16)


def _mixin(x, mod_l, gain, win, cosf, sinf, poolw, pools, qg, kvg, wq, wk, wv, qkg, bd128):
    B, T, _ = x.shape
    tm = TM_MIX
    const = lambda shape: pl.BlockSpec(shape, lambda b, i: tuple(0 for _ in shape))
    tok = lambda w: pl.BlockSpec((1, tm, w), lambda b, i: (b, i, 0))
    return pl.pallas_call(
        _mixin_kernel,
        grid=(B, T // tm),
        in_specs=[tok(D_MODEL),
                  pl.BlockSpec((1, 6, D_MODEL), lambda b, i: (b, 0, 0)),
                  const((1, D_MODEL)), const((D_MODEL, Z_COLS)),
                  tok(LANES), tok(LANES),
                  const((POOL_WIDTH, POOL_WIDTH)), const((1, POOL_WIDTH)),
                  const((1, MLA_Q_LORA)), const((1, MLA_KV_LORA)),
                  const((MLA_Q_LORA, MLA_HEADS * LANES)), const((MLA_KV_LORA, MLA_HEADS * LANES)),
                  const((MLA_KV_LORA, MLA_HEADS * HEAD_DIM)), const((2, MLA_HEADS * LANES)),
                  const((LANES, LANES))],
        out_specs=[tok(ZR_COLS), tok(POOL_WIDTH), tok(MLA_HEADS * LANES), tok(MLA_HEADS * LANES),
                   tok(MLA_HEADS * HEAD_DIM)],
        out_shape=[jax.ShapeDtypeStruct((B, T, ZR_COLS), F32),
                   jax.ShapeDtypeStruct((B, T, POOL_WIDTH), BF16),
                   jax.ShapeDtypeStruct((B, T, MLA_HEADS * LANES), BF16),
                   jax.ShapeDtypeStruct((B, T, MLA_HEADS * LANES), BF16),
                   jax.ShapeDtypeStruct((B, T, MLA_HEADS * HEAD_DIM), BF16)],
        scratch_shapes=[pltpu.VMEM((POOL_HALO + tm, POOL_WIDTH), F32)],
        compiler_params=_cparams(("arbitrary", "arbitrary")),
    )(x, mod_l, gain, win, cosf, sinf, poolw, pools, qg, kvg, wq, wk, wv, qkg, bd128)


WKV_PASSES_SCORE = 1
WKV_PASSES_INV = 1
WKV_PASSES_APPLY = 1
WKV_PASSES_STATE = 1
WKV_STEP_CHUNKS = 1
WKV_STEP_SEQS = 8
WKV_GROUPS = 2
WKV_GROUP_LEAD = 4


def _stack_heads(xp, lane):
    return jnp.concatenate([jnp.where(lane < HEAD_DIM, xp, 0.0),
                            jnp.where(lane >= HEAD_DIM, xp, 0.0)], axis=0)


def _wkv_prep(r, lw, k, v, kk, a, tri, masks):
    L = r[0].shape[0]
    nc = len(r)
    each = lambda f, *ls: [f(*xs) for xs in zip(*ls)]
    lane = lax.broadcasted_iota(jnp.int32, (L, LANES), 1)
    stack = lambda x: _stack_heads(x, lane)
    cum = each(lambda x: _mm_exact_rhs_left(tri, x), lw)
    cum_last = each(lambda c: c[L - 1:L, :], cum)
    e_w = each(jnp.exp, cum)
    e_wm = each(lambda c, x: jnp.exp(c - x), cum, lw)
    e_iw = each(lambda c: jnp.exp(-c), cum)
    e_d = each(lambda cl, c: jnp.exp(cl - c), cum_last, cum)
    beta = each(lambda x, y: x * y, kk, a)
    r_f = each(lambda x, e: x * e, r, e_w)
    a_f = each(lambda x, e: -x * e, kk, e_wm)
    a_s = each(stack, a_f)
    b_s = each(lambda x, e: stack(x * e), beta, e_iw)
    k_s = each(lambda x, e: stack(x * e), k, e_iw)
    b_d = each(lambda x, e: stack(x * e), beta, e_d)
    k_d = each(lambda x, e: stack(x * e), k, e_d)
    v_s = each(stack, v)
    yield
    g = each(lambda af, rf, bs, ks: _mm(jnp.concatenate([af, rf], axis=0),
                                        jnp.concatenate([bs, ks], axis=0), NT, WKV_PASSES_SCORE),
             a_f, r_f, b_s, k_s)
    strict, incl, levels = masks
    a_ab = each(lambda x: jnp.where(strict, x[:L, :LANES], 0.0), g)
    a_ak = each(lambda x: jnp.where(strict, x[:L, LANES:], 0.0), g)
    s_rb = each(lambda x: jnp.where(incl, x[L:, :LANES], 0.0), g)
    s_rk = each(lambda x: jnp.where(incl, x[L:, LANES:], 0.0), g)
    eye = jnp.where(levels[0][1], 1.0, 0.0)
    tinv = each(lambda x: eye + jnp.where(levels[0][0], x, 0.0), a_ab)
    yield
    for lvl_mask, _ in levels[1:]:
        et = each(lambda x, t: _mm(jnp.where(lvl_mask, x, 0.0), stack(t), NN, WKV_PASSES_INV), a_ab, tinv)
        tinv = each(lambda t, x: t + _mm(t, stack(x), NN, WKV_PASSES_INV), tinv, et)
        yield
    av = each(lambda x, y: _mm(x, y, NN, WKV_PASSES_APPLY), a_ak, v_s)
    tx = each(lambda t, x, y: _mm(t, jnp.concatenate([x, stack(y)], axis=1), NN, WKV_PASSES_APPLY),
              tinv, a_s, av)
    ta_s = each(lambda x: stack(x[:, :LANES]), tx)
    c1_s = each(lambda x: stack(x[:, LANES:]), tx)
    yield
    ra = each(lambda rf, s, x: rf + _mm(s, x, NN, WKV_PASSES_APPLY), r_f, s_rb, ta_s)
    c2 = each(lambda sb, sk, x, vs: _mm(jnp.concatenate([sb, sk], axis=1),
                                        jnp.concatenate([x, vs], axis=0),
                                        NN, WKV_PASSES_APPLY), s_rb, s_rk, c1_s, v_s)
    yield
    tb = each(lambda x, y, bd: _mm(jnp.concatenate([x, y], axis=1).T, bd, NN, WKV_PASSES_APPLY),
              ta_s, c1_s, b_d)
    c3 = each(lambda x, vs, kd: x[LANES:] + _mm(vs.T, kd, NN, WKV_PASSES_APPLY), tb, v_s, k_d)
    return [(ra[i], c2[i], jnp.exp(cum_last[i]), tb[i][:LANES], c3[i]) for i in range(nc)]


def _mm_exact_rhs_left(tri_bf16, x):
    x0 = x.astype(BF16)
    r1 = x - x0.astype(F32)
    x1 = r1.astype(BF16)
    x2 = (r1 - x1.astype(F32)).astype(BF16)
    return _dot(tri_bf16, x0) + (_dot(tri_bf16, x1) + _dot(tri_bf16, x2))


def _wkv_masks(L):
    row = lax.broadcasted_iota(jnp.int32, (L, 2 * L), 0)
    col = lax.broadcasted_iota(jnp.int32, (L, 2 * L), 1) % L
    strict = row > col
    incl = row >= col
    levels = []
    m = 1
    while m < L:
        same = (row // (2 * m)) == (col // (2 * m))
        lvl = same & ((row % (2 * m)) >= m) & ((col % (2 * m)) < m)
        levels.append((lvl, row == col))
        m *= 2
    return strict, incl, levels


def _rwkv_kernel(has_vres, *refs):
    if has_vres:
        (z_ref, vf_ref, mu_ref, vec_ref, w2a2_ref, g2v2_ref, bd_ref,
         y_ref, carry, state) = refs
    else:
        (z_ref, mu_ref, vec_ref, w2a2_ref, g2v2_ref, bd_ref,
         y_ref, vout_ref, carry, state) = refs
    c = pl.program_id(1)
    n_seq, seq_rows = z_ref.shape[0], z_ref.shape[1]
    rows = n_seq * seq_rows
    L = WKV_CHUNK
    W = RWKV_WIDTH

    @pl.when(c == 0)
    def _():
        carry[...] = jnp.zeros(carry.shape, F32)
        state[...] = jnp.zeros(state.shape, F32)

    flat = lambda ref: jnp.concatenate([ref[s] for s in range(n_seq)], axis=0)
    z = flat(z_ref)
    row = lax.broadcasted_iota(jnp.int32, z.shape, 0)
    prev = pltpu.roll(z, 1, axis=0)
    for s in range(n_seq):
        prev = jnp.where(row == s * seq_rows, carry[s, SUBLANES - 1:SUBLANES, :], prev)
        carry[s] = z[(s + 1) * seq_rows - SUBLANES:(s + 1) * seq_rows, :]
    zs_all = z + mu_ref[...] * (prev - z)
    if has_vres:
        vf_all = flat(vf_ref)
    vec = vec_ref[...]
    w0, a0, k_k, k_a, r_k, ln_g, ln_b, v0 = (vec[j:j + 1] for j in range(8))
    bd = bd_ref[...]
    masks = _wkv_masks(L)
    rowt = lax.broadcasted_iota(jnp.int32, (L, L), 0)
    colt = lax.broadcasted_iota(jnp.int32, (L, L), 1)
    tri = jnp.where(rowt >= colt, 1.0, 0.0).astype(BF16)
    n_pairs = RWKV_HEADS // 2
    S_now = [{(s, p): state[s * n_pairs + p] for s in range(n_seq) for p in range(n_pairs)}]

    def group(s0, s1):
        seqs = range(s0, s1)
        r0, r1 = s0 * seq_rows, s1 * seq_rows
        zs = zs_all[r0:r1]
        r = zs[:, 0:W]
        k = zs[:, W:2 * W]
        v = zs[:, 2 * W:3 * W]
        wa = zs[:, 3 * W:3 * W + LANES]
        gb = zs[:, 3 * W + LANES:ZR_COLS]
        lane_a = lax.broadcasted_iota(jnp.int32, wa.shape, 1)
        t1 = _dot(jnp.where(lane_a < RWKV_DECAY_LORA, jnp.tanh(wa), wa).astype(BF16), w2a2_ref[...])
        lane_g = lax.broadcasted_iota(jnp.int32, gb.shape, 1)
        t2 = _dot(jnp.where(lane_g < RWKV_GATE_LORA, _sigmoid(gb), gb).astype(BF16), g2v2_ref[...])
        yield
        xw = w0 + t1[:, :W]
        w_log = -(jnp.maximum(-xw, 0.0) + jnp.log(1.0 + jnp.exp(-jnp.abs(xw)))) - 0.5
        lw = -jnp.exp(w_log)
        a = _sigmoid(a0 + t1[:, W:])
        g = t2[:, :W]
        if has_vres:
            v = v + (vf_all[r0:r1] - v) * _sigmoid(v0 + t2[:, W:])
        else:
            for s in seqs:
                vout_ref[s] = v[(s - s0) * seq_rows:(s - s0 + 1) * seq_rows]
        kk = k * k_k
        kk = kk * jnp.minimum(lax.rsqrt(_mm_exact_rhs(kk * kk, bd)), 1e12)
        k = k * (1.0 + (a - 1.0) * k_a)
        yield
        n_chunks = seq_rows // L
        idx = [(s, ch, p) for s in seqs for ch in range(n_chunks) for p in range(n_pairs)]
        row0 = lambda s, ch: (s - s0) * seq_rows + ch * L
        cut = lambda x: [x[row0(s, ch):row0(s, ch) + L, p * LANES:(p + 1) * LANES] for s, ch, p in idx]
        res = yield from _wkv_prep(cut(r), cut(lw), cut(k), cut(v), cut(kk), cut(a), tri, masks)
        prep = dict(zip(idx, res))
        yield
        sp = [(s, p) for s in seqs for p in range(n_pairs)]
        S = {key: S_now[0][key] for key in sp}
        y_blk = {}
        for ch in range(n_chunks):
            y_s = {(s, p): _mm(prep[s, ch, p][0], S[s, p], NT, WKV_PASSES_STATE) + prep[s, ch, p][1]
                   for s, p in sp}
            for s in seqs:
                y_blk[s, ch] = jnp.concatenate([y_s[s, p] for p in range(n_pairs)], axis=1)
            S = {(s, p): S[s, p] * prep[s, ch, p][2] + _mm(S[s, p], prep[s, ch, p][3], NN, WKV_PASSES_STATE)
                 + prep[s, ch, p][4] for s, p in sp}
        S_now[0].update(S)
        y = jnp.concatenate([y_blk[s, ch] for s in seqs for ch in range(n_chunks)], axis=0)
        yield
        inv = 1.0 / HEAD_DIM
        mean = _mm_exact_rhs(y, bd) * inv
        yc = y - mean
        var = _mm_exact_rhs(yc * yc, bd) * inv
        yn = yc * lax.rsqrt(var + RWKV_LNX_EPS) * ln_g + ln_b
        bonus = _mm_exact_rhs(r * k * r_k, bd) * v
        out = ((yn + bonus) * g).astype(BF16)
        for s in seqs:
            y_ref[s] = out[(s - s0) * seq_rows:(s - s0 + 1) * seq_rows]

    per = n_seq // WKV_GROUPS
    gens = [group(gi * per, (gi + 1) * per) for gi in range(WKV_GROUPS)]
    alive = [True] * WKV_GROUPS
    tick = 0
    while any(alive):
        for gi in range(WKV_GROUPS):
            if alive[gi] and tick >= gi * WKV_GROUP_LEAD:
                try:
                    next(gens[gi])
                except StopIteration:
                    alive[gi] = False
        tick += 1
    for s in range(n_seq):
        for p in range(n_pairs):
            state[s * n_pairs + p] = S_now[0][s, p]


def _rwkv(zr, v_first, mu, vec8, w2a2, g2v2, bd64):
    B, T, _ = zr.shape
    L = WKV_CHUNK * WKV_STEP_CHUNKS
    ns = WKV_STEP_SEQS
    has_vres = v_first is not None
    const = lambda shape: pl.BlockSpec(shape, lambda b, c: tuple(0 for _ in shape))
    tok = lambda w: pl.BlockSpec((ns, L, w), lambda b, c: (b, c, 0))
    in_specs = [tok(ZR_COLS)]
    args = [zr]
    if has_vres:
        in_specs.append(tok(RWKV_WIDTH))
        args.append(v_first)
    in_specs += [const((1, ZR_COLS)), const((8, RWKV_WIDTH)), const((LANES, 2 * RWKV_WIDTH)),
                 const((2 * LANES, 2 * RWKV_WIDTH)), const((LANES, LANES))]
    args += [mu, vec8, w2a2, g2v2, bd64]
    out_specs = [tok(RWKV_WIDTH)]
    out_shape = [jax.ShapeDtypeStruct((B, T, RWKV_WIDTH), BF16)]
    if not has_vres:
        out_specs.append(tok(RWKV_WIDTH))
        out_shape.append(jax.ShapeDtypeStruct((B, T, RWKV_WIDTH), F32))
    outs = pl.pallas_call(
        functools.partial(_rwkv_kernel, has_vres),
        grid=(B // ns, T // L),
        in_specs=in_specs, out_specs=out_specs, out_shape=out_shape,
        scratch_shapes=[pltpu.VMEM((ns, SUBLANES, ZR_COLS), F32),
                        pltpu.VMEM((ns * (RWKV_HEADS // 2), LANES, LANES), F32)],
        compiler_params=_cparams(("arbitrary", "arbitrary")),
    )(*args)
    return (outs[0], v_first) if has_vres else (outs[0], outs[1])


def _attn_step(q_ref, k_ref, v_ref, m_sc, l_sc, acc_sc, masked):
    tq = q_ref.shape[1]
    tk = k_ref.shape[1]
    lane = lax.broadcasted_iota(jnp.int32, (tq, LANES), 1)
    if masked:
        rowi = lax.broadcasted_iota(jnp.int32, (tq, tk), 0)
        coli = lax.broadcasted_iota(jnp.int32, (tq, tk), 1)
        keep = coli <= rowi
    heads = range(MLA_HEADS)
    s = [_dot(q_ref[0, :, hd * LANES:(hd + 1) * LANES], k_ref[0, :, hd * LANES:(hd + 1) * LANES], NT)
         for hd in heads]
    if masked:
        s = [jnp.where(keep, x, NEG_INF) for x in s]
    m_prev = [m_sc[hd] for hd in heads]
    m_new = [jnp.maximum(m_prev[hd], jnp.max(s[hd], axis=-1, keepdims=True)) for hd in heads]
    alpha = [jnp.exp(m_prev[hd] - m_new[hd]) for hd in heads]
    p = [jnp.exp(s[hd] - jnp.concatenate([m_new[hd]] * (tk // LANES), axis=1)) for hd in heads]
    for hd in heads:
        l_sc[hd] = alpha[hd] * l_sc[hd] + jnp.sum(p[hd], axis=-1, keepdims=True)
        m_sc[hd] = m_new[hd]
    pv = [_dot(p[hd].astype(BF16), v_ref[0, :, (hd // 2) * LANES:(hd // 2 + 1) * LANES]) for hd in heads]
    first = lane < HEAD_DIM
    for pr in range(MLA_HEADS // 2):
        acc_sc[pr] = (acc_sc[pr] * jnp.where(first, alpha[2 * pr], alpha[2 * pr + 1])
                      + jnp.where(first, pv[2 * pr], pv[2 * pr + 1]))


def _attn_step_bounded(q_ref, k_ref, v_ref, cb_ref, l_sc, acc_sc, masked):
    tq = q_ref.shape[1]
    tk = k_ref.shape[1]
    lane = lax.broadcasted_iota(jnp.int32, (tq, LANES), 1)
    if masked:
        rowi = lax.broadcasted_iota(jnp.int32, (tq, tk), 0)
        coli = lax.broadcasted_iota(jnp.int32, (tq, tk), 1)
        keep = coli <= rowi
    ps = []
    for hd in range(MLA_HEADS):
        s = _dot(q_ref[0, :, hd * LANES:(hd + 1) * LANES], k_ref[0, :, hd * LANES:(hd + 1) * LANES], NT)
        c = cb_ref[0, hd:hd + 1, :]
        p = jnp.exp(s - jnp.concatenate([c] * (tk // LANES), axis=1))
        if masked:
            p = jnp.where(keep, p, 0.0)
        part = p[:, 0:LANES]
        for t in range(1, tk // LANES):
            part = part + p[:, t * LANES:(t + 1) * LANES]
        l_sc[hd] = l_sc[hd] + part
        ps.append(p.astype(BF16))
    first = lane < HEAD_DIM
    for pr in range(MLA_HEADS // 2):
        pv = _dot(jnp.concatenate(ps[2 * pr:2 * pr + 2], axis=0), v_ref[0, :, pr * LANES:(pr + 1) * LANES])
        acc_sc[pr] = acc_sc[pr] + jnp.where(first, pv[:tq], pv[tq:])


def _attn_finish(o_ref, l_sc, acc_sc, lane_partial):
    tq = o_ref.shape[1]
    lane = lax.broadcasted_iota(jnp.int32, (tq, LANES), 1)
    outs = []
    for pr in range(MLA_HEADS // 2):
        la, lb = l_sc[2 * pr], l_sc[2 * pr + 1]
        if lane_partial:
            la = jnp.sum(la, axis=-1, keepdims=True)
            lb = jnp.sum(lb, axis=-1, keepdims=True)
        outs.append(acc_sc[pr] / jnp.where(lane < HEAD_DIM, la, lb))
    o_ref[0] = jnp.concatenate(outs, axis=1).astype(BF16)


def _attn_kernel(ok_ref, qi_ref, kj_ref, q_ref, k_ref, v_ref, cb_ref, o_ref, m_sc, l_sc, acc_sc):
    i = qi_ref[pl.program_id(1)]
    j = kj_ref[pl.program_id(1)]
    bounded = ok_ref[pl.program_id(0)] == 1
    exact = jnp.logical_not(bounded)

    @pl.when(j == 0)
    def _():
        m_sc[...] = jnp.full(m_sc.shape, NEG_INF, F32)
        l_sc[...] = jnp.zeros(l_sc.shape, F32)
        acc_sc[...] = jnp.zeros(acc_sc.shape, F32)

    @pl.when(bounded & (j < i))
    def _():
        _attn_step_bounded(q_ref, k_ref, v_ref, cb_ref, l_sc, acc_sc, masked=False)

    @pl.when(bounded & (j == i))
    def _():
        _attn_step_bounded(q_ref, k_ref, v_ref, cb_ref, l_sc, acc_sc, masked=True)
        _attn_finish(o_ref, l_sc, acc_sc, lane_partial=True)

    @pl.when(exact & (j < i))
    def _():
        _attn_step(q_ref, k_ref, v_ref, m_sc, l_sc, acc_sc, masked=False)

    @pl.when(exact & (j == i))
    def _():
        _attn_step(q_ref, k_ref, v_ref, m_sc, l_sc, acc_sc, masked=True)
        _attn_finish(o_ref, l_sc, acc_sc, lane_partial=False)


def _attention(q, k, v, qk_gain):
    B, T, _ = q.shape
    nq = T // TQ
    gmax = jnp.max(jnp.abs(qk_gain), axis=1)
    c = gmax[0] * gmax[1] * (MLA_QK_DIM ** 0.5) * ATTN_BOUND_SLACK
    ok = jnp.broadcast_to((c <= ATTN_BOUND_MAX).astype(jnp.int32), (B,))
    cb = jnp.broadcast_to(c, (B, MLA_HEADS, LANES))
    pairs = [(i, j) for i in range(nq) for j in range(i + 1)]
    qi = jnp.asarray([p[0] for p in pairs], jnp.int32)
    kj = jnp.asarray([p[1] for p in pairs], jnp.int32)
    grid_spec = pltpu.PrefetchScalarGridSpec(
        num_scalar_prefetch=3,
        grid=(B, len(pairs)),
        in_specs=[pl.BlockSpec((1, TQ, MLA_HEADS * LANES), lambda b, t, ok, qi, kj: (b, qi[t], 0)),
                  pl.BlockSpec((1, TQ, MLA_HEADS * LANES), lambda b, t, ok, qi, kj: (b, kj[t], 0)),
                  pl.BlockSpec((1, TQ, MLA_HEADS * HEAD_DIM), lambda b, t, ok, qi, kj: (b, kj[t], 0)),
                  pl.BlockSpec((1, MLA_HEADS, LANES), lambda b, t, ok, qi, kj: (b, 0, 0))],
        out_specs=pl.BlockSpec((1, TQ, MLA_HEADS * HEAD_DIM), lambda b, t, ok, qi, kj: (b, qi[t], 0)),
        scratch_shapes=[pltpu.VMEM((MLA_HEADS, TQ, LANES), F32),
                        pltpu.VMEM((MLA_HEADS, TQ, LANES), F32),
                        pltpu.VMEM((MLA_HEADS // 2, TQ, LANES), F32)])
    return pl.pallas_call(
        _attn_kernel,
        grid_spec=grid_spec,
        out_shape=jax.ShapeDtypeStruct((B, T, MLA_HEADS * HEAD_DIM), BF16),
        compiler_params=_cparams(("arbitrary", "arbitrary")),
    )(ok, qi, kj, q, k, v, cb)


def _mixout_kernel(has_router, *refs):
    if has_router:
        (x_ref, yr_ref, yp_ref, ym_ref, wo_ref, mod_ref, gain_ref, rt_ref, tri_ref,
         xo_ref, h_ref, route_ref, cnt_ref, cnt_sc) = refs
    else:
        x_ref, yr_ref, yp_ref, ym_ref, wo_ref, mod_ref, gain_ref, xo_ref, h_ref = refs
    mod = mod_ref[0]
    o1 = RWKV_WIDTH
    o2 = RWKV_WIDTH + POOL_WIDTH
    mix = (_dot(yr_ref[0], wo_ref[0:o1, :]) + _dot(yp_ref[0], wo_ref[o1:o2, :])
           + _dot(ym_ref[0], wo_ref[o2:, :]))
    x = x_ref[0] + mod[2:3] * mix
    xo_ref[0] = x
    h = _rms(x) * gain_ref[...] * (1.0 + mod[4:5]) + mod[3:4]
    if not has_router:
        h_ref[0] = h.astype(BF16)
    else:
        hp = _pack_bf16_pairs(h)
        for ck in range(ROW_CHUNKS):
            h_ref[ck, 0] = hp[:, ck * PACK_CHUNK_W:(ck + 1) * PACK_CHUNK_W]
        logits = _dot(h.astype(BF16), rt_ref[...])
        lane = lax.broadcasted_iota(jnp.int32, logits.shape, 1).astype(F32)
        lg = jnp.where(lane < N_EXPERTS, logits, -jnp.inf)
        m1 = jnp.max(lg, axis=-1, keepdims=True)
        i1 = jnp.min(jnp.where(lg == m1, lane, float(LANES)), axis=-1, keepdims=True)
        lg2 = jnp.where(lane == i1, -jnp.inf, lg)
        m2 = jnp.max(lg2, axis=-1, keepdims=True)
        i2 = jnp.min(jnp.where(lg2 == m2, lane, float(LANES)), axis=-1, keepdims=True)
        e2 = jnp.exp(m2 - m1)
        g1 = 1.0 / (1.0 + e2)
        g2 = e2 / (1.0 + e2)
        first = (pl.program_id(0) == 0) & (pl.program_id(1) == 0)

        @pl.when(first)
        def _():
            cnt_sc[...] = jnp.zeros(cnt_sc.shape, F32)

        hit1 = lane == i1
        hit2 = lane == i2
        onehot = jnp.where(hit1 | hit2, 1.0, 0.0)
        prefix = _dot(tri_ref[...], onehot.astype(BF16)) + cnt_sc[0:1, :]
        r1 = jnp.sum(jnp.where(hit1, prefix, 0.0), axis=-1, keepdims=True)
        r2 = jnp.sum(jnp.where(hit2, prefix, 0.0), axis=-1, keepdims=True)
        cnt_sc[...] = cnt_sc[...] + jnp.sum(onehot, axis=0, keepdims=True)
        cnt_ref[...] = cnt_sc[...]
        vals = (i1, i2, g1, g2, r1, r2)
        route = jnp.zeros(logits.shape, F32)
        for pos, val in enumerate(vals):
            route = jnp.where(lane == pos, val, route)
        route_ref[0] = route


def _mixout(x, yr, yp, ym, wo, mod_l, gain, router_p):
    B, T, _ = x.shape
    tm = TM_OUT
    has_router = router_p is not None
    const = lambda shape: pl.BlockSpec(shape, lambda b, i: tuple(0 for _ in shape))
    tok = lambda w: pl.BlockSpec((1, tm, w), lambda b, i: (b, i, 0))
    in_specs = [tok(D_MODEL), tok(RWKV_WIDTH), tok(POOL_WIDTH), tok(MLA_HEADS * HEAD_DIM),
                const((D_MODEL, D_MODEL)), pl.BlockSpec((1, 6, D_MODEL), lambda b, i: (b, 0, 0)),
                const((1, D_MODEL))]
    args = [x, yr, yp, ym, wo, mod_l, gain]
    out_specs = [tok(D_MODEL), tok(D_MODEL)]
    out_shape = [jax.ShapeDtypeStruct((B, T, D_MODEL), F32), jax.ShapeDtypeStruct((B, T, D_MODEL), BF16)]
    scratch = []
    if has_router:
        out_specs[1] = pl.BlockSpec((ROW_CHUNKS, 1, tm, PACK_CHUNK_W), lambda b, i: (0, b, i, 0))
        out_shape[1] = jax.ShapeDtypeStruct((ROW_CHUNKS, B, T, PACK_CHUNK_W), jnp.uint32)
        ids = np.arange(tm)
        tri = jnp.asarray(ids[:, None] > ids[None, :], BF16)
        in_specs += [const((D_MODEL, LANES)), const((tm, tm))]
        args += [router_p, tri]
        out_specs += [tok(LANES), const((SUBLANES, LANES))]
        out_shape += [jax.ShapeDtypeStruct((B, T, LANES), F32),
                      jax.ShapeDtypeStruct((SUBLANES, LANES), F32)]
        scratch = [pltpu.VMEM((SUBLANES, LANES), F32)]
    return pl.pallas_call(
        functools.partial(_mixout_kernel, has_router),
        grid=(B, T // tm),
        in_specs=in_specs, out_specs=out_specs, out_shape=out_shape, scratch_shapes=scratch,
        compiler_params=_cparams(("arbitrary", "arbitrary")),
    )(*args)


def _ffn_kernel(h_ref, wg_ref, wu_ref, wo_ref, x_ref, mod_ref, o_ref, acc):
    j = pl.program_id(1)

    @pl.when(j == 0)
    def _():
        acc[...] = jnp.zeros(acc.shape, F32)

    h = h_ref[...]
    gg = _dot(h, wg_ref[...])
    uu = _dot(h, wu_ref[...])
    acc[...] += _dot((_silu(gg) * uu).astype(BF16), wo_ref[...])

    @pl.when(j == pl.num_programs(1) - 1)
    def _():
        o_ref[...] = x_ref[...] + mod_ref[0][5:6] * acc[...]


def _ffn(h2, w_in, w_out, x, mod_l):
    N = h2.shape[0]
    T = N // mod_l.shape[0]
    tm, tf = TM_FFN, TF_FFN
    nf = D_FF // tf
    per_b = T // tm
    return pl.pallas_call(
        _ffn_kernel,
        grid=(N // tm, nf),
        in_specs=[pl.BlockSpec((tm, D_MODEL), lambda i, j: (i, 0)),
                  pl.BlockSpec((D_MODEL, tf), lambda i, j: (0, j)),
                  pl.BlockSpec((D_MODEL, tf), lambda i, j: (0, j + nf)),
                  pl.BlockSpec((tf, D_MODEL), lambda i, j: (j, 0)),
                  pl.BlockSpec((tm, D_MODEL), lambda i, j: (i, 0)),
                  pl.BlockSpec((1, 6, D_MODEL), lambda i, j: (i // per_b, 0, 0))],
        out_specs=pl.BlockSpec((tm, D_MODEL), lambda i, j: (i, 0)),
        out_shape=jax.ShapeDtypeStruct((N, D_MODEL), F32),
        scratch_shapes=[pltpu.VMEM((tm, D_MODEL), F32)],
        compiler_params=_cparams(("arbitrary", "arbitrary")),
    )(h2, w_in, w_in, w_out, x, mod_l)


def _moe_kernel(be_ref, nv_ref, last_ref, x_ref, wg_ref, wu_ref, wo_ref, o_ref, acc, xm):
    i = pl.program_id(0)
    j = pl.program_id(1)

    @pl.when(i <= last_ref[0])
    def _():
        @pl.when(j == 0)
        def _():
            acc[...] = jnp.zeros(acc.shape, F32)
            row = lax.broadcasted_iota(jnp.int32, (xm.shape[0], 1), 0)
            xp = jnp.concatenate([x_ref[ck] for ck in range(ROW_CHUNKS)], axis=1)
            xp = jnp.where(row < nv_ref[i], xp, jnp.uint32(0))
            xm[...] = _unpack_bf16_pairs(xp)

        def swiglu_rows(n_rows):
            x = xm[0:n_rows, :]
            gg = _dot(x, wg_ref[0].astype(BF16))
            uu = _dot(x, wu_ref[0].astype(BF16))
            acc[0:n_rows, :] += _dot((_silu(gg) * uu).astype(BF16), wo_ref[0].astype(BF16))

        half = xm.shape[0] // 2

        @pl.when(nv_ref[i] > half)
        def _():
            swiglu_rows(xm.shape[0])

        @pl.when(nv_ref[i] <= half)
        def _():
            swiglu_rows(half)

        @pl.when(j == pl.num_programs(1) - 1)
        def _():
            yp = _pack_bf16_pairs(acc[...])
            for ck in range(ROW_CHUNKS):
                o_ref[ck] = yp[:, ck * PACK_CHUNK_W:(ck + 1) * PACK_CHUNK_W]


def _moe_experts(xs, w_in, w_out, block_exp, n_valid, last_blk):
    n_rows = xs.shape[1]
    tm, tf = MOE_BLOCK, TF_MOE
    nf = D_FF_EXPERT // tf
    blk = lambda i, last: jnp.minimum(i, last[0])
    chunk = lambda i, j, last: jnp.where(i <= last[0], j, nf - 1)
    grid_spec = pltpu.PrefetchScalarGridSpec(
        num_scalar_prefetch=3,
        grid=(n_rows // tm, nf),
        in_specs=[pl.BlockSpec((ROW_CHUNKS, tm, PACK_CHUNK_W),
                               lambda i, j, be, nv, last: (0, blk(i, last), 0)),
                  pl.BlockSpec((1, D_MODEL, tf),
                               lambda i, j, be, nv, last: (be[blk(i, last)], 0, chunk(i, j, last))),
                  pl.BlockSpec((1, D_MODEL, tf),
                               lambda i, j, be, nv, last: (be[blk(i, last)], 0, chunk(i, j, last) + nf)),
                  pl.BlockSpec((1, tf, D_MODEL),
                               lambda i, j, be, nv, last: (be[blk(i, last)], chunk(i, j, last), 0))],
        out_specs=pl.BlockSpec((ROW_CHUNKS, tm, PACK_CHUNK_W), lambda i, j, be, nv, last: (0, blk(i, last), 0)),
        scratch_shapes=[pltpu.VMEM((tm, D_MODEL), F32), pltpu.VMEM((tm, D_MODEL), BF16)])
    return pl.pallas_call(
        _moe_kernel,
        grid_spec=grid_spec,
        out_shape=jax.ShapeDtypeStruct((ROW_CHUNKS, n_rows, PACK_CHUNK_W), jnp.uint32),
        compiler_params=_cparams(("arbitrary", "arbitrary")),
    )(block_exp, n_valid, last_blk, xs, w_in, w_in, w_out)


def _sc_mesh():
    return plsc.VectorSubcoreMesh(core_axis_name="c", subcore_axis_name="s")


def _sc_scatter_rows(x, dest, n_rows):
    N, D = x.shape
    K = dest.shape[0]
    win = SC_WINDOW

    @pl.kernel(out_type=jax.ShapeDtypeStruct((n_rows, D), x.dtype), mesh=_sc_mesh(), scratch_types=[])
    def scatter(x_hbm, d_hbm, o_hbm):
        def body(x_vmem, *idx_vmem):
            for iv in idx_vmem:
                pltpu.sync_copy(x_vmem, o_hbm.at[iv.at[0]])

        pltpu.emit_pipeline(
            body,
            grid=(N // win,),
            in_specs=[pl.BlockSpec((win, D), lambda i: (i, 0))]
            + [pl.BlockSpec((1, win), functools.partial(lambda k, i: (k, i), k)) for k in range(K)],
            out_specs=[],
            core_axis_name=("c", "s"),
            dimension_semantics=(pltpu.PARALLEL,),
        )(x_hbm, *([d_hbm] * K))

    return scatter(x, dest)


def _sc_gather_rows(x, idx):
    n = idx.shape[0]
    D = x.shape[1]
    win = SC_WINDOW

    @pl.kernel(out_type=jax.ShapeDtypeStruct((n, D), x.dtype), mesh=_sc_mesh(), scratch_types=[])
    def gather(x_hbm, i_hbm, o_hbm):
        def body(i_vmem, o_vmem):
            pltpu.sync_copy(x_hbm.at[i_vmem.at[0]], o_vmem)

        pltpu.emit_pipeline(
            body,
            grid=(n // win,),
            in_specs=[pl.BlockSpec((1, win), lambda i: (0, i))],
            out_specs=[pl.BlockSpec((win, D), lambda i: (i, 0))],
            core_axis_name=("c", "s"),
            dimension_semantics=(pltpu.PARALLEL,),
        )(i_hbm, o_hbm)

    return gather(x, idx.reshape(1, n))


def _combine_kernel(x_ref, ya_ref, yb_ref, route_ref, mod_ref, o_ref):
    rt = route_ref[...]
    ya = _unpack_bf16_pairs(jnp.concatenate([ya_ref[0, ck] for ck in range(ROW_CHUNKS)], axis=1))
    yb = _unpack_bf16_pairs(jnp.concatenate([yb_ref[0, ck] for ck in range(ROW_CHUNKS)], axis=1))
    f = rt[:, 2:3] * ya.astype(F32) + rt[:, 3:4] * yb.astype(F32)
    o_ref[...] = x_ref[...] + mod_ref[0][5:6] * f


def _combine(x, y2, route, mod_l):
    N = x.shape[0]
    T = N // mod_l.shape[0]
    tm = 1024
    per_b = T // tm
    tok = pl.BlockSpec((tm, D_MODEL), lambda i: (i, 0))
    slot = lambda k: pl.BlockSpec((1, ROW_CHUNKS, tm, PACK_CHUNK_W), lambda i: (k, 0, i, 0))
    return pl.pallas_call(
        _combine_kernel,
        grid=(N // tm,),
        in_specs=[tok, slot(0), slot(1),
                  pl.BlockSpec((tm, LANES), lambda i: (i, 0)),
                  pl.BlockSpec((1, 6, D_MODEL), lambda i: (i // per_b, 0, 0))],
        out_specs=tok,
        out_shape=jax.ShapeDtypeStruct((N, D_MODEL), F32),
        compiler_params=_cparams(("arbitrary",)),
    )(x, y2, y2, route, mod_l)


def _moe(h2, route, counts, w_in, w_out, x, mod_l):
    N = x.shape[0]
    blk = MOE_BLOCK
    cnt = counts[0, :N_EXPERTS].astype(jnp.int32)
    padded = (cnt + blk - 1) // blk * blk
    pend = jnp.cumsum(padded)
    pstart = pend - padded
    e = route[:, 0:TOP_K].astype(jnp.int32)
    rank = route[:, 2 * TOP_K:3 * TOP_K].astype(jnp.int32)
    dest = (jnp.take(pstart, e) + rank).T
    n_blocks = N * TOP_K // blk + N_EXPERTS
    bstart = jnp.arange(n_blocks, dtype=jnp.int32) * blk
    block_exp = jnp.minimum(jnp.sum((bstart[:, None] >= pend[None, :]).astype(jnp.int32), axis=1),
                            N_EXPERTS - 1)
    n_valid = jnp.clip(cnt[block_exp] - (bstart - pstart[block_exp]), 0, blk).astype(jnp.int32)
    n_rows = n_blocks * blk
    dest_ck = dest[:, None, :] + (jnp.arange(ROW_CHUNKS, dtype=jnp.int32) * n_rows)[None, :, None]
    xs = _sc_scatter_rows(h2, dest_ck.reshape(TOP_K, ROW_CHUNKS * N), ROW_CHUNKS * n_rows)
    last_blk = (pend[-1:] // blk - 1).astype(jnp.int32)
    yb = _moe_experts(xs.reshape(ROW_CHUNKS, n_rows, PACK_CHUNK_W), w_in, w_out, block_exp, n_valid,
                      last_blk)
    y2 = _sc_gather_rows(yb.reshape(ROW_CHUNKS * n_rows, PACK_CHUNK_W), dest_ck.reshape(-1))
    return _combine(x, y2.reshape(TOP_K, ROW_CHUNKS, N, PACK_CHUNK_W), route, mod_l)


def _layout_w_in(w, has_vres):
    W = RWKV_WIDTH
    off_gd = 3 * W + RWKV_DECAY_LORA + RWKV_ICLR_LORA
    off_pool = off_gd + RWKV_GATE_LORA
    off_q = off_pool + POOL_WIDTH
    off_kv = off_q + MLA_Q_LORA
    off_kr = off_kv + MLA_KV_LORA
    n_base = off_kr + MLA_QK_ROPE
    d = w.shape[0]
    zeros = lambda n: jnp.zeros((d, n), w.dtype)
    vd = w[:, n_base:n_base + RWKV_VRES_LORA] if has_vres else zeros(RWKV_VRES_LORA)
    cols = [w[:, :off_gd], w[:, off_gd:off_pool], vd, zeros(ZR_COLS - off_pool - RWKV_VRES_LORA),
            w[:, off_pool:off_q], w[:, off_q:off_kv], w[:, off_kv:off_kr],
            zeros(MLA_QK_NOPE), w[:, off_kr:n_base], zeros(LANES - MLA_QK_DIM)]
    return jnp.concatenate(cols, axis=1).astype(BF16)


def _pad_heads(w, per_head, keep_from, keep_n):
    K = w.shape[0]
    wh = w.reshape(K, MLA_HEADS, per_head)[:, :, keep_from:keep_from + keep_n]
    wh = jnp.pad(wh, ((0, 0), (0, 0), (0, LANES - keep_n)))
    return wh.reshape(K, MLA_HEADS * LANES)


def kernel(x, c, positions, w_ada, b_ada, norm_gain, w_in_first, w_in_rest, mu_shift, mu_shift_v,
           rwkv_vec, rwkv_v0, rwkv_w2, rwkv_a2, rwkv_g2, rwkv_v2, pool_w, pool_scale,
           mla_q_lat_gain, mla_kv_lat_gain, mla_wq_up, mla_wkv_up, mla_qk_gain, w_out, ffn_w_in,
           ffn_w_out, moe_router, moe_w_in, moe_w_out):
    B, T, D = x.shape
    depth = w_ada.shape[0]
    W = RWKV_WIDTH
    mod = _adaln(c, w_ada, b_ada).reshape(depth, B, 6, D)
    inv_freq = ROPE_BASE ** (-jnp.arange(0, MLA_QK_ROPE, 2, dtype=F32) / MLA_QK_ROPE)
    cosf, sinf = _rope_tables(positions, inv_freq)
    hid = np.arange(LANES) // HEAD_DIM
    bd64 = jnp.asarray(hid[:, None] == hid[None, :], BF16)
    bd128 = jnp.ones((LANES, LANES), BF16)

    v_first = None
    for l in range(depth):
        has_vres = l > 0
        mod_l = mod[l]
        win = _layout_w_in(w_in_first if l == 0 else w_in_rest[l - 1], has_vres)
        poolw = jax.scipy.linalg.block_diag(*[pool_w[l, g] for g in range(len(POOL_WINDOWS))]).astype(BF16)
        wq = _pad_heads(mla_wq_up[l], MLA_QK_DIM, 0, MLA_QK_DIM).astype(BF16)
        wk = _pad_heads(mla_wkv_up[l], MLA_QK_NOPE + HEAD_DIM, 0, MLA_QK_NOPE).astype(BF16)
        wv = mla_wkv_up[l].reshape(MLA_KV_LORA, MLA_HEADS, MLA_QK_NOPE + HEAD_DIM)[:, :, MLA_QK_NOPE:]
        wv = wv.reshape(MLA_KV_LORA, MLA_HEADS * HEAD_DIM).astype(BF16)
        qkg = jnp.tile(jnp.pad(mla_qk_gain[l], ((0, 0), (0, LANES - MLA_QK_DIM))), (1, MLA_HEADS))
        zr, y_pool, q, k, v = _mixin(
            x, mod_l, norm_gain[l, 0].reshape(1, D), win, cosf, sinf, poolw,
            pool_scale[l].reshape(1, -1), mla_q_lat_gain[l].reshape(1, -1),
            mla_kv_lat_gain[l].reshape(1, -1), wq, wk, wv, qkg, bd128)

        pad_mu = ZR_COLS - mu_shift.shape[1] - RWKV_VRES_LORA
        mu_v = mu_shift_v[l - 1] if has_vres else jnp.zeros((RWKV_VRES_LORA,), F32)
        mu = jnp.concatenate([mu_shift[l], mu_v, jnp.zeros((pad_mu,), F32)]).reshape(1, ZR_COLS)
        v0 = rwkv_v0[l - 1] if has_vres else jnp.zeros((W,), F32)
        vec8 = jnp.concatenate([rwkv_vec[l], v0[None]], axis=0)
        w2a2 = jax.scipy.linalg.block_diag(rwkv_w2[l], rwkv_a2[l]).astype(BF16)
        g2 = jnp.pad(rwkv_g2[l], ((0, 2 * LANES - RWKV_GATE_LORA), (0, 0)))
        if has_vres:
            v2 = jnp.pad(rwkv_v2[l - 1], ((RWKV_GATE_LORA, 2 * LANES - RWKV_GATE_LORA - RWKV_VRES_LORA), (0, 0)))
        else:
            v2 = jnp.zeros((2 * LANES, W), F32)
        g2v2 = jnp.concatenate([g2, v2], axis=1).astype(BF16)
        y_rwkv, v_first = _rwkv(zr, v_first, mu, vec8, w2a2, g2v2, bd64)

        y_mla = _attention(q, k, v, mla_qk_gain[l])

        is_moe = (l % 2 == 1)
        router_p = None
        if is_moe:
            router_p = jnp.pad(moe_router[l // 2], ((0, 0), (0, LANES - N_EXPERTS))).astype(BF16)
        outs = _mixout(x, y_rwkv, y_pool, y_mla, w_out[l].astype(BF16), mod_l,
                       norm_gain[l, 1].reshape(1, D), router_p)
        x_mid, h2 = outs[0], outs[1]
        xf = x_mid.reshape(B * T, D)
        if is_moe:
            xo = _moe(h2.reshape(ROW_CHUNKS * B * T, PACK_CHUNK_W), outs[2].reshape(B * T, LANES), outs[3],
                      moe_w_in[l // 2], moe_w_out[l // 2], xf, mod_l)
        else:
            xo = _ffn(h2.reshape(B * T, D), ffn_w_in[l // 2].astype(BF16), ffn_w_out[l // 2].astype(BF16),
                      xf, mod_l)
        x = xo.reshape(B, T, D)
    return x
```

```python
import functools

import numpy as np
import jax
import jax.numpy as jnp
from jax import lax
from jax.experimental import pallas as pl
from jax.experimental.pallas import tpu as pltpu
from jax.experimental.pallas import tpu_sc as plsc

F32 = jnp.float32
BF16 = jnp.bfloat16

D_MODEL = 1024
HEAD_DIM = 64
RWKV_WIDTH = 512
RWKV_HEADS = RWKV_WIDTH // HEAD_DIM
POOL_WIDTH = 256
POOL_WINDOWS = (2, 4, 8, 16)
POOL_HALO = 16
MLA_HEADS = 4
MLA_QK_NOPE = 64
MLA_QK_ROPE = 32
MLA_QK_DIM = MLA_QK_NOPE + MLA_QK_ROPE
MLA_Q_LORA = 256
MLA_KV_LORA = 128
ROPE_BASE = 10000.0
RWKV_DECAY_LORA = 64
RWKV_ICLR_LORA = 64
RWKV_VRES_LORA = 32
RWKV_GATE_LORA = 160
RWKV_LNX_EPS = 64e-5
D_FF = 2816
N_EXPERTS = 8
TOP_K = 2
D_FF_EXPERT = 3584
NORM_EPS = 1e-6
NEG_INF = -1e30

LANES = 128
SUBLANES = 8
VMEM_LIMIT = 56 * 1024 * 1024

ZR_COLS = 3 * RWKV_WIDTH + (RWKV_DECAY_LORA + RWKV_ICLR_LORA) + 2 * LANES
Z_POOL_OFF = ZR_COLS
Z_QLAT_OFF = Z_POOL_OFF + POOL_WIDTH
Z_KVLAT_OFF = Z_QLAT_OFF + MLA_Q_LORA
Z_KROPE_OFF = Z_KVLAT_OFF + MLA_KV_LORA
Z_COLS = Z_KROPE_OFF + LANES

MIXIN_PIECE_W = 512
TM_MIX = 512
TM_OUT = 1024
WKV_CHUNK = 64
TQ = 512
ATTN_BOUND_SLACK = 1.02
ATTN_BOUND_MAX = 40.0
TM_FFN = 1024
TF_FFN = D_FF // 2
MOE_BLOCK = 1024
MXU_TILE = 256
TF_MOE = 2 * MXU_TILE
SC_WINDOW = 128
ROW_CHUNKS = 4
ROW_CHUNK_W = D_MODEL // ROW_CHUNKS
PACK_CHUNK_W = ROW_CHUNK_W // 2

SEGSUM_SPLITS = 1

NN = (((1,), (0,)), ((), ()))
NT = (((1,), (1,)), ((), ()))


def _dot(a, b, dims=NN):
    return lax.dot_general(a, b, dims, preferred_element_type=F32)


def _split2(a):
    hi = a.astype(BF16)
    lo = (a - hi.astype(F32)).astype(BF16)
    return hi, lo


def _mm(a, b, dims=NN, passes=3):
    if passes == 1:
        return _dot(a.astype(BF16), b.astype(BF16), dims)
    ah, al = _split2(a)
    bh, bl = _split2(b)
    return _dot(ah, bh, dims) + (_dot(ah, bl, dims) + _dot(al, bh, dims))


def _mm_exact_rhs(a, b_bf16, splits=SEGSUM_SPLITS):
    m, w = a.shape
    nb = w // LANES
    stacked = jnp.concatenate([a[:, i * LANES:(i + 1) * LANES] for i in range(nb)], axis=0)
    out = None
    rem = stacked
    for s in range(splits):
        part = rem.astype(BF16)
        term = _dot(part, b_bf16)
        out = term if out is None else out + term
        if s + 1 < splits:
            rem = rem - part.astype(F32)
    return jnp.concatenate([out[i * m:(i + 1) * m] for i in range(nb)], axis=1)


def _pack_bf16_pairs(h):
    w = h.shape[1] // 2
    lo = lax.bitcast_convert_type(h[:, :w].astype(BF16).astype(F32), jnp.uint32)
    hi = lax.bitcast_convert_type(h[:, w:].astype(BF16).astype(F32), jnp.uint32)
    return (lo >> 16) | (hi & jnp.uint32(0xFFFF0000))


def _unpack_bf16_pairs(p):
    lo = lax.bitcast_convert_type(p << 16, F32)
    hi = lax.bitcast_convert_type(p & jnp.uint32(0xFFFF0000), F32)
    return jnp.concatenate([lo, hi], axis=1).astype(BF16)


def _sigmoid(x):
    return 1.0 / (1.0 + jnp.exp(-x))


def _silu(x):
    return x * _sigmoid(x)


def _rms(x, eps=NORM_EPS):
    return x * lax.rsqrt(jnp.mean(x * x, axis=-1, keepdims=True) + eps)


def _cparams(sem):
    return pltpu.CompilerParams(dimension_semantics=sem, vmem_limit_bytes=VMEM_LIMIT)


def _adaln_kernel(c_ref, w_ref, b_ref, o_ref):
    ca = _silu(c_ref[...])
    o_ref[0] = _mm(ca, w_ref[0]) + b_ref[0]


def _adaln(c, w_ada, b_ada):
    L = w_ada.shape[0]
    B = c.shape[0]
    n = w_ada.shape[2] // D_MODEL
    return pl.pallas_call(
        _adaln_kernel,
        grid=(L, n),
        in_specs=[pl.BlockSpec((B, D_MODEL), lambda l, j: (0, 0)),
                  pl.BlockSpec((1, D_MODEL, D_MODEL), lambda l, j: (l, 0, j)),
                  pl.BlockSpec((1, 1, D_MODEL), lambda l, j: (l, 0, j))],
        out_specs=pl.BlockSpec((1, B, D_MODEL), lambda l, j: (l, 0, j)),
        out_shape=jax.ShapeDtypeStruct((L, B, n * D_MODEL), F32),
        compiler_params=_cparams(("arbitrary", "arbitrary")),
    )(c, w_ada, b_ada.reshape(L, 1, -1))


def _rope(x, cosf, sinf, lane):
    up = pltpu.roll(x, LANES - MLA_QK_ROPE // 2, axis=1)
    dn = pltpu.roll(x, MLA_QK_ROPE // 2, axis=1)
    rot = jnp.where(lane < MLA_QK_NOPE + MLA_QK_ROPE // 2, -up, dn)
    return x * cosf + rot * sinf


def _rope_kernel(pos_ref, freq_ref, cos_ref, sin_ref):
    ang = pos_ref[0].astype(F32) * freq_ref[...]
    cos_ref[0] = jnp.cos(ang)
    sin_ref[0] = jnp.sin(ang)


def _rope_tables(positions, inv_freq):
    B, T = positions.shape
    nf = inv_freq.shape[0]
    per_row = LANES // nf
    pos_p = jnp.broadcast_to(positions[:, :, None], (B, T, nf)).reshape(B, T // per_row, LANES)
    freq_p = jnp.tile(inv_freq, per_row).reshape(1, LANES)
    blk = pl.BlockSpec((1, T // per_row, LANES), lambda b: (b, 0, 0))
    cos_p, sin_p = pl.pallas_call(
        _rope_kernel,
        grid=(B,),
        in_specs=[blk, pl.BlockSpec((1, LANES), lambda b: (0, 0))],
        out_specs=[blk, blk],
        out_shape=[jax.ShapeDtypeStruct((B, T // per_row, LANES), F32)] * 2,
        compiler_params=_cparams(("arbitrary",)),
    )(pos_p, freq_p)

    def spread(tab, fill):
        t16 = tab.reshape(B, T, nf)
        return jnp.concatenate([jnp.full((B, T, MLA_QK_NOPE), fill, F32), t16, t16,
                                jnp.full((B, T, LANES - MLA_QK_DIM), fill, F32)], axis=-1)

    return spread(cos_p, 1.0), spread(sin_p, 0.0)


def _mixin_kernel(x_ref, mod_ref, gain_ref, win_ref, cos_ref, sin_ref, poolw_ref, pools_ref,
                  qg_ref, kvg_ref, wq_ref, wk_ref, wv_ref, qkg_ref, bd_ref,
                  zr_ref, yp_ref, q_ref, k_ref, v_ref, ubuf):
    i = pl.program_id(1)
    tm = x_ref.shape[1]
    x = x_ref[0]
    mod = mod_ref[0]
    @pl.when(i == 0)
    def _():
        ubuf[0:POOL_HALO, :] = jnp.zeros((POOL_HALO, POOL_WIDTH), F32)

    h = (_rms(x) * gain_ref[...] * (1.0 + mod[1:2]) + mod[0:1]).astype(BF16)
    zb = _dot(h, win_ref[:, ZR_COLS:])
    zcol = lambda off, w: zb[:, off - ZR_COLS:off - ZR_COLS + w]
    def project_piece(n):
        cols = slice(n * MIXIN_PIECE_W, min((n + 1) * MIXIN_PIECE_W, ZR_COLS))
        zr_ref[0, :, cols] = _dot(h, win_ref[:, cols])

    project_piece(0)
    u = zcol(Z_POOL_OFF, POOL_WIDTH)
    ubuf[POOL_HALO:, :] = u
    ue = ubuf[...]
    s2 = ue + pltpu.roll(ue, 1, axis=0)
    s4 = s2 + pltpu.roll(s2, 2, axis=0)
    s8 = s4 + pltpu.roll(s4, 4, axis=0)
    s16 = s8 + pltpu.roll(s8, 8, axis=0)
    ubuf[0:POOL_HALO, :] = u[tm - POOL_HALO:, :]
    lane_p = lax.broadcasted_iota(jnp.int32, (tm, POOL_WIDTH), 1)
    grp = lane_p // (POOL_WIDTH // len(POOL_WINDOWS))
    win_sum = jnp.where(grp == 0, s2[POOL_HALO:], jnp.where(grp == 1, s4[POOL_HALO:],
                        jnp.where(grp == 2, s8[POOL_HALO:], s16[POOL_HALO:])))
    win = jnp.where(grp == 0, 2, jnp.where(grp == 1, 4, jnp.where(grp == 2, 8, 16)))
    t_abs = i * tm + lax.broadcasted_iota(jnp.int32, (tm, POOL_WIDTH), 0)
    cnt = jnp.minimum(t_abs + 1, win).astype(F32)
    p = win_sum / cnt - u
    yp = _dot(p.astype(BF16), poolw_ref[...]) * pools_ref[...]
    yp_ref[0] = yp.astype(BF16)
    project_piece(1)

    lane = lax.broadcasted_iota(jnp.int32, (tm, LANES), 1)
    cosf = cos_ref[0]
    sinf = sin_ref[0]

    q_lat = zcol(Z_QLAT_OFF, MLA_Q_LORA)
    kv_lat = zcol(Z_KVLAT_OFF, MLA_KV_LORA)
    k_rope = zcol(Z_KROPE_OFF, LANES)
    qn = (_rms(q_lat) * qg_ref[...]).astype(BF16)
    kvn = (_rms(kv_lat) * kvg_ref[...]).astype(BF16)
    q = _dot(qn, wq_ref[...])
    kx = _dot(kvn, wk_ref[...])
    v = _dot(kvn, wv_ref[...])
    v_ref[0] = v.astype(BF16)
    project_piece(2)
    k_pe = _rope(k_rope, cosf, sinf, lane)
    qs, ks = [], []
    for hd in range(MLA_HEADS):
        sl = slice(hd * LANES, (hd + 1) * LANES)
        qs.append(_rope(q[:, sl], cosf, sinf, lane))
        ks.append(kx[:, sl] + k_pe)
    q = jnp.concatenate(qs, axis=1)
    k = jnp.concatenate(ks, axis=1)
    project_piece(3)
    qss = _mm_exact_rhs(q * q, bd_ref[...]) * (1.0 / MLA_QK_DIM)
    kss = _mm_exact_rhs(k * k, bd_ref[...]) * (1.0 / MLA_QK_DIM)
    qkg = qkg_ref[...]
    q = q * lax.rsqrt(qss + NORM_EPS) * qkg[0:1] * (MLA_QK_DIM ** -0.5)
    k = k * lax.rsqrt(kss + NORM_EPS) * qkg[1:2]
    q_ref[0] = q.astype(BF16)
    k_ref[0] = k.astype(BF16)


def _mixin(x, mod_l, gain, win, cosf, sinf, poolw, pools, qg, kvg, wq, wk, wv, qkg, bd128):
    B, T, _ = x.shape
    tm = TM_MIX
    const = lambda shape: pl.BlockSpec(shape, lambda b, i: tuple(0 for _ in shape))
    tok = lambda w: pl.BlockSpec((1, tm, w), lambda b, i: (b, i, 0))
    return pl.pallas_call(
        _mixin_kernel,
        grid=(B, T // tm),
        in_specs=[tok(D_MODEL),
                  pl.BlockSpec((1, 6, D_MODEL), lambda b, i: (b, 0, 0)),
                  const((1, D_MODEL)), const((D_MODEL, Z_COLS)),
                  tok(LANES), tok(LANES),
                  const((POOL_WIDTH, POOL_WIDTH)), const((1, POOL_WIDTH)),
                  const((1, MLA_Q_LORA)), const((1, MLA_KV_LORA)),
                  const((MLA_Q_LORA, MLA_HEADS * LANES)), const((MLA_KV_LORA, MLA_HEADS * LANES)),
                  const((MLA_KV_LORA, MLA_HEADS * HEAD_DIM)), const((2, MLA_HEADS * LANES)),
                  const((LANES, LANES))],
        out_specs=[tok(ZR_COLS), tok(POOL_WIDTH), tok(MLA_HEADS * LANES), tok(MLA_HEADS * LANES),
                   tok(MLA_HEADS * HEAD_DIM)],
        out_shape=[jax.ShapeDtypeStruct((B, T, ZR_COLS), F32),
                   jax.ShapeDtypeStruct((B, T, POOL_WIDTH), BF16),
                   jax.ShapeDtypeStruct((B, T, MLA_HEADS * LANES), BF16),
                   jax.ShapeDtypeStruct((B, T, MLA_HEADS * LANES), BF16),
                   jax.ShapeDtypeStruct((B, T, MLA_HEADS * HEAD_DIM), BF16)],
        scratch_shapes=[pltpu.VMEM((POOL_HALO + tm, POOL_WIDTH), F32)],
        compiler_params=_cparams(("arbitrary", "arbitrary")),
    )(x, mod_l, gain, win, cosf, sinf, poolw, pools, qg, kvg, wq, wk, wv, qkg, bd128)


WKV_PASSES_SCORE = 1
WKV_PASSES_INV = 1
WKV_PASSES_APPLY = 1
WKV_PASSES_STATE = 1
WKV_STEP_CHUNKS = 1
WKV_STEP_SEQS = 8
WKV_GROUPS = 2
WKV_GROUP_LEAD = 4


def _stack_heads(xp, lane):
    return jnp.concatenate([jnp.where(lane < HEAD_DIM, xp, 0.0),
                            jnp.where(lane >= HEAD_DIM, xp, 0.0)], axis=0)


def _wkv_prep(r, lw, k, v, kk, a, tri, masks):
    L = r[0].shape[0]
    nc = len(r)
    each = lambda f, *ls: [f(*xs) for xs in zip(*ls)]
    lane = lax.broadcasted_iota(jnp.int32, (L, LANES), 1)
    stack = lambda x: _stack_heads(x, lane)
    cum = each(lambda x: _mm_exact_rhs_left(tri, x), lw)
    cum_last = each(lambda c: c[L - 1:L, :], cum)
    e_w = each(jnp.exp, cum)
    e_wm = each(lambda c, x: jnp.exp(c - x), cum, lw)
    e_iw = each(lambda c: jnp.exp(-c), cum)
    e_d = each(lambda cl, c: jnp.exp(cl - c), cum_last, cum)
    beta = each(lambda x, y: x * y, kk, a)
    r_f = each(lambda x, e: x * e, r, e_w)
    a_f = each(lambda x, e: -x * e, kk, e_wm)
    a_s = each(stack, a_f)
    b_s = each(lambda x, e: stack(x * e), beta, e_iw)
    k_s = each(lambda x, e: stack(x * e), k, e_iw)
    b_d = each(lambda x, e: stack(x * e), beta, e_d)
    k_d = each(lambda x, e: stack(x * e), k, e_d)
    v_s = each(stack, v)
    yield
    g = each(lambda af, rf, bs, ks: _mm(jnp.concatenate([af, rf], axis=0),
                                        jnp.concatenate([bs, ks], axis=0), NT, WKV_PASSES_SCORE),
             a_f, r_f, b_s, k_s)
    strict, incl, levels = masks
    a_ab = each(lambda x: jnp.where(strict, x[:L, :LANES], 0.0), g)
    a_ak = each(lambda x: jnp.where(strict, x[:L, LANES:], 0.0), g)
    s_rb = each(lambda x: jnp.where(incl, x[L:, :LANES], 0.0), g)
    s_rk = each(lambda x: jnp.where(incl, x[L:, LANES:], 0.0), g)
    eye = jnp.where(levels[0][1], 1.0, 0.0)
    tinv = each(lambda x: eye + jnp.where(levels[0][0], x, 0.0), a_ab)
    yield
    for lvl_mask, _ in levels[1:]:
        et = each(lambda x, t: _mm(jnp.where(lvl_mask, x, 0.0), stack(t), NN, WKV_PASSES_INV), a_ab, tinv)
        tinv = each(lambda t, x: t + _mm(t, stack(x), NN, WKV_PASSES_INV), tinv, et)
        yield
    av = each(lambda x, y: _mm(x, y, NN, WKV_PASSES_APPLY), a_ak, v_s)
    tx = each(lambda t, x, y: _mm(t, jnp.concatenate([x, stack(y)], axis=1), NN, WKV_PASSES_APPLY),
              tinv, a_s, av)
    ta_s = each(lambda x: stack(x[:, :LANES]), tx)
    c1_s = each(lambda x: stack(x[:, LANES:]), tx)
    yield
    ra = each(lambda rf, s, x: rf + _mm(s, x, NN, WKV_PASSES_APPLY), r_f, s_rb, ta_s)
    c2 = each(lambda sb, sk, x, vs: _mm(jnp.concatenate([sb, sk], axis=1),
                                        jnp.concatenate([x, vs], axis=0),
                                        NN, WKV_PASSES_APPLY), s_rb, s_rk, c1_s, v_s)
    yield
    tb = each(lambda x, y, bd: _mm(jnp.concatenate([x, y], axis=1).T, bd, NN, WKV_PASSES_APPLY),
              ta_s, c1_s, b_d)
    c3 = each(lambda x, vs, kd: x[LANES:] + _mm(vs.T, kd, NN, WKV_PASSES_APPLY), tb, v_s, k_d)
    return [(ra[i], c2[i], jnp.exp(cum_last[i]), tb[i][:LANES], c3[i]) for i in range(nc)]


def _mm_exact_rhs_left(tri_bf16, x):
    x0 = x.astype(BF16)
    r1 = x - x0.astype(F32)
    x1 = r1.astype(BF16)
    x2 = (r1 - x1.astype(F32)).astype(BF16)
    return _dot(tri_bf16, x0) + (_dot(tri_bf16, x1) + _dot(tri_bf16, x2))


def _wkv_masks(L):
    row = lax.broadcasted_iota(jnp.int32, (L, 2 * L), 0)
    col = lax.broadcasted_iota(jnp.int32, (L, 2 * L), 1) % L
    strict = row > col
    incl = row >= col
    levels = []
    m = 1
    while m < L:
        same = (row // (2 * m)) == (col // (2 * m))
        lvl = same & ((row % (2 * m)) >= m) & ((col % (2 * m)) < m)
        levels.append((lvl, row == col))
        m *= 2
    return strict, incl, levels


def _rwkv_kernel(has_vres, *refs):
    if has_vres:
        (z_ref, vf_ref, mu_ref, vec_ref, w2a2_ref, g2v2_ref, bd_ref,
         y_ref, carry, state) = refs
    else:
        (z_ref, mu_ref, vec_ref, w2a2_ref, g2v2_ref, bd_ref,
         y_ref, vout_ref, carry, state) = refs
    c = pl.program_id(1)
    n_seq, seq_rows = z_ref.shape[0], z_ref.shape[1]
    rows = n_seq * seq_rows
    L = WKV_CHUNK
    W = RWKV_WIDTH

    @pl.when(c == 0)
    def _():
        carry[...] = jnp.zeros(carry.shape, F32)
        state[...] = jnp.zeros(state.shape, F32)

    flat = lambda ref: jnp.concatenate([ref[s] for s in range(n_seq)], axis=0)
    z = flat(z_ref)
    row = lax.broadcasted_iota(jnp.int32, z.shape, 0)
    prev = pltpu.roll(z, 1, axis=0)
    for s in range(n_seq):
        prev = jnp.where(row == s * seq_rows, carry[s, SUBLANES - 1:SUBLANES, :], prev)
        carry[s] = z[(s + 1) * seq_rows - SUBLANES:(s + 1) * seq_rows, :]
    zs_all = z + mu_ref[...] * (prev - z)
    if has_vres:
        vf_all = flat(vf_ref)
    vec = vec_ref[...]
    w0, a0, k_k, k_a, r_k, ln_g, ln_b, v0 = (vec[j:j + 1] for j in range(8))
    bd = bd_ref[...]
    masks = _wkv_masks(L)
    rowt = lax.broadcasted_iota(jnp.int32, (L, L), 0)
    colt = lax.broadcasted_iota(jnp.int32, (L, L), 1)
    tri = jnp.where(rowt >= colt, 1.0, 0.0).astype(BF16)
    n_pairs = RWKV_HEADS // 2
    S_now = [{(s, p): state[s * n_pairs + p] for s in range(n_seq) for p in range(n_pairs)}]

    def group(s0, s1):
        seqs = range(s0, s1)
        r0, r1 = s0 * seq_rows, s1 * seq_rows
        zs = zs_all[r0:r1]
        r = zs[:, 0:W]
        k = zs[:, W:2 * W]
        v = zs[:, 2 * W:3 * W]
        wa = zs[:, 3 * W:3 * W + LANES]
        gb = zs[:, 3 * W + LANES:ZR_COLS]
        lane_a = lax.broadcasted_iota(jnp.int32, wa.shape, 1)
        t1 = _dot(jnp.where(lane_a < RWKV_DECAY_LORA, jnp.tanh(wa), wa).astype(BF16), w2a2_ref[...])
        lane_g = lax.broadcasted_iota(jnp.int32, gb.shape, 1)
        t2 = _dot(jnp.where(lane_g < RWKV_GATE_LORA, _sigmoid(gb), gb).astype(BF16), g2v2_ref[...])
        yield
        xw = w0 + t1[:, :W]
        w_log = -(jnp.maximum(-xw, 0.0) + jnp.log(1.0 + jnp.exp(-jnp.abs(xw)))) - 0.5
        lw = -jnp.exp(w_log)
        a = _sigmoid(a0 + t1[:, W:])
        g = t2[:, :W]
        if has_vres:
            v = v + (vf_all[r0:r1] - v) * _sigmoid(v0 + t2[:, W:])
        else:
            for s in seqs:
                vout_ref[s] = v[(s - s0) * seq_rows:(s - s0 + 1) * seq_rows]
        kk = k * k_k
        kk = kk * jnp.minimum(lax.rsqrt(_mm_exact_rhs(kk * kk, bd)), 1e12)
        k = k * (1.0 + (a - 1.0) * k_a)
        yield
        n_chunks = seq_rows // L
        idx = [(s, ch, p) for s in seqs for ch in range(n_chunks) for p in range(n_pairs)]
        row0 = lambda s, ch: (s - s0) * seq_rows + ch * L
        cut = lambda x: [x[row0(s, ch):row0(s, ch) + L, p * LANES:(p + 1) * LANES] for s, ch, p in idx]
        res = yield from _wkv_prep(cut(r), cut(lw), cut(k), cut(v), cut(kk), cut(a), tri, masks)
        prep = dict(zip(idx, res))
        yield
        sp = [(s, p) for s in seqs for p in range(n_pairs)]
        S = {key: S_now[0][key] for key in sp}
        y_blk = {}
        for ch in range(n_chunks):
            y_s = {(s, p): _mm(prep[s, ch, p][0], S[s, p], NT, WKV_PASSES_STATE) + prep[s, ch, p][1]
                   for s, p in sp}
            for s in seqs:
                y_blk[s, ch] = jnp.concatenate([y_s[s, p] for p in range(n_pairs)], axis=1)
            S = {(s, p): S[s, p] * prep[s, ch, p][2] + _mm(S[s, p], prep[s, ch, p][3], NN, WKV_PASSES_STATE)
                 + prep[s, ch, p][4] for s, p in sp}
        S_now[0].update(S)
        y = jnp.concatenate([y_blk[s, ch] for s in seqs for ch in range(n_chunks)], axis=0)
        yield
        inv = 1.0 / HEAD_DIM
        mean = _mm_exact_rhs(y, bd) * inv
        yc = y - mean
        var = _mm_exact_rhs(yc * yc, bd) * inv
        yn = yc * lax.rsqrt(var + RWKV_LNX_EPS) * ln_g + ln_b
        bonus = _mm_exact_rhs(r * k * r_k, bd) * v
        out = ((yn + bonus) * g).astype(BF16)
        for s in seqs:
            y_ref[s] = out[(s - s0) * seq_rows:(s - s0 + 1) * seq_rows]

    per = n_seq // WKV_GROUPS
    gens = [group(gi * per, (gi + 1) * per) for gi in range(WKV_GROUPS)]
    alive = [True] * WKV_GROUPS
    tick = 0
    while any(alive):
        for gi in range(WKV_GROUPS):
            if alive[gi] and tick >= gi * WKV_GROUP_LEAD:
                try:
                    next(gens[gi])
                except StopIteration:
                    alive[gi] = False
        tick += 1
    for s in range(n_seq):
        for p in range(n_pairs):
            state[s * n_pairs + p] = S_now[0][s, p]


def _rwkv(zr, v_first, mu, vec8, w2a2, g2v2, bd64):
    B, T, _ = zr.shape
    L = WKV_CHUNK * WKV_STEP_CHUNKS
    ns = WKV_STEP_SEQS
    has_vres = v_first is not None
    const = lambda shape: pl.BlockSpec(shape, lambda b, c: tuple(0 for _ in shape))
    tok = lambda w: pl.BlockSpec((ns, L, w), lambda b, c: (b, c, 0))
    in_specs = [tok(ZR_COLS)]
    args = [zr]
    if has_vres:
        in_specs.append(tok(RWKV_WIDTH))
        args.append(v_first)
    in_specs += [const((1, ZR_COLS)), const((8, RWKV_WIDTH)), const((LANES, 2 * RWKV_WIDTH)),
                 const((2 * LANES, 2 * RWKV_WIDTH)), const((LANES, LANES))]
    args += [mu, vec8, w2a2, g2v2, bd64]
    out_specs = [tok(RWKV_WIDTH)]
    out_shape = [jax.ShapeDtypeStruct((B, T, RWKV_WIDTH), BF16)]
    if not has_vres:
        out_specs.append(tok(RWKV_WIDTH))
        out_shape.append(jax.ShapeDtypeStruct((B, T, RWKV_WIDTH), F32))
    outs = pl.pallas_call(
        functools.partial(_rwkv_kernel, has_vres),
        grid=(B // ns, T // L),
        in_specs=in_specs, out_specs=out_specs, out_shape=out_shape,
        scratch_shapes=[pltpu.VMEM((ns, SUBLANES, ZR_COLS), F32),
                        pltpu.VMEM((ns * (RWKV_HEADS // 2), LANES, LANES), F32)],
        compiler_params=_cparams(("arbitrary", "arbitrary")),
    )(*args)
    return (outs[0], v_first) if has_vres else (outs[0], outs[1])


def _attn_step(q_ref, k_ref, v_ref, m_sc, l_sc, acc_sc, masked):
    tq = q_ref.shape[1]
    tk = k_ref.shape[1]
    lane = lax.broadcasted_iota(jnp.int32, (tq, LANES), 1)
    if masked:
        rowi = lax.broadcasted_iota(jnp.int32, (tq, tk), 0)
        coli = lax.broadcasted_iota(jnp.int32, (tq, tk), 1)
        keep = coli <= rowi
    heads = range(MLA_HEADS)
    s = [_dot(q_ref[0, :, hd * LANES:(hd + 1) * LANES], k_ref[0, :, hd * LANES:(hd + 1) * LANES], NT)
         for hd in heads]
    if masked:
        s = [jnp.where(keep, x, NEG_INF) for x in s]
    m_prev = [m_sc[hd] for hd in heads]
    m_new = [jnp.maximum(m_prev[hd], jnp.max(s[hd], axis=-1, keepdims=True)) for hd in heads]
    alpha = [jnp.exp(m_prev[hd] - m_new[hd]) for hd in heads]
    p = [jnp.exp(s[hd] - jnp.concatenate([m_new[hd]] * (tk // LANES), axis=1)) for hd in heads]
    for hd in heads:
        l_sc[hd] = alpha[hd] * l_sc[hd] + jnp.sum(p[hd], axis=-1, keepdims=True)
        m_sc[hd] = m_new[hd]
    pv = [_dot(p[hd].astype(BF16), v_ref[0, :, (hd // 2) * LANES:(hd // 2 + 1) * LANES]) for hd in heads]
    first = lane < HEAD_DIM
    for pr in range(MLA_HEADS // 2):
        acc_sc[pr] = (acc_sc[pr] * jnp.where(first, alpha[2 * pr], alpha[2 * pr + 1])
                      + jnp.where(first, pv[2 * pr], pv[2 * pr + 1]))


def _attn_step_bounded(q_ref, k_ref, v_ref, cb_ref, l_sc, acc_sc, masked):
    tq = q_ref.shape[1]
    tk = k_ref.shape[1]
    lane = lax.broadcasted_iota(jnp.int32, (tq, LANES), 1)
    if masked:
        rowi = lax.broadcasted_iota(jnp.int32, (tq, tk), 0)
        coli = lax.broadcasted_iota(jnp.int32, (tq, tk), 1)
        keep = coli <= rowi
    ps = []
    for hd in range(MLA_HEADS):
        s = _dot(q_ref[0, :, hd * LANES:(hd + 1) * LANES], k_ref[0, :, hd * LANES:(hd + 1) * LANES], NT)
        c = cb_ref[0, hd:hd + 1, :]
        p = jnp.exp(s - jnp.concatenate([c] * (tk // LANES), axis=1))
        if masked:
            p = jnp.where(keep, p, 0.0)
        part = p[:, 0:LANES]
        for t in range(1, tk // LANES):
            part = part + p[:, t * LANES:(t + 1) * LANES]
        l_sc[hd] = l_sc[hd] + part
        ps.append(p.astype(BF16))
    first = lane < HEAD_DIM
    for pr in range(MLA_HEADS // 2):
        pv = _dot(jnp.concatenate(ps[2 * pr:2 * pr + 2], axis=0), v_ref[0, :, pr * LANES:(pr + 1) * LANES])
        acc_sc[pr] = acc_sc[pr] + jnp.where(first, pv[:tq], pv[tq:])


def _attn_finish(o_ref, l_sc, acc_sc, lane_partial):
    tq = o_ref.shape[1]
    lane = lax.broadcasted_iota(jnp.int32, (tq, LANES), 1)
    outs = []
    for pr in range(MLA_HEADS // 2):
        la, lb = l_sc[2 * pr], l_sc[2 * pr + 1]
        if lane_partial:
            la = jnp.sum(la, axis=-1, keepdims=True)
            lb = jnp.sum(lb, axis=-1, keepdims=True)
        outs.append(acc_sc[pr] / jnp.where(lane < HEAD_DIM, la, lb))
    o_ref[0] = jnp.concatenate(outs, axis=1).astype(BF16)


def _attn_kernel(ok_ref, qi_ref, kj_ref, q_ref, k_ref, v_ref, cb_ref, o_ref, m_sc, l_sc, acc_sc):
    i = qi_ref[pl.program_id(1)]
    j = kj_ref[pl.program_id(1)]
    bounded = ok_ref[pl.program_id(0)] == 1
    exact = jnp.logical_not(bounded)

    @pl.when(j == 0)
    def _():
        m_sc[...] = jnp.full(m_sc.shape, NEG_INF, F32)
        l_sc[...] = jnp.zeros(l_sc.shape, F32)
        acc_sc[...] = jnp.zeros(acc_sc.shape, F32)

    @pl.when(bounded & (j < i))
    def _():
        _attn_step_bounded(q_ref, k_ref, v_ref, cb_ref, l_sc, acc_sc, masked=False)

    @pl.when(bounded & (j == i))
    def _():
        _attn_step_bounded(q_ref, k_ref, v_ref, cb_ref, l_sc, acc_sc, masked=True)
        _attn_finish(o_ref, l_sc, acc_sc, lane_partial=True)

    @pl.when(exact & (j < i))
    def _():
        _attn_step(q_ref, k_ref, v_ref, m_sc, l_sc, acc_sc, masked=False)

    @pl.when(exact & (j == i))
    def _():
        _attn_step(q_ref, k_ref, v_ref, m_sc, l_sc, acc_sc, masked=True)
        _attn_finish(o_ref, l_sc, acc_sc, lane_partial=False)


def _attention(q, k, v, qk_gain):
    B, T, _ = q.shape
    nq = T // TQ
    gmax = jnp.max(jnp.abs(qk_gain), axis=1)
    c = gmax[0] * gmax[1] * (MLA_QK_DIM ** 0.5) * ATTN_BOUND_SLACK
    ok = jnp.broadcast_to((c <= ATTN_BOUND_MAX).astype(jnp.int32), (B,))
    cb = jnp.broadcast_to(c, (B, MLA_HEADS, LANES))
    pairs = [(i, j) for i in range(nq) for j in range(i + 1)]
    qi = jnp.asarray([p[0] for p in pairs], jnp.int32)
    kj = jnp.asarray([p[1] for p in pairs], jnp.int32)
    grid_spec = pltpu.PrefetchScalarGridSpec(
        num_scalar_prefetch=3,
        grid=(B, len(pairs)),
        in_specs=[pl.BlockSpec((1, TQ, MLA_HEADS * LANES), lambda b, t, ok, qi, kj: (b, qi[t], 0)),
                  pl.BlockSpec((1, TQ, MLA_HEADS * LANES), lambda b, t, ok, qi, kj: (b, kj[t], 0)),
                  pl.BlockSpec((1, TQ, MLA_HEADS * HEAD_DIM), lambda b, t, ok, qi, kj: (b, kj[t], 0)),
                  pl.BlockSpec((1, MLA_HEADS, LANES), lambda b, t, ok, qi, kj: (b, 0, 0))],
        out_specs=pl.BlockSpec((1, TQ, MLA_HEADS * HEAD_DIM), lambda b, t, ok, qi, kj: (b, qi[t], 0)),
        scratch_shapes=[pltpu.VMEM((MLA_HEADS, TQ, LANES), F32),
                        pltpu.VMEM((MLA_HEADS, TQ, LANES), F32),
                        pltpu.VMEM((MLA_HEADS // 2, TQ, LANES), F32)])
    return pl.pallas_call(
        _attn_kernel,
        grid_spec=grid_spec,
        out_shape=jax.ShapeDtypeStruct((B, T, MLA_HEADS * HEAD_DIM), BF16),
        compiler_params=_cparams(("arbitrary", "arbitrary")),
    )(ok, qi, kj, q, k, v, cb)


def _mixout_kernel(has_router, *refs):
    if has_router:
        (x_ref, yr_ref, yp_ref, ym_ref, wo_ref, mod_ref, gain_ref, rt_ref, tri_ref,
         xo_ref, h_ref, route_ref, cnt_ref, cnt_sc) = refs
    else:
        x_ref, yr_ref, yp_ref, ym_ref, wo_ref, mod_ref, gain_ref, xo_ref, h_ref = refs
    mod = mod_ref[0]
    o1 = RWKV_WIDTH
    o2 = RWKV_WIDTH + POOL_WIDTH
    mix = (_dot(yr_ref[0], wo_ref[0:o1, :]) + _dot(yp_ref[0], wo_ref[o1:o2, :])
           + _dot(ym_ref[0], wo_ref[o2:, :]))
    x = x_ref[0] + mod[2:3] * mix
    xo_ref[0] = x
    h = _rms(x) * gain_ref[...] * (1.0 + mod[4:5]) + mod[3:4]
    if not has_router:
        h_ref[0] = h.astype(BF16)
    else:
        hp = _pack_bf16_pairs(h)
        for ck in range(ROW_CHUNKS):
            h_ref[ck, 0] = hp[:, ck * PACK_CHUNK_W:(ck + 1) * PACK_CHUNK_W]
        logits = _dot(h.astype(BF16), rt_ref[...])
        lane = lax.broadcasted_iota(jnp.int32, logits.shape, 1).astype(F32)
        lg = jnp.where(lane < N_EXPERTS, logits, -jnp.inf)
        m1 = jnp.max(lg, axis=-1, keepdims=True)
        i1 = jnp.min(jnp.where(lg == m1, lane, float(LANES)), axis=-1, keepdims=True)
        lg2 = jnp.where(lane == i1, -jnp.inf, lg)
        m2 = jnp.max(lg2, axis=-1, keepdims=True)
        i2 = jnp.min(jnp.where(lg2 == m2, lane, float(LANES)), axis=-1, keepdims=True)
        e2 = jnp.exp(m2 - m1)
        g1 = 1.0 / (1.0 + e2)
        g2 = e2 / (1.0 + e2)
        first = (pl.program_id(0) == 0) & (pl.program_id(1) == 0)

        @pl.when(first)
        def _():
            cnt_sc[...] = jnp.zeros(cnt_sc.shape, F32)

        hit1 = lane == i1
        hit2 = lane == i2
        onehot = jnp.where(hit1 | hit2, 1.0, 0.0)
        prefix = _dot(tri_ref[...], onehot.astype(BF16)) + cnt_sc[0:1, :]
        r1 = jnp.sum(jnp.where(hit1, prefix, 0.0), axis=-1, keepdims=True)
        r2 = jnp.sum(jnp.where(hit2, prefix, 0.0), axis=-1, keepdims=True)
        cnt_sc[...] = cnt_sc[...] + jnp.sum(onehot, axis=0, keepdims=True)
        cnt_ref[...] = cnt_sc[...]
        vals = (i1, i2, g1, g2, r1, r2)
        route = jnp.zeros(logits.shape, F32)
        for pos, val in enumerate(vals):
            route = jnp.where(lane == pos, val, route)
        route_ref[0] = route


def _mixout(x, yr, yp, ym, wo, mod_l, gain, router_p):
    B, T, _ = x.shape
    tm = TM_OUT
    has_router = router_p is not None
    const = lambda shape: pl.BlockSpec(shape, lambda b, i: tuple(0 for _ in shape))
    tok = lambda w: pl.BlockSpec((1, tm, w), lambda b, i: (b, i, 0))
    in_specs = [tok(D_MODEL), tok(RWKV_WIDTH), tok(POOL_WIDTH), tok(MLA_HEADS * HEAD_DIM),
                const((D_MODEL, D_MODEL)), pl.BlockSpec((1, 6, D_MODEL), lambda b, i: (b, 0, 0)),
                const((1, D_MODEL))]
    args = [x, yr, yp, ym, wo, mod_l, gain]
    out_specs = [tok(D_MODEL), tok(D_MODEL)]
    out_shape = [jax.ShapeDtypeStruct((B, T, D_MODEL), F32), jax.ShapeDtypeStruct((B, T, D_MODEL), BF16)]
    scratch = []
    if has_router:
        out_specs[1] = pl.BlockSpec((ROW_CHUNKS, 1, tm, PACK_CHUNK_W), lambda b, i: (0, b, i, 0))
        out_shape[1] = jax.ShapeDtypeStruct((ROW_CHUNKS, B, T, PACK_CHUNK_W), jnp.uint32)
        ids = np.arange(tm)
        tri = jnp.asarray(ids[:, None] > ids[None, :], BF16)
        in_specs += [const((D_MODEL, LANES)), const((tm, tm))]
        args += [router_p, tri]
        out_specs += [tok(LANES), const((SUBLANES, LANES))]
        out_shape += [jax.ShapeDtypeStruct((B, T, LANES), F32),
                      jax.ShapeDtypeStruct((SUBLANES, LANES), F32)]
        scratch = [pltpu.VMEM((SUBLANES, LANES), F32)]
    return pl.pallas_call(
        functools.partial(_mixout_kernel, has_router),
        grid=(B, T // tm),
        in_specs=in_specs, out_specs=out_specs, out_shape=out_shape, scratch_shapes=scratch,
        compiler_params=_cparams(("arbitrary", "arbitrary")),
    )(*args)


def _ffn_kernel(h_ref, wg_ref, wu_ref, wo_ref, x_ref, mod_ref, o_ref, acc):
    j = pl.program_id(1)

    @pl.when(j == 0)
    def _():
        acc[...] = jnp.zeros(acc.shape, F32)

    h = h_ref[...]
    gg = _dot(h, wg_ref[...])
    uu = _dot(h, wu_ref[...])
    acc[...] += _dot((_silu(gg) * uu).astype(BF16), wo_ref[...])

    @pl.when(j == pl.num_programs(1) - 1)
    def _():
        o_ref[...] = x_ref[...] + mod_ref[0][5:6] * acc[...]


def _ffn(h2, w_in, w_out, x, mod_l):
    N = h2.shape[0]
    T = N // mod_l.shape[0]
    tm, tf = TM_FFN, TF_FFN
    nf = D_FF // tf
    per_b = T // tm
    return pl.pallas_call(
        _ffn_kernel,
        grid=(N // tm, nf),
        in_specs=[pl.BlockSpec((tm, D_MODEL), lambda i, j: (i, 0)),
                  pl.BlockSpec((D_MODEL, tf), lambda i, j: (0, j)),
                  pl.BlockSpec((D_MODEL, tf), lambda i, j: (0, j + nf)),
                  pl.BlockSpec((tf, D_MODEL), lambda i, j: (j, 0)),
                  pl.BlockSpec((tm, D_MODEL), lambda i, j: (i, 0)),
                  pl.BlockSpec((1, 6, D_MODEL), lambda i, j: (i // per_b, 0, 0))],
        out_specs=pl.BlockSpec((tm, D_MODEL), lambda i, j: (i, 0)),
        out_shape=jax.ShapeDtypeStruct((N, D_MODEL), F32),
        scratch_shapes=[pltpu.VMEM((tm, D_MODEL), F32)],
        compiler_params=_cparams(("arbitrary", "arbitrary")),
    )(h2, w_in, w_in, w_out, x, mod_l)


def _moe_kernel(be_ref, nv_ref, last_ref, x_ref, wg_ref, wu_ref, wo_ref, o_ref, acc, xm):
    i = pl.program_id(0)
    j = pl.program_id(1)

    @pl.when(i <= last_ref[0])
    def _():
        @pl.when(j == 0)
        def _():
            acc[...] = jnp.zeros(acc.shape, F32)
            row = lax.broadcasted_iota(jnp.int32, (xm.shape[0], 1), 0)
            xp = jnp.concatenate([x_ref[ck] for ck in range(ROW_CHUNKS)], axis=1)
            xp = jnp.where(row < nv_ref[i], xp, jnp.uint32(0))
            xm[...] = _unpack_bf16_pairs(xp)

        def swiglu_rows(n_rows):
            x = xm[0:n_rows, :]
            gg = _dot(x, wg_ref[0].astype(BF16))
            uu = _dot(x, wu_ref[0].astype(BF16))
            acc[0:n_rows, :] += _dot((_silu(gg) * uu).astype(BF16), wo_ref[0].astype(BF16))

        half = xm.shape[0] // 2

        @pl.when(nv_ref[i] > half)
        def _():
            swiglu_rows(xm.shape[0])

        @pl.when(nv_ref[i] <= half)
        def _():
            swiglu_rows(half)

        @pl.when(j == pl.num_programs(1) - 1)
        def _():
            yp = _pack_bf16_pairs(acc[...])
            for ck in range(ROW_CHUNKS):
                o_ref[ck] = yp[:, ck * PACK_CHUNK_W:(ck + 1) * PACK_CHUNK_W]


def _moe_experts(xs, w_in, w_out, block_exp, n_valid, last_blk):
    n_rows = xs.shape[1]
    tm, tf = MOE_BLOCK, TF_MOE
    nf = D_FF_EXPERT // tf
    blk = lambda i, last: jnp.minimum(i, last[0])
    chunk = lambda i, j, last: jnp.where(i <= last[0], j, nf - 1)
    grid_spec = pltpu.PrefetchScalarGridSpec(
        num_scalar_prefetch=3,
        grid=(n_rows // tm, nf),
        in_specs=[pl.BlockSpec((ROW_CHUNKS, tm, PACK_CHUNK_W),
                               lambda i, j, be, nv, last: (0, blk(i, last), 0)),
                  pl.BlockSpec((1, D_MODEL, tf),
                               lambda i, j, be, nv, last: (be[blk(i, last)], 0, chunk(i, j, last))),
                  pl.BlockSpec((1, D_MODEL, tf),
                               lambda i, j, be, nv, last: (be[blk(i, last)], 0, chunk(i, j, last) + nf)),
                  pl.BlockSpec((1, tf, D_MODEL),
                               lambda i, j, be, nv, last: (be[blk(i, last)], chunk(i, j, last), 0))],
        out_specs=pl.BlockSpec((ROW_CHUNKS, tm, PACK_CHUNK_W), lambda i, j, be, nv, last: (0, blk(i, last), 0)),
        scratch_shapes=[pltpu.VMEM((tm, D_MODEL), F32), pltpu.VMEM((tm, D_MODEL), BF16)])
    return pl.pallas_call(
        _moe_kernel,
        grid_spec=grid_spec,
        out_shape=jax.ShapeDtypeStruct((ROW_CHUNKS, n_rows, PACK_CHUNK_W), jnp.uint32),
        compiler_params=_cparams(("arbitrary", "arbitrary")),
    )(block_exp, n_valid, last_blk, xs, w_in, w_in, w_out)


def _sc_mesh():
    return plsc.VectorSubcoreMesh(core_axis_name="c", subcore_axis_name="s")


def _sc_scatter_rows(x, dest, n_rows):
    N, D = x.shape
    K = dest.shape[0]
    win = SC_WINDOW

    @pl.kernel(out_type=jax.ShapeDtypeStruct((n_rows, D), x.dtype), mesh=_sc_mesh(), scratch_types=[])
    def scatter(x_hbm, d_hbm, o_hbm):
        def body(x_vmem, *idx_vmem):
            for iv in idx_vmem:
                pltpu.sync_copy(x_vmem, o_hbm.at[iv.at[0]])

        pltpu.emit_pipeline(
            body,
            grid=(N // win,),
            in_specs=[pl.BlockSpec((win, D), lambda i: (i, 0))]
            + [pl.BlockSpec((1, win), functools.partial(lambda k, i: (k, i), k)) for k in range(K)],
            out_specs=[],
            core_axis_name=("c", "s"),
            dimension_semantics=(pltpu.PARALLEL,),
        )(x_hbm, *([d_hbm] * K))

    return scatter(x, dest)


def _sc_gather_rows(x, idx):
    n = idx.shape[0]
    D = x.shape[1]
    win = SC_WINDOW

    @pl.kernel(out_type=jax.ShapeDtypeStruct((n, D), x.dtype), mesh=_sc_mesh(), scratch_types=[])
    def gather(x_hbm, i_hbm, o_hbm):
        def body(i_vmem, o_vmem):
            pltpu.sync_copy(x_hbm.at[i_vmem.at[0]], o_vmem)

        pltpu.emit_pipeline(
            body,
            grid=(n // win,),
            in_specs=[pl.BlockSpec((1, win), lambda i: (0, i))],
            out_specs=[pl.BlockSpec((win, D), lambda i: (i, 0))],
            core_axis_name=("c", "s"),
            dimension_semantics=(pltpu.PARALLEL,),
        )(i_hbm, o_hbm)

    return gather(x, idx.reshape(1, n))


def _combine_kernel(x_ref, ya_ref, yb_ref, route_ref, mod_ref, o_ref):
    rt = route_ref[...]
    ya = _unpack_bf16_pairs(jnp.concatenate([ya_ref[0, ck] for ck in range(ROW_CHUNKS)], axis=1))
    yb = _unpack_bf16_pairs(jnp.concatenate([yb_ref[0, ck] for ck in range(ROW_CHUNKS)], axis=1))
    f = rt[:, 2:3] * ya.astype(F32) + rt[:, 3:4] * yb.astype(F32)
    o_ref[...] = x_ref[...] + mod_ref[0][5:6] * f


def _combine(x, y2, route, mod_l):
    N = x.shape[0]
    T = N // mod_l.shape[0]
    tm = 1024
    per_b = T // tm
    tok = pl.BlockSpec((tm, D_MODEL), lambda i: (i, 0))
    slot = lambda k: pl.BlockSpec((1, ROW_CHUNKS, tm, PACK_CHUNK_W), lambda i: (k, 0, i, 0))
    return pl.pallas_call(
        _combine_kernel,
        grid=(N // tm,),
        in_specs=[tok, slot(0), slot(1),
                  pl.BlockSpec((tm, LANES), lambda i: (i, 0)),
                  pl.BlockSpec((1, 6, D_MODEL), lambda i: (i // per_b, 0, 0))],
        out_specs=tok,
        out_shape=jax.ShapeDtypeStruct((N, D_MODEL), F32),
        compiler_params=_cparams(("arbitrary",)),
    )(x, y2, y2, route, mod_l)


def _moe(h2, route, counts, w_in, w_out, x, mod_l):
    N = x.shape[0]
    blk = MOE_BLOCK
    cnt = counts[0, :N_EXPERTS].astype(jnp.int32)
    padded = (cnt + blk - 1) // blk * blk
    pend = jnp.cumsum(padded)
    pstart = pend - padded
    e = route[:, 0:TOP_K].astype(jnp.int32)
    rank = route[:, 2 * TOP_K:3 * TOP_K].astype(jnp.int32)
    dest = (jnp.take(pstart, e) + rank).T
    n_blocks = N * TOP_K // blk + N_EXPERTS
    bstart = jnp.arange(n_blocks, dtype=jnp.int32) * blk
    block_exp = jnp.minimum(jnp.sum((bstart[:, None] >= pend[None, :]).astype(jnp.int32), axis=1),
                            N_EXPERTS - 1)
    n_valid = jnp.clip(cnt[block_exp] - (bstart - pstart[block_exp]), 0, blk).astype(jnp.int32)
    n_rows = n_blocks * blk
    dest_ck = dest[:, None, :] + (jnp.arange(ROW_CHUNKS, dtype=jnp.int32) * n_rows)[None, :, None]
    xs = _sc_scatter_rows(h2, dest_ck.reshape(TOP_K, ROW_CHUNKS * N), ROW_CHUNKS * n_rows)
    last_blk = (pend[-1:] // blk - 1).astype(jnp.int32)
    yb = _moe_experts(xs.reshape(ROW_CHUNKS, n_rows, PACK_CHUNK_W), w_in, w_out, block_exp, n_valid,
                      last_blk)
    y2 = _sc_gather_rows(yb.reshape(ROW_CHUNKS * n_rows, PACK_CHUNK_W), dest_ck.reshape(-1))
    return _combine(x, y2.reshape(TOP_K, ROW_CHUNKS, N, PACK_CHUNK_W), route, mod_l)


def _layout_w_in(w, has_vres):
    W = RWKV_WIDTH
    off_gd = 3 * W + RWKV_DECAY_LORA + RWKV_ICLR_LORA
    off_pool = off_gd + RWKV_GATE_LORA
    off_q = off_pool + POOL_WIDTH
    off_kv = off_q + MLA_Q_LORA
    off_kr = off_kv + MLA_KV_LORA
    n_base = off_kr + MLA_QK_ROPE
    d = w.shape[0]
    zeros = lambda n: jnp.zeros((d, n), w.dtype)
    vd = w[:, n_base:n_base + RWKV_VRES_LORA] if has_vres else zeros(RWKV_VRES_LORA)
    cols = [w[:, :off_gd], w[:, off_gd:off_pool], vd, zeros(ZR_COLS - off_pool - RWKV_VRES_LORA),
            w[:, off_pool:off_q], w[:, off_q:off_kv], w[:, off_kv:off_kr],
            zeros(MLA_QK_NOPE), w[:, off_kr:n_base], zeros(LANES - MLA_QK_DIM)]
    return jnp.concatenate(cols, axis=1).astype(BF16)


def _pad_heads(w, per_head, keep_from, keep_n):
    K = w.shape[0]
    wh = w.reshape(K, MLA_HEADS, per_head)[:, :, keep_from:keep_from + keep_n]
    wh = jnp.pad(wh, ((0, 0), (0, 0), (0, LANES - keep_n)))
    return wh.reshape(K, MLA_HEADS * LANES)


def kernel(x, c, positions, w_ada, b_ada, norm_gain, w_in_first, w_in_rest, mu_shift, mu_shift_v,
           rwkv_vec, rwkv_v0, rwkv_w2, rwkv_a2, rwkv_g2, rwkv_v2, pool_w, pool_scale,
           mla_q_lat_gain, mla_kv_lat_gain, mla_wq_up, mla_wkv_up, mla_qk_gain, w_out, ffn_w_in,
           ffn_w_out, moe_router, moe_w_in, moe_w_out):
    B, T, D = x.shape
    depth = w_ada.shape[0]
    assert D == D_MODEL and B % WKV_STEP_SEQS == 0 and D_FF_EXPERT % TF_MOE == 0
    assert T % max(TM_MIX, TM_OUT, TQ, WKV_CHUNK * WKV_STEP_CHUNKS) == 0
    assert (B * T) % max(TM_FFN, MOE_BLOCK) == 0 and (B * T * TOP_K) % MOE_BLOCK == 0
    W = RWKV_WIDTH
    mod = _adaln(c, w_ada, b_ada).reshape(depth, B, 6, D)
    inv_freq = ROPE_BASE ** (-jnp.arange(0, MLA_QK_ROPE, 2, dtype=F32) / MLA_QK_ROPE)
    cosf, sinf = _rope_tables(positions, inv_freq)
    hid = np.arange(LANES) // HEAD_DIM
    bd64 = jnp.asarray(hid[:, None] == hid[None, :], BF16)
    bd128 = jnp.ones((LANES, LANES), BF16)

    v_first = None
    for l in range(depth):
        has_vres = l > 0
        mod_l = mod[l]
        win = _layout_w_in(w_in_first if l == 0 else w_in_rest[l - 1], has_vres)
        poolw = jax.scipy.linalg.block_diag(*[pool_w[l, g] for g in range(len(POOL_WINDOWS))]).astype(BF16)
        wq = _pad_heads(mla_wq_up[l], MLA_QK_DIM, 0, MLA_QK_DIM).astype(BF16)
        wk = _pad_heads(mla_wkv_up[l], MLA_QK_NOPE + HEAD_DIM, 0, MLA_QK_NOPE).astype(BF16)
        wv = mla_wkv_up[l].reshape(MLA_KV_LORA, MLA_HEADS, MLA_QK_NOPE + HEAD_DIM)[:, :, MLA_QK_NOPE:]
        wv = wv.reshape(MLA_KV_LORA, MLA_HEADS * HEAD_DIM).astype(BF16)
        qkg = jnp.tile(jnp.pad(mla_qk_gain[l], ((0, 0), (0, LANES - MLA_QK_DIM))), (1, MLA_HEADS))
        zr, y_pool, q, k, v = _mixin(
            x, mod_l, norm_gain[l, 0].reshape(1, D), win, cosf, sinf, poolw,
            pool_scale[l].reshape(1, -1), mla_q_lat_gain[l].reshape(1, -1),
            mla_kv_lat_gain[l].reshape(1, -1), wq, wk, wv, qkg, bd128)

        pad_mu = ZR_COLS - mu_shift.shape[1] - RWKV_VRES_LORA
        mu_v = mu_shift_v[l - 1] if has_vres else jnp.zeros((RWKV_VRES_LORA,), F32)
        mu = jnp.concatenate([mu_shift[l], mu_v, jnp.zeros((pad_mu,), F32)]).reshape(1, ZR_COLS)
        v0 = rwkv_v0[l - 1] if has_vres else jnp.zeros((W,), F32)
        vec8 = jnp.concatenate([rwkv_vec[l], v0[None]], axis=0)
        w2a2 = jax.scipy.linalg.block_diag(rwkv_w2[l], rwkv_a2[l]).astype(BF16)
        g2 = jnp.pad(rwkv_g2[l], ((0, 2 * LANES - RWKV_GATE_LORA), (0, 0)))
        if has_vres:
            v2 = jnp.pad(rwkv_v2[l - 1], ((RWKV_GATE_LORA, 2 * LANES - RWKV_GATE_LORA - RWKV_VRES_LORA), (0, 0)))
        else:
            v2 = jnp.zeros((2 * LANES, W), F32)
        g2v2 = jnp.concatenate([g2, v2], axis=1).astype(BF16)
        y_rwkv, v_first = _rwkv(zr, v_first, mu, vec8, w2a2, g2v2, bd64)

        y_mla = _attention(q, k, v, mla_qk_gain[l])

        is_moe = (l % 2 == 1)
        router_p = None
        if is_moe:
            router_p = jnp.pad(moe_router[l // 2], ((0, 0), (0, LANES - N_EXPERTS))).astype(BF16)
        outs = _mixout(x, y_rwkv, y_pool, y_mla, w_out[l].astype(BF16), mod_l,
                       norm_gain[l, 1].reshape(1, D), router_p)
        x_mid, h2 = outs[0], outs[1]
        xf = x_mid.reshape(B * T, D)
        if is_moe:
            xo = _moe(h2.reshape(ROW_CHUNKS * B * T, PACK_CHUNK_W), outs[2].reshape(B * T, LANES), outs[3],
                      moe_w_in[l // 2], moe_w_out[l // 2], xf, mod_l)
        else:
            xo = _ffn(h2.reshape(B * T, D), ffn_w_in[l // 2].astype(BF16), ffn_w_out[l // 2].astype(BF16),
                      xf, mod_l)
        x = xo.reshape(B, T, D)
    return x
```

```python
import functools

import numpy as np
import jax
import jax.numpy as jnp
from jax import lax
from jax.experimental import pallas as pl
from jax.experimental.pallas import tpu as pltpu
from jax.experimental.pallas import tpu_sc as plsc

F32 = jnp.float32
BF16 = jnp.bfloat16

D_MODEL = 1024
HEAD_DIM = 64
RWKV_WIDTH = 512
RWKV_HEADS = RWKV_WIDTH // HEAD_DIM
POOL_WIDTH = 256
POOL_WINDOWS = (2, 4, 8, 16)
POOL_HALO = 16
MLA_HEADS = 4
MLA_QK_NOPE = 64
MLA_QK_ROPE = 32
MLA_QK_DIM = MLA_QK_NOPE + MLA_QK_ROPE
MLA_Q_LORA = 256
MLA_KV_LORA = 128
ROPE_BASE = 10000.0
RWKV_DECAY_LORA = 64
RWKV_ICLR_LORA = 64
RWKV_VRES_LORA = 32
RWKV_GATE_LORA = 160
RWKV_LNX_EPS = 64e-5
D_FF = 2816
N_EXPERTS = 8
TOP_K = 2
D_FF_EXPERT = 3584
NORM_EPS = 1e-6
NEG_INF = -1e30

LANES = 128
SUBLANES = 8
VMEM_LIMIT = 56 * 1024 * 1024

ZR_COLS = 3 * RWKV_WIDTH + (RWKV_DECAY_LORA + RWKV_ICLR_LORA) + 2 * LANES
Z_POOL_OFF = ZR_COLS
Z_QLAT_OFF = Z_POOL_OFF + POOL_WIDTH
Z_KVLAT_OFF = Z_QLAT_OFF + MLA_Q_LORA
Z_KROPE_OFF = Z_KVLAT_OFF + MLA_KV_LORA
Z_COLS = Z_KROPE_OFF + LANES

MIXIN_PIECE_W = 512
TM_MIX = 1024
TM_OUT = 1024
WKV_CHUNK = 64
TQ = 512
ATTN_BOUND_SLACK = 1.02
ATTN_BOUND_MAX = 40.0
TM_FFN = 1024
TF_FFN = D_FF // 2
MOE_BLOCK = 1024
MXU_TILE = 256
TF_MOE = 2 * MXU_TILE
SC_WINDOW = 128
ROW_CHUNKS = 4
ROW_CHUNK_W = D_MODEL // ROW_CHUNKS
PACK_CHUNK_W = ROW_CHUNK_W // 2

SEGSUM_SPLITS = 1

NN = (((1,), (0,)), ((), ()))
NT = (((1,), (1,)), ((), ()))


def _dot(a, b, dims=NN):
    return lax.dot_general(a, b, dims, preferred_element_type=F32)


def _split2(a):
    hi = a.astype(BF16)
    lo = (a - hi.astype(F32)).astype(BF16)
    return hi, lo


def _mm(a, b, dims=NN, passes=3):
    if passes == 1:
        return _dot(a.astype(BF16), b.astype(BF16), dims)
    ah, al = _split2(a)
    bh, bl = _split2(b)
    return _dot(ah, bh, dims) + (_dot(ah, bl, dims) + _dot(al, bh, dims))


def _mm_exact_rhs(a, b_bf16, splits=SEGSUM_SPLITS):
    m, w = a.shape
    nb = w // LANES
    stacked = jnp.concatenate([a[:, i * LANES:(i + 1) * LANES] for i in range(nb)], axis=0)
    out = None
    rem = stacked
    for s in range(splits):
        part = rem.astype(BF16)
        term = _dot(part, b_bf16)
        out = term if out is None else out + term
        if s + 1 < splits:
            rem = rem - part.astype(F32)
    return jnp.concatenate([out[i * m:(i + 1) * m] for i in range(nb)], axis=1)


def _pack_bf16_pairs(h):
    w = h.shape[1] // 2
    lo = lax.bitcast_convert_type(h[:, :w].astype(BF16).astype(F32), jnp.uint32)
    hi = lax.bitcast_convert_type(h[:, w:].astype(BF16).astype(F32), jnp.uint32)
    return (lo >> 16) | (hi & jnp.uint32(0xFFFF0000))


def _unpack_bf16_pairs(p):
    lo = lax.bitcast_convert_type(p << 16, F32)
    hi = lax.bitcast_convert_type(p & jnp.uint32(0xFFFF0000), F32)
    return jnp.concatenate([lo, hi], axis=1).astype(BF16)


def _sigmoid(x):
    return 1.0 / (1.0 + jnp.exp(-x))


def _silu(x):
    return x * _sigmoid(x)


def _rms(x, eps=NORM_EPS):
    return x * lax.rsqrt(jnp.mean(x * x, axis=-1, keepdims=True) + eps)


def _cparams(sem):
    return pltpu.CompilerParams(dimension_semantics=sem, vmem_limit_bytes=VMEM_LIMIT)


def _adaln_kernel(c_ref, w_ref, b_ref, o_ref):
    ca = _silu(c_ref[...])
    o_ref[0] = _mm(ca, w_ref[0]) + b_ref[0]


def _adaln(c, w_ada, b_ada):
    L = w_ada.shape[0]
    B = c.shape[0]
    n = w_ada.shape[2] // D_MODEL
    return pl.pallas_call(
        _adaln_kernel,
        grid=(L, n),
        in_specs=[pl.BlockSpec((B, D_MODEL), lambda l, j: (0, 0)),
                  pl.BlockSpec((1, D_MODEL, D_MODEL), lambda l, j: (l, 0, j)),
                  pl.BlockSpec((1, 1, D_MODEL), lambda l, j: (l, 0, j))],
        out_specs=pl.BlockSpec((1, B, D_MODEL), lambda l, j: (l, 0, j)),
        out_shape=jax.ShapeDtypeStruct((L, B, n * D_MODEL), F32),
        compiler_params=_cparams(("arbitrary", "arbitrary")),
    )(c, w_ada, b_ada.reshape(L, 1, -1))


def _rope(x, cosf, sinf, lane):
    up = pltpu.roll(x, LANES - MLA_QK_ROPE // 2, axis=1)
    dn = pltpu.roll(x, MLA_QK_ROPE // 2, axis=1)
    rot = jnp.where(lane < MLA_QK_NOPE + MLA_QK_ROPE // 2, -up, dn)
    return x * cosf + rot * sinf


def _rope_kernel(pos_ref, freq_ref, cos_ref, sin_ref):
    ang = pos_ref[0].astype(F32) * freq_ref[...]
    cos_ref[0] = jnp.cos(ang)
    sin_ref[0] = jnp.sin(ang)


def _rope_tables(positions, inv_freq):
    B, T = positions.shape
    nf = inv_freq.shape[0]
    per_row = LANES // nf
    pos_p = jnp.broadcast_to(positions[:, :, None], (B, T, nf)).reshape(B, T // per_row, LANES)
    freq_p = jnp.tile(inv_freq, per_row).reshape(1, LANES)
    blk = pl.BlockSpec((1, T // per_row, LANES), lambda b: (b, 0, 0))
    cos_p, sin_p = pl.pallas_call(
        _rope_kernel,
        grid=(B,),
        in_specs=[blk, pl.BlockSpec((1, LANES), lambda b: (0, 0))],
        out_specs=[blk, blk],
        out_shape=[jax.ShapeDtypeStruct((B, T // per_row, LANES), F32)] * 2,
        compiler_params=_cparams(("arbitrary",)),
    )(pos_p, freq_p)

    def spread(tab, fill):
        t16 = tab.reshape(B, T, nf)
        return jnp.concatenate([jnp.full((B, T, MLA_QK_NOPE), fill, F32), t16, t16,
                                jnp.full((B, T, LANES - MLA_QK_DIM), fill, F32)], axis=-1)

    return spread(cos_p, 1.0), spread(sin_p, 0.0)


def _mixin_kernel(x_ref, mod_ref, gain_ref, win_ref, cos_ref, sin_ref, poolw_ref, pools_ref,
                  qg_ref, kvg_ref, wq_ref, wk_ref, wv_ref, qkg_ref, bd_ref,
                  zr_ref, yp_ref, q_ref, k_ref, v_ref, ubuf):
    i = pl.program_id(1)
    tm = x_ref.shape[1]
    x = x_ref[0]
    mod = mod_ref[0]
    @pl.when(i == 0)
    def _():
        ubuf[0:POOL_HALO, :] = jnp.zeros((POOL_HALO, POOL_WIDTH), F32)

    h = (_rms(x) * gain_ref[...] * (1.0 + mod[1:2]) + mod[0:1]).astype(BF16)
    zb = _dot(h, win_ref[:, ZR_COLS:])
    zcol = lambda off, w: zb[:, off - ZR_COLS:off - ZR_COLS + w]
    def project_piece(n):
        cols = slice(n * MIXIN_PIECE_W, min((n + 1) * MIXIN_PIECE_W, ZR_COLS))
        zr_ref[0, :, cols] = _dot(h, win_ref[:, cols])

    project_piece(0)
    u = zcol(Z_POOL_OFF, POOL_WIDTH)
    ubuf[POOL_HALO:, :] = u
    ue = ubuf[...]
    s2 = ue + pltpu.roll(ue, 1, axis=0)
    s4 = s2 + pltpu.roll(s2, 2, axis=0)
    s8 = s4 + pltpu.roll(s4, 4, axis=0)
    s16 = s8 + pltpu.roll(s8, 8, axis=0)
    ubuf[0:POOL_HALO, :] = u[tm - POOL_HALO:, :]
    lane_p = lax.broadcasted_iota(jnp.int32, (tm, POOL_WIDTH), 1)
    grp = lane_p // (POOL_WIDTH // len(POOL_WINDOWS))
    win_sum = jnp.where(grp == 0, s2[POOL_HALO:], jnp.where(grp == 1, s4[POOL_HALO:],
                        jnp.where(grp == 2, s8[POOL_HALO:], s16[POOL_HALO:])))
    win = jnp.where(grp == 0, 2, jnp.where(grp == 1, 4, jnp.where(grp == 2, 8, 16)))
    t_abs = i * tm + lax.broadcasted_iota(jnp.int32, (tm, POOL_WIDTH), 0)
    cnt = jnp.minimum(t_abs + 1, win).astype(F32)
    p = win_sum / cnt - u
    yp = _dot(p.astype(BF16), poolw_ref[...]) * pools_ref[...]
    yp_ref[0] = yp.astype(BF16)
    project_piece(1)

    lane = lax.broadcasted_iota(jnp.int32, (tm, LANES), 1)
    cosf = cos_ref[0]
    sinf = sin_ref[0]

    q_lat = zcol(Z_QLAT_OFF, MLA_Q_LORA)
    kv_lat = zcol(Z_KVLAT_OFF, MLA_KV_LORA)
    k_rope = zcol(Z_KROPE_OFF, LANES)
    qn = (_rms(q_lat) * qg_ref[...]).astype(BF16)
    kvn = (_rms(kv_lat) * kvg_ref[...]).astype(BF16)
    q = _dot(qn, wq_ref[...])
    kx = _dot(kvn, wk_ref[...])
    v = _dot(kvn, wv_ref[...])
    v_ref[0] = v.astype(BF16)
    project_piece(2)
    k_pe = _rope(k_rope, cosf, sinf, lane)
    qs, ks = [], []
    for hd in range(MLA_HEADS):
        sl = slice(hd * LANES, (hd + 1) * LANES)
        qs.append(_rope(q[:, sl], cosf, sinf, lane))
        ks.append(kx[:, sl] + k_pe)
    q = jnp.concatenate(qs, axis=1)
    k = jnp.concatenate(ks, axis=1)
    project_piece(3)
    qss = _mm_exact_rhs(q * q, bd_ref[...]) * (1.0 / MLA_QK_DIM)
    kss = _mm_exact_rhs(k * k, bd_ref[...]) * (1.0 / MLA_QK_DIM)
    qkg = qkg_ref[...]
    q = q * lax.rsqrt(qss + NORM_EPS) * qkg[0:1] * (MLA_QK_DIM ** -0.5)
    k = k * lax.rsqrt(kss + NORM_EPS) * qkg[1:2]
    q_ref[0] = q.astype(BF16)
    k_ref[0] = k.astype(BF16)


def _mixin(x, mod_l, gain, win, cosf, sinf, poolw, pools, qg, kvg, wq, wk, wv, qkg, bd128):
    B, T, _ = x.shape
    tm = TM_MIX
    const = lambda shape: pl.BlockSpec(shape, lambda b, i: tuple(0 for _ in shape))
    tok = lambda w: pl.BlockSpec((1, tm, w), lambda b, i: (b, i, 0))
    return pl.pallas_call(
        _mixin_kernel,
        grid=(B, T // tm),
        in_specs=[tok(D_MODEL),
                  pl.BlockSpec((1, 6, D_MODEL), lambda b, i: (b, 0, 0)),
                  const((1, D_MODEL)), const((D_MODEL, Z_COLS)),
                  tok(LANES), tok(LANES),
                  const((POOL_WIDTH, POOL_WIDTH)), const((1, POOL_WIDTH)),
                  const((1, MLA_Q_LORA)), const((1, MLA_KV_LORA)),
                  const((MLA_Q_LORA, MLA_HEADS * LANES)), const((MLA_KV_LORA, MLA_HEADS * LANES)),
                  const((MLA_KV_LORA, MLA_HEADS * HEAD_DIM)), const((2, MLA_HEADS * LANES)),
                  const((LANES, LANES))],
        out_specs=[tok(ZR_COLS), tok(POOL_WIDTH), tok(MLA_HEADS * LANES), tok(MLA_HEADS * LANES),
                   tok(MLA_HEADS * HEAD_DIM)],
        out_shape=[jax.ShapeDtypeStruct((B, T, ZR_COLS), F32),
                   jax.ShapeDtypeStruct((B, T, POOL_WIDTH), BF16),
                   jax.ShapeDtypeStruct((B, T, MLA_HEADS * LANES), BF16),
                   jax.ShapeDtypeStruct((B, T, MLA_HEADS * LANES), BF16),
                   jax.ShapeDtypeStruct((B, T, MLA_HEADS * HEAD_DIM), BF16)],
        scratch_shapes=[pltpu.VMEM((POOL_HALO + tm, POOL_WIDTH), F32)],
        compiler_params=_cparams(("arbitrary", "arbitrary")),
    )(x, mod_l, gain, win, cosf, sinf, poolw, pools, qg, kvg, wq, wk, wv, qkg, bd128)


WKV_PASSES_SCORE = 1
WKV_PASSES_INV = 1
WKV_PASSES_APPLY = 1
WKV_PASSES_STATE = 1
WKV_STEP_CHUNKS = 1
WKV_STEP_SEQS = 8
WKV_GROUPS = 2
WKV_GROUP_LEAD = 4


def _stack_heads(xp, lane):
    return jnp.concatenate([jnp.where(lane < HEAD_DIM, xp, 0.0),
                            jnp.where(lane >= HEAD_DIM, xp, 0.0)], axis=0)


def _wkv_prep(r, lw, k, v, kk, a, tri, masks):
    L = r[0].shape[0]
    nc = len(r)
    each = lambda f, *ls: [f(*xs) for xs in zip(*ls)]
    lane = lax.broadcasted_iota(jnp.int32, (L, LANES), 1)
    stack = lambda x: _stack_heads(x, lane)
    cum = each(lambda x: _mm_exact_rhs_left(tri, x), lw)
    cum_last = each(lambda c: c[L - 1:L, :], cum)
    e_w = each(jnp.exp, cum)
    e_wm = each(lambda c, x: jnp.exp(c - x), cum, lw)
    e_iw = each(lambda c: jnp.exp(-c), cum)
    e_d = each(lambda cl, c: jnp.exp(cl - c), cum_last, cum)
    beta = each(lambda x, y: x * y, kk, a)
    r_f = each(lambda x, e: x * e, r, e_w)
    a_f = each(lambda x, e: -x * e, kk, e_wm)
    a_s = each(stack, a_f)
    b_s = each(lambda x, e: stack(x * e), beta, e_iw)
    k_s = each(lambda x, e: stack(x * e), k, e_iw)
    b_d = each(lambda x, e: stack(x * e), beta, e_d)
    k_d = each(lambda x, e: stack(x * e), k, e_d)
    v_s = each(stack, v)
    yield
    g = each(lambda af, rf, bs, ks: _mm(jnp.concatenate([af, rf], axis=0),
                                        jnp.concatenate([bs, ks], axis=0), NT, WKV_PASSES_SCORE),
             a_f, r_f, b_s, k_s)
    strict, incl, levels = masks
    a_ab = each(lambda x: jnp.where(strict, x[:L, :LANES], 0.0), g)
    a_ak = each(lambda x: jnp.where(strict, x[:L, LANES:], 0.0), g)
    s_rb = each(lambda x: jnp.where(incl, x[L:, :LANES], 0.0), g)
    s_rk = each(lambda x: jnp.where(incl, x[L:, LANES:], 0.0), g)
    eye = jnp.where(levels[0][1], 1.0, 0.0)
    tinv = each(lambda x: eye + jnp.where(levels[0][0], x, 0.0), a_ab)
    yield
    for lvl_mask, _ in levels[1:]:
        et = each(lambda x, t: _mm(jnp.where(lvl_mask, x, 0.0), stack(t), NN, WKV_PASSES_INV), a_ab, tinv)
        tinv = each(lambda t, x: t + _mm(t, stack(x), NN, WKV_PASSES_INV), tinv, et)
        yield
    av = each(lambda x, y: _mm(x, y, NN, WKV_PASSES_APPLY), a_ak, v_s)
    tx = each(lambda t, x, y: _mm(t, jnp.concatenate([x, stack(y)], axis=1), NN, WKV_PASSES_APPLY),
              tinv, a_s, av)
    ta_s = each(lambda x: stack(x[:, :LANES]), tx)
    c1_s = each(lambda x: stack(x[:, LANES:]), tx)
    yield
    ra = each(lambda rf, s, x: rf + _mm(s, x, NN, WKV_PASSES_APPLY), r_f, s_rb, ta_s)
    c2 = each(lambda sb, sk, x, vs: _mm(jnp.concatenate([sb, sk], axis=1),
                                        jnp.concatenate([x, vs], axis=0),
                                        NN, WKV_PASSES_APPLY), s_rb, s_rk, c1_s, v_s)
    yield
    tb = each(lambda x, y, bd: _mm(jnp.concatenate([x, y], axis=1).T, bd, NN, WKV_PASSES_APPLY),
              ta_s, c1_s, b_d)
    c3 = each(lambda x, vs, kd: x[LANES:] + _mm(vs.T, kd, NN, WKV_PASSES_APPLY), tb, v_s, k_d)
    return [(ra[i], c2[i], jnp.exp(cum_last[i]), tb[i][:LANES], c3[i]) for i in range(nc)]


def _mm_exact_rhs_left(tri_bf16, x):
    x0 = x.astype(BF16)
    r1 = x - x0.astype(F32)
    x1 = r1.astype(BF16)
    x2 = (r1 - x1.astype(F32)).astype(BF16)
    return _dot(tri_bf16, x0) + (_dot(tri_bf16, x1) + _dot(tri_bf16, x2))


def _wkv_masks(L):
    row = lax.broadcasted_iota(jnp.int32, (L, 2 * L), 0)
    col = lax.broadcasted_iota(jnp.int32, (L, 2 * L), 1) % L
    strict = row > col
    incl = row >= col
    levels = []
    m = 1
    while m < L:
        same = (row // (2 * m)) == (col // (2 * m))
        lvl = same & ((row % (2 * m)) >= m) & ((col % (2 * m)) < m)
        levels.append((lvl, row == col))
        m *= 2
    return strict, incl, levels


def _rwkv_kernel(has_vres, *refs):
    if has_vres:
        (z_ref, vf_ref, mu_ref, vec_ref, w2a2_ref, g2v2_ref, bd_ref,
         y_ref, carry, state) = refs
    else:
        (z_ref, mu_ref, vec_ref, w2a2_ref, g2v2_ref, bd_ref,
         y_ref, vout_ref, carry, state) = refs
    c = pl.program_id(1)
    n_seq, seq_rows = z_ref.shape[0], z_ref.shape[1]
    rows = n_seq * seq_rows
    L = WKV_CHUNK
    W = RWKV_WIDTH

    @pl.when(c == 0)
    def _():
        carry[...] = jnp.zeros(carry.shape, F32)
        state[...] = jnp.zeros(state.shape, F32)

    flat = lambda ref: jnp.concatenate([ref[s] for s in range(n_seq)], axis=0)
    z = flat(z_ref)
    row = lax.broadcasted_iota(jnp.int32, z.shape, 0)
    prev = pltpu.roll(z, 1, axis=0)
    for s in range(n_seq):
        prev = jnp.where(row == s * seq_rows, carry[s, SUBLANES - 1:SUBLANES, :], prev)
        carry[s] = z[(s + 1) * seq_rows - SUBLANES:(s + 1) * seq_rows, :]
    zs_all = z + mu_ref[...] * (prev - z)
    if has_vres:
        vf_all = flat(vf_ref)
    vec = vec_ref[...]
    w0, a0, k_k, k_a, r_k, ln_g, ln_b, v0 = (vec[j:j + 1] for j in range(8))
    bd = bd_ref[...]
    masks = _wkv_masks(L)
    rowt = lax.broadcasted_iota(jnp.int32, (L, L), 0)
    colt = lax.broadcasted_iota(jnp.int32, (L, L), 1)
    tri = jnp.where(rowt >= colt, 1.0, 0.0).astype(BF16)
    n_pairs = RWKV_HEADS // 2
    S_now = [{(s, p): state[s * n_pairs + p] for s in range(n_seq) for p in range(n_pairs)}]

    def group(s0, s1):
        seqs = range(s0, s1)
        r0, r1 = s0 * seq_rows, s1 * seq_rows
        zs = zs_all[r0:r1]
        r = zs[:, 0:W]
        k = zs[:, W:2 * W]
        v = zs[:, 2 * W:3 * W]
        wa = zs[:, 3 * W:3 * W + LANES]
        gb = zs[:, 3 * W + LANES:ZR_COLS]
        lane_a = lax.broadcasted_iota(jnp.int32, wa.shape, 1)
        t1 = _dot(jnp.where(lane_a < RWKV_DECAY_LORA, jnp.tanh(wa), wa).astype(BF16), w2a2_ref[...])
        lane_g = lax.broadcasted_iota(jnp.int32, gb.shape, 1)
        t2 = _dot(jnp.where(lane_g < RWKV_GATE_LORA, _sigmoid(gb), gb).astype(BF16), g2v2_ref[...])
        yield
        xw = w0 + t1[:, :W]
        w_log = -(jnp.maximum(-xw, 0.0) + jnp.log(1.0 + jnp.exp(-jnp.abs(xw)))) - 0.5
        lw = -jnp.exp(w_log)
        a = _sigmoid(a0 + t1[:, W:])
        g = t2[:, :W]
        if has_vres:
            v = v + (vf_all[r0:r1] - v) * _sigmoid(v0 + t2[:, W:])
        else:
            for s in seqs:
                vout_ref[s] = v[(s - s0) * seq_rows:(s - s0 + 1) * seq_rows]
        kk = k * k_k
        kk = kk * jnp.minimum(lax.rsqrt(_mm_exact_rhs(kk * kk, bd)), 1e12)
        k = k * (1.0 + (a - 1.0) * k_a)
        yield
        n_chunks = seq_rows // L
        idx = [(s, ch, p) for s in seqs for ch in range(n_chunks) for p in range(n_pairs)]
        row0 = lambda s, ch: (s - s0) * seq_rows + ch * L
        cut = lambda x: [x[row0(s, ch):row0(s, ch) + L, p * LANES:(p + 1) * LANES] for s, ch, p in idx]
        res = yield from _wkv_prep(cut(r), cut(lw), cut(k), cut(v), cut(kk), cut(a), tri, masks)
        prep = dict(zip(idx, res))
        yield
        sp = [(s, p) for s in seqs for p in range(n_pairs)]
        S = {key: S_now[0][key] for key in sp}
        y_blk = {}
        for ch in range(n_chunks):
            y_s = {(s, p): _mm(prep[s, ch, p][0], S[s, p], NT, WKV_PASSES_STATE) + prep[s, ch, p][1]
                   for s, p in sp}
            for s in seqs:
                y_blk[s, ch] = jnp.concatenate([y_s[s, p] for p in range(n_pairs)], axis=1)
            S = {(s, p): S[s, p] * prep[s, ch, p][2] + _mm(S[s, p], prep[s, ch, p][3], NN, WKV_PASSES_STATE)
                 + prep[s, ch, p][4] for s, p in sp}
        S_now[0].update(S)
        y = jnp.concatenate([y_blk[s, ch] for s in seqs for ch in range(n_chunks)], axis=0)
        yield
        inv = 1.0 / HEAD_DIM
        mean = _mm_exact_rhs(y, bd) * inv
        yc = y - mean
        var = _mm_exact_rhs(yc * yc, bd) * inv
        yn = yc * lax.rsqrt(var + RWKV_LNX_EPS) * ln_g + ln_b
        bonus = _mm_exact_rhs(r * k * r_k, bd) * v
        out = ((yn + bonus) * g).astype(BF16)
        for s in seqs:
            y_ref[s] = out[(s - s0) * seq_rows:(s - s0 + 1) * seq_rows]

    per = n_seq // WKV_GROUPS
    gens = [group(gi * per, (gi + 1) * per) for gi in range(WKV_GROUPS)]
    alive = [True] * WKV_GROUPS
    tick = 0
    while any(alive):
        for gi in range(WKV_GROUPS):
            if alive[gi] and tick >= gi * WKV_GROUP_LEAD:
                try:
                    next(gens[gi])
                except StopIteration:
                    alive[gi] = False
        tick += 1
    for s in range(n_seq):
        for p in range(n_pairs):
            state[s * n_pairs + p] = S_now[0][s, p]


def _rwkv(zr, v_first, mu, vec8, w2a2, g2v2, bd64):
    B, T, _ = zr.shape
    L = WKV_CHUNK * WKV_STEP_CHUNKS
    ns = WKV_STEP_SEQS
    has_vres = v_first is not None
    const = lambda shape: pl.BlockSpec(shape, lambda b, c: tuple(0 for _ in shape))
    tok = lambda w: pl.BlockSpec((ns, L, w), lambda b, c: (b, c, 0))
    in_specs = [tok(ZR_COLS)]
    args = [zr]
    if has_vres:
        in_specs.append(tok(RWKV_WIDTH))
        args.append(v_first)
    in_specs += [const((1, ZR_COLS)), const((8, RWKV_WIDTH)), const((LANES, 2 * RWKV_WIDTH)),
                 const((2 * LANES, 2 * RWKV_WIDTH)), const((LANES, LANES))]
    args += [mu, vec8, w2a2, g2v2, bd64]
    out_specs = [tok(RWKV_WIDTH)]
    out_shape = [jax.ShapeDtypeStruct((B, T, RWKV_WIDTH), BF16)]
    if not has_vres:
        out_specs.append(tok(RWKV_WIDTH))
        out_shape.append(jax.ShapeDtypeStruct((B, T, RWKV_WIDTH), F32))
    outs = pl.pallas_call(
        functools.partial(_rwkv_kernel, has_vres),
        grid=(B // ns, T // L),
        in_specs=in_specs, out_specs=out_specs, out_shape=out_shape,
        scratch_shapes=[pltpu.VMEM((ns, SUBLANES, ZR_COLS), F32),
                        pltpu.VMEM((ns * (RWKV_HEADS // 2), LANES, LANES), F32)],
        compiler_params=_cparams(("arbitrary", "arbitrary")),
    )(*args)
    return (outs[0], v_first) if has_vres else (outs[0], outs[1])


def _attn_step(q_ref, k_ref, v_ref, m_sc, l_sc, acc_sc, masked):
    tq = q_ref.shape[1]
    tk = k_ref.shape[1]
    lane = lax.broadcasted_iota(jnp.int32, (tq, LANES), 1)
    if masked:
        rowi = lax.broadcasted_iota(jnp.int32, (tq, tk), 0)
        coli = lax.broadcasted_iota(jnp.int32, (tq, tk), 1)
        keep = coli <= rowi
    heads = range(MLA_HEADS)
    s = [_dot(q_ref[0, :, hd * LANES:(hd + 1) * LANES], k_ref[0, :, hd * LANES:(hd + 1) * LANES], NT)
         for hd in heads]
    if masked:
        s = [jnp.where(keep, x, NEG_INF) for x in s]
    m_prev = [m_sc[hd] for hd in heads]
    m_new = [jnp.maximum(m_prev[hd], jnp.max(s[hd], axis=-1, keepdims=True)) for hd in heads]
    alpha = [jnp.exp(m_prev[hd] - m_new[hd]) for hd in heads]
    p = [jnp.exp(s[hd] - jnp.concatenate([m_new[hd]] * (tk // LANES), axis=1)) for hd in heads]
    for hd in heads:
        l_sc[hd] = alpha[hd] * l_sc[hd] + jnp.sum(p[hd], axis=-1, keepdims=True)
        m_sc[hd] = m_new[hd]
    pv = [_dot(p[hd].astype(BF16), v_ref[0, :, (hd // 2) * LANES:(hd // 2 + 1) * LANES]) for hd in heads]
    first = lane < HEAD_DIM
    for pr in range(MLA_HEADS // 2):
        acc_sc[pr] = (acc_sc[pr] * jnp.where(first, alpha[2 * pr], alpha[2 * pr + 1])
                      + jnp.where(first, pv[2 * pr], pv[2 * pr + 1]))


def _attn_step_bounded(q_ref, k_ref, v_ref, cb_ref, l_sc, acc_sc, masked):
    tq = q_ref.shape[1]
    tk = k_ref.shape[1]
    lane = lax.broadcasted_iota(jnp.int32, (tq, LANES), 1)
    if masked:
        rowi = lax.broadcasted_iota(jnp.int32, (tq, tk), 0)
        coli = lax.broadcasted_iota(jnp.int32, (tq, tk), 1)
        keep = coli <= rowi
    ps = []
    for hd in range(MLA_HEADS):
        s = _dot(q_ref[0, :, hd * LANES:(hd + 1) * LANES], k_ref[0, :, hd * LANES:(hd + 1) * LANES], NT)
        c = cb_ref[0, hd:hd + 1, :]
        p = jnp.exp(s - jnp.concatenate([c] * (tk // LANES), axis=1))
        if masked:
            p = jnp.where(keep, p, 0.0)
        part = p[:, 0:LANES]
        for t in range(1, tk // LANES):
            part = part + p[:, t * LANES:(t + 1) * LANES]
        l_sc[hd] = l_sc[hd] + part
        ps.append(p.astype(BF16))
    first = lane < HEAD_DIM
    for pr in range(MLA_HEADS // 2):
        pv = _dot(jnp.concatenate(ps[2 * pr:2 * pr + 2], axis=0), v_ref[0, :, pr * LANES:(pr + 1) * LANES])
        acc_sc[pr] = acc_sc[pr] + jnp.where(first, pv[:tq], pv[tq:])


def _attn_finish(o_ref, l_sc, acc_sc, lane_partial):
    tq = o_ref.shape[1]
    lane = lax.broadcasted_iota(jnp.int32, (tq, LANES), 1)
    outs = []
    for pr in range(MLA_HEADS // 2):
        la, lb = l_sc[2 * pr], l_sc[2 * pr + 1]
        if lane_partial:
            la = jnp.sum(la, axis=-1, keepdims=True)
            lb = jnp.sum(lb, axis=-1, keepdims=True)
        outs.append(acc_sc[pr] / jnp.where(lane < HEAD_DIM, la, lb))
    o_ref[0] = jnp.concatenate(outs, axis=1).astype(BF16)


def _attn_kernel(ok_ref, qi_ref, kj_ref, q_ref, k_ref, v_ref, cb_ref, o_ref, m_sc, l_sc, acc_sc):
    i = qi_ref[pl.program_id(1)]
    j = kj_ref[pl.program_id(1)]
    bounded = ok_ref[pl.program_id(0)] == 1
    exact = jnp.logical_not(bounded)

    @pl.when(j == 0)
    def _():
        m_sc[...] = jnp.full(m_sc.shape, NEG_INF, F32)
        l_sc[...] = jnp.zeros(l_sc.shape, F32)
        acc_sc[...] = jnp.zeros(acc_sc.shape, F32)

    @pl.when(bounded & (j < i))
    def _():
        _attn_step_bounded(q_ref, k_ref, v_ref, cb_ref, l_sc, acc_sc, masked=False)

    @pl.when(bounded & (j == i))
    def _():
        _attn_step_bounded(q_ref, k_ref, v_ref, cb_ref, l_sc, acc_sc, masked=True)
        _attn_finish(o_ref, l_sc, acc_sc, lane_partial=True)

    @pl.when(exact & (j < i))
    def _():
        _attn_step(q_ref, k_ref, v_ref, m_sc, l_sc, acc_sc, masked=False)

    @pl.when(exact & (j == i))
    def _():
        _attn_step(q_ref, k_ref, v_ref, m_sc, l_sc, acc_sc, masked=True)
        _attn_finish(o_ref, l_sc, acc_sc, lane_partial=False)


def _attention(q, k, v, qk_gain):
    B, T, _ = q.shape
    nq = T // TQ
    gmax = jnp.max(jnp.abs(qk_gain), axis=1)
    c = gmax[0] * gmax[1] * (MLA_QK_DIM ** 0.5) * ATTN_BOUND_SLACK
    ok = jnp.broadcast_to((c <= ATTN_BOUND_MAX).astype(jnp.int32), (B,))
    cb = jnp.broadcast_to(c, (B, MLA_HEADS, LANES))
    pairs = [(i, j) for i in range(nq) for j in range(i + 1)]
    qi = jnp.asarray([p[0] for p in pairs], jnp.int32)
    kj = jnp.asarray([p[1] for p in pairs], jnp.int32)
    grid_spec = pltpu.PrefetchScalarGridSpec(
        num_scalar_prefetch=3,
        grid=(B, len(pairs)),
        in_specs=[pl.BlockSpec((1, TQ, MLA_HEADS * LANES), lambda b, t, ok, qi, kj: (b, qi[t], 0)),
                  pl.BlockSpec((1, TQ, MLA_HEADS * LANES), lambda b, t, ok, qi, kj: (b, kj[t], 0)),
                  pl.BlockSpec((1, TQ, MLA_HEADS * HEAD_DIM), lambda b, t, ok, qi, kj: (b, kj[t], 0)),
                  pl.BlockSpec((1, MLA_HEADS, LANES), lambda b, t, ok, qi, kj: (b, 0, 0))],
        out_specs=pl.BlockSpec((1, TQ, MLA_HEADS * HEAD_DIM), lambda b, t, ok, qi, kj: (b, qi[t], 0)),
        scratch_shapes=[pltpu.VMEM((MLA_HEADS, TQ, LANES), F32),
                        pltpu.VMEM((MLA_HEADS, TQ, LANES), F32),
                        pltpu.VMEM((MLA_HEADS // 2, TQ, LANES), F32)])
    return pl.pallas_call(
        _attn_kernel,
        grid_spec=grid_spec,
        out_shape=jax.ShapeDtypeStruct((B, T, MLA_HEADS * HEAD_DIM), BF16),
        compiler_params=_cparams(("arbitrary", "arbitrary")),
    )(ok, qi, kj, q, k, v, cb)


def _mixout_kernel(has_router, *refs):
    if has_router:
        (x_ref, yr_ref, yp_ref, ym_ref, wo_ref, mod_ref, gain_ref, rt_ref, tri_ref,
         xo_ref, h_ref, route_ref, cnt_ref, cnt_sc) = refs
    else:
        x_ref, yr_ref, yp_ref, ym_ref, wo_ref, mod_ref, gain_ref, xo_ref, h_ref = refs
    mod = mod_ref[0]
    o1 = RWKV_WIDTH
    o2 = RWKV_WIDTH + POOL_WIDTH
    mix = (_dot(yr_ref[0], wo_ref[0:o1, :]) + _dot(yp_ref[0], wo_ref[o1:o2, :])
           + _dot(ym_ref[0], wo_ref[o2:, :]))
    x = x_ref[0] + mod[2:3] * mix
    xo_ref[0] = x
    h = _rms(x) * gain_ref[...] * (1.0 + mod[4:5]) + mod[3:4]
    if not has_router:
        h_ref[0] = h.astype(BF16)
    else:
        hp = _pack_bf16_pairs(h)
        for ck in range(ROW_CHUNKS):
            h_ref[ck, 0] = hp[:, ck * PACK_CHUNK_W:(ck + 1) * PACK_CHUNK_W]
        logits = _dot(h.astype(BF16), rt_ref[...])
        lane = lax.broadcasted_iota(jnp.int32, logits.shape, 1).astype(F32)
        lg = jnp.where(lane < N_EXPERTS, logits, -jnp.inf)
        m1 = jnp.max(lg, axis=-1, keepdims=True)
        i1 = jnp.min(jnp.where(lg == m1, lane, float(LANES)), axis=-1, keepdims=True)
        lg2 = jnp.where(lane == i1, -jnp.inf, lg)
        m2 = jnp.max(lg2, axis=-1, keepdims=True)
        i2 = jnp.min(jnp.where(lg2 == m2, lane, float(LANES)), axis=-1, keepdims=True)
        e2 = jnp.exp(m2 - m1)
        g1 = 1.0 / (1.0 + e2)
        g2 = e2 / (1.0 + e2)
        first = (pl.program_id(0) == 0) & (pl.program_id(1) == 0)

        @pl.when(first)
        def _():
            cnt_sc[...] = jnp.zeros(cnt_sc.shape, F32)

        hit1 = lane == i1
        hit2 = lane == i2
        onehot = jnp.where(hit1 | hit2, 1.0, 0.0)
        prefix = _dot(tri_ref[...], onehot.astype(BF16)) + cnt_sc[0:1, :]
        r1 = jnp.sum(jnp.where(hit1, prefix, 0.0), axis=-1, keepdims=True)
        r2 = jnp.sum(jnp.where(hit2, prefix, 0.0), axis=-1, keepdims=True)
        cnt_sc[...] = cnt_sc[...] + jnp.sum(onehot, axis=0, keepdims=True)
        cnt_ref[...] = cnt_sc[...]
        vals = (i1, i2, g1, g2, r1, r2)
        route = jnp.zeros(logits.shape, F32)
        for pos, val in enumerate(vals):
            route = jnp.where(lane == pos, val, route)
        route_ref[0] = route


def _mixout(x, yr, yp, ym, wo, mod_l, gain, router_p):
    B, T, _ = x.shape
    tm = TM_OUT
    has_router = router_p is not None
    const = lambda shape: pl.BlockSpec(shape, lambda b, i: tuple(0 for _ in shape))
    tok = lambda w: pl.BlockSpec((1, tm, w), lambda b, i: (b, i, 0))
    in_specs = [tok(D_MODEL), tok(RWKV_WIDTH), tok(POOL_WIDTH), tok(MLA_HEADS * HEAD_DIM),
                const((D_MODEL, D_MODEL)), pl.BlockSpec((1, 6, D_MODEL), lambda b, i: (b, 0, 0)),
                const((1, D_MODEL))]
    args = [x, yr, yp, ym, wo, mod_l, gain]
    out_specs = [tok(D_MODEL), tok(D_MODEL)]
    out_shape = [jax.ShapeDtypeStruct((B, T, D_MODEL), F32), jax.ShapeDtypeStruct((B, T, D_MODEL), BF16)]
    scratch = []
    if has_router:
        out_specs[1] = pl.BlockSpec((ROW_CHUNKS, 1, tm, PACK_CHUNK_W), lambda b, i: (0, b, i, 0))
        out_shape[1] = jax.ShapeDtypeStruct((ROW_CHUNKS, B, T, PACK_CHUNK_W), jnp.uint32)
        ids = np.arange(tm)
        tri = jnp.asarray(ids[:, None] > ids[None, :], BF16)
        in_specs += [const((D_MODEL, LANES)), const((tm, tm))]
        args += [router_p, tri]
        out_specs += [tok(LANES), const((SUBLANES, LANES))]
        out_shape += [jax.ShapeDtypeStruct((B, T, LANES), F32),
                      jax.ShapeDtypeStruct((SUBLANES, LANES), F32)]
        scratch = [pltpu.VMEM((SUBLANES, LANES), F32)]
    return pl.pallas_call(
        functools.partial(_mixout_kernel, has_router),
        grid=(B, T // tm),
        in_specs=in_specs, out_specs=out_specs, out_shape=out_shape, scratch_shapes=scratch,
        compiler_params=_cparams(("arbitrary", "arbitrary")),
    )(*args)


def _ffn_kernel(h_ref, wg_ref, wu_ref, wo_ref, x_ref, mod_ref, o_ref, acc):
    j = pl.program_id(1)

    @pl.when(j == 0)
    def _():
        acc[...] = jnp.zeros(acc.shape, F32)

    h = h_ref[...]
    gg = _dot(h, wg_ref[...])
    uu = _dot(h, wu_ref[...])
    acc[...] += _dot((_silu(gg) * uu).astype(BF16), wo_ref[...])

    @pl.when(j == pl.num_programs(1) - 1)
    def _():
        o_ref[...] = x_ref[...] + mod_ref[0][5:6] * acc[...]


def _ffn(h2, w_in, w_out, x, mod_l):
    N = h2.shape[0]
    T = N // mod_l.shape[0]
    tm, tf = TM_FFN, TF_FFN
    nf = D_FF // tf
    per_b = T // tm
    return pl.pallas_call(
        _ffn_kernel,
        grid=(N // tm, nf),
        in_specs=[pl.BlockSpec((tm, D_MODEL), lambda i, j: (i, 0)),
                  pl.BlockSpec((D_MODEL, tf), lambda i, j: (0, j)),
                  pl.BlockSpec((D_MODEL, tf), lambda i, j: (0, j + nf)),
                  pl.BlockSpec((tf, D_MODEL), lambda i, j: (j, 0)),
                  pl.BlockSpec((tm, D_MODEL), lambda i, j: (i, 0)),
                  pl.BlockSpec((1, 6, D_MODEL), lambda i, j: (i // per_b, 0, 0))],
        out_specs=pl.BlockSpec((tm, D_MODEL), lambda i, j: (i, 0)),
        out_shape=jax.ShapeDtypeStruct((N, D_MODEL), F32),
        scratch_shapes=[pltpu.VMEM((tm, D_MODEL), F32)],
        compiler_params=_cparams(("arbitrary", "arbitrary")),
    )(h2, w_in, w_in, w_out, x, mod_l)


def _moe_kernel(be_ref, nv_ref, last_ref, x_ref, wg_ref, wu_ref, wo_ref, o_ref, acc, xm):
    i = pl.program_id(0)
    j = pl.program_id(1)

    @pl.when(i <= last_ref[0])
    def _():
        @pl.when(j == 0)
        def _():
            acc[...] = jnp.zeros(acc.shape, F32)
            row = lax.broadcasted_iota(jnp.int32, (xm.shape[0], 1), 0)
            xp = jnp.concatenate([x_ref[ck] for ck in range(ROW_CHUNKS)], axis=1)
            xp = jnp.where(row < nv_ref[i], xp, jnp.uint32(0))
            xm[...] = _unpack_bf16_pairs(xp)

        def swiglu_rows(n_rows):
            x = xm[0:n_rows, :]
            gg = _dot(x, wg_ref[0].astype(BF16))
            uu = _dot(x, wu_ref[0].astype(BF16))
            acc[0:n_rows, :] += _dot((_silu(gg) * uu).astype(BF16), wo_ref[0].astype(BF16))

        half = xm.shape[0] // 2

        @pl.when(nv_ref[i] > half)
        def _():
            swiglu_rows(xm.shape[0])

        @pl.when(nv_ref[i] <= half)
        def _():
            swiglu_rows(half)

        @pl.when(j == pl.num_programs(1) - 1)
        def _():
            yp = _pack_bf16_pairs(acc[...])
            for ck in range(ROW_CHUNKS):
                o_ref[ck] = yp[:, ck * PACK_CHUNK_W:(ck + 1) * PACK_CHUNK_W]


def _moe_experts(xs, w_in, w_out, block_exp, n_valid, last_blk):
    n_rows = xs.shape[1]
    tm, tf = MOE_BLOCK, TF_MOE
    nf = D_FF_EXPERT // tf
    blk = lambda i, last: jnp.minimum(i, last[0])
    chunk = lambda i, j, last: jnp.where(i <= last[0], j, nf - 1)
    grid_spec = pltpu.PrefetchScalarGridSpec(
        num_scalar_prefetch=3,
        grid=(n_rows // tm, nf),
        in_specs=[pl.BlockSpec((ROW_CHUNKS, tm, PACK_CHUNK_W),
                               lambda i, j, be, nv, last: (0, blk(i, last), 0)),
                  pl.BlockSpec((1, D_MODEL, tf),
                               lambda i, j, be, nv, last: (be[blk(i, last)], 0, chunk(i, j, last))),
                  pl.BlockSpec((1, D_MODEL, tf),
                               lambda i, j, be, nv, last: (be[blk(i, last)], 0, chunk(i, j, last) + nf)),
                  pl.BlockSpec((1, tf, D_MODEL),
                               lambda i, j, be, nv, last: (be[blk(i, last)], chunk(i, j, last), 0))],
        out_specs=pl.BlockSpec((ROW_CHUNKS, tm, PACK_CHUNK_W), lambda i, j, be, nv, last: (0, blk(i, last), 0)),
        scratch_shapes=[pltpu.VMEM((tm, D_MODEL), F32), pltpu.VMEM((tm, D_MODEL), BF16)])
    return pl.pallas_call(
        _moe_kernel,
        grid_spec=grid_spec,
        out_shape=jax.ShapeDtypeStruct((ROW_CHUNKS, n_rows, PACK_CHUNK_W), jnp.uint32),
        compiler_params=_cparams(("arbitrary", "arbitrary")),
    )(block_exp, n_valid, last_blk, xs, w_in, w_in, w_out)


def _sc_mesh():
    return plsc.VectorSubcoreMesh(core_axis_name="c", subcore_axis_name="s")


def _sc_scatter_rows(x, dest, n_rows):
    N, D = x.shape
    K = dest.shape[0]
    win = SC_WINDOW

    @pl.kernel(out_type=jax.ShapeDtypeStruct((n_rows, D), x.dtype), mesh=_sc_mesh(), scratch_types=[])
    def scatter(x_hbm, d_hbm, o_hbm):
        def body(x_vmem, *idx_vmem):
            for iv in idx_vmem:
                pltpu.sync_copy(x_vmem, o_hbm.at[iv.at[0]])

        pltpu.emit_pipeline(
            body,
            grid=(N // win,),
            in_specs=[pl.BlockSpec((win, D), lambda i: (i, 0))]
            + [pl.BlockSpec((1, win), functools.partial(lambda k, i: (k, i), k)) for k in range(K)],
            out_specs=[],
            core_axis_name=("c", "s"),
            dimension_semantics=(pltpu.PARALLEL,),
        )(x_hbm, *([d_hbm] * K))

    return scatter(x, dest)


def _sc_gather_rows(x, idx):
    n = idx.shape[0]
    D = x.shape[1]
    win = SC_WINDOW

    @pl.kernel(out_type=jax.ShapeDtypeStruct((n, D), x.dtype), mesh=_sc_mesh(), scratch_types=[])
    def gather(x_hbm, i_hbm, o_hbm):
        def body(i_vmem, o_vmem):
            pltpu.sync_copy(x_hbm.at[i_vmem.at[0]], o_vmem)

        pltpu.emit_pipeline(
            body,
            grid=(n // win,),
            in_specs=[pl.BlockSpec((1, win), lambda i: (0, i))],
            out_specs=[pl.BlockSpec((win, D), lambda i: (i, 0))],
            core_axis_name=("c", "s"),
            dimension_semantics=(pltpu.PARALLEL,),
        )(i_hbm, o_hbm)

    return gather(x, idx.reshape(1, n))


def _combine_kernel(x_ref, ya_ref, yb_ref, route_ref, mod_ref, o_ref):
    rt = route_ref[...]
    ya = _unpack_bf16_pairs(jnp.concatenate([ya_ref[0, ck] for ck in range(ROW_CHUNKS)], axis=1))
    yb = _unpack_bf16_pairs(jnp.concatenate([yb_ref[0, ck] for ck in range(ROW_CHUNKS)], axis=1))
    f = rt[:, 2:3] * ya.astype(F32) + rt[:, 3:4] * yb.astype(F32)
    o_ref[...] = x_ref[...] + mod_ref[0][5:6] * f


def _combine(x, y2, route, mod_l):
    N = x.shape[0]
    T = N // mod_l.shape[0]
    tm = 1024
    per_b = T // tm
    tok = pl.BlockSpec((tm, D_MODEL), lambda i: (i, 0))
    slot = lambda k: pl.BlockSpec((1, ROW_CHUNKS, tm, PACK_CHUNK_W), lambda i: (k, 0, i, 0))
    return pl.pallas_call(
        _combine_kernel,
        grid=(N // tm,),
        in_specs=[tok, slot(0), slot(1),
                  pl.BlockSpec((tm, LANES), lambda i: (i, 0)),
                  pl.BlockSpec((1, 6, D_MODEL), lambda i: (i // per_b, 0, 0))],
        out_specs=tok,
        out_shape=jax.ShapeDtypeStruct((N, D_MODEL), F32),
        compiler_params=_cparams(("arbitrary",)),
    )(x, y2, y2, route, mod_l)


def _moe(h2, route, counts, w_in, w_out, x, mod_l):
    N = x.shape[0]
    blk = MOE_BLOCK
    cnt = counts[0, :N_EXPERTS].astype(jnp.int32)
    padded = (cnt + blk - 1) // blk * blk
    pend = jnp.cumsum(padded)
    pstart = pend - padded
    e = route[:, 0:TOP_K].astype(jnp.int32)
    rank = route[:, 2 * TOP_K:3 * TOP_K].astype(jnp.int32)
    dest = (jnp.take(pstart, e) + rank).T
    n_blocks = N * TOP_K // blk + N_EXPERTS
    bstart = jnp.arange(n_blocks, dtype=jnp.int32) * blk
    block_exp = jnp.minimum(jnp.sum((bstart[:, None] >= pend[None, :]).astype(jnp.int32), axis=1),
                            N_EXPERTS - 1)
    n_valid = jnp.clip(cnt[block_exp] - (bstart - pstart[block_exp]), 0, blk).astype(jnp.int32)
    n_rows = n_blocks * blk
    dest_ck = dest[:, None, :] + (jnp.arange(ROW_CHUNKS, dtype=jnp.int32) * n_rows)[None, :, None]
    xs = _sc_scatter_rows(h2, dest_ck.reshape(TOP_K, ROW_CHUNKS * N), ROW_CHUNKS * n_rows)
    last_blk = (pend[-1:] // blk - 1).astype(jnp.int32)
    yb = _moe_experts(xs.reshape(ROW_CHUNKS, n_rows, PACK_CHUNK_W), w_in, w_out, block_exp, n_valid,
                      last_blk)
    y2 = _sc_gather_rows(yb.reshape(ROW_CHUNKS * n_rows, PACK_CHUNK_W), dest_ck.reshape(-1))
    return _combine(x, y2.reshape(TOP_K, ROW_CHUNKS, N, PACK_CHUNK_W), route, mod_l)


def _layout_w_in(w, has_vres):
    W = RWKV_WIDTH
    off_gd = 3 * W + RWKV_DECAY_LORA + RWKV_ICLR_LORA
    off_pool = off_gd + RWKV_GATE_LORA
    off_q = off_pool + POOL_WIDTH
    off_kv = off_q + MLA_Q_LORA
    off_kr = off_kv + MLA_KV_LORA
    n_base = off_kr + MLA_QK_ROPE
    d = w.shape[0]
    zeros = lambda n: jnp.zeros((d, n), w.dtype)
    vd = w[:, n_base:n_base + RWKV_VRES_LORA] if has_vres else zeros(RWKV_VRES_LORA)
    cols = [w[:, :off_gd], w[:, off_gd:off_pool], vd, zeros(ZR_COLS - off_pool - RWKV_VRES_LORA),
            w[:, off_pool:off_q], w[:, off_q:off_kv], w[:, off_kv:off_kr],
            zeros(MLA_QK_NOPE), w[:, off_kr:n_base], zeros(LANES - MLA_QK_DIM)]
    return jnp.concatenate(cols, axis=1).astype(BF16)


def _pad_heads(w, per_head, keep_from, keep_n):
    K = w.shape[0]
    wh = w.reshape(K, MLA_HEADS, per_head)[:, :, keep_from:keep_from + keep_n]
    wh = jnp.pad(wh, ((0, 0), (0, 0), (0, LANES - keep_n)))
    return wh.reshape(K, MLA_HEADS * LANES)


def kernel(x, c, positions, w_ada, b_ada, norm_gain, w_in_first, w_in_rest, mu_shift, mu_shift_v,
           rwkv_vec, rwkv_v0, rwkv_w2, rwkv_a2, rwkv_g2, rwkv_v2, pool_w, pool_scale,
           mla_q_lat_gain, mla_kv_lat_gain, mla_wq_up, mla_wkv_up, mla_qk_gain, w_out, ffn_w_in,
           ffn_w_out, moe_router, moe_w_in, moe_w_out):
    B, T, D = x.shape
    depth = w_ada.shape[0]
    assert D == D_MODEL and B % WKV_STEP_SEQS == 0 and D_FF_EXPERT % TF_MOE == 0
    assert T % max(TM_MIX, TM_OUT, TQ, WKV_CHUNK * WKV_STEP_CHUNKS) == 0
    assert (B * T) % max(TM_FFN, MOE_BLOCK) == 0 and (B * T * TOP_K) % MOE_BLOCK == 0
    W = RWKV_WIDTH
    mod = _adaln(c, w_ada, b_ada).reshape(depth, B, 6, D)
    inv_freq = ROPE_BASE ** (-jnp.arange(0, MLA_QK_ROPE, 2, dtype=F32) / MLA_QK_ROPE)
    cosf, sinf = _rope_tables(positions, inv_freq)
    hid = np.arange(LANES) // HEAD_DIM
    bd64 = jnp.asarray(hid[:, None] == hid[None, :], BF16)
    bd128 = jnp.ones((LANES, LANES), BF16)

    v_first = None
    for l in range(depth):
        has_vres = l > 0
        mod_l = mod[l]
        win = _layout_w_in(w_in_first if l == 0 else w_in_rest[l - 1], has_vres)
        poolw = jax.scipy.linalg.block_diag(*[pool_w[l, g] for g in range(len(POOL_WINDOWS))]).astype(BF16)
        wq = _pad_heads(mla_wq_up[l], MLA_QK_DIM, 0, MLA_QK_DIM).astype(BF16)
        wk = _pad_heads(mla_wkv_up[l], MLA_QK_NOPE + HEAD_DIM, 0, MLA_QK_NOPE).astype(BF16)
        wv = mla_wkv_up[l].reshape(MLA_KV_LORA, MLA_HEADS, MLA_QK_NOPE + HEAD_DIM)[:, :, MLA_QK_NOPE:]
        wv = wv.reshape(MLA_KV_LORA, MLA_HEADS * HEAD_DIM).astype(BF16)
        qkg = jnp.tile(jnp.pad(mla_qk_gain[l], ((0, 0), (0, LANES - MLA_QK_DIM))), (1, MLA_HEADS))
        zr, y_pool, q, k, v = _mixin(
            x, mod_l, norm_gain[l, 0].reshape(1, D), win, cosf, sinf, poolw,
            pool_scale[l].reshape(1, -1), mla_q_lat_gain[l].reshape(1, -1),
            mla_kv_lat_gain[l].reshape(1, -1), wq, wk, wv, qkg, bd128)

        pad_mu = ZR_COLS - mu_shift.shape[1] - RWKV_VRES_LORA
        mu_v = mu_shift_v[l - 1] if has_vres else jnp.zeros((RWKV_VRES_LORA,), F32)
        mu = jnp.concatenate([mu_shift[l], mu_v, jnp.zeros((pad_mu,), F32)]).reshape(1, ZR_COLS)
        v0 = rwkv_v0[l - 1] if has_vres else jnp.zeros((W,), F32)
        vec8 = jnp.concatenate([rwkv_vec[l], v0[None]], axis=0)
        w2a2 = jax.scipy.linalg.block_diag(rwkv_w2[l], rwkv_a2[l]).astype(BF16)
        g2 = jnp.pad(rwkv_g2[l], ((0, 2 * LANES - RWKV_GATE_LORA), (0, 0)))
        if has_vres:
            v2 = jnp.pad(rwkv_v2[l - 1], ((RWKV_GATE_LORA, 2 * LANES - RWKV_GATE_LORA - RWKV_VRES_LORA), (0, 0)))
        else:
            v2 = jnp.zeros((2 * LANES, W), F32)
        g2v2 = jnp.concatenate([g2, v2], axis=1).astype(BF16)
        y_rwkv, v_first = _rwkv(zr, v_first, mu, vec8, w2a2, g2v2, bd64)

        y_mla = _attention(q, k, v, mla_qk_gain[l])

        is_moe = (l % 2 == 1)
        router_p = None
        if is_moe:
            router_p = jnp.pad(moe_router[l // 2], ((0, 0), (0, LANES - N_EXPERTS))).astype(BF16)
        outs = _mixout(x, y_rwkv, y_pool, y_mla, w_out[l].astype(BF16), mod_l,
                       norm_gain[l, 1].reshape(1, D), router_p)
        x_mid, h2 = outs[0], outs[1]
        xf = x_mid.reshape(B * T, D)
        if is_moe:
            xo = _moe(h2.reshape(ROW_CHUNKS * B * T, PACK_CHUNK_W), outs[2].reshape(B * T, LANES), outs[3],
                      moe_w_in[l // 2], moe_w_out[l // 2], xf, mod_l)
        else:
            xo = _ffn(h2.reshape(B * T, D), ffn_w_in[l // 2].astype(BF16), ffn_w_out[l // 2].astype(BF16),
                      xf, mod_l)
        x = xo.reshape(B, T, D)
    return x
```

```python
import functools

import numpy as np
import jax
import jax.numpy as jnp
from jax import lax
from jax.experimental import pallas as pl
from jax.experimental.pallas import tpu as pltpu
from jax.experimental.pallas import tpu_sc as plsc

F32 = jnp.float32
BF16 = jnp.bfloat16

D_MODEL = 1024
HEAD_DIM = 64
RWKV_WIDTH = 512
RWKV_HEADS = RWKV_WIDTH // HEAD_DIM
POOL_WIDTH = 256
POOL_WINDOWS = (2, 4, 8, 16)
POOL_HALO = 16
MLA_HEADS = 4
MLA_QK_NOPE = 64
MLA_QK_ROPE = 32
MLA_QK_DIM = MLA_QK_NOPE + MLA_QK_ROPE
MLA_Q_LORA = 256
MLA_KV_LORA = 128
ROPE_BASE = 10000.0
RWKV_DECAY_LORA = 64
RWKV_ICLR_LORA = 64
RWKV_VRES_LORA = 32
RWKV_GATE_LORA = 160
RWKV_LNX_EPS = 64e-5
D_FF = 2816
N_EXPERTS = 8
TOP_K = 2
D_FF_EXPERT = 3584
NORM_EPS = 1e-6
NEG_INF = -1e30

LANES = 128
SUBLANES = 8
VMEM_LIMIT = 56 * 1024 * 1024

ZR_COLS = 3 * RWKV_WIDTH + (RWKV_DECAY_LORA + RWKV_ICLR_LORA) + 2 * LANES
Z_POOL_OFF = ZR_COLS
Z_QLAT_OFF = Z_POOL_OFF + POOL_WIDTH
Z_KVLAT_OFF = Z_QLAT_OFF + MLA_Q_LORA
Z_KROPE_OFF = Z_KVLAT_OFF + MLA_KV_LORA
Z_COLS = Z_KROPE_OFF + LANES

MIXIN_PIECE_W = 512
TM_MIX = 512
TM_OUT = 1024
WKV_CHUNK = 64
TQ = 512
ATTN_BOUND_SLACK = 1.02
ATTN_BOUND_MAX = 40.0
TM_FFN = 1024
TF_FFN = D_FF // 2
MOE_BLOCK = 1024
MXU_TILE = 256
TF_MOE = 2 * MXU_TILE
SC_WINDOW = 128
ROW_CHUNKS = 4
ROW_CHUNK_W = D_MODEL // ROW_CHUNKS
PACK_CHUNK_W = ROW_CHUNK_W // 2

SEGSUM_SPLITS = 1

NN = (((1,), (0,)), ((), ()))
NT = (((1,), (1,)), ((), ()))


def _dot(a, b, dims=NN):
    return lax.dot_general(a, b, dims, preferred_element_type=F32)


def _split2(a):
    hi = a.astype(BF16)
    lo = (a - hi.astype(F32)).astype(BF16)
    return hi, lo


def _mm(a, b, dims=NN, passes=3):
    if passes == 1:
        return _dot(a.astype(BF16), b.astype(BF16), dims)
    ah, al = _split2(a)
    bh, bl = _split2(b)
    return _dot(ah, bh, dims) + (_dot(ah, bl, dims) + _dot(al, bh, dims))


def _mm_exact_rhs(a, b_bf16, splits=SEGSUM_SPLITS):
    m, w = a.shape
    nb = w // LANES
    stacked = jnp.concatenate([a[:, i * LANES:(i + 1) * LANES] for i in range(nb)], axis=0)
    out = None
    rem = stacked
    for s in range(splits):
        part = rem.astype(BF16)
        term = _dot(part, b_bf16)
        out = term if out is None else out + term
        if s + 1 < splits:
            rem = rem - part.astype(F32)
    return jnp.concatenate([out[i * m:(i + 1) * m] for i in range(nb)], axis=1)


def _pack_bf16_pairs(h):
    w = h.shape[1] // 2
    lo = lax.bitcast_convert_type(h[:, :w].astype(BF16).astype(F32), jnp.uint32)
    hi = lax.bitcast_convert_type(h[:, w:].astype(BF16).astype(F32), jnp.uint32)
    return (lo >> 16) | (hi & jnp.uint32(0xFFFF0000))


def _unpack_bf16_pairs(p):
    lo = lax.bitcast_convert_type(p << 16, F32)
    hi = lax.bitcast_convert_type(p & jnp.uint32(0xFFFF0000), F32)
    return jnp.concatenate([lo, hi], axis=1).astype(BF16)


def _sigmoid(x):
    return 1.0 / (1.0 + jnp.exp(-x))


def _silu(x):
    return x * _sigmoid(x)


def _rms(x, eps=NORM_EPS):
    return x * lax.rsqrt(jnp.mean(x * x, axis=-1, keepdims=True) + eps)


def _cparams(sem):
    return pltpu.CompilerParams(dimension_semantics=sem, vmem_limit_bytes=VMEM_LIMIT)


def _adaln_kernel(c_ref, w_ref, b_ref, o_ref):
    ca = _silu(c_ref[...])
    o_ref[0] = _mm(ca, w_ref[0]) + b_ref[0]


def _adaln(c, w_ada, b_ada):
    L = w_ada.shape[0]
    B = c.shape[0]
    n = w_ada.shape[2] // D_MODEL
    return pl.pallas_call(
        _adaln_kernel,
        grid=(L, n),
        in_specs=[pl.BlockSpec((B, D_MODEL), lambda l, j: (0, 0)),
                  pl.BlockSpec((1, D_MODEL, D_MODEL), lambda l, j: (l, 0, j)),
                  pl.BlockSpec((1, 1, D_MODEL), lambda l, j: (l, 0, j))],
        out_specs=pl.BlockSpec((1, B, D_MODEL), lambda l, j: (l, 0, j)),
        out_shape=jax.ShapeDtypeStruct((L, B, n * D_MODEL), F32),
        compiler_params=_cparams(("arbitrary", "arbitrary")),
    )(c, w_ada, b_ada.reshape(L, 1, -1))


def _rope(x, cosf, sinf, lane):
    up = pltpu.roll(x, LANES - MLA_QK_ROPE // 2, axis=1)
    dn = pltpu.roll(x, MLA_QK_ROPE // 2, axis=1)
    rot = jnp.where(lane < MLA_QK_NOPE + MLA_QK_ROPE // 2, -up, dn)
    return x * cosf + rot * sinf


def _rope_kernel(pos_ref, freq_ref, cos_ref, sin_ref):
    ang = pos_ref[0].astype(F32) * freq_ref[...]
    cos_ref[0] = jnp.cos(ang)
    sin_ref[0] = jnp.sin(ang)


def _rope_tables(positions, inv_freq):
    B, T = positions.shape
    nf = inv_freq.shape[0]
    per_row = LANES // nf
    pos_p = jnp.broadcast_to(positions[:, :, None], (B, T, nf)).reshape(B, T // per_row, LANES)
    freq_p = jnp.tile(inv_freq, per_row).reshape(1, LANES)
    blk = pl.BlockSpec((1, T // per_row, LANES), lambda b: (b, 0, 0))
    cos_p, sin_p = pl.pallas_call(
        _rope_kernel,
        grid=(B,),
        in_specs=[blk, pl.BlockSpec((1, LANES), lambda b: (0, 0))],
        out_specs=[blk, blk],
        out_shape=[jax.ShapeDtypeStruct((B, T // per_row, LANES), F32)] * 2,
        compiler_params=_cparams(("arbitrary",)),
    )(pos_p, freq_p)

    def spread(tab, fill):
        t16 = tab.reshape(B, T, nf)
        return jnp.concatenate([jnp.full((B, T, MLA_QK_NOPE), fill, F32), t16, t16,
                                jnp.full((B, T, LANES - MLA_QK_DIM), fill, F32)], axis=-1)

    return spread(cos_p, 1.0), spread(sin_p, 0.0)


def _mixin_kernel(x_ref, mod_ref, gain_ref, win_ref, cos_ref, sin_ref, poolw_ref, pools_ref,
                  qg_ref, kvg_ref, wq_ref, wk_ref, wv_ref, qkg_ref, bd_ref,
                  zr_ref, yp_ref, q_ref, k_ref, v_ref, ubuf):
    i = pl.program_id(1)
    tm = x_ref.shape[1]
    x = x_ref[0]
    mod = mod_ref[0]
    @pl.when(i == 0)
    def _():
        ubuf[0:POOL_HALO, :] = jnp.zeros((POOL_HALO, POOL_WIDTH), F32)

    h = (_rms(x) * gain_ref[...] * (1.0 + mod[1:2]) + mod[0:1]).astype(BF16)
    zb = _dot(h, win_ref[:, ZR_COLS:])
    zcol = lambda off, w: zb[:, off - ZR_COLS:off - ZR_COLS + w]
    def project_piece(n):
        cols = slice(n * MIXIN_PIECE_W, min((n + 1) * MIXIN_PIECE_W, ZR_COLS))
        zr_ref[0, :, cols] = _dot(h, win_ref[:, cols])

    project_piece(0)
    u = zcol(Z_POOL_OFF, POOL_WIDTH)
    ubuf[POOL_HALO:, :] = u
    ue = ubuf[...]
    s2 = ue + pltpu.roll(ue, 1, axis=0)
    s4 = s2 + pltpu.roll(s2, 2, axis=0)
    s8 = s4 + pltpu.roll(s4, 4, axis=0)
    s16 = s8 + pltpu.roll(s8, 8, axis=0)
    ubuf[0:POOL_HALO, :] = u[tm - POOL_HALO:, :]
    lane_p = lax.broadcasted_iota(jnp.int32, (tm, POOL_WIDTH), 1)
    grp = lane_p // (POOL_WIDTH // len(POOL_WINDOWS))
    win_sum = jnp.where(grp == 0, s2[POOL_HALO:], jnp.where(grp == 1, s4[POOL_HALO:],
                        jnp.where(grp == 2, s8[POOL_HALO:], s16[POOL_HALO:])))
    win = jnp.where(grp == 0, 2, jnp.where(grp == 1, 4, jnp.where(grp == 2, 8, 16)))
    t_abs = i * tm + lax.broadcasted_iota(jnp.int32, (tm, POOL_WIDTH), 0)
    cnt = jnp.minimum(t_abs + 1, win).astype(F32)
    p = win_sum / cnt - u
    yp = _dot(p.astype(BF16), poolw_ref[...]) * pools_ref[...]
    yp_ref[0] = yp.astype(BF16)
    project_piece(1)

    lane = lax.broadcasted_iota(jnp.int32, (tm, LANES), 1)
    cosf = cos_ref[0]
    sinf = sin_ref[0]

    q_lat = zcol(Z_QLAT_OFF, MLA_Q_LORA)
    kv_lat = zcol(Z_KVLAT_OFF, MLA_KV_LORA)
    k_rope = zcol(Z_KROPE_OFF, LANES)
    qn = (_rms(q_lat) * qg_ref[...]).astype(BF16)
    kvn = (_rms(kv_lat) * kvg_ref[...]).astype(BF16)
    q = _dot(qn, wq_ref[...])
    kx = _dot(kvn, wk_ref[...])
    v = _dot(kvn, wv_ref[...])
    v_ref[0] = v.astype(BF16)
    project_piece(2)
    k_pe = _rope(k_rope, cosf, sinf, lane)
    qs, ks = [], []
    for hd in range(MLA_HEADS):
        sl = slice(hd * LANES, (hd + 1) * LANES)
        qs.append(_rope(q[:, sl], cosf, sinf, lane))
        ks.append(kx[:, sl] + k_pe)
    q = jnp.concatenate(qs, axis=1)
    k = jnp.concatenate(ks, axis=1)
    project_piece(3)
    qss = _mm_exact_rhs(q * q, bd_ref[...]) * (1.0 / MLA_QK_DIM)
    kss = _mm_exact_rhs(k * k, bd_ref[...]) * (1.0 / MLA_QK_DIM)
    qkg = qkg_ref[...]
    q = q * lax.rsqrt(qss + NORM_EPS) * qkg[0:1] * (MLA_QK_DIM ** -0.5)
    k = k * lax.rsqrt(kss + NORM_EPS) * qkg[1:2]
    q_ref[0] = q.astype(BF16)
    k_ref[0] = k.astype(BF16)


def _mixin(x, mod_l, gain, win, cosf, sinf, poolw, pools, qg, kvg, wq, wk, wv, qkg, bd128):
    B, T, _ = x.shape
    tm = TM_MIX
    const = lambda shape: pl.BlockSpec(shape, lambda b, i: tuple(0 for _ in shape))
    tok = lambda w: pl.BlockSpec((1, tm, w), lambda b, i: (b, i, 0))
    return pl.pallas_call(
        _mixin_kernel,
        grid=(B, T // tm),
        in_specs=[tok(D_MODEL),
                  pl.BlockSpec((1, 6, D_MODEL), lambda b, i: (b, 0, 0)),
                  const((1, D_MODEL)), const((D_MODEL, Z_COLS)),
                  tok(LANES), tok(LANES),
                  const((POOL_WIDTH, POOL_WIDTH)), const((1, POOL_WIDTH)),
                  const((1, MLA_Q_LORA)), const((1, MLA_KV_LORA)),
                  const((MLA_Q_LORA, MLA_HEADS * LANES)), const((MLA_KV_LORA, MLA_HEADS * LANES)),
                  const((MLA_KV_LORA, MLA_HEADS * HEAD_DIM)), const((2, MLA_HEADS * LANES)),
                  const((LANES, LANES))],
        out_specs=[tok(ZR_COLS), tok(POOL_WIDTH), tok(MLA_HEADS * LANES), tok(MLA_HEADS * LANES),
                   tok(MLA_HEADS * HEAD_DIM)],
        out_shape=[jax.ShapeDtypeStruct((B, T, ZR_COLS), F32),
                   jax.ShapeDtypeStruct((B, T, POOL_WIDTH), BF16),
                   jax.ShapeDtypeStruct((B, T, MLA_HEADS * LANES), BF16),
                   jax.ShapeDtypeStruct((B, T, MLA_HEADS * LANES), BF16),
                   jax.ShapeDtypeStruct((B, T, MLA_HEADS * HEAD_DIM), BF16)],
        scratch_shapes=[pltpu.VMEM((POOL_HALO + tm, POOL_WIDTH), F32)],
        compiler_params=_cparams(("arbitrary", "arbitrary")),
    )(x, mod_l, gain, win, cosf, sinf, poolw, pools, qg, kvg, wq, wk, wv, qkg, bd128)


WKV_PASSES_SCORE = 1
WKV_PASSES_INV = 1
WKV_PASSES_APPLY = 1
WKV_PASSES_STATE = 1
WKV_STEP_CHUNKS = 1
WKV_STEP_SEQS = 8
WKV_GROUPS = 2
WKV_GROUP_LEAD = 4


def _stack_heads(xp, lane):
    return jnp.concatenate([jnp.where(lane < HEAD_DIM, xp, 0.0),
                            jnp.where(lane >= HEAD_DIM, xp, 0.0)], axis=0)


def _wkv_prep(r, lw, k, v, kk, a, tri, masks):
    L = r[0].shape[0]
    nc = len(r)
    each = lambda f, *ls: [f(*xs) for xs in zip(*ls)]
    lane = lax.broadcasted_iota(jnp.int32, (L, LANES), 1)
    stack = lambda x: _stack_heads(x, lane)
    cum = each(lambda x: _mm_exact_rhs_left(tri, x), lw)
    cum_last = each(lambda c: c[L - 1:L, :], cum)
    e_w = each(jnp.exp, cum)
    e_wm = each(lambda c, x: jnp.exp(c - x), cum, lw)
    e_iw = each(lambda c: jnp.exp(-c), cum)
    e_d = each(lambda cl, c: jnp.exp(cl - c), cum_last, cum)
    beta = each(lambda x, y: x * y, kk, a)
    r_f = each(lambda x, e: x * e, r, e_w)
    a_f = each(lambda x, e: -x * e, kk, e_wm)
    a_s = each(stack, a_f)
    b_s = each(lambda x, e: stack(x * e), beta, e_iw)
    k_s = each(lambda x, e: stack(x * e), k, e_iw)
    b_d = each(lambda x, e: stack(x * e), beta, e_d)
    k_d = each(lambda x, e: stack(x * e), k, e_d)
    v_s = each(stack, v)
    yield
    g = each(lambda af, rf, bs, ks: _mm(jnp.concatenate([af, rf], axis=0),
                                        jnp.concatenate([bs, ks], axis=0), NT, WKV_PASSES_SCORE),
             a_f, r_f, b_s, k_s)
    strict, incl, levels = masks
    a_ab = each(lambda x: jnp.where(strict, x[:L, :LANES], 0.0), g)
    a_ak = each(lambda x: jnp.where(strict, x[:L, LANES:], 0.0), g)
    s_rb = each(lambda x: jnp.where(incl, x[L:, :LANES], 0.0), g)
    s_rk = each(lambda x: jnp.where(incl, x[L:, LANES:], 0.0), g)
    eye = jnp.where(levels[0][1], 1.0, 0.0)
    tinv = each(lambda x: eye + jnp.where(levels[0][0], x, 0.0), a_ab)
    yield
    for lvl_mask, _ in levels[1:]:
        et = each(lambda x, t: _mm(jnp.where(lvl_mask, x, 0.0), stack(t), NN, WKV_PASSES_INV), a_ab, tinv)
        tinv = each(lambda t, x: t + _mm(t, stack(x), NN, WKV_PASSES_INV), tinv, et)
        yield
    av = each(lambda x, y: _mm(x, y, NN, WKV_PASSES_APPLY), a_ak, v_s)
    tx = each(lambda t, x, y: _mm(t, jnp.concatenate([x, stack(y)], axis=1), NN, WKV_PASSES_APPLY),
              tinv, a_s, av)
    ta_s = each(lambda x: stack(x[:, :LANES]), tx)
    c1_s = each(lambda x: stack(x[:, LANES:]), tx)
    yield
    ra = each(lambda rf, s, x: rf + _mm(s, x, NN, WKV_PASSES_APPLY), r_f, s_rb, ta_s)
    c2 = each(lambda sb, sk, x, vs: _mm(jnp.concatenate([sb, sk], axis=1),
                                        jnp.concatenate([x, vs], axis=0),
                                        NN, WKV_PASSES_APPLY), s_rb, s_rk, c1_s, v_s)
    yield
    tb = each(lambda x, y, bd: _mm(jnp.concatenate([x, y], axis=1).T, bd, NN, WKV_PASSES_APPLY),
              ta_s, c1_s, b_d)
    c3 = each(lambda x, vs, kd: x[LANES:] + _mm(vs.T, kd, NN, WKV_PASSES_APPLY), tb, v_s, k_d)
    return [(ra[i], c2[i], jnp.exp(cum_last[i]), tb[i][:LANES], c3[i]) for i in range(nc)]


def _mm_exact_rhs_left(tri_bf16, x):
    x0 = x.astype(BF16)
    x1 = (x - x0.astype(F32)).astype(BF16)
    return _dot(tri_bf16, x0) + _dot(tri_bf16, x1)


def _wkv_masks(L):
    row = lax.broadcasted_iota(jnp.int32, (L, 2 * L), 0)
    col = lax.broadcasted_iota(jnp.int32, (L, 2 * L), 1) % L
    strict = row > col
    incl = row >= col
    levels = []
    m = 1
    while m < L:
        same = (row // (2 * m)) == (col // (2 * m))
        lvl = same & ((row % (2 * m)) >= m) & ((col % (2 * m)) < m)
        levels.append((lvl, row == col))
        m *= 2
    return strict, incl, levels


def _rwkv_kernel(has_vres, *refs):
    if has_vres:
        (z_ref, vf_ref, mu_ref, vec_ref, w2a2_ref, g2v2_ref, bd_ref,
         y_ref, carry, state) = refs
    else:
        (z_ref, mu_ref, vec_ref, w2a2_ref, g2v2_ref, bd_ref,
         y_ref, vout_ref, carry, state) = refs
    c = pl.program_id(1)
    n_seq, seq_rows = z_ref.shape[0], z_ref.shape[1]
    rows = n_seq * seq_rows
    L = WKV_CHUNK
    W = RWKV_WIDTH

    @pl.when(c == 0)
    def _():
        carry[...] = jnp.zeros(carry.shape, F32)
        state[...] = jnp.zeros(state.shape, F32)

    flat = lambda ref: jnp.concatenate([ref[s] for s in range(n_seq)], axis=0)
    z = flat(z_ref)
    row = lax.broadcasted_iota(jnp.int32, z.shape, 0)
    prev = pltpu.roll(z, 1, axis=0)
    for s in range(n_seq):
        prev = jnp.where(row == s * seq_rows, carry[s, SUBLANES - 1:SUBLANES, :], prev)
        carry[s] = z[(s + 1) * seq_rows - SUBLANES:(s + 1) * seq_rows, :]
    zs_all = z + mu_ref[...] * (prev - z)
    if has_vres:
        vf_all = flat(vf_ref)
    vec = vec_ref[...]
    w0, a0, k_k, k_a, r_k, ln_g, ln_b, v0 = (vec[j:j + 1] for j in range(8))
    bd = bd_ref[...]
    masks = _wkv_masks(L)
    rowt = lax.broadcasted_iota(jnp.int32, (L, L), 0)
    colt = lax.broadcasted_iota(jnp.int32, (L, L), 1)
    tri = jnp.where(rowt >= colt, 1.0, 0.0).astype(BF16)
    n_pairs = RWKV_HEADS // 2
    S_now = [{(s, p): state[s * n_pairs + p] for s in range(n_seq) for p in range(n_pairs)}]

    def group(s0, s1):
        seqs = range(s0, s1)
        r0, r1 = s0 * seq_rows, s1 * seq_rows
        zs = zs_all[r0:r1]
        r = zs[:, 0:W]
        k = zs[:, W:2 * W]
        v = zs[:, 2 * W:3 * W]
        wa = zs[:, 3 * W:3 * W + LANES]
        gb = zs[:, 3 * W + LANES:ZR_COLS]
        lane_a = lax.broadcasted_iota(jnp.int32, wa.shape, 1)
        t1 = _dot(jnp.where(lane_a < RWKV_DECAY_LORA, jnp.tanh(wa), wa).astype(BF16), w2a2_ref[...])
        lane_g = lax.broadcasted_iota(jnp.int32, gb.shape, 1)
        t2 = _dot(jnp.where(lane_g < RWKV_GATE_LORA, _sigmoid(gb), gb).astype(BF16), g2v2_ref[...])
        yield
        xw = w0 + t1[:, :W]
        lw = -np.exp(-0.5).astype(np.float32) * _sigmoid(xw)
        a = _sigmoid(a0 + t1[:, W:])
        g = t2[:, :W]
        if has_vres:
            v = v + (vf_all[r0:r1] - v) * _sigmoid(v0 + t2[:, W:])
        else:
            for s in seqs:
                vout_ref[s] = v[(s - s0) * seq_rows:(s - s0 + 1) * seq_rows]
        kk = k * k_k
        kk = kk * jnp.minimum(lax.rsqrt(_mm_exact_rhs(kk * kk, bd)), 1e12)
        k = k * (1.0 + (a - 1.0) * k_a)
        yield
        n_chunks = seq_rows // L
        idx = [(s, ch, p) for s in seqs for ch in range(n_chunks) for p in range(n_pairs)]
        row0 = lambda s, ch: (s - s0) * seq_rows + ch * L
        cut = lambda x: [x[row0(s, ch):row0(s, ch) + L, p * LANES:(p + 1) * LANES] for s, ch, p in idx]
        res = yield from _wkv_prep(cut(r), cut(lw), cut(k), cut(v), cut(kk), cut(a), tri, masks)
        prep = dict(zip(idx, res))
        yield
        sp = [(s, p) for s in seqs for p in range(n_pairs)]
        S = {key: S_now[0][key] for key in sp}
        y_blk = {}
        for ch in range(n_chunks):
            y_s = {(s, p): _mm(prep[s, ch, p][0], S[s, p], NT, WKV_PASSES_STATE) + prep[s, ch, p][1]
                   for s, p in sp}
            for s in seqs:
                y_blk[s, ch] = jnp.concatenate([y_s[s, p] for p in range(n_pairs)], axis=1)
            S = {(s, p): S[s, p] * prep[s, ch, p][2] + _mm(S[s, p], prep[s, ch, p][3], NN, WKV_PASSES_STATE)
                 + prep[s, ch, p][4] for s, p in sp}
        S_now[0].update(S)
        y = jnp.concatenate([y_blk[s, ch] for s in seqs for ch in range(n_chunks)], axis=0)
        yield
        inv = 1.0 / HEAD_DIM
        mean = _mm_exact_rhs(y, bd) * inv
        yc = y - mean
        var = _mm_exact_rhs(yc * yc, bd) * inv
        yn = yc * lax.rsqrt(var + RWKV_LNX_EPS) * ln_g + ln_b
        bonus = _mm_exact_rhs(r * k * r_k, bd) * v
        out = ((yn + bonus) * g).astype(BF16)
        for s in seqs:
            y_ref[s] = out[(s - s0) * seq_rows:(s - s0 + 1) * seq_rows]

    per = n_seq // WKV_GROUPS
    gens = [group(gi * per, (gi + 1) * per) for gi in range(WKV_GROUPS)]
    alive = [True] * WKV_GROUPS
    tick = 0
    while any(alive):
        for gi in range(WKV_GROUPS):
            if alive[gi] and tick >= gi * WKV_GROUP_LEAD:
                try:
                    next(gens[gi])
                except StopIteration:
                    alive[gi] = False
        tick += 1
    for s in range(n_seq):
        for p in range(n_pairs):
            state[s * n_pairs + p] = S_now[0][s, p]


def _rwkv(zr, v_first, mu, vec8, w2a2, g2v2, bd64):
    B, T, _ = zr.shape
    L = WKV_CHUNK * WKV_STEP_CHUNKS
    ns = WKV_STEP_SEQS
    has_vres = v_first is not None
    const = lambda shape: pl.BlockSpec(shape, lambda b, c: tuple(0 for _ in shape))
    tok = lambda w: pl.BlockSpec((ns, L, w), lambda b, c: (b, c, 0))
    in_specs = [tok(ZR_COLS)]
    args = [zr]
    if has_vres:
        in_specs.append(tok(RWKV_WIDTH))
        args.append(v_first)
    in_specs += [const((1, ZR_COLS)), const((8, RWKV_WIDTH)), const((LANES, 2 * RWKV_WIDTH)),
                 const((2 * LANES, 2 * RWKV_WIDTH)), const((LANES, LANES))]
    args += [mu, vec8, w2a2, g2v2, bd64]
    out_specs = [tok(RWKV_WIDTH)]
    out_shape = [jax.ShapeDtypeStruct((B, T, RWKV_WIDTH), BF16)]
    if not has_vres:
        out_specs.append(tok(RWKV_WIDTH))
        out_shape.append(jax.ShapeDtypeStruct((B, T, RWKV_WIDTH), F32))
    outs = pl.pallas_call(
        functools.partial(_rwkv_kernel, has_vres),
        grid=(B // ns, T // L),
        in_specs=in_specs, out_specs=out_specs, out_shape=out_shape,
        scratch_shapes=[pltpu.VMEM((ns, SUBLANES, ZR_COLS), F32),
                        pltpu.VMEM((ns * (RWKV_HEADS // 2), LANES, LANES), F32)],
        compiler_params=_cparams(("arbitrary", "arbitrary")),
    )(*args)
    return (outs[0], v_first) if has_vres else (outs[0], outs[1])


def _attn_step(q_ref, k_ref, v_ref, m_sc, l_sc, acc_sc, masked):
    tq = q_ref.shape[1]
    tk = k_ref.shape[1]
    lane = lax.broadcasted_iota(jnp.int32, (tq, LANES), 1)
    if masked:
        rowi = lax.broadcasted_iota(jnp.int32, (tq, tk), 0)
        coli = lax.broadcasted_iota(jnp.int32, (tq, tk), 1)
        keep = coli <= rowi
    heads = range(MLA_HEADS)
    s = [_dot(q_ref[0, :, hd * LANES:(hd + 1) * LANES], k_ref[0, :, hd * LANES:(hd + 1) * LANES], NT)
         for hd in heads]
    if masked:
        s = [jnp.where(keep, x, NEG_INF) for x in s]
    m_prev = [m_sc[hd] for hd in heads]
    m_new = [jnp.maximum(m_prev[hd], jnp.max(s[hd], axis=-1, keepdims=True)) for hd in heads]
    alpha = [jnp.exp(m_prev[hd] - m_new[hd]) for hd in heads]
    p = [jnp.exp(s[hd] - jnp.concatenate([m_new[hd]] * (tk // LANES), axis=1)) for hd in heads]
    for hd in heads:
        l_sc[hd] = alpha[hd] * l_sc[hd] + jnp.sum(p[hd], axis=-1, keepdims=True)
        m_sc[hd] = m_new[hd]
    pv = [_dot(p[hd].astype(BF16), v_ref[0, :, (hd // 2) * LANES:(hd // 2 + 1) * LANES]) for hd in heads]
    first = lane < HEAD_DIM
    for pr in range(MLA_HEADS // 2):
        acc_sc[pr] = (acc_sc[pr] * jnp.where(first, alpha[2 * pr], alpha[2 * pr + 1])
                      + jnp.where(first, pv[2 * pr], pv[2 * pr + 1]))


def _attn_step_bounded(q_ref, k_ref, v_ref, cb_ref, l_sc, acc_sc, masked):
    tq = q_ref.shape[1]
    tk = k_ref.shape[1]
    lane = lax.broadcasted_iota(jnp.int32, (tq, LANES), 1)
    if masked:
        rowi = lax.broadcasted_iota(jnp.int32, (tq, tk), 0)
        coli = lax.broadcasted_iota(jnp.int32, (tq, tk), 1)
        keep = coli <= rowi
    ps = []
    for hd in range(MLA_HEADS):
        s = _dot(q_ref[0, :, hd * LANES:(hd + 1) * LANES], k_ref[0, :, hd * LANES:(hd + 1) * LANES], NT)
        c = cb_ref[0, hd:hd + 1, :]
        p = jnp.exp(s - jnp.concatenate([c] * (tk // LANES), axis=1))
        if masked:
            p = jnp.where(keep, p, 0.0)
        part = p[:, 0:LANES]
        for t in range(1, tk // LANES):
            part = part + p[:, t * LANES:(t + 1) * LANES]
        l_sc[hd] = l_sc[hd] + part
        ps.append(p.astype(BF16))
    first = lane < HEAD_DIM
    for pr in range(MLA_HEADS // 2):
        pv = _dot(jnp.concatenate(ps[2 * pr:2 * pr + 2], axis=0), v_ref[0, :, pr * LANES:(pr + 1) * LANES])
        acc_sc[pr] = acc_sc[pr] + jnp.where(first, pv[:tq], pv[tq:])


def _attn_finish(o_ref, l_sc, acc_sc, lane_partial):
    tq = o_ref.shape[1]
    lane = lax.broadcasted_iota(jnp.int32, (tq, LANES), 1)
    outs = []
    for pr in range(MLA_HEADS // 2):
        la, lb = l_sc[2 * pr], l_sc[2 * pr + 1]
        if lane_partial:
            la = jnp.sum(la, axis=-1, keepdims=True)
            lb = jnp.sum(lb, axis=-1, keepdims=True)
        outs.append(acc_sc[pr] / jnp.where(lane < HEAD_DIM, la, lb))
    o_ref[0] = jnp.concatenate(outs, axis=1).astype(BF16)


def _attn_kernel(ok_ref, qi_ref, kj_ref, q_ref, k_ref, v_ref, cb_ref, o_ref, m_sc, l_sc, acc_sc):
    i = qi_ref[pl.program_id(1)]
    j = kj_ref[pl.program_id(1)]
    bounded = ok_ref[pl.program_id(0)] == 1
    exact = jnp.logical_not(bounded)

    @pl.when(j == 0)
    def _():
        m_sc[...] = jnp.full(m_sc.shape, NEG_INF, F32)
        l_sc[...] = jnp.zeros(l_sc.shape, F32)
        acc_sc[...] = jnp.zeros(acc_sc.shape, F32)

    @pl.when(bounded & (j < i))
    def _():
        _attn_step_bounded(q_ref, k_ref, v_ref, cb_ref, l_sc, acc_sc, masked=False)

    @pl.when(bounded & (j == i))
    def _():
        _attn_step_bounded(q_ref, k_ref, v_ref, cb_ref, l_sc, acc_sc, masked=True)
        _attn_finish(o_ref, l_sc, acc_sc, lane_partial=True)

    @pl.when(exact & (j < i))
    def _():
        _attn_step(q_ref, k_ref, v_ref, m_sc, l_sc, acc_sc, masked=False)

    @pl.when(exact & (j == i))
    def _():
        _attn_step(q_ref, k_ref, v_ref, m_sc, l_sc, acc_sc, masked=True)
        _attn_finish(o_ref, l_sc, acc_sc, lane_partial=False)


def _attention(q, k, v, qk_gain):
    B, T, _ = q.shape
    nq = T // TQ
    gmax = jnp.max(jnp.abs(qk_gain), axis=1)
    c = gmax[0] * gmax[1] * (MLA_QK_DIM ** 0.5) * ATTN_BOUND_SLACK
    ok = jnp.broadcast_to((c <= ATTN_BOUND_MAX).astype(jnp.int32), (B,))
    cb = jnp.broadcast_to(c, (B, MLA_HEADS, LANES))
    pairs = [(i, j) for i in range(nq) for j in range(i + 1)]
    qi = jnp.asarray([p[0] for p in pairs], jnp.int32)
    kj = jnp.asarray([p[1] for p in pairs], jnp.int32)
    grid_spec = pltpu.PrefetchScalarGridSpec(
        num_scalar_prefetch=3,
        grid=(B, len(pairs)),
        in_specs=[pl.BlockSpec((1, TQ, MLA_HEADS * LANES), lambda b, t, ok, qi, kj: (b, qi[t], 0)),
                  pl.BlockSpec((1, TQ, MLA_HEADS * LANES), lambda b, t, ok, qi, kj: (b, kj[t], 0)),
                  pl.BlockSpec((1, TQ, MLA_HEADS * HEAD_DIM), lambda b, t, ok, qi, kj: (b, kj[t], 0)),
                  pl.BlockSpec((1, MLA_HEADS, LANES), lambda b, t, ok, qi, kj: (b, 0, 0))],
        out_specs=pl.BlockSpec((1, TQ, MLA_HEADS * HEAD_DIM), lambda b, t, ok, qi, kj: (b, qi[t], 0)),
        scratch_shapes=[pltpu.VMEM((MLA_HEADS, TQ, LANES), F32),
                        pltpu.VMEM((MLA_HEADS, TQ, LANES), F32),
                        pltpu.VMEM((MLA_HEADS // 2, TQ, LANES), F32)])
    return pl.pallas_call(
        _attn_kernel,
        grid_spec=grid_spec,
        out_shape=jax.ShapeDtypeStruct((B, T, MLA_HEADS * HEAD_DIM), BF16),
        compiler_params=_cparams(("arbitrary", "arbitrary")),
    )(ok, qi, kj, q, k, v, cb)


def _mixout_kernel(has_router, *refs):
    if has_router:
        (x_ref, yr_ref, yp_ref, ym_ref, wo_ref, mod_ref, gain_ref, rt_ref, tri_ref,
         xo_ref, h_ref, route_ref, cnt_ref, cnt_sc) = refs
    else:
        x_ref, yr_ref, yp_ref, ym_ref, wo_ref, mod_ref, gain_ref, xo_ref, h_ref = refs
    mod = mod_ref[0]
    o1 = RWKV_WIDTH
    o2 = RWKV_WIDTH + POOL_WIDTH
    mix = (_dot(yr_ref[0], wo_ref[0:o1, :]) + _dot(yp_ref[0], wo_ref[o1:o2, :])
           + _dot(ym_ref[0], wo_ref[o2:, :]))
    x = x_ref[0] + mod[2:3] * mix
    xo_ref[0] = x
    h = _rms(x) * gain_ref[...] * (1.0 + mod[4:5]) + mod[3:4]
    if not has_router:
        h_ref[0] = h.astype(BF16)
    else:
        hp = _pack_bf16_pairs(h)
        for ck in range(ROW_CHUNKS):
            h_ref[ck, 0] = hp[:, ck * PACK_CHUNK_W:(ck + 1) * PACK_CHUNK_W]
        logits = _dot(h.astype(BF16), rt_ref[...])
        lane = lax.broadcasted_iota(jnp.int32, logits.shape, 1).astype(F32)
        lg = jnp.where(lane < N_EXPERTS, logits, -jnp.inf)
        m1 = jnp.max(lg, axis=-1, keepdims=True)
        i1 = jnp.min(jnp.where(lg == m1, lane, float(LANES)), axis=-1, keepdims=True)
        lg2 = jnp.where(lane == i1, -jnp.inf, lg)
        m2 = jnp.max(lg2, axis=-1, keepdims=True)
        i2 = jnp.min(jnp.where(lg2 == m2, lane, float(LANES)), axis=-1, keepdims=True)
        e2 = jnp.exp(m2 - m1)
        g1 = 1.0 / (1.0 + e2)
        g2 = e2 / (1.0 + e2)
        first = (pl.program_id(0) == 0) & (pl.program_id(1) == 0)

        @pl.when(first)
        def _():
            cnt_sc[...] = jnp.zeros(cnt_sc.shape, F32)

        hit1 = lane == i1
        hit2 = lane == i2
        onehot = jnp.where(hit1 | hit2, 1.0, 0.0)
        prefix = _dot(tri_ref[...], onehot.astype(BF16)) + cnt_sc[0:1, :]
        r1 = jnp.sum(jnp.where(hit1, prefix, 0.0), axis=-1, keepdims=True)
        r2 = jnp.sum(jnp.where(hit2, prefix, 0.0), axis=-1, keepdims=True)
        cnt_sc[...] = cnt_sc[...] + jnp.sum(onehot, axis=0, keepdims=True)
        cnt_ref[...] = cnt_sc[...]
        vals = (i1, i2, g1, g2, r1, r2)
        route = jnp.zeros(logits.shape, F32)
        for pos, val in enumerate(vals):
            route = jnp.where(lane == pos, val, route)
        route_ref[0] = route


def _mixout(x, yr, yp, ym, wo, mod_l, gain, router_p):
    B, T, _ = x.shape
    tm = TM_OUT
    has_router = router_p is not None
    const = lambda shape: pl.BlockSpec(shape, lambda b, i: tuple(0 for _ in shape))
    tok = lambda w: pl.BlockSpec((1, tm, w), lambda b, i: (b, i, 0))
    in_specs = [tok(D_MODEL), tok(RWKV_WIDTH), tok(POOL_WIDTH), tok(MLA_HEADS * HEAD_DIM),
                const((D_MODEL, D_MODEL)), pl.BlockSpec((1, 6, D_MODEL), lambda b, i: (b, 0, 0)),
                const((1, D_MODEL))]
    args = [x, yr, yp, ym, wo, mod_l, gain]
    out_specs = [tok(D_MODEL), tok(D_MODEL)]
    out_shape = [jax.ShapeDtypeStruct((B, T, D_MODEL), F32), jax.ShapeDtypeStruct((B, T, D_MODEL), BF16)]
    scratch = []
    if has_router:
        out_specs[1] = pl.BlockSpec((ROW_CHUNKS, 1, tm, PACK_CHUNK_W), lambda b, i: (0, b, i, 0))
        out_shape[1] = jax.ShapeDtypeStruct((ROW_CHUNKS, B, T, PACK_CHUNK_W), jnp.uint32)
        ids = np.arange(tm)
        tri = jnp.asarray(ids[:, None] > ids[None, :], BF16)
        in_specs += [const((D_MODEL, LANES)), const((tm, tm))]
        args += [router_p, tri]
        out_specs += [tok(LANES), const((SUBLANES, LANES))]
        out_shape += [jax.ShapeDtypeStruct((B, T, LANES), F32),
                      jax.ShapeDtypeStruct((SUBLANES, LANES), F32)]
        scratch = [pltpu.VMEM((SUBLANES, LANES), F32)]
    return pl.pallas_call(
        functools.partial(_mixout_kernel, has_router),
        grid=(B, T // tm),
        in_specs=in_specs, out_specs=out_specs, out_shape=out_shape, scratch_shapes=scratch,
        compiler_params=_cparams(("arbitrary", "arbitrary")),
    )(*args)


def _ffn_kernel(h_ref, wg_ref, wu_ref, wo_ref, x_ref, mod_ref, o_ref, acc):
    j = pl.program_id(1)

    @pl.when(j == 0)
    def _():
        acc[...] = jnp.zeros(acc.shape, F32)

    h = h_ref[...]
    gg = _dot(h, wg_ref[...])
    uu = _dot(h, wu_ref[...])
    acc[...] += _dot((_silu(gg) * uu).astype(BF16), wo_ref[...])

    @pl.when(j == pl.num_programs(1) - 1)
    def _():
        o_ref[...] = x_ref[...] + mod_ref[0][5:6] * acc[...]


def _ffn(h2, w_in, w_out, x, mod_l):
    N = h2.shape[0]
    T = N // mod_l.shape[0]
    tm, tf = TM_FFN, TF_FFN
    nf = D_FF // tf
    per_b = T // tm
    return pl.pallas_call(
        _ffn_kernel,
        grid=(N // tm, nf),
        in_specs=[pl.BlockSpec((tm, D_MODEL), lambda i, j: (i, 0)),
                  pl.BlockSpec((D_MODEL, tf), lambda i, j: (0, j)),
                  pl.BlockSpec((D_MODEL, tf), lambda i, j: (0, j + nf)),
                  pl.BlockSpec((tf, D_MODEL), lambda i, j: (j, 0)),
                  pl.BlockSpec((tm, D_MODEL), lambda i, j: (i, 0)),
                  pl.BlockSpec((1, 6, D_MODEL), lambda i, j: (i // per_b, 0, 0))],
        out_specs=pl.BlockSpec((tm, D_MODEL), lambda i, j: (i, 0)),
        out_shape=jax.ShapeDtypeStruct((N, D_MODEL), F32),
        scratch_shapes=[pltpu.VMEM((tm, D_MODEL), F32)],
        compiler_params=_cparams(("arbitrary", "arbitrary")),
    )(h2, w_in, w_in, w_out, x, mod_l)


def _moe_kernel(be_ref, nv_ref, last_ref, x_ref, wg_ref, wu_ref, wo_ref, o_ref, acc, xm):
    i = pl.program_id(0)
    j = pl.program_id(1)

    @pl.when(i <= last_ref[0])
    def _():
        @pl.when(j == 0)
        def _():
            acc[...] = jnp.zeros(acc.shape, F32)
            row = lax.broadcasted_iota(jnp.int32, (xm.shape[0], 1), 0)
            xp = jnp.concatenate([x_ref[ck] for ck in range(ROW_CHUNKS)], axis=1)
            xp = jnp.where(row < nv_ref[i], xp, jnp.uint32(0))
            xm[...] = _unpack_bf16_pairs(xp)

        def swiglu_rows(n_rows):
            x = xm[0:n_rows, :]
            gg = _dot(x, wg_ref[0].astype(BF16))
            uu = _dot(x, wu_ref[0].astype(BF16))
            acc[0:n_rows, :] += _dot((_silu(gg) * uu).astype(BF16), wo_ref[0].astype(BF16))

        half = xm.shape[0] // 2

        @pl.when(nv_ref[i] > half)
        def _():
            swiglu_rows(xm.shape[0])

        @pl.when(nv_ref[i] <= half)
        def _():
            swiglu_rows(half)

        @pl.when(j == pl.num_programs(1) - 1)
        def _():
            yp = _pack_bf16_pairs(acc[...])
            for ck in range(ROW_CHUNKS):
                o_ref[ck] = yp[:, ck * PACK_CHUNK_W:(ck + 1) * PACK_CHUNK_W]


def _moe_experts(xs, w_in, w_out, block_exp, n_valid, last_blk):
    n_rows = xs.shape[1]
    tm, tf = MOE_BLOCK, TF_MOE
    nf = D_FF_EXPERT // tf
    blk = lambda i, last: jnp.minimum(i, last[0])
    chunk = lambda i, j, last: jnp.where(i <= last[0], j, nf - 1)
    grid_spec = pltpu.PrefetchScalarGridSpec(
        num_scalar_prefetch=3,
        grid=(n_rows // tm, nf),
        in_specs=[pl.BlockSpec((ROW_CHUNKS, tm, PACK_CHUNK_W),
                               lambda i, j, be, nv, last: (0, blk(i, last), 0)),
                  pl.BlockSpec((1, D_MODEL, tf),
                               lambda i, j, be, nv, last: (be[blk(i, last)], 0, chunk(i, j, last))),
                  pl.BlockSpec((1, D_MODEL, tf),
                               lambda i, j, be, nv, last: (be[blk(i, last)], 0, chunk(i, j, last) + nf)),
                  pl.BlockSpec((1, tf, D_MODEL),
                               lambda i, j, be, nv, last: (be[blk(i, last)], chunk(i, j, last), 0))],
        out_specs=pl.BlockSpec((ROW_CHUNKS, tm, PACK_CHUNK_W), lambda i, j, be, nv, last: (0, blk(i, last), 0)),
        scratch_shapes=[pltpu.VMEM((tm, D_MODEL), F32), pltpu.VMEM((tm, D_MODEL), BF16)])
    return pl.pallas_call(
        _moe_kernel,
        grid_spec=grid_spec,
        out_shape=jax.ShapeDtypeStruct((ROW_CHUNKS, n_rows, PACK_CHUNK_W), jnp.uint32),
        compiler_params=_cparams(("arbitrary", "arbitrary")),
    )(block_exp, n_valid, last_blk, xs, w_in, w_in, w_out)


def _sc_mesh():
    return plsc.VectorSubcoreMesh(core_axis_name="c", subcore_axis_name="s")


def _sc_scatter_rows(x, dest, n_rows):
    N, D = x.shape
    K = dest.shape[0]
    win = SC_WINDOW

    @pl.kernel(out_type=jax.ShapeDtypeStruct((n_rows, D), x.dtype), mesh=_sc_mesh(), scratch_types=[])
    def scatter(x_hbm, d_hbm, o_hbm):
        def body(x_vmem, *idx_vmem):
            for iv in idx_vmem:
                pltpu.sync_copy(x_vmem, o_hbm.at[iv.at[0]])

        pltpu.emit_pipeline(
            body,
            grid=(N // win,),
            in_specs=[pl.BlockSpec((win, D), lambda i: (i, 0))]
            + [pl.BlockSpec((1, win), functools.partial(lambda k, i: (k, i), k)) for k in range(K)],
            out_specs=[],
            core_axis_name=("c", "s"),
            dimension_semantics=(pltpu.PARALLEL,),
        )(x_hbm, *([d_hbm] * K))

    return scatter(x, dest)


def _sc_gather_rows(x, idx):
    n = idx.shape[0]
    D = x.shape[1]
    win = SC_WINDOW

    @pl.kernel(out_type=jax.ShapeDtypeStruct((n, D), x.dtype), mesh=_sc_mesh(), scratch_types=[])
    def gather(x_hbm, i_hbm, o_hbm):
        def body(i_vmem, o_vmem):
            pltpu.sync_copy(x_hbm.at[i_vmem.at[0]], o_vmem)

        pltpu.emit_pipeline(
            body,
            grid=(n // win,),
            in_specs=[pl.BlockSpec((1, win), lambda i: (0, i))],
            out_specs=[pl.BlockSpec((win, D), lambda i: (i, 0))],
            core_axis_name=("c", "s"),
            dimension_semantics=(pltpu.PARALLEL,),
        )(i_hbm, o_hbm)

    return gather(x, idx.reshape(1, n))


def _combine_kernel(x_ref, ya_ref, yb_ref, route_ref, mod_ref, o_ref):
    rt = route_ref[...]
    ya = _unpack_bf16_pairs(jnp.concatenate([ya_ref[0, ck] for ck in range(ROW_CHUNKS)], axis=1))
    yb = _unpack_bf16_pairs(jnp.concatenate([yb_ref[0, ck] for ck in range(ROW_CHUNKS)], axis=1))
    f = rt[:, 2:3] * ya.astype(F32) + rt[:, 3:4] * yb.astype(F32)
    o_ref[...] = x_ref[...] + mod_ref[0][5:6] * f


def _combine(x, y2, route, mod_l):
    N = x.shape[0]
    T = N // mod_l.shape[0]
    tm = 1024
    per_b = T // tm
    tok = pl.BlockSpec((tm, D_MODEL), lambda i: (i, 0))
    slot = lambda k: pl.BlockSpec((1, ROW_CHUNKS, tm, PACK_CHUNK_W), lambda i: (k, 0, i, 0))
    return pl.pallas_call(
        _combine_kernel,
        grid=(N // tm,),
        in_specs=[tok, slot(0), slot(1),
                  pl.BlockSpec((tm, LANES), lambda i: (i, 0)),
                  pl.BlockSpec((1, 6, D_MODEL), lambda i: (i // per_b, 0, 0))],
        out_specs=tok,
        out_shape=jax.ShapeDtypeStruct((N, D_MODEL), F32),
        compiler_params=_cparams(("arbitrary",)),
    )(x, y2, y2, route, mod_l)


def _moe(h2, route, counts, w_in, w_out, x, mod_l):
    N = x.shape[0]
    blk = MOE_BLOCK
    cnt = counts[0, :N_EXPERTS].astype(jnp.int32)
    padded = (cnt + blk - 1) // blk * blk
    pend = jnp.cumsum(padded)
    pstart = pend - padded
    e = route[:, 0:TOP_K].astype(jnp.int32)
    rank = route[:, 2 * TOP_K:3 * TOP_K].astype(jnp.int32)
    dest = (jnp.take(pstart, e) + rank).T
    n_blocks = N * TOP_K // blk + N_EXPERTS
    bstart = jnp.arange(n_blocks, dtype=jnp.int32) * blk
    block_exp = jnp.minimum(jnp.sum((bstart[:, None] >= pend[None, :]).astype(jnp.int32), axis=1),
                            N_EXPERTS - 1)
    n_valid = jnp.clip(cnt[block_exp] - (bstart - pstart[block_exp]), 0, blk).astype(jnp.int32)
    n_rows = n_blocks * blk
    dest_ck = dest[:, None, :] + (jnp.arange(ROW_CHUNKS, dtype=jnp.int32) * n_rows)[None, :, None]
    xs = _sc_scatter_rows(h2, dest_ck.reshape(TOP_K, ROW_CHUNKS * N), ROW_CHUNKS * n_rows)
    last_blk = (pend[-1:] // blk - 1).astype(jnp.int32)
    yb = _moe_experts(xs.reshape(ROW_CHUNKS, n_rows, PACK_CHUNK_W), w_in, w_out, block_exp, n_valid,
                      last_blk)
    y2 = _sc_gather_rows(yb.reshape(ROW_CHUNKS * n_rows, PACK_CHUNK_W), dest_ck.reshape(-1))
    return _combine(x, y2.reshape(TOP_K, ROW_CHUNKS, N, PACK_CHUNK_W), route, mod_l)


def _layout_w_in(w, has_vres):
    W = RWKV_WIDTH
    off_gd = 3 * W + RWKV_DECAY_LORA + RWKV_ICLR_LORA
    off_pool = off_gd + RWKV_GATE_LORA
    off_q = off_pool + POOL_WIDTH
    off_kv = off_q + MLA_Q_LORA
    off_kr = off_kv + MLA_KV_LORA
    n_base = off_kr + MLA_QK_ROPE
    d = w.shape[0]
    zeros = lambda n: jnp.zeros((d, n), w.dtype)
    vd = w[:, n_base:n_base + RWKV_VRES_LORA] if has_vres else zeros(RWKV_VRES_LORA)
    cols = [w[:, :off_gd], w[:, off_gd:off_pool], vd, zeros(ZR_COLS - off_pool - RWKV_VRES_LORA),
            w[:, off_pool:off_q], w[:, off_q:off_kv], w[:, off_kv:off_kr],
            zeros(MLA_QK_NOPE), w[:, off_kr:n_base], zeros(LANES - MLA_QK_DIM)]
    return jnp.concatenate(cols, axis=1).astype(BF16)


def _pad_heads(w, per_head, keep_from, keep_n):
    K = w.shape[0]
    wh = w.reshape(K, MLA_HEADS, per_head)[:, :, keep_from:keep_from + keep_n]
    wh = jnp.pad(wh, ((0, 0), (0, 0), (0, LANES - keep_n)))
    return wh.reshape(K, MLA_HEADS * LANES)


def kernel(x, c, positions, w_ada, b_ada, norm_gain, w_in_first, w_in_rest, mu_shift, mu_shift_v,
           rwkv_vec, rwkv_v0, rwkv_w2, rwkv_a2, rwkv_g2, rwkv_v2, pool_w, pool_scale,
           mla_q_lat_gain, mla_kv_lat_gain, mla_wq_up, mla_wkv_up, mla_qk_gain, w_out, ffn_w_in,
           ffn_w_out, moe_router, moe_w_in, moe_w_out):
    B, T, D = x.shape
    depth = w_ada.shape[0]
    assert D == D_MODEL and B % WKV_STEP_SEQS == 0 and D_FF_EXPERT % TF_MOE == 0
    assert T % max(TM_MIX, TM_OUT, TQ, WKV_CHUNK * WKV_STEP_CHUNKS) == 0
    assert (B * T) % max(TM_FFN, MOE_BLOCK) == 0 and (B * T * TOP_K) % MOE_BLOCK == 0
    W = RWKV_WIDTH
    mod = _adaln(c, w_ada, b_ada).reshape(depth, B, 6, D)
    inv_freq = ROPE_BASE ** (-jnp.arange(0, MLA_QK_ROPE, 2, dtype=F32) / MLA_QK_ROPE)
    cosf, sinf = _rope_tables(positions, inv_freq)
    hid = np.arange(LANES) // HEAD_DIM
    bd64 = jnp.asarray(hid[:, None] == hid[None, :], BF16)
    bd128 = jnp.ones((LANES, LANES), BF16)

    v_first = None
    for l in range(depth):
        has_vres = l > 0
        mod_l = mod[l]
        win = _layout_w_in(w_in_first if l == 0 else w_in_rest[l - 1], has_vres)
        poolw = jax.scipy.linalg.block_diag(*[pool_w[l, g] for g in range(len(POOL_WINDOWS))]).astype(BF16)
        wq = _pad_heads(mla_wq_up[l], MLA_QK_DIM, 0, MLA_QK_DIM).astype(BF16)
        wk = _pad_heads(mla_wkv_up[l], MLA_QK_NOPE + HEAD_DIM, 0, MLA_QK_NOPE).astype(BF16)
        wv = mla_wkv_up[l].reshape(MLA_KV_LORA, MLA_HEADS, MLA_QK_NOPE + HEAD_DIM)[:, :, MLA_QK_NOPE:]
        wv = wv.reshape(MLA_KV_LORA, MLA_HEADS * HEAD_DIM).astype(BF16)
        qkg = jnp.tile(jnp.pad(mla_qk_gain[l], ((0, 0), (0, LANES - MLA_QK_DIM))), (1, MLA_HEADS))
        zr, y_pool, q, k, v = _mixin(
            x, mod_l, norm_gain[l, 0].reshape(1, D), win, cosf, sinf, poolw,
            pool_scale[l].reshape(1, -1), mla_q_lat_gain[l].reshape(1, -1),
            mla_kv_lat_gain[l].reshape(1, -1), wq, wk, wv, qkg, bd128)

        pad_mu = ZR_COLS - mu_shift.shape[1] - RWKV_VRES_LORA
        mu_v = mu_shift_v[l - 1] if has_vres else jnp.zeros((RWKV_VRES_LORA,), F32)
        mu = jnp.concatenate([mu_shift[l], mu_v, jnp.zeros((pad_mu,), F32)]).reshape(1, ZR_COLS)
        v0 = rwkv_v0[l - 1] if has_vres else jnp.zeros((W,), F32)
        vec8 = jnp.concatenate([rwkv_vec[l], v0[None]], axis=0)
        w2a2 = jax.scipy.linalg.block_diag(rwkv_w2[l], rwkv_a2[l]).astype(BF16)
        g2 = jnp.pad(rwkv_g2[l], ((0, 2 * LANES - RWKV_GATE_LORA), (0, 0)))
        if has_vres:
            v2 = jnp.pad(rwkv_v2[l - 1], ((RWKV_GATE_LORA, 2 * LANES - RWKV_GATE_LORA - RWKV_VRES_LORA), (0, 0)))
        else:
            v2 = jnp.zeros((2 * LANES, W), F32)
        g2v2 = jnp.concatenate([g2, v2], axis=1).astype(BF16)
        y_rwkv, v_first = _rwkv(zr, v_first, mu, vec8, w2a2, g2v2, bd64)

        y_mla = _attention(q, k, v, mla_qk_gain[l])

        is_moe = (l % 2 == 1)
        router_p = None
        if is_moe:
            router_p = jnp.pad(moe_router[l // 2], ((0, 0), (0, LANES - N_EXPERTS))).astype(BF16)
        outs = _mixout(x, y_rwkv, y_pool, y_mla, w_out[l].astype(BF16), mod_l,
                       norm_gain[l, 1].reshape(1, D), router_p)
        x_mid, h2 = outs[0], outs[1]
        xf = x_mid.reshape(B * T, D)
        if is_moe:
            xo = _moe(h2.reshape(ROW_CHUNKS * B * T, PACK_CHUNK_W), outs[2].reshape(B * T, LANES), outs[3],
                      moe_w_in[l // 2], moe_w_out[l // 2], xf, mod_l)
        else:
            xo = _ffn(h2.reshape(B * T, D), ffn_w_in[l // 2].astype(BF16), ffn_w_out[l // 2].astype(BF16),
                      xf, mod_l)
        x = xo.reshape(B, T, D)
    return x
```

```python
import functools

import numpy as np
import jax
import jax.numpy as jnp
from jax import lax
from jax.experimental import pallas as pl
from jax.experimental.pallas import tpu as pltpu
from jax.experimental.pallas import tpu_sc as plsc

F32 = jnp.float32
BF16 = jnp.bfloat16

D_MODEL = 1024
HEAD_DIM = 64
RWKV_WIDTH = 512
RWKV_HEADS = RWKV_WIDTH // HEAD_DIM
POOL_WIDTH = 256
POOL_WINDOWS = (2, 4, 8, 16)
POOL_HALO = 16
MLA_HEADS = 4
MLA_QK_NOPE = 64
MLA_QK_ROPE = 32
MLA_QK_DIM = MLA_QK_NOPE + MLA_QK_ROPE
MLA_Q_LORA = 256
MLA_KV_LORA = 128
ROPE_BASE = 10000.0
RWKV_DECAY_LORA = 64
RWKV_ICLR_LORA = 64
RWKV_VRES_LORA = 32
RWKV_GATE_LORA = 160
RWKV_LNX_EPS = 64e-5
D_FF = 2816
N_EXPERTS = 8
TOP_K = 2
D_FF_EXPERT = 3584
NORM_EPS = 1e-6
NEG_INF = -1e30

LANES = 128
SUBLANES = 8
VMEM_LIMIT = 56 * 1024 * 1024

ZR_COLS = 3 * RWKV_WIDTH + (RWKV_DECAY_LORA + RWKV_ICLR_LORA) + 2 * LANES
Z_POOL_OFF = ZR_COLS
Z_QLAT_OFF = Z_POOL_OFF + POOL_WIDTH
Z_KVLAT_OFF = Z_QLAT_OFF + MLA_Q_LORA
Z_KROPE_OFF = Z_KVLAT_OFF + MLA_KV_LORA
Z_COLS = Z_KROPE_OFF + LANES

MIXIN_PIECE_W = 512
TM_MIX = 512
TM_OUT = 1024
WKV_CHUNK = 64
TQ = 512
ATTN_BOUND_SLACK = 1.02
ATTN_BOUND_MAX = 40.0
TM_FFN = 1024
TF_FFN = D_FF // 2
MOE_BLOCK = 1024
MXU_TILE = 256
TF_MOE = 2 * MXU_TILE
SC_WINDOW = 128
ROW_CHUNKS = 4
ROW_CHUNK_W = D_MODEL // ROW_CHUNKS
PACK_CHUNK_W = ROW_CHUNK_W // 2

SEGSUM_SPLITS = 1

NN = (((1,), (0,)), ((), ()))
NT = (((1,), (1,)), ((), ()))


def _dot(a, b, dims=NN):
    return lax.dot_general(a, b, dims, preferred_element_type=F32)


def _split2(a):
    hi = a.astype(BF16)
    lo = (a - hi.astype(F32)).astype(BF16)
    return hi, lo


def _mm(a, b, dims=NN, passes=3):
    if passes == 1:
        return _dot(a.astype(BF16), b.astype(BF16), dims)
    ah, al = _split2(a)
    bh, bl = _split2(b)
    return _dot(ah, bh, dims) + (_dot(ah, bl, dims) + _dot(al, bh, dims))


def _mm_exact_rhs(a, b_bf16, splits=SEGSUM_SPLITS):
    m, w = a.shape
    nb = w // LANES
    stacked = jnp.concatenate([a[:, i * LANES:(i + 1) * LANES] for i in range(nb)], axis=0)
    out = None
    rem = stacked
    for s in range(splits):
        part = rem.astype(BF16)
        term = _dot(part, b_bf16)
        out = term if out is None else out + term
        if s + 1 < splits:
            rem = rem - part.astype(F32)
    return jnp.concatenate([out[i * m:(i + 1) * m] for i in range(nb)], axis=1)


def _pack_bf16_pairs(h):
    w = h.shape[1] // 2
    lo = lax.bitcast_convert_type(h[:, :w].astype(BF16).astype(F32), jnp.uint32)
    hi = lax.bitcast_convert_type(h[:, w:].astype(BF16).astype(F32), jnp.uint32)
    return (lo >> 16) | (hi & jnp.uint32(0xFFFF0000))


def _unpack_bf16_pairs(p):
    lo = lax.bitcast_convert_type(p << 16, F32)
    hi = lax.bitcast_convert_type(p & jnp.uint32(0xFFFF0000), F32)
    return jnp.concatenate([lo, hi], axis=1).astype(BF16)


def _sigmoid(x):
    return 1.0 / (1.0 + jnp.exp(-x))


def _silu(x):
    return x * _sigmoid(x)


def _rms(x, eps=NORM_EPS):
    return x * lax.rsqrt(jnp.mean(x * x, axis=-1, keepdims=True) + eps)


def _cparams(sem):
    return pltpu.CompilerParams(dimension_semantics=sem, vmem_limit_bytes=VMEM_LIMIT)


def _adaln_kernel(c_ref, w_ref, b_ref, o_ref):
    ca = _silu(c_ref[...])
    o_ref[0] = _mm(ca, w_ref[0]) + b_ref[0]


def _adaln(c, w_ada, b_ada):
    L = w_ada.shape[0]
    B = c.shape[0]
    n = w_ada.shape[2] // D_MODEL
    return pl.pallas_call(
        _adaln_kernel,
        grid=(L, n),
        in_specs=[pl.BlockSpec((B, D_MODEL), lambda l, j: (0, 0)),
                  pl.BlockSpec((1, D_MODEL, D_MODEL), lambda l, j: (l, 0, j)),
                  pl.BlockSpec((1, 1, D_MODEL), lambda l, j: (l, 0, j))],
        out_specs=pl.BlockSpec((1, B, D_MODEL), lambda l, j: (l, 0, j)),
        out_shape=jax.ShapeDtypeStruct((L, B, n * D_MODEL), F32),
        compiler_params=_cparams(("arbitrary", "arbitrary")),
    )(c, w_ada, b_ada.reshape(L, 1, -1))


def _rope(x, cosf, sinf, lane):
    up = pltpu.roll(x, LANES - MLA_QK_ROPE // 2, axis=1)
    dn = pltpu.roll(x, MLA_QK_ROPE // 2, axis=1)
    rot = jnp.where(lane < MLA_QK_NOPE + MLA_QK_ROPE // 2, -up, dn)
    return x * cosf + rot * sinf


def _rope_kernel(pos_ref, freq_ref, cos_ref, sin_ref):
    ang = pos_ref[0].astype(F32) * freq_ref[...]
    cos_ref[0] = jnp.cos(ang)
    sin_ref[0] = jnp.sin(ang)


def _rope_tables(positions, inv_freq):
    B, T = positions.shape
    nf = inv_freq.shape[0]
    per_row = LANES // nf
    pos_p = jnp.broadcast_to(positions[:, :, None], (B, T, nf)).reshape(B, T // per_row, LANES)
    freq_p = jnp.tile(inv_freq, per_row).reshape(1, LANES)
    blk = pl.BlockSpec((1, T // per_row, LANES), lambda b: (b, 0, 0))
    cos_p, sin_p = pl.pallas_call(
        _rope_kernel,
        grid=(B,),
        in_specs=[blk, pl.BlockSpec((1, LANES), lambda b: (0, 0))],
        out_specs=[blk, blk],
        out_shape=[jax.ShapeDtypeStruct((B, T // per_row, LANES), F32)] * 2,
        compiler_params=_cparams(("arbitrary",)),
    )(pos_p, freq_p)

    def spread(tab, fill):
        t16 = tab.reshape(B, T, nf)
        return jnp.concatenate([jnp.full((B, T, MLA_QK_NOPE), fill, F32), t16, t16,
                                jnp.full((B, T, LANES - MLA_QK_DIM), fill, F32)], axis=-1)

    return spread(cos_p, 1.0), spread(sin_p, 0.0)


def _mixin_kernel(x_ref, mod_ref, gain_ref, win_ref, cos_ref, sin_ref, poolw_ref, pools_ref,
                  qg_ref, kvg_ref, wq_ref, wk_ref, wv_ref, qkg_ref, bd_ref, mu_ref,
                  zr_ref, yp_ref, q_ref, k_ref, v_ref, ubuf, zlast):
    i = pl.program_id(1)
    tm = x_ref.shape[1]
    x = x_ref[0]
    mod = mod_ref[0]
    @pl.when(i == 0)
    def _():
        ubuf[0:POOL_HALO, :] = jnp.zeros((POOL_HALO, POOL_WIDTH), F32)
        zlast[...] = jnp.zeros(zlast.shape, F32)

    h = (_rms(x) * gain_ref[...] * (1.0 + mod[1:2]) + mod[0:1]).astype(BF16)
    zb = _dot(h, win_ref[:, ZR_COLS:])
    zcol = lambda off, w: zb[:, off - ZR_COLS:off - ZR_COLS + w]
    def project_piece(n):
        cols = slice(n * MIXIN_PIECE_W, min((n + 1) * MIXIN_PIECE_W, ZR_COLS))
        zp = _dot(h, win_ref[:, cols])
        row = lax.broadcasted_iota(jnp.int32, zp.shape, 0)
        prev = jnp.where(row == 0, zlast[SUBLANES - 1:SUBLANES, cols], pltpu.roll(zp, 1, axis=0))
        zlast[:, cols] = zp[tm - SUBLANES:, :]
        zr_ref[0, :, cols] = zp + mu_ref[:, cols] * (prev - zp)

    project_piece(0)
    u = zcol(Z_POOL_OFF, POOL_WIDTH)
    ubuf[POOL_HALO:, :] = u
    ue = ubuf[...]
    s2 = ue + pltpu.roll(ue, 1, axis=0)
    s4 = s2 + pltpu.roll(s2, 2, axis=0)
    s8 = s4 + pltpu.roll(s4, 4, axis=0)
    s16 = s8 + pltpu.roll(s8, 8, axis=0)
    ubuf[0:POOL_HALO, :] = u[tm - POOL_HALO:, :]
    lane_p = lax.broadcasted_iota(jnp.int32, (tm, POOL_WIDTH), 1)
    grp = lane_p // (POOL_WIDTH // len(POOL_WINDOWS))
    win_sum = jnp.where(grp == 0, s2[POOL_HALO:], jnp.where(grp == 1, s4[POOL_HALO:],
                        jnp.where(grp == 2, s8[POOL_HALO:], s16[POOL_HALO:])))
    win = jnp.where(grp == 0, 2, jnp.where(grp == 1, 4, jnp.where(grp == 2, 8, 16)))
    t_abs = i * tm + lax.broadcasted_iota(jnp.int32, (tm, POOL_WIDTH), 0)
    cnt = jnp.minimum(t_abs + 1, win).astype(F32)
    p = win_sum / cnt - u
    yp = _dot(p.astype(BF16), poolw_ref[...]) * pools_ref[...]
    yp_ref[0] = yp.astype(BF16)
    project_piece(1)

    lane = lax.broadcasted_iota(jnp.int32, (tm, LANES), 1)
    cosf = cos_ref[0]
    sinf = sin_ref[0]

    q_lat = zcol(Z_QLAT_OFF, MLA_Q_LORA)
    kv_lat = zcol(Z_KVLAT_OFF, MLA_KV_LORA)
    k_rope = zcol(Z_KROPE_OFF, LANES)
    qn = (_rms(q_lat) * qg_ref[...]).astype(BF16)
    kvn = (_rms(kv_lat) * kvg_ref[...]).astype(BF16)
    q = _dot(qn, wq_ref[...])
    kx = _dot(kvn, wk_ref[...])
    v = _dot(kvn, wv_ref[...])
    v_ref[0] = v.astype(BF16)
    project_piece(2)
    k_pe = _rope(k_rope, cosf, sinf, lane)
    qs, ks = [], []
    for hd in range(MLA_HEADS):
        sl = slice(hd * LANES, (hd + 1) * LANES)
        qs.append(_rope(q[:, sl], cosf, sinf, lane))
        ks.append(kx[:, sl] + k_pe)
    q = jnp.concatenate(qs, axis=1)
    k = jnp.concatenate(ks, axis=1)
    project_piece(3)
    qss = _mm_exact_rhs(q * q, bd_ref[...]) * (1.0 / MLA_QK_DIM)
    kss = _mm_exact_rhs(k * k, bd_ref[...]) * (1.0 / MLA_QK_DIM)
    qkg = qkg_ref[...]
    q = q * lax.rsqrt(qss + NORM_EPS) * qkg[0:1] * (MLA_QK_DIM ** -0.5)
    k = k * lax.rsqrt(kss + NORM_EPS) * qkg[1:2]
    q_ref[0] = q.astype(BF16)
    k_ref[0] = k.astype(BF16)


def _mixin(x, mod_l, gain, win, cosf, sinf, poolw, pools, qg, kvg, wq, wk, wv, qkg, bd128, mu):
    B, T, _ = x.shape
    tm = TM_MIX
    const = lambda shape: pl.BlockSpec(shape, lambda b, i: tuple(0 for _ in shape))
    tok = lambda w: pl.BlockSpec((1, tm, w), lambda b, i: (b, i, 0))
    return pl.pallas_call(
        _mixin_kernel,
        grid=(B, T // tm),
        in_specs=[tok(D_MODEL),
                  pl.BlockSpec((1, 6, D_MODEL), lambda b, i: (b, 0, 0)),
                  const((1, D_MODEL)), const((D_MODEL, Z_COLS)),
                  tok(LANES), tok(LANES),
                  const((POOL_WIDTH, POOL_WIDTH)), const((1, POOL_WIDTH)),
                  const((1, MLA_Q_LORA)), const((1, MLA_KV_LORA)),
                  const((MLA_Q_LORA, MLA_HEADS * LANES)), const((MLA_KV_LORA, MLA_HEADS * LANES)),
                  const((MLA_KV_LORA, MLA_HEADS * HEAD_DIM)), const((2, MLA_HEADS * LANES)),
                  const((LANES, LANES)), const((1, ZR_COLS))],
        out_specs=[tok(ZR_COLS), tok(POOL_WIDTH), tok(MLA_HEADS * LANES), tok(MLA_HEADS * LANES),
                   tok(MLA_HEADS * HEAD_DIM)],
        out_shape=[jax.ShapeDtypeStruct((B, T, ZR_COLS), F32),
                   jax.ShapeDtypeStruct((B, T, POOL_WIDTH), BF16),
                   jax.ShapeDtypeStruct((B, T, MLA_HEADS * LANES), BF16),
                   jax.ShapeDtypeStruct((B, T, MLA_HEADS * LANES), BF16),
                   jax.ShapeDtypeStruct((B, T, MLA_HEADS * HEAD_DIM), BF16)],
        scratch_shapes=[pltpu.VMEM((POOL_HALO + tm, POOL_WIDTH), F32), pltpu.VMEM((SUBLANES, ZR_COLS), F32)],
        compiler_params=_cparams(("arbitrary", "arbitrary")),
    )(x, mod_l, gain, win, cosf, sinf, poolw, pools, qg, kvg, wq, wk, wv, qkg, bd128, mu)


WKV_PASSES_SCORE = 1
WKV_PASSES_INV = 1
WKV_PASSES_APPLY = 1
WKV_PASSES_STATE = 1
WKV_STEP_CHUNKS = 1
WKV_STEP_SEQS = 8
WKV_GROUPS = 2
WKV_GROUP_LEAD = 4


def _stack_heads(xp, lane):
    return jnp.concatenate([jnp.where(lane < HEAD_DIM, xp, 0.0),
                            jnp.where(lane >= HEAD_DIM, xp, 0.0)], axis=0)


def _wkv_prep(r, lw, k, v, kk, a, tri, masks):
    L = r[0].shape[0]
    nc = len(r)
    each = lambda f, *ls: [f(*xs) for xs in zip(*ls)]
    lane = lax.broadcasted_iota(jnp.int32, (L, LANES), 1)
    stack = lambda x: _stack_heads(x, lane)
    cum = each(lambda x: _mm_exact_rhs_left(tri, x), lw)
    cum_last = each(lambda c: c[L - 1:L, :], cum)
    e_w = each(jnp.exp, cum)
    e_wm = each(lambda c, x: jnp.exp(c - x), cum, lw)
    e_iw = each(lambda c: jnp.exp(-c), cum)
    e_d = each(lambda cl, c: jnp.exp(cl - c), cum_last, cum)
    beta = each(lambda x, y: x * y, kk, a)
    r_f = each(lambda x, e: x * e, r, e_w)
    a_f = each(lambda x, e: -x * e, kk, e_wm)
    a_s = each(stack, a_f)
    b_s = each(lambda x, e: stack(x * e), beta, e_iw)
    k_s = each(lambda x, e: stack(x * e), k, e_iw)
    b_d = each(lambda x, e: stack(x * e), beta, e_d)
    k_d = each(lambda x, e: stack(x * e), k, e_d)
    v_s = each(stack, v)
    yield
    g = each(lambda af, rf, bs, ks: _mm(jnp.concatenate([af, rf], axis=0),
                                        jnp.concatenate([bs, ks], axis=0), NT, WKV_PASSES_SCORE),
             a_f, r_f, b_s, k_s)
    strict, incl, levels = masks
    a_ab = each(lambda x: jnp.where(strict, x[:L, :LANES], 0.0), g)
    a_ak = each(lambda x: jnp.where(strict, x[:L, LANES:], 0.0), g)
    s_rb = each(lambda x: jnp.where(incl, x[L:, :LANES], 0.0), g)
    s_rk = each(lambda x: jnp.where(incl, x[L:, LANES:], 0.0), g)
    eye = jnp.where(levels[0][1], 1.0, 0.0)
    tinv = each(lambda x: eye + jnp.where(levels[0][0], x, 0.0), a_ab)
    yield
    for lvl_mask, _ in levels[1:]:
        et = each(lambda x, t: _mm(jnp.where(lvl_mask, x, 0.0), stack(t), NN, WKV_PASSES_INV), a_ab, tinv)
        tinv = each(lambda t, x: t + _mm(t, stack(x), NN, WKV_PASSES_INV), tinv, et)
        yield
    av = each(lambda x, y: _mm(x, y, NN, WKV_PASSES_APPLY), a_ak, v_s)
    tx = each(lambda t, x, y: _mm(t, jnp.concatenate([x, stack(y)], axis=1), NN, WKV_PASSES_APPLY),
              tinv, a_s, av)
    ta_s = each(lambda x: stack(x[:, :LANES]), tx)
    c1_s = each(lambda x: stack(x[:, LANES:]), tx)
    yield
    ra = each(lambda rf, s, x: rf + _mm(s, x, NN, WKV_PASSES_APPLY), r_f, s_rb, ta_s)
    c2 = each(lambda sb, sk, x, vs: _mm(jnp.concatenate([sb, sk], axis=1),
                                        jnp.concatenate([x, vs], axis=0),
                                        NN, WKV_PASSES_APPLY), s_rb, s_rk, c1_s, v_s)
    yield
    tb = each(lambda x, y, bd: _mm(jnp.concatenate([x, y], axis=1).T, bd, NN, WKV_PASSES_APPLY),
              ta_s, c1_s, b_d)
    c3 = each(lambda x, vs, kd: x[LANES:] + _mm(vs.T, kd, NN, WKV_PASSES_APPLY), tb, v_s, k_d)
    return [(ra[i], c2[i], jnp.exp(cum_last[i]), tb[i][:LANES], c3[i]) for i in range(nc)]


def _mm_exact_rhs_left(tri_bf16, x):
    x0 = x.astype(BF16)
    x1 = (x - x0.astype(F32)).astype(BF16)
    return _dot(tri_bf16, x0) + _dot(tri_bf16, x1)


def _wkv_masks(L):
    row = lax.broadcasted_iota(jnp.int32, (L, 2 * L), 0)
    col = lax.broadcasted_iota(jnp.int32, (L, 2 * L), 1) % L
    strict = row > col
    incl = row >= col
    levels = []
    m = 1
    while m < L:
        same = (row // (2 * m)) == (col // (2 * m))
        lvl = same & ((row % (2 * m)) >= m) & ((col % (2 * m)) < m)
        levels.append((lvl, row == col))
        m *= 2
    return strict, incl, levels


def _rwkv_kernel(has_vres, *refs):
    if has_vres:
        (z_ref, vf_ref, vec_ref, w2a2_ref, g2v2_ref, bd_ref,
         y_ref, state) = refs
    else:
        (z_ref, vec_ref, w2a2_ref, g2v2_ref, bd_ref,
         y_ref, vout_ref, state) = refs
    c = pl.program_id(1)
    n_seq, seq_rows = z_ref.shape[0], z_ref.shape[1]
    rows = n_seq * seq_rows
    L = WKV_CHUNK
    W = RWKV_WIDTH

    @pl.when(c == 0)
    def _():
        state[...] = jnp.zeros(state.shape, F32)

    flat = lambda ref: jnp.concatenate([ref[s] for s in range(n_seq)], axis=0)
    zs_all = flat(z_ref)
    if has_vres:
        vf_all = flat(vf_ref)
    vec = vec_ref[...]
    w0, a0, k_k, k_a, r_k, ln_g, ln_b, v0 = (vec[j:j + 1] for j in range(8))
    bd = bd_ref[...]
    masks = _wkv_masks(L)
    rowt = lax.broadcasted_iota(jnp.int32, (L, L), 0)
    colt = lax.broadcasted_iota(jnp.int32, (L, L), 1)
    tri = jnp.where(rowt >= colt, 1.0, 0.0).astype(BF16)
    n_pairs = RWKV_HEADS // 2
    S_now = [{(s, p): state[s * n_pairs + p] for s in range(n_seq) for p in range(n_pairs)}]

    def group(s0, s1):
        seqs = range(s0, s1)
        r0, r1 = s0 * seq_rows, s1 * seq_rows
        zs = zs_all[r0:r1]
        r = zs[:, 0:W]
        k = zs[:, W:2 * W]
        v = zs[:, 2 * W:3 * W]
        wa = zs[:, 3 * W:3 * W + LANES]
        gb = zs[:, 3 * W + LANES:ZR_COLS]
        lane_a = lax.broadcasted_iota(jnp.int32, wa.shape, 1)
        t1 = _dot(jnp.where(lane_a < RWKV_DECAY_LORA, jnp.tanh(wa), wa).astype(BF16), w2a2_ref[...])
        lane_g = lax.broadcasted_iota(jnp.int32, gb.shape, 1)
        t2 = _dot(jnp.where(lane_g < RWKV_GATE_LORA, _sigmoid(gb), gb).astype(BF16), g2v2_ref[...])
        yield
        xw = w0 + t1[:, :W]
        lw = -np.exp(-0.5).astype(np.float32) * _sigmoid(xw)
        a = _sigmoid(a0 + t1[:, W:])
        g = t2[:, :W]
        if has_vres:
            v = v + (vf_all[r0:r1] - v) * _sigmoid(v0 + t2[:, W:])
        else:
            for s in seqs:
                vout_ref[s] = v[(s - s0) * seq_rows:(s - s0 + 1) * seq_rows]
        kk = k * k_k
        kk = kk * jnp.minimum(lax.rsqrt(_mm_exact_rhs(kk * kk, bd)), 1e12)
        k = k * (1.0 + (a - 1.0) * k_a)
        yield
        n_chunks = seq_rows // L
        idx = [(s, ch, p) for s in seqs for ch in range(n_chunks) for p in range(n_pairs)]
        row0 = lambda s, ch: (s - s0) * seq_rows + ch * L
        cut = lambda x: [x[row0(s, ch):row0(s, ch) + L, p * LANES:(p + 1) * LANES] for s, ch, p in idx]
        res = yield from _wkv_prep(cut(r), cut(lw), cut(k), cut(v), cut(kk), cut(a), tri, masks)
        prep = dict(zip(idx, res))
        yield
        sp = [(s, p) for s in seqs for p in range(n_pairs)]
        S = {key: S_now[0][key] for key in sp}
        y_blk = {}
        for ch in range(n_chunks):
            y_s = {(s, p): _mm(prep[s, ch, p][0], S[s, p], NT, WKV_PASSES_STATE) + prep[s, ch, p][1]
                   for s, p in sp}
            for s in seqs:
                y_blk[s, ch] = jnp.concatenate([y_s[s, p] for p in range(n_pairs)], axis=1)
            S = {(s, p): S[s, p] * prep[s, ch, p][2] + _mm(S[s, p], prep[s, ch, p][3], NN, WKV_PASSES_STATE)
                 + prep[s, ch, p][4] for s, p in sp}
        S_now[0].update(S)
        y = jnp.concatenate([y_blk[s, ch] for s in seqs for ch in range(n_chunks)], axis=0)
        yield
        inv = 1.0 / HEAD_DIM
        mean = _mm_exact_rhs(y, bd) * inv
        yc = y - mean
        var = _mm_exact_rhs(yc * yc, bd) * inv
        yn = yc * lax.rsqrt(var + RWKV_LNX_EPS) * ln_g + ln_b
        bonus = _mm_exact_rhs(r * k * r_k, bd) * v
        out = ((yn + bonus) * g).astype(BF16)
        for s in seqs:
            y_ref[s] = out[(s - s0) * seq_rows:(s - s0 + 1) * seq_rows]

    per = n_seq // WKV_GROUPS
    gens = [group(gi * per, (gi + 1) * per) for gi in range(WKV_GROUPS)]
    alive = [True] * WKV_GROUPS
    tick = 0
    while any(alive):
        for gi in range(WKV_GROUPS):
            if alive[gi] and tick >= gi * WKV_GROUP_LEAD:
                try:
                    next(gens[gi])
                except StopIteration:
                    alive[gi] = False
        tick += 1
    for s in range(n_seq):
        for p in range(n_pairs):
            state[s * n_pairs + p] = S_now[0][s, p]


def _rwkv(zr, v_first, vec8, w2a2, g2v2, bd64):
    B, T, _ = zr.shape
    L = WKV_CHUNK * WKV_STEP_CHUNKS
    ns = WKV_STEP_SEQS
    has_vres = v_first is not None
    const = lambda shape: pl.BlockSpec(shape, lambda b, c: tuple(0 for _ in shape))
    tok = lambda w: pl.BlockSpec((ns, L, w), lambda b, c: (b, c, 0))
    in_specs = [tok(ZR_COLS)]
    args = [zr]
    if has_vres:
        in_specs.append(tok(RWKV_WIDTH))
        args.append(v_first)
    in_specs += [const((8, RWKV_WIDTH)), const((LANES, 2 * RWKV_WIDTH)),
                 const((2 * LANES, 2 * RWKV_WIDTH)), const((LANES, LANES))]
    args += [vec8, w2a2, g2v2, bd64]
    out_specs = [tok(RWKV_WIDTH)]
    out_shape = [jax.ShapeDtypeStruct((B, T, RWKV_WIDTH), BF16)]
    if not has_vres:
        out_specs.append(tok(RWKV_WIDTH))
        out_shape.append(jax.ShapeDtypeStruct((B, T, RWKV_WIDTH), F32))
    outs = pl.pallas_call(
        functools.partial(_rwkv_kernel, has_vres),
        grid=(B // ns, T // L),
        in_specs=in_specs, out_specs=out_specs, out_shape=out_shape,
        scratch_shapes=[pltpu.VMEM((ns * (RWKV_HEADS // 2), LANES, LANES), F32)],
        compiler_params=_cparams(("arbitrary", "arbitrary")),
    )(*args)
    return (outs[0], v_first) if has_vres else (outs[0], outs[1])


def _attn_step(q_ref, k_ref, v_ref, m_sc, l_sc, acc_sc, masked):
    tq = q_ref.shape[1]
    tk = k_ref.shape[1]
    lane = lax.broadcasted_iota(jnp.int32, (tq, LANES), 1)
    if masked:
        rowi = lax.broadcasted_iota(jnp.int32, (tq, tk), 0)
        coli = lax.broadcasted_iota(jnp.int32, (tq, tk), 1)
        keep = coli <= rowi
    heads = range(MLA_HEADS)
    s = [_dot(q_ref[0, :, hd * LANES:(hd + 1) * LANES], k_ref[0, :, hd * LANES:(hd + 1) * LANES], NT)
         for hd in heads]
    if masked:
        s = [jnp.where(keep, x, NEG_INF) for x in s]
    m_prev = [m_sc[hd] for hd in heads]
    m_new = [jnp.maximum(m_prev[hd], jnp.max(s[hd], axis=-1, keepdims=True)) for hd in heads]
    alpha = [jnp.exp(m_prev[hd] - m_new[hd]) for hd in heads]
    p = [jnp.exp(s[hd] - jnp.concatenate([m_new[hd]] * (tk // LANES), axis=1)) for hd in heads]
    for hd in heads:
        l_sc[hd] = alpha[hd] * l_sc[hd] + jnp.sum(p[hd], axis=-1, keepdims=True)
        m_sc[hd] = m_new[hd]
    pv = [_dot(p[hd].astype(BF16), v_ref[0, :, (hd // 2) * LANES:(hd // 2 + 1) * LANES]) for hd in heads]
    first = lane < HEAD_DIM
    for pr in range(MLA_HEADS // 2):
        acc_sc[pr] = (acc_sc[pr] * jnp.where(first, alpha[2 * pr], alpha[2 * pr + 1])
                      + jnp.where(first, pv[2 * pr], pv[2 * pr + 1]))


def _attn_step_bounded(q_ref, k_ref, v_ref, cb_ref, l_sc, acc_sc, masked):
    tq = q_ref.shape[1]
    tk = k_ref.shape[1]
    lane = lax.broadcasted_iota(jnp.int32, (tq, LANES), 1)
    if masked:
        rowi = lax.broadcasted_iota(jnp.int32, (tq, tk), 0)
        coli = lax.broadcasted_iota(jnp.int32, (tq, tk), 1)
        keep = coli <= rowi
    ps = []
    for hd in range(MLA_HEADS):
        s = _dot(q_ref[0, :, hd * LANES:(hd + 1) * LANES], k_ref[0, :, hd * LANES:(hd + 1) * LANES], NT)
        c = cb_ref[0, hd:hd + 1, :]
        p = jnp.exp(s - jnp.concatenate([c] * (tk // LANES), axis=1))
        if masked:
            p = jnp.where(keep, p, 0.0)
        part = p[:, 0:LANES]
        for t in range(1, tk // LANES):
            part = part + p[:, t * LANES:(t + 1) * LANES]
        l_sc[hd] = l_sc[hd] + part
        ps.append(p.astype(BF16))
    first = lane < HEAD_DIM
    for pr in range(MLA_HEADS // 2):
        pv = _dot(jnp.concatenate(ps[2 * pr:2 * pr + 2], axis=0), v_ref[0, :, pr * LANES:(pr + 1) * LANES])
        acc_sc[pr] = acc_sc[pr] + jnp.where(first, pv[:tq], pv[tq:])


def _attn_finish(o_ref, l_sc, acc_sc, lane_partial):
    tq = o_ref.shape[1]
    lane = lax.broadcasted_iota(jnp.int32, (tq, LANES), 1)
    outs = []
    for pr in range(MLA_HEADS // 2):
        la, lb = l_sc[2 * pr], l_sc[2 * pr + 1]
        if lane_partial:
            la = jnp.sum(la, axis=-1, keepdims=True)
            lb = jnp.sum(lb, axis=-1, keepdims=True)
        outs.append(acc_sc[pr] / jnp.where(lane < HEAD_DIM, la, lb))
    o_ref[0] = jnp.concatenate(outs, axis=1).astype(BF16)


def _attn_kernel(ok_ref, qi_ref, kj_ref, q_ref, k_ref, v_ref, cb_ref, o_ref, m_sc, l_sc, acc_sc):
    i = qi_ref[pl.program_id(1)]
    j = kj_ref[pl.program_id(1)]
    bounded = ok_ref[pl.program_id(0)] == 1
    exact = jnp.logical_not(bounded)

    @pl.when(j == 0)
    def _():
        m_sc[...] = jnp.full(m_sc.shape, NEG_INF, F32)
        l_sc[...] = jnp.zeros(l_sc.shape, F32)
        acc_sc[...] = jnp.zeros(acc_sc.shape, F32)

    @pl.when(bounded & (j < i))
    def _():
        _attn_step_bounded(q_ref, k_ref, v_ref, cb_ref, l_sc, acc_sc, masked=False)

    @pl.when(bounded & (j == i))
    def _():
        _attn_step_bounded(q_ref, k_ref, v_ref, cb_ref, l_sc, acc_sc, masked=True)
        _attn_finish(o_ref, l_sc, acc_sc, lane_partial=True)

    @pl.when(exact & (j < i))
    def _():
        _attn_step(q_ref, k_ref, v_ref, m_sc, l_sc, acc_sc, masked=False)

    @pl.when(exact & (j == i))
    def _():
        _attn_step(q_ref, k_ref, v_ref, m_sc, l_sc, acc_sc, masked=True)
        _attn_finish(o_ref, l_sc, acc_sc, lane_partial=False)


def _attention(q, k, v, qk_gain):
    B, T, _ = q.shape
    nq = T // TQ
    gmax = jnp.max(jnp.abs(qk_gain), axis=1)
    c = gmax[0] * gmax[1] * (MLA_QK_DIM ** 0.5) * ATTN_BOUND_SLACK
    ok = jnp.broadcast_to((c <= ATTN_BOUND_MAX).astype(jnp.int32), (B,))
    cb = jnp.broadcast_to(c, (B, MLA_HEADS, LANES))
    pairs = [(i, j) for i in range(nq) for j in range(i + 1)]
    qi = jnp.asarray([p[0] for p in pairs], jnp.int32)
    kj = jnp.asarray([p[1] for p in pairs], jnp.int32)
    grid_spec = pltpu.PrefetchScalarGridSpec(
        num_scalar_prefetch=3,
        grid=(B, len(pairs)),
        in_specs=[pl.BlockSpec((1, TQ, MLA_HEADS * LANES), lambda b, t, ok, qi, kj: (b, qi[t], 0)),
                  pl.BlockSpec((1, TQ, MLA_HEADS * LANES), lambda b, t, ok, qi, kj: (b, kj[t], 0)),
                  pl.BlockSpec((1, TQ, MLA_HEADS * HEAD_DIM), lambda b, t, ok, qi, kj: (b, kj[t], 0)),
                  pl.BlockSpec((1, MLA_HEADS, LANES), lambda b, t, ok, qi, kj: (b, 0, 0))],
        out_specs=pl.BlockSpec((1, TQ, MLA_HEADS * HEAD_DIM), lambda b, t, ok, qi, kj: (b, qi[t], 0)),
        scratch_shapes=[pltpu.VMEM((MLA_HEADS, TQ, LANES), F32),
                        pltpu.VMEM((MLA_HEADS, TQ, LANES), F32),
                        pltpu.VMEM((MLA_HEADS // 2, TQ, LANES), F32)])
    return pl.pallas_call(
        _attn_kernel,
        grid_spec=grid_spec,
        out_shape=jax.ShapeDtypeStruct((B, T, MLA_HEADS * HEAD_DIM), BF16),
        compiler_params=_cparams(("arbitrary", "arbitrary")),
    )(ok, qi, kj, q, k, v, cb)


def _mixout_kernel(has_router, *refs):
    if has_router:
        (x_ref, yr_ref, yp_ref, ym_ref, wo_ref, mod_ref, gain_ref, rt_ref, tri_ref,
         xo_ref, h_ref, route_ref, cnt_ref, cnt_sc) = refs
    else:
        x_ref, yr_ref, yp_ref, ym_ref, wo_ref, mod_ref, gain_ref, xo_ref, h_ref = refs
    mod = mod_ref[0]
    o1 = RWKV_WIDTH
    o2 = RWKV_WIDTH + POOL_WIDTH
    mix = (_dot(yr_ref[0], wo_ref[0:o1, :]) + _dot(yp_ref[0], wo_ref[o1:o2, :])
           + _dot(ym_ref[0], wo_ref[o2:, :]))
    x = x_ref[0] + mod[2:3] * mix
    xo_ref[0] = x
    h = _rms(x) * gain_ref[...] * (1.0 + mod[4:5]) + mod[3:4]
    if not has_router:
        h_ref[0] = h.astype(BF16)
    else:
        hp = _pack_bf16_pairs(h)
        for ck in range(ROW_CHUNKS):
            h_ref[ck, 0] = hp[:, ck * PACK_CHUNK_W:(ck + 1) * PACK_CHUNK_W]
        logits = _dot(h.astype(BF16), rt_ref[...])
        lane = lax.broadcasted_iota(jnp.int32, logits.shape, 1).astype(F32)
        lg = jnp.where(lane < N_EXPERTS, logits, -jnp.inf)
        m1 = jnp.max(lg, axis=-1, keepdims=True)
        i1 = jnp.min(jnp.where(lg == m1, lane, float(LANES)), axis=-1, keepdims=True)
        lg2 = jnp.where(lane == i1, -jnp.inf, lg)
        m2 = jnp.max(lg2, axis=-1, keepdims=True)
        i2 = jnp.min(jnp.where(lg2 == m2, lane, float(LANES)), axis=-1, keepdims=True)
        e2 = jnp.exp(m2 - m1)
        g1 = 1.0 / (1.0 + e2)
        g2 = e2 / (1.0 + e2)
        first = (pl.program_id(0) == 0) & (pl.program_id(1) == 0)

        @pl.when(first)
        def _():
            cnt_sc[...] = jnp.zeros(cnt_sc.shape, F32)

        hit1 = lane == i1
        hit2 = lane == i2
        onehot = jnp.where(hit1 | hit2, 1.0, 0.0)
        prefix = _dot(tri_ref[...], onehot.astype(BF16)) + cnt_sc[0:1, :]
        r1 = jnp.sum(jnp.where(hit1, prefix, 0.0), axis=-1, keepdims=True)
        r2 = jnp.sum(jnp.where(hit2, prefix, 0.0), axis=-1, keepdims=True)
        cnt_sc[...] = cnt_sc[...] + jnp.sum(onehot, axis=0, keepdims=True)
        cnt_ref[...] = cnt_sc[...]
        vals = (i1, i2, g1, g2, r1, r2)
        route = jnp.zeros(logits.shape, F32)
        for pos, val in enumerate(vals):
            route = jnp.where(lane == pos, val, route)
        route_ref[0] = route


def _mixout(x, yr, yp, ym, wo, mod_l, gain, router_p):
    B, T, _ = x.shape
    tm = TM_OUT
    has_router = router_p is not None
    const = lambda shape: pl.BlockSpec(shape, lambda b, i: tuple(0 for _ in shape))
    tok = lambda w: pl.BlockSpec((1, tm, w), lambda b, i: (b, i, 0))
    in_specs = [tok(D_MODEL), tok(RWKV_WIDTH), tok(POOL_WIDTH), tok(MLA_HEADS * HEAD_DIM),
                const((D_MODEL, D_MODEL)), pl.BlockSpec((1, 6, D_MODEL), lambda b, i: (b, 0, 0)),
                const((1, D_MODEL))]
    args = [x, yr, yp, ym, wo, mod_l, gain]
    out_specs = [tok(D_MODEL), tok(D_MODEL)]
    out_shape = [jax.ShapeDtypeStruct((B, T, D_MODEL), F32), jax.ShapeDtypeStruct((B, T, D_MODEL), BF16)]
    scratch = []
    if has_router:
        out_specs[1] = pl.BlockSpec((ROW_CHUNKS, 1, tm, PACK_CHUNK_W), lambda b, i: (0, b, i, 0))
        out_shape[1] = jax.ShapeDtypeStruct((ROW_CHUNKS, B, T, PACK_CHUNK_W), jnp.uint32)
        ids = np.arange(tm)
        tri = jnp.asarray(ids[:, None] > ids[None, :], BF16)
        in_specs += [const((D_MODEL, LANES)), const((tm, tm))]
        args += [router_p, tri]
        out_specs += [tok(LANES), const((SUBLANES, LANES))]
        out_shape += [jax.ShapeDtypeStruct((B, T, LANES), F32),
                      jax.ShapeDtypeStruct((SUBLANES, LANES), F32)]
        scratch = [pltpu.VMEM((SUBLANES, LANES), F32)]
    return pl.pallas_call(
        functools.partial(_mixout_kernel, has_router),
        grid=(B, T // tm),
        in_specs=in_specs, out_specs=out_specs, out_shape=out_shape, scratch_shapes=scratch,
        compiler_params=_cparams(("arbitrary", "arbitrary")),
    )(*args)


def _ffn_kernel(h_ref, wg_ref, wu_ref, wo_ref, x_ref, mod_ref, o_ref, acc):
    j = pl.program_id(1)

    @pl.when(j == 0)
    def _():
        acc[...] = jnp.zeros(acc.shape, F32)

    h = h_ref[...]
    gg = _dot(h, wg_ref[...])
    uu = _dot(h, wu_ref[...])
    acc[...] += _dot((_silu(gg) * uu).astype(BF16), wo_ref[...])

    @pl.when(j == pl.num_programs(1) - 1)
    def _():
        o_ref[...] = x_ref[...] + mod_ref[0][5:6] * acc[...]


def _ffn(h2, w_in, w_out, x, mod_l):
    N = h2.shape[0]
    T = N // mod_l.shape[0]
    tm, tf = TM_FFN, TF_FFN
    nf = D_FF // tf
    per_b = T // tm
    return pl.pallas_call(
        _ffn_kernel,
        grid=(N // tm, nf),
        in_specs=[pl.BlockSpec((tm, D_MODEL), lambda i, j: (i, 0)),
                  pl.BlockSpec((D_MODEL, tf), lambda i, j: (0, j)),
                  pl.BlockSpec((D_MODEL, tf), lambda i, j: (0, j + nf)),
                  pl.BlockSpec((tf, D_MODEL), lambda i, j: (j, 0)),
                  pl.BlockSpec((tm, D_MODEL), lambda i, j: (i, 0)),
                  pl.BlockSpec((1, 6, D_MODEL), lambda i, j: (i // per_b, 0, 0))],
        out_specs=pl.BlockSpec((tm, D_MODEL), lambda i, j: (i, 0)),
        out_shape=jax.ShapeDtypeStruct((N, D_MODEL), F32),
        scratch_shapes=[pltpu.VMEM((tm, D_MODEL), F32)],
        compiler_params=_cparams(("arbitrary", "arbitrary")),
    )(h2, w_in, w_in, w_out, x, mod_l)


def _moe_kernel(be_ref, nv_ref, last_ref, x_ref, wg_ref, wu_ref, wo_ref, o_ref, acc, xm):
    i = pl.program_id(0)
    j = pl.program_id(1)

    @pl.when(i <= last_ref[0])
    def _():
        @pl.when(j == 0)
        def _():
            acc[...] = jnp.zeros(acc.shape, F32)
            row = lax.broadcasted_iota(jnp.int32, (xm.shape[0], 1), 0)
            xp = jnp.concatenate([x_ref[ck] for ck in range(ROW_CHUNKS)], axis=1)
            xp = jnp.where(row < nv_ref[i], xp, jnp.uint32(0))
            xm[...] = _unpack_bf16_pairs(xp)

        def swiglu_rows(n_rows):
            x = xm[0:n_rows, :]
            gg = _dot(x, wg_ref[0].astype(BF16))
            uu = _dot(x, wu_ref[0].astype(BF16))
            acc[0:n_rows, :] += _dot((_silu(gg) * uu).astype(BF16), wo_ref[0].astype(BF16))

        half = xm.shape[0] // 2

        @pl.when(nv_ref[i] > half)
        def _():
            swiglu_rows(xm.shape[0])

        @pl.when(nv_ref[i] <= half)
        def _():
            swiglu_rows(half)

        @pl.when(j == pl.num_programs(1) - 1)
        def _():
            yp = _pack_bf16_pairs(acc[...])
            for ck in range(ROW_CHUNKS):
                o_ref[ck] = yp[:, ck * PACK_CHUNK_W:(ck + 1) * PACK_CHUNK_W]


def _moe_experts(xs, w_in, w_out, block_exp, n_valid, last_blk):
    n_rows = xs.shape[1]
    tm, tf = MOE_BLOCK, TF_MOE
    nf = D_FF_EXPERT // tf
    blk = lambda i, last: jnp.minimum(i, last[0])
    chunk = lambda i, j, last: jnp.where(i <= last[0], j, nf - 1)
    grid_spec = pltpu.PrefetchScalarGridSpec(
        num_scalar_prefetch=3,
        grid=(n_rows // tm, nf),
        in_specs=[pl.BlockSpec((ROW_CHUNKS, tm, PACK_CHUNK_W),
                               lambda i, j, be, nv, last: (0, blk(i, last), 0)),
                  pl.BlockSpec((1, D_MODEL, tf),
                               lambda i, j, be, nv, last: (be[blk(i, last)], 0, chunk(i, j, last))),
                  pl.BlockSpec((1, D_MODEL, tf),
                               lambda i, j, be, nv, last: (be[blk(i, last)], 0, chunk(i, j, last) + nf)),
                  pl.BlockSpec((1, tf, D_MODEL),
                               lambda i, j, be, nv, last: (be[blk(i, last)], chunk(i, j, last), 0))],
        out_specs=pl.BlockSpec((ROW_CHUNKS, tm, PACK_CHUNK_W), lambda i, j, be, nv, last: (0, blk(i, last), 0)),
        scratch_shapes=[pltpu.VMEM((tm, D_MODEL), F32), pltpu.VMEM((tm, D_MODEL), BF16)])
    return pl.pallas_call(
        _moe_kernel,
        grid_spec=grid_spec,
        out_shape=jax.ShapeDtypeStruct((ROW_CHUNKS, n_rows, PACK_CHUNK_W), jnp.uint32),
        compiler_params=_cparams(("arbitrary", "arbitrary")),
    )(block_exp, n_valid, last_blk, xs, w_in, w_in, w_out)


def _sc_mesh():
    return plsc.VectorSubcoreMesh(core_axis_name="c", subcore_axis_name="s")


def _sc_scatter_rows(x, dest, n_rows):
    N, D = x.shape
    K = dest.shape[0]
    win = SC_WINDOW

    @pl.kernel(out_type=jax.ShapeDtypeStruct((n_rows, D), x.dtype), mesh=_sc_mesh(), scratch_types=[])
    def scatter(x_hbm, d_hbm, o_hbm):
        def body(x_vmem, *idx_vmem):
            for iv in idx_vmem:
                pltpu.sync_copy(x_vmem, o_hbm.at[iv.at[0]])

        pltpu.emit_pipeline(
            body,
            grid=(N // win,),
            in_specs=[pl.BlockSpec((win, D), lambda i: (i, 0))]
            + [pl.BlockSpec((1, win), functools.partial(lambda k, i: (k, i), k)) for k in range(K)],
            out_specs=[],
            core_axis_name=("c", "s"),
            dimension_semantics=(pltpu.PARALLEL,),
        )(x_hbm, *([d_hbm] * K))

    return scatter(x, dest)


def _sc_gather_rows(x, idx):
    n = idx.shape[0]
    D = x.shape[1]
    win = SC_WINDOW

    @pl.kernel(out_type=jax.ShapeDtypeStruct((n, D), x.dtype), mesh=_sc_mesh(), scratch_types=[])
    def gather(x_hbm, i_hbm, o_hbm):
        def body(i_vmem, o_vmem):
            pltpu.sync_copy(x_hbm.at[i_vmem.at[0]], o_vmem)

        pltpu.emit_pipeline(
            body,
            grid=(n // win,),
            in_specs=[pl.BlockSpec((1, win), lambda i: (0, i))],
            out_specs=[pl.BlockSpec((win, D), lambda i: (i, 0))],
            core_axis_name=("c", "s"),
            dimension_semantics=(pltpu.PARALLEL,),
        )(i_hbm, o_hbm)

    return gather(x, idx.reshape(1, n))


def _combine_kernel(x_ref, ya_ref, yb_ref, route_ref, mod_ref, o_ref):
    rt = route_ref[...]
    ya = _unpack_bf16_pairs(jnp.concatenate([ya_ref[0, ck] for ck in range(ROW_CHUNKS)], axis=1))
    yb = _unpack_bf16_pairs(jnp.concatenate([yb_ref[0, ck] for ck in range(ROW_CHUNKS)], axis=1))
    f = rt[:, 2:3] * ya.astype(F32) + rt[:, 3:4] * yb.astype(F32)
    o_ref[...] = x_ref[...] + mod_ref[0][5:6] * f


def _combine(x, y2, route, mod_l):
    N = x.shape[0]
    T = N // mod_l.shape[0]
    tm = 1024
    per_b = T // tm
    tok = pl.BlockSpec((tm, D_MODEL), lambda i: (i, 0))
    slot = lambda k: pl.BlockSpec((1, ROW_CHUNKS, tm, PACK_CHUNK_W), lambda i: (k, 0, i, 0))
    return pl.pallas_call(
        _combine_kernel,
        grid=(N // tm,),
        in_specs=[tok, slot(0), slot(1),
                  pl.BlockSpec((tm, LANES), lambda i: (i, 0)),
                  pl.BlockSpec((1, 6, D_MODEL), lambda i: (i // per_b, 0, 0))],
        out_specs=tok,
        out_shape=jax.ShapeDtypeStruct((N, D_MODEL), F32),
        compiler_params=_cparams(("arbitrary",)),
    )(x, y2, y2, route, mod_l)


def _moe(h2, route, counts, w_in, w_out, x, mod_l):
    N = x.shape[0]
    blk = MOE_BLOCK
    cnt = counts[0, :N_EXPERTS].astype(jnp.int32)
    padded = (cnt + blk - 1) // blk * blk
    pend = jnp.cumsum(padded)
    pstart = pend - padded
    e = route[:, 0:TOP_K].astype(jnp.int32)
    rank = route[:, 2 * TOP_K:3 * TOP_K].astype(jnp.int32)
    dest = (jnp.take(pstart, e) + rank).T
    n_blocks = N * TOP_K // blk + N_EXPERTS
    bstart = jnp.arange(n_blocks, dtype=jnp.int32) * blk
    block_exp = jnp.minimum(jnp.sum((bstart[:, None] >= pend[None, :]).astype(jnp.int32), axis=1),
                            N_EXPERTS - 1)
    n_valid = jnp.clip(cnt[block_exp] - (bstart - pstart[block_exp]), 0, blk).astype(jnp.int32)
    n_rows = n_blocks * blk
    dest_ck = dest[:, None, :] + (jnp.arange(ROW_CHUNKS, dtype=jnp.int32) * n_rows)[None, :, None]
    xs = _sc_scatter_rows(h2, dest_ck.reshape(TOP_K, ROW_CHUNKS * N), ROW_CHUNKS * n_rows)
    last_blk = (pend[-1:] // blk - 1).astype(jnp.int32)
    yb = _moe_experts(xs.reshape(ROW_CHUNKS, n_rows, PACK_CHUNK_W), w_in, w_out, block_exp, n_valid,
                      last_blk)
    y2 = _sc_gather_rows(yb.reshape(ROW_CHUNKS * n_rows, PACK_CHUNK_W), dest_ck.reshape(-1))
    return _combine(x, y2.reshape(TOP_K, ROW_CHUNKS, N, PACK_CHUNK_W), route, mod_l)


def _layout_w_in(w, has_vres):
    W = RWKV_WIDTH
    off_gd = 3 * W + RWKV_DECAY_LORA + RWKV_ICLR_LORA
    off_pool = off_gd + RWKV_GATE_LORA
    off_q = off_pool + POOL_WIDTH
    off_kv = off_q + MLA_Q_LORA
    off_kr = off_kv + MLA_KV_LORA
    n_base = off_kr + MLA_QK_ROPE
    d = w.shape[0]
    zeros = lambda n: jnp.zeros((d, n), w.dtype)
    vd = w[:, n_base:n_base + RWKV_VRES_LORA] if has_vres else zeros(RWKV_VRES_LORA)
    cols = [w[:, :off_gd], w[:, off_gd:off_pool], vd, zeros(ZR_COLS - off_pool - RWKV_VRES_LORA),
            w[:, off_pool:off_q], w[:, off_q:off_kv], w[:, off_kv:off_kr],
            zeros(MLA_QK_NOPE), w[:, off_kr:n_base], zeros(LANES - MLA_QK_DIM)]
    return jnp.concatenate(cols, axis=1).astype(BF16)


def _pad_heads(w, per_head, keep_from, keep_n):
    K = w.shape[0]
    wh = w.reshape(K, MLA_HEADS, per_head)[:, :, keep_from:keep_from + keep_n]
    wh = jnp.pad(wh, ((0, 0), (0, 0), (0, LANES - keep_n)))
    return wh.reshape(K, MLA_HEADS * LANES)


def kernel(x, c, positions, w_ada, b_ada, norm_gain, w_in_first, w_in_rest, mu_shift, mu_shift_v,
           rwkv_vec, rwkv_v0, rwkv_w2, rwkv_a2, rwkv_g2, rwkv_v2, pool_w, pool_scale,
           mla_q_lat_gain, mla_kv_lat_gain, mla_wq_up, mla_wkv_up, mla_qk_gain, w_out, ffn_w_in,
           ffn_w_out, moe_router, moe_w_in, moe_w_out):
    B, T, D = x.shape
    depth = w_ada.shape[0]
    assert D == D_MODEL and B % WKV_STEP_SEQS == 0 and D_FF_EXPERT % TF_MOE == 0
    assert T % max(TM_MIX, TM_OUT, TQ, WKV_CHUNK * WKV_STEP_CHUNKS) == 0
    assert (B * T) % max(TM_FFN, MOE_BLOCK) == 0 and (B * T * TOP_K) % MOE_BLOCK == 0
    W = RWKV_WIDTH
    mod = _adaln(c, w_ada, b_ada).reshape(depth, B, 6, D)
    inv_freq = ROPE_BASE ** (-jnp.arange(0, MLA_QK_ROPE, 2, dtype=F32) / MLA_QK_ROPE)
    cosf, sinf = _rope_tables(positions, inv_freq)
    hid = np.arange(LANES) // HEAD_DIM
    bd64 = jnp.asarray(hid[:, None] == hid[None, :], BF16)
    bd128 = jnp.ones((LANES, LANES), BF16)

    v_first = None
    for l in range(depth):
        has_vres = l > 0
        mod_l = mod[l]
        win = _layout_w_in(w_in_first if l == 0 else w_in_rest[l - 1], has_vres)
        poolw = jax.scipy.linalg.block_diag(*[pool_w[l, g] for g in range(len(POOL_WINDOWS))]).astype(BF16)
        wq = _pad_heads(mla_wq_up[l], MLA_QK_DIM, 0, MLA_QK_DIM).astype(BF16)
        wk = _pad_heads(mla_wkv_up[l], MLA_QK_NOPE + HEAD_DIM, 0, MLA_QK_NOPE).astype(BF16)
        wv = mla_wkv_up[l].reshape(MLA_KV_LORA, MLA_HEADS, MLA_QK_NOPE + HEAD_DIM)[:, :, MLA_QK_NOPE:]
        wv = wv.reshape(MLA_KV_LORA, MLA_HEADS * HEAD_DIM).astype(BF16)
        qkg = jnp.tile(jnp.pad(mla_qk_gain[l], ((0, 0), (0, LANES - MLA_QK_DIM))), (1, MLA_HEADS))
        pad_mu = ZR_COLS - mu_shift.shape[1] - RWKV_VRES_LORA
        mu_v = mu_shift_v[l - 1] if has_vres else jnp.zeros((RWKV_VRES_LORA,), F32)
        mu = jnp.concatenate([mu_shift[l], mu_v, jnp.zeros((pad_mu,), F32)]).reshape(1, ZR_COLS)
        zr, y_pool, q, k, v = _mixin(
            x, mod_l, norm_gain[l, 0].reshape(1, D), win, cosf, sinf, poolw,
            pool_scale[l].reshape(1, -1), mla_q_lat_gain[l].reshape(1, -1),
            mla_kv_lat_gain[l].reshape(1, -1), wq, wk, wv, qkg, bd128, mu)

        v0 = rwkv_v0[l - 1] if has_vres else jnp.zeros((W,), F32)
        vec8 = jnp.concatenate([rwkv_vec[l], v0[None]], axis=0)
        w2a2 = jax.scipy.linalg.block_diag(rwkv_w2[l], rwkv_a2[l]).astype(BF16)
        g2 = jnp.pad(rwkv_g2[l], ((0, 2 * LANES - RWKV_GATE_LORA), (0, 0)))
        if has_vres:
            v2 = jnp.pad(rwkv_v2[l - 1], ((RWKV_GATE_LORA, 2 * LANES - RWKV_GATE_LORA - RWKV_VRES_LORA), (0, 0)))
        else:
            v2 = jnp.zeros((2 * LANES, W), F32)
        g2v2 = jnp.concatenate([g2, v2], axis=1).astype(BF16)
        y_rwkv, v_first = _rwkv(zr, v_first, vec8, w2a2, g2v2, bd64)

        y_mla = _attention(q, k, v, mla_qk_gain[l])

        is_moe = (l % 2 == 1)
        router_p = None
        if is_moe:
            router_p = jnp.pad(moe_router[l // 2], ((0, 0), (0, LANES - N_EXPERTS))).astype(BF16)
        outs = _mixout(x, y_rwkv, y_pool, y_mla, w_out[l].astype(BF16), mod_l,
                       norm_gain[l, 1].reshape(1, D), router_p)
        x_mid, h2 = outs[0], outs[1]
        xf = x_mid.reshape(B * T, D)
        if is_moe:
            xo = _moe(h2.reshape(ROW_CHUNKS * B * T, PACK_CHUNK_W), outs[2].reshape(B * T, LANES), outs[3],
                      moe_w_in[l // 2], moe_w_out[l // 2], xf, mod_l)
        else:
            xo = _ffn(h2.reshape(B * T, D), ffn_w_in[l // 2].astype(BF16), ffn_w_out[l // 2].astype(BF16),
                      xf, mod_l)
        x = xo.reshape(B, T, D)
    return x
```

```python
import functools

import numpy as np
import jax
import jax.numpy as jnp
from jax import lax
from jax.experimental import pallas as pl
from jax.experimental.pallas import tpu as pltpu
from jax.experimental.pallas import tpu_sc as plsc

F32 = jnp.float32
BF16 = jnp.bfloat16

D_MODEL = 1024
HEAD_DIM = 64
RWKV_WIDTH = 512
RWKV_HEADS = RWKV_WIDTH // HEAD_DIM
POOL_WIDTH = 256
POOL_WINDOWS = (2, 4, 8, 16)
POOL_HALO = 16
MLA_HEADS = 4
MLA_QK_NOPE = 64
MLA_QK_ROPE = 32
MLA_QK_DIM = MLA_QK_NOPE + MLA_QK_ROPE
MLA_Q_LORA = 256
MLA_KV_LORA = 128
ROPE_BASE = 10000.0
RWKV_DECAY_LORA = 64
RWKV_ICLR_LORA = 64
RWKV_VRES_LORA = 32
RWKV_GATE_LORA = 160
RWKV_LNX_EPS = 64e-5
D_FF = 2816
N_EXPERTS = 8
TOP_K = 2
D_FF_EXPERT = 3584
NORM_EPS = 1e-6
NEG_INF = -1e30

LANES = 128
SUBLANES = 8
VMEM_LIMIT = 56 * 1024 * 1024

ZR_COLS = 3 * RWKV_WIDTH + (RWKV_DECAY_LORA + RWKV_ICLR_LORA) + 2 * LANES
Z_POOL_OFF = ZR_COLS
Z_QLAT_OFF = Z_POOL_OFF + POOL_WIDTH
Z_KVLAT_OFF = Z_QLAT_OFF + MLA_Q_LORA
Z_KROPE_OFF = Z_KVLAT_OFF + MLA_KV_LORA
Z_COLS = Z_KROPE_OFF + LANES

MIXIN_PIECE_W = 512
TM_MIX = 512
TM_OUT = 1024
WKV_CHUNK = 64
TQ = 1024
ATTN_BOUND_SLACK = 1.02
ATTN_BOUND_MAX = 40.0
TM_FFN = 1024
TF_FFN = D_FF // 2
MOE_BLOCK = 1024
MXU_TILE = 256
TF_MOE = 2 * MXU_TILE
SC_WINDOW = 128
ROW_CHUNKS = 4
ROW_CHUNK_W = D_MODEL // ROW_CHUNKS
PACK_CHUNK_W = ROW_CHUNK_W // 2

SEGSUM_SPLITS = 1

NN = (((1,), (0,)), ((), ()))
NT = (((1,), (1,)), ((), ()))


def _dot(a, b, dims=NN):
    return lax.dot_general(a, b, dims, preferred_element_type=F32)


def _split2(a):
    hi = a.astype(BF16)
    lo = (a - hi.astype(F32)).astype(BF16)
    return hi, lo


def _mm(a, b, dims=NN, passes=3):
    if passes == 1:
        return _dot(a.astype(BF16), b.astype(BF16), dims)
    ah, al = _split2(a)
    bh, bl = _split2(b)
    return _dot(ah, bh, dims) + (_dot(ah, bl, dims) + _dot(al, bh, dims))


def _mm_exact_rhs(a, b_bf16, splits=SEGSUM_SPLITS):
    m, w = a.shape
    nb = w // LANES
    stacked = jnp.concatenate([a[:, i * LANES:(i + 1) * LANES] for i in range(nb)], axis=0)
    out = None
    rem = stacked
    for s in range(splits):
        part = rem.astype(BF16)
        term = _dot(part, b_bf16)
        out = term if out is None else out + term
        if s + 1 < splits:
            rem = rem - part.astype(F32)
    return jnp.concatenate([out[i * m:(i + 1) * m] for i in range(nb)], axis=1)


def _pack_bf16_pairs(h):
    w = h.shape[1] // 2
    lo = lax.bitcast_convert_type(h[:, :w].astype(BF16).astype(F32), jnp.uint32)
    hi = lax.bitcast_convert_type(h[:, w:].astype(BF16).astype(F32), jnp.uint32)
    return (lo >> 16) | (hi & jnp.uint32(0xFFFF0000))


def _unpack_bf16_pairs(p):
    lo = lax.bitcast_convert_type(p << 16, F32)
    hi = lax.bitcast_convert_type(p & jnp.uint32(0xFFFF0000), F32)
    return jnp.concatenate([lo, hi], axis=1).astype(BF16)


def _sigmoid(x):
    return 1.0 / (1.0 + jnp.exp(-x))


def _silu(x):
    return x * _sigmoid(x)


def _rms(x, eps=NORM_EPS):
    return x * lax.rsqrt(jnp.mean(x * x, axis=-1, keepdims=True) + eps)


def _cparams(sem):
    return pltpu.CompilerParams(dimension_semantics=sem, vmem_limit_bytes=VMEM_LIMIT)


def _adaln_kernel(c_ref, w_ref, b_ref, o_ref):
    ca = _silu(c_ref[...])
    o_ref[0] = _mm(ca, w_ref[0]) + b_ref[0]


def _adaln(c, w_ada, b_ada):
    L = w_ada.shape[0]
    B = c.shape[0]
    n = w_ada.shape[2] // D_MODEL
    return pl.pallas_call(
        _adaln_kernel,
        grid=(L, n),
        in_specs=[pl.BlockSpec((B, D_MODEL), lambda l, j: (0, 0)),
                  pl.BlockSpec((1, D_MODEL, D_MODEL), lambda l, j: (l, 0, j)),
                  pl.BlockSpec((1, 1, D_MODEL), lambda l, j: (l, 0, j))],
        out_specs=pl.BlockSpec((1, B, D_MODEL), lambda l, j: (l, 0, j)),
        out_shape=jax.ShapeDtypeStruct((L, B, n * D_MODEL), F32),
        compiler_params=_cparams(("arbitrary", "arbitrary")),
    )(c, w_ada, b_ada.reshape(L, 1, -1))


def _rope(x, cosf, sinf, lane):
    up = pltpu.roll(x, LANES - MLA_QK_ROPE // 2, axis=1)
    dn = pltpu.roll(x, MLA_QK_ROPE // 2, axis=1)
    rot = jnp.where(lane < MLA_QK_NOPE + MLA_QK_ROPE // 2, -up, dn)
    return x * cosf + rot * sinf


def _rope_kernel(pos_ref, freq_ref, cos_ref, sin_ref):
    ang = pos_ref[0].astype(F32) * freq_ref[...]
    cos_ref[0] = jnp.cos(ang)
    sin_ref[0] = jnp.sin(ang)


def _rope_tables(positions, inv_freq):
    B, T = positions.shape
    nf = inv_freq.shape[0]
    per_row = LANES // nf
    pos_p = jnp.broadcast_to(positions[:, :, None], (B, T, nf)).reshape(B, T // per_row, LANES)
    freq_p = jnp.tile(inv_freq, per_row).reshape(1, LANES)
    blk = pl.BlockSpec((1, T // per_row, LANES), lambda b: (b, 0, 0))
    cos_p, sin_p = pl.pallas_call(
        _rope_kernel,
        grid=(B,),
        in_specs=[blk, pl.BlockSpec((1, LANES), lambda b: (0, 0))],
        out_specs=[blk, blk],
        out_shape=[jax.ShapeDtypeStruct((B, T // per_row, LANES), F32)] * 2,
        compiler_params=_cparams(("arbitrary",)),
    )(pos_p, freq_p)

    def spread(tab, fill):
        t16 = tab.reshape(B, T, nf)
        return jnp.concatenate([jnp.full((B, T, MLA_QK_NOPE), fill, F32), t16, t16,
                                jnp.full((B, T, LANES - MLA_QK_DIM), fill, F32)], axis=-1)

    return spread(cos_p, 1.0), spread(sin_p, 0.0)


def _mixin_kernel(x_ref, mod_ref, gain_ref, win_ref, cos_ref, sin_ref, poolw_ref, pools_ref,
                  qg_ref, kvg_ref, wq_ref, wk_ref, wv_ref, qkg_ref, bd_ref, mu_ref,
                  zr_ref, yp_ref, q_ref, k_ref, v_ref, ubuf, zlast):
    i = pl.program_id(1)
    tm = x_ref.shape[1]
    x = x_ref[0]
    mod = mod_ref[0]
    @pl.when(i == 0)
    def _():
        ubuf[0:POOL_HALO, :] = jnp.zeros((POOL_HALO, POOL_WIDTH), F32)
        zlast[...] = jnp.zeros(zlast.shape, F32)

    h = (_rms(x) * gain_ref[...] * (1.0 + mod[1:2]) + mod[0:1]).astype(BF16)
    zb = _dot(h, win_ref[:, ZR_COLS:])
    zcol = lambda off, w: zb[:, off - ZR_COLS:off - ZR_COLS + w]
    def project_piece(n):
        cols = slice(n * MIXIN_PIECE_W, min((n + 1) * MIXIN_PIECE_W, ZR_COLS))
        zp = _dot(h, win_ref[:, cols])
        row = lax.broadcasted_iota(jnp.int32, zp.shape, 0)
        prev = jnp.where(row == 0, zlast[SUBLANES - 1:SUBLANES, cols], pltpu.roll(zp, 1, axis=0))
        zlast[:, cols] = zp[tm - SUBLANES:, :]
        zr_ref[0, :, cols] = zp + mu_ref[:, cols] * (prev - zp)

    project_piece(0)
    u = zcol(Z_POOL_OFF, POOL_WIDTH)
    ubuf[POOL_HALO:, :] = u
    ue = ubuf[...]
    s2 = ue + pltpu.roll(ue, 1, axis=0)
    s4 = s2 + pltpu.roll(s2, 2, axis=0)
    s8 = s4 + pltpu.roll(s4, 4, axis=0)
    s16 = s8 + pltpu.roll(s8, 8, axis=0)
    ubuf[0:POOL_HALO, :] = u[tm - POOL_HALO:, :]
    lane_p = lax.broadcasted_iota(jnp.int32, (tm, POOL_WIDTH), 1)
    grp = lane_p // (POOL_WIDTH // len(POOL_WINDOWS))
    win_sum = jnp.where(grp == 0, s2[POOL_HALO:], jnp.where(grp == 1, s4[POOL_HALO:],
                        jnp.where(grp == 2, s8[POOL_HALO:], s16[POOL_HALO:])))
    win = jnp.where(grp == 0, 2, jnp.where(grp == 1, 4, jnp.where(grp == 2, 8, 16)))
    t_abs = i * tm + lax.broadcasted_iota(jnp.int32, (tm, POOL_WIDTH), 0)
    cnt = jnp.minimum(t_abs + 1, win).astype(F32)
    p = win_sum / cnt - u
    yp = _dot(p.astype(BF16), poolw_ref[...]) * pools_ref[...]
    yp_ref[0] = yp.astype(BF16)
    project_piece(1)

    lane = lax.broadcasted_iota(jnp.int32, (tm, LANES), 1)
    cosf = cos_ref[0]
    sinf = sin_ref[0]

    q_lat = zcol(Z_QLAT_OFF, MLA_Q_LORA)
    kv_lat = zcol(Z_KVLAT_OFF, MLA_KV_LORA)
    k_rope = zcol(Z_KROPE_OFF, LANES)
    qn = (_rms(q_lat) * qg_ref[...]).astype(BF16)
    kvn = (_rms(kv_lat) * kvg_ref[...]).astype(BF16)
    q = _dot(qn, wq_ref[...])
    kx = _dot(kvn, wk_ref[...])
    v = _dot(kvn, wv_ref[...])
    v_ref[0] = v.astype(BF16)
    project_piece(2)
    k_pe = _rope(k_rope, cosf, sinf, lane)
    qs, ks = [], []
    for hd in range(MLA_HEADS):
        sl = slice(hd * LANES, (hd + 1) * LANES)
        qs.append(_rope(q[:, sl], cosf, sinf, lane))
        ks.append(kx[:, sl] + k_pe)
    q = jnp.concatenate(qs, axis=1)
    k = jnp.concatenate(ks, axis=1)
    project_piece(3)
    qss = _mm_exact_rhs(q * q, bd_ref[...]) * (1.0 / MLA_QK_DIM)
    kss = _mm_exact_rhs(k * k, bd_ref[...]) * (1.0 / MLA_QK_DIM)
    qkg = qkg_ref[...]
    q = q * lax.rsqrt(qss + NORM_EPS) * qkg[0:1] * (MLA_QK_DIM ** -0.5)
    k = k * lax.rsqrt(kss + NORM_EPS) * qkg[1:2]
    q_ref[0] = q.astype(BF16)
    k_ref[0] = k.astype(BF16)


def _mixin(x, mod_l, gain, win, cosf, sinf, poolw, pools, qg, kvg, wq, wk, wv, qkg, bd128, mu):
    B, T, _ = x.shape
    tm = TM_MIX
    const = lambda shape: pl.BlockSpec(shape, lambda b, i: tuple(0 for _ in shape))
    tok = lambda w: pl.BlockSpec((1, tm, w), lambda b, i: (b, i, 0))
    return pl.pallas_call(
        _mixin_kernel,
        grid=(B, T // tm),
        in_specs=[tok(D_MODEL),
                  pl.BlockSpec((1, 6, D_MODEL), lambda b, i: (b, 0, 0)),
                  const((1, D_MODEL)), const((D_MODEL, Z_COLS)),
                  tok(LANES), tok(LANES),
                  const((POOL_WIDTH, POOL_WIDTH)), const((1, POOL_WIDTH)),
                  const((1, MLA_Q_LORA)), const((1, MLA_KV_LORA)),
                  const((MLA_Q_LORA, MLA_HEADS * LANES)), const((MLA_KV_LORA, MLA_HEADS * LANES)),
                  const((MLA_KV_LORA, MLA_HEADS * HEAD_DIM)), const((2, MLA_HEADS * LANES)),
                  const((LANES, LANES)), const((1, ZR_COLS))],
        out_specs=[tok(ZR_COLS), tok(POOL_WIDTH), tok(MLA_HEADS * LANES), tok(MLA_HEADS * LANES),
                   tok(MLA_HEADS * HEAD_DIM)],
        out_shape=[jax.ShapeDtypeStruct((B, T, ZR_COLS), F32),
                   jax.ShapeDtypeStruct((B, T, POOL_WIDTH), BF16),
                   jax.ShapeDtypeStruct((B, T, MLA_HEADS * LANES), BF16),
                   jax.ShapeDtypeStruct((B, T, MLA_HEADS * LANES), BF16),
                   jax.ShapeDtypeStruct((B, T, MLA_HEADS * HEAD_DIM), BF16)],
        scratch_shapes=[pltpu.VMEM((POOL_HALO + tm, POOL_WIDTH), F32), pltpu.VMEM((SUBLANES, ZR_COLS), F32)],
        compiler_params=_cparams(("arbitrary", "arbitrary")),
    )(x, mod_l, gain, win, cosf, sinf, poolw, pools, qg, kvg, wq, wk, wv, qkg, bd128, mu)


WKV_PASSES_SCORE = 1
WKV_PASSES_INV = 1
WKV_PASSES_APPLY = 1
WKV_PASSES_STATE = 1
WKV_STEP_CHUNKS = 1
WKV_STEP_SEQS = 8
WKV_GROUPS = 2
WKV_GROUP_LEAD = 4


def _stack_heads(xp, lane):
    return jnp.concatenate([jnp.where(lane < HEAD_DIM, xp, 0.0),
                            jnp.where(lane >= HEAD_DIM, xp, 0.0)], axis=0)


def _wkv_prep(r, lw, k, v, kk, a, tri, masks):
    L = r[0].shape[0]
    nc = len(r)
    each = lambda f, *ls: [f(*xs) for xs in zip(*ls)]
    lane = lax.broadcasted_iota(jnp.int32, (L, LANES), 1)
    stack = lambda x: _stack_heads(x, lane)
    cum = each(lambda x: _mm_exact_rhs_left(tri, x), lw)
    cum_last = each(lambda c: c[L - 1:L, :], cum)
    e_w = each(jnp.exp, cum)
    e_wm = each(lambda c, x: jnp.exp(c - x), cum, lw)
    e_iw = each(lambda c: jnp.exp(-c), cum)
    e_d = each(lambda cl, c: jnp.exp(cl - c), cum_last, cum)
    beta = each(lambda x, y: x * y, kk, a)
    r_f = each(lambda x, e: x * e, r, e_w)
    a_f = each(lambda x, e: -x * e, kk, e_wm)
    a_s = each(stack, a_f)
    b_s = each(lambda x, e: stack(x * e), beta, e_iw)
    k_s = each(lambda x, e: stack(x * e), k, e_iw)
    b_d = each(lambda x, e: stack(x * e), beta, e_d)
    k_d = each(lambda x, e: stack(x * e), k, e_d)
    v_s = each(stack, v)
    yield
    g = each(lambda af, rf, bs, ks: _mm(jnp.concatenate([af, rf], axis=0),
                                        jnp.concatenate([bs, ks], axis=0), NT, WKV_PASSES_SCORE),
             a_f, r_f, b_s, k_s)
    strict, incl, levels = masks
    a_ab = each(lambda x: jnp.where(strict, x[:L, :LANES], 0.0), g)
    a_ak = each(lambda x: jnp.where(strict, x[:L, LANES:], 0.0), g)
    s_rb = each(lambda x: jnp.where(incl, x[L:, :LANES], 0.0), g)
    s_rk = each(lambda x: jnp.where(incl, x[L:, LANES:], 0.0), g)
    eye = jnp.where(levels[0][1], 1.0, 0.0)
    tinv = each(lambda x: eye + jnp.where(levels[0][0], x, 0.0), a_ab)
    yield
    for lvl_mask, _ in levels[1:]:
        et = each(lambda x, t: _mm(jnp.where(lvl_mask, x, 0.0), stack(t), NN, WKV_PASSES_INV), a_ab, tinv)
        tinv = each(lambda t, x: t + _mm(t, stack(x), NN, WKV_PASSES_INV), tinv, et)
        yield
    av = each(lambda x, y: _mm(x, y, NN, WKV_PASSES_APPLY), a_ak, v_s)
    tx = each(lambda t, x, y: _mm(t, jnp.concatenate([x, stack(y)], axis=1), NN, WKV_PASSES_APPLY),
              tinv, a_s, av)
    ta_s = each(lambda x: stack(x[:, :LANES]), tx)
    c1_s = each(lambda x: stack(x[:, LANES:]), tx)
    yield
    ra = each(lambda rf, s, x: rf + _mm(s, x, NN, WKV_PASSES_APPLY), r_f, s_rb, ta_s)
    c2 = each(lambda sb, sk, x, vs: _mm(jnp.concatenate([sb, sk], axis=1),
                                        jnp.concatenate([x, vs], axis=0),
                                        NN, WKV_PASSES_APPLY), s_rb, s_rk, c1_s, v_s)
    yield
    tb = each(lambda x, y, bd: _mm(jnp.concatenate([x, y], axis=1).T, bd, NN, WKV_PASSES_APPLY),
              ta_s, c1_s, b_d)
    c3 = each(lambda x, vs, kd: x[LANES:] + _mm(vs.T, kd, NN, WKV_PASSES_APPLY), tb, v_s, k_d)
    return [(ra[i], c2[i], jnp.exp(cum_last[i]), tb[i][:LANES], c3[i]) for i in range(nc)]


def _mm_exact_rhs_left(tri_bf16, x):
    x0 = x.astype(BF16)
    x1 = (x - x0.astype(F32)).astype(BF16)
    return _dot(tri_bf16, x0) + _dot(tri_bf16, x1)


def _wkv_masks(L):
    row = lax.broadcasted_iota(jnp.int32, (L, 2 * L), 0)
    col = lax.broadcasted_iota(jnp.int32, (L, 2 * L), 1) % L
    strict = row > col
    incl = row >= col
    levels = []
    m = 1
    while m < L:
        same = (row // (2 * m)) == (col // (2 * m))
        lvl = same & ((row % (2 * m)) >= m) & ((col % (2 * m)) < m)
        levels.append((lvl, row == col))
        m *= 2
    return strict, incl, levels


def _rwkv_kernel(has_vres, *refs):
    if has_vres:
        (z_ref, vf_ref, vec_ref, w2a2_ref, g2v2_ref, bd_ref,
         y_ref, state) = refs
    else:
        (z_ref, vec_ref, w2a2_ref, g2v2_ref, bd_ref,
         y_ref, vout_ref, state) = refs
    c = pl.program_id(1)
    n_seq, seq_rows = z_ref.shape[0], z_ref.shape[1]
    rows = n_seq * seq_rows
    L = WKV_CHUNK
    W = RWKV_WIDTH

    @pl.when(c == 0)
    def _():
        state[...] = jnp.zeros(state.shape, F32)

    flat = lambda ref: jnp.concatenate([ref[s] for s in range(n_seq)], axis=0)
    zs_all = flat(z_ref)
    if has_vres:
        vf_all = flat(vf_ref)
    vec = vec_ref[...]
    w0, a0, k_k, k_a, r_k, ln_g, ln_b, v0 = (vec[j:j + 1] for j in range(8))
    bd = bd_ref[...]
    masks = _wkv_masks(L)
    rowt = lax.broadcasted_iota(jnp.int32, (L, L), 0)
    colt = lax.broadcasted_iota(jnp.int32, (L, L), 1)
    tri = jnp.where(rowt >= colt, 1.0, 0.0).astype(BF16)
    n_pairs = RWKV_HEADS // 2
    S_now = [{(s, p): state[s * n_pairs + p] for s in range(n_seq) for p in range(n_pairs)}]

    def group(s0, s1):
        seqs = range(s0, s1)
        r0, r1 = s0 * seq_rows, s1 * seq_rows
        zs = zs_all[r0:r1]
        r = zs[:, 0:W]
        k = zs[:, W:2 * W]
        v = zs[:, 2 * W:3 * W]
        wa = zs[:, 3 * W:3 * W + LANES]
        gb = zs[:, 3 * W + LANES:ZR_COLS]
        lane_a = lax.broadcasted_iota(jnp.int32, wa.shape, 1)
        t1 = _dot(jnp.where(lane_a < RWKV_DECAY_LORA, jnp.tanh(wa), wa).astype(BF16), w2a2_ref[...])
        lane_g = lax.broadcasted_iota(jnp.int32, gb.shape, 1)
        t2 = _dot(jnp.where(lane_g < RWKV_GATE_LORA, _sigmoid(gb), gb).astype(BF16), g2v2_ref[...])
        yield
        xw = w0 + t1[:, :W]
        lw = -np.exp(-0.5).astype(np.float32) * _sigmoid(xw)
        a = _sigmoid(a0 + t1[:, W:])
        g = t2[:, :W]
        if has_vres:
            v = v + (vf_all[r0:r1] - v) * _sigmoid(v0 + t2[:, W:])
        else:
            for s in seqs:
                vout_ref[s] = v[(s - s0) * seq_rows:(s - s0 + 1) * seq_rows]
        kk = k * k_k
        kk = kk * jnp.minimum(lax.rsqrt(_mm_exact_rhs(kk * kk, bd)), 1e12)
        k = k * (1.0 + (a - 1.0) * k_a)
        yield
        n_chunks = seq_rows // L
        idx = [(s, ch, p) for s in seqs for ch in range(n_chunks) for p in range(n_pairs)]
        row0 = lambda s, ch: (s - s0) * seq_rows + ch * L
        cut = lambda x: [x[row0(s, ch):row0(s, ch) + L, p * LANES:(p + 1) * LANES] for s, ch, p in idx]
        res = yield from _wkv_prep(cut(r), cut(lw), cut(k), cut(v), cut(kk), cut(a), tri, masks)
        prep = dict(zip(idx, res))
        yield
        sp = [(s, p) for s in seqs for p in range(n_pairs)]
        S = {key: S_now[0][key] for key in sp}
        y_blk = {}
        for ch in range(n_chunks):
            y_s = {(s, p): _mm(prep[s, ch, p][0], S[s, p], NT, WKV_PASSES_STATE) + prep[s, ch, p][1]
                   for s, p in sp}
            for s in seqs:
                y_blk[s, ch] = jnp.concatenate([y_s[s, p] for p in range(n_pairs)], axis=1)
            S = {(s, p): S[s, p] * prep[s, ch, p][2] + _mm(S[s, p], prep[s, ch, p][3], NN, WKV_PASSES_STATE)
                 + prep[s, ch, p][4] for s, p in sp}
        S_now[0].update(S)
        y = jnp.concatenate([y_blk[s, ch] for s in seqs for ch in range(n_chunks)], axis=0)
        yield
        inv = 1.0 / HEAD_DIM
        mean = _mm_exact_rhs(y, bd) * inv
        yc = y - mean
        var = _mm_exact_rhs(yc * yc, bd) * inv
        yn = yc * lax.rsqrt(var + RWKV_LNX_EPS) * ln_g + ln_b
        bonus = _mm_exact_rhs(r * k * r_k, bd) * v
        out = ((yn + bonus) * g).astype(BF16)
        for s in seqs:
            y_ref[s] = out[(s - s0) * seq_rows:(s - s0 + 1) * seq_rows]

    per = n_seq // WKV_GROUPS
    gens = [group(gi * per, (gi + 1) * per) for gi in range(WKV_GROUPS)]
    alive = [True] * WKV_GROUPS
    tick = 0
    while any(alive):
        for gi in range(WKV_GROUPS):
            if alive[gi] and tick >= gi * WKV_GROUP_LEAD:
                try:
                    next(gens[gi])
                except StopIteration:
                    alive[gi] = False
        tick += 1
    for s in range(n_seq):
        for p in range(n_pairs):
            state[s * n_pairs + p] = S_now[0][s, p]


def _rwkv(zr, v_first, vec8, w2a2, g2v2, bd64):
    B, T, _ = zr.shape
    L = WKV_CHUNK * WKV_STEP_CHUNKS
    ns = WKV_STEP_SEQS
    has_vres = v_first is not None
    const = lambda shape: pl.BlockSpec(shape, lambda b, c: tuple(0 for _ in shape))
    tok = lambda w: pl.BlockSpec((ns, L, w), lambda b, c: (b, c, 0))
    in_specs = [tok(ZR_COLS)]
    args = [zr]
    if has_vres:
        in_specs.append(tok(RWKV_WIDTH))
        args.append(v_first)
    in_specs += [const((8, RWKV_WIDTH)), const((LANES, 2 * RWKV_WIDTH)),
                 const((2 * LANES, 2 * RWKV_WIDTH)), const((LANES, LANES))]
    args += [vec8, w2a2, g2v2, bd64]
    out_specs = [tok(RWKV_WIDTH)]
    out_shape = [jax.ShapeDtypeStruct((B, T, RWKV_WIDTH), BF16)]
    if not has_vres:
        out_specs.append(tok(RWKV_WIDTH))
        out_shape.append(jax.ShapeDtypeStruct((B, T, RWKV_WIDTH), F32))
    outs = pl.pallas_call(
        functools.partial(_rwkv_kernel, has_vres),
        grid=(B // ns, T // L),
        in_specs=in_specs, out_specs=out_specs, out_shape=out_shape,
        scratch_shapes=[pltpu.VMEM((ns * (RWKV_HEADS // 2), LANES, LANES), F32)],
        compiler_params=_cparams(("arbitrary", "arbitrary")),
    )(*args)
    return (outs[0], v_first) if has_vres else (outs[0], outs[1])


def _attn_step(q_ref, k_ref, v_ref, m_sc, l_sc, acc_sc, masked):
    tq = q_ref.shape[1]
    tk = k_ref.shape[1]
    lane = lax.broadcasted_iota(jnp.int32, (tq, LANES), 1)
    if masked:
        rowi = lax.broadcasted_iota(jnp.int32, (tq, tk), 0)
        coli = lax.broadcasted_iota(jnp.int32, (tq, tk), 1)
        keep = coli <= rowi
    heads = range(MLA_HEADS)
    s = [_dot(q_ref[0, :, hd * LANES:(hd + 1) * LANES], k_ref[0, :, hd * LANES:(hd + 1) * LANES], NT)
         for hd in heads]
    if masked:
        s = [jnp.where(keep, x, NEG_INF) for x in s]
    m_prev = [m_sc[hd] for hd in heads]
    m_new = [jnp.maximum(m_prev[hd], jnp.max(s[hd], axis=-1, keepdims=True)) for hd in heads]
    alpha = [jnp.exp(m_prev[hd] - m_new[hd]) for hd in heads]
    p = [jnp.exp(s[hd] - jnp.concatenate([m_new[hd]] * (tk // LANES), axis=1)) for hd in heads]
    for hd in heads:
        l_sc[hd] = alpha[hd] * l_sc[hd] + jnp.sum(p[hd], axis=-1, keepdims=True)
        m_sc[hd] = m_new[hd]
    pv = [_dot(p[hd].astype(BF16), v_ref[0, :, (hd // 2) * LANES:(hd // 2 + 1) * LANES]) for hd in heads]
    first = lane < HEAD_DIM
    for pr in range(MLA_HEADS // 2):
        acc_sc[pr] = (acc_sc[pr] * jnp.where(first, alpha[2 * pr], alpha[2 * pr + 1])
                      + jnp.where(first, pv[2 * pr], pv[2 * pr + 1]))


def _attn_step_bounded(q_ref, k_ref, v_ref, cb_ref, l_sc, acc_sc, masked):
    tq = q_ref.shape[1]
    tk = k_ref.shape[1]
    lane = lax.broadcasted_iota(jnp.int32, (tq, LANES), 1)
    if masked:
        rowi = lax.broadcasted_iota(jnp.int32, (tq, tk), 0)
        coli = lax.broadcasted_iota(jnp.int32, (tq, tk), 1)
        keep = coli <= rowi
    ps = []
    for hd in range(MLA_HEADS):
        s = _dot(q_ref[0, :, hd * LANES:(hd + 1) * LANES], k_ref[0, :, hd * LANES:(hd + 1) * LANES], NT)
        c = cb_ref[0, hd:hd + 1, :]
        p = jnp.exp(s - jnp.concatenate([c] * (tk // LANES), axis=1))
        if masked:
            p = jnp.where(keep, p, 0.0)
        part = p[:, 0:LANES]
        for t in range(1, tk // LANES):
            part = part + p[:, t * LANES:(t + 1) * LANES]
        l_sc[hd] = l_sc[hd] + part
        ps.append(p.astype(BF16))
    first = lane < HEAD_DIM
    for pr in range(MLA_HEADS // 2):
        pv = _dot(jnp.concatenate(ps[2 * pr:2 * pr + 2], axis=0), v_ref[0, :, pr * LANES:(pr + 1) * LANES])
        acc_sc[pr] = acc_sc[pr] + jnp.where(first, pv[:tq], pv[tq:])


def _attn_finish(o_ref, l_sc, acc_sc, lane_partial):
    tq = o_ref.shape[1]
    lane = lax.broadcasted_iota(jnp.int32, (tq, LANES), 1)
    outs = []
    for pr in range(MLA_HEADS // 2):
        la, lb = l_sc[2 * pr], l_sc[2 * pr + 1]
        if lane_partial:
            la = jnp.sum(la, axis=-1, keepdims=True)
            lb = jnp.sum(lb, axis=-1, keepdims=True)
        outs.append(acc_sc[pr] / jnp.where(lane < HEAD_DIM, la, lb))
    o_ref[0] = jnp.concatenate(outs, axis=1).astype(BF16)


def _attn_kernel(ok_ref, qi_ref, kj_ref, q_ref, k_ref, v_ref, cb_ref, o_ref, m_sc, l_sc, acc_sc):
    i = qi_ref[pl.program_id(1)]
    j = kj_ref[pl.program_id(1)]
    bounded = ok_ref[pl.program_id(0)] == 1
    exact = jnp.logical_not(bounded)

    @pl.when(j == 0)
    def _():
        m_sc[...] = jnp.full(m_sc.shape, NEG_INF, F32)
        l_sc[...] = jnp.zeros(l_sc.shape, F32)
        acc_sc[...] = jnp.zeros(acc_sc.shape, F32)

    @pl.when(bounded & (j < i))
    def _():
        _attn_step_bounded(q_ref, k_ref, v_ref, cb_ref, l_sc, acc_sc, masked=False)

    @pl.when(bounded & (j == i))
    def _():
        _attn_step_bounded(q_ref, k_ref, v_ref, cb_ref, l_sc, acc_sc, masked=True)
        _attn_finish(o_ref, l_sc, acc_sc, lane_partial=True)

    @pl.when(exact & (j < i))
    def _():
        _attn_step(q_ref, k_ref, v_ref, m_sc, l_sc, acc_sc, masked=False)

    @pl.when(exact & (j == i))
    def _():
        _attn_step(q_ref, k_ref, v_ref, m_sc, l_sc, acc_sc, masked=True)
        _attn_finish(o_ref, l_sc, acc_sc, lane_partial=False)


def _attention(q, k, v, qk_gain):
    B, T, _ = q.shape
    nq = T // TQ
    gmax = jnp.max(jnp.abs(qk_gain), axis=1)
    c = gmax[0] * gmax[1] * (MLA_QK_DIM ** 0.5) * ATTN_BOUND_SLACK
    ok = jnp.broadcast_to((c <= ATTN_BOUND_MAX).astype(jnp.int32), (B,))
    cb = jnp.broadcast_to(c, (B, MLA_HEADS, LANES))
    pairs = [(i, j) for i in range(nq) for j in range(i + 1)]
    qi = jnp.asarray([p[0] for p in pairs], jnp.int32)
    kj = jnp.asarray([p[1] for p in pairs], jnp.int32)
    grid_spec = pltpu.PrefetchScalarGridSpec(
        num_scalar_prefetch=3,
        grid=(B, len(pairs)),
        in_specs=[pl.BlockSpec((1, TQ, MLA_HEADS * LANES), lambda b, t, ok, qi, kj: (b, qi[t], 0)),
                  pl.BlockSpec((1, TQ, MLA_HEADS * LANES), lambda b, t, ok, qi, kj: (b, kj[t], 0)),
                  pl.BlockSpec((1, TQ, MLA_HEADS * HEAD_DIM), lambda b, t, ok, qi, kj: (b, kj[t], 0)),
                  pl.BlockSpec((1, MLA_HEADS, LANES), lambda b, t, ok, qi, kj: (b, 0, 0))],
        out_specs=pl.BlockSpec((1, TQ, MLA_HEADS * HEAD_DIM), lambda b, t, ok, qi, kj: (b, qi[t], 0)),
        scratch_shapes=[pltpu.VMEM((MLA_HEADS, TQ, LANES), F32),
                        pltpu.VMEM((MLA_HEADS, TQ, LANES), F32),
                        pltpu.VMEM((MLA_HEADS // 2, TQ, LANES), F32)])
    return pl.pallas_call(
        _attn_kernel,
        grid_spec=grid_spec,
        out_shape=jax.ShapeDtypeStruct((B, T, MLA_HEADS * HEAD_DIM), BF16),
        compiler_params=_cparams(("arbitrary", "arbitrary")),
    )(ok, qi, kj, q, k, v, cb)


def _mixout_kernel(has_router, *refs):
    if has_router:
        (x_ref, yr_ref, yp_ref, ym_ref, wo_ref, mod_ref, gain_ref, rt_ref, tri_ref,
         xo_ref, h_ref, route_ref, cnt_ref, cnt_sc) = refs
    else:
        x_ref, yr_ref, yp_ref, ym_ref, wo_ref, mod_ref, gain_ref, xo_ref, h_ref = refs
    mod = mod_ref[0]
    o1 = RWKV_WIDTH
    o2 = RWKV_WIDTH + POOL_WIDTH
    mix = (_dot(yr_ref[0], wo_ref[0:o1, :]) + _dot(yp_ref[0], wo_ref[o1:o2, :])
           + _dot(ym_ref[0], wo_ref[o2:, :]))
    x = x_ref[0] + mod[2:3] * mix
    xo_ref[0] = x
    h = _rms(x) * gain_ref[...] * (1.0 + mod[4:5]) + mod[3:4]
    if not has_router:
        h_ref[0] = h.astype(BF16)
    else:
        hp = _pack_bf16_pairs(h)
        for ck in range(ROW_CHUNKS):
            h_ref[ck, 0] = hp[:, ck * PACK_CHUNK_W:(ck + 1) * PACK_CHUNK_W]
        logits = _dot(h.astype(BF16), rt_ref[...])
        lane = lax.broadcasted_iota(jnp.int32, logits.shape, 1).astype(F32)
        lg = jnp.where(lane < N_EXPERTS, logits, -jnp.inf)
        m1 = jnp.max(lg, axis=-1, keepdims=True)
        i1 = jnp.min(jnp.where(lg == m1, lane, float(LANES)), axis=-1, keepdims=True)
        lg2 = jnp.where(lane == i1, -jnp.inf, lg)
        m2 = jnp.max(lg2, axis=-1, keepdims=True)
        i2 = jnp.min(jnp.where(lg2 == m2, lane, float(LANES)), axis=-1, keepdims=True)
        e2 = jnp.exp(m2 - m1)
        g1 = 1.0 / (1.0 + e2)
        g2 = e2 / (1.0 + e2)
        first = (pl.program_id(0) == 0) & (pl.program_id(1) == 0)

        @pl.when(first)
        def _():
            cnt_sc[...] = jnp.zeros(cnt_sc.shape, F32)

        hit1 = lane == i1
        hit2 = lane == i2
        onehot = jnp.where(hit1 | hit2, 1.0, 0.0)
        prefix = _dot(tri_ref[...], onehot.astype(BF16)) + cnt_sc[0:1, :]
        r1 = jnp.sum(jnp.where(hit1, prefix, 0.0), axis=-1, keepdims=True)
        r2 = jnp.sum(jnp.where(hit2, prefix, 0.0), axis=-1, keepdims=True)
        cnt_sc[...] = cnt_sc[...] + jnp.sum(onehot, axis=0, keepdims=True)
        cnt_ref[...] = cnt_sc[...]
        vals = (i1, i2, g1, g2, r1, r2)
        route = jnp.zeros(logits.shape, F32)
        for pos, val in enumerate(vals):
            route = jnp.where(lane == pos, val, route)
        route_ref[0] = route


def _mixout(x, yr, yp, ym, wo, mod_l, gain, router_p):
    B, T, _ = x.shape
    tm = TM_OUT
    has_router = router_p is not None
    const = lambda shape: pl.BlockSpec(shape, lambda b, i: tuple(0 for _ in shape))
    tok = lambda w: pl.BlockSpec((1, tm, w), lambda b, i: (b, i, 0))
    in_specs = [tok(D_MODEL), tok(RWKV_WIDTH), tok(POOL_WIDTH), tok(MLA_HEADS * HEAD_DIM),
                const((D_MODEL, D_MODEL)), pl.BlockSpec((1, 6, D_MODEL), lambda b, i: (b, 0, 0)),
                const((1, D_MODEL))]
    args = [x, yr, yp, ym, wo, mod_l, gain]
    out_specs = [tok(D_MODEL), tok(D_MODEL)]
    out_shape = [jax.ShapeDtypeStruct((B, T, D_MODEL), F32), jax.ShapeDtypeStruct((B, T, D_MODEL), BF16)]
    scratch = []
    if has_router:
        out_specs[1] = pl.BlockSpec((ROW_CHUNKS, 1, tm, PACK_CHUNK_W), lambda b, i: (0, b, i, 0))
        out_shape[1] = jax.ShapeDtypeStruct((ROW_CHUNKS, B, T, PACK_CHUNK_W), jnp.uint32)
        ids = np.arange(tm)
        tri = jnp.asarray(ids[:, None] > ids[None, :], BF16)
        in_specs += [const((D_MODEL, LANES)), const((tm, tm))]
        args += [router_p, tri]
        out_specs += [tok(LANES), const((SUBLANES, LANES))]
        out_shape += [jax.ShapeDtypeStruct((B, T, LANES), F32),
                      jax.ShapeDtypeStruct((SUBLANES, LANES), F32)]
        scratch = [pltpu.VMEM((SUBLANES, LANES), F32)]
    return pl.pallas_call(
        functools.partial(_mixout_kernel, has_router),
        grid=(B, T // tm),
        in_specs=in_specs, out_specs=out_specs, out_shape=out_shape, scratch_shapes=scratch,
        compiler_params=_cparams(("arbitrary", "arbitrary")),
    )(*args)


def _ffn_kernel(h_ref, wg_ref, wu_ref, wo_ref, x_ref, mod_ref, o_ref, acc):
    j = pl.program_id(1)

    @pl.when(j == 0)
    def _():
        acc[...] = jnp.zeros(acc.shape, F32)

    h = h_ref[...]
    gg = _dot(h, wg_ref[...])
    uu = _dot(h, wu_ref[...])
    acc[...] += _dot((_silu(gg) * uu).astype(BF16), wo_ref[...])

    @pl.when(j == pl.num_programs(1) - 1)
    def _():
        o_ref[...] = x_ref[...] + mod_ref[0][5:6] * acc[...]


def _ffn(h2, w_in, w_out, x, mod_l):
    N = h2.shape[0]
    T = N // mod_l.shape[0]
    tm, tf = TM_FFN, TF_FFN
    nf = D_FF // tf
    per_b = T // tm
    return pl.pallas_call(
        _ffn_kernel,
        grid=(N // tm, nf),
        in_specs=[pl.BlockSpec((tm, D_MODEL), lambda i, j: (i, 0)),
                  pl.BlockSpec((D_MODEL, tf), lambda i, j: (0, j)),
                  pl.BlockSpec((D_MODEL, tf), lambda i, j: (0, j + nf)),
                  pl.BlockSpec((tf, D_MODEL), lambda i, j: (j, 0)),
                  pl.BlockSpec((tm, D_MODEL), lambda i, j: (i, 0)),
                  pl.BlockSpec((1, 6, D_MODEL), lambda i, j: (i // per_b, 0, 0))],
        out_specs=pl.BlockSpec((tm, D_MODEL), lambda i, j: (i, 0)),
        out_shape=jax.ShapeDtypeStruct((N, D_MODEL), F32),
        scratch_shapes=[pltpu.VMEM((tm, D_MODEL), F32)],
        compiler_params=_cparams(("arbitrary", "arbitrary")),
    )(h2, w_in, w_in, w_out, x, mod_l)


def _moe_kernel(be_ref, nv_ref, last_ref, x_ref, wg_ref, wu_ref, wo_ref, o_ref, acc, xm):
    i = pl.program_id(0)
    j = pl.program_id(1)

    @pl.when(i <= last_ref[0])
    def _():
        @pl.when(j == 0)
        def _():
            acc[...] = jnp.zeros(acc.shape, F32)
            row = lax.broadcasted_iota(jnp.int32, (xm.shape[0], 1), 0)
            xp = jnp.concatenate([x_ref[ck] for ck in range(ROW_CHUNKS)], axis=1)
            xp = jnp.where(row < nv_ref[i], xp, jnp.uint32(0))
            xm[...] = _unpack_bf16_pairs(xp)

        def swiglu_rows(n_rows):
            x = xm[0:n_rows, :]
            gg = _dot(x, wg_ref[0].astype(BF16))
            uu = _dot(x, wu_ref[0].astype(BF16))
            acc[0:n_rows, :] += _dot((_silu(gg) * uu).astype(BF16), wo_ref[0].astype(BF16))

        half = xm.shape[0] // 2

        @pl.when(nv_ref[i] > half)
        def _():
            swiglu_rows(xm.shape[0])

        @pl.when(nv_ref[i] <= half)
        def _():
            swiglu_rows(half)

        @pl.when(j == pl.num_programs(1) - 1)
        def _():
            yp = _pack_bf16_pairs(acc[...])
            for ck in range(ROW_CHUNKS):
                o_ref[ck] = yp[:, ck * PACK_CHUNK_W:(ck + 1) * PACK_CHUNK_W]


def _moe_experts(xs, w_in, w_out, block_exp, n_valid, last_blk):
    n_rows = xs.shape[1]
    tm, tf = MOE_BLOCK, TF_MOE
    nf = D_FF_EXPERT // tf
    blk = lambda i, last: jnp.minimum(i, last[0])
    chunk = lambda i, j, last: jnp.where(i <= last[0], j, nf - 1)
    grid_spec = pltpu.PrefetchScalarGridSpec(
        num_scalar_prefetch=3,
        grid=(n_rows // tm, nf),
        in_specs=[pl.BlockSpec((ROW_CHUNKS, tm, PACK_CHUNK_W),
                               lambda i, j, be, nv, last: (0, blk(i, last), 0)),
                  pl.BlockSpec((1, D_MODEL, tf),
                               lambda i, j, be, nv, last: (be[blk(i, last)], 0, chunk(i, j, last))),
                  pl.BlockSpec((1, D_MODEL, tf),
                               lambda i, j, be, nv, last: (be[blk(i, last)], 0, chunk(i, j, last) + nf)),
                  pl.BlockSpec((1, tf, D_MODEL),
                               lambda i, j, be, nv, last: (be[blk(i, last)], chunk(i, j, last), 0))],
        out_specs=pl.BlockSpec((ROW_CHUNKS, tm, PACK_CHUNK_W), lambda i, j, be, nv, last: (0, blk(i, last), 0)),
        scratch_shapes=[pltpu.VMEM((tm, D_MODEL), F32), pltpu.VMEM((tm, D_MODEL), BF16)])
    return pl.pallas_call(
        _moe_kernel,
        grid_spec=grid_spec,
        out_shape=jax.ShapeDtypeStruct((ROW_CHUNKS, n_rows, PACK_CHUNK_W), jnp.uint32),
        compiler_params=_cparams(("arbitrary", "arbitrary")),
    )(block_exp, n_valid, last_blk, xs, w_in, w_in, w_out)


def _sc_mesh():
    return plsc.VectorSubcoreMesh(core_axis_name="c", subcore_axis_name="s")


def _sc_scatter_rows(x, dest, n_rows):
    N, D = x.shape
    K = dest.shape[0]
    win = SC_WINDOW

    @pl.kernel(out_type=jax.ShapeDtypeStruct((n_rows, D), x.dtype), mesh=_sc_mesh(), scratch_types=[])
    def scatter(x_hbm, d_hbm, o_hbm):
        def body(x_vmem, *idx_vmem):
            for iv in idx_vmem:
                pltpu.sync_copy(x_vmem, o_hbm.at[iv.at[0]])

        pltpu.emit_pipeline(
            body,
            grid=(N // win,),
            in_specs=[pl.BlockSpec((win, D), lambda i: (i, 0))]
            + [pl.BlockSpec((1, win), functools.partial(lambda k, i: (k, i), k)) for k in range(K)],
            out_specs=[],
            core_axis_name=("c", "s"),
            dimension_semantics=(pltpu.PARALLEL,),
        )(x_hbm, *([d_hbm] * K))

    return scatter(x, dest)


def _sc_gather_rows(x, idx):
    n = idx.shape[0]
    D = x.shape[1]
    win = SC_WINDOW

    @pl.kernel(out_type=jax.ShapeDtypeStruct((n, D), x.dtype), mesh=_sc_mesh(), scratch_types=[])
    def gather(x_hbm, i_hbm, o_hbm):
        def body(i_vmem, o_vmem):
            pltpu.sync_copy(x_hbm.at[i_vmem.at[0]], o_vmem)

        pltpu.emit_pipeline(
            body,
            grid=(n // win,),
            in_specs=[pl.BlockSpec((1, win), lambda i: (0, i))],
            out_specs=[pl.BlockSpec((win, D), lambda i: (i, 0))],
            core_axis_name=("c", "s"),
            dimension_semantics=(pltpu.PARALLEL,),
        )(i_hbm, o_hbm)

    return gather(x, idx.reshape(1, n))


def _combine_kernel(x_ref, ya_ref, yb_ref, route_ref, mod_ref, o_ref):
    rt = route_ref[...]
    ya = _unpack_bf16_pairs(jnp.concatenate([ya_ref[0, ck] for ck in range(ROW_CHUNKS)], axis=1))
    yb = _unpack_bf16_pairs(jnp.concatenate([yb_ref[0, ck] for ck in range(ROW_CHUNKS)], axis=1))
    f = rt[:, 2:3] * ya.astype(F32) + rt[:, 3:4] * yb.astype(F32)
    o_ref[...] = x_ref[...] + mod_ref[0][5:6] * f


def _combine(x, y2, route, mod_l):
    N = x.shape[0]
    T = N // mod_l.shape[0]
    tm = 1024
    per_b = T // tm
    tok = pl.BlockSpec((tm, D_MODEL), lambda i: (i, 0))
    slot = lambda k: pl.BlockSpec((1, ROW_CHUNKS, tm, PACK_CHUNK_W), lambda i: (k, 0, i, 0))
    return pl.pallas_call(
        _combine_kernel,
        grid=(N // tm,),
        in_specs=[tok, slot(0), slot(1),
                  pl.BlockSpec((tm, LANES), lambda i: (i, 0)),
                  pl.BlockSpec((1, 6, D_MODEL), lambda i: (i // per_b, 0, 0))],
        out_specs=tok,
        out_shape=jax.ShapeDtypeStruct((N, D_MODEL), F32),
        compiler_params=_cparams(("arbitrary",)),
    )(x, y2, y2, route, mod_l)


def _moe(h2, route, counts, w_in, w_out, x, mod_l):
    N = x.shape[0]
    blk = MOE_BLOCK
    cnt = counts[0, :N_EXPERTS].astype(jnp.int32)
    padded = (cnt + blk - 1) // blk * blk
    pend = jnp.cumsum(padded)
    pstart = pend - padded
    e = route[:, 0:TOP_K].astype(jnp.int32)
    rank = route[:, 2 * TOP_K:3 * TOP_K].astype(jnp.int32)
    dest = (jnp.take(pstart, e) + rank).T
    n_blocks = N * TOP_K // blk + N_EXPERTS
    bstart = jnp.arange(n_blocks, dtype=jnp.int32) * blk
    block_exp = jnp.minimum(jnp.sum((bstart[:, None] >= pend[None, :]).astype(jnp.int32), axis=1),
                            N_EXPERTS - 1)
    n_valid = jnp.clip(cnt[block_exp] - (bstart - pstart[block_exp]), 0, blk).astype(jnp.int32)
    n_rows = n_blocks * blk
    dest_ck = dest[:, None, :] + (jnp.arange(ROW_CHUNKS, dtype=jnp.int32) * n_rows)[None, :, None]
    xs = _sc_scatter_rows(h2, dest_ck.reshape(TOP_K, ROW_CHUNKS * N), ROW_CHUNKS * n_rows)
    last_blk = (pend[-1:] // blk - 1).astype(jnp.int32)
    yb = _moe_experts(xs.reshape(ROW_CHUNKS, n_rows, PACK_CHUNK_W), w_in, w_out, block_exp, n_valid,
                      last_blk)
    y2 = _sc_gather_rows(yb.reshape(ROW_CHUNKS * n_rows, PACK_CHUNK_W), dest_ck.reshape(-1))
    return _combine(x, y2.reshape(TOP_K, ROW_CHUNKS, N, PACK_CHUNK_W), route, mod_l)


def _layout_w_in(w, has_vres):
    W = RWKV_WIDTH
    off_gd = 3 * W + RWKV_DECAY_LORA + RWKV_ICLR_LORA
    off_pool = off_gd + RWKV_GATE_LORA
    off_q = off_pool + POOL_WIDTH
    off_kv = off_q + MLA_Q_LORA
    off_kr = off_kv + MLA_KV_LORA
    n_base = off_kr + MLA_QK_ROPE
    d = w.shape[0]
    zeros = lambda n: jnp.zeros((d, n), w.dtype)
    vd = w[:, n_base:n_base + RWKV_VRES_LORA] if has_vres else zeros(RWKV_VRES_LORA)
    cols = [w[:, :off_gd], w[:, off_gd:off_pool], vd, zeros(ZR_COLS - off_pool - RWKV_VRES_LORA),
            w[:, off_pool:off_q], w[:, off_q:off_kv], w[:, off_kv:off_kr],
            zeros(MLA_QK_NOPE), w[:, off_kr:n_base], zeros(LANES - MLA_QK_DIM)]
    return jnp.concatenate(cols, axis=1).astype(BF16)


def _pad_heads(w, per_head, keep_from, keep_n):
    K = w.shape[0]
    wh = w.reshape(K, MLA_HEADS, per_head)[:, :, keep_from:keep_from + keep_n]
    wh = jnp.pad(wh, ((0, 0), (0, 0), (0, LANES - keep_n)))
    return wh.reshape(K, MLA_HEADS * LANES)


def kernel(x, c, positions, w_ada, b_ada, norm_gain, w_in_first, w_in_rest, mu_shift, mu_shift_v,
           rwkv_vec, rwkv_v0, rwkv_w2, rwkv_a2, rwkv_g2, rwkv_v2, pool_w, pool_scale,
           mla_q_lat_gain, mla_kv_lat_gain, mla_wq_up, mla_wkv_up, mla_qk_gain, w_out, ffn_w_in,
           ffn_w_out, moe_router, moe_w_in, moe_w_out):
    B, T, D = x.shape
    depth = w_ada.shape[0]
    assert D == D_MODEL and B % WKV_STEP_SEQS == 0 and D_FF_EXPERT % TF_MOE == 0
    assert T % max(TM_MIX, TM_OUT, TQ, WKV_CHUNK * WKV_STEP_CHUNKS) == 0
    assert (B * T) % max(TM_FFN, MOE_BLOCK) == 0 and (B * T * TOP_K) % MOE_BLOCK == 0
    W = RWKV_WIDTH
    mod = _adaln(c, w_ada, b_ada).reshape(depth, B, 6, D)
    inv_freq = ROPE_BASE ** (-jnp.arange(0, MLA_QK_ROPE, 2, dtype=F32) / MLA_QK_ROPE)
    cosf, sinf = _rope_tables(positions, inv_freq)
    hid = np.arange(LANES) // HEAD_DIM
    bd64 = jnp.asarray(hid[:, None] == hid[None, :], BF16)
    bd128 = jnp.ones((LANES, LANES), BF16)

    v_first = None
    for l in range(depth):
        has_vres = l > 0
        mod_l = mod[l]
        win = _layout_w_in(w_in_first if l == 0 else w_in_rest[l - 1], has_vres)
        poolw = jax.scipy.linalg.block_diag(*[pool_w[l, g] for g in range(len(POOL_WINDOWS))]).astype(BF16)
        wq = _pad_heads(mla_wq_up[l], MLA_QK_DIM, 0, MLA_QK_DIM).astype(BF16)
        wk = _pad_heads(mla_wkv_up[l], MLA_QK_NOPE + HEAD_DIM, 0, MLA_QK_NOPE).astype(BF16)
        wv = mla_wkv_up[l].reshape(MLA_KV_LORA, MLA_HEADS, MLA_QK_NOPE + HEAD_DIM)[:, :, MLA_QK_NOPE:]
        wv = wv.reshape(MLA_KV_LORA, MLA_HEADS * HEAD_DIM).astype(BF16)
        qkg = jnp.tile(jnp.pad(mla_qk_gain[l], ((0, 0), (0, LANES - MLA_QK_DIM))), (1, MLA_HEADS))
        pad_mu = ZR_COLS - mu_shift.shape[1] - RWKV_VRES_LORA
        mu_v = mu_shift_v[l - 1] if has_vres else jnp.zeros((RWKV_VRES_LORA,), F32)
        mu = jnp.concatenate([mu_shift[l], mu_v, jnp.zeros((pad_mu,), F32)]).reshape(1, ZR_COLS)
        zr, y_pool, q, k, v = _mixin(
            x, mod_l, norm_gain[l, 0].reshape(1, D), win, cosf, sinf, poolw,
            pool_scale[l].reshape(1, -1), mla_q_lat_gain[l].reshape(1, -1),
            mla_kv_lat_gain[l].reshape(1, -1), wq, wk, wv, qkg, bd128, mu)

        v0 = rwkv_v0[l - 1] if has_vres else jnp.zeros((W,), F32)
        vec8 = jnp.concatenate([rwkv_vec[l], v0[None]], axis=0)
        w2a2 = jax.scipy.linalg.block_diag(rwkv_w2[l], rwkv_a2[l]).astype(BF16)
        g2 = jnp.pad(rwkv_g2[l], ((0, 2 * LANES - RWKV_GATE_LORA), (0, 0)))
        if has_vres:
            v2 = jnp.pad(rwkv_v2[l - 1], ((RWKV_GATE_LORA, 2 * LANES - RWKV_GATE_LORA - RWKV_VRES_LORA), (0, 0)))
        else:
            v2 = jnp.zeros((2 * LANES, W), F32)
        g2v2 = jnp.concatenate([g2, v2], axis=1).astype(BF16)
        y_rwkv, v_first = _rwkv(zr, v_first, vec8, w2a2, g2v2, bd64)

        y_mla = _attention(q, k, v, mla_qk_gain[l])

        is_moe = (l % 2 == 1)
        router_p = None
        if is_moe:
            router_p = jnp.pad(moe_router[l // 2], ((0, 0), (0, LANES - N_EXPERTS))).astype(BF16)
        outs = _mixout(x, y_rwkv, y_pool, y_mla, w_out[l].astype(BF16), mod_l,
                       norm_gain[l, 1].reshape(1, D), router_p)
        x_mid, h2 = outs[0], outs[1]
        xf = x_mid.reshape(B * T, D)
        if is_moe:
            xo = _moe(h2.reshape(ROW_CHUNKS * B * T, PACK_CHUNK_W), outs[2].reshape(B * T, LANES), outs[3],
                      moe_w_in[l // 2], moe_w_out[l // 2], xf, mod_l)
        else:
            xo = _ffn(h2.reshape(B * T, D), ffn_w_in[l // 2].astype(BF16), ffn_w_out[l // 2].astype(BF16),
                      xf, mod_l)
        x = xo.reshape(B, T, D)
    return x
```
